```python
import jax, jax.numpy as jnp
from jax import lax
import numpy as np

D_MODEL = 1024
BATCH = 1
SEQ = 16384
DEPTH = 1
DEC_BATCH = 32
DEC_SEQ = 64
PAST_LEN = 2048

CHUNK = 64
BAND_CHUNKS = 8
BAND_PAST = BAND_CHUNKS * CHUNK
A_WIDTH = D_MODEL // 2
A_HEADS = 8
A_HEAD_DIM = A_WIDTH // A_HEADS
MAX_REL = 128
N_REL = CHUNK + MAX_REL
B_WIDTH = D_MODEL // 2
B_HEADS = 4
B_DV = B_WIDTH // B_HEADS
B_DK = B_DV // 2
B_KWIDTH = B_HEADS * B_DK
GATE_RANK = 16
GATE_TAU = 16.0
IN_WIDTH = 3 * A_WIDTH + 2 * B_KWIDTH + 2 * B_WIDTH + GATE_RANK
N_GROUPS = 4
EXPERTS_PER_GROUP = 8
N_EXPERTS = N_GROUPS * EXPERTS_PER_GROUP
TOP_K = 2
EXPERT_FF = D_MODEL // 4
MOE_BLOCK = 512
EPS = 1e-6

kernel_name = 'hymba_chunkband_gla_hmoe_adaln_stream_step'


def _rms_norm(x, g):
    xf = x.astype(jnp.float32)
    y = xf * lax.rsqrt(jnp.mean(xf * xf, axis=-1, keepdims=True) + EPS)
    return (y * g.astype(jnp.float32)).astype(x.dtype)


def _adaln(c, w_ada, b_ada):
    mod = jax.nn.silu(c) @ w_ada + b_ada
    return jnp.split(mod[:, None, :], 6, axis=-1)


def _mix_inputs(h, p):
    B, S, _ = h.shape
    z = h @ p['w_in']
    o1 = A_WIDTH; o2 = 2 * A_WIDTH; o3 = 3 * A_WIDTH
    o4 = o3 + B_KWIDTH; o5 = o4 + B_KWIDTH; o6 = o5 + B_WIDTH; o7 = o6 + B_WIDTH
    qa, ka, va, qb, kb, vb, rb, gr = jnp.split(z, [o1, o2, o3, o4, o5, o6, o7], axis=-1)
    qa = _rms_norm(qa.reshape(B, S, A_HEADS, A_HEAD_DIM), p['q_norm_g'])
    ka = _rms_norm(ka.reshape(B, S, A_HEADS, A_HEAD_DIM), p['k_norm_g'])
    va = va.reshape(B, S, A_HEADS, A_HEAD_DIM)
    qb = qb.reshape(B, S, B_HEADS, B_DK) * (B_DK ** -0.5)
    kb = kb.reshape(B, S, B_HEADS, B_DK)
    vb = vb.reshape(B, S, B_HEADS, B_DV)
    log_a = jax.nn.log_sigmoid((gr @ p['w_gate_up'] + p['b_gate']).astype(jnp.float32)) / GATE_TAU
    log_a = log_a.reshape(B, S, B_HEADS, B_DK)
    return qa, ka, va, qb, kb, vb, log_a, rb


def _rel_bias(table, q_off, k_off):
    idx = jnp.clip(q_off[:, None] - k_off[None, :], -(CHUNK - 1), MAX_REL) + (CHUNK - 1)
    return table[:, idx].astype(jnp.float32)


def _band_attend(q, k, v, bias, valid):
    s = jnp.einsum('bqhd,bkhd->bhqk', q, k).astype(jnp.float32) * (A_HEAD_DIM ** -0.5) + bias
    s = jnp.where(valid, s, -jnp.inf)
    pr = jax.nn.softmax(s, axis=-1).astype(v.dtype)
    return jnp.einsum('bhqk,bkhd->bqhd', pr, v)


def _chunk_band_attention_prompt(q, k, v, table):
    B, S, H, Dh = q.shape
    n = S // CHUNK
    band = BAND_PAST + CHUNK
    pad = ((0, 0), (BAND_PAST, 0), (0, 0), (0, 0))
    kp = jnp.pad(k, pad)
    vp = jnp.pad(v, pad)
    q_off = jnp.arange(CHUNK)
    k_off = jnp.arange(band) - BAND_PAST
    bias = _rel_bias(table, q_off, k_off)
    qc = q.reshape(B, n, CHUNK, H, Dh).transpose(1, 0, 2, 3, 4)

    def one(args):
        ci, qi = args
        start = ci * CHUNK
        kb = lax.dynamic_slice_in_dim(kp, start, band, axis=1)
        vb = lax.dynamic_slice_in_dim(vp, start, band, axis=1)
        valid = (start + k_off) >= 0
        return _band_attend(qi, kb, vb, bias, valid)

    o = lax.map(one, (jnp.arange(n), qc))
    return o.transpose(1, 0, 2, 3, 4).reshape(B, S, H * Dh)


def _chunk_band_attention_sample(q, k_new, v_new, k_cache, v_cache, table):
    B, Sn, H, Dh = q.shape
    L = k_cache.shape[1]
    k = jnp.concatenate([k_cache.astype(k_new.dtype), k_new], axis=1)
    v = jnp.concatenate([v_cache.astype(v_new.dtype), v_new], axis=1)
    q_off = jnp.arange(Sn)
    k_off = jnp.concatenate([jnp.arange(L) - L, jnp.arange(Sn)])
    bias = _rel_bias(table, q_off, k_off)
    valid = jnp.ones((L + Sn,), bool)
    return _band_attend(q, k, v, bias, valid).reshape(B, Sn, H * Dh)


def _gla_chunk(state, q, k, v, log_a):
    C = q.shape[2]
    b = jnp.cumsum(log_a, axis=2)
    causal = jnp.tril(jnp.ones((C, C), bool))
    decay = jnp.exp(jnp.where(causal[:, :, None], b[:, :, :, None, :] - b[:, :, None, :, :], -jnp.inf))
    scores = jnp.einsum('bhtd,bhsd,bhtsd->bhts', q, k, decay)
    o = jnp.einsum('bhts,bhsv->bhtv', scores, v) + jnp.einsum('bhtd,bhdv->bhtv', q * jnp.exp(b), state)
    b_last = b[:, :, -1:, :]
    new_state = jnp.exp(b_last[:, :, 0, :, None]) * state + jnp.einsum('bhsd,bhsv->bhdv', k * jnp.exp(b_last - b), v)
    return new_state, o


def _gla_prompt(q, k, v, log_a):
    B, S, H, dk = q.shape
    dv = v.shape[-1]
    n = S // CHUNK

    def to_chunks(t):
        return t.astype(jnp.float32).reshape(B, n, CHUNK, H, t.shape[-1]).transpose(1, 0, 3, 2, 4)

    s0 = jnp.zeros((B, H, dk, dv), jnp.float32)
    s_fin, o = lax.scan(lambda st, xs: _gla_chunk(st, *xs), s0,
                        (to_chunks(q), to_chunks(k), to_chunks(v), to_chunks(log_a)))
    return o.transpose(1, 0, 3, 2, 4).reshape(B, S, H, dv), s_fin


def _gla_sample(q, k, v, log_a, s_prev):
    t = lambda a: a.astype(jnp.float32).transpose(0, 2, 1, 3)
    s_new, o = _gla_chunk(s_prev.astype(jnp.float32), t(q), t(k), t(v), t(log_a))
    return o.transpose(0, 2, 1, 3), s_new


def _mix_output(oa, ob, rb, p):
    B, S, _ = oa.shape
    ob = _rms_norm(ob, p['gla_norm_g']).reshape(B, S, B_WIDTH).astype(rb.dtype) * jax.nn.silu(rb)
    return jnp.concatenate([oa, ob], axis=-1) @ p['w_out']


def _hier_moe(h, p):
    B, S, D = h.shape
    T = B * S
    x = h.reshape(T, D)
    g_logit = (x @ p['w_route_group'] + p['b_route_group']).astype(jnp.float32)
    g_prob = jax.nn.softmax(g_logit, axis=-1)
    g_sel = jnp.argmax(g_logit, axis=-1)
    g_w = jnp.take_along_axis(g_prob, g_sel[:, None], axis=-1)
    e_logit = (x @ p['w_route_expert'] + p['b_route_expert']).astype(jnp.float32)
    e_logit = e_logit.reshape(T, N_GROUPS, EXPERTS_PER_GROUP)
    e_logit = jnp.take_along_axis(e_logit, g_sel[:, None, None], axis=1)[:, 0]
    top_v, top_i = lax.top_k(e_logit, TOP_K)
    w = jax.nn.softmax(top_v, axis=-1) * g_w
    ids = g_sel[:, None] * EXPERTS_PER_GROUP + top_i
    combine = jnp.sum(jax.nn.one_hot(ids, N_EXPERTS, dtype=jnp.float32) * w[..., None], axis=1).astype(h.dtype)
    pad = (-T) % MOE_BLOCK
    xb = jnp.pad(x, ((0, pad), (0, 0))).reshape(-1, MOE_BLOCK, D)
    cb = jnp.pad(combine, ((0, pad), (0, 0))).reshape(-1, MOE_BLOCK, N_EXPERTS)

    def block(args):
        xi, ci = args
        a = jax.nn.silu(jnp.einsum('td,edf->tef', xi, p['w_exp_gate'])) * jnp.einsum('td,edf->tef', xi, p['w_exp_up'])
        return jnp.einsum('tef,efd->td', a * ci[..., None], p['w_exp_down'])

    y = lax.map(block, (xb, cb)).reshape(-1, D)[:T]
    return y.reshape(B, S, D)


def _ffn_residual(x, sh, sc, gt, p):
    h = _rms_norm(x, p['norm_ffn_g']) * (1 + sc) + sh
    return x + gt * _hier_moe(h, p)


def _layer_prompt(x, c, p):
    sh1, sc1, g1, sh2, sc2, g2 = _adaln(c, p['w_ada'], p['b_ada'])
    h = _rms_norm(x, p['norm_mix_g']) * (1 + sc1) + sh1
    qa, ka, va, qb, kb, vb, log_a, rb = _mix_inputs(h, p)
    oa = _chunk_band_attention_prompt(qa, ka, va, p['rel_bias'])
    ob, s_fin = _gla_prompt(qb, kb, vb, log_a)
    x = x + g1 * _mix_output(oa, ob, rb, p)
    x = _ffn_residual(x, sh2, sc2, g2, p)
    tail = min(BAND_PAST, x.shape[1])
    return x, ka[:, -tail:], va[:, -tail:], s_fin.astype(x.dtype)


def _layer_sample(x, c, k_cache, v_cache, s_prev, p):
    sh1, sc1, g1, sh2, sc2, g2 = _adaln(c, p['w_ada'], p['b_ada'])
    h = _rms_norm(x, p['norm_mix_g']) * (1 + sc1) + sh1
    qa, ka, va, qb, kb, vb, log_a, rb = _mix_inputs(h, p)
    oa = _chunk_band_attention_sample(qa, ka, va, k_cache, v_cache, p['rel_bias'])
    ob, s_new = _gla_sample(qb, kb, vb, log_a, s_prev)
    x = x + g1 * _mix_output(oa, ob, rb, p)
    x = _ffn_residual(x, sh2, sc2, g2, p)
    return x, ka, va, s_new.astype(x.dtype)


def setup_inputs(seed: int = 0) -> dict:
    key = jax.random.key(seed)
    ks = jax.random.split(key, 32)
    f32 = jnp.float32
    a_cache = min(BAND_PAST, PAST_LEN)

    def nrm(k, shape, s):
        return jax.random.normal(k, shape, f32) * s

    return {
        'x_prompt': nrm(ks[0], (BATCH, SEQ, D_MODEL), 1.0),
        'x_sample': nrm(ks[1], (DEC_BATCH, DEC_SEQ, D_MODEL), 1.0),
        'cache_a_k': nrm(ks[2], (DEPTH, DEC_BATCH, a_cache, A_HEADS, A_HEAD_DIM), 1.0),
        'cache_a_v': nrm(ks[3], (DEPTH, DEC_BATCH, a_cache, A_HEADS, A_HEAD_DIM), 1.0),
        'state_gla': nrm(ks[4], (DEPTH, DEC_BATCH, B_HEADS, B_DK, B_DV), 1.0),
        'c_prompt': nrm(ks[5], (BATCH, D_MODEL), 1.0),
        'c_sample': nrm(ks[6], (DEC_BATCH, D_MODEL), 1.0),
        'norm_mix_g': 1.0 + nrm(ks[7], (DEPTH, D_MODEL), 0.02),
        'norm_ffn_g': 1.0 + nrm(ks[8], (DEPTH, D_MODEL), 0.02),
        'w_ada': nrm(ks[9], (DEPTH, D_MODEL, 6 * D_MODEL), D_MODEL ** -0.5),
        'b_ada': nrm(ks[10], (DEPTH, 6 * D_MODEL), 0.02),
        'w_in': nrm(ks[11], (DEPTH, D_MODEL, IN_WIDTH), D_MODEL ** -0.5),
        'q_norm_g': 1.0 + nrm(ks[12], (DEPTH, A_HEAD_DIM), 0.02),
        'k_norm_g': 1.0 + nrm(ks[13], (DEPTH, A_HEAD_DIM), 0.02),
        'rel_bias': nrm(ks[14], (DEPTH, A_HEADS, N_REL), 0.5),
        'w_gate_up': nrm(ks[15], (DEPTH, GATE_RANK, B_KWIDTH), GATE_RANK ** -0.5),
        'b_gate': nrm(ks[16], (DEPTH, B_KWIDTH), 0.1),
        'gla_norm_g': 1.0 + nrm(ks[17], (DEPTH, B_DV), 0.02),
        'w_out': nrm(ks[18], (DEPTH, D_MODEL, D_MODEL), D_MODEL ** -0.5),
        'w_route_group': nrm(ks[19], (DEPTH, D_MODEL, N_GROUPS), D_MODEL ** -0.5),
        'b_route_group': nrm(ks[20], (DEPTH, N_GROUPS), 0.01),
        'w_route_expert': nrm(ks[21], (DEPTH, D_MODEL, N_EXPERTS), D_MODEL ** -0.5),
        'b_route_expert': nrm(ks[22], (DEPTH, N_EXPERTS), 0.01),
        'w_exp_gate': nrm(ks[23], (DEPTH, N_EXPERTS, D_MODEL, EXPERT_FF), D_MODEL ** -0.5),
        'w_exp_up': nrm(ks[24], (DEPTH, N_EXPERTS, D_MODEL, EXPERT_FF), D_MODEL ** -0.5),
        'w_exp_down': nrm(ks[25], (DEPTH, N_EXPERTS, EXPERT_FF, D_MODEL), EXPERT_FF ** -0.5),
    }


def reference(x_prompt, x_sample, cache_a_k, cache_a_v, state_gla, c_prompt, c_sample,
              norm_mix_g, norm_ffn_g, w_ada, b_ada, w_in, q_norm_g, k_norm_g, rel_bias,
              w_gate_up, b_gate, gla_norm_g, w_out, w_route_group, b_route_group,
              w_route_expert, b_route_expert, w_exp_gate, w_exp_up, w_exp_down):
    y_p = x_prompt
    y_s = x_sample
    kp_l, vp_l, sp_l, ks_l, vs_l, ss_l = [], [], [], [], [], []
    for l in range(DEPTH):
        p = {
            'norm_mix_g': norm_mix_g[l], 'norm_ffn_g': norm_ffn_g[l],
            'w_ada': w_ada[l], 'b_ada': b_ada[l], 'w_in': w_in[l],
            'q_norm_g': q_norm_g[l], 'k_norm_g': k_norm_g[l], 'rel_bias': rel_bias[l],
            'w_gate_up': w_gate_up[l], 'b_gate': b_gate[l], 'gla_norm_g': gla_norm_g[l],
            'w_out': w_out[l], 'w_route_group': w_route_group[l], 'b_route_group': b_route_group[l],
            'w_route_expert': w_route_expert[l], 'b_route_expert': b_route_expert[l],
            'w_exp_gate': w_exp_gate[l], 'w_exp_up': w_exp_up[l], 'w_exp_down': w_exp_down[l],
        }
        y_p, kp, vp, sp = _layer_prompt(y_p, c_prompt, p)
        y_s, ksn, vsn, ssn = _layer_sample(y_s, c_sample, cache_a_k[l], cache_a_v[l], state_gla[l], p)
        kp_l.append(kp); vp_l.append(vp); sp_l.append(sp)
        ks_l.append(ksn); vs_l.append(vsn); ss_l.append(ssn)
    new_k_prompt = jnp.stack(kp_l)
    new_v_prompt = jnp.stack(vp_l)
    new_gla_prompt = jnp.stack(sp_l)
    new_k_sample = jnp.stack(ks_l)
    new_v_sample = jnp.stack(vs_l)
    new_gla_sample = jnp.stack(ss_l)
    return (y_p, y_s, new_k_prompt, new_v_prompt, new_gla_prompt, new_k_sample, new_v_sample, new_gla_sample)
```

```python
import functools

import numpy as np
import jax
import jax.numpy as jnp
from jax import lax
from jax.experimental import pallas as pl
from jax.experimental.pallas import tpu as pltpu

F32 = jnp.float32
BF16 = jnp.bfloat16

D_MODEL = 1024
CHUNK = 64
LOG_CHUNK = 6
BAND_CHUNKS = 8
BAND_PAST = BAND_CHUNKS * CHUNK
A_WIDTH = 512
A_HEADS = 8
A_HEAD_DIM = 64
MAX_REL = 128
N_REL = CHUNK + MAX_REL
B_WIDTH = 512
B_HEADS = 4
B_DV = 128
B_DK = 64
B_KWIDTH = 256
GATE_RANK = 16
GATE_TAU = 16.0
N_GROUPS = 4
EXPERTS_PER_GROUP = 8
N_EXPERTS = 32
EXPERT_FF = 256
EPS = 1e-6

LANES = 128
IN_MAIN = 3 * A_WIDTH + 2 * B_KWIDTH + 2 * B_WIDTH
IN_PAD = IN_MAIN + LANES
TOK_TILE = 512
ROWS_PER_TILE = TOK_TILE // CHUNK
Q_CHUNKS = 4
Q_ROWS = Q_CHUNKS * CHUNK
ROLL_W = 1024
NEG = -1e30
ROUTE_OFF = N_GROUPS
VMEM_LIMIT = 56 * 1024 * 1024


def _params(n_axes=1):
    return pltpu.CompilerParams(dimension_semantics=("arbitrary",) * n_axes,
                                vmem_limit_bytes=VMEM_LIMIT)


def _split(a):
    hi = a.astype(BF16)
    lo = (a - hi.astype(F32)).astype(BF16)
    return hi, lo


def _dot(a, b):
    return jnp.dot(a, b, preferred_element_type=F32)


def _dot3(a, b):
    ah, al = _split(a)
    bh, bl = _split(b)
    return _dot(ah, bh) + _dot(al, bh) + _dot(ah, bl)


def _dot_nt(a, b):
    return lax.dot_general(a, b, (((1,), (1,)), ((), ())), preferred_element_type=F32)


def _dot_tn(a, b):
    return lax.dot_general(a, b, (((0,), (0,)), ((), ())), preferred_element_type=F32)


def _silu(x):
    return x / (1.0 + jnp.exp(-x))


def _rows_to_tokens(rows, n):
    r = rows.shape[0]
    return jnp.broadcast_to(rows[:, None, :], (r, CHUNK, n)).reshape(r * CHUNK, n)


def _adaln_kernel(c_ref, w_ref, b_ref, o_ref):
    a = _silu(c_ref[...])
    o_ref[...] = _dot3(a, w_ref[...]) + b_ref[...]


def _adaln(c_rows, w_ada, b_ada):
    r = c_rows.shape[0]
    n = w_ada.shape[1]
    tn = 1024
    return pl.pallas_call(
        _adaln_kernel,
        grid=(n // tn,),
        in_specs=[pl.BlockSpec((r, D_MODEL), lambda j: (0, 0)),
                  pl.BlockSpec((D_MODEL, tn), lambda j: (0, j)),
                  pl.BlockSpec((1, tn), lambda j: (0, j))],
        out_specs=pl.BlockSpec((r, tn), lambda j: (0, j)),
        out_shape=jax.ShapeDtypeStruct((r, n), F32),
        compiler_params=_params(),
        name="adaln",
    )(c_rows, w_ada, b_ada.reshape(1, n))


def _head_rms(z, bd_ref, g):
    ms = _dot((z * z).astype(BF16), bd_ref[...]) * (1.0 / A_HEAD_DIM)
    return z * lax.rsqrt(ms + EPS) * g


def _inproj_kernel(n_ptiles, xp_ref, xs_ref, mod_ref, gmix_ref, w_ref, bd_ref, gq_ref, gk_ref,
                   wgu_ref, bg_ref,
                   q_ref, k_ref, v_ref, kf_ref, vf_ref, gla_ref, la_ref):
    i = pl.program_id(0)
    x = jnp.where(i < n_ptiles, xp_ref[...], xs_ref[...])
    ms = jnp.mean(x * x, axis=-1, keepdims=True)
    xn = x * lax.rsqrt(ms + EPS) * gmix_ref[...]
    sh = _rows_to_tokens(mod_ref[:, 0:D_MODEL], D_MODEL)
    sc = _rows_to_tokens(mod_ref[:, D_MODEL:2 * D_MODEL], D_MODEL)
    hb = (xn * (1.0 + sc) + sh).astype(BF16)

    zq = _dot(hb, w_ref[:, 0:A_WIDTH])
    q_ref[...] = (_head_rms(zq, bd_ref, gq_ref[...]) * (A_HEAD_DIM ** -0.5)).astype(BF16)
    zk = _dot(hb, w_ref[:, A_WIDTH:2 * A_WIDTH])
    kn = _head_rms(zk, bd_ref, gk_ref[...])
    k_ref[...] = kn.astype(BF16)
    kf_ref[...] = kn
    zv = _dot(hb, w_ref[:, 2 * A_WIDTH:3 * A_WIDTH])
    v_ref[...] = zv.astype(BF16)
    vf_ref[...] = zv

    o = 3 * A_WIDTH
    zqb = _dot(hb, w_ref[:, o:o + B_KWIDTH]) * (B_DK ** -0.5)
    gla_ref[:, 0:B_KWIDTH] = zqb.astype(BF16)
    for c in range(B_KWIDTH, 2 * B_KWIDTH + 2 * B_WIDTH, 256):
        gla_ref[:, c:c + 256] = _dot(hb, w_ref[:, o + c:o + c + 256]).astype(BF16)

    gr = _dot(hb, w_ref[:, IN_MAIN:IN_PAD])
    logit = _dot(gr.astype(BF16), wgu_ref[...]) + bg_ref[...]
    log_sig = jnp.minimum(logit, 0.0) - jnp.log1p(jnp.exp(-jnp.abs(logit)))
    la_ref[...] = log_sig * (1.0 / GATE_TAU)


def _inproj(xp, xs, mod, gmix, w_in_p, bd, gq, gk, wgu_p, bg, n_ptiles, n_stiles, prep):
    n_tiles = n_ptiles + n_stiles
    t = n_tiles * TOK_TILE
    tail_tiles = 1 + n_stiles
    pblocks = prep // ROWS_PER_TILE
    const = lambda i: (0, 0)
    row = lambda i: (i, 0)
    tail = lambda i: (jnp.maximum(i - (n_ptiles - 1), 0), 0)
    return pl.pallas_call(
        functools.partial(_inproj_kernel, n_ptiles),
        grid=(n_tiles,),
        in_specs=[pl.BlockSpec((TOK_TILE, D_MODEL), lambda i: (jnp.minimum(i, n_ptiles - 1), 0)),
                  pl.BlockSpec((TOK_TILE, D_MODEL), lambda i: (jnp.maximum(i - n_ptiles, 0), 0)),
                  pl.BlockSpec((ROWS_PER_TILE, 6 * D_MODEL),
                               lambda i: (jnp.maximum(i - n_ptiles + pblocks, 0), 0)),
                  pl.BlockSpec((1, D_MODEL), const),
                  pl.BlockSpec((D_MODEL, IN_PAD), const),
                  pl.BlockSpec((A_WIDTH, A_WIDTH), const),
                  pl.BlockSpec((1, A_WIDTH), const),
                  pl.BlockSpec((1, A_WIDTH), const),
                  pl.BlockSpec((LANES, B_KWIDTH), const),
                  pl.BlockSpec((1, B_KWIDTH), const)],
        out_specs=[pl.BlockSpec((TOK_TILE, A_WIDTH), row),
                   pl.BlockSpec((TOK_TILE, A_WIDTH), row),
                   pl.BlockSpec((TOK_TILE, A_WIDTH), row),
                   pl.BlockSpec((TOK_TILE, A_WIDTH), tail),
                   pl.BlockSpec((TOK_TILE, A_WIDTH), tail),
                   pl.BlockSpec((TOK_TILE, 2 * B_KWIDTH + 2 * B_WIDTH), row),
                   pl.BlockSpec((TOK_TILE, B_KWIDTH), row)],
        out_shape=[jax.ShapeDtypeStruct((t, A_WIDTH), BF16),
                   jax.ShapeDtypeStruct((t, A_WIDTH), BF16),
                   jax.ShapeDtypeStruct((t, A_WIDTH), BF16),
                   jax.ShapeDtypeStruct((tail_tiles * TOK_TILE, A_WIDTH), F32),
                   jax.ShapeDtypeStruct((tail_tiles * TOK_TILE, A_WIDTH), F32),
                   jax.ShapeDtypeStruct((t, 2 * B_KWIDTH + 2 * B_WIDTH), BF16),
                   jax.ShapeDtypeStruct((t, B_KWIDTH), F32)],
        compiler_params=_params(),
        name="inproj",
    )(xp, xs, mod, gmix, w_in_p, bd, gq, gk, wgu_p, bg)


def _bias_lanes(n_keys):
    l = np.arange(ROLL_W)
    d = np.where(l < n_keys, BAND_PAST - l, BAND_PAST - l + ROLL_W)
    return np.clip(d, -(CHUNK - 1), MAX_REL) + (CHUNK - 1)


def _build_bias(u_ref, bias_sc, m_rows, n_keys):
    qi = lax.broadcasted_iota(jnp.int32, (m_rows, n_keys), 0) >> LOG_CHUNK
    kc = lax.broadcasted_iota(jnp.int32, (m_rows, n_keys), 1) >> LOG_CHUNK
    band = (kc >= qi) & (kc <= qi + BAND_CHUNKS)
    for h in range(A_HEADS):
        src = jnp.broadcast_to(u_ref[h:h + 1, :], (m_rows, ROLL_W))
        toe = pltpu.roll(src, 0, 1, stride=1, stride_axis=0)
        bias_sc[h] = jnp.where(band, toe[:, 0:n_keys], NEG)


def _attend(q, kcat, vcat, bias_sc, key_mask):
    m_rows = q.shape[0]
    lane = lax.broadcasted_iota(jnp.int32, (m_rows, LANES), 1)
    half = [lane < A_HEAD_DIM, lane >= A_HEAD_DIM]
    outs = []
    for p in range(A_HEADS // 2):
        lanes = slice(p * LANES, (p + 1) * LANES)
        qp, kp, vp = q[:, lanes], kcat[:, lanes], vcat[:, lanes]
        halves = []
        for hh in range(2):
            qm = jnp.where(half[hh], qp, jnp.zeros_like(qp))
            s = _dot_nt(qm, kp) + bias_sc[2 * p + hh]
            if key_mask is not None:
                s = s + key_mask
            e = jnp.exp(s - jnp.max(s, axis=-1, keepdims=True))
            l = jnp.sum(e, axis=-1, keepdims=True)
            halves.append(_dot(e.astype(BF16), vp) / l)
        outs.append(jnp.where(half[0], halves[0], halves[1]))
    return jnp.concatenate(outs, axis=-1)


def _attn_prompt_kernel(u_ref, q_ref, k0_ref, k1_ref, k2_ref, v0_ref, v1_ref, v2_ref, o_ref, bias_sc):
    j = pl.program_id(0)
    n_keys = 3 * Q_ROWS

    @pl.when(j == 0)
    def _():
        _build_bias(u_ref, bias_sc, Q_ROWS, n_keys)

    kcat = jnp.concatenate([k0_ref[...], k1_ref[...], k2_ref[...]], axis=0)
    vcat = jnp.concatenate([v0_ref[...], v1_ref[...], v2_ref[...]], axis=0)
    kw = lax.broadcasted_iota(jnp.int32, (1, n_keys), 1)
    key_mask = jnp.where(kw >= (2 - j) * Q_ROWS, 0.0, NEG)
    o_ref[...] = _attend(q_ref[...], kcat, vcat, bias_sc, key_mask).astype(BF16)


def _attn_prompt(u, q, k, v, n_steps):
    const = lambda j: (0, 0)
    blk = lambda d: pl.BlockSpec((Q_ROWS, A_WIDTH), lambda j, d=d: (jnp.maximum(j - d, 0), 0))
    return pl.pallas_call(
        _attn_prompt_kernel,
        grid=(n_steps,),
        in_specs=[pl.BlockSpec((A_HEADS, ROLL_W), const),
                  blk(0), blk(2), blk(1), blk(0), blk(2), blk(1), blk(0)],
        out_specs=pl.BlockSpec((Q_ROWS, A_WIDTH), lambda j: (j, 0)),
        out_shape=jax.ShapeDtypeStruct((n_steps * Q_ROWS, A_WIDTH), BF16),
        scratch_shapes=[pltpu.VMEM((A_HEADS, Q_ROWS, 3 * Q_ROWS), F32)],
        compiler_params=_params(),
        name="attn_prompt",
    )(u, q, k, k, k, v, v, v)


SAMPLE_KEYS = BAND_PAST + 2 * CHUNK


def _attn_sample_kernel(u_ref, q_ref, kn_ref, vn_ref, kc_ref, vc_ref, o_ref, bias_sc):
    @pl.when(pl.program_id(0) == 0)
    def _():
        _build_bias(u_ref, bias_sc, CHUNK, SAMPLE_KEYS)

    pad = jnp.zeros((CHUNK, A_WIDTH), BF16)
    kcat = jnp.concatenate([kc_ref[0].astype(BF16), kn_ref[...], pad], axis=0)
    vcat = jnp.concatenate([vc_ref[0].astype(BF16), vn_ref[...], pad], axis=0)
    o_ref[...] = _attend(q_ref[...], kcat, vcat, bias_sc, None).astype(BF16)


def _attn_sample(u, q, k, v, kc, vc, first_chunk, n_seq):
    new = pl.BlockSpec((CHUNK, A_WIDTH), lambda b: (first_chunk + b, 0))
    cache = pl.BlockSpec((1, BAND_PAST, A_WIDTH), lambda b: (b, 0, 0))
    return pl.pallas_call(
        _attn_sample_kernel,
        grid=(n_seq,),
        in_specs=[pl.BlockSpec((A_HEADS, ROLL_W), lambda b: (0, 0)), new, new, new, cache, cache],
        out_specs=pl.BlockSpec((CHUNK, A_WIDTH), lambda b: (b, 0)),
        out_shape=jax.ShapeDtypeStruct((n_seq * CHUNK, A_WIDTH), BF16),
        scratch_shapes=[pltpu.VMEM((A_HEADS, CHUNK, SAMPLE_KEYS), F32)],
        compiler_params=_params(),
        name="attn_sample",
    )(u, q, k, v, kc, vc)


GLA_CHUNKS = 4


def _gla_block(n_chunks, gla_ref, la_ref, ltri_ref, g_ref, st_sc, o_ref):
    rows = n_chunks * CHUNK
    la = la_ref[...]
    la_hi, la_lo = _split(la)
    b = _dot(ltri_ref[...], la_hi) + _dot(ltri_ref[...], la_lo)
    b3 = b.reshape(n_chunks, CHUNK, B_KWIDTH)
    b_mid = b3[:, CHUNK // 2 - 1:CHUNK // 2, :]
    b_last = b3[:, CHUNK - 1:CHUNK, :]
    q = gla_ref[:, 0:B_KWIDTH].astype(F32).reshape(n_chunks, CHUNK, B_KWIDTH)
    k = gla_ref[:, B_KWIDTH:2 * B_KWIDTH].astype(F32).reshape(n_chunks, CHUNK, B_KWIDTH)
    q_start = (q * jnp.exp(b3)).reshape(rows, B_KWIDTH).astype(BF16)
    q_mid = (q * jnp.exp(b3 - b_mid)).reshape(rows, B_KWIDTH).astype(BF16)
    k_mid = (k * jnp.exp(b_mid - b3)).reshape(rows, B_KWIDTH).astype(BF16)
    k_end = (k * jnp.exp(b_last - b3)).reshape(rows, B_KWIDTH).astype(BF16)
    dec = jnp.exp(b_last)

    ti = lax.broadcasted_iota(jnp.int32, (rows, rows), 0)
    si = lax.broadcasted_iota(jnp.int32, (rows, rows), 1)
    causal = (si <= ti) & ((si >> LOG_CHUNK) == (ti >> LOG_CHUNK))
    lane_r = lax.broadcasted_iota(jnp.int32, (rows, LANES), 1)
    half_r = [lane_r < B_DK, lane_r >= B_DK]
    lane_c = lax.broadcasted_iota(jnp.int32, (CHUNK, LANES), 1)
    half_c = [lane_c < B_DK, lane_c >= B_DK]
    half_s = lax.broadcasted_iota(jnp.int32, (B_DV, LANES), 1) < B_DK

    for p in range(B_HEADS // 2):
        lanes = slice(p * LANES, (p + 1) * LANES)
        qs_p, qm_p, km_p, ke_p = q_start[:, lanes], q_mid[:, lanes], k_mid[:, lanes], k_end[:, lanes]
        vs = [gla_ref[:, 2 * B_KWIDTH + (2 * p + hh) * B_DV:2 * B_KWIDTH + (2 * p + hh + 1) * B_DV]
              for hh in range(2)]
        intra = []
        for hh in range(2):
            qm = jnp.where(half_r[hh], qm_p, jnp.zeros_like(qm_p))
            sc = jnp.where(causal, _dot_nt(qm, km_p), 0.0)
            intra.append(_dot(sc.astype(BF16), vs[hh]))
        inter = [[], []]
        st = st_sc[p]
        for c in range(n_chunks):
            cr = slice(c * CHUNK, (c + 1) * CHUNK)
            st_b = st.astype(BF16)
            for hh in range(2):
                qc = jnp.where(half_c[hh], qs_p[cr], jnp.zeros((CHUNK, LANES), BF16))
                inter[hh].append(_dot_nt(qc, st_b))
            upd = jnp.where(half_s, _dot_tn(vs[0][cr], ke_p[cr]), _dot_tn(vs[1][cr], ke_p[cr]))
            st = st * dec[c, :, lanes] + upd
        st_sc[p] = st
        for hh in range(2):
            h = 2 * p + hh
            o = intra[hh] + jnp.concatenate(inter[hh], axis=0)
            ms = jnp.mean(o * o, axis=-1, keepdims=True)
            on = o * lax.rsqrt(ms + EPS) * g_ref[...]
            r = gla_ref[:, 2 * B_KWIDTH + B_WIDTH + h * B_DV:2 * B_KWIDTH + B_WIDTH + (h + 1) * B_DV]
            o_ref[:, h * B_DV:(h + 1) * B_DV] = (on * _silu(r.astype(F32))).astype(BF16)


def _gla_prompt_kernel(gla_ref, la_ref, ltri_ref, g_ref, o_ref, sfin_ref, st_sc):
    @pl.when(pl.program_id(0) == 0)
    def _():
        st_sc[...] = jnp.zeros_like(st_sc)

    _gla_block(GLA_CHUNKS, gla_ref, la_ref, ltri_ref, g_ref, st_sc, o_ref)
    sfin_ref[...] = st_sc[...]


def _gla_sample_kernel(gla_ref, la_ref, ltri_ref, g_ref, s0_ref, o_ref, sfin_ref, st_sc):
    st_sc[...] = s0_ref[0]
    _gla_block(1, gla_ref, la_ref, ltri_ref, g_ref, st_sc, o_ref)
    sfin_ref[0] = st_sc[...]


def _ltri(n_chunks):
    r = np.arange(n_chunks * CHUNK)
    m = (r[None, :] <= r[:, None]) & (r[None, :] // CHUNK == r[:, None] // CHUNK)
    return jnp.asarray(m, BF16)


_GLA_W = 2 * B_KWIDTH + 2 * B_WIDTH
_ST_SHAPE = (B_HEADS // 2, B_DV, LANES)


def _gla_prompt(gla, la, g, n_steps):
    rows = GLA_CHUNKS * CHUNK
    const = lambda j: (0, 0)
    return pl.pallas_call(
        _gla_prompt_kernel,
        grid=(n_steps,),
        in_specs=[pl.BlockSpec((rows, _GLA_W), lambda j: (j, 0)),
                  pl.BlockSpec((rows, B_KWIDTH), lambda j: (j, 0)),
                  pl.BlockSpec((rows, rows), const),
                  pl.BlockSpec((1, B_DV), const)],
        out_specs=[pl.BlockSpec((rows, B_WIDTH), lambda j: (j, 0)),
                   pl.BlockSpec(_ST_SHAPE, lambda j: (0, 0, 0))],
        out_shape=[jax.ShapeDtypeStruct((n_steps * rows, B_WIDTH), BF16),
                   jax.ShapeDtypeStruct(_ST_SHAPE, F32)],
        scratch_shapes=[pltpu.VMEM(_ST_SHAPE, F32)],
        compiler_params=_params(),
        name="gla_prompt",
    )(gla, la, _ltri(GLA_CHUNKS), g)


def _gla_sample(gla, la, g, s0, first_chunk, n_seq):
    const = lambda b: (0, 0)
    st_spec = pl.BlockSpec((1,) + _ST_SHAPE, lambda b: (b, 0, 0, 0))
    return pl.pallas_call(
        _gla_sample_kernel,
        grid=(n_seq,),
        in_specs=[pl.BlockSpec((CHUNK, _GLA_W), lambda b: (first_chunk + b, 0)),
                  pl.BlockSpec((CHUNK, B_KWIDTH), lambda b: (first_chunk + b, 0)),
                  pl.BlockSpec((CHUNK, CHUNK), const),
                  pl.BlockSpec((1, B_DV), const),
                  st_spec],
        out_specs=[pl.BlockSpec((CHUNK, B_WIDTH), lambda b: (b, 0)), st_spec],
        out_shape=[jax.ShapeDtypeStruct((n_seq * CHUNK, B_WIDTH), BF16),
                   jax.ShapeDtypeStruct((n_seq,) + _ST_SHAPE, F32)],
        scratch_shapes=[pltpu.VMEM(_ST_SHAPE, F32)],
        compiler_params=_params(),
        name="gla_sample",
    )(gla, la, _ltri(1), g, s0)


def _state_to_pairs(s):
    lead = s.shape[:-3]
    s = s.reshape(lead + (B_HEADS // 2, 2, B_DK, B_DV))
    s = jnp.moveaxis(s, -1, -3)
    return s.reshape(lead + (B_HEADS // 2, B_DV, 2 * B_DK))


def _pairs_to_state(s):
    lead = s.shape[:-3]
    s = s.reshape(lead + (B_HEADS // 2, B_DV, 2, B_DK))
    s = jnp.moveaxis(s, -3, -1)
    return s.reshape(lead + (B_HEADS, B_DK, B_DV))


def _route(logits):
    lane = lax.broadcasted_iota(jnp.int32, logits.shape, 1)
    lane_f = lane.astype(F32)
    big = float(LANES)
    gmask = lane < N_GROUPS
    gl = jnp.where(gmask, logits, NEG)
    gmax = jnp.max(gl, axis=-1, keepdims=True)
    gsel = jnp.min(jnp.where(gl == gmax, lane_f, big), axis=-1, keepdims=True)
    gsum = jnp.sum(jnp.where(gmask, jnp.exp(gl - gmax), 0.0), axis=-1, keepdims=True)
    g_w = 1.0 / gsum
    e_lo = ROUTE_OFF + gsel * EXPERTS_PER_GROUP
    emask = (lane_f >= e_lo) & (lane_f < e_lo + EXPERTS_PER_GROUP)
    el = jnp.where(emask, logits, NEG)
    v1 = jnp.max(el, axis=-1, keepdims=True)
    i1 = jnp.min(jnp.where(el == v1, lane_f, big), axis=-1, keepdims=True)
    el2 = jnp.where(lane_f == i1, NEG, el)
    v2 = jnp.max(el2, axis=-1, keepdims=True)
    i2 = jnp.min(jnp.where(el2 == v2, lane_f, big), axis=-1, keepdims=True)
    t = jnp.exp(v2 - v1)
    w1 = g_w / (1.0 + t)
    w2 = g_w * t / (1.0 + t)
    return jnp.where(lane_f == i1, w1, 0.0) + jnp.where(lane_f == i2, w2, 0.0)


def _outproj_kernel(n_ptiles, oa_ref, ob_ref, wo_ref, xp_ref, xs_ref, mod_ref, gffn_ref, wr_ref, br_ref,
                    x1_ref, h2_ref, cmb_ref):
    i = pl.program_id(0)
    x = jnp.where(i < n_ptiles, xp_ref[...], xs_ref[...])
    mix = _dot(oa_ref[...], wo_ref[0:A_WIDTH, :]) + _dot(ob_ref[...], wo_ref[A_WIDTH:D_MODEL, :])
    gate1 = _rows_to_tokens(mod_ref[:, 2 * D_MODEL:3 * D_MODEL], D_MODEL)
    x1 = x + gate1 * mix
    x1_ref[...] = x1
    ms = jnp.mean(x1 * x1, axis=-1, keepdims=True)
    xn = x1 * lax.rsqrt(ms + EPS) * gffn_ref[...]
    sh = _rows_to_tokens(mod_ref[:, 3 * D_MODEL:4 * D_MODEL], D_MODEL)
    sc = _rows_to_tokens(mod_ref[:, 4 * D_MODEL:5 * D_MODEL], D_MODEL)
    h2 = xn * (1.0 + sc) + sh
    h2_ref[...] = h2.astype(BF16)
    cmb_ref[...] = _route(_dot3(h2, wr_ref[...]) + br_ref[...])


def _outproj(oa, ob, w_out, xp, xs, mod, gffn, wr, br, n_ptiles, n_stiles, prep):
    n_tiles = n_ptiles + n_stiles
    t = n_tiles * TOK_TILE
    pblocks = prep // ROWS_PER_TILE
    const = lambda i: (0, 0)
    row = lambda i: (i, 0)
    return pl.pallas_call(
        functools.partial(_outproj_kernel, n_ptiles),
        grid=(n_tiles,),
        in_specs=[pl.BlockSpec((TOK_TILE, A_WIDTH), row),
                  pl.BlockSpec((TOK_TILE, B_WIDTH), row),
                  pl.BlockSpec((D_MODEL, D_MODEL), const),
                  pl.BlockSpec((TOK_TILE, D_MODEL), lambda i: (jnp.minimum(i, n_ptiles - 1), 0)),
                  pl.BlockSpec((TOK_TILE, D_MODEL), lambda i: (jnp.maximum(i - n_ptiles, 0), 0)),
                  pl.BlockSpec((ROWS_PER_TILE, 6 * D_MODEL),
                               lambda i: (jnp.maximum(i - n_ptiles + pblocks, 0), 0)),
                  pl.BlockSpec((1, D_MODEL), const),
                  pl.BlockSpec((D_MODEL, LANES), const),
                  pl.BlockSpec((1, LANES), const)],
        out_specs=[pl.BlockSpec((TOK_TILE, D_MODEL), row),
                   pl.BlockSpec((TOK_TILE, D_MODEL), row),
                   pl.BlockSpec((TOK_TILE, LANES), row)],
        out_shape=[jax.ShapeDtypeStruct((t, D_MODEL), F32),
                   jax.ShapeDtypeStruct((t, D_MODEL), BF16),
                   jax.ShapeDtypeStruct((t, LANES), F32)],
        compiler_params=_params(),
        name="outproj",
    )(oa, ob, w_out, xp, xs, mod, gffn, wr, br)


def _moe_kernel(n_ptiles, h2_ref, cmb_ref, wg_ref, wu_ref, wd_ref, x1_ref, mod_ref, yp_ref, ys_ref, acc_sc):
    i = pl.program_id(0)
    e = pl.program_id(1)

    @pl.when(e == 0)
    def _():
        acc_sc[...] = jnp.zeros_like(acc_sc)

    h2 = h2_ref[...]
    lane = lax.broadcasted_iota(jnp.int32, cmb_ref.shape, 1)
    ce = jnp.sum(jnp.where(lane == e + ROUTE_OFF, cmb_ref[...], 0.0), axis=-1, keepdims=True)
    a = _silu(_dot(h2, wg_ref[0])) * _dot(h2, wu_ref[0]) * ce
    acc_sc[...] += _dot(a.astype(BF16), wd_ref[0])

    @pl.when(e == N_EXPERTS - 1)
    def _():
        gate2 = _rows_to_tokens(mod_ref[:, 5 * D_MODEL:6 * D_MODEL], D_MODEL)
        y = x1_ref[...] + gate2 * acc_sc[...]

        @pl.when(i < n_ptiles)
        def _():
            yp_ref[...] = y

        @pl.when(i >= n_ptiles)
        def _():
            ys_ref[...] = y


def _moe(h2, cmb, wg, wu, wd, x1, mod, tile, n_ptiles, n_stiles, prep):
    n_tiles = n_ptiles + n_stiles
    rows = tile // CHUNK
    pblocks = prep // rows
    tok = lambda i, e: (i, 0)
    wspec = lambda shape: pl.BlockSpec((1,) + shape, lambda i, e: (e, 0, 0))
    return pl.pallas_call(
        functools.partial(_moe_kernel, n_ptiles),
        grid=(n_tiles, N_EXPERTS),
        in_specs=[pl.BlockSpec((tile, D_MODEL), tok),
                  pl.BlockSpec((tile, LANES), tok),
                  wspec((D_MODEL, EXPERT_FF)), wspec((D_MODEL, EXPERT_FF)), wspec((EXPERT_FF, D_MODEL)),
                  pl.BlockSpec((tile, D_MODEL), tok),
                  pl.BlockSpec((rows, 6 * D_MODEL), lambda i, e: (jnp.maximum(i - n_ptiles + pblocks, 0), 0))],
        out_specs=[pl.BlockSpec((tile, D_MODEL), lambda i, e: (jnp.minimum(i, n_ptiles - 1), 0)),
                   pl.BlockSpec((tile, D_MODEL), lambda i, e: (jnp.maximum(i - n_ptiles, 0), 0))],
        out_shape=[jax.ShapeDtypeStruct((n_ptiles * tile, D_MODEL), F32),
                   jax.ShapeDtypeStruct((n_stiles * tile, D_MODEL), F32)],
        scratch_shapes=[pltpu.VMEM((tile, D_MODEL), F32)],
        compiler_params=_params(2),
        name="moe",
    )(h2, cmb, wg, wu, wd, x1, mod)


def _layer(xp, xs, cache_k, cache_v, state, c_prompt, c_sample, norm_mix_g, norm_ffn_g, w_ada, b_ada, w_in,
           q_norm_g, k_norm_g, rel_bias, w_gate_up, b_gate, gla_norm_g, w_out, w_route_group,
           b_route_group, w_route_expert, b_route_expert, w_exp_gate, w_exp_up, w_exp_down):
    batch, seq, _ = xp.shape
    n_seq, dec_seq, _ = xs.shape
    assert batch == 1 and dec_seq == CHUNK and cache_k.shape[1] == BAND_PAST
    assert seq % TOK_TILE == 0 and seq >= BAND_PAST and (n_seq * CHUNK) % TOK_TILE == 0
    n_ptok, n_stok = seq, n_seq * CHUNK
    n_ptiles, n_stiles = n_ptok // TOK_TILE, n_stok // TOK_TILE
    moe_tile = 1024 if (n_ptok % 1024 == 0 and n_stok % 1024 == 0) else TOK_TILE
    prep = moe_tile // CHUNK

    xp2 = xp.reshape(n_ptok, D_MODEL)
    xs2 = xs.reshape(n_stok, D_MODEL)
    c_rows = jnp.concatenate([jnp.broadcast_to(c_prompt, (prep, D_MODEL)), c_sample], axis=0)
    mod = _adaln(c_rows, w_ada, b_ada)

    w_in_p = jnp.pad(w_in, ((0, 0), (0, IN_PAD - w_in.shape[1]))).astype(BF16)
    wgu_p = jnp.pad(w_gate_up, ((0, LANES - GATE_RANK), (0, 0))).astype(BF16)
    head = np.arange(A_WIDTH) // A_HEAD_DIM
    bd = jnp.asarray(head[:, None] == head[None, :], BF16)
    gq = jnp.tile(q_norm_g, A_HEADS).reshape(1, A_WIDTH)
    gk = jnp.tile(k_norm_g, A_HEADS).reshape(1, A_WIDTH)
    q, k, v, kf, vf, gla, la = _inproj(
        xp2, xs2, mod, norm_mix_g.reshape(1, D_MODEL), w_in_p, bd, gq, gk, wgu_p,
        b_gate.reshape(1, B_KWIDTH), n_ptiles, n_stiles, prep)

    first_chunk = n_ptok // CHUNK
    oa_p = _attn_prompt(rel_bias[:, _bias_lanes(3 * Q_ROWS)], q, k, v, n_ptok // Q_ROWS)
    oa_s = _attn_sample(rel_bias[:, _bias_lanes(SAMPLE_KEYS)], q, k, v,
                        cache_k.reshape(n_seq, BAND_PAST, A_WIDTH), cache_v.reshape(n_seq, BAND_PAST, A_WIDTH),
                        first_chunk, n_seq)
    g_gla = gla_norm_g.reshape(1, B_DV)
    ob_p, sfin_p = _gla_prompt(gla, la, g_gla, n_ptok // (GLA_CHUNKS * CHUNK))
    ob_s, sfin_s = _gla_sample(gla, la, g_gla, _state_to_pairs(state), first_chunk, n_seq)
    oa = jnp.concatenate([oa_p, oa_s], axis=0)
    ob = jnp.concatenate([ob_p, ob_s], axis=0)

    wr = jnp.pad(jnp.concatenate([w_route_group, w_route_expert], axis=1),
                 ((0, 0), (0, LANES - N_GROUPS - N_EXPERTS)))
    br = jnp.pad(jnp.concatenate([b_route_group, b_route_expert]), (0, LANES - N_GROUPS - N_EXPERTS))
    x1, h2, cmb = _outproj(oa, ob, w_out.astype(BF16), xp2, xs2, mod, norm_ffn_g.reshape(1, D_MODEL),
                           wr, br.reshape(1, LANES), n_ptiles, n_stiles, prep)

    yp, ys = _moe(h2, cmb, w_exp_gate.astype(BF16), w_exp_up.astype(BF16), w_exp_down.astype(BF16),
                  x1, mod, moe_tile, n_ptok // moe_tile, n_stok // moe_tile, prep)

    tail = min(BAND_PAST, seq)
    heads = (A_HEADS, A_HEAD_DIM)
    return (yp.reshape(1, seq, D_MODEL), ys.reshape(n_seq, CHUNK, D_MODEL),
            kf[TOK_TILE - tail:TOK_TILE].reshape((1, tail) + heads),
            vf[TOK_TILE - tail:TOK_TILE].reshape((1, tail) + heads),
            _pairs_to_state(sfin_p)[None],
            kf[TOK_TILE:].reshape((n_seq, CHUNK) + heads),
            vf[TOK_TILE:].reshape((n_seq, CHUNK) + heads),
            _pairs_to_state(sfin_s))


def kernel(x_prompt, x_sample, cache_a_k, cache_a_v, state_gla, c_prompt, c_sample, norm_mix_g, norm_ffn_g,
           w_ada, b_ada, w_in, q_norm_g, k_norm_g, rel_bias, w_gate_up, b_gate, gla_norm_g, w_out,
           w_route_group, b_route_group, w_route_expert, b_route_expert, w_exp_gate, w_exp_up, w_exp_down):
    depth = w_in.shape[0]
    yp, ys = x_prompt, x_sample
    outs = [[] for _ in range(6)]
    for l in range(depth):
        yp, ys, kp, vp, sp, ks, vs, ss = _layer(
            yp, ys, cache_a_k[l], cache_a_v[l], state_gla[l], c_prompt, c_sample, norm_mix_g[l], norm_ffn_g[l],
            w_ada[l], b_ada[l], w_in[l], q_norm_g[l], k_norm_g[l], rel_bias[l], w_gate_up[l], b_gate[l],
            gla_norm_g[l], w_out[l], w_route_group[l], b_route_group[l], w_route_expert[l], b_route_expert[l],
            w_exp_gate[l], w_exp_up[l], w_exp_down[l])
        for lst, val in zip(outs, (kp, vp, sp, ks, vs, ss)):
            lst.append(val)
    return (yp, ys) + tuple(jnp.stack(o) for o in outs)
```

```python
import functools

import numpy as np
import jax
import jax.numpy as jnp
from jax import lax
from jax.experimental import pallas as pl
from jax.experimental.pallas import tpu as pltpu

F32 = jnp.float32
BF16 = jnp.bfloat16
U32 = jnp.uint32

D_MODEL = 1024
CHUNK = 64
LOG_CHUNK = 6
BAND_CHUNKS = 8
BAND_PAST = BAND_CHUNKS * CHUNK
A_WIDTH = 512
A_HEADS = 8
A_HEAD_DIM = 64
MAX_REL = 128
N_REL = CHUNK + MAX_REL
B_WIDTH = 512
B_HEADS = 4
B_DV = 128
B_DK = 64
B_KWIDTH = 256
GATE_RANK = 16
GATE_TAU = 16.0
N_GROUPS = 4
EXPERTS_PER_GROUP = 8
N_EXPERTS = 32
EXPERT_FF = 256
EPS = 1e-6

LANES = 128
IN_MAIN = 3 * A_WIDTH + 2 * B_KWIDTH + 2 * B_WIDTH
IN_PAD = IN_MAIN + LANES
TOK_TILE = 512
ROWS_PER_TILE = TOK_TILE // CHUNK
Q_CHUNKS = 4
Q_ROWS = Q_CHUNKS * CHUNK
ROLL_W = 1024
NEG = -1e30
ROUTE_OFF = N_GROUPS
VMEM_LIMIT = 56 * 1024 * 1024


def _params(n_axes=1):
    return pltpu.CompilerParams(dimension_semantics=("arbitrary",) * n_axes,
                                vmem_limit_bytes=VMEM_LIMIT)


def _split(a):
    hi = a.astype(BF16)
    lo = (a - hi.astype(F32)).astype(BF16)
    return hi, lo


def _dot(a, b):
    return jnp.dot(a, b, preferred_element_type=F32)


def _dot3(a, b):
    ah, al = _split(a)
    bh, bl = _split(b)
    return _dot(ah, bh) + _dot(al, bh) + _dot(ah, bl)


def _dot_nt(a, b):
    return lax.dot_general(a, b, (((1,), (1,)), ((), ())), preferred_element_type=F32)


def _dot_tn(a, b):
    return lax.dot_general(a, b, (((0,), (0,)), ((), ())), preferred_element_type=F32)


def _silu(x):
    return x / (1.0 + jnp.exp(-x))


def _rows_to_tokens(rows, n):
    r = rows.shape[0]
    return jnp.broadcast_to(rows[:, None, :], (r, CHUNK, n)).reshape(r * CHUNK, n)


def _adaln_kernel(c_ref, w_ref, b_ref, o_ref):
    a = _silu(c_ref[...])
    o_ref[...] = _dot3(a, w_ref[...]) + b_ref[...]


def _adaln(c_rows, w_ada, b_ada):
    r = c_rows.shape[0]
    n = w_ada.shape[1]
    tn = 1024
    return pl.pallas_call(
        _adaln_kernel,
        grid=(n // tn,),
        in_specs=[pl.BlockSpec((r, D_MODEL), lambda j: (0, 0)),
                  pl.BlockSpec((D_MODEL, tn), lambda j: (0, j)),
                  pl.BlockSpec((1, tn), lambda j: (0, j))],
        out_specs=pl.BlockSpec((r, tn), lambda j: (0, j)),
        out_shape=jax.ShapeDtypeStruct((r, n), F32),
        compiler_params=_params(),
        name="adaln",
    )(c_rows, w_ada, b_ada.reshape(1, n))


def _head_rms(z, bd_ref, g):
    ms = _dot((z * z).astype(BF16), bd_ref[...]) * (1.0 / A_HEAD_DIM)
    return z * lax.rsqrt(ms + EPS) * g


def _inproj_kernel(n_ptiles, xp_ref, xs_ref, mod_ref, gmix_ref, w_ref, bd_ref, gq_ref, gk_ref,
                   wgu_ref, bg_ref,
                   q_ref, k_ref, v_ref, kf_ref, vf_ref, gla_ref, la_ref):
    i = pl.program_id(0)
    x = jnp.where(i < n_ptiles, xp_ref[...], xs_ref[...])
    ms = jnp.mean(x * x, axis=-1, keepdims=True)
    xn = x * lax.rsqrt(ms + EPS) * gmix_ref[...]
    sh = _rows_to_tokens(mod_ref[:, 0:D_MODEL], D_MODEL)
    sc = _rows_to_tokens(mod_ref[:, D_MODEL:2 * D_MODEL], D_MODEL)
    hb = (xn * (1.0 + sc) + sh).astype(BF16)

    zq = _dot(hb, w_ref[:, 0:A_WIDTH])
    q_ref[...] = (_head_rms(zq, bd_ref, gq_ref[...]) * (A_HEAD_DIM ** -0.5)).astype(BF16)
    zk = _dot(hb, w_ref[:, A_WIDTH:2 * A_WIDTH])
    kn = _head_rms(zk, bd_ref, gk_ref[...])
    k_ref[...] = kn.astype(BF16)
    kf_ref[...] = kn
    zv = _dot(hb, w_ref[:, 2 * A_WIDTH:3 * A_WIDTH])
    v_ref[...] = zv.astype(BF16)
    vf_ref[...] = zv

    o = 3 * A_WIDTH
    zqb = _dot(hb, w_ref[:, o:o + B_KWIDTH]) * (B_DK ** -0.5)
    gla_ref[:, 0:B_KWIDTH] = zqb.astype(BF16)
    for c in range(B_KWIDTH, 2 * B_KWIDTH + 2 * B_WIDTH, 256):
        gla_ref[:, c:c + 256] = _dot(hb, w_ref[:, o + c:o + c + 256]).astype(BF16)

    gr = _dot(hb, w_ref[:, IN_MAIN:IN_PAD])
    logit = _dot(gr.astype(BF16), wgu_ref[...]) + bg_ref[...]
    log_sig = jnp.minimum(logit, 0.0) - jnp.log1p(jnp.exp(-jnp.abs(logit)))
    la_ref[...] = log_sig * (1.0 / GATE_TAU)


def _inproj(xp, xs, mod, gmix, w_in_p, bd, gq, gk, wgu_p, bg, n_ptiles, n_stiles, prep):
    n_tiles = n_ptiles + n_stiles
    t = n_tiles * TOK_TILE
    tail_tiles = 1 + n_stiles
    pblocks = prep // ROWS_PER_TILE
    const = lambda i: (0, 0)
    row = lambda i: (i, 0)
    tail = lambda i: (jnp.maximum(i - (n_ptiles - 1), 0), 0)
    return pl.pallas_call(
        functools.partial(_inproj_kernel, n_ptiles),
        grid=(n_tiles,),
        in_specs=[pl.BlockSpec((TOK_TILE, D_MODEL), lambda i: (jnp.minimum(i, n_ptiles - 1), 0)),
                  pl.BlockSpec((TOK_TILE, D_MODEL), lambda i: (jnp.maximum(i - n_ptiles, 0), 0)),
                  pl.BlockSpec((ROWS_PER_TILE, 6 * D_MODEL),
                               lambda i: (jnp.maximum(i - n_ptiles + pblocks, 0), 0)),
                  pl.BlockSpec((1, D_MODEL), const),
                  pl.BlockSpec((D_MODEL, IN_PAD), const),
                  pl.BlockSpec((A_WIDTH, A_WIDTH), const),
                  pl.BlockSpec((1, A_WIDTH), const),
                  pl.BlockSpec((1, A_WIDTH), const),
                  pl.BlockSpec((LANES, B_KWIDTH), const),
                  pl.BlockSpec((1, B_KWIDTH), const)],
        out_specs=[pl.BlockSpec((TOK_TILE, A_WIDTH), row),
                   pl.BlockSpec((TOK_TILE, A_WIDTH), row),
                   pl.BlockSpec((TOK_TILE, A_WIDTH), row),
                   pl.BlockSpec((TOK_TILE, A_WIDTH), tail),
                   pl.BlockSpec((TOK_TILE, A_WIDTH), tail),
                   pl.BlockSpec((TOK_TILE, 2 * B_KWIDTH + 2 * B_WIDTH), row),
                   pl.BlockSpec((TOK_TILE, B_KWIDTH), row)],
        out_shape=[jax.ShapeDtypeStruct((t, A_WIDTH), BF16),
                   jax.ShapeDtypeStruct((t, A_WIDTH), BF16),
                   jax.ShapeDtypeStruct((t, A_WIDTH), BF16),
                   jax.ShapeDtypeStruct((tail_tiles * TOK_TILE, A_WIDTH), F32),
                   jax.ShapeDtypeStruct((tail_tiles * TOK_TILE, A_WIDTH), F32),
                   jax.ShapeDtypeStruct((t, 2 * B_KWIDTH + 2 * B_WIDTH), BF16),
                   jax.ShapeDtypeStruct((t, B_KWIDTH), F32)],
        compiler_params=_params(),
        name="inproj",
    )(xp, xs, mod, gmix, w_in_p, bd, gq, gk, wgu_p, bg)


def _bias_lanes(n_keys):
    l = np.arange(ROLL_W)
    d = np.where(l < n_keys, BAND_PAST - l, BAND_PAST - l + ROLL_W)
    return np.clip(d, -(CHUNK - 1), MAX_REL) + (CHUNK - 1)


def _build_bias(u_ref, bias_sc, m_rows, n_keys):
    qi = lax.broadcasted_iota(jnp.int32, (m_rows, n_keys), 0) >> LOG_CHUNK
    kc = lax.broadcasted_iota(jnp.int32, (m_rows, n_keys), 1) >> LOG_CHUNK
    band = (kc >= qi) & (kc <= qi + BAND_CHUNKS)
    for h in range(A_HEADS):
        src = jnp.broadcast_to(u_ref[h:h + 1, :], (m_rows, ROLL_W))
        toe = pltpu.roll(src, 0, 1, stride=1, stride_axis=0)
        bias_sc[h] = jnp.where(band, toe[:, 0:n_keys], NEG)


def _attend(q, kcat, vcat, bias_sc, key_mask):
    m_rows = q.shape[0]
    lane = lax.broadcasted_iota(jnp.int32, (m_rows, LANES), 1)
    half = [lane < A_HEAD_DIM, lane >= A_HEAD_DIM]
    outs = []
    for p in range(A_HEADS // 2):
        lanes = slice(p * LANES, (p + 1) * LANES)
        qp, kp, vp = q[:, lanes], kcat[:, lanes], vcat[:, lanes]
        halves = []
        for hh in range(2):
            qm = jnp.where(half[hh], qp, jnp.zeros_like(qp))
            s = _dot_nt(qm, kp) + bias_sc[2 * p + hh]
            if key_mask is not None:
                s = s + key_mask
            e = jnp.exp(s - jnp.max(s, axis=-1, keepdims=True))
            l = jnp.sum(e, axis=-1, keepdims=True)
            halves.append(_dot(e.astype(BF16), vp) / l)
        outs.append(jnp.where(half[0], halves[0], halves[1]))
    return jnp.concatenate(outs, axis=-1)


def _attn_prompt_kernel(u_ref, q_ref, k0_ref, k1_ref, k2_ref, v0_ref, v1_ref, v2_ref, o_ref, bias_sc):
    j = pl.program_id(0)
    n_keys = 3 * Q_ROWS

    @pl.when(j == 0)
    def _():
        _build_bias(u_ref, bias_sc, Q_ROWS, n_keys)

    kcat = jnp.concatenate([k0_ref[...], k1_ref[...], k2_ref[...]], axis=0)
    vcat = jnp.concatenate([v0_ref[...], v1_ref[...], v2_ref[...]], axis=0)
    kw = lax.broadcasted_iota(jnp.int32, (1, n_keys), 1)
    key_mask = jnp.where(kw >= (2 - j) * Q_ROWS, 0.0, NEG)
    o_ref[...] = _attend(q_ref[...], kcat, vcat, bias_sc, key_mask).astype(BF16)


def _attn_prompt(u, q, k, v, n_steps):
    const = lambda j: (0, 0)
    blk = lambda d: pl.BlockSpec((Q_ROWS, A_WIDTH), lambda j, d=d: (jnp.maximum(j - d, 0), 0))
    return pl.pallas_call(
        _attn_prompt_kernel,
        grid=(n_steps,),
        in_specs=[pl.BlockSpec((A_HEADS, ROLL_W), const),
                  blk(0), blk(2), blk(1), blk(0), blk(2), blk(1), blk(0)],
        out_specs=pl.BlockSpec((Q_ROWS, A_WIDTH), lambda j: (j, 0)),
        out_shape=jax.ShapeDtypeStruct((n_steps * Q_ROWS, A_WIDTH), BF16),
        scratch_shapes=[pltpu.VMEM((A_HEADS, Q_ROWS, 3 * Q_ROWS), F32)],
        compiler_params=_params(),
        name="attn_prompt",
    )(u, q, k, k, k, v, v, v)


SAMPLE_KEYS = BAND_PAST + 2 * CHUNK


def _attn_sample_kernel(u_ref, q_ref, kn_ref, vn_ref, kc_ref, vc_ref, o_ref, bias_sc):
    @pl.when(pl.program_id(0) == 0)
    def _():
        _build_bias(u_ref, bias_sc, CHUNK, SAMPLE_KEYS)

    pad = jnp.zeros((CHUNK, A_WIDTH), BF16)
    kcat = jnp.concatenate([kc_ref[0].astype(BF16), kn_ref[...], pad], axis=0)
    vcat = jnp.concatenate([vc_ref[0].astype(BF16), vn_ref[...], pad], axis=0)
    o_ref[...] = _attend(q_ref[...], kcat, vcat, bias_sc, None).astype(BF16)


def _attn_sample(u, q, k, v, kc, vc, first_chunk, n_seq):
    new = pl.BlockSpec((CHUNK, A_WIDTH), lambda b: (first_chunk + b, 0))
    cache = pl.BlockSpec((1, BAND_PAST, A_WIDTH), lambda b: (b, 0, 0))
    return pl.pallas_call(
        _attn_sample_kernel,
        grid=(n_seq,),
        in_specs=[pl.BlockSpec((A_HEADS, ROLL_W), lambda b: (0, 0)), new, new, new, cache, cache],
        out_specs=pl.BlockSpec((CHUNK, A_WIDTH), lambda b: (b, 0)),
        out_shape=jax.ShapeDtypeStruct((n_seq * CHUNK, A_WIDTH), BF16),
        scratch_shapes=[pltpu.VMEM((A_HEADS, CHUNK, SAMPLE_KEYS), F32)],
        compiler_params=_params(),
        name="attn_sample",
    )(u, q, k, v, kc, vc)


GLA_CHUNKS = 4


def _gla_block(n_chunks, gla_ref, la_ref, ltri_ref, g_ref, st_sc, o_ref):
    rows = n_chunks * CHUNK
    la = la_ref[...]
    la_hi, la_lo = _split(la)
    b = _dot(ltri_ref[...], la_hi) + _dot(ltri_ref[...], la_lo)
    b3 = b.reshape(n_chunks, CHUNK, B_KWIDTH)
    b_mid = b3[:, CHUNK // 2 - 1:CHUNK // 2, :]
    b_last = b3[:, CHUNK - 1:CHUNK, :]
    q = gla_ref[:, 0:B_KWIDTH].astype(F32).reshape(n_chunks, CHUNK, B_KWIDTH)
    k = gla_ref[:, B_KWIDTH:2 * B_KWIDTH].astype(F32).reshape(n_chunks, CHUNK, B_KWIDTH)
    q_start = (q * jnp.exp(b3)).reshape(rows, B_KWIDTH).astype(BF16)
    q_mid = (q * jnp.exp(b3 - b_mid)).reshape(rows, B_KWIDTH).astype(BF16)
    k_mid = (k * jnp.exp(b_mid - b3)).reshape(rows, B_KWIDTH).astype(BF16)
    k_end = (k * jnp.exp(b_last - b3)).reshape(rows, B_KWIDTH).astype(BF16)
    dec = jnp.exp(b_last)

    ti = lax.broadcasted_iota(jnp.int32, (rows, rows), 0)
    si = lax.broadcasted_iota(jnp.int32, (rows, rows), 1)
    causal = (si <= ti) & ((si >> LOG_CHUNK) == (ti >> LOG_CHUNK))
    lane_r = lax.broadcasted_iota(jnp.int32, (rows, LANES), 1)
    half_r = [lane_r < B_DK, lane_r >= B_DK]
    lane_c = lax.broadcasted_iota(jnp.int32, (CHUNK, LANES), 1)
    half_c = [lane_c < B_DK, lane_c >= B_DK]
    half_s = lax.broadcasted_iota(jnp.int32, (B_DV, LANES), 1) < B_DK

    for p in range(B_HEADS // 2):
        lanes = slice(p * LANES, (p + 1) * LANES)
        qs_p, qm_p, km_p, ke_p = q_start[:, lanes], q_mid[:, lanes], k_mid[:, lanes], k_end[:, lanes]
        vs = [gla_ref[:, 2 * B_KWIDTH + (2 * p + hh) * B_DV:2 * B_KWIDTH + (2 * p + hh + 1) * B_DV]
              for hh in range(2)]
        intra = []
        for hh in range(2):
            qm = jnp.where(half_r[hh], qm_p, jnp.zeros_like(qm_p))
            sc = jnp.where(causal, _dot_nt(qm, km_p), 0.0)
            intra.append(_dot(sc.astype(BF16), vs[hh]))
        inter = [[], []]
        st = st_sc[p]
        for c in range(n_chunks):
            cr = slice(c * CHUNK, (c + 1) * CHUNK)
            st_b = st.astype(BF16)
            for hh in range(2):
                qc = jnp.where(half_c[hh], qs_p[cr], jnp.zeros((CHUNK, LANES), BF16))
                inter[hh].append(_dot_nt(qc, st_b))
            upd = jnp.where(half_s, _dot_tn(vs[0][cr], ke_p[cr]), _dot_tn(vs[1][cr], ke_p[cr]))
            st = st * dec[c, :, lanes] + upd
        st_sc[p] = st
        for hh in range(2):
            h = 2 * p + hh
            o = intra[hh] + jnp.concatenate(inter[hh], axis=0)
            ms = jnp.mean(o * o, axis=-1, keepdims=True)
            on = o * lax.rsqrt(ms + EPS) * g_ref[...]
            r = gla_ref[:, 2 * B_KWIDTH + B_WIDTH + h * B_DV:2 * B_KWIDTH + B_WIDTH + (h + 1) * B_DV]
            o_ref[:, h * B_DV:(h + 1) * B_DV] = (on * _silu(r.astype(F32))).astype(BF16)


def _gla_prompt_kernel(gla_ref, la_ref, ltri_ref, g_ref, o_ref, sfin_ref, st_sc):
    @pl.when(pl.program_id(0) == 0)
    def _():
        st_sc[...] = jnp.zeros_like(st_sc)

    _gla_block(GLA_CHUNKS, gla_ref, la_ref, ltri_ref, g_ref, st_sc, o_ref)
    sfin_ref[...] = st_sc[...]


def _gla_sample_kernel(gla_ref, la_ref, ltri_ref, g_ref, s0_ref, o_ref, sfin_ref, st_sc):
    st_sc[...] = s0_ref[0]
    _gla_block(1, gla_ref, la_ref, ltri_ref, g_ref, st_sc, o_ref)
    sfin_ref[0] = st_sc[...]


def _ltri(n_chunks):
    r = np.arange(n_chunks * CHUNK)
    m = (r[None, :] <= r[:, None]) & (r[None, :] // CHUNK == r[:, None] // CHUNK)
    return jnp.asarray(m, BF16)


_GLA_W = 2 * B_KWIDTH + 2 * B_WIDTH
_ST_SHAPE = (B_HEADS // 2, B_DV, LANES)


def _gla_prompt(gla, la, g, n_steps):
    rows = GLA_CHUNKS * CHUNK
    const = lambda j: (0, 0)
    return pl.pallas_call(
        _gla_prompt_kernel,
        grid=(n_steps,),
        in_specs=[pl.BlockSpec((rows, _GLA_W), lambda j: (j, 0)),
                  pl.BlockSpec((rows, B_KWIDTH), lambda j: (j, 0)),
                  pl.BlockSpec((rows, rows), const),
                  pl.BlockSpec((1, B_DV), const)],
        out_specs=[pl.BlockSpec((rows, B_WIDTH), lambda j: (j, 0)),
                   pl.BlockSpec(_ST_SHAPE, lambda j: (0, 0, 0))],
        out_shape=[jax.ShapeDtypeStruct((n_steps * rows, B_WIDTH), BF16),
                   jax.ShapeDtypeStruct(_ST_SHAPE, F32)],
        scratch_shapes=[pltpu.VMEM(_ST_SHAPE, F32)],
        compiler_params=_params(),
        name="gla_prompt",
    )(gla, la, _ltri(GLA_CHUNKS), g)


def _gla_sample(gla, la, g, s0, first_chunk, n_seq):
    const = lambda b: (0, 0)
    st_spec = pl.BlockSpec((1,) + _ST_SHAPE, lambda b: (b, 0, 0, 0))
    return pl.pallas_call(
        _gla_sample_kernel,
        grid=(n_seq,),
        in_specs=[pl.BlockSpec((CHUNK, _GLA_W), lambda b: (first_chunk + b, 0)),
                  pl.BlockSpec((CHUNK, B_KWIDTH), lambda b: (first_chunk + b, 0)),
                  pl.BlockSpec((CHUNK, CHUNK), const),
                  pl.BlockSpec((1, B_DV), const),
                  st_spec],
        out_specs=[pl.BlockSpec((CHUNK, B_WIDTH), lambda b: (b, 0)), st_spec],
        out_shape=[jax.ShapeDtypeStruct((n_seq * CHUNK, B_WIDTH), BF16),
                   jax.ShapeDtypeStruct((n_seq,) + _ST_SHAPE, F32)],
        scratch_shapes=[pltpu.VMEM(_ST_SHAPE, F32)],
        compiler_params=_params(),
        name="gla_sample",
    )(gla, la, _ltri(1), g, s0)


def _state_to_pairs(s):
    lead = s.shape[:-3]
    s = s.reshape(lead + (B_HEADS // 2, 2, B_DK, B_DV))
    s = jnp.moveaxis(s, -1, -3)
    return s.reshape(lead + (B_HEADS // 2, B_DV, 2 * B_DK))


def _pairs_to_state(s):
    lead = s.shape[:-3]
    s = s.reshape(lead + (B_HEADS // 2, B_DV, 2, B_DK))
    s = jnp.moveaxis(s, -3, -1)
    return s.reshape(lead + (B_HEADS, B_DK, B_DV))


def _route(logits):
    lane = lax.broadcasted_iota(jnp.int32, logits.shape, 1)
    lane_f = lane.astype(F32)
    big = float(LANES)
    gmask = lane < N_GROUPS
    gl = jnp.where(gmask, logits, NEG)
    gmax = jnp.max(gl, axis=-1, keepdims=True)
    gsel = jnp.min(jnp.where(gl == gmax, lane_f, big), axis=-1, keepdims=True)
    gsum = jnp.sum(jnp.where(gmask, jnp.exp(gl - gmax), 0.0), axis=-1, keepdims=True)
    g_w = 1.0 / gsum
    e_lo = ROUTE_OFF + gsel * EXPERTS_PER_GROUP
    emask = (lane_f >= e_lo) & (lane_f < e_lo + EXPERTS_PER_GROUP)
    el = jnp.where(emask, logits, NEG)
    v1 = jnp.max(el, axis=-1, keepdims=True)
    i1 = jnp.min(jnp.where(el == v1, lane_f, big), axis=-1, keepdims=True)
    el2 = jnp.where(lane_f == i1, NEG, el)
    v2 = jnp.max(el2, axis=-1, keepdims=True)
    i2 = jnp.min(jnp.where(el2 == v2, lane_f, big), axis=-1, keepdims=True)
    t = jnp.exp(v2 - v1)
    w1 = g_w / (1.0 + t)
    w2 = g_w * t / (1.0 + t)
    return lane_f, i1, i2, w1, w2


def _pack_rows(z32_sc, x, o_ref, rows):
    half = D_MODEL // 2
    for s in range(half // LANES):
        z32_sc[s, pl.ds(0, rows, stride=2), :] = x[:, s * LANES:(s + 1) * LANES]
        z32_sc[s, pl.ds(1, rows, stride=2), :] = x[:, half + s * LANES:half + (s + 1) * LANES]
        o_ref[:, s * LANES:(s + 1) * LANES] = pltpu.bitcast(z32_sc[s].astype(BF16), U32)


def _unpack_rows(z32_sc, words, rows):
    half = D_MODEL // 2
    lo, hi = [], []
    for s in range(half // LANES):
        z32_sc[s] = pltpu.bitcast(words[:, s * LANES:(s + 1) * LANES], BF16).astype(F32)
        lo.append(z32_sc[s, pl.ds(0, rows, stride=2), :])
        hi.append(z32_sc[s, pl.ds(1, rows, stride=2), :])
    return jnp.concatenate(lo, axis=1), jnp.concatenate(hi, axis=1)


def _outproj_kernel(n_ptiles, tiles_per_sb, oa_ref, ob_ref, wo_ref, xp_ref, xs_ref, mod_ref, gffn_ref, wr_ref,
                    br_ref, ltri_ref, x1_ref, h2p_ref, meta_ref, cnt_ref, z32_sc, cnt_sc):
    i = pl.program_id(0)
    x = jnp.where(i < n_ptiles, xp_ref[...], xs_ref[...])
    mix = _dot(oa_ref[...], wo_ref[0:A_WIDTH, :]) + _dot(ob_ref[...], wo_ref[A_WIDTH:D_MODEL, :])
    gate1 = _rows_to_tokens(mod_ref[:, 2 * D_MODEL:3 * D_MODEL], D_MODEL)
    x1 = x + gate1 * mix
    x1_ref[...] = x1
    ms = jnp.mean(x1 * x1, axis=-1, keepdims=True)
    xn = x1 * lax.rsqrt(ms + EPS) * gffn_ref[...]
    sh = _rows_to_tokens(mod_ref[:, 3 * D_MODEL:4 * D_MODEL], D_MODEL)
    sc = _rows_to_tokens(mod_ref[:, 4 * D_MODEL:5 * D_MODEL], D_MODEL)
    h2 = xn * (1.0 + sc) + sh
    _pack_rows(z32_sc, h2, h2p_ref, TOK_TILE)

    lane_f, i1, i2, w1, w2 = _route(_dot3(h2, wr_ref[...]) + br_ref[...])

    @pl.when(lax.rem(i, tiles_per_sb) == 0)
    def _():
        cnt_sc[...] = jnp.zeros_like(cnt_sc)

    sel = jnp.where((lane_f == i1) | (lane_f == i2), 1.0, 0.0).astype(BF16)
    before = _dot(ltri_ref[...], sel) + cnt_sc[0:1, :]
    rank1 = jnp.sum(jnp.where(lane_f == i1, before, 0.0), axis=-1, keepdims=True)
    rank2 = jnp.sum(jnp.where(lane_f == i2, before, 0.0), axis=-1, keepdims=True)
    cnt = cnt_sc[...] + _dot(jnp.ones((8, TOK_TILE), BF16), sel)
    cnt_sc[...] = cnt
    cnt_ref[0] = cnt
    cols = (i1, i2, rank1, rank2, w1, w2)
    meta = jnp.zeros_like(lane_f)
    for c, col in enumerate(cols):
        meta = jnp.where(lane_f == float(c), col, meta)
    meta_ref[...] = meta


def _outproj(oa, ob, w_out, xp, xs, mod, gffn, wr, br, n_ptiles, n_stiles, prep, sb):
    n_tiles = n_ptiles + n_stiles
    t = n_tiles * TOK_TILE
    pblocks = prep // ROWS_PER_TILE
    tiles_per_sb = sb // TOK_TILE
    const = lambda i: (0, 0)
    row = lambda i: (i, 0)
    r = np.arange(TOK_TILE)
    ltri = jnp.asarray(r[None, :] < r[:, None], BF16)
    return pl.pallas_call(
        functools.partial(_outproj_kernel, n_ptiles, tiles_per_sb),
        grid=(n_tiles,),
        in_specs=[pl.BlockSpec((TOK_TILE, A_WIDTH), row),
                  pl.BlockSpec((TOK_TILE, B_WIDTH), row),
                  pl.BlockSpec((D_MODEL, D_MODEL), const),
                  pl.BlockSpec((TOK_TILE, D_MODEL), lambda i: (jnp.minimum(i, n_ptiles - 1), 0)),
                  pl.BlockSpec((TOK_TILE, D_MODEL), lambda i: (jnp.maximum(i - n_ptiles, 0), 0)),
                  pl.BlockSpec((ROWS_PER_TILE, 6 * D_MODEL),
                               lambda i: (jnp.maximum(i - n_ptiles + pblocks, 0), 0)),
                  pl.BlockSpec((1, D_MODEL), const),
                  pl.BlockSpec((D_MODEL, LANES), const),
                  pl.BlockSpec((1, LANES), const),
                  pl.BlockSpec((TOK_TILE, TOK_TILE), const)],
        out_specs=[pl.BlockSpec((TOK_TILE, D_MODEL), row),
                   pl.BlockSpec((TOK_TILE, D_MODEL // 2), row),
                   pl.BlockSpec((TOK_TILE, LANES), row),
                   pl.BlockSpec((1, 8, LANES), lambda i: (i // tiles_per_sb, 0, 0))],
        out_shape=[jax.ShapeDtypeStruct((t, D_MODEL), F32),
                   jax.ShapeDtypeStruct((t, D_MODEL // 2), U32),
                   jax.ShapeDtypeStruct((t, LANES), F32),
                   jax.ShapeDtypeStruct((t // sb, 8, LANES), F32)],
        scratch_shapes=[pltpu.VMEM((D_MODEL // 2 // LANES, 2 * TOK_TILE, LANES), F32),
                        pltpu.VMEM((8, LANES), F32)],
        compiler_params=_params(),
        name="outproj",
    )(oa, ob, w_out, xp, xs, mod, gffn, wr, br, ltri)


MOE_SUPER_BLOCK = 2048
SEG_ALIGN = 16
FFN_ROWS = 256
ROW_W = D_MODEL // 2
COPY_UNROLL = 8


def _sorted_rows(sb):
    return 2 * sb + N_EXPERTS * SEG_ALIGN + FFN_ROWS


def _moe_pos_kernel(meta_ref, cnt_ref, ustrict_ref, posw_ref, seg_ref):
    cnt = cnt_ref[0]
    units = jnp.floor((cnt + (SEG_ALIGN - 1)) * (1.0 / SEG_ALIGN))
    off = _dot(units.astype(BF16), ustrict_ref[...]) * SEG_ALIGN
    seg_ref[0] = jnp.concatenate([off[0:4], (units * SEG_ALIGN)[0:4]], axis=0)
    meta = meta_ref[...]
    lane_f = lax.broadcasted_iota(jnp.int32, meta.shape, 1).astype(F32)
    off_row = off[0:1, :]
    pos = []
    for k in range(2):
        e_lane = meta[:, k:k + 1]
        base = jnp.sum(jnp.where(lane_f == e_lane, off_row, 0.0), axis=-1, keepdims=True)
        pos.append(base + meta[:, 2 + k:3 + k])
    out = jnp.zeros_like(meta)
    for c, col in enumerate((pos[0], pos[1], meta[:, 4:5], meta[:, 5:6])):
        out = jnp.where(lane_f == float(c), col, out)
    posw_ref[...] = out


def _moe_pos(meta, cnt, sb):
    n_sb = meta.shape[0] // sb
    r = np.arange(LANES)
    ustrict = jnp.asarray(r[:, None] < r[None, :], BF16)
    return pl.pallas_call(
        _moe_pos_kernel,
        grid=(n_sb,),
        in_specs=[pl.BlockSpec((sb, LANES), lambda s: (s, 0)),
                  pl.BlockSpec((1, 8, LANES), lambda s: (s, 0, 0)),
                  pl.BlockSpec((LANES, LANES), lambda s: (0, 0))],
        out_specs=[pl.BlockSpec((sb, LANES), lambda s: (s, 0)),
                   pl.BlockSpec((1, 8, LANES), lambda s: (s, 0, 0))],
        out_shape=[jax.ShapeDtypeStruct(meta.shape, F32),
                   jax.ShapeDtypeStruct((n_sb, 8, LANES), F32)],
        compiler_params=_params(),
        name="moe_pos",
    )(meta, cnt, ustrict)


def _moe_dispatch_kernel(sb, h2p_ref, p1_ref, p2_ref, xs_ref):
    xs_ref[...] = jnp.zeros_like(xs_ref)

    def step(g, carry):
        for u in range(COPY_UNROLL):
            t = g * COPY_UNROLL + u
            row = h2p_ref[pl.ds(t, 1), :]
            xs_ref[0, pl.ds(p1_ref[t], 1), :] = row
            xs_ref[0, pl.ds(p2_ref[t], 1), :] = row
        return carry

    lax.fori_loop(0, sb // COPY_UNROLL, step, 0)


def _smem_vec(n, index_map):
    return pl.BlockSpec((n,), index_map, memory_space=pltpu.SMEM)


def _moe_dispatch(h2p, p1, p2, sb):
    n_sb = h2p.shape[0] // sb
    rs = _sorted_rows(sb)
    vec = _smem_vec(sb, lambda s: (s,))
    return pl.pallas_call(
        functools.partial(_moe_dispatch_kernel, sb),
        grid=(n_sb,),
        in_specs=[pl.BlockSpec((sb, ROW_W), lambda s: (s, 0)), vec, vec],
        out_specs=pl.BlockSpec((1, rs, ROW_W), lambda s: (s, 0, 0)),
        out_shape=jax.ShapeDtypeStruct((n_sb, rs, ROW_W), U32),
        compiler_params=_params(),
        name="moe_dispatch",
    )(h2p, p1, p2)


def _moe_ffn_kernel(off_ref, cnt_ref, xs_ref, wg_ref, wu_ref, wd_ref, ys_ref, z32_sc):
    s = pl.program_id(0)
    e = pl.program_id(1)
    half = D_MODEL // 2

    @pl.when(e == 0)
    def _():
        ys_ref[...] = jnp.zeros_like(ys_ref)

    seg = s * N_EXPERTS + e
    off = off_ref[seg]
    n_win = (cnt_ref[seg] + (FFN_ROWS - 1)) // FFN_ROWS

    def window(j, carry):
        r = pl.multiple_of(off + j * FFN_ROWS, SEG_ALIGN)
        lo, hi = _unpack_rows(z32_sc, xs_ref[0, pl.ds(r, FFN_ROWS), :], FFN_ROWS)
        lo, hi = lo.astype(BF16), hi.astype(BF16)
        g = _dot(lo, wg_ref[0, 0:half, :]) + _dot(hi, wg_ref[0, half:D_MODEL, :])
        u = _dot(lo, wu_ref[0, 0:half, :]) + _dot(hi, wu_ref[0, half:D_MODEL, :])
        y = _dot((_silu(g) * u).astype(BF16), wd_ref[0])
        _pack_rows(z32_sc, y, ys_ref.at[0, pl.ds(r, FFN_ROWS)], FFN_ROWS)
        return carry

    lax.fori_loop(0, n_win, window, 0)


def _moe_ffn(xs, seg_off, seg_cnt, wg, wu, wd):
    n_sb, rs, _ = xs.shape
    wspec = lambda shape: pl.BlockSpec((1,) + shape, lambda s, e, *_: (e, 0, 0))
    return pl.pallas_call(
        _moe_ffn_kernel,
        grid_spec=pltpu.PrefetchScalarGridSpec(
            num_scalar_prefetch=2,
            grid=(n_sb, N_EXPERTS),
            in_specs=[pl.BlockSpec((1, rs, ROW_W), lambda s, e, *_: (s, 0, 0)),
                      wspec((D_MODEL, EXPERT_FF)), wspec((D_MODEL, EXPERT_FF)), wspec((EXPERT_FF, D_MODEL))],
            out_specs=pl.BlockSpec((1, rs, D_MODEL // 2), lambda s, e, *_: (s, 0, 0)),
            scratch_shapes=[pltpu.VMEM((D_MODEL // 2 // LANES, 2 * FFN_ROWS, LANES), F32)]),
        out_shape=jax.ShapeDtypeStruct((n_sb, rs, D_MODEL // 2), U32),
        compiler_params=_params(2),
        name="moe_ffn",
    )(seg_off, seg_cnt, xs, wg, wu, wd)


def _moe_combine_kernel(n_psb, ys_ref, p1_ref, p2_ref, posw_ref, x1_ref, mod_ref, yp_ref, yo_ref,
                        g1_sc, g2_sc, z32_sc):
    s = pl.program_id(0)

    def step(g, carry):
        base = pl.multiple_of(g * COPY_UNROLL, COPY_UNROLL)
        for p_ref, g_sc in ((p1_ref, g1_sc), (p2_ref, g2_sc)):
            rows = [ys_ref[0, pl.ds(p_ref[base + u], 1), :] for u in range(COPY_UNROLL)]
            g_sc[pl.ds(base, COPY_UNROLL), :] = jnp.concatenate(rows, axis=0)
        return carry

    lax.fori_loop(0, TOK_TILE // COPY_UNROLL, step, 0)
    lo1, hi1 = _unpack_rows(z32_sc, g1_sc[...], TOK_TILE)
    lo2, hi2 = _unpack_rows(z32_sc, g2_sc[...], TOK_TILE)
    w1, w2 = posw_ref[:, 2:3], posw_ref[:, 3:4]
    moe = jnp.concatenate([w1 * lo1 + w2 * lo2, w1 * hi1 + w2 * hi2], axis=1)
    gate2 = _rows_to_tokens(mod_ref[:, 5 * D_MODEL:6 * D_MODEL], D_MODEL)
    y = x1_ref[...] + gate2 * moe

    @pl.when(s < n_psb)
    def _():
        yp_ref[...] = y

    @pl.when(s >= n_psb)
    def _():
        yo_ref[...] = y


def _moe_combine(ys, p1, p2, posw, x1, mod, sb, n_ptiles, n_stiles, prep):
    n_sb, rs, _ = ys.shape
    tps = sb // TOK_TILE
    n_psb = n_ptiles // tps
    pblocks = prep // ROWS_PER_TILE
    tile = lambda s, j: s * tps + j
    vec = _smem_vec(TOK_TILE, lambda s, j: (tile(s, j),))
    return pl.pallas_call(
        functools.partial(_moe_combine_kernel, n_psb),
        grid=(n_sb, tps),
        in_specs=[pl.BlockSpec((1, rs, D_MODEL // 2), lambda s, j: (s, 0, 0)), vec, vec,
                  pl.BlockSpec((TOK_TILE, LANES), lambda s, j: (tile(s, j), 0)),
                  pl.BlockSpec((TOK_TILE, D_MODEL), lambda s, j: (tile(s, j), 0)),
                  pl.BlockSpec((ROWS_PER_TILE, 6 * D_MODEL),
                               lambda s, j: (jnp.maximum(tile(s, j) - n_ptiles + pblocks, 0), 0))],
        out_specs=[pl.BlockSpec((TOK_TILE, D_MODEL), lambda s, j: (jnp.minimum(tile(s, j), n_ptiles - 1), 0)),
                   pl.BlockSpec((TOK_TILE, D_MODEL), lambda s, j: (jnp.maximum(tile(s, j) - n_ptiles, 0), 0))],
        out_shape=[jax.ShapeDtypeStruct((n_ptiles * TOK_TILE, D_MODEL), F32),
                   jax.ShapeDtypeStruct((n_stiles * TOK_TILE, D_MODEL), F32)],
        scratch_shapes=[pltpu.VMEM((TOK_TILE, D_MODEL // 2), U32),
                        pltpu.VMEM((TOK_TILE, D_MODEL // 2), U32),
                        pltpu.VMEM((D_MODEL // 2 // LANES, 2 * TOK_TILE, LANES), F32)],
        compiler_params=_params(2),
        name="moe_combine",
    )(ys, p1, p2, posw, x1, mod)


def _layer(xp, xs, cache_k, cache_v, state, c_prompt, c_sample, norm_mix_g, norm_ffn_g, w_ada, b_ada, w_in,
           q_norm_g, k_norm_g, rel_bias, w_gate_up, b_gate, gla_norm_g, w_out, w_route_group,
           b_route_group, w_route_expert, b_route_expert, w_exp_gate, w_exp_up, w_exp_down):
    batch, seq, _ = xp.shape
    n_seq, dec_seq, _ = xs.shape
    assert batch == 1 and dec_seq == CHUNK and cache_k.shape[1] == BAND_PAST
    assert seq % TOK_TILE == 0 and seq >= BAND_PAST and (n_seq * CHUNK) % TOK_TILE == 0
    n_ptok, n_stok = seq, n_seq * CHUNK
    n_ptiles, n_stiles = n_ptok // TOK_TILE, n_stok // TOK_TILE
    sb = MOE_SUPER_BLOCK if (n_ptok % MOE_SUPER_BLOCK == 0 and n_stok % MOE_SUPER_BLOCK == 0) else TOK_TILE
    prep = ROWS_PER_TILE

    xp2 = xp.reshape(n_ptok, D_MODEL)
    xs2 = xs.reshape(n_stok, D_MODEL)
    c_rows = jnp.concatenate([jnp.broadcast_to(c_prompt, (prep, D_MODEL)), c_sample], axis=0)
    mod = _adaln(c_rows, w_ada, b_ada)

    w_in_p = jnp.pad(w_in, ((0, 0), (0, IN_PAD - w_in.shape[1]))).astype(BF16)
    wgu_p = jnp.pad(w_gate_up, ((0, LANES - GATE_RANK), (0, 0))).astype(BF16)
    head = np.arange(A_WIDTH) // A_HEAD_DIM
    bd = jnp.asarray(head[:, None] == head[None, :], BF16)
    gq = jnp.tile(q_norm_g, A_HEADS).reshape(1, A_WIDTH)
    gk = jnp.tile(k_norm_g, A_HEADS).reshape(1, A_WIDTH)
    q, k, v, kf, vf, gla, la = _inproj(
        xp2, xs2, mod, norm_mix_g.reshape(1, D_MODEL), w_in_p, bd, gq, gk, wgu_p,
        b_gate.reshape(1, B_KWIDTH), n_ptiles, n_stiles, prep)

    first_chunk = n_ptok // CHUNK
    oa_p = _attn_prompt(rel_bias[:, _bias_lanes(3 * Q_ROWS)], q, k, v, n_ptok // Q_ROWS)
    oa_s = _attn_sample(rel_bias[:, _bias_lanes(SAMPLE_KEYS)], q, k, v,
                        cache_k.reshape(n_seq, BAND_PAST, A_WIDTH), cache_v.reshape(n_seq, BAND_PAST, A_WIDTH),
                        first_chunk, n_seq)
    g_gla = gla_norm_g.reshape(1, B_DV)
    ob_p, sfin_p = _gla_prompt(gla, la, g_gla, n_ptok // (GLA_CHUNKS * CHUNK))
    ob_s, sfin_s = _gla_sample(gla, la, g_gla, _state_to_pairs(state), first_chunk, n_seq)
    oa = jnp.concatenate([oa_p, oa_s], axis=0)
    ob = jnp.concatenate([ob_p, ob_s], axis=0)

    wr = jnp.pad(jnp.concatenate([w_route_group, w_route_expert], axis=1),
                 ((0, 0), (0, LANES - N_GROUPS - N_EXPERTS)))
    br = jnp.pad(jnp.concatenate([b_route_group, b_route_expert]), (0, LANES - N_GROUPS - N_EXPERTS))
    x1, h2p, meta, cnt = _outproj(oa, ob, w_out.astype(BF16), xp2, xs2, mod, norm_ffn_g.reshape(1, D_MODEL),
                                  wr, br.reshape(1, LANES), n_ptiles, n_stiles, prep, sb)

    posw, seg = _moe_pos(meta, cnt, sb)
    p1, p2 = posw[:, 0].astype(jnp.int32), posw[:, 1].astype(jnp.int32)
    experts = slice(ROUTE_OFF, ROUTE_OFF + N_EXPERTS)
    seg_off = seg[:, 0, experts].astype(jnp.int32).reshape(-1)
    seg_cnt = seg[:, 4, experts].astype(jnp.int32).reshape(-1)
    xs_sorted = _moe_dispatch(h2p, p1, p2, sb)
    ys_sorted = _moe_ffn(xs_sorted, seg_off, seg_cnt,
                         w_exp_gate.astype(BF16), w_exp_up.astype(BF16), w_exp_down.astype(BF16))
    yp, ys = _moe_combine(ys_sorted, p1, p2, posw, x1, mod, sb, n_ptiles, n_stiles, prep)

    tail = min(BAND_PAST, seq)
    heads = (A_HEADS, A_HEAD_DIM)
    return (yp.reshape(1, seq, D_MODEL), ys.reshape(n_seq, CHUNK, D_MODEL),
            kf[TOK_TILE - tail:TOK_TILE].reshape((1, tail) + heads),
            vf[TOK_TILE - tail:TOK_TILE].reshape((1, tail) + heads),
            _pairs_to_state(sfin_p)[None],
            kf[TOK_TILE:].reshape((n_seq, CHUNK) + heads),
            vf[TOK_TILE:].reshape((n_seq, CHUNK) + heads),
            _pairs_to_state(sfin_s))


def kernel(x_prompt, x_sample, cache_a_k, cache_a_v, state_gla, c_prompt, c_sample, norm_mix_g, norm_ffn_g,
           w_ada, b_ada, w_in, q_norm_g, k_norm_g, rel_bias, w_gate_up, b_gate, gla_norm_g, w_out,
           w_route_group, b_route_group, w_route_expert, b_route_expert, w_exp_gate, w_exp_up, w_exp_down):
    depth = w_in.shape[0]
    yp, ys = x_prompt, x_sample
    outs = [[] for _ in range(6)]
    for l in range(depth):
        yp, ys, kp, vp, sp, ks, vs, ss = _layer(
            yp, ys, cache_a_k[l], cache_a_v[l], state_gla[l], c_prompt, c_sample, norm_mix_g[l], norm_ffn_g[l],
            w_ada[l], b_ada[l], w_in[l], q_norm_g[l], k_norm_g[l], rel_bias[l], w_gate_up[l], b_gate[l],
            gla_norm_g[l], w_out[l], w_route_group[l], b_route_group[l], w_route_expert[l], b_route_expert[l],
            w_exp_gate[l], w_exp_up[l], w_exp_down[l])
        for lst, val in zip(outs, (kp, vp, sp, ks, vs, ss)):
            lst.append(val)
    return (yp, ys) + tuple(jnp.stack(o) for o in outs)
```

```python
import functools

import numpy as np
import jax
import jax.numpy as jnp
from jax import lax
from jax.experimental import pallas as pl
from jax.experimental.pallas import tpu as pltpu

F32 = jnp.float32
BF16 = jnp.bfloat16
U32 = jnp.uint32

D_MODEL = 1024
CHUNK = 64
LOG_CHUNK = 6
BAND_CHUNKS = 8
BAND_PAST = BAND_CHUNKS * CHUNK
A_WIDTH = 512
A_HEADS = 8
A_HEAD_DIM = 64
MAX_REL = 128
N_REL = CHUNK + MAX_REL
B_WIDTH = 512
B_HEADS = 4
B_DV = 128
B_DK = 64
B_KWIDTH = 256
GATE_RANK = 16
GATE_TAU = 16.0
N_GROUPS = 4
EXPERTS_PER_GROUP = 8
N_EXPERTS = 32
EXPERT_FF = 256
EPS = 1e-6

LANES = 128
IN_MAIN = 3 * A_WIDTH + 2 * B_KWIDTH + 2 * B_WIDTH
IN_PAD = IN_MAIN + LANES
TOK_TILE = 512
ROWS_PER_TILE = TOK_TILE // CHUNK
Q_CHUNKS = 4
Q_ROWS = Q_CHUNKS * CHUNK
ROLL_W = 1024
NEG = -1e30
ROUTE_OFF = N_GROUPS
VMEM_LIMIT = 56 * 1024 * 1024


def _params(n_axes=1):
    return pltpu.CompilerParams(dimension_semantics=("arbitrary",) * n_axes,
                                vmem_limit_bytes=VMEM_LIMIT)


def _split(a):
    hi = a.astype(BF16)
    lo = (a - hi.astype(F32)).astype(BF16)
    return hi, lo


def _dot(a, b):
    return jnp.dot(a, b, preferred_element_type=F32)


def _dot3(a, b):
    ah, al = _split(a)
    bh, bl = _split(b)
    return _dot(ah, bh) + _dot(al, bh) + _dot(ah, bl)


def _dot_nt(a, b):
    return lax.dot_general(a, b, (((1,), (1,)), ((), ())), preferred_element_type=F32)


def _dot_tn(a, b):
    return lax.dot_general(a, b, (((0,), (0,)), ((), ())), preferred_element_type=F32)


def _silu(x):
    return x / (1.0 + jnp.exp(-x))


def _rows_to_tokens(rows, n):
    r = rows.shape[0]
    return jnp.broadcast_to(rows[:, None, :], (r, CHUNK, n)).reshape(r * CHUNK, n)


def _adaln_kernel(c_ref, w_ref, b_ref, o_ref):
    a = _silu(c_ref[...])
    o_ref[...] = _dot3(a, w_ref[...]) + b_ref[...]


def _adaln(c_rows, w_ada, b_ada):
    r = c_rows.shape[0]
    n = w_ada.shape[1]
    tn = 1024
    return pl.pallas_call(
        _adaln_kernel,
        grid=(n // tn,),
        in_specs=[pl.BlockSpec((r, D_MODEL), lambda j: (0, 0)),
                  pl.BlockSpec((D_MODEL, tn), lambda j: (0, j)),
                  pl.BlockSpec((1, tn), lambda j: (0, j))],
        out_specs=pl.BlockSpec((r, tn), lambda j: (0, j)),
        out_shape=jax.ShapeDtypeStruct((r, n), F32),
        compiler_params=_params(),
        name="adaln",
    )(c_rows, w_ada, b_ada.reshape(1, n))


def _head_rms(z, bd_ref, g):
    ms = _dot((z * z).astype(BF16), bd_ref[...]) * (1.0 / A_HEAD_DIM)
    return z * lax.rsqrt(ms + EPS) * g


def _inproj_kernel(n_ptiles, xp_ref, xs_ref, mod_ref, gmix_ref, w_ref, bd_ref, gq_ref, gk_ref,
                   wgu_ref, bg_ref,
                   q_ref, k_ref, v_ref, kf_ref, vf_ref, gla_ref, la_ref):
    i = pl.program_id(0)
    x = jnp.where(i < n_ptiles, xp_ref[...], xs_ref[...])
    ms = jnp.mean(x * x, axis=-1, keepdims=True)
    xn = x * lax.rsqrt(ms + EPS) * gmix_ref[...]
    sh = _rows_to_tokens(mod_ref[:, 0:D_MODEL], D_MODEL)
    sc = _rows_to_tokens(mod_ref[:, D_MODEL:2 * D_MODEL], D_MODEL)
    hb = (xn * (1.0 + sc) + sh).astype(BF16)

    zq = _dot(hb, w_ref[:, 0:A_WIDTH])
    q_ref[...] = (_head_rms(zq, bd_ref, gq_ref[...]) * (A_HEAD_DIM ** -0.5)).astype(BF16)
    zk = _dot(hb, w_ref[:, A_WIDTH:2 * A_WIDTH])
    kn = _head_rms(zk, bd_ref, gk_ref[...])
    k_ref[...] = kn.astype(BF16)
    kf_ref[...] = kn
    zv = _dot(hb, w_ref[:, 2 * A_WIDTH:3 * A_WIDTH])
    v_ref[...] = zv.astype(BF16)
    vf_ref[...] = zv

    o = 3 * A_WIDTH
    zqb = _dot(hb, w_ref[:, o:o + B_KWIDTH]) * (B_DK ** -0.5)
    gla_ref[:, 0:B_KWIDTH] = zqb.astype(BF16)
    for c in range(B_KWIDTH, 2 * B_KWIDTH + 2 * B_WIDTH, 256):
        gla_ref[:, c:c + 256] = _dot(hb, w_ref[:, o + c:o + c + 256]).astype(BF16)

    gr = _dot(hb, w_ref[:, IN_MAIN:IN_PAD])
    logit = _dot(gr.astype(BF16), wgu_ref[...]) + bg_ref[...]
    log_sig = jnp.minimum(logit, 0.0) - jnp.log1p(jnp.exp(-jnp.abs(logit)))
    la_ref[...] = log_sig * (1.0 / GATE_TAU)


def _inproj(xp, xs, mod, gmix, w_in_p, bd, gq, gk, wgu_p, bg, n_ptiles, n_stiles, prep):
    n_tiles = n_ptiles + n_stiles
    t = n_tiles * TOK_TILE
    tail_tiles = 1 + n_stiles
    pblocks = prep // ROWS_PER_TILE
    const = lambda i: (0, 0)
    row = lambda i: (i, 0)
    tail = lambda i: (jnp.maximum(i - (n_ptiles - 1), 0), 0)
    return pl.pallas_call(
        functools.partial(_inproj_kernel, n_ptiles),
        grid=(n_tiles,),
        in_specs=[pl.BlockSpec((TOK_TILE, D_MODEL), lambda i: (jnp.minimum(i, n_ptiles - 1), 0)),
                  pl.BlockSpec((TOK_TILE, D_MODEL), lambda i: (jnp.maximum(i - n_ptiles, 0), 0)),
                  pl.BlockSpec((ROWS_PER_TILE, 6 * D_MODEL),
                               lambda i: (jnp.maximum(i - n_ptiles + pblocks, 0), 0)),
                  pl.BlockSpec((1, D_MODEL), const),
                  pl.BlockSpec((D_MODEL, IN_PAD), const),
                  pl.BlockSpec((A_WIDTH, A_WIDTH), const),
                  pl.BlockSpec((1, A_WIDTH), const),
                  pl.BlockSpec((1, A_WIDTH), const),
                  pl.BlockSpec((LANES, B_KWIDTH), const),
                  pl.BlockSpec((1, B_KWIDTH), const)],
        out_specs=[pl.BlockSpec((TOK_TILE, A_WIDTH), row),
                   pl.BlockSpec((TOK_TILE, A_WIDTH), row),
                   pl.BlockSpec((TOK_TILE, A_WIDTH), row),
                   pl.BlockSpec((TOK_TILE, A_WIDTH), tail),
                   pl.BlockSpec((TOK_TILE, A_WIDTH), tail),
                   pl.BlockSpec((TOK_TILE, 2 * B_KWIDTH + 2 * B_WIDTH), row),
                   pl.BlockSpec((TOK_TILE, B_KWIDTH), row)],
        out_shape=[jax.ShapeDtypeStruct((t, A_WIDTH), BF16),
                   jax.ShapeDtypeStruct((t, A_WIDTH), BF16),
                   jax.ShapeDtypeStruct((t, A_WIDTH), BF16),
                   jax.ShapeDtypeStruct((tail_tiles * TOK_TILE, A_WIDTH), F32),
                   jax.ShapeDtypeStruct((tail_tiles * TOK_TILE, A_WIDTH), F32),
                   jax.ShapeDtypeStruct((t, 2 * B_KWIDTH + 2 * B_WIDTH), BF16),
                   jax.ShapeDtypeStruct((t, B_KWIDTH), F32)],
        compiler_params=_params(),
        name="inproj",
    )(xp, xs, mod, gmix, w_in_p, bd, gq, gk, wgu_p, bg)


def _bias_lanes(n_keys):
    l = np.arange(ROLL_W)
    d = np.where(l < n_keys, BAND_PAST - l, BAND_PAST - l + ROLL_W)
    return np.clip(d, -(CHUNK - 1), MAX_REL) + (CHUNK - 1)


def _build_bias(u_ref, bias_sc, m_rows, n_keys):
    qi = lax.broadcasted_iota(jnp.int32, (m_rows, n_keys), 0) >> LOG_CHUNK
    kc = lax.broadcasted_iota(jnp.int32, (m_rows, n_keys), 1) >> LOG_CHUNK
    band = (kc >= qi) & (kc <= qi + BAND_CHUNKS)
    for h in range(A_HEADS):
        src = jnp.broadcast_to(u_ref[h:h + 1, :], (m_rows, ROLL_W))
        toe = pltpu.roll(src, 0, 1, stride=1, stride_axis=0)
        bias_sc[h] = jnp.where(band, toe[:, 0:n_keys], NEG)


def _attend(q, kcat, vcat, bias_sc, key_mask):
    m_rows = q.shape[0]
    lane = lax.broadcasted_iota(jnp.int32, (m_rows, LANES), 1)
    half = [lane < A_HEAD_DIM, lane >= A_HEAD_DIM]
    outs = []
    for p in range(A_HEADS // 2):
        lanes = slice(p * LANES, (p + 1) * LANES)
        qp, kp, vp = q[:, lanes], kcat[:, lanes], vcat[:, lanes]
        halves = []
        for hh in range(2):
            qm = jnp.where(half[hh], qp, jnp.zeros_like(qp))
            s = _dot_nt(qm, kp) + bias_sc[2 * p + hh]
            if key_mask is not None:
                s = s + key_mask
            e = jnp.exp(s - jnp.max(s, axis=-1, keepdims=True))
            l = jnp.sum(e, axis=-1, keepdims=True)
            halves.append(_dot(e.astype(BF16), vp) / l)
        outs.append(jnp.where(half[0], halves[0], halves[1]))
    return jnp.concatenate(outs, axis=-1)


def _attn_prompt_kernel(u_ref, q_ref, k0_ref, k1_ref, k2_ref, v0_ref, v1_ref, v2_ref, o_ref, bias_sc):
    j = pl.program_id(0)
    n_keys = 3 * Q_ROWS

    @pl.when(j == 0)
    def _():
        _build_bias(u_ref, bias_sc, Q_ROWS, n_keys)

    kcat = jnp.concatenate([k0_ref[...], k1_ref[...], k2_ref[...]], axis=0)
    vcat = jnp.concatenate([v0_ref[...], v1_ref[...], v2_ref[...]], axis=0)
    kw = lax.broadcasted_iota(jnp.int32, (1, n_keys), 1)
    key_mask = jnp.where(kw >= (2 - j) * Q_ROWS, 0.0, NEG)
    o_ref[...] = _attend(q_ref[...], kcat, vcat, bias_sc, key_mask).astype(BF16)


def _attn_prompt(u, q, k, v, n_steps):
    const = lambda j: (0, 0)
    blk = lambda d: pl.BlockSpec((Q_ROWS, A_WIDTH), lambda j, d=d: (jnp.maximum(j - d, 0), 0))
    return pl.pallas_call(
        _attn_prompt_kernel,
        grid=(n_steps,),
        in_specs=[pl.BlockSpec((A_HEADS, ROLL_W), const),
                  blk(0), blk(2), blk(1), blk(0), blk(2), blk(1), blk(0)],
        out_specs=pl.BlockSpec((Q_ROWS, A_WIDTH), lambda j: (j, 0)),
        out_shape=jax.ShapeDtypeStruct((n_steps * Q_ROWS, A_WIDTH), BF16),
        scratch_shapes=[pltpu.VMEM((A_HEADS, Q_ROWS, 3 * Q_ROWS), F32)],
        compiler_params=_params(),
        name="attn_prompt",
    )(u, q, k, k, k, v, v, v)


SAMPLE_KEYS = BAND_PAST + 2 * CHUNK


def _attn_sample_kernel(u_ref, q_ref, kn_ref, vn_ref, kc_ref, vc_ref, o_ref, bias_sc):
    @pl.when(pl.program_id(0) == 0)
    def _():
        _build_bias(u_ref, bias_sc, CHUNK, SAMPLE_KEYS)

    pad = jnp.zeros((CHUNK, A_WIDTH), BF16)
    kcat = jnp.concatenate([kc_ref[0].astype(BF16), kn_ref[...], pad], axis=0)
    vcat = jnp.concatenate([vc_ref[0].astype(BF16), vn_ref[...], pad], axis=0)
    o_ref[...] = _attend(q_ref[...], kcat, vcat, bias_sc, None).astype(BF16)


def _attn_sample(u, q, k, v, kc, vc, first_chunk, n_seq):
    new = pl.BlockSpec((CHUNK, A_WIDTH), lambda b: (first_chunk + b, 0))
    cache = pl.BlockSpec((1, BAND_PAST, A_WIDTH), lambda b: (b, 0, 0))
    return pl.pallas_call(
        _attn_sample_kernel,
        grid=(n_seq,),
        in_specs=[pl.BlockSpec((A_HEADS, ROLL_W), lambda b: (0, 0)), new, new, new, cache, cache],
        out_specs=pl.BlockSpec((CHUNK, A_WIDTH), lambda b: (b, 0)),
        out_shape=jax.ShapeDtypeStruct((n_seq * CHUNK, A_WIDTH), BF16),
        scratch_shapes=[pltpu.VMEM((A_HEADS, CHUNK, SAMPLE_KEYS), F32)],
        compiler_params=_params(),
        name="attn_sample",
    )(u, q, k, v, kc, vc)


GLA_CHUNKS = 4


def _gla_block(n_chunks, gla_ref, la_ref, ltri_ref, g_ref, st_sc, o_ref):
    rows = n_chunks * CHUNK
    la = la_ref[...]
    la_hi, la_lo = _split(la)
    b = _dot(ltri_ref[...], la_hi) + _dot(ltri_ref[...], la_lo)
    b3 = b.reshape(n_chunks, CHUNK, B_KWIDTH)
    b_mid = b3[:, CHUNK // 2 - 1:CHUNK // 2, :]
    b_last = b3[:, CHUNK - 1:CHUNK, :]
    q = gla_ref[:, 0:B_KWIDTH].astype(F32).reshape(n_chunks, CHUNK, B_KWIDTH)
    k = gla_ref[:, B_KWIDTH:2 * B_KWIDTH].astype(F32).reshape(n_chunks, CHUNK, B_KWIDTH)
    q_start = (q * jnp.exp(b3)).reshape(rows, B_KWIDTH).astype(BF16)
    q_mid = (q * jnp.exp(b3 - b_mid)).reshape(rows, B_KWIDTH).astype(BF16)
    k_mid = (k * jnp.exp(b_mid - b3)).reshape(rows, B_KWIDTH).astype(BF16)
    k_end = (k * jnp.exp(b_last - b3)).reshape(rows, B_KWIDTH).astype(BF16)
    dec = jnp.exp(b_last)

    ti = lax.broadcasted_iota(jnp.int32, (rows, rows), 0)
    si = lax.broadcasted_iota(jnp.int32, (rows, rows), 1)
    causal = (si <= ti) & ((si >> LOG_CHUNK) == (ti >> LOG_CHUNK))
    lane_r = lax.broadcasted_iota(jnp.int32, (rows, LANES), 1)
    half_r = [lane_r < B_DK, lane_r >= B_DK]
    lane_c = lax.broadcasted_iota(jnp.int32, (CHUNK, LANES), 1)
    half_c = [lane_c < B_DK, lane_c >= B_DK]
    half_s = lax.broadcasted_iota(jnp.int32, (B_DV, LANES), 1) < B_DK

    for p in range(B_HEADS // 2):
        lanes = slice(p * LANES, (p + 1) * LANES)
        qs_p, qm_p, km_p, ke_p = q_start[:, lanes], q_mid[:, lanes], k_mid[:, lanes], k_end[:, lanes]
        vs = [gla_ref[:, 2 * B_KWIDTH + (2 * p + hh) * B_DV:2 * B_KWIDTH + (2 * p + hh + 1) * B_DV]
              for hh in range(2)]
        intra = []
        for hh in range(2):
            qm = jnp.where(half_r[hh], qm_p, jnp.zeros_like(qm_p))
            sc = jnp.where(causal, _dot_nt(qm, km_p), 0.0)
            intra.append(_dot(sc.astype(BF16), vs[hh]))
        inter = [[], []]
        st = st_sc[p]
        for c in range(n_chunks):
            cr = slice(c * CHUNK, (c + 1) * CHUNK)
            st_b = st.astype(BF16)
            for hh in range(2):
                qc = jnp.where(half_c[hh], qs_p[cr], jnp.zeros((CHUNK, LANES), BF16))
                inter[hh].append(_dot_nt(qc, st_b))
            upd = jnp.where(half_s, _dot_tn(vs[0][cr], ke_p[cr]), _dot_tn(vs[1][cr], ke_p[cr]))
            st = st * dec[c, :, lanes] + upd
        st_sc[p] = st
        for hh in range(2):
            h = 2 * p + hh
            o = intra[hh] + jnp.concatenate(inter[hh], axis=0)
            ms = jnp.mean(o * o, axis=-1, keepdims=True)
            on = o * lax.rsqrt(ms + EPS) * g_ref[...]
            r = gla_ref[:, 2 * B_KWIDTH + B_WIDTH + h * B_DV:2 * B_KWIDTH + B_WIDTH + (h + 1) * B_DV]
            o_ref[:, h * B_DV:(h + 1) * B_DV] = (on * _silu(r.astype(F32))).astype(BF16)


def _gla_prompt_kernel(gla_ref, la_ref, ltri_ref, g_ref, o_ref, sfin_ref, st_sc):
    @pl.when(pl.program_id(0) == 0)
    def _():
        st_sc[...] = jnp.zeros_like(st_sc)

    _gla_block(GLA_CHUNKS, gla_ref, la_ref, ltri_ref, g_ref, st_sc, o_ref)
    sfin_ref[...] = st_sc[...]


def _gla_sample_kernel(gla_ref, la_ref, ltri_ref, g_ref, s0_ref, o_ref, sfin_ref, st_sc):
    st_sc[...] = s0_ref[0]
    _gla_block(1, gla_ref, la_ref, ltri_ref, g_ref, st_sc, o_ref)
    sfin_ref[0] = st_sc[...]


def _ltri(n_chunks):
    r = np.arange(n_chunks * CHUNK)
    m = (r[None, :] <= r[:, None]) & (r[None, :] // CHUNK == r[:, None] // CHUNK)
    return jnp.asarray(m, BF16)


_GLA_W = 2 * B_KWIDTH + 2 * B_WIDTH
_ST_SHAPE = (B_HEADS // 2, B_DV, LANES)


def _gla_prompt(gla, la, g, n_steps):
    rows = GLA_CHUNKS * CHUNK
    const = lambda j: (0, 0)
    return pl.pallas_call(
        _gla_prompt_kernel,
        grid=(n_steps,),
        in_specs=[pl.BlockSpec((rows, _GLA_W), lambda j: (j, 0)),
                  pl.BlockSpec((rows, B_KWIDTH), lambda j: (j, 0)),
                  pl.BlockSpec((rows, rows), const),
                  pl.BlockSpec((1, B_DV), const)],
        out_specs=[pl.BlockSpec((rows, B_WIDTH), lambda j: (j, 0)),
                   pl.BlockSpec(_ST_SHAPE, lambda j: (0, 0, 0))],
        out_shape=[jax.ShapeDtypeStruct((n_steps * rows, B_WIDTH), BF16),
                   jax.ShapeDtypeStruct(_ST_SHAPE, F32)],
        scratch_shapes=[pltpu.VMEM(_ST_SHAPE, F32)],
        compiler_params=_params(),
        name="gla_prompt",
    )(gla, la, _ltri(GLA_CHUNKS), g)


def _gla_sample(gla, la, g, s0, first_chunk, n_seq):
    const = lambda b: (0, 0)
    st_spec = pl.BlockSpec((1,) + _ST_SHAPE, lambda b: (b, 0, 0, 0))
    return pl.pallas_call(
        _gla_sample_kernel,
        grid=(n_seq,),
        in_specs=[pl.BlockSpec((CHUNK, _GLA_W), lambda b: (first_chunk + b, 0)),
                  pl.BlockSpec((CHUNK, B_KWIDTH), lambda b: (first_chunk + b, 0)),
                  pl.BlockSpec((CHUNK, CHUNK), const),
                  pl.BlockSpec((1, B_DV), const),
                  st_spec],
        out_specs=[pl.BlockSpec((CHUNK, B_WIDTH), lambda b: (b, 0)), st_spec],
        out_shape=[jax.ShapeDtypeStruct((n_seq * CHUNK, B_WIDTH), BF16),
                   jax.ShapeDtypeStruct((n_seq,) + _ST_SHAPE, F32)],
        scratch_shapes=[pltpu.VMEM(_ST_SHAPE, F32)],
        compiler_params=_params(),
        name="gla_sample",
    )(gla, la, _ltri(1), g, s0)


def _state_to_pairs(s):
    lead = s.shape[:-3]
    s = s.reshape(lead + (B_HEADS // 2, 2, B_DK, B_DV))
    s = jnp.moveaxis(s, -1, -3)
    return s.reshape(lead + (B_HEADS // 2, B_DV, 2 * B_DK))


def _pairs_to_state(s):
    lead = s.shape[:-3]
    s = s.reshape(lead + (B_HEADS // 2, B_DV, 2, B_DK))
    s = jnp.moveaxis(s, -3, -1)
    return s.reshape(lead + (B_HEADS, B_DK, B_DV))


def _route(logits):
    lane = lax.broadcasted_iota(jnp.int32, logits.shape, 1)
    lane_f = lane.astype(F32)
    big = float(LANES)
    gmask = lane < N_GROUPS
    gl = jnp.where(gmask, logits, NEG)
    gmax = jnp.max(gl, axis=-1, keepdims=True)
    gsel = jnp.min(jnp.where(gl == gmax, lane_f, big), axis=-1, keepdims=True)
    gsum = jnp.sum(jnp.where(gmask, jnp.exp(gl - gmax), 0.0), axis=-1, keepdims=True)
    g_w = 1.0 / gsum
    e_lo = ROUTE_OFF + gsel * EXPERTS_PER_GROUP
    emask = (lane_f >= e_lo) & (lane_f < e_lo + EXPERTS_PER_GROUP)
    el = jnp.where(emask, logits, NEG)
    v1 = jnp.max(el, axis=-1, keepdims=True)
    i1 = jnp.min(jnp.where(el == v1, lane_f, big), axis=-1, keepdims=True)
    el2 = jnp.where(lane_f == i1, NEG, el)
    v2 = jnp.max(el2, axis=-1, keepdims=True)
    i2 = jnp.min(jnp.where(el2 == v2, lane_f, big), axis=-1, keepdims=True)
    t = jnp.exp(v2 - v1)
    w1 = g_w / (1.0 + t)
    w2 = g_w * t / (1.0 + t)
    return lane_f, i1, i2, w1, w2


def _pack_rows(z32_sc, x, o_ref, rows):
    half = D_MODEL // 2
    for s in range(half // LANES):
        z32_sc[s, pl.ds(0, rows, stride=2), :] = x[:, s * LANES:(s + 1) * LANES]
        z32_sc[s, pl.ds(1, rows, stride=2), :] = x[:, half + s * LANES:half + (s + 1) * LANES]
        o_ref[:, s * LANES:(s + 1) * LANES] = pltpu.bitcast(z32_sc[s].astype(BF16), U32)


def _unpack_rows(z32_sc, words, rows):
    half = D_MODEL // 2
    lo, hi = [], []
    for s in range(half // LANES):
        z32_sc[s] = pltpu.bitcast(words[:, s * LANES:(s + 1) * LANES], BF16).astype(F32)
        lo.append(z32_sc[s, pl.ds(0, rows, stride=2), :])
        hi.append(z32_sc[s, pl.ds(1, rows, stride=2), :])
    return jnp.concatenate(lo, axis=1), jnp.concatenate(hi, axis=1)


def _outproj_kernel(n_ptiles, tiles_per_sb, oap_ref, oas_ref, obp_ref, obs_ref, wo_ref, xp_ref, xs_ref, mod_ref,
                    gffn_ref, wr_ref, br_ref, ltri_ref, x1_ref, h2p_ref, meta_ref, cnt_ref, z32_sc, cnt_sc):
    i = pl.program_id(0)
    is_prompt = i < n_ptiles
    x = jnp.where(is_prompt, xp_ref[...], xs_ref[...])
    oa = jnp.where(is_prompt, oap_ref[...], oas_ref[...])
    ob = jnp.where(is_prompt, obp_ref[...], obs_ref[...])
    mix = _dot(oa, wo_ref[0:A_WIDTH, :]) + _dot(ob, wo_ref[A_WIDTH:D_MODEL, :])
    gate1 = _rows_to_tokens(mod_ref[:, 2 * D_MODEL:3 * D_MODEL], D_MODEL)
    x1 = x + gate1 * mix
    x1_ref[...] = x1
    ms = jnp.mean(x1 * x1, axis=-1, keepdims=True)
    xn = x1 * lax.rsqrt(ms + EPS) * gffn_ref[...]
    sh = _rows_to_tokens(mod_ref[:, 3 * D_MODEL:4 * D_MODEL], D_MODEL)
    sc = _rows_to_tokens(mod_ref[:, 4 * D_MODEL:5 * D_MODEL], D_MODEL)
    h2 = xn * (1.0 + sc) + sh
    _pack_rows(z32_sc, h2, h2p_ref, TOK_TILE)

    lane_f, i1, i2, w1, w2 = _route(_dot3(h2, wr_ref[...]) + br_ref[...])

    @pl.when(lax.rem(i, tiles_per_sb) == 0)
    def _():
        cnt_sc[...] = jnp.zeros_like(cnt_sc)

    sel = jnp.where((lane_f == i1) | (lane_f == i2), 1.0, 0.0).astype(BF16)
    before = _dot(ltri_ref[...], sel) + cnt_sc[0:1, :]
    rank1 = jnp.sum(jnp.where(lane_f == i1, before, 0.0), axis=-1, keepdims=True)
    rank2 = jnp.sum(jnp.where(lane_f == i2, before, 0.0), axis=-1, keepdims=True)
    cnt = cnt_sc[...] + _dot(jnp.ones((8, TOK_TILE), BF16), sel)
    cnt_sc[...] = cnt
    cnt_ref[0] = cnt
    cols = (i1, i2, rank1, rank2, w1, w2)
    meta = jnp.zeros_like(lane_f)
    for c, col in enumerate(cols):
        meta = jnp.where(lane_f == float(c), col, meta)
    meta_ref[...] = meta


def _outproj(oa_p, oa_s, ob_p, ob_s, w_out, xp, xs, mod, gffn, wr, br, n_ptiles, n_stiles, prep, sb):
    n_tiles = n_ptiles + n_stiles
    t = n_tiles * TOK_TILE
    pblocks = prep // ROWS_PER_TILE
    tiles_per_sb = sb // TOK_TILE
    const = lambda i: (0, 0)
    row = lambda i: (i, 0)
    prow = lambda i: (jnp.minimum(i, n_ptiles - 1), 0)
    srow = lambda i: (jnp.maximum(i - n_ptiles, 0), 0)
    r = np.arange(TOK_TILE)
    ltri = jnp.asarray(r[None, :] < r[:, None], BF16)
    return pl.pallas_call(
        functools.partial(_outproj_kernel, n_ptiles, tiles_per_sb),
        grid=(n_tiles,),
        in_specs=[pl.BlockSpec((TOK_TILE, A_WIDTH), prow),
                  pl.BlockSpec((TOK_TILE, A_WIDTH), srow),
                  pl.BlockSpec((TOK_TILE, B_WIDTH), prow),
                  pl.BlockSpec((TOK_TILE, B_WIDTH), srow),
                  pl.BlockSpec((D_MODEL, D_MODEL), const),
                  pl.BlockSpec((TOK_TILE, D_MODEL), prow),
                  pl.BlockSpec((TOK_TILE, D_MODEL), srow),
                  pl.BlockSpec((ROWS_PER_TILE, 6 * D_MODEL),
                               lambda i: (jnp.maximum(i - n_ptiles + pblocks, 0), 0)),
                  pl.BlockSpec((1, D_MODEL), const),
                  pl.BlockSpec((D_MODEL, LANES), const),
                  pl.BlockSpec((1, LANES), const),
                  pl.BlockSpec((TOK_TILE, TOK_TILE), const)],
        out_specs=[pl.BlockSpec((TOK_TILE, D_MODEL), row),
                   pl.BlockSpec((TOK_TILE, D_MODEL // 2), row),
                   pl.BlockSpec((TOK_TILE, LANES), row),
                   pl.BlockSpec((1, 8, LANES), lambda i: (i // tiles_per_sb, 0, 0))],
        out_shape=[jax.ShapeDtypeStruct((t, D_MODEL), F32),
                   jax.ShapeDtypeStruct((t, D_MODEL // 2), U32),
                   jax.ShapeDtypeStruct((t, LANES), F32),
                   jax.ShapeDtypeStruct((t // sb, 8, LANES), F32)],
        scratch_shapes=[pltpu.VMEM((D_MODEL // 2 // LANES, 2 * TOK_TILE, LANES), F32),
                        pltpu.VMEM((8, LANES), F32)],
        compiler_params=_params(),
        name="outproj",
    )(oa_p, oa_s, ob_p, ob_s, w_out, xp, xs, mod, gffn, wr, br, ltri)


MOE_SUPER_BLOCK = 2048
SEG_ALIGN = 16
FFN_ROWS = 256
ROW_W = D_MODEL // 2
COPY_UNROLL = 8


def _sorted_rows(sb):
    return 2 * sb + N_EXPERTS * SEG_ALIGN + FFN_ROWS


def _moe_pos_kernel(meta_ref, cnt_ref, ustrict_ref, posw_ref, seg_ref):
    cnt = cnt_ref[0]
    units = jnp.floor((cnt + (SEG_ALIGN - 1)) * (1.0 / SEG_ALIGN))
    off = _dot(units.astype(BF16), ustrict_ref[...]) * SEG_ALIGN
    seg_ref[0] = jnp.concatenate([off[0:4], (units * SEG_ALIGN)[0:4]], axis=0)
    meta = meta_ref[...]
    lane_f = lax.broadcasted_iota(jnp.int32, meta.shape, 1).astype(F32)
    off_row = off[0:1, :]
    pos = []
    for k in range(2):
        e_lane = meta[:, k:k + 1]
        base = jnp.sum(jnp.where(lane_f == e_lane, off_row, 0.0), axis=-1, keepdims=True)
        pos.append(base + meta[:, 2 + k:3 + k])
    out = jnp.zeros_like(meta)
    for c, col in enumerate((pos[0], pos[1], meta[:, 4:5], meta[:, 5:6])):
        out = jnp.where(lane_f == float(c), col, out)
    posw_ref[...] = out


def _moe_pos(meta, cnt, sb):
    n_sb = meta.shape[0] // sb
    r = np.arange(LANES)
    ustrict = jnp.asarray(r[:, None] < r[None, :], BF16)
    return pl.pallas_call(
        _moe_pos_kernel,
        grid=(n_sb,),
        in_specs=[pl.BlockSpec((sb, LANES), lambda s: (s, 0)),
                  pl.BlockSpec((1, 8, LANES), lambda s: (s, 0, 0)),
                  pl.BlockSpec((LANES, LANES), lambda s: (0, 0))],
        out_specs=[pl.BlockSpec((sb, LANES), lambda s: (s, 0)),
                   pl.BlockSpec((1, 8, LANES), lambda s: (s, 0, 0))],
        out_shape=[jax.ShapeDtypeStruct(meta.shape, F32),
                   jax.ShapeDtypeStruct((n_sb, 8, LANES), F32)],
        compiler_params=_params(),
        name="moe_pos",
    )(meta, cnt, ustrict)


def _moe_dispatch_kernel(sb, h2p_ref, p1_ref, p2_ref, xs_ref):
    xs_ref[...] = jnp.zeros_like(xs_ref)

    def step(g, carry):
        for u in range(COPY_UNROLL):
            t = g * COPY_UNROLL + u
            row = h2p_ref[pl.ds(t, 1), :]
            xs_ref[0, pl.ds(p1_ref[t], 1), :] = row
            xs_ref[0, pl.ds(p2_ref[t], 1), :] = row
        return carry

    lax.fori_loop(0, sb // COPY_UNROLL, step, 0)


def _smem_vec(n, index_map):
    return pl.BlockSpec((n,), index_map, memory_space=pltpu.SMEM)


def _moe_dispatch(h2p, p1, p2, sb):
    n_sb = h2p.shape[0] // sb
    rs = _sorted_rows(sb)
    vec = _smem_vec(sb, lambda s: (s,))
    return pl.pallas_call(
        functools.partial(_moe_dispatch_kernel, sb),
        grid=(n_sb,),
        in_specs=[pl.BlockSpec((sb, ROW_W), lambda s: (s, 0)), vec, vec],
        out_specs=pl.BlockSpec((1, rs, ROW_W), lambda s: (s, 0, 0)),
        out_shape=jax.ShapeDtypeStruct((n_sb, rs, ROW_W), U32),
        compiler_params=_params(),
        name="moe_dispatch",
    )(h2p, p1, p2)


def _moe_ffn_kernel(off_ref, cnt_ref, xs_ref, wg_ref, wu_ref, wd_ref, ys_ref, z32_sc):
    s = pl.program_id(0)
    e = pl.program_id(1)
    half = D_MODEL // 2

    @pl.when(e == 0)
    def _():
        ys_ref[...] = jnp.zeros_like(ys_ref)

    seg = s * N_EXPERTS + e
    off = off_ref[seg]
    n_win = (cnt_ref[seg] + (FFN_ROWS - 1)) // FFN_ROWS

    def window(j, carry):
        r = pl.multiple_of(off + j * FFN_ROWS, SEG_ALIGN)
        lo, hi = _unpack_rows(z32_sc, xs_ref[0, pl.ds(r, FFN_ROWS), :], FFN_ROWS)
        lo, hi = lo.astype(BF16), hi.astype(BF16)
        g = _dot(lo, wg_ref[0, 0:half, :]) + _dot(hi, wg_ref[0, half:D_MODEL, :])
        u = _dot(lo, wu_ref[0, 0:half, :]) + _dot(hi, wu_ref[0, half:D_MODEL, :])
        y = _dot((_silu(g) * u).astype(BF16), wd_ref[0])
        _pack_rows(z32_sc, y, ys_ref.at[0, pl.ds(r, FFN_ROWS)], FFN_ROWS)
        return carry

    lax.fori_loop(0, n_win, window, 0)


def _moe_ffn(xs, seg_off, seg_cnt, wg, wu, wd):
    n_sb, rs, _ = xs.shape
    wspec = lambda shape: pl.BlockSpec((1,) + shape, lambda s, e, *_: (e, 0, 0))
    return pl.pallas_call(
        _moe_ffn_kernel,
        grid_spec=pltpu.PrefetchScalarGridSpec(
            num_scalar_prefetch=2,
            grid=(n_sb, N_EXPERTS),
            in_specs=[pl.BlockSpec((1, rs, ROW_W), lambda s, e, *_: (s, 0, 0)),
                      wspec((D_MODEL, EXPERT_FF)), wspec((D_MODEL, EXPERT_FF)), wspec((EXPERT_FF, D_MODEL))],
            out_specs=pl.BlockSpec((1, rs, D_MODEL // 2), lambda s, e, *_: (s, 0, 0)),
            scratch_shapes=[pltpu.VMEM((D_MODEL // 2 // LANES, 2 * FFN_ROWS, LANES), F32)]),
        out_shape=jax.ShapeDtypeStruct((n_sb, rs, D_MODEL // 2), U32),
        compiler_params=_params(2),
        name="moe_ffn",
    )(seg_off, seg_cnt, xs, wg, wu, wd)


def _moe_combine_kernel(n_psb, ys_ref, p1_ref, p2_ref, posw_ref, x1_ref, mod_ref, yp_ref, yo_ref,
                        g1_sc, g2_sc, z32_sc):
    s = pl.program_id(0)

    def step(g, carry):
        base = pl.multiple_of(g * COPY_UNROLL, COPY_UNROLL)
        for p_ref, g_sc in ((p1_ref, g1_sc), (p2_ref, g2_sc)):
            rows = [ys_ref[0, pl.ds(p_ref[base + u], 1), :] for u in range(COPY_UNROLL)]
            g_sc[pl.ds(base, COPY_UNROLL), :] = jnp.concatenate(rows, axis=0)
        return carry

    lax.fori_loop(0, TOK_TILE // COPY_UNROLL, step, 0)
    lo1, hi1 = _unpack_rows(z32_sc, g1_sc[...], TOK_TILE)
    lo2, hi2 = _unpack_rows(z32_sc, g2_sc[...], TOK_TILE)
    w1, w2 = posw_ref[:, 2:3], posw_ref[:, 3:4]
    moe = jnp.concatenate([w1 * lo1 + w2 * lo2, w1 * hi1 + w2 * hi2], axis=1)
    gate2 = _rows_to_tokens(mod_ref[:, 5 * D_MODEL:6 * D_MODEL], D_MODEL)
    y = x1_ref[...] + gate2 * moe

    @pl.when(s < n_psb)
    def _():
        yp_ref[...] = y

    @pl.when(s >= n_psb)
    def _():
        yo_ref[...] = y


def _moe_combine(ys, p1, p2, posw, x1, mod, sb, n_ptiles, n_stiles, prep):
    n_sb, rs, _ = ys.shape
    tps = sb // TOK_TILE
    n_psb = n_ptiles // tps
    pblocks = prep // ROWS_PER_TILE
    tile = lambda s, j: s * tps + j
    vec = _smem_vec(TOK_TILE, lambda s, j: (tile(s, j),))
    return pl.pallas_call(
        functools.partial(_moe_combine_kernel, n_psb),
        grid=(n_sb, tps),
        in_specs=[pl.BlockSpec((1, rs, D_MODEL // 2), lambda s, j: (s, 0, 0)), vec, vec,
                  pl.BlockSpec((TOK_TILE, LANES), lambda s, j: (tile(s, j), 0)),
                  pl.BlockSpec((TOK_TILE, D_MODEL), lambda s, j: (tile(s, j), 0)),
                  pl.BlockSpec((ROWS_PER_TILE, 6 * D_MODEL),
                               lambda s, j: (jnp.maximum(tile(s, j) - n_ptiles + pblocks, 0), 0))],
        out_specs=[pl.BlockSpec((TOK_TILE, D_MODEL), lambda s, j: (jnp.minimum(tile(s, j), n_ptiles - 1), 0)),
                   pl.BlockSpec((TOK_TILE, D_MODEL), lambda s, j: (jnp.maximum(tile(s, j) - n_ptiles, 0), 0))],
        out_shape=[jax.ShapeDtypeStruct((n_ptiles * TOK_TILE, D_MODEL), F32),
                   jax.ShapeDtypeStruct((n_stiles * TOK_TILE, D_MODEL), F32)],
        scratch_shapes=[pltpu.VMEM((TOK_TILE, D_MODEL // 2), U32),
                        pltpu.VMEM((TOK_TILE, D_MODEL // 2), U32),
                        pltpu.VMEM((D_MODEL // 2 // LANES, 2 * TOK_TILE, LANES), F32)],
        compiler_params=_params(2),
        name="moe_combine",
    )(ys, p1, p2, posw, x1, mod)


def _layer(xp, xs, cache_k, cache_v, state, c_prompt, c_sample, norm_mix_g, norm_ffn_g, w_ada, b_ada, w_in,
           q_norm_g, k_norm_g, rel_bias, w_gate_up, b_gate, gla_norm_g, w_out, w_route_group,
           b_route_group, w_route_expert, b_route_expert, w_exp_gate, w_exp_up, w_exp_down):
    batch, seq, _ = xp.shape
    n_seq, dec_seq, _ = xs.shape
    assert batch == 1 and dec_seq == CHUNK and cache_k.shape[1] == BAND_PAST
    assert seq % TOK_TILE == 0 and seq >= BAND_PAST and (n_seq * CHUNK) % TOK_TILE == 0
    n_ptok, n_stok = seq, n_seq * CHUNK
    n_ptiles, n_stiles = n_ptok // TOK_TILE, n_stok // TOK_TILE
    sb = MOE_SUPER_BLOCK if (n_ptok % MOE_SUPER_BLOCK == 0 and n_stok % MOE_SUPER_BLOCK == 0) else TOK_TILE
    prep = ROWS_PER_TILE

    xp2 = xp.reshape(n_ptok, D_MODEL)
    xs2 = xs.reshape(n_stok, D_MODEL)
    c_rows = jnp.concatenate([jnp.broadcast_to(c_prompt, (prep, D_MODEL)), c_sample], axis=0)
    mod = _adaln(c_rows, w_ada, b_ada)

    w_in_p = jnp.pad(w_in, ((0, 0), (0, IN_PAD - w_in.shape[1]))).astype(BF16)
    wgu_p = jnp.pad(w_gate_up, ((0, LANES - GATE_RANK), (0, 0))).astype(BF16)
    head = np.arange(A_WIDTH) // A_HEAD_DIM
    bd = jnp.asarray(head[:, None] == head[None, :], BF16)
    gq = jnp.tile(q_norm_g, A_HEADS).reshape(1, A_WIDTH)
    gk = jnp.tile(k_norm_g, A_HEADS).reshape(1, A_WIDTH)
    q, k, v, kf, vf, gla, la = _inproj(
        xp2, xs2, mod, norm_mix_g.reshape(1, D_MODEL), w_in_p, bd, gq, gk, wgu_p,
        b_gate.reshape(1, B_KWIDTH), n_ptiles, n_stiles, prep)

    first_chunk = n_ptok // CHUNK
    oa_p = _attn_prompt(rel_bias[:, _bias_lanes(3 * Q_ROWS)], q, k, v, n_ptok // Q_ROWS)
    oa_s = _attn_sample(rel_bias[:, _bias_lanes(SAMPLE_KEYS)], q, k, v,
                        cache_k.reshape(n_seq, BAND_PAST, A_WIDTH), cache_v.reshape(n_seq, BAND_PAST, A_WIDTH),
                        first_chunk, n_seq)
    g_gla = gla_norm_g.reshape(1, B_DV)
    ob_p, sfin_p = _gla_prompt(gla, la, g_gla, n_ptok // (GLA_CHUNKS * CHUNK))
    ob_s, sfin_s = _gla_sample(gla, la, g_gla, _state_to_pairs(state), first_chunk, n_seq)

    wr = jnp.pad(jnp.concatenate([w_route_group, w_route_expert], axis=1),
                 ((0, 0), (0, LANES - N_GROUPS - N_EXPERTS)))
    br = jnp.pad(jnp.concatenate([b_route_group, b_route_expert]), (0, LANES - N_GROUPS - N_EXPERTS))
    x1, h2p, meta, cnt = _outproj(oa_p, oa_s, ob_p, ob_s, w_out.astype(BF16), xp2, xs2, mod,
                                  norm_ffn_g.reshape(1, D_MODEL), wr, br.reshape(1, LANES),
                                  n_ptiles, n_stiles, prep, sb)

    posw, seg = _moe_pos(meta, cnt, sb)
    p1, p2 = posw[:, 0].astype(jnp.int32), posw[:, 1].astype(jnp.int32)
    experts = slice(ROUTE_OFF, ROUTE_OFF + N_EXPERTS)
    seg_off = seg[:, 0, experts].astype(jnp.int32).reshape(-1)
    seg_cnt = seg[:, 4, experts].astype(jnp.int32).reshape(-1)
    xs_sorted = _moe_dispatch(h2p, p1, p2, sb)
    ys_sorted = _moe_ffn(xs_sorted, seg_off, seg_cnt,
                         w_exp_gate.astype(BF16), w_exp_up.astype(BF16), w_exp_down.astype(BF16))
    yp, ys = _moe_combine(ys_sorted, p1, p2, posw, x1, mod, sb, n_ptiles, n_stiles, prep)

    tail = min(BAND_PAST, seq)
    heads = (A_HEADS, A_HEAD_DIM)
    return (yp.reshape(1, seq, D_MODEL), ys.reshape(n_seq, CHUNK, D_MODEL),
            kf[TOK_TILE - tail:TOK_TILE].reshape((1, tail) + heads),
            vf[TOK_TILE - tail:TOK_TILE].reshape((1, tail) + heads),
            _pairs_to_state(sfin_p)[None],
            kf[TOK_TILE:].reshape((n_seq, CHUNK) + heads),
            vf[TOK_TILE:].reshape((n_seq, CHUNK) + heads),
            _pairs_to_state(sfin_s))


def kernel(x_prompt, x_sample, cache_a_k, cache_a_v, state_gla, c_prompt, c_sample, norm_mix_g, norm_ffn_g,
           w_ada, b_ada, w_in, q_norm_g, k_norm_g, rel_bias, w_gate_up, b_gate, gla_norm_g, w_out,
           w_route_group, b_route_group, w_route_expert, b_route_expert, w_exp_gate, w_exp_up, w_exp_down):
    depth = w_in.shape[0]
    yp, ys = x_prompt, x_sample
    outs = [[] for _ in range(6)]
    for l in range(depth):
        yp, ys, kp, vp, sp, ks, vs, ss = _layer(
            yp, ys, cache_a_k[l], cache_a_v[l], state_gla[l], c_prompt, c_sample, norm_mix_g[l], norm_ffn_g[l],
            w_ada[l], b_ada[l], w_in[l], q_norm_g[l], k_norm_g[l], rel_bias[l], w_gate_up[l], b_gate[l],
            gla_norm_g[l], w_out[l], w_route_group[l], b_route_group[l], w_route_expert[l], b_route_expert[l],
            w_exp_gate[l], w_exp_up[l], w_exp_down[l])
        for lst, val in zip(outs, (kp, vp, sp, ks, vs, ss)):
            lst.append(val)
    return (yp, ys) + tuple(jnp.stack(o) for o in outs)
```

```python
import functools

import numpy as np
import jax
import jax.numpy as jnp
from jax import lax
from jax.experimental import pallas as pl
from jax.experimental.pallas import tpu as pltpu

F32 = jnp.float32
BF16 = jnp.bfloat16
U32 = jnp.uint32

D_MODEL = 1024
CHUNK = 64
LOG_CHUNK = 6
BAND_CHUNKS = 8
BAND_PAST = BAND_CHUNKS * CHUNK
A_WIDTH = 512
A_HEADS = 8
A_HEAD_DIM = 64
MAX_REL = 128
N_REL = CHUNK + MAX_REL
B_WIDTH = 512
B_HEADS = 4
B_DV = 128
B_DK = 64
B_KWIDTH = 256
GATE_RANK = 16
GATE_TAU = 16.0
N_GROUPS = 4
EXPERTS_PER_GROUP = 8
N_EXPERTS = 32
EXPERT_FF = 256
EPS = 1e-6

LANES = 128
IN_MAIN = 3 * A_WIDTH + 2 * B_KWIDTH + 2 * B_WIDTH
IN_PAD = IN_MAIN + LANES
TOK_TILE = 512
ROWS_PER_TILE = TOK_TILE // CHUNK
Q_CHUNKS = 4
Q_ROWS = Q_CHUNKS * CHUNK
ROLL_W = 1024
NEG = -1e30
ROUTE_OFF = N_GROUPS
VMEM_LIMIT = 56 * 1024 * 1024


def _params(n_axes=1):
    return pltpu.CompilerParams(dimension_semantics=("arbitrary",) * n_axes,
                                vmem_limit_bytes=VMEM_LIMIT)


def _split(a):
    hi = a.astype(BF16)
    lo = (a - hi.astype(F32)).astype(BF16)
    return hi, lo


def _dot(a, b):
    return jnp.dot(a, b, preferred_element_type=F32)


def _dot3(a, b):
    ah, al = _split(a)
    bh, bl = _split(b)
    return _dot(ah, bh) + _dot(al, bh) + _dot(ah, bl)


def _dot_nt(a, b):
    return lax.dot_general(a, b, (((1,), (1,)), ((), ())), preferred_element_type=F32)


def _dot_tn(a, b):
    return lax.dot_general(a, b, (((0,), (0,)), ((), ())), preferred_element_type=F32)


def _silu(x):
    return x / (1.0 + jnp.exp(-x))


def _rows_to_tokens(rows, n):
    r = rows.shape[0]
    return jnp.broadcast_to(rows[:, None, :], (r, CHUNK, n)).reshape(r * CHUNK, n)


def _adaln_kernel(c_ref, w_ref, b_ref, o_ref):
    a = _silu(c_ref[...])
    o_ref[...] = _dot3(a, w_ref[...]) + b_ref[...]


def _adaln(c_rows, w_ada, b_ada):
    r = c_rows.shape[0]
    n = w_ada.shape[1]
    tn = 1024
    return pl.pallas_call(
        _adaln_kernel,
        grid=(n // tn,),
        in_specs=[pl.BlockSpec((r, D_MODEL), lambda j: (0, 0)),
                  pl.BlockSpec((D_MODEL, tn), lambda j: (0, j)),
                  pl.BlockSpec((1, tn), lambda j: (0, j))],
        out_specs=pl.BlockSpec((r, tn), lambda j: (0, j)),
        out_shape=jax.ShapeDtypeStruct((r, n), F32),
        compiler_params=_params(),
        name="adaln",
    )(c_rows, w_ada, b_ada.reshape(1, n))


def _head_rms(z, bd_ref, g):
    ms = _dot((z * z).astype(BF16), bd_ref[...]) * (1.0 / A_HEAD_DIM)
    return z * lax.rsqrt(ms + EPS) * g


def _inproj_kernel(n_ptiles, xp_ref, xs_ref, mod_ref, gmix_ref, w_ref, bd_ref, gq_ref, gk_ref,
                   wgu_ref, bg_ref,
                   q_ref, k_ref, v_ref, kf_ref, vf_ref, gla_ref, la_ref):
    i = pl.program_id(0)
    x = jnp.where(i < n_ptiles, xp_ref[...], xs_ref[...])
    ms = jnp.mean(x * x, axis=-1, keepdims=True)
    xn = x * lax.rsqrt(ms + EPS) * gmix_ref[...]
    sh = _rows_to_tokens(mod_ref[:, 0:D_MODEL], D_MODEL)
    sc = _rows_to_tokens(mod_ref[:, D_MODEL:2 * D_MODEL], D_MODEL)
    hb = (xn * (1.0 + sc) + sh).astype(BF16)

    zq = _dot(hb, w_ref[:, 0:A_WIDTH])
    q_ref[...] = (_head_rms(zq, bd_ref, gq_ref[...]) * (A_HEAD_DIM ** -0.5)).astype(BF16)
    zk = _dot(hb, w_ref[:, A_WIDTH:2 * A_WIDTH])
    kn = _head_rms(zk, bd_ref, gk_ref[...])
    k_ref[...] = kn.astype(BF16)
    kf_ref[...] = kn
    zv = _dot(hb, w_ref[:, 2 * A_WIDTH:3 * A_WIDTH])
    v_ref[...] = zv.astype(BF16)
    vf_ref[...] = zv

    o = 3 * A_WIDTH
    zqb = _dot(hb, w_ref[:, o:o + B_KWIDTH]) * (B_DK ** -0.5)
    gla_ref[:, 0:B_KWIDTH] = zqb.astype(BF16)
    for c in range(B_KWIDTH, 2 * B_KWIDTH + 2 * B_WIDTH, 256):
        gla_ref[:, c:c + 256] = _dot(hb, w_ref[:, o + c:o + c + 256]).astype(BF16)

    gr = _dot(hb, w_ref[:, IN_MAIN:IN_PAD])
    logit = _dot(gr.astype(BF16), wgu_ref[...]) + bg_ref[...]
    log_sig = jnp.minimum(logit, 0.0) - jnp.log1p(jnp.exp(-jnp.abs(logit)))
    la_ref[...] = log_sig * (1.0 / GATE_TAU)


def _inproj(xp, xs, mod, gmix, w_in_p, bd, gq, gk, wgu_p, bg, n_ptiles, n_stiles, prep):
    n_tiles = n_ptiles + n_stiles
    t = n_tiles * TOK_TILE
    tail_tiles = 1 + n_stiles
    pblocks = prep // ROWS_PER_TILE
    const = lambda i: (0, 0)
    row = lambda i: (i, 0)
    tail = lambda i: (jnp.maximum(i - (n_ptiles - 1), 0), 0)
    return pl.pallas_call(
        functools.partial(_inproj_kernel, n_ptiles),
        grid=(n_tiles,),
        in_specs=[pl.BlockSpec((TOK_TILE, D_MODEL), lambda i: (jnp.minimum(i, n_ptiles - 1), 0)),
                  pl.BlockSpec((TOK_TILE, D_MODEL), lambda i: (jnp.maximum(i - n_ptiles, 0), 0)),
                  pl.BlockSpec((ROWS_PER_TILE, 6 * D_MODEL),
                               lambda i: (jnp.maximum(i - n_ptiles + pblocks, 0), 0)),
                  pl.BlockSpec((1, D_MODEL), const),
                  pl.BlockSpec((D_MODEL, IN_PAD), const),
                  pl.BlockSpec((A_WIDTH, A_WIDTH), const),
                  pl.BlockSpec((1, A_WIDTH), const),
                  pl.BlockSpec((1, A_WIDTH), const),
                  pl.BlockSpec((LANES, B_KWIDTH), const),
                  pl.BlockSpec((1, B_KWIDTH), const)],
        out_specs=[pl.BlockSpec((TOK_TILE, A_WIDTH), row),
                   pl.BlockSpec((TOK_TILE, A_WIDTH), row),
                   pl.BlockSpec((TOK_TILE, A_WIDTH), row),
                   pl.BlockSpec((TOK_TILE, A_WIDTH), tail),
                   pl.BlockSpec((TOK_TILE, A_WIDTH), tail),
                   pl.BlockSpec((TOK_TILE, 2 * B_KWIDTH + 2 * B_WIDTH), row),
                   pl.BlockSpec((TOK_TILE, B_KWIDTH), row)],
        out_shape=[jax.ShapeDtypeStruct((t, A_WIDTH), BF16),
                   jax.ShapeDtypeStruct((t, A_WIDTH), BF16),
                   jax.ShapeDtypeStruct((t, A_WIDTH), BF16),
                   jax.ShapeDtypeStruct((tail_tiles * TOK_TILE, A_WIDTH), F32),
                   jax.ShapeDtypeStruct((tail_tiles * TOK_TILE, A_WIDTH), F32),
                   jax.ShapeDtypeStruct((t, 2 * B_KWIDTH + 2 * B_WIDTH), BF16),
                   jax.ShapeDtypeStruct((t, B_KWIDTH), F32)],
        compiler_params=_params(),
        name="inproj",
    )(xp, xs, mod, gmix, w_in_p, bd, gq, gk, wgu_p, bg)


def _bias_lanes(n_keys):
    l = np.arange(ROLL_W)
    d = np.where(l < n_keys, BAND_PAST - l, BAND_PAST - l + ROLL_W)
    return np.clip(d, -(CHUNK - 1), MAX_REL) + (CHUNK - 1)


def _build_bias(u_ref, bias_sc, m_rows, n_keys):
    qi = lax.broadcasted_iota(jnp.int32, (m_rows, n_keys), 0) >> LOG_CHUNK
    kc = lax.broadcasted_iota(jnp.int32, (m_rows, n_keys), 1) >> LOG_CHUNK
    band = (kc >= qi) & (kc <= qi + BAND_CHUNKS)
    for h in range(A_HEADS):
        src = jnp.broadcast_to(u_ref[h:h + 1, :], (m_rows, ROLL_W))
        toe = pltpu.roll(src, 0, 1, stride=1, stride_axis=0)
        bias_sc[h] = jnp.where(band, toe[:, 0:n_keys], NEG)


def _attend(q, kcat, vcat, bias_sc, key_mask):
    m_rows = q.shape[0]
    lane = lax.broadcasted_iota(jnp.int32, (m_rows, LANES), 1)
    half = [lane < A_HEAD_DIM, lane >= A_HEAD_DIM]
    outs = []
    for p in range(A_HEADS // 2):
        lanes = slice(p * LANES, (p + 1) * LANES)
        qp, kp, vp = q[:, lanes], kcat[:, lanes], vcat[:, lanes]
        halves = []
        for hh in range(2):
            qm = jnp.where(half[hh], qp, jnp.zeros_like(qp))
            s = _dot_nt(qm, kp) + bias_sc[2 * p + hh]
            if key_mask is not None:
                s = s + key_mask
            e = jnp.exp(s - jnp.max(s, axis=-1, keepdims=True))
            l = jnp.sum(e, axis=-1, keepdims=True)
            halves.append(_dot(e.astype(BF16), vp) / l)
        outs.append(jnp.where(half[0], halves[0], halves[1]))
    return jnp.concatenate(outs, axis=-1)


def _attn_prompt_kernel(u_ref, q_ref, k0_ref, k1_ref, k2_ref, v0_ref, v1_ref, v2_ref, o_ref, bias_sc):
    j = pl.program_id(0)
    n_keys = 3 * Q_ROWS

    @pl.when(j == 0)
    def _():
        _build_bias(u_ref, bias_sc, Q_ROWS, n_keys)

    kcat = jnp.concatenate([k0_ref[...], k1_ref[...], k2_ref[...]], axis=0)
    vcat = jnp.concatenate([v0_ref[...], v1_ref[...], v2_ref[...]], axis=0)
    kw = lax.broadcasted_iota(jnp.int32, (1, n_keys), 1)
    key_mask = jnp.where(kw >= (2 - j) * Q_ROWS, 0.0, NEG)
    o_ref[...] = _attend(q_ref[...], kcat, vcat, bias_sc, key_mask).astype(BF16)


def _attn_prompt(u, q, k, v, n_steps):
    const = lambda j: (0, 0)
    blk = lambda d: pl.BlockSpec((Q_ROWS, A_WIDTH), lambda j, d=d: (jnp.maximum(j - d, 0), 0))
    return pl.pallas_call(
        _attn_prompt_kernel,
        grid=(n_steps,),
        in_specs=[pl.BlockSpec((A_HEADS, ROLL_W), const),
                  blk(0), blk(2), blk(1), blk(0), blk(2), blk(1), blk(0)],
        out_specs=pl.BlockSpec((Q_ROWS, A_WIDTH), lambda j: (j, 0)),
        out_shape=jax.ShapeDtypeStruct((n_steps * Q_ROWS, A_WIDTH), BF16),
        scratch_shapes=[pltpu.VMEM((A_HEADS, Q_ROWS, 3 * Q_ROWS), F32)],
        compiler_params=_params(),
        name="attn_prompt",
    )(u, q, k, k, k, v, v, v)


SAMPLE_KEYS = BAND_PAST + 2 * CHUNK


def _attn_sample_kernel(u_ref, q_ref, kn_ref, vn_ref, kc_ref, vc_ref, o_ref, bias_sc):
    @pl.when(pl.program_id(0) == 0)
    def _():
        _build_bias(u_ref, bias_sc, CHUNK, SAMPLE_KEYS)

    pad = jnp.zeros((CHUNK, A_WIDTH), BF16)
    kcat = jnp.concatenate([kc_ref[0].astype(BF16), kn_ref[...], pad], axis=0)
    vcat = jnp.concatenate([vc_ref[0].astype(BF16), vn_ref[...], pad], axis=0)
    o_ref[...] = _attend(q_ref[...], kcat, vcat, bias_sc, None).astype(BF16)


def _attn_sample(u, q, k, v, kc, vc, first_chunk, n_seq):
    new = pl.BlockSpec((CHUNK, A_WIDTH), lambda b: (first_chunk + b, 0))
    cache = pl.BlockSpec((1, BAND_PAST, A_WIDTH), lambda b: (b, 0, 0))
    return pl.pallas_call(
        _attn_sample_kernel,
        grid=(n_seq,),
        in_specs=[pl.BlockSpec((A_HEADS, ROLL_W), lambda b: (0, 0)), new, new, new, cache, cache],
        out_specs=pl.BlockSpec((CHUNK, A_WIDTH), lambda b: (b, 0)),
        out_shape=jax.ShapeDtypeStruct((n_seq * CHUNK, A_WIDTH), BF16),
        scratch_shapes=[pltpu.VMEM((A_HEADS, CHUNK, SAMPLE_KEYS), F32)],
        compiler_params=_params(),
        name="attn_sample",
    )(u, q, k, v, kc, vc)


GLA_CHUNKS = 4


def _gla_block(n_chunks, gla_ref, la_ref, ltri_ref, g_ref, st_sc, o_ref):
    rows = n_chunks * CHUNK
    la = la_ref[...]
    la_hi, la_lo = _split(la)
    b = _dot(ltri_ref[...], la_hi) + _dot(ltri_ref[...], la_lo)
    b3 = b.reshape(n_chunks, CHUNK, B_KWIDTH)
    b_mid = b3[:, CHUNK // 2 - 1:CHUNK // 2, :]
    b_last = b3[:, CHUNK - 1:CHUNK, :]
    q = gla_ref[:, 0:B_KWIDTH].astype(F32).reshape(n_chunks, CHUNK, B_KWIDTH)
    k = gla_ref[:, B_KWIDTH:2 * B_KWIDTH].astype(F32).reshape(n_chunks, CHUNK, B_KWIDTH)
    q_start = (q * jnp.exp(b3)).reshape(rows, B_KWIDTH).astype(BF16)
    q_mid = (q * jnp.exp(b3 - b_mid)).reshape(rows, B_KWIDTH).astype(BF16)
    k_mid = (k * jnp.exp(b_mid - b3)).reshape(rows, B_KWIDTH).astype(BF16)
    k_end = (k * jnp.exp(b_last - b3)).reshape(rows, B_KWIDTH).astype(BF16)
    dec = jnp.exp(b_last)

    ti = lax.broadcasted_iota(jnp.int32, (rows, rows), 0)
    si = lax.broadcasted_iota(jnp.int32, (rows, rows), 1)
    causal = (si <= ti) & ((si >> LOG_CHUNK) == (ti >> LOG_CHUNK))
    lane_r = lax.broadcasted_iota(jnp.int32, (rows, LANES), 1)
    half_r = [lane_r < B_DK, lane_r >= B_DK]
    lane_c = lax.broadcasted_iota(jnp.int32, (CHUNK, LANES), 1)
    half_c = [lane_c < B_DK, lane_c >= B_DK]
    half_s = lax.broadcasted_iota(jnp.int32, (B_DV, LANES), 1) < B_DK

    for p in range(B_HEADS // 2):
        lanes = slice(p * LANES, (p + 1) * LANES)
        qs_p, qm_p, km_p, ke_p = q_start[:, lanes], q_mid[:, lanes], k_mid[:, lanes], k_end[:, lanes]
        vs = [gla_ref[:, 2 * B_KWIDTH + (2 * p + hh) * B_DV:2 * B_KWIDTH + (2 * p + hh + 1) * B_DV]
              for hh in range(2)]
        intra = []
        for hh in range(2):
            qm = jnp.where(half_r[hh], qm_p, jnp.zeros_like(qm_p))
            sc = jnp.where(causal, _dot_nt(qm, km_p), 0.0)
            intra.append(_dot(sc.astype(BF16), vs[hh]))
        inter = [[], []]
        st = st_sc[p]
        for c in range(n_chunks):
            cr = slice(c * CHUNK, (c + 1) * CHUNK)
            st_b = st.astype(BF16)
            for hh in range(2):
                qc = jnp.where(half_c[hh], qs_p[cr], jnp.zeros((CHUNK, LANES), BF16))
                inter[hh].append(_dot_nt(qc, st_b))
            upd = jnp.where(half_s, _dot_tn(vs[0][cr], ke_p[cr]), _dot_tn(vs[1][cr], ke_p[cr]))
            st = st * dec[c, :, lanes] + upd
        st_sc[p] = st
        for hh in range(2):
            h = 2 * p + hh
            o = intra[hh] + jnp.concatenate(inter[hh], axis=0)
            ms = jnp.mean(o * o, axis=-1, keepdims=True)
            on = o * lax.rsqrt(ms + EPS) * g_ref[...]
            r = gla_ref[:, 2 * B_KWIDTH + B_WIDTH + h * B_DV:2 * B_KWIDTH + B_WIDTH + (h + 1) * B_DV]
            o_ref[:, h * B_DV:(h + 1) * B_DV] = (on * _silu(r.astype(F32))).astype(BF16)


def _gla_prompt_kernel(gla_ref, la_ref, ltri_ref, g_ref, o_ref, sfin_ref, st_sc):
    @pl.when(pl.program_id(0) == 0)
    def _():
        st_sc[...] = jnp.zeros_like(st_sc)

    _gla_block(GLA_CHUNKS, gla_ref, la_ref, ltri_ref, g_ref, st_sc, o_ref)
    sfin_ref[...] = st_sc[...]


def _gla_sample_kernel(gla_ref, la_ref, ltri_ref, g_ref, s0_ref, o_ref, sfin_ref, st_sc):
    st_sc[...] = s0_ref[0]
    _gla_block(1, gla_ref, la_ref, ltri_ref, g_ref, st_sc, o_ref)
    sfin_ref[0] = st_sc[...]


def _ltri(n_chunks):
    r = np.arange(n_chunks * CHUNK)
    m = (r[None, :] <= r[:, None]) & (r[None, :] // CHUNK == r[:, None] // CHUNK)
    return jnp.asarray(m, BF16)


_GLA_W = 2 * B_KWIDTH + 2 * B_WIDTH
_ST_SHAPE = (B_HEADS // 2, B_DV, LANES)


def _gla_prompt(gla, la, g, n_steps):
    rows = GLA_CHUNKS * CHUNK
    const = lambda j: (0, 0)
    return pl.pallas_call(
        _gla_prompt_kernel,
        grid=(n_steps,),
        in_specs=[pl.BlockSpec((rows, _GLA_W), lambda j: (j, 0)),
                  pl.BlockSpec((rows, B_KWIDTH), lambda j: (j, 0)),
                  pl.BlockSpec((rows, rows), const),
                  pl.BlockSpec((1, B_DV), const)],
        out_specs=[pl.BlockSpec((rows, B_WIDTH), lambda j: (j, 0)),
                   pl.BlockSpec(_ST_SHAPE, lambda j: (0, 0, 0))],
        out_shape=[jax.ShapeDtypeStruct((n_steps * rows, B_WIDTH), BF16),
                   jax.ShapeDtypeStruct(_ST_SHAPE, F32)],
        scratch_shapes=[pltpu.VMEM(_ST_SHAPE, F32)],
        compiler_params=_params(),
        name="gla_prompt",
    )(gla, la, _ltri(GLA_CHUNKS), g)


def _gla_sample(gla, la, g, s0, first_chunk, n_seq):
    const = lambda b: (0, 0)
    st_spec = pl.BlockSpec((1,) + _ST_SHAPE, lambda b: (b, 0, 0, 0))
    return pl.pallas_call(
        _gla_sample_kernel,
        grid=(n_seq,),
        in_specs=[pl.BlockSpec((CHUNK, _GLA_W), lambda b: (first_chunk + b, 0)),
                  pl.BlockSpec((CHUNK, B_KWIDTH), lambda b: (first_chunk + b, 0)),
                  pl.BlockSpec((CHUNK, CHUNK), const),
                  pl.BlockSpec((1, B_DV), const),
                  st_spec],
        out_specs=[pl.BlockSpec((CHUNK, B_WIDTH), lambda b: (b, 0)), st_spec],
        out_shape=[jax.ShapeDtypeStruct((n_seq * CHUNK, B_WIDTH), BF16),
                   jax.ShapeDtypeStruct((n_seq,) + _ST_SHAPE, F32)],
        scratch_shapes=[pltpu.VMEM(_ST_SHAPE, F32)],
        compiler_params=_params(),
        name="gla_sample",
    )(gla, la, _ltri(1), g, s0)


def _state_to_pairs(s):
    lead = s.shape[:-3]
    s = s.reshape(lead + (B_HEADS // 2, 2, B_DK, B_DV))
    s = jnp.moveaxis(s, -1, -3)
    return s.reshape(lead + (B_HEADS // 2, B_DV, 2 * B_DK))


def _pairs_to_state(s):
    lead = s.shape[:-3]
    s = s.reshape(lead + (B_HEADS // 2, B_DV, 2, B_DK))
    s = jnp.moveaxis(s, -3, -1)
    return s.reshape(lead + (B_HEADS, B_DK, B_DV))


def _route(logits):
    lane = lax.broadcasted_iota(jnp.int32, logits.shape, 1)
    lane_f = lane.astype(F32)
    big = float(LANES)
    gmask = lane < N_GROUPS
    gl = jnp.where(gmask, logits, NEG)
    gmax = jnp.max(gl, axis=-1, keepdims=True)
    gsel = jnp.min(jnp.where(gl == gmax, lane_f, big), axis=-1, keepdims=True)
    gsum = jnp.sum(jnp.where(gmask, jnp.exp(gl - gmax), 0.0), axis=-1, keepdims=True)
    g_w = 1.0 / gsum
    e_lo = ROUTE_OFF + gsel * EXPERTS_PER_GROUP
    emask = (lane_f >= e_lo) & (lane_f < e_lo + EXPERTS_PER_GROUP)
    el = jnp.where(emask, logits, NEG)
    v1 = jnp.max(el, axis=-1, keepdims=True)
    i1 = jnp.min(jnp.where(el == v1, lane_f, big), axis=-1, keepdims=True)
    el2 = jnp.where(lane_f == i1, NEG, el)
    v2 = jnp.max(el2, axis=-1, keepdims=True)
    i2 = jnp.min(jnp.where(el2 == v2, lane_f, big), axis=-1, keepdims=True)
    t = jnp.exp(v2 - v1)
    w1 = g_w / (1.0 + t)
    w2 = g_w * t / (1.0 + t)
    return lane_f, i1, i2, w1, w2


ROW_PIECES = D_MODEL // 2 // LANES
SUBLANES = 8
ROW_TILE = ROW_PIECES * SUBLANES


def _pack_rows(z32_sc, x, rows):
    half = D_MODEL // 2
    out = []
    for c in range(ROW_PIECES):
        z32_sc[c, pl.ds(0, rows, stride=2), :] = x[:, c * LANES:(c + 1) * LANES]
        z32_sc[c, pl.ds(1, rows, stride=2), :] = x[:, half + c * LANES:half + (c + 1) * LANES]
        out.append(pltpu.bitcast(z32_sc[c].astype(BF16), U32))
    return out


def _unpack_rows(z32_sc, pieces, rows):
    lo, hi = [], []
    for c in range(ROW_PIECES):
        z32_sc[c] = pltpu.bitcast(pieces[c], BF16).astype(F32)
        lo.append(z32_sc[c, pl.ds(0, rows, stride=2), :])
        hi.append(z32_sc[c, pl.ds(1, rows, stride=2), :])
    return jnp.concatenate(lo, axis=1), jnp.concatenate(hi, axis=1)


def _to_row_tiled(pieces, rows):
    return jnp.stack([p.reshape(rows // SUBLANES, SUBLANES, LANES) for p in pieces], axis=1)


def _from_row_tiled(tiled, rows):
    return [tiled[:, c].reshape(rows, LANES) for c in range(ROW_PIECES)]


def _outproj_kernel(n_ptiles, tiles_per_sb, oap_ref, oas_ref, obp_ref, obs_ref, wo_ref, xp_ref, xs_ref, mod_ref,
                    gffn_ref, wr_ref, br_ref, ltri_ref, x1_ref, h2p_ref, meta_ref, cnt_ref, z32_sc, cnt_sc):
    i = pl.program_id(0)
    is_prompt = i < n_ptiles
    x = jnp.where(is_prompt, xp_ref[...], xs_ref[...])
    oa = jnp.where(is_prompt, oap_ref[...], oas_ref[...])
    ob = jnp.where(is_prompt, obp_ref[...], obs_ref[...])
    mix = _dot(oa, wo_ref[0:A_WIDTH, :]) + _dot(ob, wo_ref[A_WIDTH:D_MODEL, :])
    gate1 = _rows_to_tokens(mod_ref[:, 2 * D_MODEL:3 * D_MODEL], D_MODEL)
    x1 = x + gate1 * mix
    x1_ref[...] = x1
    ms = jnp.mean(x1 * x1, axis=-1, keepdims=True)
    xn = x1 * lax.rsqrt(ms + EPS) * gffn_ref[...]
    sh = _rows_to_tokens(mod_ref[:, 3 * D_MODEL:4 * D_MODEL], D_MODEL)
    sc = _rows_to_tokens(mod_ref[:, 4 * D_MODEL:5 * D_MODEL], D_MODEL)
    h2 = xn * (1.0 + sc) + sh
    h2p_ref[...] = _to_row_tiled(_pack_rows(z32_sc, h2, TOK_TILE), TOK_TILE)

    lane_f, i1, i2, w1, w2 = _route(_dot3(h2, wr_ref[...]) + br_ref[...])

    @pl.when(lax.rem(i, tiles_per_sb) == 0)
    def _():
        cnt_sc[...] = jnp.zeros_like(cnt_sc)

    sel = jnp.where((lane_f == i1) | (lane_f == i2), 1.0, 0.0).astype(BF16)
    before = _dot(ltri_ref[...], sel) + cnt_sc[0:1, :]
    rank1 = jnp.sum(jnp.where(lane_f == i1, before, 0.0), axis=-1, keepdims=True)
    rank2 = jnp.sum(jnp.where(lane_f == i2, before, 0.0), axis=-1, keepdims=True)
    cnt = cnt_sc[...] + _dot(jnp.ones((8, TOK_TILE), BF16), sel)
    cnt_sc[...] = cnt
    cnt_ref[0] = cnt
    cols = (i1, i2, rank1, rank2, w1, w2)
    meta = jnp.zeros_like(lane_f)
    for c, col in enumerate(cols):
        meta = jnp.where(lane_f == float(c), col, meta)
    meta_ref[...] = meta


def _outproj(oa_p, oa_s, ob_p, ob_s, w_out, xp, xs, mod, gffn, wr, br, n_ptiles, n_stiles, prep, sb):
    n_tiles = n_ptiles + n_stiles
    t = n_tiles * TOK_TILE
    pblocks = prep // ROWS_PER_TILE
    tiles_per_sb = sb // TOK_TILE
    const = lambda i: (0, 0)
    row = lambda i: (i, 0)
    prow = lambda i: (jnp.minimum(i, n_ptiles - 1), 0)
    srow = lambda i: (jnp.maximum(i - n_ptiles, 0), 0)
    r = np.arange(TOK_TILE)
    ltri = jnp.asarray(r[None, :] < r[:, None], BF16)
    return pl.pallas_call(
        functools.partial(_outproj_kernel, n_ptiles, tiles_per_sb),
        grid=(n_tiles,),
        in_specs=[pl.BlockSpec((TOK_TILE, A_WIDTH), prow),
                  pl.BlockSpec((TOK_TILE, A_WIDTH), srow),
                  pl.BlockSpec((TOK_TILE, B_WIDTH), prow),
                  pl.BlockSpec((TOK_TILE, B_WIDTH), srow),
                  pl.BlockSpec((D_MODEL, D_MODEL), const),
                  pl.BlockSpec((TOK_TILE, D_MODEL), prow),
                  pl.BlockSpec((TOK_TILE, D_MODEL), srow),
                  pl.BlockSpec((ROWS_PER_TILE, 6 * D_MODEL),
                               lambda i: (jnp.maximum(i - n_ptiles + pblocks, 0), 0)),
                  pl.BlockSpec((1, D_MODEL), const),
                  pl.BlockSpec((D_MODEL, LANES), const),
                  pl.BlockSpec((1, LANES), const),
                  pl.BlockSpec((TOK_TILE, TOK_TILE), const)],
        out_specs=[pl.BlockSpec((TOK_TILE, D_MODEL), row),
                   pl.BlockSpec((TOK_TILE // SUBLANES, ROW_PIECES, SUBLANES, LANES), lambda i: (i, 0, 0, 0)),
                   pl.BlockSpec((TOK_TILE, LANES), row),
                   pl.BlockSpec((1, 8, LANES), lambda i: (i // tiles_per_sb, 0, 0))],
        out_shape=[jax.ShapeDtypeStruct((t, D_MODEL), F32),
                   jax.ShapeDtypeStruct((t // SUBLANES, ROW_PIECES, SUBLANES, LANES), U32),
                   jax.ShapeDtypeStruct((t, LANES), F32),
                   jax.ShapeDtypeStruct((t // sb, 8, LANES), F32)],
        scratch_shapes=[pltpu.VMEM((D_MODEL // 2 // LANES, 2 * TOK_TILE, LANES), F32),
                        pltpu.VMEM((8, LANES), F32)],
        compiler_params=_params(),
        name="outproj",
    )(oa_p, oa_s, ob_p, ob_s, w_out, xp, xs, mod, gffn, wr, br, ltri)


MOE_SUPER_BLOCK = 2048
SEG_ALIGN = 16
FFN_ROWS = 256


def _sorted_rows(sb):
    return 2 * sb + N_EXPERTS * SEG_ALIGN + FFN_ROWS


def _moe_pos_kernel(meta_ref, cnt_ref, ustrict_ref, posw_ref, seg_ref):
    cnt = cnt_ref[0]
    units = jnp.floor((cnt + (SEG_ALIGN - 1)) * (1.0 / SEG_ALIGN))
    off = _dot(units.astype(BF16), ustrict_ref[...]) * SEG_ALIGN
    seg_ref[0] = jnp.concatenate([off[0:4], (units * SEG_ALIGN)[0:4]], axis=0)
    meta = meta_ref[...]
    lane_f = lax.broadcasted_iota(jnp.int32, meta.shape, 1).astype(F32)
    off_row = off[0:1, :]
    pos = []
    for k in range(2):
        e_lane = meta[:, k:k + 1]
        base = jnp.sum(jnp.where(lane_f == e_lane, off_row, 0.0), axis=-1, keepdims=True)
        p = base + meta[:, 2 + k:3 + k]
        tile = jnp.floor(p * (1.0 / SUBLANES))
        pos.append(tile * (ROW_TILE - SUBLANES) + p)
    out = jnp.zeros_like(meta)
    for c, col in enumerate((pos[0], pos[1], meta[:, 4:5], meta[:, 5:6])):
        out = jnp.where(lane_f == float(c), col, out)
    posw_ref[...] = out


def _moe_pos(meta, cnt, sb):
    n_sb = meta.shape[0] // sb
    r = np.arange(LANES)
    ustrict = jnp.asarray(r[:, None] < r[None, :], BF16)
    return pl.pallas_call(
        _moe_pos_kernel,
        grid=(n_sb,),
        in_specs=[pl.BlockSpec((sb, LANES), lambda s: (s, 0)),
                  pl.BlockSpec((1, 8, LANES), lambda s: (s, 0, 0)),
                  pl.BlockSpec((LANES, LANES), lambda s: (0, 0))],
        out_specs=[pl.BlockSpec((sb, LANES), lambda s: (s, 0)),
                   pl.BlockSpec((1, 8, LANES), lambda s: (s, 0, 0))],
        out_shape=[jax.ShapeDtypeStruct(meta.shape, F32),
                   jax.ShapeDtypeStruct((n_sb, 8, LANES), F32)],
        compiler_params=_params(),
        name="moe_pos",
    )(meta, cnt, ustrict)


def _token_rows(start):
    return pl.ds(start, ROW_PIECES, stride=SUBLANES)


def _moe_dispatch_kernel(sb, h2p_ref, a1_ref, a2_ref, xs_ref):
    xs_ref[...] = jnp.zeros_like(xs_ref)

    def step(g, carry):
        src = pl.multiple_of(g * ROW_TILE, ROW_TILE)
        for u in range(SUBLANES):
            t = g * SUBLANES + u
            row = h2p_ref[_token_rows(src + u), :]
            xs_ref[0, _token_rows(a1_ref[t]), :] = row
            xs_ref[0, _token_rows(a2_ref[t]), :] = row
        return carry

    lax.fori_loop(0, sb // SUBLANES, step, 0)


def _smem_vec(n, index_map):
    return pl.BlockSpec((n,), index_map, memory_space=pltpu.SMEM)


def _moe_dispatch(h2p, a1, a2, sb):
    n_sb = h2p.shape[0] // (sb * ROW_PIECES)
    rs = _sorted_rows(sb)
    vec = _smem_vec(sb, lambda s: (s,))
    return pl.pallas_call(
        functools.partial(_moe_dispatch_kernel, sb),
        grid=(n_sb,),
        in_specs=[pl.BlockSpec((sb * ROW_PIECES, LANES), lambda s: (s, 0)), vec, vec],
        out_specs=pl.BlockSpec((1, rs * ROW_PIECES, LANES), lambda s: (s, 0, 0)),
        out_shape=jax.ShapeDtypeStruct((n_sb, rs * ROW_PIECES, LANES), U32),
        compiler_params=_params(),
        name="moe_dispatch",
    )(h2p, a1, a2)


def _moe_ffn_kernel(off_ref, cnt_ref, xs_ref, wg_ref, wu_ref, wd_ref, ys_ref, z32_sc):
    s = pl.program_id(0)
    e = pl.program_id(1)
    half = D_MODEL // 2

    @pl.when(e == 0)
    def _():
        ys_ref[...] = jnp.zeros_like(ys_ref)

    seg = s * N_EXPERTS + e
    off = off_ref[seg]
    n_win = (cnt_ref[seg] + (FFN_ROWS - 1)) // FFN_ROWS

    def window(j, carry):
        rows = pl.ds(pl.multiple_of((off + j * FFN_ROWS) * ROW_PIECES, ROW_TILE), FFN_ROWS * ROW_PIECES)
        tiled = xs_ref[0, rows, :].reshape(FFN_ROWS // SUBLANES, ROW_PIECES, SUBLANES, LANES)
        lo, hi = _unpack_rows(z32_sc, _from_row_tiled(tiled, FFN_ROWS), FFN_ROWS)
        lo, hi = lo.astype(BF16), hi.astype(BF16)
        g = _dot(lo, wg_ref[0, 0:half, :]) + _dot(hi, wg_ref[0, half:D_MODEL, :])
        u = _dot(lo, wu_ref[0, 0:half, :]) + _dot(hi, wu_ref[0, half:D_MODEL, :])
        y = _dot((_silu(g) * u).astype(BF16), wd_ref[0])
        packed = _to_row_tiled(_pack_rows(z32_sc, y, FFN_ROWS), FFN_ROWS)
        ys_ref[0, rows, :] = packed.reshape(FFN_ROWS * ROW_PIECES, LANES)
        return carry

    lax.fori_loop(0, n_win, window, 0)


def _moe_ffn(xs, seg_off, seg_cnt, wg, wu, wd):
    n_sb, flat_rows, _ = xs.shape
    wspec = lambda shape: pl.BlockSpec((1,) + shape, lambda s, e, *_: (e, 0, 0))
    sorted_spec = pl.BlockSpec((1, flat_rows, LANES), lambda s, e, *_: (s, 0, 0))
    return pl.pallas_call(
        _moe_ffn_kernel,
        grid_spec=pltpu.PrefetchScalarGridSpec(
            num_scalar_prefetch=2,
            grid=(n_sb, N_EXPERTS),
            in_specs=[sorted_spec,
                      wspec((D_MODEL, EXPERT_FF)), wspec((D_MODEL, EXPERT_FF)), wspec((EXPERT_FF, D_MODEL))],
            out_specs=sorted_spec,
            scratch_shapes=[pltpu.VMEM((ROW_PIECES, 2 * FFN_ROWS, LANES), F32)]),
        out_shape=jax.ShapeDtypeStruct(xs.shape, U32),
        compiler_params=_params(2),
        name="moe_ffn",
    )(seg_off, seg_cnt, xs, wg, wu, wd)


def _moe_combine_kernel(n_psb, ys_ref, a1_ref, a2_ref, posw_ref, x1_ref, mod_ref, yp_ref, yo_ref,
                        g1_sc, g2_sc, z32_sc):
    s = pl.program_id(0)

    def step(g, carry):
        dst = pl.multiple_of(g * ROW_TILE, ROW_TILE)
        for u in range(SUBLANES):
            t = g * SUBLANES + u
            g1_sc[_token_rows(dst + u), :] = ys_ref[0, _token_rows(a1_ref[t]), :]
            g2_sc[_token_rows(dst + u), :] = ys_ref[0, _token_rows(a2_ref[t]), :]
        return carry

    lax.fori_loop(0, TOK_TILE // SUBLANES, step, 0)
    tiled = (TOK_TILE // SUBLANES, ROW_PIECES, SUBLANES, LANES)
    lo1, hi1 = _unpack_rows(z32_sc, _from_row_tiled(g1_sc[...].reshape(tiled), TOK_TILE), TOK_TILE)
    lo2, hi2 = _unpack_rows(z32_sc, _from_row_tiled(g2_sc[...].reshape(tiled), TOK_TILE), TOK_TILE)
    w1, w2 = posw_ref[:, 2:3], posw_ref[:, 3:4]
    moe = jnp.concatenate([w1 * lo1 + w2 * lo2, w1 * hi1 + w2 * hi2], axis=1)
    gate2 = _rows_to_tokens(mod_ref[:, 5 * D_MODEL:6 * D_MODEL], D_MODEL)
    y = x1_ref[...] + gate2 * moe

    @pl.when(s < n_psb)
    def _():
        yp_ref[...] = y

    @pl.when(s >= n_psb)
    def _():
        yo_ref[...] = y


def _moe_combine(ys, p1, p2, posw, x1, mod, sb, n_ptiles, n_stiles, prep):
    n_sb, flat_rows, _ = ys.shape
    tps = sb // TOK_TILE
    n_psb = n_ptiles // tps
    pblocks = prep // ROWS_PER_TILE
    tile = lambda s, j: s * tps + j
    vec = _smem_vec(TOK_TILE, lambda s, j: (tile(s, j),))
    return pl.pallas_call(
        functools.partial(_moe_combine_kernel, n_psb),
        grid=(n_sb, tps),
        in_specs=[pl.BlockSpec((1, flat_rows, LANES), lambda s, j: (s, 0, 0)), vec, vec,
                  pl.BlockSpec((TOK_TILE, LANES), lambda s, j: (tile(s, j), 0)),
                  pl.BlockSpec((TOK_TILE, D_MODEL), lambda s, j: (tile(s, j), 0)),
                  pl.BlockSpec((ROWS_PER_TILE, 6 * D_MODEL),
                               lambda s, j: (jnp.maximum(tile(s, j) - n_ptiles + pblocks, 0), 0))],
        out_specs=[pl.BlockSpec((TOK_TILE, D_MODEL), lambda s, j: (jnp.minimum(tile(s, j), n_ptiles - 1), 0)),
                   pl.BlockSpec((TOK_TILE, D_MODEL), lambda s, j: (jnp.maximum(tile(s, j) - n_ptiles, 0), 0))],
        out_shape=[jax.ShapeDtypeStruct((n_ptiles * TOK_TILE, D_MODEL), F32),
                   jax.ShapeDtypeStruct((n_stiles * TOK_TILE, D_MODEL), F32)],
        scratch_shapes=[pltpu.VMEM((TOK_TILE * ROW_PIECES, LANES), U32),
                        pltpu.VMEM((TOK_TILE * ROW_PIECES, LANES), U32),
                        pltpu.VMEM((ROW_PIECES, 2 * TOK_TILE, LANES), F32)],
        compiler_params=_params(2),
        name="moe_combine",
    )(ys, p1, p2, posw, x1, mod)


def _layer(xp, xs, cache_k, cache_v, state, c_prompt, c_sample, norm_mix_g, norm_ffn_g, w_ada, b_ada, w_in,
           q_norm_g, k_norm_g, rel_bias, w_gate_up, b_gate, gla_norm_g, w_out, w_route_group,
           b_route_group, w_route_expert, b_route_expert, w_exp_gate, w_exp_up, w_exp_down):
    batch, seq, _ = xp.shape
    n_seq, dec_seq, _ = xs.shape
    assert batch == 1 and dec_seq == CHUNK and cache_k.shape[1] == BAND_PAST
    assert seq % TOK_TILE == 0 and seq >= BAND_PAST and (n_seq * CHUNK) % TOK_TILE == 0
    n_ptok, n_stok = seq, n_seq * CHUNK
    n_ptiles, n_stiles = n_ptok // TOK_TILE, n_stok // TOK_TILE
    sb = MOE_SUPER_BLOCK if (n_ptok % MOE_SUPER_BLOCK == 0 and n_stok % MOE_SUPER_BLOCK == 0) else TOK_TILE
    prep = ROWS_PER_TILE

    xp2 = xp.reshape(n_ptok, D_MODEL)
    xs2 = xs.reshape(n_stok, D_MODEL)
    c_rows = jnp.concatenate([jnp.broadcast_to(c_prompt, (prep, D_MODEL)), c_sample], axis=0)
    mod = _adaln(c_rows, w_ada, b_ada)

    w_in_p = jnp.pad(w_in, ((0, 0), (0, IN_PAD - w_in.shape[1]))).astype(BF16)
    wgu_p = jnp.pad(w_gate_up, ((0, LANES - GATE_RANK), (0, 0))).astype(BF16)
    head = np.arange(A_WIDTH) // A_HEAD_DIM
    bd = jnp.asarray(head[:, None] == head[None, :], BF16)
    gq = jnp.tile(q_norm_g, A_HEADS).reshape(1, A_WIDTH)
    gk = jnp.tile(k_norm_g, A_HEADS).reshape(1, A_WIDTH)
    q, k, v, kf, vf, gla, la = _inproj(
        xp2, xs2, mod, norm_mix_g.reshape(1, D_MODEL), w_in_p, bd, gq, gk, wgu_p,
        b_gate.reshape(1, B_KWIDTH), n_ptiles, n_stiles, prep)

    first_chunk = n_ptok // CHUNK
    oa_p = _attn_prompt(rel_bias[:, _bias_lanes(3 * Q_ROWS)], q, k, v, n_ptok // Q_ROWS)
    oa_s = _attn_sample(rel_bias[:, _bias_lanes(SAMPLE_KEYS)], q, k, v,
                        cache_k.reshape(n_seq, BAND_PAST, A_WIDTH), cache_v.reshape(n_seq, BAND_PAST, A_WIDTH),
                        first_chunk, n_seq)
    g_gla = gla_norm_g.reshape(1, B_DV)
    ob_p, sfin_p = _gla_prompt(gla, la, g_gla, n_ptok // (GLA_CHUNKS * CHUNK))
    ob_s, sfin_s = _gla_sample(gla, la, g_gla, _state_to_pairs(state), first_chunk, n_seq)

    wr = jnp.pad(jnp.concatenate([w_route_group, w_route_expert], axis=1),
                 ((0, 0), (0, LANES - N_GROUPS - N_EXPERTS)))
    br = jnp.pad(jnp.concatenate([b_route_group, b_route_expert]), (0, LANES - N_GROUPS - N_EXPERTS))
    x1, h2p, meta, cnt = _outproj(oa_p, oa_s, ob_p, ob_s, w_out.astype(BF16), xp2, xs2, mod,
                                  norm_ffn_g.reshape(1, D_MODEL), wr, br.reshape(1, LANES),
                                  n_ptiles, n_stiles, prep, sb)

    posw, seg = _moe_pos(meta, cnt, sb)
    p1, p2 = posw[:, 0].astype(jnp.int32), posw[:, 1].astype(jnp.int32)
    experts = slice(ROUTE_OFF, ROUTE_OFF + N_EXPERTS)
    seg_off = seg[:, 0, experts].astype(jnp.int32).reshape(-1)
    seg_cnt = seg[:, 4, experts].astype(jnp.int32).reshape(-1)
    xs_sorted = _moe_dispatch(h2p.reshape(-1, LANES), p1, p2, sb)
    ys_sorted = _moe_ffn(xs_sorted, seg_off, seg_cnt,
                         w_exp_gate.astype(BF16), w_exp_up.astype(BF16), w_exp_down.astype(BF16))
    yp, ys = _moe_combine(ys_sorted, p1, p2, posw, x1, mod, sb, n_ptiles, n_stiles, prep)

    tail = min(BAND_PAST, seq)
    heads = (A_HEADS, A_HEAD_DIM)
    return (yp.reshape(1, seq, D_MODEL), ys.reshape(n_seq, CHUNK, D_MODEL),
            kf[TOK_TILE - tail:TOK_TILE].reshape((1, tail) + heads),
            vf[TOK_TILE - tail:TOK_TILE].reshape((1, tail) + heads),
            _pairs_to_state(sfin_p)[None],
            kf[TOK_TILE:].reshape((n_seq, CHUNK) + heads),
            vf[TOK_TILE:].reshape((n_seq, CHUNK) + heads),
            _pairs_to_state(sfin_s))


def kernel(x_prompt, x_sample, cache_a_k, cache_a_v, state_gla, c_prompt, c_sample, norm_mix_g, norm_ffn_g,
           w_ada, b_ada, w_in, q_norm_g, k_norm_g, rel_bias, w_gate_up, b_gate, gla_norm_g, w_out,
           w_route_group, b_route_group, w_route_expert, b_route_expert, w_exp_gate, w_exp_up, w_exp_down):
    depth = w_in.shape[0]
    yp, ys = x_prompt, x_sample
    outs = [[] for _ in range(6)]
    for l in range(depth):
        yp, ys, kp, vp, sp, ks, vs, ss = _layer(
            yp, ys, cache_a_k[l], cache_a_v[l], state_gla[l], c_prompt, c_sample, norm_mix_g[l], norm_ffn_g[l],
            w_ada[l], b_ada[l], w_in[l], q_norm_g[l], k_norm_g[l], rel_bias[l], w_gate_up[l], b_gate[l],
            gla_norm_g[l], w_out[l], w_route_group[l], b_route_group[l], w_route_expert[l], b_route_expert[l],
            w_exp_gate[l], w_exp_up[l], w_exp_down[l])
        for lst, val in zip(outs, (kp, vp, sp, ks, vs, ss)):
            lst.append(val)
    return (yp, ys) + tuple(jnp.stack(o) for o in outs)
```

```python
import functools

import numpy as np
import jax
import jax.numpy as jnp
from jax import lax
from jax.experimental import pallas as pl
from jax.experimental.pallas import tpu as pltpu

F32 = jnp.float32
BF16 = jnp.bfloat16
U32 = jnp.uint32

D_MODEL = 1024
CHUNK = 64
LOG_CHUNK = 6
BAND_CHUNKS = 8
BAND_PAST = BAND_CHUNKS * CHUNK
A_WIDTH = 512
A_HEADS = 8
A_HEAD_DIM = 64
MAX_REL = 128
N_REL = CHUNK + MAX_REL
B_WIDTH = 512
B_HEADS = 4
B_DV = 128
B_DK = 64
B_KWIDTH = 256
GATE_RANK = 16
GATE_TAU = 16.0
N_GROUPS = 4
EXPERTS_PER_GROUP = 8
N_EXPERTS = 32
EXPERT_FF = 256
EPS = 1e-6

LANES = 128
IN_MAIN = 3 * A_WIDTH + 2 * B_KWIDTH + 2 * B_WIDTH
IN_PAD = IN_MAIN + LANES
TOK_TILE = 512
ROWS_PER_TILE = TOK_TILE // CHUNK
Q_CHUNKS = 4
Q_ROWS = Q_CHUNKS * CHUNK
ROLL_W = 1024
NEG = -1e30
ROUTE_OFF = N_GROUPS
VMEM_LIMIT = 56 * 1024 * 1024


def _params(n_axes=1):
    return pltpu.CompilerParams(dimension_semantics=("arbitrary",) * n_axes,
                                vmem_limit_bytes=VMEM_LIMIT)


def _split(a):
    hi = a.astype(BF16)
    lo = (a - hi.astype(F32)).astype(BF16)
    return hi, lo


def _dot(a, b):
    return jnp.dot(a, b, preferred_element_type=F32)


def _dot3(a, b):
    ah, al = _split(a)
    bh, bl = _split(b)
    return _dot(ah, bh) + _dot(al, bh) + _dot(ah, bl)


def _dot_nt(a, b):
    return lax.dot_general(a, b, (((1,), (1,)), ((), ())), preferred_element_type=F32)


def _dot_tn(a, b):
    return lax.dot_general(a, b, (((0,), (0,)), ((), ())), preferred_element_type=F32)


def _silu(x):
    return x / (1.0 + jnp.exp(-x))


def _rows_to_tokens(rows, n):
    r = rows.shape[0]
    return jnp.broadcast_to(rows[:, None, :], (r, CHUNK, n)).reshape(r * CHUNK, n)


def _adaln_kernel(c_ref, w_ref, b_ref, o_ref):
    a = _silu(c_ref[...])
    o_ref[...] = _dot3(a, w_ref[...]) + b_ref[...]


def _adaln(c_rows, w_ada, b_ada):
    r = c_rows.shape[0]
    n = w_ada.shape[1]
    tn = 1024
    return pl.pallas_call(
        _adaln_kernel,
        grid=(n // tn,),
        in_specs=[pl.BlockSpec((r, D_MODEL), lambda j: (0, 0)),
                  pl.BlockSpec((D_MODEL, tn), lambda j: (0, j)),
                  pl.BlockSpec((1, tn), lambda j: (0, j))],
        out_specs=pl.BlockSpec((r, tn), lambda j: (0, j)),
        out_shape=jax.ShapeDtypeStruct((r, n), F32),
        compiler_params=_params(),
        name="adaln",
    )(c_rows, w_ada, b_ada.reshape(1, n))


def _head_rms(z, bd_ref, g):
    ms = _dot((z * z).astype(BF16), bd_ref[...]) * (1.0 / A_HEAD_DIM)
    return z * lax.rsqrt(ms + EPS) * g


def _inproj_kernel(n_ptiles, xp_ref, xs_ref, mod_ref, gmix_ref, w_ref, bd_ref, gq_ref, gk_ref,
                   wgu_ref, bg_ref,
                   q_ref, k_ref, v_ref, kf_ref, vf_ref, gla_ref, la_ref):
    i = pl.program_id(0)
    x = jnp.where(i < n_ptiles, xp_ref[...], xs_ref[...])
    ms = jnp.mean(x * x, axis=-1, keepdims=True)
    xn = x * lax.rsqrt(ms + EPS) * gmix_ref[...]
    sh = _rows_to_tokens(mod_ref[:, 0:D_MODEL], D_MODEL)
    sc = _rows_to_tokens(mod_ref[:, D_MODEL:2 * D_MODEL], D_MODEL)
    hb = (xn * (1.0 + sc) + sh).astype(BF16)

    zq = _dot(hb, w_ref[:, 0:A_WIDTH])
    q_ref[...] = (_head_rms(zq, bd_ref, gq_ref[...]) * (A_HEAD_DIM ** -0.5)).astype(BF16)
    zk = _dot(hb, w_ref[:, A_WIDTH:2 * A_WIDTH])
    kn = _head_rms(zk, bd_ref, gk_ref[...])
    k_ref[...] = kn.astype(BF16)
    kf_ref[...] = kn
    zv = _dot(hb, w_ref[:, 2 * A_WIDTH:3 * A_WIDTH])
    v_ref[...] = zv.astype(BF16)
    vf_ref[...] = zv

    o = 3 * A_WIDTH
    zqb = _dot(hb, w_ref[:, o:o + B_KWIDTH]) * (B_DK ** -0.5)
    gla_ref[:, 0:B_KWIDTH] = zqb.astype(BF16)
    for c in range(B_KWIDTH, 2 * B_KWIDTH + 2 * B_WIDTH, 256):
        gla_ref[:, c:c + 256] = _dot(hb, w_ref[:, o + c:o + c + 256]).astype(BF16)

    gr = _dot(hb, w_ref[:, IN_MAIN:IN_PAD])
    logit = _dot(gr.astype(BF16), wgu_ref[...]) + bg_ref[...]
    log_sig = jnp.minimum(logit, 0.0) - jnp.log1p(jnp.exp(-jnp.abs(logit)))
    la_ref[...] = log_sig * (1.0 / GATE_TAU)


def _inproj(xp, xs, mod, gmix, w_in_p, bd, gq, gk, wgu_p, bg, n_ptiles, n_stiles, prep):
    n_tiles = n_ptiles + n_stiles
    t = n_tiles * TOK_TILE
    tail_tiles = 1 + n_stiles
    pblocks = prep // ROWS_PER_TILE
    const = lambda i: (0, 0)
    row = lambda i: (i, 0)
    tail = lambda i: (jnp.maximum(i - (n_ptiles - 1), 0), 0)
    return pl.pallas_call(
        functools.partial(_inproj_kernel, n_ptiles),
        grid=(n_tiles,),
        in_specs=[pl.BlockSpec((TOK_TILE, D_MODEL), lambda i: (jnp.minimum(i, n_ptiles - 1), 0)),
                  pl.BlockSpec((TOK_TILE, D_MODEL), lambda i: (jnp.maximum(i - n_ptiles, 0), 0)),
                  pl.BlockSpec((ROWS_PER_TILE, 6 * D_MODEL),
                               lambda i: (jnp.maximum(i - n_ptiles + pblocks, 0), 0)),
                  pl.BlockSpec((1, D_MODEL), const),
                  pl.BlockSpec((D_MODEL, IN_PAD), const),
                  pl.BlockSpec((A_WIDTH, A_WIDTH), const),
                  pl.BlockSpec((1, A_WIDTH), const),
                  pl.BlockSpec((1, A_WIDTH), const),
                  pl.BlockSpec((LANES, B_KWIDTH), const),
                  pl.BlockSpec((1, B_KWIDTH), const)],
        out_specs=[pl.BlockSpec((TOK_TILE, A_WIDTH), row),
                   pl.BlockSpec((TOK_TILE, A_WIDTH), row),
                   pl.BlockSpec((TOK_TILE, A_WIDTH), row),
                   pl.BlockSpec((TOK_TILE, A_WIDTH), tail),
                   pl.BlockSpec((TOK_TILE, A_WIDTH), tail),
                   pl.BlockSpec((TOK_TILE, 2 * B_KWIDTH + 2 * B_WIDTH), row),
                   pl.BlockSpec((TOK_TILE, B_KWIDTH), row)],
        out_shape=[jax.ShapeDtypeStruct((t, A_WIDTH), BF16),
                   jax.ShapeDtypeStruct((t, A_WIDTH), BF16),
                   jax.ShapeDtypeStruct((t, A_WIDTH), BF16),
                   jax.ShapeDtypeStruct((tail_tiles * TOK_TILE, A_WIDTH), F32),
                   jax.ShapeDtypeStruct((tail_tiles * TOK_TILE, A_WIDTH), F32),
                   jax.ShapeDtypeStruct((t, 2 * B_KWIDTH + 2 * B_WIDTH), BF16),
                   jax.ShapeDtypeStruct((t, B_KWIDTH), F32)],
        compiler_params=_params(),
        name="inproj",
    )(xp, xs, mod, gmix, w_in_p, bd, gq, gk, wgu_p, bg)


def _bias_lanes(n_keys):
    l = np.arange(ROLL_W)
    d = np.where(l < n_keys, BAND_PAST - l, BAND_PAST - l + ROLL_W)
    return np.clip(d, -(CHUNK - 1), MAX_REL) + (CHUNK - 1)


def _build_bias(u_ref, bias_sc, m_rows, n_keys):
    qi = lax.broadcasted_iota(jnp.int32, (m_rows, n_keys), 0) >> LOG_CHUNK
    kc = lax.broadcasted_iota(jnp.int32, (m_rows, n_keys), 1) >> LOG_CHUNK
    band = (kc >= qi) & (kc <= qi + BAND_CHUNKS)
    for h in range(A_HEADS):
        src = jnp.broadcast_to(u_ref[h:h + 1, :], (m_rows, ROLL_W))
        toe = pltpu.roll(src, 0, 1, stride=1, stride_axis=0)
        bias_sc[h] = jnp.where(band, toe[:, 0:n_keys], NEG)


def _attend(q, kcat, vcat, bias_sc, key_mask):
    m_rows = q.shape[0]
    lane = lax.broadcasted_iota(jnp.int32, (m_rows, LANES), 1)
    half = [lane < A_HEAD_DIM, lane >= A_HEAD_DIM]
    outs = []
    for p in range(A_HEADS // 2):
        lanes = slice(p * LANES, (p + 1) * LANES)
        qp, kp, vp = q[:, lanes], kcat[:, lanes], vcat[:, lanes]
        halves = []
        for hh in range(2):
            qm = jnp.where(half[hh], qp, jnp.zeros_like(qp))
            s = _dot_nt(qm, kp) + bias_sc[2 * p + hh]
            if key_mask is not None:
                s = s + key_mask
            e = jnp.exp(s - jnp.max(s, axis=-1, keepdims=True))
            l = jnp.sum(e, axis=-1, keepdims=True)
            halves.append(_dot(e.astype(BF16), vp) / l)
        outs.append(jnp.where(half[0], halves[0], halves[1]))
    return jnp.concatenate(outs, axis=-1)


def _attn_prompt_kernel(u_ref, q_ref, k0_ref, k1_ref, k2_ref, v0_ref, v1_ref, v2_ref, o_ref, bias_sc):
    j = pl.program_id(0)
    n_keys = 3 * Q_ROWS

    @pl.when(j == 0)
    def _():
        _build_bias(u_ref, bias_sc, Q_ROWS, n_keys)

    kcat = jnp.concatenate([k0_ref[...], k1_ref[...], k2_ref[...]], axis=0)
    vcat = jnp.concatenate([v0_ref[...], v1_ref[...], v2_ref[...]], axis=0)
    kw = lax.broadcasted_iota(jnp.int32, (1, n_keys), 1)
    key_mask = jnp.where(kw >= (2 - j) * Q_ROWS, 0.0, NEG)
    o_ref[...] = _attend(q_ref[...], kcat, vcat, bias_sc, key_mask).astype(BF16)


def _attn_prompt(u, q, k, v, n_steps):
    const = lambda j: (0, 0)
    blk = lambda d: pl.BlockSpec((Q_ROWS, A_WIDTH), lambda j, d=d: (jnp.maximum(j - d, 0), 0))
    return pl.pallas_call(
        _attn_prompt_kernel,
        grid=(n_steps,),
        in_specs=[pl.BlockSpec((A_HEADS, ROLL_W), const),
                  blk(0), blk(2), blk(1), blk(0), blk(2), blk(1), blk(0)],
        out_specs=pl.BlockSpec((Q_ROWS, A_WIDTH), lambda j: (j, 0)),
        out_shape=jax.ShapeDtypeStruct((n_steps * Q_ROWS, A_WIDTH), BF16),
        scratch_shapes=[pltpu.VMEM((A_HEADS, Q_ROWS, 3 * Q_ROWS), F32)],
        compiler_params=_params(),
        name="attn_prompt",
    )(u, q, k, k, k, v, v, v)


SAMPLE_KEYS = BAND_PAST + 2 * CHUNK


def _attn_sample_kernel(u_ref, q_ref, kn_ref, vn_ref, kc_ref, vc_ref, o_ref, bias_sc):
    @pl.when(pl.program_id(0) == 0)
    def _():
        _build_bias(u_ref, bias_sc, CHUNK, SAMPLE_KEYS)

    pad = jnp.zeros((CHUNK, A_WIDTH), BF16)
    kcat = jnp.concatenate([kc_ref[0].astype(BF16), kn_ref[...], pad], axis=0)
    vcat = jnp.concatenate([vc_ref[0].astype(BF16), vn_ref[...], pad], axis=0)
    o_ref[...] = _attend(q_ref[...], kcat, vcat, bias_sc, None).astype(BF16)


def _attn_sample(u, q, k, v, kc, vc, first_chunk, n_seq):
    new = pl.BlockSpec((CHUNK, A_WIDTH), lambda b: (first_chunk + b, 0))
    cache = pl.BlockSpec((1, BAND_PAST, A_WIDTH), lambda b: (b, 0, 0))
    return pl.pallas_call(
        _attn_sample_kernel,
        grid=(n_seq,),
        in_specs=[pl.BlockSpec((A_HEADS, ROLL_W), lambda b: (0, 0)), new, new, new, cache, cache],
        out_specs=pl.BlockSpec((CHUNK, A_WIDTH), lambda b: (b, 0)),
        out_shape=jax.ShapeDtypeStruct((n_seq * CHUNK, A_WIDTH), BF16),
        scratch_shapes=[pltpu.VMEM((A_HEADS, CHUNK, SAMPLE_KEYS), F32)],
        compiler_params=_params(),
        name="attn_sample",
    )(u, q, k, v, kc, vc)


GLA_CHUNKS = 4


def _gla_block(n_chunks, gla_ref, la_ref, ltri_ref, g_ref, st_sc, o_ref):
    rows = n_chunks * CHUNK
    la = la_ref[...]
    la_hi, la_lo = _split(la)
    b = _dot(ltri_ref[...], la_hi) + _dot(ltri_ref[...], la_lo)
    b3 = b.reshape(n_chunks, CHUNK, B_KWIDTH)
    b_mid = b3[:, CHUNK // 2 - 1:CHUNK // 2, :]
    b_last = b3[:, CHUNK - 1:CHUNK, :]
    q = gla_ref[:, 0:B_KWIDTH].astype(F32).reshape(n_chunks, CHUNK, B_KWIDTH)
    k = gla_ref[:, B_KWIDTH:2 * B_KWIDTH].astype(F32).reshape(n_chunks, CHUNK, B_KWIDTH)
    q_start = (q * jnp.exp(b3)).reshape(rows, B_KWIDTH).astype(BF16)
    q_mid = (q * jnp.exp(b3 - b_mid)).reshape(rows, B_KWIDTH).astype(BF16)
    k_mid = (k * jnp.exp(b_mid - b3)).reshape(rows, B_KWIDTH).astype(BF16)
    k_end = (k * jnp.exp(b_last - b3)).reshape(rows, B_KWIDTH).astype(BF16)
    dec = jnp.exp(b_last)

    ti = lax.broadcasted_iota(jnp.int32, (rows, rows), 0)
    si = lax.broadcasted_iota(jnp.int32, (rows, rows), 1)
    causal = (si <= ti) & ((si >> LOG_CHUNK) == (ti >> LOG_CHUNK))
    lane_r = lax.broadcasted_iota(jnp.int32, (rows, LANES), 1)
    half_r = [lane_r < B_DK, lane_r >= B_DK]
    lane_c = lax.broadcasted_iota(jnp.int32, (CHUNK, LANES), 1)
    half_c = [lane_c < B_DK, lane_c >= B_DK]
    half_s = lax.broadcasted_iota(jnp.int32, (B_DV, LANES), 1) < B_DK

    for p in range(B_HEADS // 2):
        lanes = slice(p * LANES, (p + 1) * LANES)
        qs_p, qm_p, km_p, ke_p = q_start[:, lanes], q_mid[:, lanes], k_mid[:, lanes], k_end[:, lanes]
        vs = [gla_ref[:, 2 * B_KWIDTH + (2 * p + hh) * B_DV:2 * B_KWIDTH + (2 * p + hh + 1) * B_DV]
              for hh in range(2)]
        intra = []
        for hh in range(2):
            qm = jnp.where(half_r[hh], qm_p, jnp.zeros_like(qm_p))
            sc = jnp.where(causal, _dot_nt(qm, km_p), 0.0)
            intra.append(_dot(sc.astype(BF16), vs[hh]))
        inter = [[], []]
        st = st_sc[p]
        for c in range(n_chunks):
            cr = slice(c * CHUNK, (c + 1) * CHUNK)
            st_b = st.astype(BF16)
            for hh in range(2):
                qc = jnp.where(half_c[hh], qs_p[cr], jnp.zeros((CHUNK, LANES), BF16))
                inter[hh].append(_dot_nt(qc, st_b))
            upd = jnp.where(half_s, _dot_tn(vs[0][cr], ke_p[cr]), _dot_tn(vs[1][cr], ke_p[cr]))
            st = st * dec[c, :, lanes] + upd
        st_sc[p] = st
        for hh in range(2):
            h = 2 * p + hh
            o = intra[hh] + jnp.concatenate(inter[hh], axis=0)
            ms = jnp.mean(o * o, axis=-1, keepdims=True)
            on = o * lax.rsqrt(ms + EPS) * g_ref[...]
            r = gla_ref[:, 2 * B_KWIDTH + B_WIDTH + h * B_DV:2 * B_KWIDTH + B_WIDTH + (h + 1) * B_DV]
            o_ref[:, h * B_DV:(h + 1) * B_DV] = (on * _silu(r.astype(F32))).astype(BF16)


def _gla_prompt_kernel(gla_ref, la_ref, ltri_ref, g_ref, o_ref, sfin_ref, st_sc):
    @pl.when(pl.program_id(0) == 0)
    def _():
        st_sc[...] = jnp.zeros_like(st_sc)

    _gla_block(GLA_CHUNKS, gla_ref, la_ref, ltri_ref, g_ref, st_sc, o_ref)
    sfin_ref[...] = st_sc[...]


def _gla_sample_kernel(gla_ref, la_ref, ltri_ref, g_ref, s0_ref, o_ref, sfin_ref, st_sc):
    st_sc[...] = s0_ref[0]
    _gla_block(1, gla_ref, la_ref, ltri_ref, g_ref, st_sc, o_ref)
    sfin_ref[0] = st_sc[...]


def _ltri(n_chunks):
    r = np.arange(n_chunks * CHUNK)
    m = (r[None, :] <= r[:, None]) & (r[None, :] // CHUNK == r[:, None] // CHUNK)
    return jnp.asarray(m, BF16)


_GLA_W = 2 * B_KWIDTH + 2 * B_WIDTH
_ST_SHAPE = (B_HEADS // 2, B_DV, LANES)


def _gla_prompt(gla, la, g, n_steps):
    rows = GLA_CHUNKS * CHUNK
    const = lambda j: (0, 0)
    return pl.pallas_call(
        _gla_prompt_kernel,
        grid=(n_steps,),
        in_specs=[pl.BlockSpec((rows, _GLA_W), lambda j: (j, 0)),
                  pl.BlockSpec((rows, B_KWIDTH), lambda j: (j, 0)),
                  pl.BlockSpec((rows, rows), const),
                  pl.BlockSpec((1, B_DV), const)],
        out_specs=[pl.BlockSpec((rows, B_WIDTH), lambda j: (j, 0)),
                   pl.BlockSpec(_ST_SHAPE, lambda j: (0, 0, 0))],
        out_shape=[jax.ShapeDtypeStruct((n_steps * rows, B_WIDTH), BF16),
                   jax.ShapeDtypeStruct(_ST_SHAPE, F32)],
        scratch_shapes=[pltpu.VMEM(_ST_SHAPE, F32)],
        compiler_params=_params(),
        name="gla_prompt",
    )(gla, la, _ltri(GLA_CHUNKS), g)


def _gla_sample(gla, la, g, s0, first_chunk, n_seq):
    const = lambda b: (0, 0)
    st_spec = pl.BlockSpec((1,) + _ST_SHAPE, lambda b: (b, 0, 0, 0))
    return pl.pallas_call(
        _gla_sample_kernel,
        grid=(n_seq,),
        in_specs=[pl.BlockSpec((CHUNK, _GLA_W), lambda b: (first_chunk + b, 0)),
                  pl.BlockSpec((CHUNK, B_KWIDTH), lambda b: (first_chunk + b, 0)),
                  pl.BlockSpec((CHUNK, CHUNK), const),
                  pl.BlockSpec((1, B_DV), const),
                  st_spec],
        out_specs=[pl.BlockSpec((CHUNK, B_WIDTH), lambda b: (b, 0)), st_spec],
        out_shape=[jax.ShapeDtypeStruct((n_seq * CHUNK, B_WIDTH), BF16),
                   jax.ShapeDtypeStruct((n_seq,) + _ST_SHAPE, F32)],
        scratch_shapes=[pltpu.VMEM(_ST_SHAPE, F32)],
        compiler_params=_params(),
        name="gla_sample",
    )(gla, la, _ltri(1), g, s0)


def _state_to_pairs(s):
    lead = s.shape[:-3]
    s = s.reshape(lead + (B_HEADS // 2, 2, B_DK, B_DV))
    s = jnp.moveaxis(s, -1, -3)
    return s.reshape(lead + (B_HEADS // 2, B_DV, 2 * B_DK))


def _pairs_to_state(s):
    lead = s.shape[:-3]
    s = s.reshape(lead + (B_HEADS // 2, B_DV, 2, B_DK))
    s = jnp.moveaxis(s, -3, -1)
    return s.reshape(lead + (B_HEADS, B_DK, B_DV))


def _route(logits):
    lane = lax.broadcasted_iota(jnp.int32, logits.shape, 1)
    lane_f = lane.astype(F32)
    big = float(LANES)
    gmask = lane < N_GROUPS
    gl = jnp.where(gmask, logits, NEG)
    gmax = jnp.max(gl, axis=-1, keepdims=True)
    gsel = jnp.min(jnp.where(gl == gmax, lane_f, big), axis=-1, keepdims=True)
    gsum = jnp.sum(jnp.where(gmask, jnp.exp(gl - gmax), 0.0), axis=-1, keepdims=True)
    g_w = 1.0 / gsum
    e_lo = ROUTE_OFF + gsel * EXPERTS_PER_GROUP
    emask = (lane_f >= e_lo) & (lane_f < e_lo + EXPERTS_PER_GROUP)
    el = jnp.where(emask, logits, NEG)
    v1 = jnp.max(el, axis=-1, keepdims=True)
    i1 = jnp.min(jnp.where(el == v1, lane_f, big), axis=-1, keepdims=True)
    el2 = jnp.where(lane_f == i1, NEG, el)
    v2 = jnp.max(el2, axis=-1, keepdims=True)
    i2 = jnp.min(jnp.where(el2 == v2, lane_f, big), axis=-1, keepdims=True)
    t = jnp.exp(v2 - v1)
    w1 = g_w / (1.0 + t)
    w2 = g_w * t / (1.0 + t)
    return lane_f, i1, i2, w1, w2


ROW_PIECES = D_MODEL // 2 // LANES
SUBLANES = 8
ROW_TILE = ROW_PIECES * SUBLANES


def _pack_rows(z32_sc, x, rows):
    half = D_MODEL // 2
    out = []
    for c in range(ROW_PIECES):
        z32_sc[c, pl.ds(0, rows, stride=2), :] = x[:, c * LANES:(c + 1) * LANES]
        z32_sc[c, pl.ds(1, rows, stride=2), :] = x[:, half + c * LANES:half + (c + 1) * LANES]
        out.append(pltpu.bitcast(z32_sc[c].astype(BF16), U32))
    return out


def _unpack_rows(z32_sc, pieces, rows):
    lo, hi = [], []
    for c in range(ROW_PIECES):
        z32_sc[c] = pltpu.bitcast(pieces[c], BF16).astype(F32)
        lo.append(z32_sc[c, pl.ds(0, rows, stride=2), :])
        hi.append(z32_sc[c, pl.ds(1, rows, stride=2), :])
    return jnp.concatenate(lo, axis=1), jnp.concatenate(hi, axis=1)


def _to_row_tiled(pieces, rows):
    return jnp.stack([p.reshape(rows // SUBLANES, SUBLANES, LANES) for p in pieces], axis=1)


def _from_row_tiled(tiled, rows):
    return [tiled[:, c].reshape(rows, LANES) for c in range(ROW_PIECES)]


def _outproj_kernel(n_ptiles, tiles_per_sb, oap_ref, oas_ref, obp_ref, obs_ref, wo_ref, xp_ref, xs_ref, mod_ref,
                    gffn_ref, wr_ref, br_ref, ltri_ref, x1_ref, h2p_ref, meta_ref, cnt_ref, z32_sc, cnt_sc):
    i = pl.program_id(0)
    is_prompt = i < n_ptiles
    x = jnp.where(is_prompt, xp_ref[...], xs_ref[...])
    oa = jnp.where(is_prompt, oap_ref[...], oas_ref[...])
    ob = jnp.where(is_prompt, obp_ref[...], obs_ref[...])
    mix = _dot(oa, wo_ref[0:A_WIDTH, :]) + _dot(ob, wo_ref[A_WIDTH:D_MODEL, :])
    gate1 = _rows_to_tokens(mod_ref[:, 2 * D_MODEL:3 * D_MODEL], D_MODEL)
    x1 = x + gate1 * mix
    x1_ref[...] = x1
    ms = jnp.mean(x1 * x1, axis=-1, keepdims=True)
    xn = x1 * lax.rsqrt(ms + EPS) * gffn_ref[...]
    sh = _rows_to_tokens(mod_ref[:, 3 * D_MODEL:4 * D_MODEL], D_MODEL)
    sc = _rows_to_tokens(mod_ref[:, 4 * D_MODEL:5 * D_MODEL], D_MODEL)
    h2 = xn * (1.0 + sc) + sh
    h2p_ref[...] = _to_row_tiled(_pack_rows(z32_sc, h2, TOK_TILE), TOK_TILE)

    lane_f, i1, i2, w1, w2 = _route(_dot3(h2, wr_ref[...]) + br_ref[...])

    @pl.when(lax.rem(i, tiles_per_sb) == 0)
    def _():
        cnt_sc[...] = jnp.zeros_like(cnt_sc)

    sel = jnp.where((lane_f == i1) | (lane_f == i2), 1.0, 0.0).astype(BF16)
    before = _dot(ltri_ref[...], sel) + cnt_sc[0:1, :]
    rank1 = jnp.sum(jnp.where(lane_f == i1, before, 0.0), axis=-1, keepdims=True)
    rank2 = jnp.sum(jnp.where(lane_f == i2, before, 0.0), axis=-1, keepdims=True)
    cnt = cnt_sc[...] + _dot(jnp.ones((8, TOK_TILE), BF16), sel)
    cnt_sc[...] = cnt
    cnt_ref[0] = cnt
    cols = (i1, i2, rank1, rank2, w1, w2)
    meta = jnp.zeros_like(lane_f)
    for c, col in enumerate(cols):
        meta = jnp.where(lane_f == float(c), col, meta)
    meta_ref[...] = meta


def _outproj(oa_p, oa_s, ob_p, ob_s, w_out, xp, xs, mod, gffn, wr, br, n_ptiles, n_stiles, prep, sb):
    n_tiles = n_ptiles + n_stiles
    t = n_tiles * TOK_TILE
    pblocks = prep // ROWS_PER_TILE
    tiles_per_sb = sb // TOK_TILE
    const = lambda i: (0, 0)
    row = lambda i: (i, 0)
    prow = lambda i: (jnp.minimum(i, n_ptiles - 1), 0)
    srow = lambda i: (jnp.maximum(i - n_ptiles, 0), 0)
    r = np.arange(TOK_TILE)
    ltri = jnp.asarray(r[None, :] < r[:, None], BF16)
    return pl.pallas_call(
        functools.partial(_outproj_kernel, n_ptiles, tiles_per_sb),
        grid=(n_tiles,),
        in_specs=[pl.BlockSpec((TOK_TILE, A_WIDTH), prow),
                  pl.BlockSpec((TOK_TILE, A_WIDTH), srow),
                  pl.BlockSpec((TOK_TILE, B_WIDTH), prow),
                  pl.BlockSpec((TOK_TILE, B_WIDTH), srow),
                  pl.BlockSpec((D_MODEL, D_MODEL), const),
                  pl.BlockSpec((TOK_TILE, D_MODEL), prow),
                  pl.BlockSpec((TOK_TILE, D_MODEL), srow),
                  pl.BlockSpec((ROWS_PER_TILE, 6 * D_MODEL),
                               lambda i: (jnp.maximum(i - n_ptiles + pblocks, 0), 0)),
                  pl.BlockSpec((1, D_MODEL), const),
                  pl.BlockSpec((D_MODEL, LANES), const),
                  pl.BlockSpec((1, LANES), const),
                  pl.BlockSpec((TOK_TILE, TOK_TILE), const)],
        out_specs=[pl.BlockSpec((TOK_TILE, D_MODEL), row),
                   pl.BlockSpec((TOK_TILE // SUBLANES, ROW_PIECES, SUBLANES, LANES), lambda i: (i, 0, 0, 0)),
                   pl.BlockSpec((TOK_TILE, LANES), row),
                   pl.BlockSpec((1, 8, LANES), lambda i: (i // tiles_per_sb, 0, 0))],
        out_shape=[jax.ShapeDtypeStruct((t, D_MODEL), F32),
                   jax.ShapeDtypeStruct((t // SUBLANES, ROW_PIECES, SUBLANES, LANES), U32),
                   jax.ShapeDtypeStruct((t, LANES), F32),
                   jax.ShapeDtypeStruct((t // sb, 8, LANES), F32)],
        scratch_shapes=[pltpu.VMEM((D_MODEL // 2 // LANES, 2 * TOK_TILE, LANES), F32),
                        pltpu.VMEM((8, LANES), F32)],
        compiler_params=_params(),
        name="outproj",
    )(oa_p, oa_s, ob_p, ob_s, w_out, xp, xs, mod, gffn, wr, br, ltri)


MOE_SUPER_BLOCK = 2048
SEG_ALIGN = SUBLANES
CHUNK_ROWS = SEG_ALIGN * ROW_PIECES
FFN_ROWS = 256
PLAN_ROWS = LANES


def _local_rows(sb):
    return 2 * sb + N_EXPERTS * SEG_ALIGN


def _sorted_tiles(n_tokens, sb):
    rows = 2 * n_tokens + (n_tokens // sb) * N_EXPERTS * SEG_ALIGN + N_EXPERTS * FFN_ROWS
    return -(-rows // FFN_ROWS)


def _moe_plan_kernel(n_blocks, total_chunks, meta_ref, cnt_ref, ustrict_ref, lstrict_ref,
                     posw_ref, tab_ref, tile_ref):
    b = pl.program_id(0)
    per_tile = FFN_ROWS // SEG_ALIGN
    cnt = cnt_ref[...]
    chunks = jnp.floor((cnt + (SEG_ALIGN - 1)) * (1.0 / SEG_ALIGN))
    chunks_b = chunks.astype(BF16)
    loc = _dot(chunks_b, ustrict_ref[...])
    before = _dot(lstrict_ref[...], chunks_b)
    tot = _dot(jnp.ones((PLAN_ROWS, PLAN_ROWS), BF16), chunks_b)
    tiles = jnp.floor((tot + (per_tile - 1)) * (1.0 / per_tile))
    tile_off = _dot(tiles.astype(BF16), ustrict_ref[...])
    n_tiles = jnp.sum(tiles[0:1], axis=-1, keepdims=True)
    lane1 = lax.broadcasted_iota(jnp.int32, (PLAN_ROWS, LANES), 1)
    tail = lane1 == ROUTE_OFF + N_EXPERTS
    pad_off = jnp.where(tail, n_tiles * per_tile, tile_off * per_tile + tot)
    pad_n = jnp.where(tail, total_chunks - n_tiles * per_tile, tiles * per_tile - tot)
    blk_chunks = jnp.sum(chunks, axis=-1, keepdims=True)
    row = lax.broadcasted_iota(jnp.int32, (PLAN_ROWS, LANES), 0)
    misc = jnp.where(row == 0, pad_off, jnp.where(row == 1, pad_n, jnp.where(row == 2, n_tiles, 0.0)))
    tab_ref[0] = loc
    tab_ref[1] = chunks
    tab_ref[2] = tile_off * per_tile + before
    tab_ref[3] = misc
    tab_ref[4] = jnp.broadcast_to(blk_chunks, (PLAN_ROWS, LANES))
    t_idx = lax.broadcasted_iota(jnp.int32, tile_ref.shape, 0).astype(F32)
    lane_t = lax.broadcasted_iota(jnp.int32, tile_ref.shape, 1)
    is_expert = (lane_t >= ROUTE_OFF) & (lane_t < ROUTE_OFF + N_EXPERTS)
    ends = (tile_off + tiles)[0:1, :]
    owner = jnp.sum(jnp.where(is_expert & (ends <= t_idx), 1.0, 0.0), axis=-1, keepdims=True)
    tile_ref[...] = jnp.broadcast_to(jnp.minimum(owner, N_EXPERTS - 1.0), tile_ref.shape)

    own = jnp.floor((cnt_ref[pl.ds(b, 1), :] + (SEG_ALIGN - 1)) * (1.0 / SEG_ALIGN))
    own_off = _dot(jnp.broadcast_to(own, (SUBLANES, LANES)).astype(BF16), ustrict_ref[...]) * SEG_ALIGN
    meta = meta_ref[...]
    lane_f = lax.broadcasted_iota(jnp.int32, meta.shape, 1).astype(F32)
    off_row = own_off[0:1, :]
    pos = []
    for k in range(2):
        e_lane = meta[:, k:k + 1]
        base = jnp.sum(jnp.where(lane_f == e_lane, off_row, 0.0), axis=-1, keepdims=True)
        p = base + meta[:, 2 + k:3 + k]
        tile = jnp.floor(p * (1.0 / SUBLANES))
        pos.append(tile * (ROW_TILE - SUBLANES) + p)
    out = jnp.zeros_like(meta)
    for c, col in enumerate((pos[0], pos[1], meta[:, 4:5], meta[:, 5:6])):
        out = jnp.where(lane_f == float(c), col, out)
    posw_ref[...] = out


def _moe_plan(meta, cnt, sb):
    n_blocks = meta.shape[0] // sb
    assert n_blocks <= PLAN_ROWS and sb // SEG_ALIGN <= 256
    n_tiles = _sorted_tiles(meta.shape[0], sb)
    tile_rows = -(-n_tiles // SUBLANES) * SUBLANES
    r = np.arange(LANES)
    ustrict = jnp.asarray(r[:, None] < r[None, :], BF16)
    lstrict = jnp.asarray(r[None, :] < r[:, None], BF16)
    cnt_all = jnp.pad(cnt[:, 0, :], ((0, PLAN_ROWS - n_blocks), (0, 0)))
    const = lambda s: (0, 0)
    posw, tab, tile_owner = pl.pallas_call(
        functools.partial(_moe_plan_kernel, n_blocks, float(n_tiles * (FFN_ROWS // SEG_ALIGN))),
        grid=(n_blocks,),
        in_specs=[pl.BlockSpec((sb, LANES), lambda s: (s, 0)),
                  pl.BlockSpec((PLAN_ROWS, LANES), const),
                  pl.BlockSpec((LANES, LANES), const),
                  pl.BlockSpec((PLAN_ROWS, PLAN_ROWS), const)],
        out_specs=[pl.BlockSpec((sb, LANES), lambda s: (s, 0)),
                   pl.BlockSpec((5, PLAN_ROWS, LANES), lambda s: (0, 0, 0)),
                   pl.BlockSpec((tile_rows, LANES), const)],
        out_shape=[jax.ShapeDtypeStruct(meta.shape, F32),
                   jax.ShapeDtypeStruct((5, PLAN_ROWS, LANES), F32),
                   jax.ShapeDtypeStruct((tile_rows, LANES), F32)],
        compiler_params=_params(),
        name="moe_plan",
    )(meta, cnt_all, ustrict, lstrict)
    experts = slice(ROUTE_OFF, ROUTE_OFF + N_EXPERTS)
    to_i32 = lambda x: x.astype(jnp.int32).reshape(-1)
    plan = dict(
        loc=to_i32(tab[0, :n_blocks, experts]), n=to_i32(tab[1, :n_blocks, experts]),
        dst=to_i32(tab[2, :n_blocks, experts]),
        pad_off=to_i32(tab[3, 0, ROUTE_OFF:ROUTE_OFF + N_EXPERTS + 1]),
        pad_n=to_i32(tab[3, 1, ROUTE_OFF:ROUTE_OFF + N_EXPERTS + 1]),
        n_tiles=to_i32(tab[3, 2, 0:1]), blk=to_i32(tab[4, :n_blocks, 0]),
        owner=to_i32(tile_owner[:n_tiles, 0]))
    return posw, plan


def _token_rows(start):
    return pl.ds(start, ROW_PIECES, stride=SUBLANES)


def _chunk(ref, index):
    start = index * CHUNK_ROWS
    if not isinstance(index, int):
        start = pl.multiple_of(start, CHUNK_ROWS)
    return ref.at[pl.ds(start, CHUNK_ROWS)]


def _for_each_chunk(block, loc_ref, n_ref, dst_ref, start_copy):
    def per_expert(e, carry):
        seg = block * N_EXPERTS + e

        def per_chunk(c, inner):
            start_copy(loc_ref[seg] + c, dst_ref[seg] + c)
            return inner

        return lax.fori_loop(0, n_ref[seg], per_chunk, carry)

    lax.fori_loop(0, N_EXPERTS, per_expert, 0)


def _wait_chunks(n, copy):
    def one(_, carry):
        copy.wait()
        return carry

    lax.fori_loop(0, n, one, 0)


def _moe_dispatch_kernel(sb, loc_ref, n_ref, dst_ref, padoff_ref, padn_ref, blk_ref,
                         h2p_ref, a1_ref, a2_ref, xs_hbm, local_sc, zero_sc, sem):
    b = pl.program_id(0)
    local_sc[...] = jnp.zeros_like(local_sc)

    def step(g, carry):
        src = pl.multiple_of(g * ROW_TILE, ROW_TILE)
        for u in range(SUBLANES):
            t = g * SUBLANES + u
            row = h2p_ref[_token_rows(src + u), :]
            local_sc[_token_rows(a1_ref[t]), :] = row
            local_sc[_token_rows(a2_ref[t]), :] = row
        return carry

    lax.fori_loop(0, sb // SUBLANES, step, 0)

    def copy(src, dst):
        return pltpu.make_async_copy(_chunk(local_sc, src), _chunk(xs_hbm, dst), sem)

    _for_each_chunk(b, loc_ref, n_ref, dst_ref, lambda src, dst: copy(src, dst).start())

    @pl.when(b == 0)
    def _():
        zero_sc[...] = jnp.zeros_like(zero_sc)

        def per_pad(e, total):
            def per_chunk(c, inner):
                pltpu.make_async_copy(zero_sc, _chunk(xs_hbm, padoff_ref[e] + c), sem).start()
                return inner

            lax.fori_loop(0, padn_ref[e], per_chunk, 0)
            return total + padn_ref[e]

        _wait_chunks(lax.fori_loop(0, N_EXPERTS + 1, per_pad, 0), copy(0, 0))

    _wait_chunks(blk_ref[b], copy(0, 0))


def _smem_vec(n, index_map):
    return pl.BlockSpec((n,), index_map, memory_space=pltpu.SMEM)


def _moe_dispatch(h2p, a1, a2, plan, sb, n_tiles):
    n_blocks = h2p.shape[0] // (sb * ROW_PIECES)
    vec = _smem_vec(sb, lambda s, *_: (s,))
    return pl.pallas_call(
        functools.partial(_moe_dispatch_kernel, sb),
        grid_spec=pltpu.PrefetchScalarGridSpec(
            num_scalar_prefetch=6,
            grid=(n_blocks,),
            in_specs=[pl.BlockSpec((sb * ROW_PIECES, LANES), lambda s, *_: (s, 0)), vec, vec],
            out_specs=pl.BlockSpec(memory_space=pl.ANY),
            scratch_shapes=[pltpu.VMEM((_local_rows(sb) * ROW_PIECES, LANES), U32),
                            pltpu.VMEM((CHUNK_ROWS, LANES), U32),
                            pltpu.SemaphoreType.DMA(())]),
        out_shape=jax.ShapeDtypeStruct((n_tiles * FFN_ROWS * ROW_PIECES, LANES), U32),
        compiler_params=_params(),
        name="moe_dispatch",
    )(plan["loc"], plan["n"], plan["dst"], plan["pad_off"], plan["pad_n"], plan["blk"], h2p, a1, a2)


def _moe_ffn_kernel(owner_ref, ntiles_ref, xs_ref, wg_ref, wu_ref, wd_ref, ys_ref,
                    wg_sc, wu_sc, wd_sc, z32_sc):
    i = pl.program_id(0)
    half = D_MODEL // 2
    tiled = (FFN_ROWS // SUBLANES, ROW_PIECES, SUBLANES, LANES)
    used = i < ntiles_ref[0]

    @pl.when(used & ((i == 0) | (owner_ref[i] != owner_ref[jnp.maximum(i - 1, 0)])))
    def _():
        wg_sc[...] = wg_ref[0].astype(BF16)
        wu_sc[...] = wu_ref[0].astype(BF16)
        wd_sc[...] = wd_ref[0].astype(BF16)

    @pl.when(used)
    def _():
        lo, hi = _unpack_rows(z32_sc, _from_row_tiled(xs_ref[...].reshape(tiled), FFN_ROWS), FFN_ROWS)
        lo, hi = lo.astype(BF16), hi.astype(BF16)
        g = _dot(lo, wg_sc[0:half, :]) + _dot(hi, wg_sc[half:D_MODEL, :])
        u = _dot(lo, wu_sc[0:half, :]) + _dot(hi, wu_sc[half:D_MODEL, :])
        y = _dot((_silu(g) * u).astype(BF16), wd_sc[...])
        packed = _to_row_tiled(_pack_rows(z32_sc, y, FFN_ROWS), FFN_ROWS)
        ys_ref[...] = packed.reshape(FFN_ROWS * ROW_PIECES, LANES)

    @pl.when(jnp.logical_not(used))
    def _():
        ys_ref[...] = jnp.zeros_like(ys_ref)


def _moe_ffn(xs, plan, wg, wu, wd):
    flat = FFN_ROWS * ROW_PIECES
    n_tiles = xs.shape[0] // flat
    last_used = lambda i, owner, nt: jnp.minimum(i, nt[0] - 1)
    wspec = lambda shape: pl.BlockSpec((1,) + shape, lambda i, owner, nt: (owner[last_used(i, owner, nt)], 0, 0))
    return pl.pallas_call(
        _moe_ffn_kernel,
        grid_spec=pltpu.PrefetchScalarGridSpec(
            num_scalar_prefetch=2,
            grid=(n_tiles,),
            in_specs=[pl.BlockSpec((flat, LANES), lambda i, owner, nt: (last_used(i, owner, nt), 0)),
                      wspec((D_MODEL, EXPERT_FF)), wspec((D_MODEL, EXPERT_FF)), wspec((EXPERT_FF, D_MODEL))],
            out_specs=pl.BlockSpec((flat, LANES), lambda i, owner, nt: (i, 0)),
            scratch_shapes=[pltpu.VMEM((D_MODEL, EXPERT_FF), BF16),
                            pltpu.VMEM((D_MODEL, EXPERT_FF), BF16),
                            pltpu.VMEM((EXPERT_FF, D_MODEL), BF16),
                            pltpu.VMEM((ROW_PIECES, 2 * FFN_ROWS, LANES), F32)]),
        out_shape=jax.ShapeDtypeStruct(xs.shape, U32),
        compiler_params=_params(),
        name="moe_ffn",
    )(plan["owner"], plan["n_tiles"], xs, wg, wu, wd)


def _moe_combine_kernel(n_psb, loc_ref, n_ref, dst_ref, blk_ref,
                        ys_hbm, a1_ref, a2_ref, posw_ref, x1_ref, mod_ref, yp_ref, yo_ref,
                        local_sc, g1_sc, g2_sc, z32_sc, sem):
    s = pl.program_id(0)

    @pl.when(pl.program_id(1) == 0)
    def _():
        def copy(loc, dst):
            return pltpu.make_async_copy(_chunk(ys_hbm, dst), _chunk(local_sc, loc), sem)

        _for_each_chunk(s, loc_ref, n_ref, dst_ref, lambda loc, dst: copy(loc, dst).start())
        _wait_chunks(blk_ref[s], copy(0, 0))

    def step(g, carry):
        dst = pl.multiple_of(g * ROW_TILE, ROW_TILE)
        for u in range(SUBLANES):
            t = g * SUBLANES + u
            g1_sc[_token_rows(dst + u), :] = local_sc[_token_rows(a1_ref[t]), :]
            g2_sc[_token_rows(dst + u), :] = local_sc[_token_rows(a2_ref[t]), :]
        return carry

    lax.fori_loop(0, TOK_TILE // SUBLANES, step, 0)
    tiled = (TOK_TILE // SUBLANES, ROW_PIECES, SUBLANES, LANES)
    lo1, hi1 = _unpack_rows(z32_sc, _from_row_tiled(g1_sc[...].reshape(tiled), TOK_TILE), TOK_TILE)
    lo2, hi2 = _unpack_rows(z32_sc, _from_row_tiled(g2_sc[...].reshape(tiled), TOK_TILE), TOK_TILE)
    w1, w2 = posw_ref[:, 2:3], posw_ref[:, 3:4]
    moe = jnp.concatenate([w1 * lo1 + w2 * lo2, w1 * hi1 + w2 * hi2], axis=1)
    gate2 = _rows_to_tokens(mod_ref[:, 5 * D_MODEL:6 * D_MODEL], D_MODEL)
    y = x1_ref[...] + gate2 * moe

    @pl.when(s < n_psb)
    def _():
        yp_ref[...] = y

    @pl.when(s >= n_psb)
    def _():
        yo_ref[...] = y


def _moe_combine(ys, a1, a2, posw, plan, x1, mod, sb, n_ptiles, n_stiles, prep):
    tps = sb // TOK_TILE
    n_blocks = (n_ptiles + n_stiles) // tps
    n_psb = n_ptiles // tps
    pblocks = prep // ROWS_PER_TILE
    tile = lambda s, j: s * tps + j
    vec = _smem_vec(TOK_TILE, lambda s, j, *_: (tile(s, j),))
    return pl.pallas_call(
        functools.partial(_moe_combine_kernel, n_psb),
        grid_spec=pltpu.PrefetchScalarGridSpec(
            num_scalar_prefetch=4,
            grid=(n_blocks, tps),
            in_specs=[pl.BlockSpec(memory_space=pl.ANY), vec, vec,
                      pl.BlockSpec((TOK_TILE, LANES), lambda s, j, *_: (tile(s, j), 0)),
                      pl.BlockSpec((TOK_TILE, D_MODEL), lambda s, j, *_: (tile(s, j), 0)),
                      pl.BlockSpec((ROWS_PER_TILE, 6 * D_MODEL),
                                   lambda s, j, *_: (jnp.maximum(tile(s, j) - n_ptiles + pblocks, 0), 0))],
            out_specs=[pl.BlockSpec((TOK_TILE, D_MODEL),
                                    lambda s, j, *_: (jnp.minimum(tile(s, j), n_ptiles - 1), 0)),
                       pl.BlockSpec((TOK_TILE, D_MODEL),
                                    lambda s, j, *_: (jnp.maximum(tile(s, j) - n_ptiles, 0), 0))],
            scratch_shapes=[pltpu.VMEM((_local_rows(sb) * ROW_PIECES, LANES), U32),
                            pltpu.VMEM((TOK_TILE * ROW_PIECES, LANES), U32),
                            pltpu.VMEM((TOK_TILE * ROW_PIECES, LANES), U32),
                            pltpu.VMEM((ROW_PIECES, 2 * TOK_TILE, LANES), F32),
                            pltpu.SemaphoreType.DMA(())]),
        out_shape=[jax.ShapeDtypeStruct((n_ptiles * TOK_TILE, D_MODEL), F32),
                   jax.ShapeDtypeStruct((n_stiles * TOK_TILE, D_MODEL), F32)],
        compiler_params=_params(2),
        name="moe_combine",
    )(plan["loc"], plan["n"], plan["dst"], plan["blk"], ys, a1, a2, posw, x1, mod)


def _layer(xp, xs, cache_k, cache_v, state, c_prompt, c_sample, norm_mix_g, norm_ffn_g, w_ada, b_ada, w_in,
           q_norm_g, k_norm_g, rel_bias, w_gate_up, b_gate, gla_norm_g, w_out, w_route_group,
           b_route_group, w_route_expert, b_route_expert, w_exp_gate, w_exp_up, w_exp_down):
    batch, seq, _ = xp.shape
    n_seq, dec_seq, _ = xs.shape
    assert batch == 1 and dec_seq == CHUNK and cache_k.shape[1] == BAND_PAST
    assert seq % TOK_TILE == 0 and seq >= BAND_PAST and (n_seq * CHUNK) % TOK_TILE == 0
    n_ptok, n_stok = seq, n_seq * CHUNK
    n_ptiles, n_stiles = n_ptok // TOK_TILE, n_stok // TOK_TILE
    sb = MOE_SUPER_BLOCK if (n_ptok % MOE_SUPER_BLOCK == 0 and n_stok % MOE_SUPER_BLOCK == 0) else TOK_TILE
    prep = ROWS_PER_TILE

    xp2 = xp.reshape(n_ptok, D_MODEL)
    xs2 = xs.reshape(n_stok, D_MODEL)
    c_rows = jnp.concatenate([jnp.broadcast_to(c_prompt, (prep, D_MODEL)), c_sample], axis=0)
    mod = _adaln(c_rows, w_ada, b_ada)

    w_in_p = jnp.pad(w_in, ((0, 0), (0, IN_PAD - w_in.shape[1]))).astype(BF16)
    wgu_p = jnp.pad(w_gate_up, ((0, LANES - GATE_RANK), (0, 0))).astype(BF16)
    head = np.arange(A_WIDTH) // A_HEAD_DIM
    bd = jnp.asarray(head[:, None] == head[None, :], BF16)
    gq = jnp.tile(q_norm_g, A_HEADS).reshape(1, A_WIDTH)
    gk = jnp.tile(k_norm_g, A_HEADS).reshape(1, A_WIDTH)
    q, k, v, kf, vf, gla, la = _inproj(
        xp2, xs2, mod, norm_mix_g.reshape(1, D_MODEL), w_in_p, bd, gq, gk, wgu_p,
        b_gate.reshape(1, B_KWIDTH), n_ptiles, n_stiles, prep)

    first_chunk = n_ptok // CHUNK
    oa_p = _attn_prompt(rel_bias[:, _bias_lanes(3 * Q_ROWS)], q, k, v, n_ptok // Q_ROWS)
    oa_s = _attn_sample(rel_bias[:, _bias_lanes(SAMPLE_KEYS)], q, k, v,
                        cache_k.reshape(n_seq, BAND_PAST, A_WIDTH), cache_v.reshape(n_seq, BAND_PAST, A_WIDTH),
                        first_chunk, n_seq)
    g_gla = gla_norm_g.reshape(1, B_DV)
    ob_p, sfin_p = _gla_prompt(gla, la, g_gla, n_ptok // (GLA_CHUNKS * CHUNK))
    ob_s, sfin_s = _gla_sample(gla, la, g_gla, _state_to_pairs(state), first_chunk, n_seq)

    wr = jnp.pad(jnp.concatenate([w_route_group, w_route_expert], axis=1),
                 ((0, 0), (0, LANES - N_GROUPS - N_EXPERTS)))
    br = jnp.pad(jnp.concatenate([b_route_group, b_route_expert]), (0, LANES - N_GROUPS - N_EXPERTS))
    x1, h2p, meta, cnt = _outproj(oa_p, oa_s, ob_p, ob_s, w_out.astype(BF16), xp2, xs2, mod,
                                  norm_ffn_g.reshape(1, D_MODEL), wr, br.reshape(1, LANES),
                                  n_ptiles, n_stiles, prep, sb)

    posw, plan = _moe_plan(meta, cnt, sb)
    a1, a2 = posw[:, 0].astype(jnp.int32), posw[:, 1].astype(jnp.int32)
    xs_sorted = _moe_dispatch(h2p.reshape(-1, LANES), a1, a2, plan, sb, _sorted_tiles(n_ptok + n_stok, sb))
    ys_sorted = _moe_ffn(xs_sorted, plan, w_exp_gate, w_exp_up, w_exp_down)
    yp, ys = _moe_combine(ys_sorted, a1, a2, posw, plan, x1, mod, sb, n_ptiles, n_stiles, prep)

    tail = min(BAND_PAST, seq)
    heads = (A_HEADS, A_HEAD_DIM)
    return (yp.reshape(1, seq, D_MODEL), ys.reshape(n_seq, CHUNK, D_MODEL),
            kf[TOK_TILE - tail:TOK_TILE].reshape((1, tail) + heads),
            vf[TOK_TILE - tail:TOK_TILE].reshape((1, tail) + heads),
            _pairs_to_state(sfin_p)[None],
            kf[TOK_TILE:].reshape((n_seq, CHUNK) + heads),
            vf[TOK_TILE:].reshape((n_seq, CHUNK) + heads),
            _pairs_to_state(sfin_s))


def kernel(x_prompt, x_sample, cache_a_k, cache_a_v, state_gla, c_prompt, c_sample, norm_mix_g, norm_ffn_g,
           w_ada, b_ada, w_in, q_norm_g, k_norm_g, rel_bias, w_gate_up, b_gate, gla_norm_g, w_out,
           w_route_group, b_route_group, w_route_expert, b_route_expert, w_exp_gate, w_exp_up, w_exp_down):
    depth = w_in.shape[0]
    yp, ys = x_prompt, x_sample
    outs = [[] for _ in range(6)]
    for l in range(depth):
        yp, ys, kp, vp, sp, ks, vs, ss = _layer(
            yp, ys, cache_a_k[l], cache_a_v[l], state_gla[l], c_prompt, c_sample, norm_mix_g[l], norm_ffn_g[l],
            w_ada[l], b_ada[l], w_in[l], q_norm_g[l], k_norm_g[l], rel_bias[l], w_gate_up[l], b_gate[l],
            gla_norm_g[l], w_out[l], w_route_group[l], b_route_group[l], w_route_expert[l], b_route_expert[l],
            w_exp_gate[l], w_exp_up[l], w_exp_down[l])
        for lst, val in zip(outs, (kp, vp, sp, ks, vs, ss)):
            lst.append(val)
    return (yp, ys) + tuple(jnp.stack(o) for o in outs)
```

```python
import functools

import numpy as np
import jax
import jax.numpy as jnp
from jax import lax
from jax.experimental import pallas as pl
from jax.experimental.pallas import tpu as pltpu

F32 = jnp.float32
BF16 = jnp.bfloat16
U32 = jnp.uint32

D_MODEL = 1024
CHUNK = 64
LOG_CHUNK = 6
BAND_CHUNKS = 8
BAND_PAST = BAND_CHUNKS * CHUNK
A_WIDTH = 512
A_HEADS = 8
A_HEAD_DIM = 64
MAX_REL = 128
N_REL = CHUNK + MAX_REL
B_WIDTH = 512
B_HEADS = 4
B_DV = 128
B_DK = 64
B_KWIDTH = 256
GATE_RANK = 16
GATE_TAU = 16.0
N_GROUPS = 4
EXPERTS_PER_GROUP = 8
N_EXPERTS = 32
EXPERT_FF = 256
EPS = 1e-6

LANES = 128
IN_MAIN = 3 * A_WIDTH + 2 * B_KWIDTH + 2 * B_WIDTH
IN_PAD = IN_MAIN + LANES
TOK_TILE = 512
ROWS_PER_TILE = TOK_TILE // CHUNK
Q_CHUNKS = 4
Q_ROWS = Q_CHUNKS * CHUNK
ROLL_W = 1024
NEG = -1e30
ROUTE_OFF = N_GROUPS
VMEM_LIMIT = 56 * 1024 * 1024


def _params(n_axes=1):
    return pltpu.CompilerParams(dimension_semantics=("arbitrary",) * n_axes,
                                vmem_limit_bytes=VMEM_LIMIT)


def _split(a):
    hi = a.astype(BF16)
    lo = (a - hi.astype(F32)).astype(BF16)
    return hi, lo


def _dot(a, b):
    return jnp.dot(a, b, preferred_element_type=F32)


def _dot3(a, b):
    ah, al = _split(a)
    bh, bl = _split(b)
    return _dot(ah, bh) + _dot(al, bh) + _dot(ah, bl)


def _dot_nt(a, b):
    return lax.dot_general(a, b, (((1,), (1,)), ((), ())), preferred_element_type=F32)


def _dot_tn(a, b):
    return lax.dot_general(a, b, (((0,), (0,)), ((), ())), preferred_element_type=F32)


def _silu(x):
    return x / (1.0 + jnp.exp(-x))


def _rows_to_tokens(rows, n):
    r = rows.shape[0]
    return jnp.broadcast_to(rows[:, None, :], (r, CHUNK, n)).reshape(r * CHUNK, n)


def _adaln_kernel(c_ref, w_ref, b_ref, o_ref):
    a = _silu(c_ref[...])
    o_ref[...] = _dot3(a, w_ref[...]) + b_ref[...]


def _adaln(c_rows, w_ada, b_ada):
    r = c_rows.shape[0]
    n = w_ada.shape[1]
    tn = 1024
    return pl.pallas_call(
        _adaln_kernel,
        grid=(n // tn,),
        in_specs=[pl.BlockSpec((r, D_MODEL), lambda j: (0, 0)),
                  pl.BlockSpec((D_MODEL, tn), lambda j: (0, j)),
                  pl.BlockSpec((1, tn), lambda j: (0, j))],
        out_specs=pl.BlockSpec((r, tn), lambda j: (0, j)),
        out_shape=jax.ShapeDtypeStruct((r, n), F32),
        compiler_params=_params(),
        name="adaln",
    )(c_rows, w_ada, b_ada.reshape(1, n))


def _head_rms(z, bd_ref, g):
    ms = _dot((z * z).astype(BF16), bd_ref[...]) * (1.0 / A_HEAD_DIM)
    return z * lax.rsqrt(ms + EPS) * g


def _inproj_kernel(n_ptiles, xp_ref, xs_ref, mod_ref, gmix_ref, w_ref, bd_ref, gq_ref, gk_ref,
                   wgu_ref, bg_ref,
                   q_ref, k_ref, v_ref, kf_ref, vf_ref, gla_ref, la_ref):
    i = pl.program_id(0)
    x = jnp.where(i < n_ptiles, xp_ref[...], xs_ref[...])
    ms = jnp.mean(x * x, axis=-1, keepdims=True)
    xn = x * lax.rsqrt(ms + EPS) * gmix_ref[...]
    sh = _rows_to_tokens(mod_ref[:, 0:D_MODEL], D_MODEL)
    sc = _rows_to_tokens(mod_ref[:, D_MODEL:2 * D_MODEL], D_MODEL)
    hb = (xn * (1.0 + sc) + sh).astype(BF16)

    zq = _dot(hb, w_ref[:, 0:A_WIDTH])
    q_ref[...] = (_head_rms(zq, bd_ref, gq_ref[...]) * (A_HEAD_DIM ** -0.5)).astype(BF16)
    zk = _dot(hb, w_ref[:, A_WIDTH:2 * A_WIDTH])
    kn = _head_rms(zk, bd_ref, gk_ref[...])
    k_ref[...] = kn.astype(BF16)
    kf_ref[...] = kn
    zv = _dot(hb, w_ref[:, 2 * A_WIDTH:3 * A_WIDTH])
    v_ref[...] = zv.astype(BF16)
    vf_ref[...] = zv

    o = 3 * A_WIDTH
    zqb = _dot(hb, w_ref[:, o:o + B_KWIDTH]) * (B_DK ** -0.5)
    gla_ref[:, 0:B_KWIDTH] = zqb.astype(BF16)
    for c in range(B_KWIDTH, 2 * B_KWIDTH + 2 * B_WIDTH, 256):
        gla_ref[:, c:c + 256] = _dot(hb, w_ref[:, o + c:o + c + 256]).astype(BF16)

    gr = _dot(hb, w_ref[:, IN_MAIN:IN_PAD])
    logit = _dot(gr.astype(BF16), wgu_ref[...]) + bg_ref[...]
    log_sig = jnp.minimum(logit, 0.0) - jnp.log1p(jnp.exp(-jnp.abs(logit)))
    la_ref[...] = log_sig * (1.0 / GATE_TAU)


def _inproj(xp, xs, mod, gmix, w_in_p, bd, gq, gk, wgu_p, bg, n_ptiles, n_stiles, prep):
    n_tiles = n_ptiles + n_stiles
    t = n_tiles * TOK_TILE
    tail_tiles = 1 + n_stiles
    pblocks = prep // ROWS_PER_TILE
    const = lambda i: (0, 0)
    row = lambda i: (i, 0)
    tail = lambda i: (jnp.maximum(i - (n_ptiles - 1), 0), 0)
    return pl.pallas_call(
        functools.partial(_inproj_kernel, n_ptiles),
        grid=(n_tiles,),
        in_specs=[pl.BlockSpec((TOK_TILE, D_MODEL), lambda i: (jnp.minimum(i, n_ptiles - 1), 0)),
                  pl.BlockSpec((TOK_TILE, D_MODEL), lambda i: (jnp.maximum(i - n_ptiles, 0), 0)),
                  pl.BlockSpec((ROWS_PER_TILE, 6 * D_MODEL),
                               lambda i: (jnp.maximum(i - n_ptiles + pblocks, 0), 0)),
                  pl.BlockSpec((1, D_MODEL), const),
                  pl.BlockSpec((D_MODEL, IN_PAD), const),
                  pl.BlockSpec((A_WIDTH, A_WIDTH), const),
                  pl.BlockSpec((1, A_WIDTH), const),
                  pl.BlockSpec((1, A_WIDTH), const),
                  pl.BlockSpec((LANES, B_KWIDTH), const),
                  pl.BlockSpec((1, B_KWIDTH), const)],
        out_specs=[pl.BlockSpec((TOK_TILE, A_WIDTH), row),
                   pl.BlockSpec((TOK_TILE, A_WIDTH), row),
                   pl.BlockSpec((TOK_TILE, A_WIDTH), row),
                   pl.BlockSpec((TOK_TILE, A_WIDTH), tail),
                   pl.BlockSpec((TOK_TILE, A_WIDTH), tail),
                   pl.BlockSpec((TOK_TILE, 2 * B_KWIDTH + 2 * B_WIDTH), row),
                   pl.BlockSpec((TOK_TILE, B_KWIDTH), row)],
        out_shape=[jax.ShapeDtypeStruct((t, A_WIDTH), BF16),
                   jax.ShapeDtypeStruct((t, A_WIDTH), BF16),
                   jax.ShapeDtypeStruct((t, A_WIDTH), BF16),
                   jax.ShapeDtypeStruct((tail_tiles * TOK_TILE, A_WIDTH), F32),
                   jax.ShapeDtypeStruct((tail_tiles * TOK_TILE, A_WIDTH), F32),
                   jax.ShapeDtypeStruct((t, 2 * B_KWIDTH + 2 * B_WIDTH), BF16),
                   jax.ShapeDtypeStruct((t, B_KWIDTH), F32)],
        compiler_params=_params(),
        name="inproj",
    )(xp, xs, mod, gmix, w_in_p, bd, gq, gk, wgu_p, bg)


def _bias_lanes(n_keys):
    l = np.arange(ROLL_W)
    d = np.where(l < n_keys, BAND_PAST - l, BAND_PAST - l + ROLL_W)
    return np.clip(d, -(CHUNK - 1), MAX_REL) + (CHUNK - 1)


def _build_bias(u_ref, bias_sc, m_rows, n_keys):
    qi = lax.broadcasted_iota(jnp.int32, (m_rows, n_keys), 0) >> LOG_CHUNK
    kc = lax.broadcasted_iota(jnp.int32, (m_rows, n_keys), 1) >> LOG_CHUNK
    band = (kc >= qi) & (kc <= qi + BAND_CHUNKS)
    for h in range(A_HEADS):
        src = jnp.broadcast_to(u_ref[h:h + 1, :], (m_rows, ROLL_W))
        toe = pltpu.roll(src, 0, 1, stride=1, stride_axis=0)
        bias_sc[h] = jnp.where(band, toe[:, 0:n_keys], NEG)


def _attend(q, kcat, vcat, bias_sc, key_mask):
    m_rows = q.shape[0]
    lane = lax.broadcasted_iota(jnp.int32, (m_rows, LANES), 1)
    half = [lane < A_HEAD_DIM, lane >= A_HEAD_DIM]
    outs = []
    for p in range(A_HEADS // 2):
        lanes = slice(p * LANES, (p + 1) * LANES)
        qp, kp, vp = q[:, lanes], kcat[:, lanes], vcat[:, lanes]
        halves = []
        for hh in range(2):
            qm = jnp.where(half[hh], qp, jnp.zeros_like(qp))
            s = _dot_nt(qm, kp) + bias_sc[2 * p + hh]
            if key_mask is not None:
                s = s + key_mask
            e = jnp.exp(s - jnp.max(s, axis=-1, keepdims=True))
            l = jnp.sum(e, axis=-1, keepdims=True)
            halves.append(_dot(e.astype(BF16), vp) / l)
        outs.append(jnp.where(half[0], halves[0], halves[1]))
    return jnp.concatenate(outs, axis=-1)


def _attn_prompt_kernel(u_ref, q_ref, k0_ref, k1_ref, k2_ref, v0_ref, v1_ref, v2_ref, o_ref, bias_sc):
    j = pl.program_id(0)
    n_keys = 3 * Q_ROWS

    @pl.when(j == 0)
    def _():
        _build_bias(u_ref, bias_sc, Q_ROWS, n_keys)

    kcat = jnp.concatenate([k0_ref[...], k1_ref[...], k2_ref[...]], axis=0)
    vcat = jnp.concatenate([v0_ref[...], v1_ref[...], v2_ref[...]], axis=0)
    kw = lax.broadcasted_iota(jnp.int32, (1, n_keys), 1)
    key_mask = jnp.where(kw >= (2 - j) * Q_ROWS, 0.0, NEG)
    o_ref[...] = _attend(q_ref[...], kcat, vcat, bias_sc, key_mask).astype(BF16)


def _attn_prompt(u, q, k, v, n_steps):
    const = lambda j: (0, 0)
    blk = lambda d: pl.BlockSpec((Q_ROWS, A_WIDTH), lambda j, d=d: (jnp.maximum(j - d, 0), 0))
    return pl.pallas_call(
        _attn_prompt_kernel,
        grid=(n_steps,),
        in_specs=[pl.BlockSpec((A_HEADS, ROLL_W), const),
                  blk(0), blk(2), blk(1), blk(0), blk(2), blk(1), blk(0)],
        out_specs=pl.BlockSpec((Q_ROWS, A_WIDTH), lambda j: (j, 0)),
        out_shape=jax.ShapeDtypeStruct((n_steps * Q_ROWS, A_WIDTH), BF16),
        scratch_shapes=[pltpu.VMEM((A_HEADS, Q_ROWS, 3 * Q_ROWS), F32)],
        compiler_params=_params(),
        name="attn_prompt",
    )(u, q, k, k, k, v, v, v)


SAMPLE_KEYS = BAND_PAST + 2 * CHUNK


def _attn_sample_kernel(u_ref, q_ref, kn_ref, vn_ref, kc_ref, vc_ref, o_ref, bias_sc):
    @pl.when(pl.program_id(0) == 0)
    def _():
        _build_bias(u_ref, bias_sc, CHUNK, SAMPLE_KEYS)

    pad = jnp.zeros((CHUNK, A_WIDTH), BF16)
    kcat = jnp.concatenate([kc_ref[0].astype(BF16), kn_ref[...], pad], axis=0)
    vcat = jnp.concatenate([vc_ref[0].astype(BF16), vn_ref[...], pad], axis=0)
    o_ref[...] = _attend(q_ref[...], kcat, vcat, bias_sc, None).astype(BF16)


def _attn_sample(u, q, k, v, kc, vc, first_chunk, n_seq):
    new = pl.BlockSpec((CHUNK, A_WIDTH), lambda b: (first_chunk + b, 0))
    cache = pl.BlockSpec((1, BAND_PAST, A_WIDTH), lambda b: (b, 0, 0))
    return pl.pallas_call(
        _attn_sample_kernel,
        grid=(n_seq,),
        in_specs=[pl.BlockSpec((A_HEADS, ROLL_W), lambda b: (0, 0)), new, new, new, cache, cache],
        out_specs=pl.BlockSpec((CHUNK, A_WIDTH), lambda b: (b, 0)),
        out_shape=jax.ShapeDtypeStruct((n_seq * CHUNK, A_WIDTH), BF16),
        scratch_shapes=[pltpu.VMEM((A_HEADS, CHUNK, SAMPLE_KEYS), F32)],
        compiler_params=_params(),
        name="attn_sample",
    )(u, q, k, v, kc, vc)


GLA_CHUNKS = 4


def _gla_block(n_chunks, gla_ref, la_ref, ltri_ref, g_ref, st_sc, o_ref):
    rows = n_chunks * CHUNK
    la = la_ref[...]
    la_hi, la_lo = _split(la)
    b = _dot(ltri_ref[...], la_hi) + _dot(ltri_ref[...], la_lo)
    b3 = b.reshape(n_chunks, CHUNK, B_KWIDTH)
    b_mid = b3[:, CHUNK // 2 - 1:CHUNK // 2, :]
    b_last = b3[:, CHUNK - 1:CHUNK, :]
    q = gla_ref[:, 0:B_KWIDTH].astype(F32).reshape(n_chunks, CHUNK, B_KWIDTH)
    k = gla_ref[:, B_KWIDTH:2 * B_KWIDTH].astype(F32).reshape(n_chunks, CHUNK, B_KWIDTH)
    q_start = (q * jnp.exp(b3)).reshape(rows, B_KWIDTH).astype(BF16)
    q_mid = (q * jnp.exp(b3 - b_mid)).reshape(rows, B_KWIDTH).astype(BF16)
    k_mid = (k * jnp.exp(b_mid - b3)).reshape(rows, B_KWIDTH).astype(BF16)
    k_end = (k * jnp.exp(b_last - b3)).reshape(rows, B_KWIDTH).astype(BF16)
    dec = jnp.exp(b_last)

    ti = lax.broadcasted_iota(jnp.int32, (rows, rows), 0)
    si = lax.broadcasted_iota(jnp.int32, (rows, rows), 1)
    causal = (si <= ti) & ((si >> LOG_CHUNK) == (ti >> LOG_CHUNK))
    lane_r = lax.broadcasted_iota(jnp.int32, (rows, LANES), 1)
    half_r = [lane_r < B_DK, lane_r >= B_DK]
    lane_c = lax.broadcasted_iota(jnp.int32, (CHUNK, LANES), 1)
    half_c = [lane_c < B_DK, lane_c >= B_DK]
    half_s = lax.broadcasted_iota(jnp.int32, (B_DV, LANES), 1) < B_DK

    for p in range(B_HEADS // 2):
        lanes = slice(p * LANES, (p + 1) * LANES)
        qs_p, qm_p, km_p, ke_p = q_start[:, lanes], q_mid[:, lanes], k_mid[:, lanes], k_end[:, lanes]
        vs = [gla_ref[:, 2 * B_KWIDTH + (2 * p + hh) * B_DV:2 * B_KWIDTH + (2 * p + hh + 1) * B_DV]
              for hh in range(2)]
        intra = []
        for hh in range(2):
            qm = jnp.where(half_r[hh], qm_p, jnp.zeros_like(qm_p))
            sc = jnp.where(causal, _dot_nt(qm, km_p), 0.0)
            intra.append(_dot(sc.astype(BF16), vs[hh]))
        inter = [[], []]
        st = st_sc[p]
        for c in range(n_chunks):
            cr = slice(c * CHUNK, (c + 1) * CHUNK)
            st_b = st.astype(BF16)
            for hh in range(2):
                qc = jnp.where(half_c[hh], qs_p[cr], jnp.zeros((CHUNK, LANES), BF16))
                inter[hh].append(_dot_nt(qc, st_b))
            upd = jnp.where(half_s, _dot_tn(vs[0][cr], ke_p[cr]), _dot_tn(vs[1][cr], ke_p[cr]))
            st = st * dec[c, :, lanes] + upd
        st_sc[p] = st
        for hh in range(2):
            h = 2 * p + hh
            o = intra[hh] + jnp.concatenate(inter[hh], axis=0)
            ms = jnp.mean(o * o, axis=-1, keepdims=True)
            on = o * lax.rsqrt(ms + EPS) * g_ref[...]
            r = gla_ref[:, 2 * B_KWIDTH + B_WIDTH + h * B_DV:2 * B_KWIDTH + B_WIDTH + (h + 1) * B_DV]
            o_ref[:, h * B_DV:(h + 1) * B_DV] = (on * _silu(r.astype(F32))).astype(BF16)


def _gla_prompt_kernel(gla_ref, la_ref, ltri_ref, g_ref, o_ref, sfin_ref, st_sc):
    @pl.when(pl.program_id(0) == 0)
    def _():
        st_sc[...] = jnp.zeros_like(st_sc)

    _gla_block(GLA_CHUNKS, gla_ref, la_ref, ltri_ref, g_ref, st_sc, o_ref)
    sfin_ref[...] = st_sc[...]


def _gla_sample_kernel(gla_ref, la_ref, ltri_ref, g_ref, s0_ref, o_ref, sfin_ref, st_sc):
    st_sc[...] = s0_ref[0]
    _gla_block(1, gla_ref, la_ref, ltri_ref, g_ref, st_sc, o_ref)
    sfin_ref[0] = st_sc[...]


def _ltri(n_chunks):
    r = np.arange(n_chunks * CHUNK)
    m = (r[None, :] <= r[:, None]) & (r[None, :] // CHUNK == r[:, None] // CHUNK)
    return jnp.asarray(m, BF16)


_GLA_W = 2 * B_KWIDTH + 2 * B_WIDTH
_ST_SHAPE = (B_HEADS // 2, B_DV, LANES)


def _gla_prompt(gla, la, g, n_steps):
    rows = GLA_CHUNKS * CHUNK
    const = lambda j: (0, 0)
    return pl.pallas_call(
        _gla_prompt_kernel,
        grid=(n_steps,),
        in_specs=[pl.BlockSpec((rows, _GLA_W), lambda j: (j, 0)),
                  pl.BlockSpec((rows, B_KWIDTH), lambda j: (j, 0)),
                  pl.BlockSpec((rows, rows), const),
                  pl.BlockSpec((1, B_DV), const)],
        out_specs=[pl.BlockSpec((rows, B_WIDTH), lambda j: (j, 0)),
                   pl.BlockSpec(_ST_SHAPE, lambda j: (0, 0, 0))],
        out_shape=[jax.ShapeDtypeStruct((n_steps * rows, B_WIDTH), BF16),
                   jax.ShapeDtypeStruct(_ST_SHAPE, F32)],
        scratch_shapes=[pltpu.VMEM(_ST_SHAPE, F32)],
        compiler_params=_params(),
        name="gla_prompt",
    )(gla, la, _ltri(GLA_CHUNKS), g)


def _gla_sample(gla, la, g, s0, first_chunk, n_seq):
    const = lambda b: (0, 0)
    st_spec = pl.BlockSpec((1,) + _ST_SHAPE, lambda b: (b, 0, 0, 0))
    return pl.pallas_call(
        _gla_sample_kernel,
        grid=(n_seq,),
        in_specs=[pl.BlockSpec((CHUNK, _GLA_W), lambda b: (first_chunk + b, 0)),
                  pl.BlockSpec((CHUNK, B_KWIDTH), lambda b: (first_chunk + b, 0)),
                  pl.BlockSpec((CHUNK, CHUNK), const),
                  pl.BlockSpec((1, B_DV), const),
                  st_spec],
        out_specs=[pl.BlockSpec((CHUNK, B_WIDTH), lambda b: (b, 0)), st_spec],
        out_shape=[jax.ShapeDtypeStruct((n_seq * CHUNK, B_WIDTH), BF16),
                   jax.ShapeDtypeStruct((n_seq,) + _ST_SHAPE, F32)],
        scratch_shapes=[pltpu.VMEM(_ST_SHAPE, F32)],
        compiler_params=_params(),
        name="gla_sample",
    )(gla, la, _ltri(1), g, s0)


def _state_to_pairs(s):
    lead = s.shape[:-3]
    s = s.reshape(lead + (B_HEADS // 2, 2, B_DK, B_DV))
    s = jnp.moveaxis(s, -1, -3)
    return s.reshape(lead + (B_HEADS // 2, B_DV, 2 * B_DK))


def _pairs_to_state(s):
    lead = s.shape[:-3]
    s = s.reshape(lead + (B_HEADS // 2, B_DV, 2, B_DK))
    s = jnp.moveaxis(s, -3, -1)
    return s.reshape(lead + (B_HEADS, B_DK, B_DV))


def _route(logits):
    lane = lax.broadcasted_iota(jnp.int32, logits.shape, 1)
    lane_f = lane.astype(F32)
    big = float(LANES)
    gmask = lane < N_GROUPS
    gl = jnp.where(gmask, logits, NEG)
    gmax = jnp.max(gl, axis=-1, keepdims=True)
    gsel = jnp.min(jnp.where(gl == gmax, lane_f, big), axis=-1, keepdims=True)
    gsum = jnp.sum(jnp.where(gmask, jnp.exp(gl - gmax), 0.0), axis=-1, keepdims=True)
    g_w = 1.0 / gsum
    e_lo = ROUTE_OFF + gsel * EXPERTS_PER_GROUP
    emask = (lane_f >= e_lo) & (lane_f < e_lo + EXPERTS_PER_GROUP)
    el = jnp.where(emask, logits, NEG)
    v1 = jnp.max(el, axis=-1, keepdims=True)
    i1 = jnp.min(jnp.where(el == v1, lane_f, big), axis=-1, keepdims=True)
    el2 = jnp.where(lane_f == i1, NEG, el)
    v2 = jnp.max(el2, axis=-1, keepdims=True)
    i2 = jnp.min(jnp.where(el2 == v2, lane_f, big), axis=-1, keepdims=True)
    t = jnp.exp(v2 - v1)
    w1 = g_w / (1.0 + t)
    w2 = g_w * t / (1.0 + t)
    return lane_f, i1, i2, w1, w2


ROW_PIECES = D_MODEL // 2 // LANES
SUBLANES = 8
ROW_TILE = ROW_PIECES * SUBLANES


def _pack_rows(z32_sc, x, rows):
    half = D_MODEL // 2
    out = []
    for c in range(ROW_PIECES):
        z32_sc[c, pl.ds(0, rows, stride=2), :] = x[:, c * LANES:(c + 1) * LANES]
        z32_sc[c, pl.ds(1, rows, stride=2), :] = x[:, half + c * LANES:half + (c + 1) * LANES]
        out.append(z32_sc[c].astype(BF16))
    return out


def _unpack_rows(z32_sc, pieces, rows):
    lo, hi = [], []
    for c in range(ROW_PIECES):
        z32_sc[c] = pieces[c].astype(F32)
        lo.append(z32_sc[c, pl.ds(0, rows, stride=2), :])
        hi.append(z32_sc[c, pl.ds(1, rows, stride=2), :])
    return jnp.concatenate(lo, axis=1), jnp.concatenate(hi, axis=1)


def _to_row_tiled(pieces, tokens):
    per_tile = pieces[0].shape[0] * SUBLANES // tokens
    return jnp.stack([p.reshape(tokens // SUBLANES, per_tile, LANES) for p in pieces], axis=1)


def _from_row_tiled(flat, tokens):
    per_tile = flat.shape[0] // (tokens // SUBLANES) // ROW_PIECES
    tiled = flat.reshape(tokens // SUBLANES, ROW_PIECES, per_tile, LANES)
    return [tiled[:, c].reshape(tokens // SUBLANES * per_tile, LANES) for c in range(ROW_PIECES)]


def _flatten_tiled(tiled):
    return tiled.reshape(-1, LANES)


def _outproj_kernel(n_ptiles, tiles_per_sb, oap_ref, oas_ref, obp_ref, obs_ref, wo_ref, xp_ref, xs_ref, mod_ref,
                    gffn_ref, wr_ref, br_ref, ltri_ref, x1_ref, h2p_ref, meta_ref, cnt_ref, z32_sc, cnt_sc):
    i = pl.program_id(0)
    is_prompt = i < n_ptiles
    x = jnp.where(is_prompt, xp_ref[...], xs_ref[...])
    oa = jnp.where(is_prompt, oap_ref[...], oas_ref[...])
    ob = jnp.where(is_prompt, obp_ref[...], obs_ref[...])
    mix = _dot(oa, wo_ref[0:A_WIDTH, :]) + _dot(ob, wo_ref[A_WIDTH:D_MODEL, :])
    gate1 = _rows_to_tokens(mod_ref[:, 2 * D_MODEL:3 * D_MODEL], D_MODEL)
    x1 = x + gate1 * mix
    x1_ref[...] = x1
    ms = jnp.mean(x1 * x1, axis=-1, keepdims=True)
    xn = x1 * lax.rsqrt(ms + EPS) * gffn_ref[...]
    sh = _rows_to_tokens(mod_ref[:, 3 * D_MODEL:4 * D_MODEL], D_MODEL)
    sc = _rows_to_tokens(mod_ref[:, 4 * D_MODEL:5 * D_MODEL], D_MODEL)
    h2 = xn * (1.0 + sc) + sh
    words = [pltpu.bitcast(p, U32) for p in _pack_rows(z32_sc, h2, TOK_TILE)]
    h2p_ref[...] = _to_row_tiled(words, TOK_TILE)

    lane_f, i1, i2, w1, w2 = _route(_dot3(h2, wr_ref[...]) + br_ref[...])

    @pl.when(lax.rem(i, tiles_per_sb) == 0)
    def _():
        cnt_sc[...] = jnp.zeros_like(cnt_sc)

    sel = jnp.where((lane_f == i1) | (lane_f == i2), 1.0, 0.0).astype(BF16)
    before = _dot(ltri_ref[...], sel) + cnt_sc[0:1, :]
    rank1 = jnp.sum(jnp.where(lane_f == i1, before, 0.0), axis=-1, keepdims=True)
    rank2 = jnp.sum(jnp.where(lane_f == i2, before, 0.0), axis=-1, keepdims=True)
    cnt = cnt_sc[...] + _dot(jnp.ones((8, TOK_TILE), BF16), sel)
    cnt_sc[...] = cnt
    cnt_ref[0] = cnt
    cols = (i1, i2, rank1, rank2, w1, w2)
    meta = jnp.zeros_like(lane_f)
    for c, col in enumerate(cols):
        meta = jnp.where(lane_f == float(c), col, meta)
    meta_ref[...] = meta


def _outproj(oa_p, oa_s, ob_p, ob_s, w_out, xp, xs, mod, gffn, wr, br, n_ptiles, n_stiles, prep, sb):
    n_tiles = n_ptiles + n_stiles
    t = n_tiles * TOK_TILE
    pblocks = prep // ROWS_PER_TILE
    tiles_per_sb = sb // TOK_TILE
    const = lambda i: (0, 0)
    row = lambda i: (i, 0)
    prow = lambda i: (jnp.minimum(i, n_ptiles - 1), 0)
    srow = lambda i: (jnp.maximum(i - n_ptiles, 0), 0)
    r = np.arange(TOK_TILE)
    ltri = jnp.asarray(r[None, :] < r[:, None], BF16)
    return pl.pallas_call(
        functools.partial(_outproj_kernel, n_ptiles, tiles_per_sb),
        grid=(n_tiles,),
        in_specs=[pl.BlockSpec((TOK_TILE, A_WIDTH), prow),
                  pl.BlockSpec((TOK_TILE, A_WIDTH), srow),
                  pl.BlockSpec((TOK_TILE, B_WIDTH), prow),
                  pl.BlockSpec((TOK_TILE, B_WIDTH), srow),
                  pl.BlockSpec((D_MODEL, D_MODEL), const),
                  pl.BlockSpec((TOK_TILE, D_MODEL), prow),
                  pl.BlockSpec((TOK_TILE, D_MODEL), srow),
                  pl.BlockSpec((ROWS_PER_TILE, 6 * D_MODEL),
                               lambda i: (jnp.maximum(i - n_ptiles + pblocks, 0), 0)),
                  pl.BlockSpec((1, D_MODEL), const),
                  pl.BlockSpec((D_MODEL, LANES), const),
                  pl.BlockSpec((1, LANES), const),
                  pl.BlockSpec((TOK_TILE, TOK_TILE), const)],
        out_specs=[pl.BlockSpec((TOK_TILE, D_MODEL), row),
                   pl.BlockSpec((TOK_TILE // SUBLANES, ROW_PIECES, SUBLANES, LANES), lambda i: (i, 0, 0, 0)),
                   pl.BlockSpec((TOK_TILE, LANES), row),
                   pl.BlockSpec((1, 8, LANES), lambda i: (i // tiles_per_sb, 0, 0))],
        out_shape=[jax.ShapeDtypeStruct((t, D_MODEL), F32),
                   jax.ShapeDtypeStruct((t // SUBLANES, ROW_PIECES, SUBLANES, LANES), U32),
                   jax.ShapeDtypeStruct((t, LANES), F32),
                   jax.ShapeDtypeStruct((t // sb, 8, LANES), F32)],
        scratch_shapes=[pltpu.VMEM((D_MODEL // 2 // LANES, 2 * TOK_TILE, LANES), F32),
                        pltpu.VMEM((8, LANES), F32)],
        compiler_params=_params(),
        name="outproj",
    )(oa_p, oa_s, ob_p, ob_s, w_out, xp, xs, mod, gffn, wr, br, ltri)


MOE_SUPER_BLOCK = 2048
SEG_ALIGN = SUBLANES
CHUNK_BF16_ROWS = 2 * SEG_ALIGN * ROW_PIECES
SEG_BITS = 9
PAD_BITS = 5
FFN_ROWS = 256
PLAN_ROWS = LANES


def _local_rows(sb):
    return 2 * sb + N_EXPERTS * SEG_ALIGN


def _sorted_tiles(n_tokens, sb):
    rows = 2 * n_tokens + (n_tokens // sb) * N_EXPERTS * SEG_ALIGN + N_EXPERTS * FFN_ROWS
    return -(-rows // FFN_ROWS)


def _moe_plan_kernel(n_blocks, total_chunks, meta_ref, cnt_ref, ustrict_ref, lstrict_ref,
                     posw_ref, tab_ref, tile_ref):
    b = pl.program_id(0)
    per_tile = FFN_ROWS // SEG_ALIGN
    cnt = cnt_ref[...]
    chunks = jnp.floor((cnt + (SEG_ALIGN - 1)) * (1.0 / SEG_ALIGN))
    chunks_b = chunks.astype(BF16)
    loc = _dot(chunks_b, ustrict_ref[...])
    before = _dot(lstrict_ref[...], chunks_b)
    tot = _dot(jnp.ones((PLAN_ROWS, PLAN_ROWS), BF16), chunks_b)
    tiles = jnp.floor((tot + (per_tile - 1)) * (1.0 / per_tile))
    tile_off = _dot(tiles.astype(BF16), ustrict_ref[...])
    n_tiles = jnp.sum(tiles[0:1], axis=-1, keepdims=True)
    lane1 = lax.broadcasted_iota(jnp.int32, (PLAN_ROWS, LANES), 1)
    tail = lane1 == ROUTE_OFF + N_EXPERTS
    pad_off = jnp.where(tail, n_tiles * per_tile, tile_off * per_tile + tot)
    pad_n = jnp.where(tail, total_chunks - n_tiles * per_tile, tiles * per_tile - tot)
    blk_chunks = jnp.sum(chunks, axis=-1, keepdims=True)
    row = lax.broadcasted_iota(jnp.int32, (PLAN_ROWS, LANES), 0)
    misc = jnp.where(row == 0, pad_off, jnp.where(row == 1, pad_n, jnp.where(row == 2, n_tiles, 0.0)))
    tab_ref[0] = loc
    tab_ref[1] = chunks
    tab_ref[2] = tile_off * per_tile + before
    tab_ref[3] = misc
    tab_ref[4] = jnp.broadcast_to(blk_chunks, (PLAN_ROWS, LANES))
    t_idx = lax.broadcasted_iota(jnp.int32, tile_ref.shape, 0).astype(F32)
    lane_t = lax.broadcasted_iota(jnp.int32, tile_ref.shape, 1)
    is_expert = (lane_t >= ROUTE_OFF) & (lane_t < ROUTE_OFF + N_EXPERTS)
    ends = (tile_off + tiles)[0:1, :]
    owner = jnp.sum(jnp.where(is_expert & (ends <= t_idx), 1.0, 0.0), axis=-1, keepdims=True)
    tile_ref[...] = jnp.broadcast_to(jnp.minimum(owner, N_EXPERTS - 1.0), tile_ref.shape)

    own = jnp.floor((cnt_ref[pl.ds(b, 1), :] + (SEG_ALIGN - 1)) * (1.0 / SEG_ALIGN))
    own_off = _dot(jnp.broadcast_to(own, (SUBLANES, LANES)).astype(BF16), ustrict_ref[...]) * SEG_ALIGN
    meta = meta_ref[...]
    lane_f = lax.broadcasted_iota(jnp.int32, meta.shape, 1).astype(F32)
    off_row = own_off[0:1, :]
    pos = []
    for k in range(2):
        e_lane = meta[:, k:k + 1]
        base = jnp.sum(jnp.where(lane_f == e_lane, off_row, 0.0), axis=-1, keepdims=True)
        p = base + meta[:, 2 + k:3 + k]
        tile = jnp.floor(p * (1.0 / SUBLANES))
        pos.append(tile * (ROW_TILE - SUBLANES) + p)
    out = jnp.zeros_like(meta)
    for c, col in enumerate((pos[0], pos[1], meta[:, 4:5], meta[:, 5:6])):
        out = jnp.where(lane_f == float(c), col, out)
    posw_ref[...] = out


def _moe_plan(meta, cnt, sb):
    n_blocks = meta.shape[0] // sb
    assert n_blocks <= PLAN_ROWS and sb // SEG_ALIGN <= 256
    n_tiles = _sorted_tiles(meta.shape[0], sb)
    tile_rows = -(-n_tiles // SUBLANES) * SUBLANES
    r = np.arange(LANES)
    ustrict = jnp.asarray(r[:, None] < r[None, :], BF16)
    lstrict = jnp.asarray(r[None, :] < r[:, None], BF16)
    cnt_all = jnp.pad(cnt[:, 0, :], ((0, PLAN_ROWS - n_blocks), (0, 0)))
    const = lambda s: (0, 0)
    posw, tab, tile_owner = pl.pallas_call(
        functools.partial(_moe_plan_kernel, n_blocks, float(n_tiles * (FFN_ROWS // SEG_ALIGN))),
        grid=(n_blocks,),
        in_specs=[pl.BlockSpec((sb, LANES), lambda s: (s, 0)),
                  pl.BlockSpec((PLAN_ROWS, LANES), const),
                  pl.BlockSpec((LANES, LANES), const),
                  pl.BlockSpec((PLAN_ROWS, PLAN_ROWS), const)],
        out_specs=[pl.BlockSpec((sb, LANES), lambda s: (s, 0)),
                   pl.BlockSpec((5, PLAN_ROWS, LANES), lambda s: (0, 0, 0)),
                   pl.BlockSpec((tile_rows, LANES), const)],
        out_shape=[jax.ShapeDtypeStruct(meta.shape, F32),
                   jax.ShapeDtypeStruct((5, PLAN_ROWS, LANES), F32),
                   jax.ShapeDtypeStruct((tile_rows, LANES), F32)],
        compiler_params=_params(),
        name="moe_plan",
    )(meta, cnt_all, ustrict, lstrict)
    experts = slice(ROUTE_OFF, ROUTE_OFF + N_EXPERTS)
    to_i32 = lambda x: x.astype(jnp.int32).reshape(-1)
    plan = dict(
        loc=to_i32(tab[0, :n_blocks, experts]), n=to_i32(tab[1, :n_blocks, experts]),
        dst=to_i32(tab[2, :n_blocks, experts]),
        pad_off=to_i32(tab[3, 0, ROUTE_OFF:ROUTE_OFF + N_EXPERTS + 1]),
        pad_n=to_i32(tab[3, 1, ROUTE_OFF:ROUTE_OFF + N_EXPERTS + 1]),
        n_tiles=to_i32(tab[3, 2, 0:1]), blk=to_i32(tab[4, :n_blocks, 0]),
        owner=to_i32(tile_owner[:n_tiles, 0]))
    return posw, plan


def _token_rows(start):
    return pl.ds(start, ROW_PIECES, stride=SUBLANES)


def _pow2_copies(src_ref, dst_ref, src_chunk, dst_chunk, n, n_bits, sem, act):
    done = 0
    for k in reversed(range(n_bits)):
        take = (n >> k) & 1
        rows = CHUNK_BF16_ROWS << k
        src0 = 0 if src_chunk is None else pl.multiple_of((src_chunk + done) * CHUNK_BF16_ROWS, CHUNK_BF16_ROWS)
        dst0 = pl.multiple_of((dst_chunk + done) * CHUNK_BF16_ROWS, CHUNK_BF16_ROWS)

        @pl.when(take == 1)
        def _(src0=src0, dst0=dst0, rows=rows):
            act(pltpu.make_async_copy(src_ref.at[pl.ds(src0, rows)], dst_ref.at[pl.ds(dst0, rows)], sem))

        done = done + take * (1 << k)


def _segment_copies(block, loc_ref, n_ref, dst_ref, local_ref, global_ref, to_global, sem, act):
    def per_expert(e, carry):
        seg = block * N_EXPERTS + e
        if to_global:
            _pow2_copies(local_ref, global_ref, loc_ref[seg], dst_ref[seg], n_ref[seg], SEG_BITS, sem, act)
        else:
            _pow2_copies(global_ref, local_ref, dst_ref[seg], loc_ref[seg], n_ref[seg], SEG_BITS, sem, act)
        return carry

    lax.fori_loop(0, N_EXPERTS, per_expert, 0)


def _zero_fill(zero_ref, global_ref, padoff_ref, padn_ref, sem, act):
    full = 1 << PAD_BITS

    def per_pad(e, carry):
        def per_full(c, inner):
            dst0 = pl.multiple_of((padoff_ref[e] + c * full) * CHUNK_BF16_ROWS, CHUNK_BF16_ROWS)
            act(pltpu.make_async_copy(zero_ref, global_ref.at[pl.ds(dst0, full * CHUNK_BF16_ROWS)], sem))
            return inner

        n_full = padn_ref[e] >> PAD_BITS
        lax.fori_loop(0, n_full, per_full, 0)
        _pow2_copies(zero_ref, global_ref, None, padoff_ref[e] + n_full * full, padn_ref[e] & (full - 1),
                     PAD_BITS, sem, act)
        return carry

    lax.fori_loop(0, N_EXPERTS + 1, per_pad, 0)


STAGE_SLAB = 1024


def _restage(src_sc, dst_sc, dst_dtype):
    ratio = dst_sc.shape[0] / src_sc.shape[0]
    n_slabs = src_sc.shape[0] // (STAGE_SLAB if ratio > 1 else 2 * STAGE_SLAB)
    src_rows = src_sc.shape[0] // n_slabs
    dst_rows = dst_sc.shape[0] // n_slabs

    def slab(i, carry):
        s0 = pl.multiple_of(i * src_rows, src_rows)
        d0 = pl.multiple_of(i * dst_rows, dst_rows)
        dst_sc[pl.ds(d0, dst_rows), :] = pltpu.bitcast(src_sc[pl.ds(s0, src_rows), :], dst_dtype)
        return carry

    lax.fori_loop(0, n_slabs, slab, 0)


def _moe_dispatch_kernel(sb, loc_ref, n_ref, dst_ref, padoff_ref, padn_ref,
                         h2p_ref, a1_ref, a2_ref, xs_hbm, local_sc, stage_sc, zero_sc, sem):
    b = pl.program_id(0)
    local_sc[...] = jnp.zeros_like(local_sc)

    def step(g, carry):
        src = pl.multiple_of(g * ROW_TILE, ROW_TILE)
        for u in range(SUBLANES):
            t = g * SUBLANES + u
            row = h2p_ref[_token_rows(src + u), :]
            local_sc[_token_rows(a1_ref[t]), :] = row
            local_sc[_token_rows(a2_ref[t]), :] = row
        return carry

    lax.fori_loop(0, sb // SUBLANES, step, 0)
    _restage(local_sc, stage_sc, BF16)
    segments = functools.partial(_segment_copies, b, loc_ref, n_ref, dst_ref, stage_sc, xs_hbm, True, sem)
    segments(lambda c: c.start())

    @pl.when(b == 0)
    def _():
        zero_sc[...] = jnp.zeros_like(zero_sc)
        _zero_fill(zero_sc, xs_hbm, padoff_ref, padn_ref, sem, lambda c: c.start())
        _zero_fill(zero_sc, xs_hbm, padoff_ref, padn_ref, sem, lambda c: c.wait())

    segments(lambda c: c.wait())


def _smem_vec(n, index_map):
    return pl.BlockSpec((n,), index_map, memory_space=pltpu.SMEM)


def _moe_dispatch(h2p, a1, a2, plan, sb, n_tiles):
    n_blocks = h2p.shape[0] // (sb * ROW_PIECES)
    local_flat = _local_rows(sb) * ROW_PIECES
    vec = _smem_vec(sb, lambda s, *_: (s,))
    return pl.pallas_call(
        functools.partial(_moe_dispatch_kernel, sb),
        grid_spec=pltpu.PrefetchScalarGridSpec(
            num_scalar_prefetch=5,
            grid=(n_blocks,),
            in_specs=[pl.BlockSpec((sb * ROW_PIECES, LANES), lambda s, *_: (s, 0)), vec, vec],
            out_specs=pl.BlockSpec(memory_space=pl.ANY),
            scratch_shapes=[pltpu.VMEM((local_flat, LANES), U32),
                            pltpu.VMEM((2 * local_flat, LANES), BF16),
                            pltpu.VMEM(((1 << PAD_BITS) * CHUNK_BF16_ROWS, LANES), BF16),
                            pltpu.SemaphoreType.DMA(())]),
        out_shape=jax.ShapeDtypeStruct((n_tiles * FFN_ROWS * ROW_PIECES * 2, LANES), BF16),
        compiler_params=_params(),
        name="moe_dispatch",
    )(plan["loc"], plan["n"], plan["dst"], plan["pad_off"], plan["pad_n"], h2p, a1, a2)


def _moe_ffn_kernel(owner_ref, ntiles_ref, xs_ref, wg_ref, wu_ref, wd_ref, ys_ref,
                    wg_sc, wu_sc, wd_sc, z32_sc):
    i = pl.program_id(0)
    half = D_MODEL // 2
    used = i < ntiles_ref[0]

    @pl.when(used & ((i == 0) | (owner_ref[i] != owner_ref[jnp.maximum(i - 1, 0)])))
    def _():
        wg_sc[...] = wg_ref[0].astype(BF16)
        wu_sc[...] = wu_ref[0].astype(BF16)
        wd_sc[...] = wd_ref[0].astype(BF16)

    @pl.when(used)
    def _():
        lo, hi = _unpack_rows(z32_sc, _from_row_tiled(xs_ref[...], FFN_ROWS), FFN_ROWS)
        lo, hi = lo.astype(BF16), hi.astype(BF16)
        g = _dot(lo, wg_sc[0:half, :]) + _dot(hi, wg_sc[half:D_MODEL, :])
        u = _dot(lo, wu_sc[0:half, :]) + _dot(hi, wu_sc[half:D_MODEL, :])
        y = _dot((_silu(g) * u).astype(BF16), wd_sc[...])
        ys_ref[...] = _flatten_tiled(_to_row_tiled(_pack_rows(z32_sc, y, FFN_ROWS), FFN_ROWS))

    @pl.when(jnp.logical_not(used))
    def _():
        ys_ref[...] = jnp.zeros_like(ys_ref)


def _moe_ffn(xs, plan, wg, wu, wd):
    flat = FFN_ROWS * ROW_PIECES * 2
    n_tiles = xs.shape[0] // flat
    last_used = lambda i, owner, nt: jnp.minimum(i, nt[0] - 1)
    wspec = lambda shape: pl.BlockSpec((1,) + shape, lambda i, owner, nt: (owner[last_used(i, owner, nt)], 0, 0))
    return pl.pallas_call(
        _moe_ffn_kernel,
        grid_spec=pltpu.PrefetchScalarGridSpec(
            num_scalar_prefetch=2,
            grid=(n_tiles,),
            in_specs=[pl.BlockSpec((flat, LANES), lambda i, owner, nt: (last_used(i, owner, nt), 0)),
                      wspec((D_MODEL, EXPERT_FF)), wspec((D_MODEL, EXPERT_FF)), wspec((EXPERT_FF, D_MODEL))],
            out_specs=pl.BlockSpec((flat, LANES), lambda i, owner, nt: (i, 0)),
            scratch_shapes=[pltpu.VMEM((D_MODEL, EXPERT_FF), BF16),
                            pltpu.VMEM((D_MODEL, EXPERT_FF), BF16),
                            pltpu.VMEM((EXPERT_FF, D_MODEL), BF16),
                            pltpu.VMEM((ROW_PIECES, 2 * FFN_ROWS, LANES), F32)]),
        out_shape=jax.ShapeDtypeStruct(xs.shape, BF16),
        compiler_params=_params(),
        name="moe_ffn",
    )(plan["owner"], plan["n_tiles"], xs, wg, wu, wd)


def _moe_combine_kernel(n_psb, loc_ref, n_ref, dst_ref,
                        ys_hbm, a1_ref, a2_ref, posw_ref, x1_ref, mod_ref, yp_ref, yo_ref,
                        local_sc, stage_sc, g1_sc, g2_sc, z32_sc, sem):
    s = pl.program_id(0)

    @pl.when(pl.program_id(1) == 0)
    def _():
        segments = functools.partial(_segment_copies, s, loc_ref, n_ref, dst_ref, stage_sc, ys_hbm, False, sem)
        segments(lambda c: c.start())
        segments(lambda c: c.wait())
        _restage(stage_sc, local_sc, U32)

    def step(g, carry):
        dst = pl.multiple_of(g * ROW_TILE, ROW_TILE)
        for u in range(SUBLANES):
            t = g * SUBLANES + u
            g1_sc[_token_rows(dst + u), :] = local_sc[_token_rows(a1_ref[t]), :]
            g2_sc[_token_rows(dst + u), :] = local_sc[_token_rows(a2_ref[t]), :]
        return carry

    lax.fori_loop(0, TOK_TILE // SUBLANES, step, 0)
    halves = lambda g_sc: [pltpu.bitcast(p, BF16) for p in _from_row_tiled(g_sc[...], TOK_TILE)]
    lo1, hi1 = _unpack_rows(z32_sc, halves(g1_sc), TOK_TILE)
    lo2, hi2 = _unpack_rows(z32_sc, halves(g2_sc), TOK_TILE)
    w1, w2 = posw_ref[:, 2:3], posw_ref[:, 3:4]
    moe = jnp.concatenate([w1 * lo1 + w2 * lo2, w1 * hi1 + w2 * hi2], axis=1)
    gate2 = _rows_to_tokens(mod_ref[:, 5 * D_MODEL:6 * D_MODEL], D_MODEL)
    y = x1_ref[...] + gate2 * moe

    @pl.when(s < n_psb)
    def _():
        yp_ref[...] = y

    @pl.when(s >= n_psb)
    def _():
        yo_ref[...] = y


def _moe_combine(ys, a1, a2, posw, plan, x1, mod, sb, n_ptiles, n_stiles, prep):
    tps = sb // TOK_TILE
    n_blocks = (n_ptiles + n_stiles) // tps
    n_psb = n_ptiles // tps
    pblocks = prep // ROWS_PER_TILE
    tile = lambda s, j: s * tps + j
    vec = _smem_vec(TOK_TILE, lambda s, j, *_: (tile(s, j),))
    return pl.pallas_call(
        functools.partial(_moe_combine_kernel, n_psb),
        grid_spec=pltpu.PrefetchScalarGridSpec(
            num_scalar_prefetch=3,
            grid=(n_blocks, tps),
            in_specs=[pl.BlockSpec(memory_space=pl.ANY), vec, vec,
                      pl.BlockSpec((TOK_TILE, LANES), lambda s, j, *_: (tile(s, j), 0)),
                      pl.BlockSpec((TOK_TILE, D_MODEL), lambda s, j, *_: (tile(s, j), 0)),
                      pl.BlockSpec((ROWS_PER_TILE, 6 * D_MODEL),
                                   lambda s, j, *_: (jnp.maximum(tile(s, j) - n_ptiles + pblocks, 0), 0))],
            out_specs=[pl.BlockSpec((TOK_TILE, D_MODEL),
                                    lambda s, j, *_: (jnp.minimum(tile(s, j), n_ptiles - 1), 0)),
                       pl.BlockSpec((TOK_TILE, D_MODEL),
                                    lambda s, j, *_: (jnp.maximum(tile(s, j) - n_ptiles, 0), 0))],
            scratch_shapes=[pltpu.VMEM((_local_rows(sb) * ROW_PIECES, LANES), U32),
                            pltpu.VMEM((_local_rows(sb) * ROW_PIECES * 2, LANES), BF16),
                            pltpu.VMEM((TOK_TILE * ROW_PIECES, LANES), U32),
                            pltpu.VMEM((TOK_TILE * ROW_PIECES, LANES), U32),
                            pltpu.VMEM((ROW_PIECES, 2 * TOK_TILE, LANES), F32),
                            pltpu.SemaphoreType.DMA(())]),
        out_shape=[jax.ShapeDtypeStruct((n_ptiles * TOK_TILE, D_MODEL), F32),
                   jax.ShapeDtypeStruct((n_stiles * TOK_TILE, D_MODEL), F32)],
        compiler_params=_params(2),
        name="moe_combine",
    )(plan["loc"], plan["n"], plan["dst"], ys, a1, a2, posw, x1, mod)


def _layer(xp, xs, cache_k, cache_v, state, c_prompt, c_sample, norm_mix_g, norm_ffn_g, w_ada, b_ada, w_in,
           q_norm_g, k_norm_g, rel_bias, w_gate_up, b_gate, gla_norm_g, w_out, w_route_group,
           b_route_group, w_route_expert, b_route_expert, w_exp_gate, w_exp_up, w_exp_down):
    batch, seq, _ = xp.shape
    n_seq, dec_seq, _ = xs.shape
    assert batch == 1 and dec_seq == CHUNK and cache_k.shape[1] == BAND_PAST
    assert seq % TOK_TILE == 0 and seq >= BAND_PAST and (n_seq * CHUNK) % TOK_TILE == 0
    n_ptok, n_stok = seq, n_seq * CHUNK
    n_ptiles, n_stiles = n_ptok // TOK_TILE, n_stok // TOK_TILE
    sb = MOE_SUPER_BLOCK if (n_ptok % MOE_SUPER_BLOCK == 0 and n_stok % MOE_SUPER_BLOCK == 0) else TOK_TILE
    prep = ROWS_PER_TILE

    xp2 = xp.reshape(n_ptok, D_MODEL)
    xs2 = xs.reshape(n_stok, D_MODEL)
    c_rows = jnp.concatenate([jnp.broadcast_to(c_prompt, (prep, D_MODEL)), c_sample], axis=0)
    mod = _adaln(c_rows, w_ada, b_ada)

    w_in_p = jnp.pad(w_in, ((0, 0), (0, IN_PAD - w_in.shape[1]))).astype(BF16)
    wgu_p = jnp.pad(w_gate_up, ((0, LANES - GATE_RANK), (0, 0))).astype(BF16)
    head = np.arange(A_WIDTH) // A_HEAD_DIM
    bd = jnp.asarray(head[:, None] == head[None, :], BF16)
    gq = jnp.tile(q_norm_g, A_HEADS).reshape(1, A_WIDTH)
    gk = jnp.tile(k_norm_g, A_HEADS).reshape(1, A_WIDTH)
    q, k, v, kf, vf, gla, la = _inproj(
        xp2, xs2, mod, norm_mix_g.reshape(1, D_MODEL), w_in_p, bd, gq, gk, wgu_p,
        b_gate.reshape(1, B_KWIDTH), n_ptiles, n_stiles, prep)

    first_chunk = n_ptok // CHUNK
    oa_p = _attn_prompt(rel_bias[:, _bias_lanes(3 * Q_ROWS)], q, k, v, n_ptok // Q_ROWS)
    oa_s = _attn_sample(rel_bias[:, _bias_lanes(SAMPLE_KEYS)], q, k, v,
                        cache_k.reshape(n_seq, BAND_PAST, A_WIDTH), cache_v.reshape(n_seq, BAND_PAST, A_WIDTH),
                        first_chunk, n_seq)
    g_gla = gla_norm_g.reshape(1, B_DV)
    ob_p, sfin_p = _gla_prompt(gla, la, g_gla, n_ptok // (GLA_CHUNKS * CHUNK))
    ob_s, sfin_s = _gla_sample(gla, la, g_gla, _state_to_pairs(state), first_chunk, n_seq)

    wr = jnp.pad(jnp.concatenate([w_route_group, w_route_expert], axis=1),
                 ((0, 0), (0, LANES - N_GROUPS - N_EXPERTS)))
    br = jnp.pad(jnp.concatenate([b_route_group, b_route_expert]), (0, LANES - N_GROUPS - N_EXPERTS))
    x1, h2p, meta, cnt = _outproj(oa_p, oa_s, ob_p, ob_s, w_out.astype(BF16), xp2, xs2, mod,
                                  norm_ffn_g.reshape(1, D_MODEL), wr, br.reshape(1, LANES),
                                  n_ptiles, n_stiles, prep, sb)

    posw, plan = _moe_plan(meta, cnt, sb)
    a1, a2 = posw[:, 0].astype(jnp.int32), posw[:, 1].astype(jnp.int32)
    xs_sorted = _moe_dispatch(h2p.reshape(-1, LANES), a1, a2, plan, sb, _sorted_tiles(n_ptok + n_stok, sb))
    ys_sorted = _moe_ffn(xs_sorted, plan, w_exp_gate, w_exp_up, w_exp_down)
    yp, ys = _moe_combine(ys_sorted, a1, a2, posw, plan, x1, mod, sb, n_ptiles, n_stiles, prep)

    tail = min(BAND_PAST, seq)
    heads = (A_HEADS, A_HEAD_DIM)
    return (yp.reshape(1, seq, D_MODEL), ys.reshape(n_seq, CHUNK, D_MODEL),
            kf[TOK_TILE - tail:TOK_TILE].reshape((1, tail) + heads),
            vf[TOK_TILE - tail:TOK_TILE].reshape((1, tail) + heads),
            _pairs_to_state(sfin_p)[None],
            kf[TOK_TILE:].reshape((n_seq, CHUNK) + heads),
            vf[TOK_TILE:].reshape((n_seq, CHUNK) + heads),
            _pairs_to_state(sfin_s))


def kernel(x_prompt, x_sample, cache_a_k, cache_a_v, state_gla, c_prompt, c_sample, norm_mix_g, norm_ffn_g,
           w_ada, b_ada, w_in, q_norm_g, k_norm_g, rel_bias, w_gate_up, b_gate, gla_norm_g, w_out,
           w_route_group, b_route_group, w_route_expert, b_route_expert, w_exp_gate, w_exp_up, w_exp_down):
    depth = w_in.shape[0]
    yp, ys = x_prompt, x_sample
    outs = [[] for _ in range(6)]
    for l in range(depth):
        yp, ys, kp, vp, sp, ks, vs, ss = _layer(
            yp, ys, cache_a_k[l], cache_a_v[l], state_gla[l], c_prompt, c_sample, norm_mix_g[l], norm_ffn_g[l],
            w_ada[l], b_ada[l], w_in[l], q_norm_g[l], k_norm_g[l], rel_bias[l], w_gate_up[l], b_gate[l],
            gla_norm_g[l], w_out[l], w_route_group[l], b_route_group[l], w_route_expert[l], b_route_expert[l],
            w_exp_gate[l], w_exp_up[l], w_exp_down[l])
        for lst, val in zip(outs, (kp, vp, sp, ks, vs, ss)):
            lst.append(val)
    return (yp, ys) + tuple(jnp.stack(o) for o in outs)
```

```python
import functools

import numpy as np
import jax
import jax.numpy as jnp
from jax import lax
from jax.experimental import pallas as pl
from jax.experimental.pallas import tpu as pltpu

F32 = jnp.float32
BF16 = jnp.bfloat16
U32 = jnp.uint32

D_MODEL = 1024
CHUNK = 64
LOG_CHUNK = 6
BAND_CHUNKS = 8
BAND_PAST = BAND_CHUNKS * CHUNK
A_WIDTH = 512
A_HEADS = 8
A_HEAD_DIM = 64
MAX_REL = 128
N_REL = CHUNK + MAX_REL
B_WIDTH = 512
B_HEADS = 4
B_DV = 128
B_DK = 64
B_KWIDTH = 256
GATE_RANK = 16
GATE_TAU = 16.0
N_GROUPS = 4
EXPERTS_PER_GROUP = 8
N_EXPERTS = 32
EXPERT_FF = 256
EPS = 1e-6

LANES = 128
IN_MAIN = 3 * A_WIDTH + 2 * B_KWIDTH + 2 * B_WIDTH
IN_PAD = IN_MAIN + LANES
TOK_TILE = 512
ROWS_PER_TILE = TOK_TILE // CHUNK
Q_CHUNKS = 4
Q_ROWS = Q_CHUNKS * CHUNK
ROLL_W = 1024
NEG = -1e30
ROUTE_OFF = N_GROUPS
VMEM_LIMIT = 56 * 1024 * 1024


def _params(n_axes=1):
    return pltpu.CompilerParams(dimension_semantics=("arbitrary",) * n_axes,
                                vmem_limit_bytes=VMEM_LIMIT)


def _split(a):
    hi = a.astype(BF16)
    lo = (a - hi.astype(F32)).astype(BF16)
    return hi, lo


def _dot(a, b):
    return jnp.dot(a, b, preferred_element_type=F32)


def _dot3(a, b):
    ah, al = _split(a)
    bh, bl = _split(b)
    return _dot(ah, bh) + _dot(al, bh) + _dot(ah, bl)


def _dot_nt(a, b):
    return lax.dot_general(a, b, (((1,), (1,)), ((), ())), preferred_element_type=F32)


def _dot_tn(a, b):
    return lax.dot_general(a, b, (((0,), (0,)), ((), ())), preferred_element_type=F32)


def _silu(x):
    return x / (1.0 + jnp.exp(-x))


def _rows_to_tokens(rows, n):
    r = rows.shape[0]
    return jnp.broadcast_to(rows[:, None, :], (r, CHUNK, n)).reshape(r * CHUNK, n)


def _adaln_kernel(c_ref, w_ref, b_ref, o_ref):
    a = _silu(c_ref[...])
    o_ref[...] = _dot3(a, w_ref[...]) + b_ref[...]


def _adaln(c_rows, w_ada, b_ada):
    r = c_rows.shape[0]
    n = w_ada.shape[1]
    tn = 1024
    return pl.pallas_call(
        _adaln_kernel,
        grid=(n // tn,),
        in_specs=[pl.BlockSpec((r, D_MODEL), lambda j: (0, 0)),
                  pl.BlockSpec((D_MODEL, tn), lambda j: (0, j)),
                  pl.BlockSpec((1, tn), lambda j: (0, j))],
        out_specs=pl.BlockSpec((r, tn), lambda j: (0, j)),
        out_shape=jax.ShapeDtypeStruct((r, n), F32),
        compiler_params=_params(),
        name="adaln",
    )(c_rows, w_ada, b_ada.reshape(1, n))


def _head_rms(z, bd_ref, g):
    ms = _dot((z * z).astype(BF16), bd_ref[...]) * (1.0 / A_HEAD_DIM)
    return z * lax.rsqrt(ms + EPS) * g


def _inproj_kernel(n_ptiles, xp_ref, xs_ref, mod_ref, gmix_ref, w_ref, bd_ref, gq_ref, gk_ref,
                   wgu_ref, bg_ref,
                   q_ref, k_ref, v_ref, kf_ref, vf_ref, gla_ref, la_ref):
    i = pl.program_id(0)
    x = jnp.where(i < n_ptiles, xp_ref[...], xs_ref[...])
    ms = jnp.mean(x * x, axis=-1, keepdims=True)
    xn = x * lax.rsqrt(ms + EPS) * gmix_ref[...]
    sh = _rows_to_tokens(mod_ref[:, 0:D_MODEL], D_MODEL)
    sc = _rows_to_tokens(mod_ref[:, D_MODEL:2 * D_MODEL], D_MODEL)
    hb = (xn * (1.0 + sc) + sh).astype(BF16)

    zq = _dot(hb, w_ref[:, 0:A_WIDTH])
    q_ref[...] = (_head_rms(zq, bd_ref, gq_ref[...]) * (LOG2E * A_HEAD_DIM ** -0.5)).astype(BF16)
    zk = _dot(hb, w_ref[:, A_WIDTH:2 * A_WIDTH])
    kn = _head_rms(zk, bd_ref, gk_ref[...])
    k_ref[...] = kn.astype(BF16)
    kf_ref[...] = kn
    zv = _dot(hb, w_ref[:, 2 * A_WIDTH:3 * A_WIDTH])
    v_ref[...] = zv.astype(BF16)
    vf_ref[...] = zv

    o = 3 * A_WIDTH
    zqb = _dot(hb, w_ref[:, o:o + B_KWIDTH]) * (B_DK ** -0.5)
    gla_ref[:, 0:B_KWIDTH] = zqb.astype(BF16)
    for c in range(B_KWIDTH, 2 * B_KWIDTH + 2 * B_WIDTH, 256):
        gla_ref[:, c:c + 256] = _dot(hb, w_ref[:, o + c:o + c + 256]).astype(BF16)

    gr = _dot(hb, w_ref[:, IN_MAIN:IN_PAD])
    logit = _dot(gr.astype(BF16), wgu_ref[...]) + bg_ref[...]
    log_sig = jnp.minimum(logit, 0.0) - jnp.log1p(jnp.exp(-jnp.abs(logit)))
    la_ref[...] = log_sig * (1.0 / GATE_TAU)


def _inproj(xp, xs, mod, gmix, w_in_p, bd, gq, gk, wgu_p, bg, n_ptiles, n_stiles, prep):
    n_tiles = n_ptiles + n_stiles
    t = n_tiles * TOK_TILE
    tail_tiles = 1 + n_stiles
    pblocks = prep // ROWS_PER_TILE
    const = lambda i: (0, 0)
    row = lambda i: (i, 0)
    tail = lambda i: (jnp.maximum(i - (n_ptiles - 1), 0), 0)
    return pl.pallas_call(
        functools.partial(_inproj_kernel, n_ptiles),
        grid=(n_tiles,),
        in_specs=[pl.BlockSpec((TOK_TILE, D_MODEL), lambda i: (jnp.minimum(i, n_ptiles - 1), 0)),
                  pl.BlockSpec((TOK_TILE, D_MODEL), lambda i: (jnp.maximum(i - n_ptiles, 0), 0)),
                  pl.BlockSpec((ROWS_PER_TILE, 6 * D_MODEL),
                               lambda i: (jnp.maximum(i - n_ptiles + pblocks, 0), 0)),
                  pl.BlockSpec((1, D_MODEL), const),
                  pl.BlockSpec((D_MODEL, IN_PAD), const),
                  pl.BlockSpec((A_WIDTH, A_WIDTH), const),
                  pl.BlockSpec((1, A_WIDTH), const),
                  pl.BlockSpec((1, A_WIDTH), const),
                  pl.BlockSpec((LANES, B_KWIDTH), const),
                  pl.BlockSpec((1, B_KWIDTH), const)],
        out_specs=[pl.BlockSpec((TOK_TILE, A_WIDTH), row),
                   pl.BlockSpec((TOK_TILE, A_WIDTH), row),
                   pl.BlockSpec((TOK_TILE, A_WIDTH), row),
                   pl.BlockSpec((TOK_TILE, A_WIDTH), tail),
                   pl.BlockSpec((TOK_TILE, A_WIDTH), tail),
                   pl.BlockSpec((TOK_TILE, 2 * B_KWIDTH + 2 * B_WIDTH), row),
                   pl.BlockSpec((TOK_TILE, B_KWIDTH), row)],
        out_shape=[jax.ShapeDtypeStruct((t, A_WIDTH), BF16),
                   jax.ShapeDtypeStruct((t, A_WIDTH), BF16),
                   jax.ShapeDtypeStruct((t, A_WIDTH), BF16),
                   jax.ShapeDtypeStruct((tail_tiles * TOK_TILE, A_WIDTH), F32),
                   jax.ShapeDtypeStruct((tail_tiles * TOK_TILE, A_WIDTH), F32),
                   jax.ShapeDtypeStruct((t, 2 * B_KWIDTH + 2 * B_WIDTH), BF16),
                   jax.ShapeDtypeStruct((t, B_KWIDTH), F32)],
        compiler_params=_params(),
        name="inproj",
    )(xp, xs, mod, gmix, w_in_p, bd, gq, gk, wgu_p, bg)


def _bias_lanes(n_keys):
    l = np.arange(ROLL_W)
    d = np.where(l < n_keys, BAND_PAST - l, BAND_PAST - l + ROLL_W)
    return np.clip(d, -(CHUNK - 1), MAX_REL) + (CHUNK - 1)


LOG2E = 1.4426950408889634


def _band_mask(m_rows, n_keys, first_col):
    qi = lax.broadcasted_iota(jnp.int32, (m_rows, n_keys), 0) >> LOG_CHUNK
    kw = lax.broadcasted_iota(jnp.int32, (m_rows, n_keys), 1)
    kc = kw >> LOG_CHUNK
    return (kc >= qi) & (kc <= qi + BAND_CHUNKS) & (kw >= first_col)


def _bias_tile(u_ref, h, ok):
    m_rows, n_keys = ok.shape
    src = jnp.broadcast_to(u_ref[h:h + 1, :] * LOG2E, (m_rows, ROLL_W))
    toe = pltpu.roll(src, 0, 1, stride=1, stride_axis=0)
    return jnp.where(ok, toe[:, 0:n_keys], NEG)


def _attend(q, kcat, vcat, bias_sc):
    m_rows = q.shape[0]
    first = lax.broadcasted_iota(jnp.int32, (m_rows, LANES), 1) < A_HEAD_DIM
    outs = []
    for p in range(A_HEADS // 2):
        lanes = slice(p * LANES, (p + 1) * LANES)
        qp, kp, vp = q[:, lanes], kcat[:, lanes], vcat[:, lanes]
        zero = jnp.zeros_like(qp)
        q2 = jnp.concatenate([jnp.where(first, qp, zero), jnp.where(first, zero, qp)], axis=0)
        s = _dot_nt(q2, kp) + bias_sc[p]
        e = jnp.exp2(s - jnp.max(s, axis=-1, keepdims=True))
        l = jnp.sum(e, axis=-1, keepdims=True)
        o2 = _dot(e.astype(BF16), vp) / l
        outs.append(jnp.where(first, o2[0:m_rows], o2[m_rows:2 * m_rows]))
    return jnp.concatenate(outs, axis=-1)


def _attn_prompt_kernel(u_ref, q_ref, k0_ref, k1_ref, k2_ref, v0_ref, v1_ref, v2_ref, o_ref, bias_sc):
    j = pl.program_id(0)
    n_keys = 3 * Q_ROWS

    @pl.when(j <= 2)
    def _():
        ok = _band_mask(Q_ROWS, n_keys, (2 - j) * Q_ROWS)
        for h in range(A_HEADS):
            bias_sc[h // 2, (h % 2) * Q_ROWS:(h % 2 + 1) * Q_ROWS, :] = _bias_tile(u_ref, h, ok)

    kcat = jnp.concatenate([k0_ref[...], k1_ref[...], k2_ref[...]], axis=0)
    vcat = jnp.concatenate([v0_ref[...], v1_ref[...], v2_ref[...]], axis=0)
    o_ref[...] = _attend(q_ref[...], kcat, vcat, bias_sc).astype(BF16)


def _attn_prompt(u, q, k, v, n_steps):
    const = lambda j: (0, 0)
    blk = lambda d: pl.BlockSpec((Q_ROWS, A_WIDTH), lambda j, d=d: (jnp.maximum(j - d, 0), 0))
    return pl.pallas_call(
        _attn_prompt_kernel,
        grid=(n_steps,),
        in_specs=[pl.BlockSpec((A_HEADS, ROLL_W), const),
                  blk(0), blk(2), blk(1), blk(0), blk(2), blk(1), blk(0)],
        out_specs=pl.BlockSpec((Q_ROWS, A_WIDTH), lambda j: (j, 0)),
        out_shape=jax.ShapeDtypeStruct((n_steps * Q_ROWS, A_WIDTH), BF16),
        scratch_shapes=[pltpu.VMEM((A_HEADS // 2, 2 * Q_ROWS, 3 * Q_ROWS), F32)],
        compiler_params=_params(),
        name="attn_prompt",
    )(u, q, k, k, k, v, v, v)


SAMPLE_KEYS = BAND_PAST + 2 * CHUNK


def _attn_sample_kernel(u_ref, q_ref, kn_ref, vn_ref, kc_ref, vc_ref, o_ref, bias_sc):
    @pl.when(pl.program_id(0) == 0)
    def _():
        ok = _band_mask(CHUNK, SAMPLE_KEYS, 0)
        for p in range(A_HEADS // 2):
            pair = jnp.concatenate([_bias_tile(u_ref, 2 * p, ok), _bias_tile(u_ref, 2 * p + 1, ok)], axis=0)
            bias_sc[p] = pair.T

    pad = jnp.zeros((CHUNK, A_WIDTH), BF16)
    kcat = jnp.concatenate([kc_ref[0].astype(BF16), kn_ref[...], pad], axis=0)
    vcat = jnp.concatenate([vc_ref[0].astype(BF16), vn_ref[...], pad], axis=0)
    q = q_ref[...]
    lane = lax.broadcasted_iota(jnp.int32, (CHUNK, LANES), 1)
    first = lane < A_HEAD_DIM
    zero = jnp.zeros((CHUNK, LANES), BF16)
    outs = []
    for p in range(A_HEADS // 2):
        lanes = slice(p * LANES, (p + 1) * LANES)
        qp = q[:, lanes]
        q_rows = jnp.concatenate([jnp.where(first, qp, zero), jnp.where(first, zero, qp)], axis=0)
        s = _dot_nt(kcat[:, lanes], q_rows) + bias_sc[p]
        e = jnp.exp2(s - jnp.max(s, axis=0, keepdims=True))
        pn = (e * (1.0 / jnp.sum(e, axis=0, keepdims=True))).astype(BF16)
        r = _dot_tn(pn, vcat[:, lanes])
        outs.append(jnp.where(first, r[0:CHUNK], r[CHUNK:2 * CHUNK]))
    o_ref[...] = jnp.concatenate(outs, axis=-1).astype(BF16)


def _attn_sample(u, q, k, v, kc, vc, first_chunk, n_seq):
    new = pl.BlockSpec((CHUNK, A_WIDTH), lambda b: (first_chunk + b, 0))
    cache = pl.BlockSpec((1, BAND_PAST, A_WIDTH), lambda b: (b, 0, 0))
    return pl.pallas_call(
        _attn_sample_kernel,
        grid=(n_seq,),
        in_specs=[pl.BlockSpec((A_HEADS, ROLL_W), lambda b: (0, 0)), new, new, new, cache, cache],
        out_specs=pl.BlockSpec((CHUNK, A_WIDTH), lambda b: (b, 0)),
        out_shape=jax.ShapeDtypeStruct((n_seq * CHUNK, A_WIDTH), BF16),
        scratch_shapes=[pltpu.VMEM((A_HEADS // 2, SAMPLE_KEYS, LANES), F32)],
        compiler_params=_params(),
        name="attn_sample",
    )(u, q, k, v, kc, vc)


GLA_CHUNKS = 4


def _gla_block(n_chunks, gla_ref, la_ref, ltri_ref, g_ref, st_sc, o_ref):
    rows = n_chunks * CHUNK
    la = la_ref[...]
    la_hi, la_lo = _split(la)
    b = _dot(ltri_ref[...], la_hi) + _dot(ltri_ref[...], la_lo)
    b3 = b.reshape(n_chunks, CHUNK, B_KWIDTH)
    b_mid = b3[:, CHUNK // 2 - 1:CHUNK // 2, :]
    b_last = b3[:, CHUNK - 1:CHUNK, :]
    q = gla_ref[:, 0:B_KWIDTH].astype(F32).reshape(n_chunks, CHUNK, B_KWIDTH)
    k = gla_ref[:, B_KWIDTH:2 * B_KWIDTH].astype(F32).reshape(n_chunks, CHUNK, B_KWIDTH)
    q_start = (q * jnp.exp(b3)).reshape(rows, B_KWIDTH).astype(BF16)
    q_mid = (q * jnp.exp(b3 - b_mid)).reshape(rows, B_KWIDTH).astype(BF16)
    k_mid = (k * jnp.exp(b_mid - b3)).reshape(rows, B_KWIDTH).astype(BF16)
    k_end = (k * jnp.exp(b_last - b3)).reshape(rows, B_KWIDTH).astype(BF16)
    dec = jnp.exp(b_last)

    ti = lax.broadcasted_iota(jnp.int32, (rows, rows), 0)
    si = lax.broadcasted_iota(jnp.int32, (rows, rows), 1)
    causal = (si <= ti) & ((si >> LOG_CHUNK) == (ti >> LOG_CHUNK))
    lane_r = lax.broadcasted_iota(jnp.int32, (rows, LANES), 1)
    half_r = [lane_r < B_DK, lane_r >= B_DK]
    lane_c = lax.broadcasted_iota(jnp.int32, (CHUNK, LANES), 1)
    half_c = [lane_c < B_DK, lane_c >= B_DK]
    half_s = lax.broadcasted_iota(jnp.int32, (B_DV, LANES), 1) < B_DK

    for p in range(B_HEADS // 2):
        lanes = slice(p * LANES, (p + 1) * LANES)
        qs_p, qm_p, km_p, ke_p = q_start[:, lanes], q_mid[:, lanes], k_mid[:, lanes], k_end[:, lanes]
        vs = [gla_ref[:, 2 * B_KWIDTH + (2 * p + hh) * B_DV:2 * B_KWIDTH + (2 * p + hh + 1) * B_DV]
              for hh in range(2)]
        intra = []
        for hh in range(2):
            qm = jnp.where(half_r[hh], qm_p, jnp.zeros_like(qm_p))
            sc = jnp.where(causal, _dot_nt(qm, km_p), 0.0)
            intra.append(_dot(sc.astype(BF16), vs[hh]))
        inter = [[], []]
        st = st_sc[p]
        for c in range(n_chunks):
            cr = slice(c * CHUNK, (c + 1) * CHUNK)
            st_b = st.astype(BF16)
            for hh in range(2):
                qc = jnp.where(half_c[hh], qs_p[cr], jnp.zeros((CHUNK, LANES), BF16))
                inter[hh].append(_dot_nt(qc, st_b))
            upd = jnp.where(half_s, _dot_tn(vs[0][cr], ke_p[cr]), _dot_tn(vs[1][cr], ke_p[cr]))
            st = st * dec[c, :, lanes] + upd
        st_sc[p] = st
        for hh in range(2):
            h = 2 * p + hh
            o = intra[hh] + jnp.concatenate(inter[hh], axis=0)
            ms = jnp.mean(o * o, axis=-1, keepdims=True)
            on = o * lax.rsqrt(ms + EPS) * g_ref[...]
            r = gla_ref[:, 2 * B_KWIDTH + B_WIDTH + h * B_DV:2 * B_KWIDTH + B_WIDTH + (h + 1) * B_DV]
            o_ref[:, h * B_DV:(h + 1) * B_DV] = (on * _silu(r.astype(F32))).astype(BF16)


def _gla_prompt_kernel(gla_ref, la_ref, ltri_ref, g_ref, o_ref, sfin_ref, st_sc):
    @pl.when(pl.program_id(0) == 0)
    def _():
        st_sc[...] = jnp.zeros_like(st_sc)

    _gla_block(GLA_CHUNKS, gla_ref, la_ref, ltri_ref, g_ref, st_sc, o_ref)
    sfin_ref[...] = st_sc[...]


def _gla_sample_kernel(gla_ref, la_ref, ltri_ref, g_ref, s0_ref, o_ref, sfin_ref, st_sc):
    st_sc[...] = s0_ref[0]
    _gla_block(1, gla_ref, la_ref, ltri_ref, g_ref, st_sc, o_ref)
    sfin_ref[0] = st_sc[...]


def _ltri(n_chunks):
    r = np.arange(n_chunks * CHUNK)
    m = (r[None, :] <= r[:, None]) & (r[None, :] // CHUNK == r[:, None] // CHUNK)
    return jnp.asarray(m, BF16)


_GLA_W = 2 * B_KWIDTH + 2 * B_WIDTH
_ST_SHAPE = (B_HEADS // 2, B_DV, LANES)


def _gla_prompt(gla, la, g, n_steps):
    rows = GLA_CHUNKS * CHUNK
    const = lambda j: (0, 0)
    return pl.pallas_call(
        _gla_prompt_kernel,
        grid=(n_steps,),
        in_specs=[pl.BlockSpec((rows, _GLA_W), lambda j: (j, 0)),
                  pl.BlockSpec((rows, B_KWIDTH), lambda j: (j, 0)),
                  pl.BlockSpec((rows, rows), const),
                  pl.BlockSpec((1, B_DV), const)],
        out_specs=[pl.BlockSpec((rows, B_WIDTH), lambda j: (j, 0)),
                   pl.BlockSpec(_ST_SHAPE, lambda j: (0, 0, 0))],
        out_shape=[jax.ShapeDtypeStruct((n_steps * rows, B_WIDTH), BF16),
                   jax.ShapeDtypeStruct(_ST_SHAPE, F32)],
        scratch_shapes=[pltpu.VMEM(_ST_SHAPE, F32)],
        compiler_params=_params(),
        name="gla_prompt",
    )(gla, la, _ltri(GLA_CHUNKS), g)


def _gla_sample(gla, la, g, s0, first_chunk, n_seq):
    const = lambda b: (0, 0)
    st_spec = pl.BlockSpec((1,) + _ST_SHAPE, lambda b: (b, 0, 0, 0))
    return pl.pallas_call(
        _gla_sample_kernel,
        grid=(n_seq,),
        in_specs=[pl.BlockSpec((CHUNK, _GLA_W), lambda b: (first_chunk + b, 0)),
                  pl.BlockSpec((CHUNK, B_KWIDTH), lambda b: (first_chunk + b, 0)),
                  pl.BlockSpec((CHUNK, CHUNK), const),
                  pl.BlockSpec((1, B_DV), const),
                  st_spec],
        out_specs=[pl.BlockSpec((CHUNK, B_WIDTH), lambda b: (b, 0)), st_spec],
        out_shape=[jax.ShapeDtypeStruct((n_seq * CHUNK, B_WIDTH), BF16),
                   jax.ShapeDtypeStruct((n_seq,) + _ST_SHAPE, F32)],
        scratch_shapes=[pltpu.VMEM(_ST_SHAPE, F32)],
        compiler_params=_params(),
        name="gla_sample",
    )(gla, la, _ltri(1), g, s0)


def _state_to_pairs(s):
    lead = s.shape[:-3]
    s = s.reshape(lead + (B_HEADS // 2, 2, B_DK, B_DV))
    s = jnp.moveaxis(s, -1, -3)
    return s.reshape(lead + (B_HEADS // 2, B_DV, 2 * B_DK))


def _pairs_to_state(s):
    lead = s.shape[:-3]
    s = s.reshape(lead + (B_HEADS // 2, B_DV, 2, B_DK))
    s = jnp.moveaxis(s, -3, -1)
    return s.reshape(lead + (B_HEADS, B_DK, B_DV))


def _route(logits):
    lane = lax.broadcasted_iota(jnp.int32, logits.shape, 1)
    lane_f = lane.astype(F32)
    big = float(LANES)
    gmask = lane < N_GROUPS
    gl = jnp.where(gmask, logits, NEG)
    gmax = jnp.max(gl, axis=-1, keepdims=True)
    gsel = jnp.min(jnp.where(gl == gmax, lane_f, big), axis=-1, keepdims=True)
    gsum = jnp.sum(jnp.where(gmask, jnp.exp(gl - gmax), 0.0), axis=-1, keepdims=True)
    g_w = 1.0 / gsum
    e_lo = ROUTE_OFF + gsel * EXPERTS_PER_GROUP
    emask = (lane_f >= e_lo) & (lane_f < e_lo + EXPERTS_PER_GROUP)
    el = jnp.where(emask, logits, NEG)
    v1 = jnp.max(el, axis=-1, keepdims=True)
    i1 = jnp.min(jnp.where(el == v1, lane_f, big), axis=-1, keepdims=True)
    el2 = jnp.where(lane_f == i1, NEG, el)
    v2 = jnp.max(el2, axis=-1, keepdims=True)
    i2 = jnp.min(jnp.where(el2 == v2, lane_f, big), axis=-1, keepdims=True)
    t = jnp.exp(v2 - v1)
    w1 = g_w / (1.0 + t)
    w2 = g_w * t / (1.0 + t)
    return lane_f, i1, i2, w1, w2


ROW_PIECES = D_MODEL // 2 // LANES
SUBLANES = 8
ROW_TILE = ROW_PIECES * SUBLANES


def _pack_rows(z32_sc, x, rows):
    half = D_MODEL // 2
    out = []
    for c in range(ROW_PIECES):
        z32_sc[c, pl.ds(0, rows, stride=2), :] = x[:, c * LANES:(c + 1) * LANES]
        z32_sc[c, pl.ds(1, rows, stride=2), :] = x[:, half + c * LANES:half + (c + 1) * LANES]
        out.append(z32_sc[c].astype(BF16))
    return out


def _unpack_rows(z32_sc, pieces, rows):
    lo, hi = [], []
    for c in range(ROW_PIECES):
        z32_sc[c] = pieces[c].astype(F32)
        lo.append(z32_sc[c, pl.ds(0, rows, stride=2), :])
        hi.append(z32_sc[c, pl.ds(1, rows, stride=2), :])
    return jnp.concatenate(lo, axis=1), jnp.concatenate(hi, axis=1)


def _to_row_tiled(pieces, tokens):
    per_tile = pieces[0].shape[0] * SUBLANES // tokens
    return jnp.stack([p.reshape(tokens // SUBLANES, per_tile, LANES) for p in pieces], axis=1)


def _from_row_tiled(flat, tokens):
    per_tile = flat.shape[0] // (tokens // SUBLANES) // ROW_PIECES
    tiled = flat.reshape(tokens // SUBLANES, ROW_PIECES, per_tile, LANES)
    return [tiled[:, c].reshape(tokens // SUBLANES * per_tile, LANES) for c in range(ROW_PIECES)]


def _flatten_tiled(tiled):
    return tiled.reshape(-1, LANES)


def _outproj_kernel(n_ptiles, tiles_per_sb, oap_ref, oas_ref, obp_ref, obs_ref, wo_ref, xp_ref, xs_ref, mod_ref,
                    gffn_ref, wr_ref, br_ref, ltri_ref, x1_ref, h2p_ref, meta_ref, cnt_ref, z32_sc, cnt_sc):
    i = pl.program_id(0)
    is_prompt = i < n_ptiles
    x = jnp.where(is_prompt, xp_ref[...], xs_ref[...])
    oa = jnp.where(is_prompt, oap_ref[...], oas_ref[...])
    ob = jnp.where(is_prompt, obp_ref[...], obs_ref[...])
    mix = _dot(oa, wo_ref[0:A_WIDTH, :]) + _dot(ob, wo_ref[A_WIDTH:D_MODEL, :])
    gate1 = _rows_to_tokens(mod_ref[:, 2 * D_MODEL:3 * D_MODEL], D_MODEL)
    x1 = x + gate1 * mix
    x1_ref[...] = x1
    ms = jnp.mean(x1 * x1, axis=-1, keepdims=True)
    xn = x1 * lax.rsqrt(ms + EPS) * gffn_ref[...]
    sh = _rows_to_tokens(mod_ref[:, 3 * D_MODEL:4 * D_MODEL], D_MODEL)
    sc = _rows_to_tokens(mod_ref[:, 4 * D_MODEL:5 * D_MODEL], D_MODEL)
    h2 = xn * (1.0 + sc) + sh
    words = [pltpu.bitcast(p, U32) for p in _pack_rows(z32_sc, h2, TOK_TILE)]
    h2p_ref[...] = _to_row_tiled(words, TOK_TILE)

    lane_f, i1, i2, w1, w2 = _route(_dot3(h2, wr_ref[...]) + br_ref[...])

    @pl.when(lax.rem(i, tiles_per_sb) == 0)
    def _():
        cnt_sc[...] = jnp.zeros_like(cnt_sc)

    sel = jnp.where((lane_f == i1) | (lane_f == i2), 1.0, 0.0).astype(BF16)
    before = _dot(ltri_ref[...], sel) + cnt_sc[0:1, :]
    rank1 = jnp.sum(jnp.where(lane_f == i1, before, 0.0), axis=-1, keepdims=True)
    rank2 = jnp.sum(jnp.where(lane_f == i2, before, 0.0), axis=-1, keepdims=True)
    cnt = cnt_sc[...] + _dot(jnp.ones((8, TOK_TILE), BF16), sel)
    cnt_sc[...] = cnt
    cnt_ref[0] = cnt
    cols = (i1, i2, rank1, rank2, w1, w2)
    meta = jnp.zeros_like(lane_f)
    for c, col in enumerate(cols):
        meta = jnp.where(lane_f == float(c), col, meta)
    meta_ref[...] = meta


def _outproj(oa_p, oa_s, ob_p, ob_s, w_out, xp, xs, mod, gffn, wr, br, n_ptiles, n_stiles, prep, sb):
    n_tiles = n_ptiles + n_stiles
    t = n_tiles * TOK_TILE
    pblocks = prep // ROWS_PER_TILE
    tiles_per_sb = sb // TOK_TILE
    const = lambda i: (0, 0)
    row = lambda i: (i, 0)
    prow = lambda i: (jnp.minimum(i, n_ptiles - 1), 0)
    srow = lambda i: (jnp.maximum(i - n_ptiles, 0), 0)
    r = np.arange(TOK_TILE)
    ltri = jnp.asarray(r[None, :] < r[:, None], BF16)
    return pl.pallas_call(
        functools.partial(_outproj_kernel, n_ptiles, tiles_per_sb),
        grid=(n_tiles,),
        in_specs=[pl.BlockSpec((TOK_TILE, A_WIDTH), prow),
                  pl.BlockSpec((TOK_TILE, A_WIDTH), srow),
                  pl.BlockSpec((TOK_TILE, B_WIDTH), prow),
                  pl.BlockSpec((TOK_TILE, B_WIDTH), srow),
                  pl.BlockSpec((D_MODEL, D_MODEL), const),
                  pl.BlockSpec((TOK_TILE, D_MODEL), prow),
                  pl.BlockSpec((TOK_TILE, D_MODEL), srow),
                  pl.BlockSpec((ROWS_PER_TILE, 6 * D_MODEL),
                               lambda i: (jnp.maximum(i - n_ptiles + pblocks, 0), 0)),
                  pl.BlockSpec((1, D_MODEL), const),
                  pl.BlockSpec((D_MODEL, LANES), const),
                  pl.BlockSpec((1, LANES), const),
                  pl.BlockSpec((TOK_TILE, TOK_TILE), const)],
        out_specs=[pl.BlockSpec((TOK_TILE, D_MODEL), row),
                   pl.BlockSpec((TOK_TILE // SUBLANES, ROW_PIECES, SUBLANES, LANES), lambda i: (i, 0, 0, 0)),
                   pl.BlockSpec((TOK_TILE, LANES), row),
                   pl.BlockSpec((1, 8, LANES), lambda i: (i // tiles_per_sb, 0, 0))],
        out_shape=[jax.ShapeDtypeStruct((t, D_MODEL), F32),
                   jax.ShapeDtypeStruct((t // SUBLANES, ROW_PIECES, SUBLANES, LANES), U32),
                   jax.ShapeDtypeStruct((t, LANES), F32),
                   jax.ShapeDtypeStruct((t // sb, 8, LANES), F32)],
        scratch_shapes=[pltpu.VMEM((D_MODEL // 2 // LANES, 2 * TOK_TILE, LANES), F32),
                        pltpu.VMEM((8, LANES), F32)],
        compiler_params=_params(),
        name="outproj",
    )(oa_p, oa_s, ob_p, ob_s, w_out, xp, xs, mod, gffn, wr, br, ltri)


MOE_SUPER_BLOCK = 2048
SEG_ALIGN = SUBLANES
CHUNK_BF16_ROWS = 2 * SEG_ALIGN * ROW_PIECES
SEG_BITS = 9
PAD_BITS = 5
FFN_ROWS = 256
PLAN_ROWS = LANES


def _local_rows(sb):
    return 2 * sb + N_EXPERTS * SEG_ALIGN


def _sorted_tiles(n_tokens, sb):
    rows = 2 * n_tokens + (n_tokens // sb) * N_EXPERTS * SEG_ALIGN + N_EXPERTS * FFN_ROWS
    return -(-rows // FFN_ROWS)


def _moe_plan_kernel(n_blocks, total_chunks, meta_ref, cnt_ref, ustrict_ref, lstrict_ref,
                     posw_ref, tab_ref, tile_ref):
    b = pl.program_id(0)
    per_tile = FFN_ROWS // SEG_ALIGN
    cnt = cnt_ref[...]
    chunks = jnp.floor((cnt + (SEG_ALIGN - 1)) * (1.0 / SEG_ALIGN))
    chunks_b = chunks.astype(BF16)
    loc = _dot(chunks_b, ustrict_ref[...])
    before = _dot(lstrict_ref[...], chunks_b)
    tot = _dot(jnp.ones((PLAN_ROWS, PLAN_ROWS), BF16), chunks_b)
    tiles = jnp.floor((tot + (per_tile - 1)) * (1.0 / per_tile))
    tile_off = _dot(tiles.astype(BF16), ustrict_ref[...])
    n_tiles = jnp.sum(tiles[0:1], axis=-1, keepdims=True)
    lane1 = lax.broadcasted_iota(jnp.int32, (PLAN_ROWS, LANES), 1)
    tail = lane1 == ROUTE_OFF + N_EXPERTS
    pad_off = jnp.where(tail, n_tiles * per_tile, tile_off * per_tile + tot)
    pad_n = jnp.where(tail, total_chunks - n_tiles * per_tile, tiles * per_tile - tot)
    blk_chunks = jnp.sum(chunks, axis=-1, keepdims=True)
    row = lax.broadcasted_iota(jnp.int32, (PLAN_ROWS, LANES), 0)
    misc = jnp.where(row == 0, pad_off, jnp.where(row == 1, pad_n, jnp.where(row == 2, n_tiles, 0.0)))
    tab_ref[0] = loc
    tab_ref[1] = chunks
    tab_ref[2] = tile_off * per_tile + before
    tab_ref[3] = misc
    tab_ref[4] = jnp.broadcast_to(blk_chunks, (PLAN_ROWS, LANES))
    t_idx = lax.broadcasted_iota(jnp.int32, tile_ref.shape, 0).astype(F32)
    lane_t = lax.broadcasted_iota(jnp.int32, tile_ref.shape, 1)
    is_expert = (lane_t >= ROUTE_OFF) & (lane_t < ROUTE_OFF + N_EXPERTS)
    ends = (tile_off + tiles)[0:1, :]
    owner = jnp.sum(jnp.where(is_expert & (ends <= t_idx), 1.0, 0.0), axis=-1, keepdims=True)
    tile_ref[...] = jnp.broadcast_to(jnp.minimum(owner, N_EXPERTS - 1.0), tile_ref.shape)

    own = jnp.floor((cnt_ref[pl.ds(b, 1), :] + (SEG_ALIGN - 1)) * (1.0 / SEG_ALIGN))
    own_off = _dot(jnp.broadcast_to(own, (SUBLANES, LANES)).astype(BF16), ustrict_ref[...]) * SEG_ALIGN
    meta = meta_ref[...]
    lane_f = lax.broadcasted_iota(jnp.int32, meta.shape, 1).astype(F32)
    off_row = own_off[0:1, :]
    pos = []
    for k in range(2):
        e_lane = meta[:, k:k + 1]
        base = jnp.sum(jnp.where(lane_f == e_lane, off_row, 0.0), axis=-1, keepdims=True)
        p = base + meta[:, 2 + k:3 + k]
        tile = jnp.floor(p * (1.0 / SUBLANES))
        pos.append(tile * (ROW_TILE - SUBLANES) + p)
    out = jnp.zeros_like(meta)
    for c, col in enumerate((pos[0], pos[1], meta[:, 4:5], meta[:, 5:6])):
        out = jnp.where(lane_f == float(c), col, out)
    posw_ref[...] = out


def _moe_plan(meta, cnt, sb):
    n_blocks = meta.shape[0] // sb
    assert n_blocks <= PLAN_ROWS and sb // SEG_ALIGN <= 256
    n_tiles = _sorted_tiles(meta.shape[0], sb)
    tile_rows = -(-n_tiles // SUBLANES) * SUBLANES
    r = np.arange(LANES)
    ustrict = jnp.asarray(r[:, None] < r[None, :], BF16)
    lstrict = jnp.asarray(r[None, :] < r[:, None], BF16)
    cnt_all = jnp.pad(cnt[:, 0, :], ((0, PLAN_ROWS - n_blocks), (0, 0)))
    const = lambda s: (0, 0)
    posw, tab, tile_owner = pl.pallas_call(
        functools.partial(_moe_plan_kernel, n_blocks, float(n_tiles * (FFN_ROWS // SEG_ALIGN))),
        grid=(n_blocks,),
        in_specs=[pl.BlockSpec((sb, LANES), lambda s: (s, 0)),
                  pl.BlockSpec((PLAN_ROWS, LANES), const),
                  pl.BlockSpec((LANES, LANES), const),
                  pl.BlockSpec((PLAN_ROWS, PLAN_ROWS), const)],
        out_specs=[pl.BlockSpec((sb, LANES), lambda s: (s, 0)),
                   pl.BlockSpec((5, PLAN_ROWS, LANES), lambda s: (0, 0, 0)),
                   pl.BlockSpec((tile_rows, LANES), const)],
        out_shape=[jax.ShapeDtypeStruct(meta.shape, F32),
                   jax.ShapeDtypeStruct((5, PLAN_ROWS, LANES), F32),
                   jax.ShapeDtypeStruct((tile_rows, LANES), F32)],
        compiler_params=_params(),
        name="moe_plan",
    )(meta, cnt_all, ustrict, lstrict)
    experts = slice(ROUTE_OFF, ROUTE_OFF + N_EXPERTS)
    to_i32 = lambda x: x.astype(jnp.int32).reshape(-1)
    plan = dict(
        loc=to_i32(tab[0, :n_blocks, experts]), n=to_i32(tab[1, :n_blocks, experts]),
        dst=to_i32(tab[2, :n_blocks, experts]),
        pad_off=to_i32(tab[3, 0, ROUTE_OFF:ROUTE_OFF + N_EXPERTS + 1]),
        pad_n=to_i32(tab[3, 1, ROUTE_OFF:ROUTE_OFF + N_EXPERTS + 1]),
        n_tiles=to_i32(tab[3, 2, 0:1]), blk=to_i32(tab[4, :n_blocks, 0]),
        owner=to_i32(tile_owner[:n_tiles, 0]))
    return posw, plan


def _token_rows(start):
    return pl.ds(start, ROW_PIECES, stride=SUBLANES)


def _pow2_copies(src_ref, dst_ref, src_chunk, dst_chunk, n, n_bits, sem, act):
    done = 0
    for k in reversed(range(n_bits)):
        take = (n >> k) & 1
        rows = CHUNK_BF16_ROWS << k
        src0 = 0 if src_chunk is None else pl.multiple_of((src_chunk + done) * CHUNK_BF16_ROWS, CHUNK_BF16_ROWS)
        dst0 = pl.multiple_of((dst_chunk + done) * CHUNK_BF16_ROWS, CHUNK_BF16_ROWS)

        @pl.when(take == 1)
        def _(src0=src0, dst0=dst0, rows=rows):
            act(pltpu.make_async_copy(src_ref.at[pl.ds(src0, rows)], dst_ref.at[pl.ds(dst0, rows)], sem))

        done = done + take * (1 << k)


def _segment_copies(block, loc_ref, n_ref, dst_ref, local_ref, global_ref, to_global, sem, act):
    def per_expert(e, carry):
        seg = block * N_EXPERTS + e
        if to_global:
            _pow2_copies(local_ref, global_ref, loc_ref[seg], dst_ref[seg], n_ref[seg], SEG_BITS, sem, act)
        else:
            _pow2_copies(global_ref, local_ref, dst_ref[seg], loc_ref[seg], n_ref[seg], SEG_BITS, sem, act)
        return carry

    lax.fori_loop(0, N_EXPERTS, per_expert, 0)


def _zero_fill(zero_ref, global_ref, padoff_ref, padn_ref, sem, act):
    full = 1 << PAD_BITS

    def per_pad(e, carry):
        def per_full(c, inner):
            dst0 = pl.multiple_of((padoff_ref[e] + c * full) * CHUNK_BF16_ROWS, CHUNK_BF16_ROWS)
            act(pltpu.make_async_copy(zero_ref, global_ref.at[pl.ds(dst0, full * CHUNK_BF16_ROWS)], sem))
            return inner

        n_full = padn_ref[e] >> PAD_BITS
        lax.fori_loop(0, n_full, per_full, 0)
        _pow2_copies(zero_ref, global_ref, None, padoff_ref[e] + n_full * full, padn_ref[e] & (full - 1),
                     PAD_BITS, sem, act)
        return carry

    lax.fori_loop(0, N_EXPERTS + 1, per_pad, 0)


STAGE_SLAB = 1024


def _restage(src_sc, dst_sc, dst_dtype):
    ratio = dst_sc.shape[0] / src_sc.shape[0]
    n_slabs = src_sc.shape[0] // (STAGE_SLAB if ratio > 1 else 2 * STAGE_SLAB)
    src_rows = src_sc.shape[0] // n_slabs
    dst_rows = dst_sc.shape[0] // n_slabs

    def slab(i, carry):
        s0 = pl.multiple_of(i * src_rows, src_rows)
        d0 = pl.multiple_of(i * dst_rows, dst_rows)
        dst_sc[pl.ds(d0, dst_rows), :] = pltpu.bitcast(src_sc[pl.ds(s0, src_rows), :], dst_dtype)
        return carry

    lax.fori_loop(0, n_slabs, slab, 0)


def _moe_dispatch_kernel(sb, loc_ref, n_ref, dst_ref, padoff_ref, padn_ref,
                         h2p_ref, a1_ref, a2_ref, xs_hbm, local_sc, stage_sc, zero_sc, sem):
    b = pl.program_id(0)
    local_sc[...] = jnp.zeros_like(local_sc)

    def step(g, carry):
        src = pl.multiple_of(g * ROW_TILE, ROW_TILE)
        for u in range(SUBLANES):
            t = g * SUBLANES + u
            row = h2p_ref[_token_rows(src + u), :]
            local_sc[_token_rows(a1_ref[t]), :] = row
            local_sc[_token_rows(a2_ref[t]), :] = row
        return carry

    lax.fori_loop(0, sb // SUBLANES, step, 0)
    _restage(local_sc, stage_sc, BF16)
    segments = functools.partial(_segment_copies, b, loc_ref, n_ref, dst_ref, stage_sc, xs_hbm, True, sem)
    segments(lambda c: c.start())

    @pl.when(b == 0)
    def _():
        zero_sc[...] = jnp.zeros_like(zero_sc)
        _zero_fill(zero_sc, xs_hbm, padoff_ref, padn_ref, sem, lambda c: c.start())
        _zero_fill(zero_sc, xs_hbm, padoff_ref, padn_ref, sem, lambda c: c.wait())

    segments(lambda c: c.wait())


def _smem_vec(n, index_map):
    return pl.BlockSpec((n,), index_map, memory_space=pltpu.SMEM)


def _moe_dispatch(h2p, a1, a2, plan, sb, n_tiles):
    n_blocks = h2p.shape[0] // (sb * ROW_PIECES)
    local_flat = _local_rows(sb) * ROW_PIECES
    vec = _smem_vec(sb, lambda s, *_: (s,))
    return pl.pallas_call(
        functools.partial(_moe_dispatch_kernel, sb),
        grid_spec=pltpu.PrefetchScalarGridSpec(
            num_scalar_prefetch=5,
            grid=(n_blocks,),
            in_specs=[pl.BlockSpec((sb * ROW_PIECES, LANES), lambda s, *_: (s, 0)), vec, vec],
            out_specs=pl.BlockSpec(memory_space=pl.ANY),
            scratch_shapes=[pltpu.VMEM((local_flat, LANES), U32),
                            pltpu.VMEM((2 * local_flat, LANES), BF16),
                            pltpu.VMEM(((1 << PAD_BITS) * CHUNK_BF16_ROWS, LANES), BF16),
                            pltpu.SemaphoreType.DMA(())]),
        out_shape=jax.ShapeDtypeStruct((n_tiles * FFN_ROWS * ROW_PIECES * 2, LANES), BF16),
        compiler_params=_params(),
        name="moe_dispatch",
    )(plan["loc"], plan["n"], plan["dst"], plan["pad_off"], plan["pad_n"], h2p, a1, a2)


def _moe_ffn_kernel(owner_ref, ntiles_ref, xs_ref, wg_ref, wu_ref, wd_ref, ys_ref,
                    wg_sc, wu_sc, wd_sc, z32_sc):
    i = pl.program_id(0)
    half = D_MODEL // 2
    used = i < ntiles_ref[0]

    @pl.when(used & ((i == 0) | (owner_ref[i] != owner_ref[jnp.maximum(i - 1, 0)])))
    def _():
        wg_sc[...] = wg_ref[0].astype(BF16)
        wu_sc[...] = wu_ref[0].astype(BF16)
        wd_sc[...] = wd_ref[0].astype(BF16)

    @pl.when(used)
    def _():
        lo, hi = _unpack_rows(z32_sc, _from_row_tiled(xs_ref[...], FFN_ROWS), FFN_ROWS)
        lo, hi = lo.astype(BF16), hi.astype(BF16)
        g = _dot(lo, wg_sc[0:half, :]) + _dot(hi, wg_sc[half:D_MODEL, :])
        u = _dot(lo, wu_sc[0:half, :]) + _dot(hi, wu_sc[half:D_MODEL, :])
        y = _dot((_silu(g) * u).astype(BF16), wd_sc[...])
        ys_ref[...] = _flatten_tiled(_to_row_tiled(_pack_rows(z32_sc, y, FFN_ROWS), FFN_ROWS))

    @pl.when(jnp.logical_not(used))
    def _():
        ys_ref[...] = jnp.zeros_like(ys_ref)


def _moe_ffn(xs, plan, wg, wu, wd):
    flat = FFN_ROWS * ROW_PIECES * 2
    n_tiles = xs.shape[0] // flat
    last_used = lambda i, owner, nt: jnp.minimum(i, nt[0] - 1)
    wspec = lambda shape: pl.BlockSpec((1,) + shape, lambda i, owner, nt: (owner[last_used(i, owner, nt)], 0, 0))
    return pl.pallas_call(
        _moe_ffn_kernel,
        grid_spec=pltpu.PrefetchScalarGridSpec(
            num_scalar_prefetch=2,
            grid=(n_tiles,),
            in_specs=[pl.BlockSpec((flat, LANES), lambda i, owner, nt: (last_used(i, owner, nt), 0)),
                      wspec((D_MODEL, EXPERT_FF)), wspec((D_MODEL, EXPERT_FF)), wspec((EXPERT_FF, D_MODEL))],
            out_specs=pl.BlockSpec((flat, LANES), lambda i, owner, nt: (i, 0)),
            scratch_shapes=[pltpu.VMEM((D_MODEL, EXPERT_FF), BF16),
                            pltpu.VMEM((D_MODEL, EXPERT_FF), BF16),
                            pltpu.VMEM((EXPERT_FF, D_MODEL), BF16),
                            pltpu.VMEM((ROW_PIECES, 2 * FFN_ROWS, LANES), F32)]),
        out_shape=jax.ShapeDtypeStruct(xs.shape, BF16),
        compiler_params=_params(),
        name="moe_ffn",
    )(plan["owner"], plan["n_tiles"], xs, wg, wu, wd)


def _moe_combine_kernel(n_psb, loc_ref, n_ref, dst_ref,
                        ys_hbm, a1_ref, a2_ref, posw_ref, x1_ref, mod_ref, yp_ref, yo_ref,
                        local_sc, stage_sc, g1_sc, g2_sc, z32_sc, sem):
    s = pl.program_id(0)

    @pl.when(pl.program_id(1) == 0)
    def _():
        segments = functools.partial(_segment_copies, s, loc_ref, n_ref, dst_ref, stage_sc, ys_hbm, False, sem)
        segments(lambda c: c.start())
        segments(lambda c: c.wait())
        _restage(stage_sc, local_sc, U32)

    def step(g, carry):
        dst = pl.multiple_of(g * ROW_TILE, ROW_TILE)
        for u in range(SUBLANES):
            t = g * SUBLANES + u
            g1_sc[_token_rows(dst + u), :] = local_sc[_token_rows(a1_ref[t]), :]
            g2_sc[_token_rows(dst + u), :] = local_sc[_token_rows(a2_ref[t]), :]
        return carry

    lax.fori_loop(0, TOK_TILE // SUBLANES, step, 0)
    halves = lambda g_sc: [pltpu.bitcast(p, BF16) for p in _from_row_tiled(g_sc[...], TOK_TILE)]
    lo1, hi1 = _unpack_rows(z32_sc, halves(g1_sc), TOK_TILE)
    lo2, hi2 = _unpack_rows(z32_sc, halves(g2_sc), TOK_TILE)
    w1, w2 = posw_ref[:, 2:3], posw_ref[:, 3:4]
    moe = jnp.concatenate([w1 * lo1 + w2 * lo2, w1 * hi1 + w2 * hi2], axis=1)
    gate2 = _rows_to_tokens(mod_ref[:, 5 * D_MODEL:6 * D_MODEL], D_MODEL)
    y = x1_ref[...] + gate2 * moe

    @pl.when(s < n_psb)
    def _():
        yp_ref[...] = y

    @pl.when(s >= n_psb)
    def _():
        yo_ref[...] = y


def _moe_combine(ys, a1, a2, posw, plan, x1, mod, sb, n_ptiles, n_stiles, prep):
    tps = sb // TOK_TILE
    n_blocks = (n_ptiles + n_stiles) // tps
    n_psb = n_ptiles // tps
    pblocks = prep // ROWS_PER_TILE
    tile = lambda s, j: s * tps + j
    vec = _smem_vec(TOK_TILE, lambda s, j, *_: (tile(s, j),))
    return pl.pallas_call(
        functools.partial(_moe_combine_kernel, n_psb),
        grid_spec=pltpu.PrefetchScalarGridSpec(
            num_scalar_prefetch=3,
            grid=(n_blocks, tps),
            in_specs=[pl.BlockSpec(memory_space=pl.ANY), vec, vec,
                      pl.BlockSpec((TOK_TILE, LANES), lambda s, j, *_: (tile(s, j), 0)),
                      pl.BlockSpec((TOK_TILE, D_MODEL), lambda s, j, *_: (tile(s, j), 0)),
                      pl.BlockSpec((ROWS_PER_TILE, 6 * D_MODEL),
                                   lambda s, j, *_: (jnp.maximum(tile(s, j) - n_ptiles + pblocks, 0), 0))],
            out_specs=[pl.BlockSpec((TOK_TILE, D_MODEL),
                                    lambda s, j, *_: (jnp.minimum(tile(s, j), n_ptiles - 1), 0)),
                       pl.BlockSpec((TOK_TILE, D_MODEL),
                                    lambda s, j, *_: (jnp.maximum(tile(s, j) - n_ptiles, 0), 0))],
            scratch_shapes=[pltpu.VMEM((_local_rows(sb) * ROW_PIECES, LANES), U32),
                            pltpu.VMEM((_local_rows(sb) * ROW_PIECES * 2, LANES), BF16),
                            pltpu.VMEM((TOK_TILE * ROW_PIECES, LANES), U32),
                            pltpu.VMEM((TOK_TILE * ROW_PIECES, LANES), U32),
                            pltpu.VMEM((ROW_PIECES, 2 * TOK_TILE, LANES), F32),
                            pltpu.SemaphoreType.DMA(())]),
        out_shape=[jax.ShapeDtypeStruct((n_ptiles * TOK_TILE, D_MODEL), F32),
                   jax.ShapeDtypeStruct((n_stiles * TOK_TILE, D_MODEL), F32)],
        compiler_params=_params(2),
        name="moe_combine",
    )(plan["loc"], plan["n"], plan["dst"], ys, a1, a2, posw, x1, mod)


def _layer(xp, xs, cache_k, cache_v, state, c_prompt, c_sample, norm_mix_g, norm_ffn_g, w_ada, b_ada, w_in,
           q_norm_g, k_norm_g, rel_bias, w_gate_up, b_gate, gla_norm_g, w_out, w_route_group,
           b_route_group, w_route_expert, b_route_expert, w_exp_gate, w_exp_up, w_exp_down):
    batch, seq, _ = xp.shape
    n_seq, dec_seq, _ = xs.shape
    assert batch == 1 and dec_seq == CHUNK and cache_k.shape[1] == BAND_PAST
    assert seq % TOK_TILE == 0 and seq >= BAND_PAST and (n_seq * CHUNK) % TOK_TILE == 0
    n_ptok, n_stok = seq, n_seq * CHUNK
    n_ptiles, n_stiles = n_ptok // TOK_TILE, n_stok // TOK_TILE
    sb = MOE_SUPER_BLOCK if (n_ptok % MOE_SUPER_BLOCK == 0 and n_stok % MOE_SUPER_BLOCK == 0) else TOK_TILE
    prep = ROWS_PER_TILE

    xp2 = xp.reshape(n_ptok, D_MODEL)
    xs2 = xs.reshape(n_stok, D_MODEL)
    c_rows = jnp.concatenate([jnp.broadcast_to(c_prompt, (prep, D_MODEL)), c_sample], axis=0)
    mod = _adaln(c_rows, w_ada, b_ada)

    w_in_p = jnp.pad(w_in, ((0, 0), (0, IN_PAD - w_in.shape[1]))).astype(BF16)
    wgu_p = jnp.pad(w_gate_up, ((0, LANES - GATE_RANK), (0, 0))).astype(BF16)
    head = np.arange(A_WIDTH) // A_HEAD_DIM
    bd = jnp.asarray(head[:, None] == head[None, :], BF16)
    gq = jnp.tile(q_norm_g, A_HEADS).reshape(1, A_WIDTH)
    gk = jnp.tile(k_norm_g, A_HEADS).reshape(1, A_WIDTH)
    q, k, v, kf, vf, gla, la = _inproj(
        xp2, xs2, mod, norm_mix_g.reshape(1, D_MODEL), w_in_p, bd, gq, gk, wgu_p,
        b_gate.reshape(1, B_KWIDTH), n_ptiles, n_stiles, prep)

    first_chunk = n_ptok // CHUNK
    oa_p = _attn_prompt(rel_bias[:, _bias_lanes(3 * Q_ROWS)], q, k, v, n_ptok // Q_ROWS)
    oa_s = _attn_sample(rel_bias[:, _bias_lanes(SAMPLE_KEYS)], q, k, v,
                        cache_k.reshape(n_seq, BAND_PAST, A_WIDTH), cache_v.reshape(n_seq, BAND_PAST, A_WIDTH),
                        first_chunk, n_seq)
    g_gla = gla_norm_g.reshape(1, B_DV)
    ob_p, sfin_p = _gla_prompt(gla, la, g_gla, n_ptok // (GLA_CHUNKS * CHUNK))
    ob_s, sfin_s = _gla_sample(gla, la, g_gla, _state_to_pairs(state), first_chunk, n_seq)

    wr = jnp.pad(jnp.concatenate([w_route_group, w_route_expert], axis=1),
                 ((0, 0), (0, LANES - N_GROUPS - N_EXPERTS)))
    br = jnp.pad(jnp.concatenate([b_route_group, b_route_expert]), (0, LANES - N_GROUPS - N_EXPERTS))
    x1, h2p, meta, cnt = _outproj(oa_p, oa_s, ob_p, ob_s, w_out.astype(BF16), xp2, xs2, mod,
                                  norm_ffn_g.reshape(1, D_MODEL), wr, br.reshape(1, LANES),
                                  n_ptiles, n_stiles, prep, sb)

    posw, plan = _moe_plan(meta, cnt, sb)
    a1, a2 = posw[:, 0].astype(jnp.int32), posw[:, 1].astype(jnp.int32)
    xs_sorted = _moe_dispatch(h2p.reshape(-1, LANES), a1, a2, plan, sb, _sorted_tiles(n_ptok + n_stok, sb))
    ys_sorted = _moe_ffn(xs_sorted, plan, w_exp_gate, w_exp_up, w_exp_down)
    yp, ys = _moe_combine(ys_sorted, a1, a2, posw, plan, x1, mod, sb, n_ptiles, n_stiles, prep)

    tail = min(BAND_PAST, seq)
    heads = (A_HEADS, A_HEAD_DIM)
    return (yp.reshape(1, seq, D_MODEL), ys.reshape(n_seq, CHUNK, D_MODEL),
            kf[TOK_TILE - tail:TOK_TILE].reshape((1, tail) + heads),
            vf[TOK_TILE - tail:TOK_TILE].reshape((1, tail) + heads),
            _pairs_to_state(sfin_p)[None],
            kf[TOK_TILE:].reshape((n_seq, CHUNK) + heads),
            vf[TOK_TILE:].reshape((n_seq, CHUNK) + heads),
            _pairs_to_state(sfin_s))


def kernel(x_prompt, x_sample, cache_a_k, cache_a_v, state_gla, c_prompt, c_sample, norm_mix_g, norm_ffn_g,
           w_ada, b_ada, w_in, q_norm_g, k_norm_g, rel_bias, w_gate_up, b_gate, gla_norm_g, w_out,
           w_route_group, b_route_group, w_route_expert, b_route_expert, w_exp_gate, w_exp_up, w_exp_down):
    depth = w_in.shape[0]
    yp, ys = x_prompt, x_sample
    outs = [[] for _ in range(6)]
    for l in range(depth):
        yp, ys, kp, vp, sp, ks, vs, ss = _layer(
            yp, ys, cache_a_k[l], cache_a_v[l], state_gla[l], c_prompt, c_sample, norm_mix_g[l], norm_ffn_g[l],
            w_ada[l], b_ada[l], w_in[l], q_norm_g[l], k_norm_g[l], rel_bias[l], w_gate_up[l], b_gate[l],
            gla_norm_g[l], w_out[l], w_route_group[l], b_route_group[l], w_route_expert[l], b_route_expert[l],
            w_exp_gate[l], w_exp_up[l], w_exp_down[l])
        for lst, val in zip(outs, (kp, vp, sp, ks, vs, ss)):
            lst.append(val)
    return (yp, ys) + tuple(jnp.stack(o) for o in outs)
```

```python
import functools

import numpy as np
import jax
import jax.numpy as jnp
from jax import lax
from jax.experimental import pallas as pl
from jax.experimental.pallas import tpu as pltpu

F32 = jnp.float32
BF16 = jnp.bfloat16
U32 = jnp.uint32

D_MODEL = 1024
CHUNK = 64
LOG_CHUNK = 6
BAND_CHUNKS = 8
BAND_PAST = BAND_CHUNKS * CHUNK
A_WIDTH = 512
A_HEADS = 8
A_HEAD_DIM = 64
MAX_REL = 128
N_REL = CHUNK + MAX_REL
B_WIDTH = 512
B_HEADS = 4
B_DV = 128
B_DK = 64
B_KWIDTH = 256
GATE_RANK = 16
GATE_TAU = 16.0
N_GROUPS = 4
EXPERTS_PER_GROUP = 8
N_EXPERTS = 32
EXPERT_FF = 256
EPS = 1e-6

LANES = 128
IN_MAIN = 3 * A_WIDTH + 2 * B_KWIDTH + 2 * B_WIDTH
IN_PAD = IN_MAIN + LANES
TOK_TILE = 512
ROWS_PER_TILE = TOK_TILE // CHUNK
Q_CHUNKS = 4
Q_ROWS = Q_CHUNKS * CHUNK
ROLL_W = 1024
NEG = -1e30
ROUTE_OFF = N_GROUPS
VMEM_LIMIT = 56 * 1024 * 1024


def _params(n_axes=1):
    return pltpu.CompilerParams(dimension_semantics=("arbitrary",) * n_axes,
                                vmem_limit_bytes=VMEM_LIMIT)


def _split(a):
    hi = a.astype(BF16)
    lo = (a - hi.astype(F32)).astype(BF16)
    return hi, lo


def _dot(a, b):
    return jnp.dot(a, b, preferred_element_type=F32)


def _dot3(a, b):
    ah, al = _split(a)
    bh, bl = _split(b)
    return _dot(ah, bh) + _dot(al, bh) + _dot(ah, bl)


def _dot_nt(a, b):
    return lax.dot_general(a, b, (((1,), (1,)), ((), ())), preferred_element_type=F32)


def _dot_tn(a, b):
    return lax.dot_general(a, b, (((0,), (0,)), ((), ())), preferred_element_type=F32)


def _silu(x):
    return x / (1.0 + jnp.exp(-x))


def _rows_to_tokens(rows, n):
    r = rows.shape[0]
    return jnp.broadcast_to(rows[:, None, :], (r, CHUNK, n)).reshape(r * CHUNK, n)


def _adaln_kernel(c_ref, w_ref, b_ref, o_ref):
    a = _silu(c_ref[...])
    o_ref[...] = _dot3(a, w_ref[...]) + b_ref[...]


def _adaln(c_rows, w_ada, b_ada):
    r = c_rows.shape[0]
    n = w_ada.shape[1]
    tn = 1024
    return pl.pallas_call(
        _adaln_kernel,
        grid=(n // tn,),
        in_specs=[pl.BlockSpec((r, D_MODEL), lambda j: (0, 0)),
                  pl.BlockSpec((D_MODEL, tn), lambda j: (0, j)),
                  pl.BlockSpec((1, tn), lambda j: (0, j))],
        out_specs=pl.BlockSpec((r, tn), lambda j: (0, j)),
        out_shape=jax.ShapeDtypeStruct((r, n), F32),
        compiler_params=_params(),
        name="adaln",
    )(c_rows, w_ada, b_ada.reshape(1, n))


def _head_rms(z, bd_ref, g):
    ms = _dot((z * z).astype(BF16), bd_ref[...]) * (1.0 / A_HEAD_DIM)
    return z * lax.rsqrt(ms + EPS) * g


def _inproj_kernel(n_ptiles, xp_ref, xs_ref, mod_ref, gmix_ref, w_ref, bd_ref, gq_ref, gk_ref,
                   wgu_ref, bg_ref,
                   q_ref, k_ref, v_ref, kf_ref, vf_ref, gla_ref, la_ref):
    i = pl.program_id(0)
    x = jnp.where(i < n_ptiles, xp_ref[...], xs_ref[...])
    ms = jnp.mean(x * x, axis=-1, keepdims=True)
    xn = x * lax.rsqrt(ms + EPS) * gmix_ref[...]
    sh = _rows_to_tokens(mod_ref[:, 0:D_MODEL], D_MODEL)
    sc = _rows_to_tokens(mod_ref[:, D_MODEL:2 * D_MODEL], D_MODEL)
    hb = (xn * (1.0 + sc) + sh).astype(BF16)

    zq = _dot(hb, w_ref[:, 0:A_WIDTH])
    q_ref[...] = (_head_rms(zq, bd_ref, gq_ref[...]) * (LOG2E * A_HEAD_DIM ** -0.5)).astype(BF16)
    zk = _dot(hb, w_ref[:, A_WIDTH:2 * A_WIDTH])
    kn = _head_rms(zk, bd_ref, gk_ref[...])
    k_ref[...] = kn.astype(BF16)
    kf_ref[...] = kn
    zv = _dot(hb, w_ref[:, 2 * A_WIDTH:3 * A_WIDTH])
    v_ref[...] = zv.astype(BF16)
    vf_ref[...] = zv

    o = 3 * A_WIDTH
    zqb = _dot(hb, w_ref[:, o:o + B_KWIDTH]) * (B_DK ** -0.5)
    gla_ref[:, 0:B_KWIDTH] = zqb.astype(BF16)
    for c in range(B_KWIDTH, 2 * B_KWIDTH + 2 * B_WIDTH, 256):
        gla_ref[:, c:c + 256] = _dot(hb, w_ref[:, o + c:o + c + 256]).astype(BF16)

    gr = _dot(hb, w_ref[:, IN_MAIN:IN_PAD])
    logit = _dot(gr.astype(BF16), wgu_ref[...]) + bg_ref[...]
    log_sig = jnp.minimum(logit, 0.0) - jnp.log1p(jnp.exp(-jnp.abs(logit)))
    la_ref[...] = log_sig * (1.0 / GATE_TAU)


def _inproj(xp, xs, mod, gmix, w_in_p, bd, gq, gk, wgu_p, bg, n_ptiles, n_stiles, prep):
    n_tiles = n_ptiles + n_stiles
    t = n_tiles * TOK_TILE
    tail_tiles = 1 + n_stiles
    pblocks = prep // ROWS_PER_TILE
    const = lambda i: (0, 0)
    row = lambda i: (i, 0)
    tail = lambda i: (jnp.maximum(i - (n_ptiles - 1), 0), 0)
    return pl.pallas_call(
        functools.partial(_inproj_kernel, n_ptiles),
        grid=(n_tiles,),
        in_specs=[pl.BlockSpec((TOK_TILE, D_MODEL), lambda i: (jnp.minimum(i, n_ptiles - 1), 0)),
                  pl.BlockSpec((TOK_TILE, D_MODEL), lambda i: (jnp.maximum(i - n_ptiles, 0), 0)),
                  pl.BlockSpec((ROWS_PER_TILE, 6 * D_MODEL),
                               lambda i: (jnp.maximum(i - n_ptiles + pblocks, 0), 0)),
                  pl.BlockSpec((1, D_MODEL), const),
                  pl.BlockSpec((D_MODEL, IN_PAD), const),
                  pl.BlockSpec((A_WIDTH, A_WIDTH), const),
                  pl.BlockSpec((1, A_WIDTH), const),
                  pl.BlockSpec((1, A_WIDTH), const),
                  pl.BlockSpec((LANES, B_KWIDTH), const),
                  pl.BlockSpec((1, B_KWIDTH), const)],
        out_specs=[pl.BlockSpec((TOK_TILE, A_WIDTH), row),
                   pl.BlockSpec((TOK_TILE, A_WIDTH), row),
                   pl.BlockSpec((TOK_TILE, A_WIDTH), row),
                   pl.BlockSpec((TOK_TILE, A_WIDTH), tail),
                   pl.BlockSpec((TOK_TILE, A_WIDTH), tail),
                   pl.BlockSpec((TOK_TILE, 2 * B_KWIDTH + 2 * B_WIDTH), row),
                   pl.BlockSpec((TOK_TILE, B_KWIDTH), row)],
        out_shape=[jax.ShapeDtypeStruct((t, A_WIDTH), BF16),
                   jax.ShapeDtypeStruct((t, A_WIDTH), BF16),
                   jax.ShapeDtypeStruct((t, A_WIDTH), BF16),
                   jax.ShapeDtypeStruct((tail_tiles * TOK_TILE, A_WIDTH), F32),
                   jax.ShapeDtypeStruct((tail_tiles * TOK_TILE, A_WIDTH), F32),
                   jax.ShapeDtypeStruct((t, 2 * B_KWIDTH + 2 * B_WIDTH), BF16),
                   jax.ShapeDtypeStruct((t, B_KWIDTH), F32)],
        compiler_params=_params(),
        name="inproj",
    )(xp, xs, mod, gmix, w_in_p, bd, gq, gk, wgu_p, bg)


def _bias_lanes(n_keys):
    l = np.arange(ROLL_W)
    d = np.where(l < n_keys, BAND_PAST - l, BAND_PAST - l + ROLL_W)
    return np.clip(d, -(CHUNK - 1), MAX_REL) + (CHUNK - 1)


LOG2E = 1.4426950408889634


def _band_mask(m_rows, n_keys, first_col):
    qi = lax.broadcasted_iota(jnp.int32, (m_rows, n_keys), 0) >> LOG_CHUNK
    kw = lax.broadcasted_iota(jnp.int32, (m_rows, n_keys), 1)
    kc = kw >> LOG_CHUNK
    return (kc >= qi) & (kc <= qi + BAND_CHUNKS) & (kw >= first_col)


def _bias_tile(u_ref, h, ok):
    m_rows, n_keys = ok.shape
    src = jnp.broadcast_to(u_ref[h:h + 1, :] * LOG2E, (m_rows, ROLL_W))
    toe = pltpu.roll(src, 0, 1, stride=1, stride_axis=0)
    return jnp.where(ok, toe[:, 0:n_keys], NEG)


def _attend(q, kcat, vcat, bias_sc):
    m_rows = q.shape[0]
    first = lax.broadcasted_iota(jnp.int32, (m_rows, LANES), 1) < A_HEAD_DIM
    outs = []
    for p in range(A_HEADS // 2):
        lanes = slice(p * LANES, (p + 1) * LANES)
        qp, kp, vp = q[:, lanes], kcat[:, lanes], vcat[:, lanes]
        zero = jnp.zeros_like(qp)
        q2 = jnp.concatenate([jnp.where(first, qp, zero), jnp.where(first, zero, qp)], axis=0)
        s = _dot_nt(q2, kp) + bias_sc[p]
        e = jnp.exp2(s - jnp.max(s, axis=-1, keepdims=True))
        l = jnp.sum(e, axis=-1, keepdims=True)
        o2 = _dot(e.astype(BF16), vp) / l
        outs.append(jnp.where(first, o2[0:m_rows], o2[m_rows:2 * m_rows]))
    return jnp.concatenate(outs, axis=-1)


def _attn_prompt_kernel(u_ref, q_ref, k0_ref, k1_ref, k2_ref, v0_ref, v1_ref, v2_ref, o_ref, bias_sc):
    j = pl.program_id(0)
    n_keys = 3 * Q_ROWS

    @pl.when(j <= 2)
    def _():
        ok = _band_mask(Q_ROWS, n_keys, (2 - j) * Q_ROWS)
        for h in range(A_HEADS):
            bias_sc[h // 2, (h % 2) * Q_ROWS:(h % 2 + 1) * Q_ROWS, :] = _bias_tile(u_ref, h, ok)

    kcat = jnp.concatenate([k0_ref[...], k1_ref[...], k2_ref[...]], axis=0)
    vcat = jnp.concatenate([v0_ref[...], v1_ref[...], v2_ref[...]], axis=0)
    o_ref[...] = _attend(q_ref[...], kcat, vcat, bias_sc).astype(BF16)


def _attn_prompt(u, q, k, v, n_steps):
    const = lambda j: (0, 0)
    blk = lambda d: pl.BlockSpec((Q_ROWS, A_WIDTH), lambda j, d=d: (jnp.maximum(j - d, 0), 0))
    return pl.pallas_call(
        _attn_prompt_kernel,
        grid=(n_steps,),
        in_specs=[pl.BlockSpec((A_HEADS, ROLL_W), const),
                  blk(0), blk(2), blk(1), blk(0), blk(2), blk(1), blk(0)],
        out_specs=pl.BlockSpec((Q_ROWS, A_WIDTH), lambda j: (j, 0)),
        out_shape=jax.ShapeDtypeStruct((n_steps * Q_ROWS, A_WIDTH), BF16),
        scratch_shapes=[pltpu.VMEM((A_HEADS // 2, 2 * Q_ROWS, 3 * Q_ROWS), F32)],
        compiler_params=_params(),
        name="attn_prompt",
    )(u, q, k, k, k, v, v, v)


SAMPLE_KEYS = BAND_PAST + 2 * CHUNK


def _attn_sample_kernel(u_ref, q_ref, kn_ref, vn_ref, kc_ref, vc_ref, o_ref, bias_sc):
    @pl.when(pl.program_id(0) == 0)
    def _():
        ok = _band_mask(CHUNK, SAMPLE_KEYS, 0)
        for p in range(A_HEADS // 2):
            pair = jnp.concatenate([_bias_tile(u_ref, 2 * p, ok), _bias_tile(u_ref, 2 * p + 1, ok)], axis=0)
            bias_sc[p] = pair.T

    pad = jnp.zeros((CHUNK, A_WIDTH), BF16)
    kcat = jnp.concatenate([kc_ref[0].astype(BF16), kn_ref[...], pad], axis=0)
    vcat = jnp.concatenate([vc_ref[0].astype(BF16), vn_ref[...], pad], axis=0)
    q = q_ref[...]
    lane = lax.broadcasted_iota(jnp.int32, (CHUNK, LANES), 1)
    first = lane < A_HEAD_DIM
    zero = jnp.zeros((CHUNK, LANES), BF16)
    outs = []
    for p in range(A_HEADS // 2):
        lanes = slice(p * LANES, (p + 1) * LANES)
        qp = q[:, lanes]
        q_rows = jnp.concatenate([jnp.where(first, qp, zero), jnp.where(first, zero, qp)], axis=0)
        s = _dot_nt(kcat[:, lanes], q_rows) + bias_sc[p]
        e = jnp.exp2(s - jnp.max(s, axis=0, keepdims=True))
        pn = (e * (1.0 / jnp.sum(e, axis=0, keepdims=True))).astype(BF16)
        r = _dot_tn(pn, vcat[:, lanes])
        outs.append(jnp.where(first, r[0:CHUNK], r[CHUNK:2 * CHUNK]))
    o_ref[...] = jnp.concatenate(outs, axis=-1).astype(BF16)


def _attn_sample(u, q, k, v, kc, vc, first_chunk, n_seq):
    new = pl.BlockSpec((CHUNK, A_WIDTH), lambda b: (first_chunk + b, 0))
    cache = pl.BlockSpec((1, BAND_PAST, A_WIDTH), lambda b: (b, 0, 0))
    return pl.pallas_call(
        _attn_sample_kernel,
        grid=(n_seq,),
        in_specs=[pl.BlockSpec((A_HEADS, ROLL_W), lambda b: (0, 0)), new, new, new, cache, cache],
        out_specs=pl.BlockSpec((CHUNK, A_WIDTH), lambda b: (b, 0)),
        out_shape=jax.ShapeDtypeStruct((n_seq * CHUNK, A_WIDTH), BF16),
        scratch_shapes=[pltpu.VMEM((A_HEADS // 2, SAMPLE_KEYS, LANES), F32)],
        compiler_params=_params(),
        name="attn_sample",
    )(u, q, k, v, kc, vc)


GLA_CHUNKS = 4


def _gla_block(n_chunks, gla_ref, la_ref, ltri_ref, g_ref, st_sc, o_ref):
    rows = n_chunks * CHUNK
    la = la_ref[...]
    la_hi, la_lo = _split(la)
    b = _dot(ltri_ref[...], la_hi) + _dot(ltri_ref[...], la_lo)
    b3 = b.reshape(n_chunks, CHUNK, B_KWIDTH)
    b_mid = b3[:, CHUNK // 2 - 1:CHUNK // 2, :]
    b_last = b3[:, CHUNK - 1:CHUNK, :]
    q = gla_ref[:, 0:B_KWIDTH].astype(F32).reshape(n_chunks, CHUNK, B_KWIDTH)
    k = gla_ref[:, B_KWIDTH:2 * B_KWIDTH].astype(F32).reshape(n_chunks, CHUNK, B_KWIDTH)
    q_start = (q * jnp.exp(b3)).reshape(rows, B_KWIDTH).astype(BF16)
    q_mid = (q * jnp.exp(b3 - b_mid)).reshape(rows, B_KWIDTH).astype(BF16)
    k_mid = (k * jnp.exp(b_mid - b3)).reshape(rows, B_KWIDTH).astype(BF16)
    k_end = (k * jnp.exp(b_last - b3)).reshape(rows, B_KWIDTH).astype(BF16)
    dec = jnp.exp(b_last)

    ti = lax.broadcasted_iota(jnp.int32, (rows, rows), 0)
    si = lax.broadcasted_iota(jnp.int32, (rows, rows), 1)
    causal = (si <= ti) & ((si >> LOG_CHUNK) == (ti >> LOG_CHUNK))
    lane_r = lax.broadcasted_iota(jnp.int32, (rows, LANES), 1)
    half_r = [lane_r < B_DK, lane_r >= B_DK]
    lane_c = lax.broadcasted_iota(jnp.int32, (CHUNK, LANES), 1)
    half_c = [lane_c < B_DK, lane_c >= B_DK]
    half_s = lax.broadcasted_iota(jnp.int32, (B_DV, LANES), 1) < B_DK

    for p in range(B_HEADS // 2):
        lanes = slice(p * LANES, (p + 1) * LANES)
        qs_p, qm_p, km_p, ke_p = q_start[:, lanes], q_mid[:, lanes], k_mid[:, lanes], k_end[:, lanes]
        vs = [gla_ref[:, 2 * B_KWIDTH + (2 * p + hh) * B_DV:2 * B_KWIDTH + (2 * p + hh + 1) * B_DV]
              for hh in range(2)]
        intra = []
        for hh in range(2):
            qm = jnp.where(half_r[hh], qm_p, jnp.zeros_like(qm_p))
            sc = jnp.where(causal, _dot_nt(qm, km_p), 0.0)
            intra.append(_dot(sc.astype(BF16), vs[hh]))
        inter = [[], []]
        st = st_sc[p]
        for c in range(n_chunks):
            cr = slice(c * CHUNK, (c + 1) * CHUNK)
            st_b = st.astype(BF16)
            for hh in range(2):
                qc = jnp.where(half_c[hh], qs_p[cr], jnp.zeros((CHUNK, LANES), BF16))
                inter[hh].append(_dot_nt(qc, st_b))
            upd = jnp.where(half_s, _dot_tn(vs[0][cr], ke_p[cr]), _dot_tn(vs[1][cr], ke_p[cr]))
            st = st * dec[c, :, lanes] + upd
        st_sc[p] = st
        for hh in range(2):
            h = 2 * p + hh
            o = intra[hh] + jnp.concatenate(inter[hh], axis=0)
            ms = jnp.mean(o * o, axis=-1, keepdims=True)
            on = o * lax.rsqrt(ms + EPS) * g_ref[...]
            r = gla_ref[:, 2 * B_KWIDTH + B_WIDTH + h * B_DV:2 * B_KWIDTH + B_WIDTH + (h + 1) * B_DV]
            o_ref[:, h * B_DV:(h + 1) * B_DV] = (on * _silu(r.astype(F32))).astype(BF16)


def _gla_prompt_kernel(gla_ref, la_ref, ltri_ref, g_ref, o_ref, sfin_ref, st_sc):
    @pl.when(pl.program_id(0) == 0)
    def _():
        st_sc[...] = jnp.zeros_like(st_sc)

    _gla_block(GLA_CHUNKS, gla_ref, la_ref, ltri_ref, g_ref, st_sc, o_ref)
    sfin_ref[...] = st_sc[...]


def _gla_sample_kernel(gla_ref, la_ref, ltri_ref, g_ref, s0_ref, o_ref, sfin_ref, st_sc):
    st_sc[...] = s0_ref[0]
    _gla_block(1, gla_ref, la_ref, ltri_ref, g_ref, st_sc, o_ref)
    sfin_ref[0] = st_sc[...]


def _ltri(n_chunks):
    r = np.arange(n_chunks * CHUNK)
    m = (r[None, :] <= r[:, None]) & (r[None, :] // CHUNK == r[:, None] // CHUNK)
    return jnp.asarray(m, BF16)


_GLA_W = 2 * B_KWIDTH + 2 * B_WIDTH
_ST_SHAPE = (B_HEADS // 2, B_DV, LANES)


def _gla_prompt(gla, la, g, n_steps):
    rows = GLA_CHUNKS * CHUNK
    const = lambda j: (0, 0)
    return pl.pallas_call(
        _gla_prompt_kernel,
        grid=(n_steps,),
        in_specs=[pl.BlockSpec((rows, _GLA_W), lambda j: (j, 0)),
                  pl.BlockSpec((rows, B_KWIDTH), lambda j: (j, 0)),
                  pl.BlockSpec((rows, rows), const),
                  pl.BlockSpec((1, B_DV), const)],
        out_specs=[pl.BlockSpec((rows, B_WIDTH), lambda j: (j, 0)),
                   pl.BlockSpec(_ST_SHAPE, lambda j: (0, 0, 0))],
        out_shape=[jax.ShapeDtypeStruct((n_steps * rows, B_WIDTH), BF16),
                   jax.ShapeDtypeStruct(_ST_SHAPE, F32)],
        scratch_shapes=[pltpu.VMEM(_ST_SHAPE, F32)],
        compiler_params=_params(),
        name="gla_prompt",
    )(gla, la, _ltri(GLA_CHUNKS), g)


def _gla_sample(gla, la, g, s0, first_chunk, n_seq):
    const = lambda b: (0, 0)
    st_spec = pl.BlockSpec((1,) + _ST_SHAPE, lambda b: (b, 0, 0, 0))
    return pl.pallas_call(
        _gla_sample_kernel,
        grid=(n_seq,),
        in_specs=[pl.BlockSpec((CHUNK, _GLA_W), lambda b: (first_chunk + b, 0)),
                  pl.BlockSpec((CHUNK, B_KWIDTH), lambda b: (first_chunk + b, 0)),
                  pl.BlockSpec((CHUNK, CHUNK), const),
                  pl.BlockSpec((1, B_DV), const),
                  st_spec],
        out_specs=[pl.BlockSpec((CHUNK, B_WIDTH), lambda b: (b, 0)), st_spec],
        out_shape=[jax.ShapeDtypeStruct((n_seq * CHUNK, B_WIDTH), BF16),
                   jax.ShapeDtypeStruct((n_seq,) + _ST_SHAPE, F32)],
        scratch_shapes=[pltpu.VMEM(_ST_SHAPE, F32)],
        compiler_params=_params(),
        name="gla_sample",
    )(gla, la, _ltri(1), g, s0)


def _state_to_pairs(s):
    lead = s.shape[:-3]
    s = s.reshape(lead + (B_HEADS // 2, 2, B_DK, B_DV))
    s = jnp.moveaxis(s, -1, -3)
    return s.reshape(lead + (B_HEADS // 2, B_DV, 2 * B_DK))


def _pairs_to_state(s):
    lead = s.shape[:-3]
    s = s.reshape(lead + (B_HEADS // 2, B_DV, 2, B_DK))
    s = jnp.moveaxis(s, -3, -1)
    return s.reshape(lead + (B_HEADS, B_DK, B_DV))


def _route(logits):
    lane = lax.broadcasted_iota(jnp.int32, logits.shape, 1)
    lane_f = lane.astype(F32)
    big = float(LANES)
    gmask = lane < N_GROUPS
    gl = jnp.where(gmask, logits, NEG)
    gmax = jnp.max(gl, axis=-1, keepdims=True)
    gsel = jnp.min(jnp.where(gl == gmax, lane_f, big), axis=-1, keepdims=True)
    gsum = jnp.sum(jnp.where(gmask, jnp.exp(gl - gmax), 0.0), axis=-1, keepdims=True)
    g_w = 1.0 / gsum
    e_lo = ROUTE_OFF + gsel * EXPERTS_PER_GROUP
    emask = (lane_f >= e_lo) & (lane_f < e_lo + EXPERTS_PER_GROUP)
    el = jnp.where(emask, logits, NEG)
    v1 = jnp.max(el, axis=-1, keepdims=True)
    i1 = jnp.min(jnp.where(el == v1, lane_f, big), axis=-1, keepdims=True)
    el2 = jnp.where(lane_f == i1, NEG, el)
    v2 = jnp.max(el2, axis=-1, keepdims=True)
    i2 = jnp.min(jnp.where(el2 == v2, lane_f, big), axis=-1, keepdims=True)
    t = jnp.exp(v2 - v1)
    w1 = g_w / (1.0 + t)
    w2 = g_w * t / (1.0 + t)
    return lane_f, i1, i2, w1, w2


ROW_PIECES = D_MODEL // 2 // LANES
SUBLANES = 8
ROW_TILE = ROW_PIECES * SUBLANES


def _pack_rows(z32_sc, x, rows):
    half = D_MODEL // 2
    out = []
    for c in range(ROW_PIECES):
        z32_sc[c, pl.ds(0, rows, stride=2), :] = x[:, c * LANES:(c + 1) * LANES]
        z32_sc[c, pl.ds(1, rows, stride=2), :] = x[:, half + c * LANES:half + (c + 1) * LANES]
        out.append(z32_sc[c].astype(BF16))
    return out


def _unpack_rows(z32_sc, pieces, rows):
    lo, hi = [], []
    for c in range(ROW_PIECES):
        z32_sc[c] = pieces[c].astype(F32)
        lo.append(z32_sc[c, pl.ds(0, rows, stride=2), :])
        hi.append(z32_sc[c, pl.ds(1, rows, stride=2), :])
    return jnp.concatenate(lo, axis=1), jnp.concatenate(hi, axis=1)


def _to_row_tiled(pieces, tokens):
    per_tile = pieces[0].shape[0] * SUBLANES // tokens
    return jnp.stack([p.reshape(tokens // SUBLANES, per_tile, LANES) for p in pieces], axis=1)


def _from_row_tiled(flat, tokens):
    per_tile = flat.shape[0] // (tokens // SUBLANES) // ROW_PIECES
    tiled = flat.reshape(tokens // SUBLANES, ROW_PIECES, per_tile, LANES)
    return [tiled[:, c].reshape(tokens // SUBLANES * per_tile, LANES) for c in range(ROW_PIECES)]


def _flatten_tiled(tiled):
    return tiled.reshape(-1, LANES)


def _outproj_kernel(n_ptiles, tiles_per_sb, oap_ref, oas_ref, obp_ref, obs_ref, wo_ref, xp_ref, xs_ref, mod_ref,
                    gffn_ref, wr_ref, br_ref, ltri_ref, x1_ref, h2p_ref, meta_ref, cnt_ref, z32_sc, cnt_sc):
    i = pl.program_id(0)
    is_prompt = i < n_ptiles
    x = jnp.where(is_prompt, xp_ref[...], xs_ref[...])
    oa = jnp.where(is_prompt, oap_ref[...], oas_ref[...])
    ob = jnp.where(is_prompt, obp_ref[...], obs_ref[...])
    mix = _dot(oa, wo_ref[0:A_WIDTH, :]) + _dot(ob, wo_ref[A_WIDTH:D_MODEL, :])
    gate1 = _rows_to_tokens(mod_ref[:, 2 * D_MODEL:3 * D_MODEL], D_MODEL)
    x1 = x + gate1 * mix
    x1_ref[...] = x1
    ms = jnp.mean(x1 * x1, axis=-1, keepdims=True)
    xn = x1 * lax.rsqrt(ms + EPS) * gffn_ref[...]
    sh = _rows_to_tokens(mod_ref[:, 3 * D_MODEL:4 * D_MODEL], D_MODEL)
    sc = _rows_to_tokens(mod_ref[:, 4 * D_MODEL:5 * D_MODEL], D_MODEL)
    h2 = xn * (1.0 + sc) + sh
    words = [pltpu.bitcast(p, U32) for p in _pack_rows(z32_sc, h2, TOK_TILE)]
    h2p_ref[...] = _to_row_tiled(words, TOK_TILE)

    lane_f, i1, i2, w1, w2 = _route(_dot3(h2, wr_ref[...]) + br_ref[...])

    @pl.when(lax.rem(i, tiles_per_sb) == 0)
    def _():
        cnt_sc[...] = jnp.zeros_like(cnt_sc)

    sel = jnp.where((lane_f == i1) | (lane_f == i2), 1.0, 0.0).astype(BF16)
    before = _dot(ltri_ref[...], sel) + cnt_sc[0:1, :]
    rank1 = jnp.sum(jnp.where(lane_f == i1, before, 0.0), axis=-1, keepdims=True)
    rank2 = jnp.sum(jnp.where(lane_f == i2, before, 0.0), axis=-1, keepdims=True)
    cnt = cnt_sc[...] + _dot(jnp.ones((8, TOK_TILE), BF16), sel)
    cnt_sc[...] = cnt
    cnt_ref[0] = cnt
    cols = (i1, i2, rank1, rank2, w1, w2)
    meta = jnp.zeros_like(lane_f)
    for c, col in enumerate(cols):
        meta = jnp.where(lane_f == float(c), col, meta)
    meta_ref[...] = meta


def _outproj(oa_p, oa_s, ob_p, ob_s, w_out, xp, xs, mod, gffn, wr, br, n_ptiles, n_stiles, prep, sb):
    n_tiles = n_ptiles + n_stiles
    t = n_tiles * TOK_TILE
    pblocks = prep // ROWS_PER_TILE
    tiles_per_sb = sb // TOK_TILE
    const = lambda i: (0, 0)
    row = lambda i: (i, 0)
    prow = lambda i: (jnp.minimum(i, n_ptiles - 1), 0)
    srow = lambda i: (jnp.maximum(i - n_ptiles, 0), 0)
    r = np.arange(TOK_TILE)
    ltri = jnp.asarray(r[None, :] < r[:, None], BF16)
    return pl.pallas_call(
        functools.partial(_outproj_kernel, n_ptiles, tiles_per_sb),
        grid=(n_tiles,),
        in_specs=[pl.BlockSpec((TOK_TILE, A_WIDTH), prow),
                  pl.BlockSpec((TOK_TILE, A_WIDTH), srow),
                  pl.BlockSpec((TOK_TILE, B_WIDTH), prow),
                  pl.BlockSpec((TOK_TILE, B_WIDTH), srow),
                  pl.BlockSpec((D_MODEL, D_MODEL), const),
                  pl.BlockSpec((TOK_TILE, D_MODEL), prow),
                  pl.BlockSpec((TOK_TILE, D_MODEL), srow),
                  pl.BlockSpec((ROWS_PER_TILE, 6 * D_MODEL),
                               lambda i: (jnp.maximum(i - n_ptiles + pblocks, 0), 0)),
                  pl.BlockSpec((1, D_MODEL), const),
                  pl.BlockSpec((D_MODEL, LANES), const),
                  pl.BlockSpec((1, LANES), const),
                  pl.BlockSpec((TOK_TILE, TOK_TILE), const)],
        out_specs=[pl.BlockSpec((TOK_TILE, D_MODEL), row),
                   pl.BlockSpec((TOK_TILE // SUBLANES, ROW_PIECES, SUBLANES, LANES), lambda i: (i, 0, 0, 0)),
                   pl.BlockSpec((TOK_TILE, LANES), row),
                   pl.BlockSpec((1, 8, LANES), lambda i: (i // tiles_per_sb, 0, 0))],
        out_shape=[jax.ShapeDtypeStruct((t, D_MODEL), F32),
                   jax.ShapeDtypeStruct((t // SUBLANES, ROW_PIECES, SUBLANES, LANES), U32),
                   jax.ShapeDtypeStruct((t, LANES), F32),
                   jax.ShapeDtypeStruct((t // sb, 8, LANES), F32)],
        scratch_shapes=[pltpu.VMEM((D_MODEL // 2 // LANES, 2 * TOK_TILE, LANES), F32),
                        pltpu.VMEM((8, LANES), F32)],
        compiler_params=_params(),
        name="outproj",
    )(oa_p, oa_s, ob_p, ob_s, w_out, xp, xs, mod, gffn, wr, br, ltri)


MOE_SUPER_BLOCK = 2048
SEG_ALIGN = SUBLANES
CHUNK_BF16_ROWS = 2 * SEG_ALIGN * ROW_PIECES
SEG_BITS = 9
PAD_BITS = 5
FFN_ROWS = 512
PLAN_ROWS = LANES


def _local_rows(sb):
    return 2 * sb + N_EXPERTS * SEG_ALIGN


def _sorted_tiles(n_tokens, sb):
    rows = 2 * n_tokens + (n_tokens // sb) * N_EXPERTS * SEG_ALIGN + N_EXPERTS * FFN_ROWS
    return -(-rows // FFN_ROWS)


def _moe_plan_kernel(n_blocks, total_chunks, meta_ref, cnt_ref, ustrict_ref, lstrict_ref,
                     posw_ref, tab_ref, tile_ref):
    b = pl.program_id(0)
    per_tile = FFN_ROWS // SEG_ALIGN

    @pl.when(b == 0)
    def _():
        cnt = cnt_ref[...]
        chunks = jnp.floor((cnt + (SEG_ALIGN - 1)) * (1.0 / SEG_ALIGN))
        chunks_b = chunks.astype(BF16)
        loc = _dot(chunks_b, ustrict_ref[...])
        before = _dot(lstrict_ref[...], chunks_b)
        tot = _dot(jnp.ones((PLAN_ROWS, PLAN_ROWS), BF16), chunks_b)
        tiles = jnp.floor((tot + (per_tile - 1)) * (1.0 / per_tile))
        tile_off = _dot(tiles.astype(BF16), ustrict_ref[...])
        n_tiles = jnp.sum(tiles[0:1], axis=-1, keepdims=True)
        lane1 = lax.broadcasted_iota(jnp.int32, (PLAN_ROWS, LANES), 1)
        tail = lane1 == ROUTE_OFF + N_EXPERTS
        pad_off = jnp.where(tail, n_tiles * per_tile, tile_off * per_tile + tot)
        pad_n = jnp.where(tail, total_chunks - n_tiles * per_tile, tiles * per_tile - tot)
        row = lax.broadcasted_iota(jnp.int32, (PLAN_ROWS, LANES), 0)
        tab_ref[0] = loc
        tab_ref[1] = chunks
        tab_ref[2] = tile_off * per_tile + before
        tab_ref[3] = jnp.where(row == 0, pad_off, jnp.where(row == 1, pad_n, jnp.where(row == 2, n_tiles, 0.0)))
        t_idx = lax.broadcasted_iota(jnp.int32, tile_ref.shape, 0).astype(F32)
        lane_t = lax.broadcasted_iota(jnp.int32, tile_ref.shape, 1)
        is_expert = (lane_t >= ROUTE_OFF) & (lane_t < ROUTE_OFF + N_EXPERTS)
        ends = (tile_off + tiles)[0:1, :]
        owner = jnp.sum(jnp.where(is_expert & (ends <= t_idx), 1.0, 0.0), axis=-1, keepdims=True)
        tile_ref[...] = jnp.broadcast_to(jnp.minimum(owner, N_EXPERTS - 1.0), tile_ref.shape)

    own = jnp.floor((cnt_ref[pl.ds(b, 1), :] + (SEG_ALIGN - 1)) * (1.0 / SEG_ALIGN))
    own_off = _dot(jnp.broadcast_to(own, (SUBLANES, LANES)).astype(BF16), ustrict_ref[...]) * SEG_ALIGN
    meta = meta_ref[...]
    lane_f = lax.broadcasted_iota(jnp.int32, meta.shape, 1).astype(F32)
    off_row = own_off[0:1, :]
    pos = []
    for k in range(2):
        e_lane = meta[:, k:k + 1]
        base = jnp.sum(jnp.where(lane_f == e_lane, off_row, 0.0), axis=-1, keepdims=True)
        p = base + meta[:, 2 + k:3 + k]
        tile = jnp.floor(p * (1.0 / SUBLANES))
        pos.append(tile * (ROW_TILE - SUBLANES) + p)
    out = jnp.zeros_like(meta)
    for c, col in enumerate((pos[0], pos[1], meta[:, 4:5], meta[:, 5:6])):
        out = jnp.where(lane_f == float(c), col, out)
    posw_ref[...] = out


def _moe_plan(meta, cnt, sb):
    n_blocks = meta.shape[0] // sb
    assert n_blocks <= PLAN_ROWS and sb // SEG_ALIGN <= 256
    n_tiles = _sorted_tiles(meta.shape[0], sb)
    tile_rows = -(-n_tiles // SUBLANES) * SUBLANES
    r = np.arange(LANES)
    ustrict = jnp.asarray(r[:, None] < r[None, :], BF16)
    lstrict = jnp.asarray(r[None, :] < r[:, None], BF16)
    cnt_all = jnp.pad(cnt[:, 0, :], ((0, PLAN_ROWS - n_blocks), (0, 0)))
    const = lambda s: (0, 0)
    posw, tab, tile_owner = pl.pallas_call(
        functools.partial(_moe_plan_kernel, n_blocks, float(n_tiles * (FFN_ROWS // SEG_ALIGN))),
        grid=(n_blocks,),
        in_specs=[pl.BlockSpec((sb, LANES), lambda s: (s, 0)),
                  pl.BlockSpec((PLAN_ROWS, LANES), const),
                  pl.BlockSpec((LANES, LANES), const),
                  pl.BlockSpec((PLAN_ROWS, PLAN_ROWS), const)],
        out_specs=[pl.BlockSpec((sb, LANES), lambda s: (s, 0)),
                   pl.BlockSpec((4, PLAN_ROWS, LANES), lambda s: (0, 0, 0)),
                   pl.BlockSpec((tile_rows, LANES), const)],
        out_shape=[jax.ShapeDtypeStruct(meta.shape, F32),
                   jax.ShapeDtypeStruct((4, PLAN_ROWS, LANES), F32),
                   jax.ShapeDtypeStruct((tile_rows, LANES), F32)],
        compiler_params=_params(),
        name="moe_plan",
    )(meta, cnt_all, ustrict, lstrict)
    experts = slice(ROUTE_OFF, ROUTE_OFF + N_EXPERTS)
    to_i32 = lambda x: x.astype(jnp.int32).reshape(-1)
    plan = dict(
        loc=to_i32(tab[0, :n_blocks, experts]), n=to_i32(tab[1, :n_blocks, experts]),
        dst=to_i32(tab[2, :n_blocks, experts]),
        pad_off=to_i32(tab[3, 0, ROUTE_OFF:ROUTE_OFF + N_EXPERTS + 1]),
        pad_n=to_i32(tab[3, 1, ROUTE_OFF:ROUTE_OFF + N_EXPERTS + 1]),
        n_tiles=to_i32(tab[3, 2, 0:1]),
        owner=to_i32(tile_owner[:n_tiles, 0]))
    return posw, plan


def _token_rows(start):
    return pl.ds(start, ROW_PIECES, stride=SUBLANES)


def _pow2_copies(src_ref, dst_ref, src_chunk, dst_chunk, n, n_bits, sem, act):
    done = 0
    for k in reversed(range(n_bits)):
        take = (n >> k) & 1
        rows = CHUNK_BF16_ROWS << k
        src0 = 0 if src_chunk is None else pl.multiple_of((src_chunk + done) * CHUNK_BF16_ROWS, CHUNK_BF16_ROWS)
        dst0 = pl.multiple_of((dst_chunk + done) * CHUNK_BF16_ROWS, CHUNK_BF16_ROWS)

        @pl.when(take == 1)
        def _(src0=src0, dst0=dst0, rows=rows):
            act(pltpu.make_async_copy(src_ref.at[pl.ds(src0, rows)], dst_ref.at[pl.ds(dst0, rows)], sem))

        done = done + take * (1 << k)


def _segment_copies(block, loc_ref, n_ref, dst_ref, local_ref, global_ref, to_global, sem, act):
    def per_expert(e, carry):
        seg = block * N_EXPERTS + e
        if to_global:
            _pow2_copies(local_ref, global_ref, loc_ref[seg], dst_ref[seg], n_ref[seg], SEG_BITS, sem, act)
        else:
            _pow2_copies(global_ref, local_ref, dst_ref[seg], loc_ref[seg], n_ref[seg], SEG_BITS, sem, act)
        return carry

    lax.fori_loop(0, N_EXPERTS, per_expert, 0)


def _zero_fill(zero_ref, global_ref, padoff_ref, padn_ref, sem, act):
    full = 1 << PAD_BITS

    def per_pad(e, carry):
        def per_full(c, inner):
            dst0 = pl.multiple_of((padoff_ref[e] + c * full) * CHUNK_BF16_ROWS, CHUNK_BF16_ROWS)
            act(pltpu.make_async_copy(zero_ref, global_ref.at[pl.ds(dst0, full * CHUNK_BF16_ROWS)], sem))
            return inner

        n_full = padn_ref[e] >> PAD_BITS
        lax.fori_loop(0, n_full, per_full, 0)
        _pow2_copies(zero_ref, global_ref, None, padoff_ref[e] + n_full * full, padn_ref[e] & (full - 1),
                     PAD_BITS, sem, act)
        return carry

    lax.fori_loop(0, N_EXPERTS + 1, per_pad, 0)


STAGE_SLAB = 1024


def _restage(src_sc, dst_sc, dst_dtype):
    ratio = dst_sc.shape[0] / src_sc.shape[0]
    n_slabs = src_sc.shape[0] // (STAGE_SLAB if ratio > 1 else 2 * STAGE_SLAB)
    src_rows = src_sc.shape[0] // n_slabs
    dst_rows = dst_sc.shape[0] // n_slabs

    def slab(i, carry):
        s0 = pl.multiple_of(i * src_rows, src_rows)
        d0 = pl.multiple_of(i * dst_rows, dst_rows)
        dst_sc[pl.ds(d0, dst_rows), :] = pltpu.bitcast(src_sc[pl.ds(s0, src_rows), :], dst_dtype)
        return carry

    lax.fori_loop(0, n_slabs, slab, 0)


def _moe_dispatch_kernel(sb, loc_ref, n_ref, dst_ref, padoff_ref, padn_ref,
                         h2p_ref, a1_ref, a2_ref, xs_hbm, local_sc, stage_sc, zero_sc, sem):
    b = pl.program_id(0)
    local_sc[...] = jnp.zeros_like(local_sc)

    def step(g, carry):
        src = pl.multiple_of(g * ROW_TILE, ROW_TILE)
        for u in range(SUBLANES):
            t = g * SUBLANES + u
            row = h2p_ref[_token_rows(src + u), :]
            local_sc[_token_rows(a1_ref[t]), :] = row
            local_sc[_token_rows(a2_ref[t]), :] = row
        return carry

    lax.fori_loop(0, sb // SUBLANES, step, 0)
    _restage(local_sc, stage_sc, BF16)
    segments = functools.partial(_segment_copies, b, loc_ref, n_ref, dst_ref, stage_sc, xs_hbm, True, sem)
    segments(lambda c: c.start())

    @pl.when(b == 0)
    def _():
        zero_sc[...] = jnp.zeros_like(zero_sc)
        _zero_fill(zero_sc, xs_hbm, padoff_ref, padn_ref, sem, lambda c: c.start())
        _zero_fill(zero_sc, xs_hbm, padoff_ref, padn_ref, sem, lambda c: c.wait())

    segments(lambda c: c.wait())


def _smem_vec(n, index_map):
    return pl.BlockSpec((n,), index_map, memory_space=pltpu.SMEM)


def _moe_dispatch(h2p, a1, a2, plan, sb, n_tiles):
    n_blocks = h2p.shape[0] // (sb * ROW_PIECES)
    local_flat = _local_rows(sb) * ROW_PIECES
    vec = _smem_vec(sb, lambda s, *_: (s,))
    return pl.pallas_call(
        functools.partial(_moe_dispatch_kernel, sb),
        grid_spec=pltpu.PrefetchScalarGridSpec(
            num_scalar_prefetch=5,
            grid=(n_blocks,),
            in_specs=[pl.BlockSpec((sb * ROW_PIECES, LANES), lambda s, *_: (s, 0)), vec, vec],
            out_specs=pl.BlockSpec(memory_space=pl.ANY),
            scratch_shapes=[pltpu.VMEM((local_flat, LANES), U32),
                            pltpu.VMEM((2 * local_flat, LANES), BF16),
                            pltpu.VMEM(((1 << PAD_BITS) * CHUNK_BF16_ROWS, LANES), BF16),
                            pltpu.SemaphoreType.DMA(())]),
        out_shape=jax.ShapeDtypeStruct((n_tiles * FFN_ROWS * ROW_PIECES * 2, LANES), BF16),
        compiler_params=_params(),
        name="moe_dispatch",
    )(plan["loc"], plan["n"], plan["dst"], plan["pad_off"], plan["pad_n"], h2p, a1, a2)


def _moe_ffn_kernel(owner_ref, ntiles_ref, xs_ref, wg_ref, wu_ref, wd_ref, ys_ref,
                    wg_sc, wu_sc, wd_sc, z32_sc):
    i = pl.program_id(0)
    half = D_MODEL // 2
    used = i < ntiles_ref[0]

    @pl.when(used & ((i == 0) | (owner_ref[i] != owner_ref[jnp.maximum(i - 1, 0)])))
    def _():
        wg_sc[...] = wg_ref[0].astype(BF16)
        wu_sc[...] = wu_ref[0].astype(BF16)
        wd_sc[...] = wd_ref[0].astype(BF16)

    @pl.when(used)
    def _():
        lo, hi = _unpack_rows(z32_sc, _from_row_tiled(xs_ref[...], FFN_ROWS), FFN_ROWS)
        lo, hi = lo.astype(BF16), hi.astype(BF16)
        g = _dot(lo, wg_sc[0:half, :]) + _dot(hi, wg_sc[half:D_MODEL, :])
        u = _dot(lo, wu_sc[0:half, :]) + _dot(hi, wu_sc[half:D_MODEL, :])
        y = _dot((_silu(g) * u).astype(BF16), wd_sc[...])
        ys_ref[...] = _flatten_tiled(_to_row_tiled(_pack_rows(z32_sc, y, FFN_ROWS), FFN_ROWS))

    @pl.when(jnp.logical_not(used))
    def _():
        ys_ref[...] = jnp.zeros_like(ys_ref)


def _moe_ffn(xs, plan, wg, wu, wd):
    flat = FFN_ROWS * ROW_PIECES * 2
    n_tiles = xs.shape[0] // flat
    last_used = lambda i, owner, nt: jnp.minimum(i, nt[0] - 1)
    wspec = lambda shape: pl.BlockSpec((1,) + shape, lambda i, owner, nt: (owner[last_used(i, owner, nt)], 0, 0))
    return pl.pallas_call(
        _moe_ffn_kernel,
        grid_spec=pltpu.PrefetchScalarGridSpec(
            num_scalar_prefetch=2,
            grid=(n_tiles,),
            in_specs=[pl.BlockSpec((flat, LANES), lambda i, owner, nt: (last_used(i, owner, nt), 0)),
                      wspec((D_MODEL, EXPERT_FF)), wspec((D_MODEL, EXPERT_FF)), wspec((EXPERT_FF, D_MODEL))],
            out_specs=pl.BlockSpec((flat, LANES), lambda i, owner, nt: (i, 0)),
            scratch_shapes=[pltpu.VMEM((D_MODEL, EXPERT_FF), BF16),
                            pltpu.VMEM((D_MODEL, EXPERT_FF), BF16),
                            pltpu.VMEM((EXPERT_FF, D_MODEL), BF16),
                            pltpu.VMEM((ROW_PIECES, 2 * FFN_ROWS, LANES), F32)]),
        out_shape=jax.ShapeDtypeStruct(xs.shape, BF16),
        compiler_params=_params(),
        name="moe_ffn",
    )(plan["owner"], plan["n_tiles"], xs, wg, wu, wd)


def _moe_combine_kernel(n_psb, loc_ref, n_ref, dst_ref,
                        ys_hbm, a1_ref, a2_ref, posw_ref, x1_ref, mod_ref, yp_ref, yo_ref,
                        local_sc, stage_sc, g1_sc, g2_sc, z32_sc, sem):
    s = pl.program_id(0)

    @pl.when(pl.program_id(1) == 0)
    def _():
        segments = functools.partial(_segment_copies, s, loc_ref, n_ref, dst_ref, stage_sc, ys_hbm, False, sem)
        segments(lambda c: c.start())
        segments(lambda c: c.wait())
        _restage(stage_sc, local_sc, U32)

    def step(g, carry):
        dst = pl.multiple_of(g * ROW_TILE, ROW_TILE)
        for u in range(SUBLANES):
            t = g * SUBLANES + u
            g1_sc[_token_rows(dst + u), :] = local_sc[_token_rows(a1_ref[t]), :]
            g2_sc[_token_rows(dst + u), :] = local_sc[_token_rows(a2_ref[t]), :]
        return carry

    lax.fori_loop(0, TOK_TILE // SUBLANES, step, 0)
    halves = lambda g_sc: [pltpu.bitcast(p, BF16) for p in _from_row_tiled(g_sc[...], TOK_TILE)]
    lo1, hi1 = _unpack_rows(z32_sc, halves(g1_sc), TOK_TILE)
    lo2, hi2 = _unpack_rows(z32_sc, halves(g2_sc), TOK_TILE)
    w1, w2 = posw_ref[:, 2:3], posw_ref[:, 3:4]
    moe = jnp.concatenate([w1 * lo1 + w2 * lo2, w1 * hi1 + w2 * hi2], axis=1)
    gate2 = _rows_to_tokens(mod_ref[:, 5 * D_MODEL:6 * D_MODEL], D_MODEL)
    y = x1_ref[...] + gate2 * moe

    @pl.when(s < n_psb)
    def _():
        yp_ref[...] = y

    @pl.when(s >= n_psb)
    def _():
        yo_ref[...] = y


def _moe_combine(ys, a1, a2, posw, plan, x1, mod, sb, n_ptiles, n_stiles, prep):
    tps = sb // TOK_TILE
    n_blocks = (n_ptiles + n_stiles) // tps
    n_psb = n_ptiles // tps
    pblocks = prep // ROWS_PER_TILE
    tile = lambda s, j: s * tps + j
    vec = _smem_vec(TOK_TILE, lambda s, j, *_: (tile(s, j),))
    return pl.pallas_call(
        functools.partial(_moe_combine_kernel, n_psb),
        grid_spec=pltpu.PrefetchScalarGridSpec(
            num_scalar_prefetch=3,
            grid=(n_blocks, tps),
            in_specs=[pl.BlockSpec(memory_space=pl.ANY), vec, vec,
                      pl.BlockSpec((TOK_TILE, LANES), lambda s, j, *_: (tile(s, j), 0)),
                      pl.BlockSpec((TOK_TILE, D_MODEL), lambda s, j, *_: (tile(s, j), 0)),
                      pl.BlockSpec((ROWS_PER_TILE, 6 * D_MODEL),
                                   lambda s, j, *_: (jnp.maximum(tile(s, j) - n_ptiles + pblocks, 0), 0))],
            out_specs=[pl.BlockSpec((TOK_TILE, D_MODEL),
                                    lambda s, j, *_: (jnp.minimum(tile(s, j), n_ptiles - 1), 0)),
                       pl.BlockSpec((TOK_TILE, D_MODEL),
                                    lambda s, j, *_: (jnp.maximum(tile(s, j) - n_ptiles, 0), 0))],
            scratch_shapes=[pltpu.VMEM((_local_rows(sb) * ROW_PIECES, LANES), U32),
                            pltpu.VMEM((_local_rows(sb) * ROW_PIECES * 2, LANES), BF16),
                            pltpu.VMEM((TOK_TILE * ROW_PIECES, LANES), U32),
                            pltpu.VMEM((TOK_TILE * ROW_PIECES, LANES), U32),
                            pltpu.VMEM((ROW_PIECES, 2 * TOK_TILE, LANES), F32),
                            pltpu.SemaphoreType.DMA(())]),
        out_shape=[jax.ShapeDtypeStruct((n_ptiles * TOK_TILE, D_MODEL), F32),
                   jax.ShapeDtypeStruct((n_stiles * TOK_TILE, D_MODEL), F32)],
        compiler_params=_params(2),
        name="moe_combine",
    )(plan["loc"], plan["n"], plan["dst"], ys, a1, a2, posw, x1, mod)


def _layer(xp, xs, cache_k, cache_v, state, c_prompt, c_sample, norm_mix_g, norm_ffn_g, w_ada, b_ada, w_in,
           q_norm_g, k_norm_g, rel_bias, w_gate_up, b_gate, gla_norm_g, w_out, w_route_group,
           b_route_group, w_route_expert, b_route_expert, w_exp_gate, w_exp_up, w_exp_down):
    batch, seq, _ = xp.shape
    n_seq, dec_seq, _ = xs.shape
    assert batch == 1 and dec_seq == CHUNK and cache_k.shape[1] == BAND_PAST
    assert seq % TOK_TILE == 0 and seq >= BAND_PAST and (n_seq * CHUNK) % TOK_TILE == 0
    n_ptok, n_stok = seq, n_seq * CHUNK
    n_ptiles, n_stiles = n_ptok // TOK_TILE, n_stok // TOK_TILE
    sb = MOE_SUPER_BLOCK if (n_ptok % MOE_SUPER_BLOCK == 0 and n_stok % MOE_SUPER_BLOCK == 0) else TOK_TILE
    prep = ROWS_PER_TILE

    xp2 = xp.reshape(n_ptok, D_MODEL)
    xs2 = xs.reshape(n_stok, D_MODEL)
    c_rows = jnp.concatenate([jnp.broadcast_to(c_prompt, (prep, D_MODEL)), c_sample], axis=0)
    mod = _adaln(c_rows, w_ada, b_ada)

    w_in_p = jnp.pad(w_in, ((0, 0), (0, IN_PAD - w_in.shape[1]))).astype(BF16)
    wgu_p = jnp.pad(w_gate_up, ((0, LANES - GATE_RANK), (0, 0))).astype(BF16)
    head = np.arange(A_WIDTH) // A_HEAD_DIM
    bd = jnp.asarray(head[:, None] == head[None, :], BF16)
    gq = jnp.tile(q_norm_g, A_HEADS).reshape(1, A_WIDTH)
    gk = jnp.tile(k_norm_g, A_HEADS).reshape(1, A_WIDTH)
    q, k, v, kf, vf, gla, la = _inproj(
        xp2, xs2, mod, norm_mix_g.reshape(1, D_MODEL), w_in_p, bd, gq, gk, wgu_p,
        b_gate.reshape(1, B_KWIDTH), n_ptiles, n_stiles, prep)

    first_chunk = n_ptok // CHUNK
    oa_p = _attn_prompt(rel_bias[:, _bias_lanes(3 * Q_ROWS)], q, k, v, n_ptok // Q_ROWS)
    oa_s = _attn_sample(rel_bias[:, _bias_lanes(SAMPLE_KEYS)], q, k, v,
                        cache_k.reshape(n_seq, BAND_PAST, A_WIDTH), cache_v.reshape(n_seq, BAND_PAST, A_WIDTH),
                        first_chunk, n_seq)
    g_gla = gla_norm_g.reshape(1, B_DV)
    ob_p, sfin_p = _gla_prompt(gla, la, g_gla, n_ptok // (GLA_CHUNKS * CHUNK))
    ob_s, sfin_s = _gla_sample(gla, la, g_gla, _state_to_pairs(state), first_chunk, n_seq)

    wr = jnp.pad(jnp.concatenate([w_route_group, w_route_expert], axis=1),
                 ((0, 0), (0, LANES - N_GROUPS - N_EXPERTS)))
    br = jnp.pad(jnp.concatenate([b_route_group, b_route_expert]), (0, LANES - N_GROUPS - N_EXPERTS))
    x1, h2p, meta, cnt = _outproj(oa_p, oa_s, ob_p, ob_s, w_out.astype(BF16), xp2, xs2, mod,
                                  norm_ffn_g.reshape(1, D_MODEL), wr, br.reshape(1, LANES),
                                  n_ptiles, n_stiles, prep, sb)

    posw, plan = _moe_plan(meta, cnt, sb)
    a1, a2 = posw[:, 0].astype(jnp.int32), posw[:, 1].astype(jnp.int32)
    xs_sorted = _moe_dispatch(h2p.reshape(-1, LANES), a1, a2, plan, sb, _sorted_tiles(n_ptok + n_stok, sb))
    ys_sorted = _moe_ffn(xs_sorted, plan, w_exp_gate, w_exp_up, w_exp_down)
    yp, ys = _moe_combine(ys_sorted, a1, a2, posw, plan, x1, mod, sb, n_ptiles, n_stiles, prep)

    tail = min(BAND_PAST, seq)
    heads = (A_HEADS, A_HEAD_DIM)
    return (yp.reshape(1, seq, D_MODEL), ys.reshape(n_seq, CHUNK, D_MODEL),
            kf[TOK_TILE - tail:TOK_TILE].reshape((1, tail) + heads),
            vf[TOK_TILE - tail:TOK_TILE].reshape((1, tail) + heads),
            _pairs_to_state(sfin_p)[None],
            kf[TOK_TILE:].reshape((n_seq, CHUNK) + heads),
            vf[TOK_TILE:].reshape((n_seq, CHUNK) + heads),
            _pairs_to_state(sfin_s))


def kernel(x_prompt, x_sample, cache_a_k, cache_a_v, state_gla, c_prompt, c_sample, norm_mix_g, norm_ffn_g,
           w_ada, b_ada, w_in, q_norm_g, k_norm_g, rel_bias, w_gate_up, b_gate, gla_norm_g, w_out,
           w_route_group, b_route_group, w_route_expert, b_route_expert, w_exp_gate, w_exp_up, w_exp_down):
    depth = w_in.shape[0]
    yp, ys = x_prompt, x_sample
    outs = [[] for _ in range(6)]
    for l in range(depth):
        yp, ys, kp, vp, sp, ks, vs, ss = _layer(
            yp, ys, cache_a_k[l], cache_a_v[l], state_gla[l], c_prompt, c_sample, norm_mix_g[l], norm_ffn_g[l],
            w_ada[l], b_ada[l], w_in[l], q_norm_g[l], k_norm_g[l], rel_bias[l], w_gate_up[l], b_gate[l],
            gla_norm_g[l], w_out[l], w_route_group[l], b_route_group[l], w_route_expert[l], b_route_expert[l],
            w_exp_gate[l], w_exp_up[l], w_exp_down[l])
        for lst, val in zip(outs, (kp, vp, sp, ks, vs, ss)):
            lst.append(val)
    return (yp, ys) + tuple(jnp.stack(o) for o in outs)
```

```python
import functools

import numpy as np
import jax
import jax.numpy as jnp
from jax import lax
from jax.experimental import pallas as pl
from jax.experimental.pallas import tpu as pltpu

F32 = jnp.float32
BF16 = jnp.bfloat16
U32 = jnp.uint32

D_MODEL = 1024
CHUNK = 64
LOG_CHUNK = 6
BAND_CHUNKS = 8
BAND_PAST = BAND_CHUNKS * CHUNK
A_WIDTH = 512
A_HEADS = 8
A_HEAD_DIM = 64
MAX_REL = 128
N_REL = CHUNK + MAX_REL
B_WIDTH = 512
B_HEADS = 4
B_DV = 128
B_DK = 64
B_KWIDTH = 256
GATE_RANK = 16
GATE_TAU = 16.0
N_GROUPS = 4
EXPERTS_PER_GROUP = 8
N_EXPERTS = 32
EXPERT_FF = 256
EPS = 1e-6

LANES = 128
IN_MAIN = 3 * A_WIDTH + 2 * B_KWIDTH + 2 * B_WIDTH
IN_PAD = IN_MAIN + LANES
TOK_TILE = 512
ROWS_PER_TILE = TOK_TILE // CHUNK
Q_CHUNKS = 4
Q_ROWS = Q_CHUNKS * CHUNK
ROLL_W = 1024
NEG = -1e30
ROUTE_OFF = N_GROUPS
VMEM_LIMIT = 56 * 1024 * 1024


def _params(n_axes=1):
    return pltpu.CompilerParams(dimension_semantics=("arbitrary",) * n_axes,
                                vmem_limit_bytes=VMEM_LIMIT)


def _split(a):
    hi = a.astype(BF16)
    lo = (a - hi.astype(F32)).astype(BF16)
    return hi, lo


def _dot(a, b):
    return jnp.dot(a, b, preferred_element_type=F32)


def _dot3(a, b):
    ah, al = _split(a)
    bh, bl = _split(b)
    return _dot(ah, bh) + _dot(al, bh) + _dot(ah, bl)


def _dot_nt(a, b):
    return lax.dot_general(a, b, (((1,), (1,)), ((), ())), preferred_element_type=F32)


def _dot_tn(a, b):
    return lax.dot_general(a, b, (((0,), (0,)), ((), ())), preferred_element_type=F32)


def _silu(x):
    return x / (1.0 + jnp.exp(-x))


def _rows_to_tokens(rows, n):
    r = rows.shape[0]
    return jnp.broadcast_to(rows[:, None, :], (r, CHUNK, n)).reshape(r * CHUNK, n)


def _adaln_kernel(c_ref, w_ref, b_ref, o_ref):
    a = _silu(c_ref[...])
    o_ref[...] = _dot3(a, w_ref[...]) + b_ref[...]


def _adaln(c_rows, w_ada, b_ada):
    r = c_rows.shape[0]
    n = w_ada.shape[1]
    tn = 1024
    return pl.pallas_call(
        _adaln_kernel,
        grid=(n // tn,),
        in_specs=[pl.BlockSpec((r, D_MODEL), lambda j: (0, 0)),
                  pl.BlockSpec((D_MODEL, tn), lambda j: (0, j)),
                  pl.BlockSpec((1, tn), lambda j: (0, j))],
        out_specs=pl.BlockSpec((r, tn), lambda j: (0, j)),
        out_shape=jax.ShapeDtypeStruct((r, n), F32),
        compiler_params=_params(),
        name="adaln",
    )(c_rows, w_ada, b_ada.reshape(1, n))


def _head_rms(z, bd_ref, g):
    ms = _dot((z * z).astype(BF16), bd_ref[...]) * (1.0 / A_HEAD_DIM)
    return z * lax.rsqrt(ms + EPS) * g


def _inproj_kernel(n_ptiles, xp_ref, xs_ref, mod_ref, gmix_ref, w_ref, bd_ref, gq_ref, gk_ref,
                   wgu_ref, bg_ref,
                   q_ref, k_ref, v_ref, kf_ref, vf_ref, gla_ref, la_ref):
    i = pl.program_id(0)
    x = jnp.where(i < n_ptiles, xp_ref[...], xs_ref[...])
    ms = jnp.mean(x * x, axis=-1, keepdims=True)
    xn = x * lax.rsqrt(ms + EPS) * gmix_ref[...]
    sh = _rows_to_tokens(mod_ref[:, 0:D_MODEL], D_MODEL)
    sc = _rows_to_tokens(mod_ref[:, D_MODEL:2 * D_MODEL], D_MODEL)
    hb = (xn * (1.0 + sc) + sh).astype(BF16)

    zq = _dot(hb, w_ref[:, 0:A_WIDTH])
    q_ref[...] = (_head_rms(zq, bd_ref, gq_ref[...]) * (LOG2E * A_HEAD_DIM ** -0.5)).astype(BF16)
    zk = _dot(hb, w_ref[:, A_WIDTH:2 * A_WIDTH])
    kn = _head_rms(zk, bd_ref, gk_ref[...])
    k_ref[...] = kn.astype(BF16)
    kf_ref[...] = kn
    zv = _dot(hb, w_ref[:, 2 * A_WIDTH:3 * A_WIDTH])
    v_ref[...] = zv.astype(BF16)
    vf_ref[...] = zv

    o = 3 * A_WIDTH
    zqb = _dot(hb, w_ref[:, o:o + B_KWIDTH]) * (B_DK ** -0.5)
    gla_ref[:, 0:B_KWIDTH] = zqb.astype(BF16)
    for c in range(B_KWIDTH, 2 * B_KWIDTH + 2 * B_WIDTH, 256):
        gla_ref[:, c:c + 256] = _dot(hb, w_ref[:, o + c:o + c + 256]).astype(BF16)

    gr = _dot(hb, w_ref[:, IN_MAIN:IN_PAD])
    logit = _dot(gr.astype(BF16), wgu_ref[...]) + bg_ref[...]
    log_sig = jnp.minimum(logit, 0.0) - jnp.log1p(jnp.exp(-jnp.abs(logit)))
    la_ref[...] = log_sig * (1.0 / GATE_TAU)


def _inproj(xp, xs, mod, gmix, w_in_p, bd, gq, gk, wgu_p, bg, n_ptiles, n_stiles, prep):
    n_tiles = n_ptiles + n_stiles
    t = n_tiles * TOK_TILE
    tail_tiles = 1 + n_stiles
    pblocks = prep // ROWS_PER_TILE
    const = lambda i: (0, 0)
    row = lambda i: (i, 0)
    tail = lambda i: (jnp.maximum(i - (n_ptiles - 1), 0), 0)
    return pl.pallas_call(
        functools.partial(_inproj_kernel, n_ptiles),
        grid=(n_tiles,),
        in_specs=[pl.BlockSpec((TOK_TILE, D_MODEL), lambda i: (jnp.minimum(i, n_ptiles - 1), 0)),
                  pl.BlockSpec((TOK_TILE, D_MODEL), lambda i: (jnp.maximum(i - n_ptiles, 0), 0)),
                  pl.BlockSpec((ROWS_PER_TILE, 6 * D_MODEL),
                               lambda i: (jnp.maximum(i - n_ptiles + pblocks, 0), 0)),
                  pl.BlockSpec((1, D_MODEL), const),
                  pl.BlockSpec((D_MODEL, IN_PAD), const),
                  pl.BlockSpec((A_WIDTH, A_WIDTH), const),
                  pl.BlockSpec((1, A_WIDTH), const),
                  pl.BlockSpec((1, A_WIDTH), const),
                  pl.BlockSpec((LANES, B_KWIDTH), const),
                  pl.BlockSpec((1, B_KWIDTH), const)],
        out_specs=[pl.BlockSpec((TOK_TILE, A_WIDTH), row),
                   pl.BlockSpec((TOK_TILE, A_WIDTH), row),
                   pl.BlockSpec((TOK_TILE, A_WIDTH), row),
                   pl.BlockSpec((TOK_TILE, A_WIDTH), tail),
                   pl.BlockSpec((TOK_TILE, A_WIDTH), tail),
                   pl.BlockSpec((TOK_TILE, 2 * B_KWIDTH + 2 * B_WIDTH), row),
                   pl.BlockSpec((TOK_TILE, B_KWIDTH), row)],
        out_shape=[jax.ShapeDtypeStruct((t, A_WIDTH), BF16),
                   jax.ShapeDtypeStruct((t, A_WIDTH), BF16),
                   jax.ShapeDtypeStruct((t, A_WIDTH), BF16),
                   jax.ShapeDtypeStruct((tail_tiles * TOK_TILE, A_WIDTH), F32),
                   jax.ShapeDtypeStruct((tail_tiles * TOK_TILE, A_WIDTH), F32),
                   jax.ShapeDtypeStruct((t, 2 * B_KWIDTH + 2 * B_WIDTH), BF16),
                   jax.ShapeDtypeStruct((t, B_KWIDTH), F32)],
        compiler_params=_params(),
        name="inproj",
    )(xp, xs, mod, gmix, w_in_p, bd, gq, gk, wgu_p, bg)


def _bias_lanes(n_keys):
    l = np.arange(ROLL_W)
    d = np.where(l < n_keys, BAND_PAST - l, BAND_PAST - l + ROLL_W)
    return np.clip(d, -(CHUNK - 1), MAX_REL) + (CHUNK - 1)


LOG2E = 1.4426950408889634


def _band_mask(m_rows, n_keys, first_col):
    qi = lax.broadcasted_iota(jnp.int32, (m_rows, n_keys), 0) >> LOG_CHUNK
    kw = lax.broadcasted_iota(jnp.int32, (m_rows, n_keys), 1)
    kc = kw >> LOG_CHUNK
    return (kc >= qi) & (kc <= qi + BAND_CHUNKS) & (kw >= first_col)


def _bias_tile(u_ref, h, ok):
    m_rows, n_keys = ok.shape
    src = jnp.broadcast_to(u_ref[h:h + 1, :] * LOG2E, (m_rows, ROLL_W))
    toe = pltpu.roll(src, 0, 1, stride=1, stride_axis=0)
    return jnp.where(ok, toe[:, 0:n_keys], NEG)


def _attend(q, kcat, vcat, bias_sc):
    m_rows = q.shape[0]
    first = lax.broadcasted_iota(jnp.int32, (m_rows, LANES), 1) < A_HEAD_DIM
    outs = []
    for p in range(A_HEADS // 2):
        lanes = slice(p * LANES, (p + 1) * LANES)
        qp, kp, vp = q[:, lanes], kcat[:, lanes], vcat[:, lanes]
        zero = jnp.zeros_like(qp)
        q2 = jnp.concatenate([jnp.where(first, qp, zero), jnp.where(first, zero, qp)], axis=0)
        s = _dot_nt(q2, kp) + bias_sc[p]
        e = jnp.exp2(s - jnp.max(s, axis=-1, keepdims=True))
        l = jnp.sum(e, axis=-1, keepdims=True)
        o2 = _dot(e.astype(BF16), vp) / l
        outs.append(jnp.where(first, o2[0:m_rows], o2[m_rows:2 * m_rows]))
    return jnp.concatenate(outs, axis=-1)


def _attn_prompt_kernel(u_ref, q_ref, k0_ref, k1_ref, k2_ref, v0_ref, v1_ref, v2_ref, o_ref, bias_sc):
    j = pl.program_id(0)
    n_keys = 3 * Q_ROWS

    @pl.when(j <= 2)
    def _():
        ok = _band_mask(Q_ROWS, n_keys, (2 - j) * Q_ROWS)
        for h in range(A_HEADS):
            bias_sc[h // 2, (h % 2) * Q_ROWS:(h % 2 + 1) * Q_ROWS, :] = _bias_tile(u_ref, h, ok)

    kcat = jnp.concatenate([k0_ref[...], k1_ref[...], k2_ref[...]], axis=0)
    vcat = jnp.concatenate([v0_ref[...], v1_ref[...], v2_ref[...]], axis=0)
    o_ref[...] = _attend(q_ref[...], kcat, vcat, bias_sc).astype(BF16)


def _attn_prompt(u, q, k, v, n_steps):
    const = lambda j: (0, 0)
    blk = lambda d: pl.BlockSpec((Q_ROWS, A_WIDTH), lambda j, d=d: (jnp.maximum(j - d, 0), 0))
    return pl.pallas_call(
        _attn_prompt_kernel,
        grid=(n_steps,),
        in_specs=[pl.BlockSpec((A_HEADS, ROLL_W), const),
                  blk(0), blk(2), blk(1), blk(0), blk(2), blk(1), blk(0)],
        out_specs=pl.BlockSpec((Q_ROWS, A_WIDTH), lambda j: (j, 0)),
        out_shape=jax.ShapeDtypeStruct((n_steps * Q_ROWS, A_WIDTH), BF16),
        scratch_shapes=[pltpu.VMEM((A_HEADS // 2, 2 * Q_ROWS, 3 * Q_ROWS), F32)],
        compiler_params=_params(),
        name="attn_prompt",
    )(u, q, k, k, k, v, v, v)


SAMPLE_KEYS = BAND_PAST + 2 * CHUNK


def _attn_sample_kernel(u_ref, q_ref, kn_ref, vn_ref, kc_ref, vc_ref, o_ref, bias_sc):
    @pl.when(pl.program_id(0) == 0)
    def _():
        ok = _band_mask(CHUNK, SAMPLE_KEYS, 0)
        for p in range(A_HEADS // 2):
            pair = jnp.concatenate([_bias_tile(u_ref, 2 * p, ok), _bias_tile(u_ref, 2 * p + 1, ok)], axis=0)
            bias_sc[p] = pair.T

    pad = jnp.zeros((CHUNK, A_WIDTH), BF16)
    kcat = jnp.concatenate([kc_ref[0].astype(BF16), kn_ref[...], pad], axis=0)
    vcat = jnp.concatenate([vc_ref[0].astype(BF16), vn_ref[...], pad], axis=0)
    q = q_ref[...]
    lane = lax.broadcasted_iota(jnp.int32, (CHUNK, LANES), 1)
    first = lane < A_HEAD_DIM
    zero = jnp.zeros((CHUNK, LANES), BF16)
    outs = []
    for p in range(A_HEADS // 2):
        lanes = slice(p * LANES, (p + 1) * LANES)
        qp = q[:, lanes]
        q_rows = jnp.concatenate([jnp.where(first, qp, zero), jnp.where(first, zero, qp)], axis=0)
        s = _dot_nt(kcat[:, lanes], q_rows) + bias_sc[p]
        e = jnp.exp2(s - jnp.max(s, axis=0, keepdims=True))
        pn = (e * (1.0 / jnp.sum(e, axis=0, keepdims=True))).astype(BF16)
        r = _dot_tn(pn, vcat[:, lanes])
        outs.append(jnp.where(first, r[0:CHUNK], r[CHUNK:2 * CHUNK]))
    o_ref[...] = jnp.concatenate(outs, axis=-1).astype(BF16)


def _attn_sample(u, q, k, v, kc, vc, first_chunk, n_seq):
    new = pl.BlockSpec((CHUNK, A_WIDTH), lambda b: (first_chunk + b, 0))
    cache = pl.BlockSpec((1, BAND_PAST, A_WIDTH), lambda b: (b, 0, 0))
    return pl.pallas_call(
        _attn_sample_kernel,
        grid=(n_seq,),
        in_specs=[pl.BlockSpec((A_HEADS, ROLL_W), lambda b: (0, 0)), new, new, new, cache, cache],
        out_specs=pl.BlockSpec((CHUNK, A_WIDTH), lambda b: (b, 0)),
        out_shape=jax.ShapeDtypeStruct((n_seq * CHUNK, A_WIDTH), BF16),
        scratch_shapes=[pltpu.VMEM((A_HEADS // 2, SAMPLE_KEYS, LANES), F32)],
        compiler_params=_params(),
        name="attn_sample",
    )(u, q, k, v, kc, vc)


GLA_CHUNKS = 4


def _gla_block(n_chunks, gla_ref, la_ref, ltri_ref, g_ref, st_sc, o_ref):
    rows = n_chunks * CHUNK
    la = la_ref[...]
    la_hi, la_lo = _split(la)
    b = _dot(ltri_ref[...], la_hi) + _dot(ltri_ref[...], la_lo)
    b3 = b.reshape(n_chunks, CHUNK, B_KWIDTH)
    b_mid = b3[:, CHUNK // 2 - 1:CHUNK // 2, :]
    b_last = b3[:, CHUNK - 1:CHUNK, :]
    q = gla_ref[:, 0:B_KWIDTH].astype(F32).reshape(n_chunks, CHUNK, B_KWIDTH)
    k = gla_ref[:, B_KWIDTH:2 * B_KWIDTH].astype(F32).reshape(n_chunks, CHUNK, B_KWIDTH)
    q_start = (q * jnp.exp(b3)).reshape(rows, B_KWIDTH).astype(BF16)
    q_mid = (q * jnp.exp(b3 - b_mid)).reshape(rows, B_KWIDTH).astype(BF16)
    k_mid = (k * jnp.exp(b_mid - b3)).reshape(rows, B_KWIDTH).astype(BF16)
    k_end = (k * jnp.exp(b_last - b3)).reshape(rows, B_KWIDTH).astype(BF16)
    dec = jnp.exp(b_last)

    ti = lax.broadcasted_iota(jnp.int32, (2 * rows, rows), 0) & (rows - 1)
    si = lax.broadcasted_iota(jnp.int32, (2 * rows, rows), 1)
    causal = (si <= ti) & ((si >> LOG_CHUNK) == (ti >> LOG_CHUNK))
    first_r = lax.broadcasted_iota(jnp.int32, (rows, LANES), 1) < B_DK
    first_c = lax.broadcasted_iota(jnp.int32, (CHUNK, LANES), 1) < B_DK
    first_s = lax.broadcasted_iota(jnp.int32, (B_DV, LANES), 1) < B_DK

    def stack_heads(x, first):
        zero = jnp.zeros_like(x)
        return jnp.concatenate([jnp.where(first, x, zero), jnp.where(first, zero, x)], axis=0)

    for p in range(B_HEADS // 2):
        lanes = slice(p * LANES, (p + 1) * LANES)
        qs_p, qm_p, km_p, ke_p = q_start[:, lanes], q_mid[:, lanes], k_mid[:, lanes], k_end[:, lanes]
        v_pair = gla_ref[:, 2 * B_KWIDTH + 2 * p * B_DV:2 * B_KWIDTH + (2 * p + 2) * B_DV]
        sc = jnp.where(causal, _dot_nt(stack_heads(qm_p, first_r), km_p), 0.0)
        o2 = _dot(sc.astype(BF16), v_pair)
        intra = [o2[0:rows, 0:B_DV], o2[rows:2 * rows, B_DV:2 * B_DV]]
        inter = [[], []]
        st = st_sc[p]
        for c in range(n_chunks):
            cr = slice(c * CHUNK, (c + 1) * CHUNK)
            r2 = _dot_nt(stack_heads(qs_p[cr], first_c), st.astype(BF16))
            inter[0].append(r2[0:CHUNK])
            inter[1].append(r2[CHUNK:2 * CHUNK])
            u2 = _dot_tn(v_pair[cr], ke_p[cr])
            st = st * dec[c, :, lanes] + jnp.where(first_s, u2[0:B_DV], u2[B_DV:2 * B_DV])
        st_sc[p] = st
        for hh in range(2):
            h = 2 * p + hh
            o = intra[hh] + jnp.concatenate(inter[hh], axis=0)
            ms = jnp.mean(o * o, axis=-1, keepdims=True)
            on = o * lax.rsqrt(ms + EPS) * g_ref[...]
            r = gla_ref[:, 2 * B_KWIDTH + B_WIDTH + h * B_DV:2 * B_KWIDTH + B_WIDTH + (h + 1) * B_DV]
            o_ref[:, h * B_DV:(h + 1) * B_DV] = (on * _silu(r.astype(F32))).astype(BF16)


def _gla_prompt_kernel(gla_ref, la_ref, ltri_ref, g_ref, o_ref, sfin_ref, st_sc):
    @pl.when(pl.program_id(0) == 0)
    def _():
        st_sc[...] = jnp.zeros_like(st_sc)

    _gla_block(GLA_CHUNKS, gla_ref, la_ref, ltri_ref, g_ref, st_sc, o_ref)
    sfin_ref[...] = st_sc[...]


def _gla_sample_kernel(gla_ref, la_ref, ltri_ref, g_ref, s0_ref, o_ref, sfin_ref, st_sc):
    st_sc[...] = s0_ref[0]
    _gla_block(1, gla_ref, la_ref, ltri_ref, g_ref, st_sc, o_ref)
    sfin_ref[0] = st_sc[...]


def _ltri(n_chunks):
    r = np.arange(n_chunks * CHUNK)
    m = (r[None, :] <= r[:, None]) & (r[None, :] // CHUNK == r[:, None] // CHUNK)
    return jnp.asarray(m, BF16)


_GLA_W = 2 * B_KWIDTH + 2 * B_WIDTH
_ST_SHAPE = (B_HEADS // 2, B_DV, LANES)


def _gla_prompt(gla, la, g, n_steps):
    rows = GLA_CHUNKS * CHUNK
    const = lambda j: (0, 0)
    return pl.pallas_call(
        _gla_prompt_kernel,
        grid=(n_steps,),
        in_specs=[pl.BlockSpec((rows, _GLA_W), lambda j: (j, 0)),
                  pl.BlockSpec((rows, B_KWIDTH), lambda j: (j, 0)),
                  pl.BlockSpec((rows, rows), const),
                  pl.BlockSpec((1, B_DV), const)],
        out_specs=[pl.BlockSpec((rows, B_WIDTH), lambda j: (j, 0)),
                   pl.BlockSpec(_ST_SHAPE, lambda j: (0, 0, 0))],
        out_shape=[jax.ShapeDtypeStruct((n_steps * rows, B_WIDTH), BF16),
                   jax.ShapeDtypeStruct(_ST_SHAPE, F32)],
        scratch_shapes=[pltpu.VMEM(_ST_SHAPE, F32)],
        compiler_params=_params(),
        name="gla_prompt",
    )(gla, la, _ltri(GLA_CHUNKS), g)


def _gla_sample(gla, la, g, s0, first_chunk, n_seq):
    const = lambda b: (0, 0)
    st_spec = pl.BlockSpec((1,) + _ST_SHAPE, lambda b: (b, 0, 0, 0))
    return pl.pallas_call(
        _gla_sample_kernel,
        grid=(n_seq,),
        in_specs=[pl.BlockSpec((CHUNK, _GLA_W), lambda b: (first_chunk + b, 0)),
                  pl.BlockSpec((CHUNK, B_KWIDTH), lambda b: (first_chunk + b, 0)),
                  pl.BlockSpec((CHUNK, CHUNK), const),
                  pl.BlockSpec((1, B_DV), const),
                  st_spec],
        out_specs=[pl.BlockSpec((CHUNK, B_WIDTH), lambda b: (b, 0)), st_spec],
        out_shape=[jax.ShapeDtypeStruct((n_seq * CHUNK, B_WIDTH), BF16),
                   jax.ShapeDtypeStruct((n_seq,) + _ST_SHAPE, F32)],
        scratch_shapes=[pltpu.VMEM(_ST_SHAPE, F32)],
        compiler_params=_params(),
        name="gla_sample",
    )(gla, la, _ltri(1), g, s0)


def _state_to_pairs(s):
    lead = s.shape[:-3]
    s = s.reshape(lead + (B_HEADS // 2, 2, B_DK, B_DV))
    s = jnp.moveaxis(s, -1, -3)
    return s.reshape(lead + (B_HEADS // 2, B_DV, 2 * B_DK))


def _pairs_to_state(s):
    lead = s.shape[:-3]
    s = s.reshape(lead + (B_HEADS // 2, B_DV, 2, B_DK))
    s = jnp.moveaxis(s, -3, -1)
    return s.reshape(lead + (B_HEADS, B_DK, B_DV))


def _route(logits):
    lane = lax.broadcasted_iota(jnp.int32, logits.shape, 1)
    lane_f = lane.astype(F32)
    big = float(LANES)
    gmask = lane < N_GROUPS
    gl = jnp.where(gmask, logits, NEG)
    gmax = jnp.max(gl, axis=-1, keepdims=True)
    gsel = jnp.min(jnp.where(gl == gmax, lane_f, big), axis=-1, keepdims=True)
    gsum = jnp.sum(jnp.where(gmask, jnp.exp(gl - gmax), 0.0), axis=-1, keepdims=True)
    g_w = 1.0 / gsum
    e_lo = ROUTE_OFF + gsel * EXPERTS_PER_GROUP
    emask = (lane_f >= e_lo) & (lane_f < e_lo + EXPERTS_PER_GROUP)
    el = jnp.where(emask, logits, NEG)
    v1 = jnp.max(el, axis=-1, keepdims=True)
    i1 = jnp.min(jnp.where(el == v1, lane_f, big), axis=-1, keepdims=True)
    el2 = jnp.where(lane_f == i1, NEG, el)
    v2 = jnp.max(el2, axis=-1, keepdims=True)
    i2 = jnp.min(jnp.where(el2 == v2, lane_f, big), axis=-1, keepdims=True)
    t = jnp.exp(v2 - v1)
    w1 = g_w / (1.0 + t)
    w2 = g_w * t / (1.0 + t)
    return lane_f, i1, i2, w1, w2


ROW_PIECES = D_MODEL // 2 // LANES
SUBLANES = 8
ROW_TILE = ROW_PIECES * SUBLANES


def _pack_rows(z32_sc, x, rows):
    half = D_MODEL // 2
    out = []
    for c in range(ROW_PIECES):
        z32_sc[c, pl.ds(0, rows, stride=2), :] = x[:, c * LANES:(c + 1) * LANES]
        z32_sc[c, pl.ds(1, rows, stride=2), :] = x[:, half + c * LANES:half + (c + 1) * LANES]
        out.append(z32_sc[c].astype(BF16))
    return out


def _unpack_rows(z32_sc, pieces, rows):
    lo, hi = [], []
    for c in range(ROW_PIECES):
        z32_sc[c] = pieces[c].astype(F32)
        lo.append(z32_sc[c, pl.ds(0, rows, stride=2), :])
        hi.append(z32_sc[c, pl.ds(1, rows, stride=2), :])
    return jnp.concatenate(lo, axis=1), jnp.concatenate(hi, axis=1)


def _to_row_tiled(pieces, tokens):
    per_tile = pieces[0].shape[0] * SUBLANES // tokens
    return jnp.stack([p.reshape(tokens // SUBLANES, per_tile, LANES) for p in pieces], axis=1)


def _from_row_tiled(flat, tokens):
    per_tile = flat.shape[0] // (tokens // SUBLANES) // ROW_PIECES
    tiled = flat.reshape(tokens // SUBLANES, ROW_PIECES, per_tile, LANES)
    return [tiled[:, c].reshape(tokens // SUBLANES * per_tile, LANES) for c in range(ROW_PIECES)]


def _flatten_tiled(tiled):
    return tiled.reshape(-1, LANES)


def _outproj_kernel(n_ptiles, tiles_per_sb, oap_ref, oas_ref, obp_ref, obs_ref, wo_ref, xp_ref, xs_ref, mod_ref,
                    gffn_ref, wr_ref, br_ref, ltri_ref, x1_ref, h2p_ref, meta_ref, cnt_ref, z32_sc, cnt_sc):
    i = pl.program_id(0)
    is_prompt = i < n_ptiles
    x = jnp.where(is_prompt, xp_ref[...], xs_ref[...])
    oa = jnp.where(is_prompt, oap_ref[...], oas_ref[...])
    ob = jnp.where(is_prompt, obp_ref[...], obs_ref[...])
    mix = _dot(oa, wo_ref[0:A_WIDTH, :]) + _dot(ob, wo_ref[A_WIDTH:D_MODEL, :])
    gate1 = _rows_to_tokens(mod_ref[:, 2 * D_MODEL:3 * D_MODEL], D_MODEL)
    x1 = x + gate1 * mix
    x1_ref[...] = x1
    ms = jnp.mean(x1 * x1, axis=-1, keepdims=True)
    xn = x1 * lax.rsqrt(ms + EPS) * gffn_ref[...]
    sh = _rows_to_tokens(mod_ref[:, 3 * D_MODEL:4 * D_MODEL], D_MODEL)
    sc = _rows_to_tokens(mod_ref[:, 4 * D_MODEL:5 * D_MODEL], D_MODEL)
    h2 = xn * (1.0 + sc) + sh
    words = [pltpu.bitcast(p, U32) for p in _pack_rows(z32_sc, h2, TOK_TILE)]
    h2p_ref[...] = _to_row_tiled(words, TOK_TILE)

    lane_f, i1, i2, w1, w2 = _route(_dot3(h2, wr_ref[...]) + br_ref[...])

    @pl.when(lax.rem(i, tiles_per_sb) == 0)
    def _():
        cnt_sc[...] = jnp.zeros_like(cnt_sc)

    sel = jnp.where((lane_f == i1) | (lane_f == i2), 1.0, 0.0).astype(BF16)
    before = _dot(ltri_ref[...], sel) + cnt_sc[0:1, :]
    rank1 = jnp.sum(jnp.where(lane_f == i1, before, 0.0), axis=-1, keepdims=True)
    rank2 = jnp.sum(jnp.where(lane_f == i2, before, 0.0), axis=-1, keepdims=True)
    cnt = cnt_sc[...] + _dot(jnp.ones((8, TOK_TILE), BF16), sel)
    cnt_sc[...] = cnt
    cnt_ref[0] = cnt
    cols = (i1, i2, rank1, rank2, w1, w2)
    meta = jnp.zeros_like(lane_f)
    for c, col in enumerate(cols):
        meta = jnp.where(lane_f == float(c), col, meta)
    meta_ref[...] = meta


def _outproj(oa_p, oa_s, ob_p, ob_s, w_out, xp, xs, mod, gffn, wr, br, n_ptiles, n_stiles, prep, sb):
    n_tiles = n_ptiles + n_stiles
    t = n_tiles * TOK_TILE
    pblocks = prep // ROWS_PER_TILE
    tiles_per_sb = sb // TOK_TILE
    const = lambda i: (0, 0)
    row = lambda i: (i, 0)
    prow = lambda i: (jnp.minimum(i, n_ptiles - 1), 0)
    srow = lambda i: (jnp.maximum(i - n_ptiles, 0), 0)
    r = np.arange(TOK_TILE)
    ltri = jnp.asarray(r[None, :] < r[:, None], BF16)
    return pl.pallas_call(
        functools.partial(_outproj_kernel, n_ptiles, tiles_per_sb),
        grid=(n_tiles,),
        in_specs=[pl.BlockSpec((TOK_TILE, A_WIDTH), prow),
                  pl.BlockSpec((TOK_TILE, A_WIDTH), srow),
                  pl.BlockSpec((TOK_TILE, B_WIDTH), prow),
                  pl.BlockSpec((TOK_TILE, B_WIDTH), srow),
                  pl.BlockSpec((D_MODEL, D_MODEL), const),
                  pl.BlockSpec((TOK_TILE, D_MODEL), prow),
                  pl.BlockSpec((TOK_TILE, D_MODEL), srow),
                  pl.BlockSpec((ROWS_PER_TILE, 6 * D_MODEL),
                               lambda i: (jnp.maximum(i - n_ptiles + pblocks, 0), 0)),
                  pl.BlockSpec((1, D_MODEL), const),
                  pl.BlockSpec((D_MODEL, LANES), const),
                  pl.BlockSpec((1, LANES), const),
                  pl.BlockSpec((TOK_TILE, TOK_TILE), const)],
        out_specs=[pl.BlockSpec((TOK_TILE, D_MODEL), row),
                   pl.BlockSpec((TOK_TILE // SUBLANES, ROW_PIECES, SUBLANES, LANES), lambda i: (i, 0, 0, 0)),
                   pl.BlockSpec((TOK_TILE, LANES), row),
                   pl.BlockSpec((1, 8, LANES), lambda i: (i // tiles_per_sb, 0, 0))],
        out_shape=[jax.ShapeDtypeStruct((t, D_MODEL), F32),
                   jax.ShapeDtypeStruct((t // SUBLANES, ROW_PIECES, SUBLANES, LANES), U32),
                   jax.ShapeDtypeStruct((t, LANES), F32),
                   jax.ShapeDtypeStruct((t // sb, 8, LANES), F32)],
        scratch_shapes=[pltpu.VMEM((D_MODEL // 2 // LANES, 2 * TOK_TILE, LANES), F32),
                        pltpu.VMEM((8, LANES), F32)],
        compiler_params=_params(),
        name="outproj",
    )(oa_p, oa_s, ob_p, ob_s, w_out, xp, xs, mod, gffn, wr, br, ltri)


MOE_SUPER_BLOCK = 2048
SEG_ALIGN = SUBLANES
CHUNK_BF16_ROWS = 2 * SEG_ALIGN * ROW_PIECES
SEG_BITS = 9
PAD_BITS = 5
FFN_ROWS = 512
PLAN_ROWS = LANES


def _local_rows(sb):
    return 2 * sb + N_EXPERTS * SEG_ALIGN


def _sorted_tiles(n_tokens, sb):
    rows = 2 * n_tokens + (n_tokens // sb) * N_EXPERTS * SEG_ALIGN + N_EXPERTS * FFN_ROWS
    return -(-rows // FFN_ROWS)


def _moe_plan_kernel(n_blocks, total_chunks, meta_ref, cnt_ref, ustrict_ref, lstrict_ref,
                     posw_ref, tab_ref, tile_ref):
    b = pl.program_id(0)
    per_tile = FFN_ROWS // SEG_ALIGN

    @pl.when(b == 0)
    def _():
        cnt = cnt_ref[...]
        chunks = jnp.floor((cnt + (SEG_ALIGN - 1)) * (1.0 / SEG_ALIGN))
        chunks_b = chunks.astype(BF16)
        loc = _dot(chunks_b, ustrict_ref[...])
        before = _dot(lstrict_ref[...], chunks_b)
        tot = _dot(jnp.ones((PLAN_ROWS, PLAN_ROWS), BF16), chunks_b)
        tiles = jnp.floor((tot + (per_tile - 1)) * (1.0 / per_tile))
        tile_off = _dot(tiles.astype(BF16), ustrict_ref[...])
        n_tiles = jnp.sum(tiles[0:1], axis=-1, keepdims=True)
        lane1 = lax.broadcasted_iota(jnp.int32, (PLAN_ROWS, LANES), 1)
        tail = lane1 == ROUTE_OFF + N_EXPERTS
        pad_off = jnp.where(tail, n_tiles * per_tile, tile_off * per_tile + tot)
        pad_n = jnp.where(tail, total_chunks - n_tiles * per_tile, tiles * per_tile - tot)
        row = lax.broadcasted_iota(jnp.int32, (PLAN_ROWS, LANES), 0)
        tab_ref[0] = loc
        tab_ref[1] = chunks
        tab_ref[2] = tile_off * per_tile + before
        tab_ref[3] = jnp.where(row == 0, pad_off, jnp.where(row == 1, pad_n, jnp.where(row == 2, n_tiles, 0.0)))
        t_idx = lax.broadcasted_iota(jnp.int32, tile_ref.shape, 0).astype(F32)
        lane_t = lax.broadcasted_iota(jnp.int32, tile_ref.shape, 1)
        is_expert = (lane_t >= ROUTE_OFF) & (lane_t < ROUTE_OFF + N_EXPERTS)
        ends = (tile_off + tiles)[0:1, :]
        owner = jnp.sum(jnp.where(is_expert & (ends <= t_idx), 1.0, 0.0), axis=-1, keepdims=True)
        tile_ref[...] = jnp.broadcast_to(jnp.minimum(owner, N_EXPERTS - 1.0), tile_ref.shape)

    own = jnp.floor((cnt_ref[pl.ds(b, 1), :] + (SEG_ALIGN - 1)) * (1.0 / SEG_ALIGN))
    own_off = _dot(jnp.broadcast_to(own, (SUBLANES, LANES)).astype(BF16), ustrict_ref[...]) * SEG_ALIGN
    meta = meta_ref[...]
    lane_f = lax.broadcasted_iota(jnp.int32, meta.shape, 1).astype(F32)
    off_row = own_off[0:1, :]
    pos = []
    for k in range(2):
        e_lane = meta[:, k:k + 1]
        base = jnp.sum(jnp.where(lane_f == e_lane, off_row, 0.0), axis=-1, keepdims=True)
        p = base + meta[:, 2 + k:3 + k]
        tile = jnp.floor(p * (1.0 / SUBLANES))
        pos.append(tile * (ROW_TILE - SUBLANES) + p)
    out = jnp.zeros_like(meta)
    for c, col in enumerate((pos[0], pos[1], meta[:, 4:5], meta[:, 5:6])):
        out = jnp.where(lane_f == float(c), col, out)
    posw_ref[...] = out


def _moe_plan(meta, cnt, sb):
    n_blocks = meta.shape[0] // sb
    assert n_blocks <= PLAN_ROWS and sb // SEG_ALIGN <= 256
    n_tiles = _sorted_tiles(meta.shape[0], sb)
    tile_rows = -(-n_tiles // SUBLANES) * SUBLANES
    r = np.arange(LANES)
    ustrict = jnp.asarray(r[:, None] < r[None, :], BF16)
    lstrict = jnp.asarray(r[None, :] < r[:, None], BF16)
    cnt_all = jnp.pad(cnt[:, 0, :], ((0, PLAN_ROWS - n_blocks), (0, 0)))
    const = lambda s: (0, 0)
    posw, tab, tile_owner = pl.pallas_call(
        functools.partial(_moe_plan_kernel, n_blocks, float(n_tiles * (FFN_ROWS // SEG_ALIGN))),
        grid=(n_blocks,),
        in_specs=[pl.BlockSpec((sb, LANES), lambda s: (s, 0)),
                  pl.BlockSpec((PLAN_ROWS, LANES), const),
                  pl.BlockSpec((LANES, LANES), const),
                  pl.BlockSpec((PLAN_ROWS, PLAN_ROWS), const)],
        out_specs=[pl.BlockSpec((sb, LANES), lambda s: (s, 0)),
                   pl.BlockSpec((4, PLAN_ROWS, LANES), lambda s: (0, 0, 0)),
                   pl.BlockSpec((tile_rows, LANES), const)],
        out_shape=[jax.ShapeDtypeStruct(meta.shape, F32),
                   jax.ShapeDtypeStruct((4, PLAN_ROWS, LANES), F32),
                   jax.ShapeDtypeStruct((tile_rows, LANES), F32)],
        compiler_params=_params(),
        name="moe_plan",
    )(meta, cnt_all, ustrict, lstrict)
    experts = slice(ROUTE_OFF, ROUTE_OFF + N_EXPERTS)
    to_i32 = lambda x: x.astype(jnp.int32).reshape(-1)
    plan = dict(
        loc=to_i32(tab[0, :n_blocks, experts]), n=to_i32(tab[1, :n_blocks, experts]),
        dst=to_i32(tab[2, :n_blocks, experts]),
        pad_off=to_i32(tab[3, 0, ROUTE_OFF:ROUTE_OFF + N_EXPERTS + 1]),
        pad_n=to_i32(tab[3, 1, ROUTE_OFF:ROUTE_OFF + N_EXPERTS + 1]),
        n_tiles=to_i32(tab[3, 2, 0:1]),
        owner=to_i32(tile_owner[:n_tiles, 0]))
    return posw, plan


def _token_rows(start):
    return pl.ds(start, ROW_PIECES, stride=SUBLANES)


def _pow2_copies(src_ref, dst_ref, src_chunk, dst_chunk, n, n_bits, sem, act):
    done = 0
    for k in reversed(range(n_bits)):
        take = (n >> k) & 1
        rows = CHUNK_BF16_ROWS << k
        src0 = 0 if src_chunk is None else pl.multiple_of((src_chunk + done) * CHUNK_BF16_ROWS, CHUNK_BF16_ROWS)
        dst0 = pl.multiple_of((dst_chunk + done) * CHUNK_BF16_ROWS, CHUNK_BF16_ROWS)

        @pl.when(take == 1)
        def _(src0=src0, dst0=dst0, rows=rows):
            act(pltpu.make_async_copy(src_ref.at[pl.ds(src0, rows)], dst_ref.at[pl.ds(dst0, rows)], sem))

        done = done + take * (1 << k)


def _segment_copies(block, loc_ref, n_ref, dst_ref, local_ref, global_ref, to_global, sem, act):
    def per_expert(e, carry):
        seg = block * N_EXPERTS + e
        if to_global:
            _pow2_copies(local_ref, global_ref, loc_ref[seg], dst_ref[seg], n_ref[seg], SEG_BITS, sem, act)
        else:
            _pow2_copies(global_ref, local_ref, dst_ref[seg], loc_ref[seg], n_ref[seg], SEG_BITS, sem, act)
        return carry

    lax.fori_loop(0, N_EXPERTS, per_expert, 0)


def _zero_fill(zero_ref, global_ref, padoff_ref, padn_ref, sem, act):
    full = 1 << PAD_BITS

    def per_pad(e, carry):
        def per_full(c, inner):
            dst0 = pl.multiple_of((padoff_ref[e] + c * full) * CHUNK_BF16_ROWS, CHUNK_BF16_ROWS)
            act(pltpu.make_async_copy(zero_ref, global_ref.at[pl.ds(dst0, full * CHUNK_BF16_ROWS)], sem))
            return inner

        n_full = padn_ref[e] >> PAD_BITS
        lax.fori_loop(0, n_full, per_full, 0)
        _pow2_copies(zero_ref, global_ref, None, padoff_ref[e] + n_full * full, padn_ref[e] & (full - 1),
                     PAD_BITS, sem, act)
        return carry

    lax.fori_loop(0, N_EXPERTS + 1, per_pad, 0)


STAGE_SLAB = 1024


def _restage(src_sc, dst_sc, dst_dtype):
    ratio = dst_sc.shape[0] / src_sc.shape[0]
    n_slabs = src_sc.shape[0] // (STAGE_SLAB if ratio > 1 else 2 * STAGE_SLAB)
    src_rows = src_sc.shape[0] // n_slabs
    dst_rows = dst_sc.shape[0] // n_slabs

    def slab(i, carry):
        s0 = pl.multiple_of(i * src_rows, src_rows)
        d0 = pl.multiple_of(i * dst_rows, dst_rows)
        dst_sc[pl.ds(d0, dst_rows), :] = pltpu.bitcast(src_sc[pl.ds(s0, src_rows), :], dst_dtype)
        return carry

    lax.fori_loop(0, n_slabs, slab, 0)


def _moe_dispatch_kernel(sb, n_blocks, loc_ref, n_ref, dst_ref, padoff_ref, padn_ref,
                         h2p_ref, a1_ref, a2_ref, xs_hbm, local_sc, stage_sc, zero_sc, sems, zero_sem):
    b = pl.program_id(0)
    slot = b & 1

    def segments(block, buf, act):
        _segment_copies(block, loc_ref, n_ref, dst_ref, stage_sc.at[buf], xs_hbm, True, sems.at[buf], act)

    local_sc[...] = jnp.zeros_like(local_sc)

    def step(g, carry):
        src = pl.multiple_of(g * ROW_TILE, ROW_TILE)
        for u in range(SUBLANES):
            t = g * SUBLANES + u
            row = h2p_ref[_token_rows(src + u), :]
            local_sc[_token_rows(a1_ref[t]), :] = row
            local_sc[_token_rows(a2_ref[t]), :] = row
        return carry

    lax.fori_loop(0, sb // SUBLANES, step, 0)

    @pl.when(b > 0)
    def _():
        segments(b - 1, 1 - slot, lambda c: c.wait())

    _restage(local_sc, stage_sc.at[slot], BF16)
    segments(b, slot, lambda c: c.start())

    @pl.when(b == 0)
    def _():
        zero_sc[...] = jnp.zeros_like(zero_sc)
        _zero_fill(zero_sc, xs_hbm, padoff_ref, padn_ref, zero_sem, lambda c: c.start())
        _zero_fill(zero_sc, xs_hbm, padoff_ref, padn_ref, zero_sem, lambda c: c.wait())

    @pl.when(b == n_blocks - 1)
    def _():
        segments(b, slot, lambda c: c.wait())


def _smem_vec(n, index_map):
    return pl.BlockSpec((n,), index_map, memory_space=pltpu.SMEM)


def _moe_dispatch(h2p, a1, a2, plan, sb, n_tiles):
    n_blocks = h2p.shape[0] // (sb * ROW_PIECES)
    local_flat = _local_rows(sb) * ROW_PIECES
    vec = _smem_vec(sb, lambda s, *_: (s,))
    return pl.pallas_call(
        functools.partial(_moe_dispatch_kernel, sb, n_blocks),
        grid_spec=pltpu.PrefetchScalarGridSpec(
            num_scalar_prefetch=5,
            grid=(n_blocks,),
            in_specs=[pl.BlockSpec((sb * ROW_PIECES, LANES), lambda s, *_: (s, 0)), vec, vec],
            out_specs=pl.BlockSpec(memory_space=pl.ANY),
            scratch_shapes=[pltpu.VMEM((local_flat, LANES), U32),
                            pltpu.VMEM((2, 2 * local_flat, LANES), BF16),
                            pltpu.VMEM(((1 << PAD_BITS) * CHUNK_BF16_ROWS, LANES), BF16),
                            pltpu.SemaphoreType.DMA((2,)),
                            pltpu.SemaphoreType.DMA(())]),
        out_shape=jax.ShapeDtypeStruct((n_tiles * FFN_ROWS * ROW_PIECES * 2, LANES), BF16),
        compiler_params=_params(),
        name="moe_dispatch",
    )(plan["loc"], plan["n"], plan["dst"], plan["pad_off"], plan["pad_n"], h2p, a1, a2)


def _moe_ffn_kernel(owner_ref, ntiles_ref, xs_ref, wg_ref, wu_ref, wd_ref, ys_ref,
                    wg_sc, wu_sc, wd_sc, z32_sc):
    i = pl.program_id(0)
    half = D_MODEL // 2
    used = i < ntiles_ref[0]

    @pl.when(used & ((i == 0) | (owner_ref[i] != owner_ref[jnp.maximum(i - 1, 0)])))
    def _():
        wg_sc[...] = wg_ref[0].astype(BF16)
        wu_sc[...] = wu_ref[0].astype(BF16)
        wd_sc[...] = wd_ref[0].astype(BF16)

    @pl.when(used)
    def _():
        lo, hi = _unpack_rows(z32_sc, _from_row_tiled(xs_ref[...], FFN_ROWS), FFN_ROWS)
        lo, hi = lo.astype(BF16), hi.astype(BF16)
        g = _dot(lo, wg_sc[0:half, :]) + _dot(hi, wg_sc[half:D_MODEL, :])
        u = _dot(lo, wu_sc[0:half, :]) + _dot(hi, wu_sc[half:D_MODEL, :])
        y = _dot((_silu(g) * u).astype(BF16), wd_sc[...])
        ys_ref[...] = _flatten_tiled(_to_row_tiled(_pack_rows(z32_sc, y, FFN_ROWS), FFN_ROWS))

    @pl.when(jnp.logical_not(used))
    def _():
        ys_ref[...] = jnp.zeros_like(ys_ref)


def _moe_ffn(xs, plan, wg, wu, wd):
    flat = FFN_ROWS * ROW_PIECES * 2
    n_tiles = xs.shape[0] // flat
    last_used = lambda i, owner, nt: jnp.minimum(i, nt[0] - 1)
    wspec = lambda shape: pl.BlockSpec((1,) + shape, lambda i, owner, nt: (owner[last_used(i, owner, nt)], 0, 0))
    return pl.pallas_call(
        _moe_ffn_kernel,
        grid_spec=pltpu.PrefetchScalarGridSpec(
            num_scalar_prefetch=2,
            grid=(n_tiles,),
            in_specs=[pl.BlockSpec((flat, LANES), lambda i, owner, nt: (last_used(i, owner, nt), 0)),
                      wspec((D_MODEL, EXPERT_FF)), wspec((D_MODEL, EXPERT_FF)), wspec((EXPERT_FF, D_MODEL))],
            out_specs=pl.BlockSpec((flat, LANES), lambda i, owner, nt: (i, 0)),
            scratch_shapes=[pltpu.VMEM((D_MODEL, EXPERT_FF), BF16),
                            pltpu.VMEM((D_MODEL, EXPERT_FF), BF16),
                            pltpu.VMEM((EXPERT_FF, D_MODEL), BF16),
                            pltpu.VMEM((ROW_PIECES, 2 * FFN_ROWS, LANES), F32)]),
        out_shape=jax.ShapeDtypeStruct(xs.shape, BF16),
        compiler_params=_params(),
        name="moe_ffn",
    )(plan["owner"], plan["n_tiles"], xs, wg, wu, wd)


def _moe_combine_kernel(n_psb, n_blocks, loc_ref, n_ref, dst_ref,
                        ys_hbm, a1_ref, a2_ref, posw_ref, x1_ref, mod_ref, yp_ref, yo_ref,
                        local_sc, stage_sc, g1_sc, g2_sc, z32_sc, sems):
    s = pl.program_id(0)
    slot = s & 1

    def segments(block, buf, act):
        _segment_copies(block, loc_ref, n_ref, dst_ref, stage_sc.at[buf], ys_hbm, False, sems.at[buf], act)

    @pl.when(pl.program_id(1) == 0)
    def _():
        @pl.when(s == 0)
        def _():
            segments(s, slot, lambda c: c.start())

        segments(s, slot, lambda c: c.wait())
        _restage(stage_sc.at[slot], local_sc, U32)

        @pl.when(s + 1 < n_blocks)
        def _():
            segments(s + 1, 1 - slot, lambda c: c.start())

    def step(g, carry):
        dst = pl.multiple_of(g * ROW_TILE, ROW_TILE)
        for u in range(SUBLANES):
            t = g * SUBLANES + u
            g1_sc[_token_rows(dst + u), :] = local_sc[_token_rows(a1_ref[t]), :]
            g2_sc[_token_rows(dst + u), :] = local_sc[_token_rows(a2_ref[t]), :]
        return carry

    lax.fori_loop(0, TOK_TILE // SUBLANES, step, 0)
    halves = lambda g_sc: [pltpu.bitcast(p, BF16) for p in _from_row_tiled(g_sc[...], TOK_TILE)]
    lo1, hi1 = _unpack_rows(z32_sc, halves(g1_sc), TOK_TILE)
    lo2, hi2 = _unpack_rows(z32_sc, halves(g2_sc), TOK_TILE)
    w1, w2 = posw_ref[:, 2:3], posw_ref[:, 3:4]
    moe = jnp.concatenate([w1 * lo1 + w2 * lo2, w1 * hi1 + w2 * hi2], axis=1)
    gate2 = _rows_to_tokens(mod_ref[:, 5 * D_MODEL:6 * D_MODEL], D_MODEL)
    y = x1_ref[...] + gate2 * moe

    @pl.when(s < n_psb)
    def _():
        yp_ref[...] = y

    @pl.when(s >= n_psb)
    def _():
        yo_ref[...] = y


def _moe_combine(ys, a1, a2, posw, plan, x1, mod, sb, n_ptiles, n_stiles, prep):
    tps = sb // TOK_TILE
    n_blocks = (n_ptiles + n_stiles) // tps
    n_psb = n_ptiles // tps
    pblocks = prep // ROWS_PER_TILE
    tile = lambda s, j: s * tps + j
    vec = _smem_vec(TOK_TILE, lambda s, j, *_: (tile(s, j),))
    return pl.pallas_call(
        functools.partial(_moe_combine_kernel, n_psb, n_blocks),
        grid_spec=pltpu.PrefetchScalarGridSpec(
            num_scalar_prefetch=3,
            grid=(n_blocks, tps),
            in_specs=[pl.BlockSpec(memory_space=pl.ANY), vec, vec,
                      pl.BlockSpec((TOK_TILE, LANES), lambda s, j, *_: (tile(s, j), 0)),
                      pl.BlockSpec((TOK_TILE, D_MODEL), lambda s, j, *_: (tile(s, j), 0)),
                      pl.BlockSpec((ROWS_PER_TILE, 6 * D_MODEL),
                                   lambda s, j, *_: (jnp.maximum(tile(s, j) - n_ptiles + pblocks, 0), 0))],
            out_specs=[pl.BlockSpec((TOK_TILE, D_MODEL),
                                    lambda s, j, *_: (jnp.minimum(tile(s, j), n_ptiles - 1), 0)),
                       pl.BlockSpec((TOK_TILE, D_MODEL),
                                    lambda s, j, *_: (jnp.maximum(tile(s, j) - n_ptiles, 0), 0))],
            scratch_shapes=[pltpu.VMEM((_local_rows(sb) * ROW_PIECES, LANES), U32),
                            pltpu.VMEM((2, _local_rows(sb) * ROW_PIECES * 2, LANES), BF16),
                            pltpu.VMEM((TOK_TILE * ROW_PIECES, LANES), U32),
                            pltpu.VMEM((TOK_TILE * ROW_PIECES, LANES), U32),
                            pltpu.VMEM((ROW_PIECES, 2 * TOK_TILE, LANES), F32),
                            pltpu.SemaphoreType.DMA((2,))]),
        out_shape=[jax.ShapeDtypeStruct((n_ptiles * TOK_TILE, D_MODEL), F32),
                   jax.ShapeDtypeStruct((n_stiles * TOK_TILE, D_MODEL), F32)],
        compiler_params=_params(2),
        name="moe_combine",
    )(plan["loc"], plan["n"], plan["dst"], ys, a1, a2, posw, x1, mod)


def _layer(xp, xs, cache_k, cache_v, state, c_prompt, c_sample, norm_mix_g, norm_ffn_g, w_ada, b_ada, w_in,
           q_norm_g, k_norm_g, rel_bias, w_gate_up, b_gate, gla_norm_g, w_out, w_route_group,
           b_route_group, w_route_expert, b_route_expert, w_exp_gate, w_exp_up, w_exp_down):
    batch, seq, _ = xp.shape
    n_seq, dec_seq, _ = xs.shape
    assert batch == 1 and dec_seq == CHUNK and cache_k.shape[1] == BAND_PAST
    assert seq % TOK_TILE == 0 and seq >= BAND_PAST and (n_seq * CHUNK) % TOK_TILE == 0
    n_ptok, n_stok = seq, n_seq * CHUNK
    n_ptiles, n_stiles = n_ptok // TOK_TILE, n_stok // TOK_TILE
    sb = MOE_SUPER_BLOCK if (n_ptok % MOE_SUPER_BLOCK == 0 and n_stok % MOE_SUPER_BLOCK == 0) else TOK_TILE
    prep = ROWS_PER_TILE

    xp2 = xp.reshape(n_ptok, D_MODEL)
    xs2 = xs.reshape(n_stok, D_MODEL)
    c_rows = jnp.concatenate([jnp.broadcast_to(c_prompt, (prep, D_MODEL)), c_sample], axis=0)
    mod = _adaln(c_rows, w_ada, b_ada)

    w_in_p = jnp.pad(w_in, ((0, 0), (0, IN_PAD - w_in.shape[1]))).astype(BF16)
    wgu_p = jnp.pad(w_gate_up, ((0, LANES - GATE_RANK), (0, 0))).astype(BF16)
    head = np.arange(A_WIDTH) // A_HEAD_DIM
    bd = jnp.asarray(head[:, None] == head[None, :], BF16)
    gq = jnp.tile(q_norm_g, A_HEADS).reshape(1, A_WIDTH)
    gk = jnp.tile(k_norm_g, A_HEADS).reshape(1, A_WIDTH)
    q, k, v, kf, vf, gla, la = _inproj(
        xp2, xs2, mod, norm_mix_g.reshape(1, D_MODEL), w_in_p, bd, gq, gk, wgu_p,
        b_gate.reshape(1, B_KWIDTH), n_ptiles, n_stiles, prep)

    first_chunk = n_ptok // CHUNK
    oa_p = _attn_prompt(rel_bias[:, _bias_lanes(3 * Q_ROWS)], q, k, v, n_ptok // Q_ROWS)
    oa_s = _attn_sample(rel_bias[:, _bias_lanes(SAMPLE_KEYS)], q, k, v,
                        cache_k.reshape(n_seq, BAND_PAST, A_WIDTH), cache_v.reshape(n_seq, BAND_PAST, A_WIDTH),
                        first_chunk, n_seq)
    g_gla = gla_norm_g.reshape(1, B_DV)
    ob_p, sfin_p = _gla_prompt(gla, la, g_gla, n_ptok // (GLA_CHUNKS * CHUNK))
    ob_s, sfin_s = _gla_sample(gla, la, g_gla, _state_to_pairs(state), first_chunk, n_seq)

    wr = jnp.pad(jnp.concatenate([w_route_group, w_route_expert], axis=1),
                 ((0, 0), (0, LANES - N_GROUPS - N_EXPERTS)))
    br = jnp.pad(jnp.concatenate([b_route_group, b_route_expert]), (0, LANES - N_GROUPS - N_EXPERTS))
    x1, h2p, meta, cnt = _outproj(oa_p, oa_s, ob_p, ob_s, w_out.astype(BF16), xp2, xs2, mod,
                                  norm_ffn_g.reshape(1, D_MODEL), wr, br.reshape(1, LANES),
                                  n_ptiles, n_stiles, prep, sb)

    posw, plan = _moe_plan(meta, cnt, sb)
    a1, a2 = posw[:, 0].astype(jnp.int32), posw[:, 1].astype(jnp.int32)
    xs_sorted = _moe_dispatch(h2p.reshape(-1, LANES), a1, a2, plan, sb, _sorted_tiles(n_ptok + n_stok, sb))
    ys_sorted = _moe_ffn(xs_sorted, plan, w_exp_gate, w_exp_up, w_exp_down)
    yp, ys = _moe_combine(ys_sorted, a1, a2, posw, plan, x1, mod, sb, n_ptiles, n_stiles, prep)

    tail = min(BAND_PAST, seq)
    heads = (A_HEADS, A_HEAD_DIM)
    return (yp.reshape(1, seq, D_MODEL), ys.reshape(n_seq, CHUNK, D_MODEL),
            kf[TOK_TILE - tail:TOK_TILE].reshape((1, tail) + heads),
            vf[TOK_TILE - tail:TOK_TILE].reshape((1, tail) + heads),
            _pairs_to_state(sfin_p)[None],
            kf[TOK_TILE:].reshape((n_seq, CHUNK) + heads),
            vf[TOK_TILE:].reshape((n_seq, CHUNK) + heads),
            _pairs_to_state(sfin_s))


def kernel(x_prompt, x_sample, cache_a_k, cache_a_v, state_gla, c_prompt, c_sample, norm_mix_g, norm_ffn_g,
           w_ada, b_ada, w_in, q_norm_g, k_norm_g, rel_bias, w_gate_up, b_gate, gla_norm_g, w_out,
           w_route_group, b_route_group, w_route_expert, b_route_expert, w_exp_gate, w_exp_up, w_exp_down):
    depth = w_in.shape[0]
    yp, ys = x_prompt, x_sample
    outs = [[] for _ in range(6)]
    for l in range(depth):
        yp, ys, kp, vp, sp, ks, vs, ss = _layer(
            yp, ys, cache_a_k[l], cache_a_v[l], state_gla[l], c_prompt, c_sample, norm_mix_g[l], norm_ffn_g[l],
            w_ada[l], b_ada[l], w_in[l], q_norm_g[l], k_norm_g[l], rel_bias[l], w_gate_up[l], b_gate[l],
            gla_norm_g[l], w_out[l], w_route_group[l], b_route_group[l], w_route_expert[l], b_route_expert[l],
            w_exp_gate[l], w_exp_up[l], w_exp_down[l])
        for lst, val in zip(outs, (kp, vp, sp, ks, vs, ss)):
            lst.append(val)
    return (yp, ys) + tuple(jnp.stack(o) for o in outs)
```

```python
import functools

import numpy as np
import jax
import jax.numpy as jnp
from jax import lax
from jax.experimental import pallas as pl
from jax.experimental.pallas import tpu as pltpu

F32 = jnp.float32
BF16 = jnp.bfloat16
U32 = jnp.uint32

D_MODEL = 1024
CHUNK = 64
LOG_CHUNK = 6
BAND_CHUNKS = 8
BAND_PAST = BAND_CHUNKS * CHUNK
A_WIDTH = 512
A_HEADS = 8
A_HEAD_DIM = 64
MAX_REL = 128
N_REL = CHUNK + MAX_REL
B_WIDTH = 512
B_HEADS = 4
B_DV = 128
B_DK = 64
B_KWIDTH = 256
GATE_RANK = 16
GATE_TAU = 16.0
N_GROUPS = 4
EXPERTS_PER_GROUP = 8
N_EXPERTS = 32
EXPERT_FF = 256
EPS = 1e-6

LANES = 128
IN_MAIN = 3 * A_WIDTH + 2 * B_KWIDTH + 2 * B_WIDTH
IN_PAD = IN_MAIN + LANES
TOK_TILE = 512
ROWS_PER_TILE = TOK_TILE // CHUNK
Q_CHUNKS = 4
Q_ROWS = Q_CHUNKS * CHUNK
ROLL_W = 1024
NEG = -1e30
ROUTE_OFF = N_GROUPS
VMEM_LIMIT = 56 * 1024 * 1024


def _params(n_axes=1):
    return pltpu.CompilerParams(dimension_semantics=("arbitrary",) * n_axes,
                                vmem_limit_bytes=VMEM_LIMIT)


def _split(a):
    hi = a.astype(BF16)
    lo = (a - hi.astype(F32)).astype(BF16)
    return hi, lo


def _dot(a, b):
    return jnp.dot(a, b, preferred_element_type=F32)


def _dot3(a, b):
    ah, al = _split(a)
    bh, bl = _split(b)
    return _dot(ah, bh) + _dot(al, bh) + _dot(ah, bl)


def _dot_nt(a, b):
    return lax.dot_general(a, b, (((1,), (1,)), ((), ())), preferred_element_type=F32)


def _dot_tn(a, b):
    return lax.dot_general(a, b, (((0,), (0,)), ((), ())), preferred_element_type=F32)


def _silu(x):
    return x / (1.0 + jnp.exp(-x))


def _rows_to_tokens(rows, n):
    r = rows.shape[0]
    return jnp.broadcast_to(rows[:, None, :], (r, CHUNK, n)).reshape(r * CHUNK, n)


def _adaln_kernel(c_ref, w_ref, b_ref, o_ref):
    a = _silu(c_ref[...])
    o_ref[...] = _dot3(a, w_ref[...]) + b_ref[...]


def _adaln(c_rows, w_ada, b_ada):
    r = c_rows.shape[0]
    n = w_ada.shape[1]
    tn = 1024
    return pl.pallas_call(
        _adaln_kernel,
        grid=(n // tn,),
        in_specs=[pl.BlockSpec((r, D_MODEL), lambda j: (0, 0)),
                  pl.BlockSpec((D_MODEL, tn), lambda j: (0, j)),
                  pl.BlockSpec((1, tn), lambda j: (0, j))],
        out_specs=pl.BlockSpec((r, tn), lambda j: (0, j)),
        out_shape=jax.ShapeDtypeStruct((r, n), F32),
        compiler_params=_params(),
        name="adaln",
    )(c_rows, w_ada, b_ada.reshape(1, n))


def _head_rms(z, bd_ref, g):
    ms = _dot((z * z).astype(BF16), bd_ref[...]) * (1.0 / A_HEAD_DIM)
    return z * lax.rsqrt(ms + EPS) * g


def _inproj_kernel(n_ptiles, xp_ref, xs_ref, mod_ref, gmix_ref, w_ref, bd_ref, gq_ref, gk_ref,
                   wgu_ref, bg_ref,
                   q_ref, k_ref, v_ref, kf_ref, vf_ref, gla_ref, la_ref):
    i = pl.program_id(0)
    x = jnp.where(i < n_ptiles, xp_ref[...], xs_ref[...])
    ms = jnp.mean(x * x, axis=-1, keepdims=True)
    xn = x * lax.rsqrt(ms + EPS) * gmix_ref[...]
    sh = _rows_to_tokens(mod_ref[:, 0:D_MODEL], D_MODEL)
    sc = _rows_to_tokens(mod_ref[:, D_MODEL:2 * D_MODEL], D_MODEL)
    hb = (xn * (1.0 + sc) + sh).astype(BF16)

    zq = _dot(hb, w_ref[:, 0:A_WIDTH])
    q_ref[...] = (_head_rms(zq, bd_ref, gq_ref[...]) * (LOG2E * A_HEAD_DIM ** -0.5)).astype(BF16)
    zk = _dot(hb, w_ref[:, A_WIDTH:2 * A_WIDTH])
    kn = _head_rms(zk, bd_ref, gk_ref[...])
    k_ref[...] = kn.astype(BF16)
    kf_ref[...] = kn
    zv = _dot(hb, w_ref[:, 2 * A_WIDTH:3 * A_WIDTH])
    v_ref[...] = zv.astype(BF16)
    vf_ref[...] = zv

    o = 3 * A_WIDTH
    zqb = _dot(hb, w_ref[:, o:o + B_KWIDTH]) * (B_DK ** -0.5)
    gla_ref[:, 0:B_KWIDTH] = zqb.astype(BF16)
    for c in range(B_KWIDTH, 2 * B_KWIDTH + 2 * B_WIDTH, 256):
        gla_ref[:, c:c + 256] = _dot(hb, w_ref[:, o + c:o + c + 256]).astype(BF16)

    gr = _dot(hb, w_ref[:, IN_MAIN:IN_PAD])
    logit = _dot(gr.astype(BF16), wgu_ref[...]) + bg_ref[...]
    log_sig = jnp.minimum(logit, 0.0) - jnp.log1p(jnp.exp(-jnp.abs(logit)))
    la_ref[...] = log_sig * (1.0 / GATE_TAU)


def _inproj(xp, xs, mod, gmix, w_in_p, bd, gq, gk, wgu_p, bg, n_ptiles, n_stiles, prep):
    n_tiles = n_ptiles + n_stiles
    t = n_tiles * TOK_TILE
    tail_tiles = 1 + n_stiles
    pblocks = prep // ROWS_PER_TILE
    const = lambda i: (0, 0)
    row = lambda i: (i, 0)
    tail = lambda i: (jnp.maximum(i - (n_ptiles - 1), 0), 0)
    return pl.pallas_call(
        functools.partial(_inproj_kernel, n_ptiles),
        grid=(n_tiles,),
        in_specs=[pl.BlockSpec((TOK_TILE, D_MODEL), lambda i: (jnp.minimum(i, n_ptiles - 1), 0)),
                  pl.BlockSpec((TOK_TILE, D_MODEL), lambda i: (jnp.maximum(i - n_ptiles, 0), 0)),
                  pl.BlockSpec((ROWS_PER_TILE, 6 * D_MODEL),
                               lambda i: (jnp.maximum(i - n_ptiles + pblocks, 0), 0)),
                  pl.BlockSpec((1, D_MODEL), const),
                  pl.BlockSpec((D_MODEL, IN_PAD), const),
                  pl.BlockSpec((A_WIDTH, A_WIDTH), const),
                  pl.BlockSpec((1, A_WIDTH), const),
                  pl.BlockSpec((1, A_WIDTH), const),
                  pl.BlockSpec((LANES, B_KWIDTH), const),
                  pl.BlockSpec((1, B_KWIDTH), const)],
        out_specs=[pl.BlockSpec((TOK_TILE, A_WIDTH), row),
                   pl.BlockSpec((TOK_TILE, A_WIDTH), row),
                   pl.BlockSpec((TOK_TILE, A_WIDTH), row),
                   pl.BlockSpec((TOK_TILE, A_WIDTH), tail),
                   pl.BlockSpec((TOK_TILE, A_WIDTH), tail),
                   pl.BlockSpec((TOK_TILE, 2 * B_KWIDTH + 2 * B_WIDTH), row),
                   pl.BlockSpec((TOK_TILE, B_KWIDTH), row)],
        out_shape=[jax.ShapeDtypeStruct((t, A_WIDTH), BF16),
                   jax.ShapeDtypeStruct((t, A_WIDTH), BF16),
                   jax.ShapeDtypeStruct((t, A_WIDTH), BF16),
                   jax.ShapeDtypeStruct((tail_tiles * TOK_TILE, A_WIDTH), F32),
                   jax.ShapeDtypeStruct((tail_tiles * TOK_TILE, A_WIDTH), F32),
                   jax.ShapeDtypeStruct((t, 2 * B_KWIDTH + 2 * B_WIDTH), BF16),
                   jax.ShapeDtypeStruct((t, B_KWIDTH), F32)],
        compiler_params=_params(),
        name="inproj",
    )(xp, xs, mod, gmix, w_in_p, bd, gq, gk, wgu_p, bg)


def _bias_lanes(n_keys):
    l = np.arange(ROLL_W)
    d = np.where(l < n_keys, BAND_PAST - l, BAND_PAST - l + ROLL_W)
    return np.clip(d, -(CHUNK - 1), MAX_REL) + (CHUNK - 1)


LOG2E = 1.4426950408889634


def _band_mask(m_rows, n_keys, first_col):
    qi = lax.broadcasted_iota(jnp.int32, (m_rows, n_keys), 0) >> LOG_CHUNK
    kw = lax.broadcasted_iota(jnp.int32, (m_rows, n_keys), 1)
    kc = kw >> LOG_CHUNK
    return (kc >= qi) & (kc <= qi + BAND_CHUNKS) & (kw >= first_col)


def _bias_tile(u_ref, h, ok):
    m_rows, n_keys = ok.shape
    src = jnp.broadcast_to(u_ref[h:h + 1, :] * LOG2E, (m_rows, ROLL_W))
    toe = pltpu.roll(src, 0, 1, stride=1, stride_axis=0)
    return jnp.where(ok, toe[:, 0:n_keys], NEG)


def _attend(q, kcat, vcat, bias_sc):
    m_rows = q.shape[0]
    first = lax.broadcasted_iota(jnp.int32, (m_rows, LANES), 1) < A_HEAD_DIM
    outs = []
    for p in range(A_HEADS // 2):
        lanes = slice(p * LANES, (p + 1) * LANES)
        qp, kp, vp = q[:, lanes], kcat[:, lanes], vcat[:, lanes]
        zero = jnp.zeros_like(qp)
        q2 = jnp.concatenate([jnp.where(first, qp, zero), jnp.where(first, zero, qp)], axis=0)
        s = _dot_nt(q2, kp) + bias_sc[p]
        e = jnp.exp2(s - jnp.max(s, axis=-1, keepdims=True))
        l = jnp.sum(e, axis=-1, keepdims=True)
        o2 = _dot(e.astype(BF16), vp) / l
        outs.append(jnp.where(first, o2[0:m_rows], o2[m_rows:2 * m_rows]))
    return jnp.concatenate(outs, axis=-1)


ATTN_SUB = 2
ATTN_WIN = 3


def _attn_prompt_kernel(u_ref, q_ref, *refs):
    k_refs = refs[0:ATTN_SUB + ATTN_WIN - 1]
    v_refs = refs[ATTN_SUB + ATTN_WIN - 1:2 * (ATTN_SUB + ATTN_WIN - 1)]
    o_ref, bias_sc = refs[-2:]
    j = pl.program_id(0)
    n_keys = ATTN_WIN * Q_ROWS

    @pl.when(j <= 1)
    def _():
        for sub in range(ATTN_SUB):
            ok = _band_mask(Q_ROWS, n_keys, (ATTN_WIN - 1 - (ATTN_SUB * j + sub)) * Q_ROWS)
            for h in range(A_HEADS):
                bias_sc[sub, h // 2, (h % 2) * Q_ROWS:(h % 2 + 1) * Q_ROWS, :] = _bias_tile(u_ref, h, ok)

    ks = [r[...] for r in k_refs]
    vs = [r[...] for r in v_refs]
    for sub in range(ATTN_SUB):
        rows = slice(sub * Q_ROWS, (sub + 1) * Q_ROWS)
        kcat = jnp.concatenate(ks[sub:sub + ATTN_WIN], axis=0)
        vcat = jnp.concatenate(vs[sub:sub + ATTN_WIN], axis=0)
        o_ref[rows, :] = _attend(q_ref[rows, :], kcat, vcat, bias_sc.at[sub]).astype(BF16)


def _attn_prompt(u, q, k, v, n_steps):
    const = lambda j: (0, 0)
    n_blk = ATTN_SUB + ATTN_WIN - 1
    blk = lambda d: pl.BlockSpec((Q_ROWS, A_WIDTH),
                                 lambda j, d=d: (jnp.maximum(ATTN_SUB * j - (ATTN_WIN - 1) + d, 0), 0))
    step_rows = ATTN_SUB * Q_ROWS
    return pl.pallas_call(
        _attn_prompt_kernel,
        grid=(n_steps,),
        in_specs=[pl.BlockSpec((A_HEADS, ROLL_W), const), pl.BlockSpec((step_rows, A_WIDTH), lambda j: (j, 0))]
                 + [blk(d) for d in range(n_blk)] * 2,
        out_specs=pl.BlockSpec((step_rows, A_WIDTH), lambda j: (j, 0)),
        out_shape=jax.ShapeDtypeStruct((n_steps * step_rows, A_WIDTH), BF16),
        scratch_shapes=[pltpu.VMEM((ATTN_SUB, A_HEADS // 2, 2 * Q_ROWS, ATTN_WIN * Q_ROWS), F32)],
        compiler_params=_params(),
        name="attn_prompt",
    )(u, q, *([k] * n_blk), *([v] * n_blk))


SAMPLE_KEYS = BAND_PAST + 2 * CHUNK


def _attn_sample_kernel(u_ref, q_ref, kn_ref, vn_ref, kc_ref, vc_ref, o_ref, bias_sc):
    @pl.when(pl.program_id(0) == 0)
    def _():
        ok = _band_mask(CHUNK, SAMPLE_KEYS, 0)
        for p in range(A_HEADS // 2):
            pair = jnp.concatenate([_bias_tile(u_ref, 2 * p, ok), _bias_tile(u_ref, 2 * p + 1, ok)], axis=0)
            bias_sc[p] = pair.T

    pad = jnp.zeros((CHUNK, A_WIDTH), BF16)
    kcat = jnp.concatenate([kc_ref[0].astype(BF16), kn_ref[...], pad], axis=0)
    vcat = jnp.concatenate([vc_ref[0].astype(BF16), vn_ref[...], pad], axis=0)
    q = q_ref[...]
    lane = lax.broadcasted_iota(jnp.int32, (CHUNK, LANES), 1)
    first = lane < A_HEAD_DIM
    zero = jnp.zeros((CHUNK, LANES), BF16)
    outs = []
    for p in range(A_HEADS // 2):
        lanes = slice(p * LANES, (p + 1) * LANES)
        qp = q[:, lanes]
        q_rows = jnp.concatenate([jnp.where(first, qp, zero), jnp.where(first, zero, qp)], axis=0)
        s = _dot_nt(kcat[:, lanes], q_rows) + bias_sc[p]
        e = jnp.exp2(s - jnp.max(s, axis=0, keepdims=True))
        pn = (e * (1.0 / jnp.sum(e, axis=0, keepdims=True))).astype(BF16)
        r = _dot_tn(pn, vcat[:, lanes])
        outs.append(jnp.where(first, r[0:CHUNK], r[CHUNK:2 * CHUNK]))
    o_ref[...] = jnp.concatenate(outs, axis=-1).astype(BF16)


def _attn_sample(u, q, k, v, kc, vc, first_chunk, n_seq):
    new = pl.BlockSpec((CHUNK, A_WIDTH), lambda b: (first_chunk + b, 0))
    cache = pl.BlockSpec((1, BAND_PAST, A_WIDTH), lambda b: (b, 0, 0))
    return pl.pallas_call(
        _attn_sample_kernel,
        grid=(n_seq,),
        in_specs=[pl.BlockSpec((A_HEADS, ROLL_W), lambda b: (0, 0)), new, new, new, cache, cache],
        out_specs=pl.BlockSpec((CHUNK, A_WIDTH), lambda b: (b, 0)),
        out_shape=jax.ShapeDtypeStruct((n_seq * CHUNK, A_WIDTH), BF16),
        scratch_shapes=[pltpu.VMEM((A_HEADS // 2, SAMPLE_KEYS, LANES), F32)],
        compiler_params=_params(),
        name="attn_sample",
    )(u, q, k, v, kc, vc)


GLA_CHUNKS = 4
GLA_SUB = 2


def _gla_block(n_chunks, gla_ref, la_ref, ltri_ref, g_ref, st_sc, o_ref):
    rows = n_chunks * CHUNK
    la = la_ref[...]
    la_hi, la_lo = _split(la)
    b = _dot(ltri_ref[...], la_hi) + _dot(ltri_ref[...], la_lo)
    b3 = b.reshape(n_chunks, CHUNK, B_KWIDTH)
    b_mid = b3[:, CHUNK // 2 - 1:CHUNK // 2, :]
    b_last = b3[:, CHUNK - 1:CHUNK, :]
    q = gla_ref[:, 0:B_KWIDTH].astype(F32).reshape(n_chunks, CHUNK, B_KWIDTH)
    k = gla_ref[:, B_KWIDTH:2 * B_KWIDTH].astype(F32).reshape(n_chunks, CHUNK, B_KWIDTH)
    q_start = (q * jnp.exp(b3)).reshape(rows, B_KWIDTH).astype(BF16)
    q_mid = (q * jnp.exp(b3 - b_mid)).reshape(rows, B_KWIDTH).astype(BF16)
    k_mid = (k * jnp.exp(b_mid - b3)).reshape(rows, B_KWIDTH).astype(BF16)
    k_end = (k * jnp.exp(b_last - b3)).reshape(rows, B_KWIDTH).astype(BF16)
    dec = jnp.exp(b_last)

    ti = lax.broadcasted_iota(jnp.int32, (2 * rows, rows), 0) & (rows - 1)
    si = lax.broadcasted_iota(jnp.int32, (2 * rows, rows), 1)
    causal = (si <= ti) & ((si >> LOG_CHUNK) == (ti >> LOG_CHUNK))
    first_r = lax.broadcasted_iota(jnp.int32, (rows, LANES), 1) < B_DK
    first_c = lax.broadcasted_iota(jnp.int32, (CHUNK, LANES), 1) < B_DK
    first_s = lax.broadcasted_iota(jnp.int32, (B_DV, LANES), 1) < B_DK

    def stack_heads(x, first):
        zero = jnp.zeros_like(x)
        return jnp.concatenate([jnp.where(first, x, zero), jnp.where(first, zero, x)], axis=0)

    for p in range(B_HEADS // 2):
        lanes = slice(p * LANES, (p + 1) * LANES)
        qs_p, qm_p, km_p, ke_p = q_start[:, lanes], q_mid[:, lanes], k_mid[:, lanes], k_end[:, lanes]
        v_pair = gla_ref[:, 2 * B_KWIDTH + 2 * p * B_DV:2 * B_KWIDTH + (2 * p + 2) * B_DV]
        sc = jnp.where(causal, _dot_nt(stack_heads(qm_p, first_r), km_p), 0.0)
        o2 = _dot(sc.astype(BF16), v_pair)
        intra = [o2[0:rows, 0:B_DV], o2[rows:2 * rows, B_DV:2 * B_DV]]
        inter = [[], []]
        st = st_sc[p]
        for c in range(n_chunks):
            cr = slice(c * CHUNK, (c + 1) * CHUNK)
            r2 = _dot_nt(stack_heads(qs_p[cr], first_c), st.astype(BF16))
            inter[0].append(r2[0:CHUNK])
            inter[1].append(r2[CHUNK:2 * CHUNK])
            u2 = _dot_tn(v_pair[cr], ke_p[cr])
            st = st * dec[c, :, lanes] + jnp.where(first_s, u2[0:B_DV], u2[B_DV:2 * B_DV])
        st_sc[p] = st
        for hh in range(2):
            h = 2 * p + hh
            o = intra[hh] + jnp.concatenate(inter[hh], axis=0)
            ms = jnp.mean(o * o, axis=-1, keepdims=True)
            on = o * lax.rsqrt(ms + EPS) * g_ref[...]
            r = gla_ref[:, 2 * B_KWIDTH + B_WIDTH + h * B_DV:2 * B_KWIDTH + B_WIDTH + (h + 1) * B_DV]
            o_ref[:, h * B_DV:(h + 1) * B_DV] = (on * _silu(r.astype(F32))).astype(BF16)


def _gla_prompt_kernel(gla_ref, la_ref, ltri_ref, g_ref, o_ref, sfin_ref, st_sc):
    @pl.when(pl.program_id(0) == 0)
    def _():
        st_sc[...] = jnp.zeros_like(st_sc)

    rows = GLA_CHUNKS * CHUNK
    for sub in range(GLA_SUB):
        part = pl.ds(sub * rows, rows)
        _gla_block(GLA_CHUNKS, gla_ref.at[part], la_ref.at[part], ltri_ref, g_ref, st_sc, o_ref.at[part])
    sfin_ref[...] = st_sc[...]


def _gla_sample_kernel(gla_ref, la_ref, ltri_ref, g_ref, s0_ref, o_ref, sfin_ref, st_sc):
    st_sc[...] = s0_ref[0]
    _gla_block(1, gla_ref, la_ref, ltri_ref, g_ref, st_sc, o_ref)
    sfin_ref[0] = st_sc[...]


def _ltri(n_chunks):
    r = np.arange(n_chunks * CHUNK)
    m = (r[None, :] <= r[:, None]) & (r[None, :] // CHUNK == r[:, None] // CHUNK)
    return jnp.asarray(m, BF16)


_GLA_W = 2 * B_KWIDTH + 2 * B_WIDTH
_ST_SHAPE = (B_HEADS // 2, B_DV, LANES)


def _gla_prompt(gla, la, g, n_steps):
    rows = GLA_SUB * GLA_CHUNKS * CHUNK
    const = lambda j: (0, 0)
    return pl.pallas_call(
        _gla_prompt_kernel,
        grid=(n_steps,),
        in_specs=[pl.BlockSpec((rows, _GLA_W), lambda j: (j, 0)),
                  pl.BlockSpec((rows, B_KWIDTH), lambda j: (j, 0)),
                  pl.BlockSpec((GLA_CHUNKS * CHUNK, GLA_CHUNKS * CHUNK), const),
                  pl.BlockSpec((1, B_DV), const)],
        out_specs=[pl.BlockSpec((rows, B_WIDTH), lambda j: (j, 0)),
                   pl.BlockSpec(_ST_SHAPE, lambda j: (0, 0, 0))],
        out_shape=[jax.ShapeDtypeStruct((n_steps * rows, B_WIDTH), BF16),
                   jax.ShapeDtypeStruct(_ST_SHAPE, F32)],
        scratch_shapes=[pltpu.VMEM(_ST_SHAPE, F32)],
        compiler_params=_params(),
        name="gla_prompt",
    )(gla, la, _ltri(GLA_CHUNKS), g)


def _gla_sample(gla, la, g, s0, first_chunk, n_seq):
    const = lambda b: (0, 0)
    st_spec = pl.BlockSpec((1,) + _ST_SHAPE, lambda b: (b, 0, 0, 0))
    return pl.pallas_call(
        _gla_sample_kernel,
        grid=(n_seq,),
        in_specs=[pl.BlockSpec((CHUNK, _GLA_W), lambda b: (first_chunk + b, 0)),
                  pl.BlockSpec((CHUNK, B_KWIDTH), lambda b: (first_chunk + b, 0)),
                  pl.BlockSpec((CHUNK, CHUNK), const),
                  pl.BlockSpec((1, B_DV), const),
                  st_spec],
        out_specs=[pl.BlockSpec((CHUNK, B_WIDTH), lambda b: (b, 0)), st_spec],
        out_shape=[jax.ShapeDtypeStruct((n_seq * CHUNK, B_WIDTH), BF16),
                   jax.ShapeDtypeStruct((n_seq,) + _ST_SHAPE, F32)],
        scratch_shapes=[pltpu.VMEM(_ST_SHAPE, F32)],
        compiler_params=_params(),
        name="gla_sample",
    )(gla, la, _ltri(1), g, s0)


def _state_to_pairs(s):
    lead = s.shape[:-3]
    s = s.reshape(lead + (B_HEADS // 2, 2, B_DK, B_DV))
    s = jnp.moveaxis(s, -1, -3)
    return s.reshape(lead + (B_HEADS // 2, B_DV, 2 * B_DK))


def _pairs_to_state(s):
    lead = s.shape[:-3]
    s = s.reshape(lead + (B_HEADS // 2, B_DV, 2, B_DK))
    s = jnp.moveaxis(s, -3, -1)
    return s.reshape(lead + (B_HEADS, B_DK, B_DV))


def _route(logits):
    lane = lax.broadcasted_iota(jnp.int32, logits.shape, 1)
    lane_f = lane.astype(F32)
    big = float(LANES)
    gmask = lane < N_GROUPS
    gl = jnp.where(gmask, logits, NEG)
    gmax = jnp.max(gl, axis=-1, keepdims=True)
    gsel = jnp.min(jnp.where(gl == gmax, lane_f, big), axis=-1, keepdims=True)
    gsum = jnp.sum(jnp.where(gmask, jnp.exp(gl - gmax), 0.0), axis=-1, keepdims=True)
    g_w = 1.0 / gsum
    e_lo = ROUTE_OFF + gsel * EXPERTS_PER_GROUP
    emask = (lane_f >= e_lo) & (lane_f < e_lo + EXPERTS_PER_GROUP)
    el = jnp.where(emask, logits, NEG)
    v1 = jnp.max(el, axis=-1, keepdims=True)
    i1 = jnp.min(jnp.where(el == v1, lane_f, big), axis=-1, keepdims=True)
    el2 = jnp.where(lane_f == i1, NEG, el)
    v2 = jnp.max(el2, axis=-1, keepdims=True)
    i2 = jnp.min(jnp.where(el2 == v2, lane_f, big), axis=-1, keepdims=True)
    t = jnp.exp(v2 - v1)
    w1 = g_w / (1.0 + t)
    w2 = g_w * t / (1.0 + t)
    return lane_f, i1, i2, w1, w2


ROW_PIECES = D_MODEL // 2 // LANES
SUBLANES = 8
ROW_TILE = ROW_PIECES * SUBLANES


def _pack_rows(z32_sc, x, rows):
    half = D_MODEL // 2
    out = []
    for c in range(ROW_PIECES):
        z32_sc[c, pl.ds(0, rows, stride=2), :] = x[:, c * LANES:(c + 1) * LANES]
        z32_sc[c, pl.ds(1, rows, stride=2), :] = x[:, half + c * LANES:half + (c + 1) * LANES]
        out.append(z32_sc[c].astype(BF16))
    return out


def _unpack_rows(z32_sc, pieces, rows):
    lo, hi = [], []
    for c in range(ROW_PIECES):
        z32_sc[c] = pieces[c].astype(F32)
        lo.append(z32_sc[c, pl.ds(0, rows, stride=2), :])
        hi.append(z32_sc[c, pl.ds(1, rows, stride=2), :])
    return jnp.concatenate(lo, axis=1), jnp.concatenate(hi, axis=1)


def _to_row_tiled(pieces, tokens):
    per_tile = pieces[0].shape[0] * SUBLANES // tokens
    return jnp.stack([p.reshape(tokens // SUBLANES, per_tile, LANES) for p in pieces], axis=1)


def _from_row_tiled(flat, tokens):
    per_tile = flat.shape[0] // (tokens // SUBLANES) // ROW_PIECES
    tiled = flat.reshape(tokens // SUBLANES, ROW_PIECES, per_tile, LANES)
    return [tiled[:, c].reshape(tokens // SUBLANES * per_tile, LANES) for c in range(ROW_PIECES)]


def _flatten_tiled(tiled):
    return tiled.reshape(-1, LANES)


def _outproj_kernel(n_ptiles, tiles_per_sb, oap_ref, oas_ref, obp_ref, obs_ref, wo_ref, xp_ref, xs_ref, mod_ref,
                    gffn_ref, wr_ref, br_ref, ltri_ref, x1_ref, h2p_ref, meta_ref, cnt_ref, z32_sc, cnt_sc):
    i = pl.program_id(0)
    is_prompt = i < n_ptiles
    x = jnp.where(is_prompt, xp_ref[...], xs_ref[...])
    oa = jnp.where(is_prompt, oap_ref[...], oas_ref[...])
    ob = jnp.where(is_prompt, obp_ref[...], obs_ref[...])
    mix = _dot(oa, wo_ref[0:A_WIDTH, :]) + _dot(ob, wo_ref[A_WIDTH:D_MODEL, :])
    gate1 = _rows_to_tokens(mod_ref[:, 2 * D_MODEL:3 * D_MODEL], D_MODEL)
    x1 = x + gate1 * mix
    x1_ref[...] = x1
    ms = jnp.mean(x1 * x1, axis=-1, keepdims=True)
    xn = x1 * lax.rsqrt(ms + EPS) * gffn_ref[...]
    sh = _rows_to_tokens(mod_ref[:, 3 * D_MODEL:4 * D_MODEL], D_MODEL)
    sc = _rows_to_tokens(mod_ref[:, 4 * D_MODEL:5 * D_MODEL], D_MODEL)
    h2 = xn * (1.0 + sc) + sh
    words = [pltpu.bitcast(p, U32) for p in _pack_rows(z32_sc, h2, TOK_TILE)]
    h2p_ref[...] = _to_row_tiled(words, TOK_TILE)

    lane_f, i1, i2, w1, w2 = _route(_dot3(h2, wr_ref[...]) + br_ref[...])

    @pl.when(lax.rem(i, tiles_per_sb) == 0)
    def _():
        cnt_sc[...] = jnp.zeros_like(cnt_sc)

    sel = jnp.where((lane_f == i1) | (lane_f == i2), 1.0, 0.0).astype(BF16)
    before = _dot(ltri_ref[...], sel) + cnt_sc[0:1, :]
    rank1 = jnp.sum(jnp.where(lane_f == i1, before, 0.0), axis=-1, keepdims=True)
    rank2 = jnp.sum(jnp.where(lane_f == i2, before, 0.0), axis=-1, keepdims=True)
    cnt = cnt_sc[...] + _dot(jnp.ones((8, TOK_TILE), BF16), sel)
    cnt_sc[...] = cnt
    cnt_ref[0] = cnt
    cols = (i1, i2, rank1, rank2, w1, w2)
    meta = jnp.zeros_like(lane_f)
    for c, col in enumerate(cols):
        meta = jnp.where(lane_f == float(c), col, meta)
    meta_ref[...] = meta


def _outproj(oa_p, oa_s, ob_p, ob_s, w_out, xp, xs, mod, gffn, wr, br, n_ptiles, n_stiles, prep, sb):
    n_tiles = n_ptiles + n_stiles
    t = n_tiles * TOK_TILE
    pblocks = prep // ROWS_PER_TILE
    tiles_per_sb = sb // TOK_TILE
    const = lambda i: (0, 0)
    row = lambda i: (i, 0)
    prow = lambda i: (jnp.minimum(i, n_ptiles - 1), 0)
    srow = lambda i: (jnp.maximum(i - n_ptiles, 0), 0)
    r = np.arange(TOK_TILE)
    ltri = jnp.asarray(r[None, :] < r[:, None], BF16)
    return pl.pallas_call(
        functools.partial(_outproj_kernel, n_ptiles, tiles_per_sb),
        grid=(n_tiles,),
        in_specs=[pl.BlockSpec((TOK_TILE, A_WIDTH), prow),
                  pl.BlockSpec((TOK_TILE, A_WIDTH), srow),
                  pl.BlockSpec((TOK_TILE, B_WIDTH), prow),
                  pl.BlockSpec((TOK_TILE, B_WIDTH), srow),
                  pl.BlockSpec((D_MODEL, D_MODEL), const),
                  pl.BlockSpec((TOK_TILE, D_MODEL), prow),
                  pl.BlockSpec((TOK_TILE, D_MODEL), srow),
                  pl.BlockSpec((ROWS_PER_TILE, 6 * D_MODEL),
                               lambda i: (jnp.maximum(i - n_ptiles + pblocks, 0), 0)),
                  pl.BlockSpec((1, D_MODEL), const),
                  pl.BlockSpec((D_MODEL, LANES), const),
                  pl.BlockSpec((1, LANES), const),
                  pl.BlockSpec((TOK_TILE, TOK_TILE), const)],
        out_specs=[pl.BlockSpec((TOK_TILE, D_MODEL), row),
                   pl.BlockSpec((TOK_TILE // SUBLANES, ROW_PIECES, SUBLANES, LANES), lambda i: (i, 0, 0, 0)),
                   pl.BlockSpec((TOK_TILE, LANES), row),
                   pl.BlockSpec((1, 8, LANES), lambda i: (i // tiles_per_sb, 0, 0))],
        out_shape=[jax.ShapeDtypeStruct((t, D_MODEL), F32),
                   jax.ShapeDtypeStruct((t // SUBLANES, ROW_PIECES, SUBLANES, LANES), U32),
                   jax.ShapeDtypeStruct((t, LANES), F32),
                   jax.ShapeDtypeStruct((t // sb, 8, LANES), F32)],
        scratch_shapes=[pltpu.VMEM((D_MODEL // 2 // LANES, 2 * TOK_TILE, LANES), F32),
                        pltpu.VMEM((8, LANES), F32)],
        compiler_params=_params(),
        name="outproj",
    )(oa_p, oa_s, ob_p, ob_s, w_out, xp, xs, mod, gffn, wr, br, ltri)


MOE_SUPER_BLOCK = 2048
SEG_ALIGN = SUBLANES
CHUNK_BF16_ROWS = 2 * SEG_ALIGN * ROW_PIECES
SEG_BITS = 9
PAD_BITS = 5
FFN_ROWS = 512
PLAN_ROWS = LANES


def _local_rows(sb):
    return 2 * sb + N_EXPERTS * SEG_ALIGN


def _sorted_tiles(n_tokens, sb):
    rows = 2 * n_tokens + (n_tokens // sb) * N_EXPERTS * SEG_ALIGN + N_EXPERTS * FFN_ROWS
    return -(-rows // FFN_ROWS)


def _moe_plan_kernel(n_blocks, total_chunks, meta_ref, cnt_ref, ustrict_ref, lstrict_ref,
                     posw_ref, tab_ref, tile_ref):
    b = pl.program_id(0)
    per_tile = FFN_ROWS // SEG_ALIGN

    @pl.when(b == 0)
    def _():
        cnt = cnt_ref[...]
        chunks = jnp.floor((cnt + (SEG_ALIGN - 1)) * (1.0 / SEG_ALIGN))
        chunks_b = chunks.astype(BF16)
        loc = _dot(chunks_b, ustrict_ref[...])
        before = _dot(lstrict_ref[...], chunks_b)
        tot = _dot(jnp.ones((PLAN_ROWS, PLAN_ROWS), BF16), chunks_b)
        tiles = jnp.floor((tot + (per_tile - 1)) * (1.0 / per_tile))
        tile_off = _dot(tiles.astype(BF16), ustrict_ref[...])
        n_tiles = jnp.sum(tiles[0:1], axis=-1, keepdims=True)
        lane1 = lax.broadcasted_iota(jnp.int32, (PLAN_ROWS, LANES), 1)
        tail = lane1 == ROUTE_OFF + N_EXPERTS
        pad_off = jnp.where(tail, n_tiles * per_tile, tile_off * per_tile + tot)
        pad_n = jnp.where(tail, total_chunks - n_tiles * per_tile, tiles * per_tile - tot)
        row = lax.broadcasted_iota(jnp.int32, (PLAN_ROWS, LANES), 0)
        tab_ref[0] = loc
        tab_ref[1] = chunks
        tab_ref[2] = tile_off * per_tile + before
        tab_ref[3] = jnp.where(row == 0, pad_off, jnp.where(row == 1, pad_n, jnp.where(row == 2, n_tiles, 0.0)))
        t_idx = lax.broadcasted_iota(jnp.int32, tile_ref.shape, 0).astype(F32)
        lane_t = lax.broadcasted_iota(jnp.int32, tile_ref.shape, 1)
        is_expert = (lane_t >= ROUTE_OFF) & (lane_t < ROUTE_OFF + N_EXPERTS)
        ends = (tile_off + tiles)[0:1, :]
        owner = jnp.sum(jnp.where(is_expert & (ends <= t_idx), 1.0, 0.0), axis=-1, keepdims=True)
        tile_ref[...] = jnp.broadcast_to(jnp.minimum(owner, N_EXPERTS - 1.0), tile_ref.shape)

    own = jnp.floor((cnt_ref[pl.ds(b, 1), :] + (SEG_ALIGN - 1)) * (1.0 / SEG_ALIGN))
    own_off = _dot(jnp.broadcast_to(own, (SUBLANES, LANES)).astype(BF16), ustrict_ref[...]) * SEG_ALIGN
    meta = meta_ref[...]
    lane_f = lax.broadcasted_iota(jnp.int32, meta.shape, 1).astype(F32)
    off_row = own_off[0:1, :]
    pos = []
    for k in range(2):
        e_lane = meta[:, k:k + 1]
        base = jnp.sum(jnp.where(lane_f == e_lane, off_row, 0.0), axis=-1, keepdims=True)
        p = base + meta[:, 2 + k:3 + k]
        tile = jnp.floor(p * (1.0 / SUBLANES))
        pos.append(tile * (ROW_TILE - SUBLANES) + p)
    out = jnp.zeros_like(meta)
    for c, col in enumerate((pos[0], pos[1], meta[:, 4:5], meta[:, 5:6])):
        out = jnp.where(lane_f == float(c), col, out)
    posw_ref[...] = out


def _moe_plan(meta, cnt, sb):
    n_blocks = meta.shape[0] // sb
    assert n_blocks <= PLAN_ROWS and sb // SEG_ALIGN <= 256
    n_tiles = _sorted_tiles(meta.shape[0], sb)
    tile_rows = -(-n_tiles // SUBLANES) * SUBLANES
    r = np.arange(LANES)
    ustrict = jnp.asarray(r[:, None] < r[None, :], BF16)
    lstrict = jnp.asarray(r[None, :] < r[:, None], BF16)
    cnt_all = jnp.pad(cnt[:, 0, :], ((0, PLAN_ROWS - n_blocks), (0, 0)))
    const = lambda s: (0, 0)
    posw, tab, tile_owner = pl.pallas_call(
        functools.partial(_moe_plan_kernel, n_blocks, float(n_tiles * (FFN_ROWS // SEG_ALIGN))),
        grid=(n_blocks,),
        in_specs=[pl.BlockSpec((sb, LANES), lambda s: (s, 0)),
                  pl.BlockSpec((PLAN_ROWS, LANES), const),
                  pl.BlockSpec((LANES, LANES), const),
                  pl.BlockSpec((PLAN_ROWS, PLAN_ROWS), const)],
        out_specs=[pl.BlockSpec((sb, LANES), lambda s: (s, 0)),
                   pl.BlockSpec((4, PLAN_ROWS, LANES), lambda s: (0, 0, 0)),
                   pl.BlockSpec((tile_rows, LANES), const)],
        out_shape=[jax.ShapeDtypeStruct(meta.shape, F32),
                   jax.ShapeDtypeStruct((4, PLAN_ROWS, LANES), F32),
                   jax.ShapeDtypeStruct((tile_rows, LANES), F32)],
        compiler_params=_params(),
        name="moe_plan",
    )(meta, cnt_all, ustrict, lstrict)
    experts = slice(ROUTE_OFF, ROUTE_OFF + N_EXPERTS)
    to_i32 = lambda x: x.astype(jnp.int32).reshape(-1)
    plan = dict(
        loc=to_i32(tab[0, :n_blocks, experts]), n=to_i32(tab[1, :n_blocks, experts]),
        dst=to_i32(tab[2, :n_blocks, experts]),
        pad_off=to_i32(tab[3, 0, ROUTE_OFF:ROUTE_OFF + N_EXPERTS + 1]),
        pad_n=to_i32(tab[3, 1, ROUTE_OFF:ROUTE_OFF + N_EXPERTS + 1]),
        n_tiles=to_i32(tab[3, 2, 0:1]),
        owner=to_i32(tile_owner[:n_tiles, 0]))
    return posw, plan


def _token_rows(start):
    return pl.ds(start, ROW_PIECES, stride=SUBLANES)


def _pow2_copies(src_ref, dst_ref, src_chunk, dst_chunk, n, n_bits, sem, act):
    done = 0
    for k in reversed(range(n_bits)):
        take = (n >> k) & 1
        rows = CHUNK_BF16_ROWS << k
        src0 = 0 if src_chunk is None else pl.multiple_of((src_chunk + done) * CHUNK_BF16_ROWS, CHUNK_BF16_ROWS)
        dst0 = pl.multiple_of((dst_chunk + done) * CHUNK_BF16_ROWS, CHUNK_BF16_ROWS)

        @pl.when(take == 1)
        def _(src0=src0, dst0=dst0, rows=rows):
            act(pltpu.make_async_copy(src_ref.at[pl.ds(src0, rows)], dst_ref.at[pl.ds(dst0, rows)], sem))

        done = done + take * (1 << k)


def _segment_copies(block, loc_ref, n_ref, dst_ref, local_ref, global_ref, to_global, sem, act):
    def per_expert(e, carry):
        seg = block * N_EXPERTS + e
        if to_global:
            _pow2_copies(local_ref, global_ref, loc_ref[seg], dst_ref[seg], n_ref[seg], SEG_BITS, sem, act)
        else:
            _pow2_copies(global_ref, local_ref, dst_ref[seg], loc_ref[seg], n_ref[seg], SEG_BITS, sem, act)
        return carry

    lax.fori_loop(0, N_EXPERTS, per_expert, 0)


def _zero_fill(zero_ref, global_ref, padoff_ref, padn_ref, sem, act):
    full = 1 << PAD_BITS

    def per_pad(e, carry):
        def per_full(c, inner):
            dst0 = pl.multiple_of((padoff_ref[e] + c * full) * CHUNK_BF16_ROWS, CHUNK_BF16_ROWS)
            act(pltpu.make_async_copy(zero_ref, global_ref.at[pl.ds(dst0, full * CHUNK_BF16_ROWS)], sem))
            return inner

        n_full = padn_ref[e] >> PAD_BITS
        lax.fori_loop(0, n_full, per_full, 0)
        _pow2_copies(zero_ref, global_ref, None, padoff_ref[e] + n_full * full, padn_ref[e] & (full - 1),
                     PAD_BITS, sem, act)
        return carry

    lax.fori_loop(0, N_EXPERTS + 1, per_pad, 0)


STAGE_SLAB = 1024


def _restage(src_sc, dst_sc, dst_dtype):
    ratio = dst_sc.shape[0] / src_sc.shape[0]
    n_slabs = src_sc.shape[0] // (STAGE_SLAB if ratio > 1 else 2 * STAGE_SLAB)
    src_rows = src_sc.shape[0] // n_slabs
    dst_rows = dst_sc.shape[0] // n_slabs

    def slab(i, carry):
        s0 = pl.multiple_of(i * src_rows, src_rows)
        d0 = pl.multiple_of(i * dst_rows, dst_rows)
        dst_sc[pl.ds(d0, dst_rows), :] = pltpu.bitcast(src_sc[pl.ds(s0, src_rows), :], dst_dtype)
        return carry

    lax.fori_loop(0, n_slabs, slab, 0)


def _moe_dispatch_kernel(sb, n_blocks, loc_ref, n_ref, dst_ref, padoff_ref, padn_ref,
                         h2p_ref, a1_ref, a2_ref, xs_hbm, local_sc, stage_sc, zero_sc, sems, zero_sem):
    b = pl.program_id(0)
    slot = b & 1

    def segments(block, buf, act):
        _segment_copies(block, loc_ref, n_ref, dst_ref, stage_sc.at[buf], xs_hbm, True, sems.at[buf], act)

    local_sc[...] = jnp.zeros_like(local_sc)

    def step(g, carry):
        src = pl.multiple_of(g * ROW_TILE, ROW_TILE)
        for u in range(SUBLANES):
            t = g * SUBLANES + u
            row = h2p_ref[_token_rows(src + u), :]
            local_sc[_token_rows(a1_ref[t]), :] = row
            local_sc[_token_rows(a2_ref[t]), :] = row
        return carry

    lax.fori_loop(0, sb // SUBLANES, step, 0)

    @pl.when(b > 0)
    def _():
        segments(b - 1, 1 - slot, lambda c: c.wait())

    _restage(local_sc, stage_sc.at[slot], BF16)
    segments(b, slot, lambda c: c.start())

    @pl.when(b == 0)
    def _():
        zero_sc[...] = jnp.zeros_like(zero_sc)
        _zero_fill(zero_sc, xs_hbm, padoff_ref, padn_ref, zero_sem, lambda c: c.start())
        _zero_fill(zero_sc, xs_hbm, padoff_ref, padn_ref, zero_sem, lambda c: c.wait())

    @pl.when(b == n_blocks - 1)
    def _():
        segments(b, slot, lambda c: c.wait())


def _smem_vec(n, index_map):
    return pl.BlockSpec((n,), index_map, memory_space=pltpu.SMEM)


def _moe_dispatch(h2p, a1, a2, plan, sb, n_tiles):
    n_blocks = h2p.shape[0] // (sb * ROW_PIECES)
    local_flat = _local_rows(sb) * ROW_PIECES
    vec = _smem_vec(sb, lambda s, *_: (s,))
    return pl.pallas_call(
        functools.partial(_moe_dispatch_kernel, sb, n_blocks),
        grid_spec=pltpu.PrefetchScalarGridSpec(
            num_scalar_prefetch=5,
            grid=(n_blocks,),
            in_specs=[pl.BlockSpec((sb * ROW_PIECES, LANES), lambda s, *_: (s, 0)), vec, vec],
            out_specs=pl.BlockSpec(memory_space=pl.ANY),
            scratch_shapes=[pltpu.VMEM((local_flat, LANES), U32),
                            pltpu.VMEM((2, 2 * local_flat, LANES), BF16),
                            pltpu.VMEM(((1 << PAD_BITS) * CHUNK_BF16_ROWS, LANES), BF16),
                            pltpu.SemaphoreType.DMA((2,)),
                            pltpu.SemaphoreType.DMA(())]),
        out_shape=jax.ShapeDtypeStruct((n_tiles * FFN_ROWS * ROW_PIECES * 2, LANES), BF16),
        compiler_params=_params(),
        name="moe_dispatch",
    )(plan["loc"], plan["n"], plan["dst"], plan["pad_off"], plan["pad_n"], h2p, a1, a2)


def _moe_ffn_kernel(owner_ref, ntiles_ref, xs_ref, wg_ref, wu_ref, wd_ref, ys_ref,
                    wg_sc, wu_sc, wd_sc, z32_sc):
    i = pl.program_id(0)
    half = D_MODEL // 2
    used = i < ntiles_ref[0]

    @pl.when(used & ((i == 0) | (owner_ref[i] != owner_ref[jnp.maximum(i - 1, 0)])))
    def _():
        wg_sc[...] = wg_ref[0].astype(BF16)
        wu_sc[...] = wu_ref[0].astype(BF16)
        wd_sc[...] = wd_ref[0].astype(BF16)

    @pl.when(used)
    def _():
        lo, hi = _unpack_rows(z32_sc, _from_row_tiled(xs_ref[...], FFN_ROWS), FFN_ROWS)
        lo, hi = lo.astype(BF16), hi.astype(BF16)
        g = _dot(lo, wg_sc[0:half, :]) + _dot(hi, wg_sc[half:D_MODEL, :])
        u = _dot(lo, wu_sc[0:half, :]) + _dot(hi, wu_sc[half:D_MODEL, :])
        y = _dot((_silu(g) * u).astype(BF16), wd_sc[...])
        ys_ref[...] = _flatten_tiled(_to_row_tiled(_pack_rows(z32_sc, y, FFN_ROWS), FFN_ROWS))

    @pl.when(jnp.logical_not(used))
    def _():
        ys_ref[...] = jnp.zeros_like(ys_ref)


def _moe_ffn(xs, plan, wg, wu, wd):
    flat = FFN_ROWS * ROW_PIECES * 2
    n_tiles = xs.shape[0] // flat
    last_used = lambda i, owner, nt: jnp.minimum(i, nt[0] - 1)
    wspec = lambda shape: pl.BlockSpec((1,) + shape, lambda i, owner, nt: (owner[last_used(i, owner, nt)], 0, 0))
    return pl.pallas_call(
        _moe_ffn_kernel,
        grid_spec=pltpu.PrefetchScalarGridSpec(
            num_scalar_prefetch=2,
            grid=(n_tiles,),
            in_specs=[pl.BlockSpec((flat, LANES), lambda i, owner, nt: (last_used(i, owner, nt), 0)),
                      wspec((D_MODEL, EXPERT_FF)), wspec((D_MODEL, EXPERT_FF)), wspec((EXPERT_FF, D_MODEL))],
            out_specs=pl.BlockSpec((flat, LANES), lambda i, owner, nt: (i, 0)),
            scratch_shapes=[pltpu.VMEM((D_MODEL, EXPERT_FF), BF16),
                            pltpu.VMEM((D_MODEL, EXPERT_FF), BF16),
                            pltpu.VMEM((EXPERT_FF, D_MODEL), BF16),
                            pltpu.VMEM((ROW_PIECES, 2 * FFN_ROWS, LANES), F32)]),
        out_shape=jax.ShapeDtypeStruct(xs.shape, BF16),
        compiler_params=_params(),
        name="moe_ffn",
    )(plan["owner"], plan["n_tiles"], xs, wg, wu, wd)


def _moe_combine_kernel(n_psb, n_blocks, loc_ref, n_ref, dst_ref,
                        ys_hbm, a1_ref, a2_ref, posw_ref, x1_ref, mod_ref, yp_ref, yo_ref,
                        local_sc, stage_sc, g1_sc, g2_sc, z32_sc, sems):
    s = pl.program_id(0)
    slot = s & 1

    def segments(block, buf, act):
        _segment_copies(block, loc_ref, n_ref, dst_ref, stage_sc.at[buf], ys_hbm, False, sems.at[buf], act)

    @pl.when(pl.program_id(1) == 0)
    def _():
        @pl.when(s == 0)
        def _():
            segments(s, slot, lambda c: c.start())

        segments(s, slot, lambda c: c.wait())
        _restage(stage_sc.at[slot], local_sc, U32)

        @pl.when(s + 1 < n_blocks)
        def _():
            segments(s + 1, 1 - slot, lambda c: c.start())

    def step(g, carry):
        dst = pl.multiple_of(g * ROW_TILE, ROW_TILE)
        for u in range(SUBLANES):
            t = g * SUBLANES + u
            g1_sc[_token_rows(dst + u), :] = local_sc[_token_rows(a1_ref[t]), :]
            g2_sc[_token_rows(dst + u), :] = local_sc[_token_rows(a2_ref[t]), :]
        return carry

    lax.fori_loop(0, TOK_TILE // SUBLANES, step, 0)
    halves = lambda g_sc: [pltpu.bitcast(p, BF16) for p in _from_row_tiled(g_sc[...], TOK_TILE)]
    lo1, hi1 = _unpack_rows(z32_sc, halves(g1_sc), TOK_TILE)
    lo2, hi2 = _unpack_rows(z32_sc, halves(g2_sc), TOK_TILE)
    w1, w2 = posw_ref[:, 2:3], posw_ref[:, 3:4]
    moe = jnp.concatenate([w1 * lo1 + w2 * lo2, w1 * hi1 + w2 * hi2], axis=1)
    gate2 = _rows_to_tokens(mod_ref[:, 5 * D_MODEL:6 * D_MODEL], D_MODEL)
    y = x1_ref[...] + gate2 * moe

    @pl.when(s < n_psb)
    def _():
        yp_ref[...] = y

    @pl.when(s >= n_psb)
    def _():
        yo_ref[...] = y


def _moe_combine(ys, a1, a2, posw, plan, x1, mod, sb, n_ptiles, n_stiles, prep):
    tps = sb // TOK_TILE
    n_blocks = (n_ptiles + n_stiles) // tps
    n_psb = n_ptiles // tps
    pblocks = prep // ROWS_PER_TILE
    tile = lambda s, j: s * tps + j
    vec = _smem_vec(TOK_TILE, lambda s, j, *_: (tile(s, j),))
    return pl.pallas_call(
        functools.partial(_moe_combine_kernel, n_psb, n_blocks),
        grid_spec=pltpu.PrefetchScalarGridSpec(
            num_scalar_prefetch=3,
            grid=(n_blocks, tps),
            in_specs=[pl.BlockSpec(memory_space=pl.ANY), vec, vec,
                      pl.BlockSpec((TOK_TILE, LANES), lambda s, j, *_: (tile(s, j), 0)),
                      pl.BlockSpec((TOK_TILE, D_MODEL), lambda s, j, *_: (tile(s, j), 0)),
                      pl.BlockSpec((ROWS_PER_TILE, 6 * D_MODEL),
                                   lambda s, j, *_: (jnp.maximum(tile(s, j) - n_ptiles + pblocks, 0), 0))],
            out_specs=[pl.BlockSpec((TOK_TILE, D_MODEL),
                                    lambda s, j, *_: (jnp.minimum(tile(s, j), n_ptiles - 1), 0)),
                       pl.BlockSpec((TOK_TILE, D_MODEL),
                                    lambda s, j, *_: (jnp.maximum(tile(s, j) - n_ptiles, 0), 0))],
            scratch_shapes=[pltpu.VMEM((_local_rows(sb) * ROW_PIECES, LANES), U32),
                            pltpu.VMEM((2, _local_rows(sb) * ROW_PIECES * 2, LANES), BF16),
                            pltpu.VMEM((TOK_TILE * ROW_PIECES, LANES), U32),
                            pltpu.VMEM((TOK_TILE * ROW_PIECES, LANES), U32),
                            pltpu.VMEM((ROW_PIECES, 2 * TOK_TILE, LANES), F32),
                            pltpu.SemaphoreType.DMA((2,))]),
        out_shape=[jax.ShapeDtypeStruct((n_ptiles * TOK_TILE, D_MODEL), F32),
                   jax.ShapeDtypeStruct((n_stiles * TOK_TILE, D_MODEL), F32)],
        compiler_params=_params(2),
        name="moe_combine",
    )(plan["loc"], plan["n"], plan["dst"], ys, a1, a2, posw, x1, mod)


def _layer(xp, xs, cache_k, cache_v, state, c_prompt, c_sample, norm_mix_g, norm_ffn_g, w_ada, b_ada, w_in,
           q_norm_g, k_norm_g, rel_bias, w_gate_up, b_gate, gla_norm_g, w_out, w_route_group,
           b_route_group, w_route_expert, b_route_expert, w_exp_gate, w_exp_up, w_exp_down):
    batch, seq, _ = xp.shape
    n_seq, dec_seq, _ = xs.shape
    assert batch == 1 and dec_seq == CHUNK and cache_k.shape[1] == BAND_PAST
    assert seq % TOK_TILE == 0 and seq >= BAND_PAST and (n_seq * CHUNK) % TOK_TILE == 0
    n_ptok, n_stok = seq, n_seq * CHUNK
    n_ptiles, n_stiles = n_ptok // TOK_TILE, n_stok // TOK_TILE
    sb = MOE_SUPER_BLOCK if (n_ptok % MOE_SUPER_BLOCK == 0 and n_stok % MOE_SUPER_BLOCK == 0) else TOK_TILE
    prep = ROWS_PER_TILE

    xp2 = xp.reshape(n_ptok, D_MODEL)
    xs2 = xs.reshape(n_stok, D_MODEL)
    c_rows = jnp.concatenate([jnp.broadcast_to(c_prompt, (prep, D_MODEL)), c_sample], axis=0)
    mod = _adaln(c_rows, w_ada, b_ada)

    w_in_p = jnp.pad(w_in, ((0, 0), (0, IN_PAD - w_in.shape[1]))).astype(BF16)
    wgu_p = jnp.pad(w_gate_up, ((0, LANES - GATE_RANK), (0, 0))).astype(BF16)
    head = np.arange(A_WIDTH) // A_HEAD_DIM
    bd = jnp.asarray(head[:, None] == head[None, :], BF16)
    gq = jnp.tile(q_norm_g, A_HEADS).reshape(1, A_WIDTH)
    gk = jnp.tile(k_norm_g, A_HEADS).reshape(1, A_WIDTH)
    q, k, v, kf, vf, gla, la = _inproj(
        xp2, xs2, mod, norm_mix_g.reshape(1, D_MODEL), w_in_p, bd, gq, gk, wgu_p,
        b_gate.reshape(1, B_KWIDTH), n_ptiles, n_stiles, prep)

    first_chunk = n_ptok // CHUNK
    oa_p = _attn_prompt(rel_bias[:, _bias_lanes(ATTN_WIN * Q_ROWS)], q, k, v, n_ptok // (ATTN_SUB * Q_ROWS))
    oa_s = _attn_sample(rel_bias[:, _bias_lanes(SAMPLE_KEYS)], q, k, v,
                        cache_k.reshape(n_seq, BAND_PAST, A_WIDTH), cache_v.reshape(n_seq, BAND_PAST, A_WIDTH),
                        first_chunk, n_seq)
    g_gla = gla_norm_g.reshape(1, B_DV)
    ob_p, sfin_p = _gla_prompt(gla, la, g_gla, n_ptok // (GLA_SUB * GLA_CHUNKS * CHUNK))
    ob_s, sfin_s = _gla_sample(gla, la, g_gla, _state_to_pairs(state), first_chunk, n_seq)

    wr = jnp.pad(jnp.concatenate([w_route_group, w_route_expert], axis=1),
                 ((0, 0), (0, LANES - N_GROUPS - N_EXPERTS)))
    br = jnp.pad(jnp.concatenate([b_route_group, b_route_expert]), (0, LANES - N_GROUPS - N_EXPERTS))
    x1, h2p, meta, cnt = _outproj(oa_p, oa_s, ob_p, ob_s, w_out.astype(BF16), xp2, xs2, mod,
                                  norm_ffn_g.reshape(1, D_MODEL), wr, br.reshape(1, LANES),
                                  n_ptiles, n_stiles, prep, sb)

    posw, plan = _moe_plan(meta, cnt, sb)
    a1, a2 = posw[:, 0].astype(jnp.int32), posw[:, 1].astype(jnp.int32)
    xs_sorted = _moe_dispatch(h2p.reshape(-1, LANES), a1, a2, plan, sb, _sorted_tiles(n_ptok + n_stok, sb))
    ys_sorted = _moe_ffn(xs_sorted, plan, w_exp_gate, w_exp_up, w_exp_down)
    yp, ys = _moe_combine(ys_sorted, a1, a2, posw, plan, x1, mod, sb, n_ptiles, n_stiles, prep)

    tail = min(BAND_PAST, seq)
    heads = (A_HEADS, A_HEAD_DIM)
    return (yp.reshape(1, seq, D_MODEL), ys.reshape(n_seq, CHUNK, D_MODEL),
            kf[TOK_TILE - tail:TOK_TILE].reshape((1, tail) + heads),
            vf[TOK_TILE - tail:TOK_TILE].reshape((1, tail) + heads),
            _pairs_to_state(sfin_p)[None],
            kf[TOK_TILE:].reshape((n_seq, CHUNK) + heads),
            vf[TOK_TILE:].reshape((n_seq, CHUNK) + heads),
            _pairs_to_state(sfin_s))


def kernel(x_prompt, x_sample, cache_a_k, cache_a_v, state_gla, c_prompt, c_sample, norm_mix_g, norm_ffn_g,
           w_ada, b_ada, w_in, q_norm_g, k_norm_g, rel_bias, w_gate_up, b_gate, gla_norm_g, w_out,
           w_route_group, b_route_group, w_route_expert, b_route_expert, w_exp_gate, w_exp_up, w_exp_down):
    depth = w_in.shape[0]
    yp, ys = x_prompt, x_sample
    outs = [[] for _ in range(6)]
    for l in range(depth):
        yp, ys, kp, vp, sp, ks, vs, ss = _layer(
            yp, ys, cache_a_k[l], cache_a_v[l], state_gla[l], c_prompt, c_sample, norm_mix_g[l], norm_ffn_g[l],
            w_ada[l], b_ada[l], w_in[l], q_norm_g[l], k_norm_g[l], rel_bias[l], w_gate_up[l], b_gate[l],
            gla_norm_g[l], w_out[l], w_route_group[l], b_route_group[l], w_route_expert[l], b_route_expert[l],
            w_exp_gate[l], w_exp_up[l], w_exp_down[l])
        for lst, val in zip(outs, (kp, vp, sp, ks, vs, ss)):
            lst.append(val)
    return (yp, ys) + tuple(jnp.stack(o) for o in outs)
```

```python
import functools

import numpy as np
import jax
import jax.numpy as jnp
from jax import lax
from jax.experimental import pallas as pl
from jax.experimental.pallas import tpu as pltpu

F32 = jnp.float32
BF16 = jnp.bfloat16
U32 = jnp.uint32

D_MODEL = 1024
CHUNK = 64
LOG_CHUNK = 6
BAND_CHUNKS = 8
BAND_PAST = BAND_CHUNKS * CHUNK
A_WIDTH = 512
A_HEADS = 8
A_HEAD_DIM = 64
MAX_REL = 128
N_REL = CHUNK + MAX_REL
B_WIDTH = 512
B_HEADS = 4
B_DV = 128
B_DK = 64
B_KWIDTH = 256
GATE_RANK = 16
GATE_TAU = 16.0
N_GROUPS = 4
EXPERTS_PER_GROUP = 8
N_EXPERTS = 32
EXPERT_FF = 256
EPS = 1e-6

LANES = 128
IN_MAIN = 3 * A_WIDTH + 2 * B_KWIDTH + 2 * B_WIDTH
IN_PAD = IN_MAIN + LANES
TOK_TILE = 512
ROWS_PER_TILE = TOK_TILE // CHUNK
Q_CHUNKS = 4
Q_ROWS = Q_CHUNKS * CHUNK
ROLL_W = 1024
NEG = -1e30
ROUTE_OFF = N_GROUPS
VMEM_LIMIT = 56 * 1024 * 1024


def _params(n_axes=1):
    return pltpu.CompilerParams(dimension_semantics=("arbitrary",) * n_axes,
                                vmem_limit_bytes=VMEM_LIMIT)


def _split(a):
    hi = a.astype(BF16)
    lo = (a - hi.astype(F32)).astype(BF16)
    return hi, lo


def _dot(a, b):
    return jnp.dot(a, b, preferred_element_type=F32)


def _dot3(a, b):
    ah, al = _split(a)
    bh, bl = _split(b)
    return _dot(ah, bh) + _dot(al, bh) + _dot(ah, bl)


def _dot_nt(a, b):
    return lax.dot_general(a, b, (((1,), (1,)), ((), ())), preferred_element_type=F32)


def _dot_tn(a, b):
    return lax.dot_general(a, b, (((0,), (0,)), ((), ())), preferred_element_type=F32)


def _silu(x):
    return x / (1.0 + jnp.exp(-x))


def _rows_to_tokens(rows, n):
    r = rows.shape[0]
    return jnp.broadcast_to(rows[:, None, :], (r, CHUNK, n)).reshape(r * CHUNK, n)


def _adaln_kernel(c_ref, w_ref, b_ref, o_ref):
    a = _silu(c_ref[...])
    o_ref[...] = _dot3(a, w_ref[...]) + b_ref[...]


def _adaln(c_rows, w_ada, b_ada):
    r = c_rows.shape[0]
    n = w_ada.shape[1]
    tn = 1024
    return pl.pallas_call(
        _adaln_kernel,
        grid=(n // tn,),
        in_specs=[pl.BlockSpec((r, D_MODEL), lambda j: (0, 0)),
                  pl.BlockSpec((D_MODEL, tn), lambda j: (0, j)),
                  pl.BlockSpec((1, tn), lambda j: (0, j))],
        out_specs=pl.BlockSpec((r, tn), lambda j: (0, j)),
        out_shape=jax.ShapeDtypeStruct((r, n), F32),
        compiler_params=_params(),
        name="adaln",
    )(c_rows, w_ada, b_ada.reshape(1, n))


def _head_rms(z, bd_ref, g):
    ms = _dot((z * z).astype(BF16), bd_ref[...]) * (1.0 / A_HEAD_DIM)
    return z * lax.rsqrt(ms + EPS) * g


def _inproj_kernel(n_ptiles, xp_ref, xs_ref, mod_ref, gmix_ref, w_ref, bd_ref, gq_ref, gk_ref,
                   wgu_ref, bg_ref,
                   q_ref, k_ref, v_ref, kf_ref, vf_ref, gla_ref, la_ref):
    i = pl.program_id(0)
    x = jnp.where(i < n_ptiles, xp_ref[...], xs_ref[...])
    ms = jnp.mean(x * x, axis=-1, keepdims=True)
    xn = x * lax.rsqrt(ms + EPS) * gmix_ref[...]
    sh = _rows_to_tokens(mod_ref[:, 0:D_MODEL], D_MODEL)
    sc = _rows_to_tokens(mod_ref[:, D_MODEL:2 * D_MODEL], D_MODEL)
    hb = (xn * (1.0 + sc) + sh).astype(BF16)

    zq = _dot(hb, w_ref[:, 0:A_WIDTH])
    q_ref[...] = (_head_rms(zq, bd_ref, gq_ref[...]) * (LOG2E * A_HEAD_DIM ** -0.5)).astype(BF16)
    zk = _dot(hb, w_ref[:, A_WIDTH:2 * A_WIDTH])
    kn = _head_rms(zk, bd_ref, gk_ref[...])
    k_ref[...] = kn.astype(BF16)
    kf_ref[...] = kn
    zv = _dot(hb, w_ref[:, 2 * A_WIDTH:3 * A_WIDTH])
    v_ref[...] = zv.astype(BF16)
    vf_ref[...] = zv

    o = 3 * A_WIDTH
    zqb = _dot(hb, w_ref[:, o:o + B_KWIDTH]) * (B_DK ** -0.5)
    gla_ref[:, 0:B_KWIDTH] = zqb.astype(BF16)
    for c in range(B_KWIDTH, 2 * B_KWIDTH + 2 * B_WIDTH, 256):
        gla_ref[:, c:c + 256] = _dot(hb, w_ref[:, o + c:o + c + 256]).astype(BF16)

    gr = _dot(hb, w_ref[:, IN_MAIN:IN_PAD])
    logit = _dot(gr.astype(BF16), wgu_ref[...]) + bg_ref[...]
    log_sig = jnp.minimum(logit, 0.0) - jnp.log1p(jnp.exp(-jnp.abs(logit)))
    la_ref[...] = log_sig * (1.0 / GATE_TAU)


def _inproj(xp, xs, mod, gmix, w_in_p, bd, gq, gk, wgu_p, bg, n_ptiles, n_stiles, prep):
    n_tiles = n_ptiles + n_stiles
    t = n_tiles * TOK_TILE
    tail_tiles = 1 + n_stiles
    pblocks = prep // ROWS_PER_TILE
    const = lambda i: (0, 0)
    row = lambda i: (i, 0)
    tail = lambda i: (jnp.maximum(i - (n_ptiles - 1), 0), 0)
    return pl.pallas_call(
        functools.partial(_inproj_kernel, n_ptiles),
        grid=(n_tiles,),
        in_specs=[pl.BlockSpec((TOK_TILE, D_MODEL), lambda i: (jnp.minimum(i, n_ptiles - 1), 0)),
                  pl.BlockSpec((TOK_TILE, D_MODEL), lambda i: (jnp.maximum(i - n_ptiles, 0), 0)),
                  pl.BlockSpec((ROWS_PER_TILE, 6 * D_MODEL),
                               lambda i: (jnp.maximum(i - n_ptiles + pblocks, 0), 0)),
                  pl.BlockSpec((1, D_MODEL), const),
                  pl.BlockSpec((D_MODEL, IN_PAD), const),
                  pl.BlockSpec((A_WIDTH, A_WIDTH), const),
                  pl.BlockSpec((1, A_WIDTH), const),
                  pl.BlockSpec((1, A_WIDTH), const),
                  pl.BlockSpec((LANES, B_KWIDTH), const),
                  pl.BlockSpec((1, B_KWIDTH), const)],
        out_specs=[pl.BlockSpec((TOK_TILE, A_WIDTH), row),
                   pl.BlockSpec((TOK_TILE, A_WIDTH), row),
                   pl.BlockSpec((TOK_TILE, A_WIDTH), row),
                   pl.BlockSpec((TOK_TILE, A_WIDTH), tail),
                   pl.BlockSpec((TOK_TILE, A_WIDTH), tail),
                   pl.BlockSpec((TOK_TILE, 2 * B_KWIDTH + 2 * B_WIDTH), row),
                   pl.BlockSpec((TOK_TILE, B_KWIDTH), row)],
        out_shape=[jax.ShapeDtypeStruct((t, A_WIDTH), BF16),
                   jax.ShapeDtypeStruct((t, A_WIDTH), BF16),
                   jax.ShapeDtypeStruct((t, A_WIDTH), BF16),
                   jax.ShapeDtypeStruct((tail_tiles * TOK_TILE, A_WIDTH), F32),
                   jax.ShapeDtypeStruct((tail_tiles * TOK_TILE, A_WIDTH), F32),
                   jax.ShapeDtypeStruct((t, 2 * B_KWIDTH + 2 * B_WIDTH), BF16),
                   jax.ShapeDtypeStruct((t, B_KWIDTH), F32)],
        compiler_params=_params(),
        name="inproj",
    )(xp, xs, mod, gmix, w_in_p, bd, gq, gk, wgu_p, bg)


def _bias_lanes(n_keys):
    l = np.arange(ROLL_W)
    d = np.where(l < n_keys, BAND_PAST - l, BAND_PAST - l + ROLL_W)
    return np.clip(d, -(CHUNK - 1), MAX_REL) + (CHUNK - 1)


LOG2E = 1.4426950408889634


def _band_mask(m_rows, n_keys, first_col):
    qi = lax.broadcasted_iota(jnp.int32, (m_rows, n_keys), 0) >> LOG_CHUNK
    kw = lax.broadcasted_iota(jnp.int32, (m_rows, n_keys), 1)
    kc = kw >> LOG_CHUNK
    return (kc >= qi) & (kc <= qi + BAND_CHUNKS) & (kw >= first_col)


def _bias_tile(u_ref, h, ok):
    m_rows, n_keys = ok.shape
    src = jnp.broadcast_to(u_ref[h:h + 1, :] * LOG2E, (m_rows, ROLL_W))
    toe = pltpu.roll(src, 0, 1, stride=1, stride_axis=0)
    return jnp.where(ok, toe[:, 0:n_keys], NEG)


def _attend(q, kcat, vcat, bias_sc):
    m_rows = q.shape[0]
    first = lax.broadcasted_iota(jnp.int32, (m_rows, LANES), 1) < A_HEAD_DIM
    outs = []
    for p in range(A_HEADS // 2):
        lanes = slice(p * LANES, (p + 1) * LANES)
        qp, kp, vp = q[:, lanes], kcat[:, lanes], vcat[:, lanes]
        zero = jnp.zeros_like(qp)
        q2 = jnp.concatenate([jnp.where(first, qp, zero), jnp.where(first, zero, qp)], axis=0)
        s = _dot_nt(q2, kp) + bias_sc[p]
        e = jnp.exp2(s - jnp.max(s, axis=-1, keepdims=True))
        l = jnp.sum(e, axis=-1, keepdims=True)
        o2 = _dot(e.astype(BF16), vp) / l
        outs.append(jnp.where(first, o2[0:m_rows], o2[m_rows:2 * m_rows]))
    return jnp.concatenate(outs, axis=-1)


ATTN_SUB = 4
ATTN_WIN = 3


def _attn_prompt_kernel(u_ref, q_ref, *refs):
    k_refs = refs[0:ATTN_SUB + ATTN_WIN - 1]
    v_refs = refs[ATTN_SUB + ATTN_WIN - 1:2 * (ATTN_SUB + ATTN_WIN - 1)]
    o_ref, bias_sc = refs[-2:]
    j = pl.program_id(0)
    n_keys = ATTN_WIN * Q_ROWS

    @pl.when(j == 0)
    def _():
        for g in range(ATTN_WIN):
            ok = _band_mask(Q_ROWS, n_keys, (ATTN_WIN - 1 - g) * Q_ROWS)
            for h in range(A_HEADS):
                bias_sc[g, h // 2, (h % 2) * Q_ROWS:(h % 2 + 1) * Q_ROWS, :] = _bias_tile(u_ref, h, ok)

    ks = [r[...] for r in k_refs]
    vs = [r[...] for r in v_refs]
    for sub in range(ATTN_SUB):
        rows = slice(sub * Q_ROWS, (sub + 1) * Q_ROWS)
        kcat = jnp.concatenate(ks[sub:sub + ATTN_WIN], axis=0)
        vcat = jnp.concatenate(vs[sub:sub + ATTN_WIN], axis=0)
        bias = bias_sc.at[jnp.minimum(ATTN_SUB * j + sub, ATTN_WIN - 1)]
        o_ref[rows, :] = _attend(q_ref[rows, :], kcat, vcat, bias).astype(BF16)


def _attn_prompt(u, q, k, v, n_steps):
    const = lambda j: (0, 0)
    n_blk = ATTN_SUB + ATTN_WIN - 1
    blk = lambda d: pl.BlockSpec((Q_ROWS, A_WIDTH),
                                 lambda j, d=d: (jnp.maximum(ATTN_SUB * j - (ATTN_WIN - 1) + d, 0), 0))
    step_rows = ATTN_SUB * Q_ROWS
    return pl.pallas_call(
        _attn_prompt_kernel,
        grid=(n_steps,),
        in_specs=[pl.BlockSpec((A_HEADS, ROLL_W), const), pl.BlockSpec((step_rows, A_WIDTH), lambda j: (j, 0))]
                 + [blk(d) for d in range(n_blk)] * 2,
        out_specs=pl.BlockSpec((step_rows, A_WIDTH), lambda j: (j, 0)),
        out_shape=jax.ShapeDtypeStruct((n_steps * step_rows, A_WIDTH), BF16),
        scratch_shapes=[pltpu.VMEM((ATTN_WIN, A_HEADS // 2, 2 * Q_ROWS, ATTN_WIN * Q_ROWS), F32)],
        compiler_params=_params(),
        name="attn_prompt",
    )(u, q, *([k] * n_blk), *([v] * n_blk))


SAMPLE_KEYS = BAND_PAST + 2 * CHUNK


def _attn_sample_kernel(u_ref, q_ref, kn_ref, vn_ref, kc_ref, vc_ref, o_ref, bias_sc):
    @pl.when(pl.program_id(0) == 0)
    def _():
        ok = _band_mask(CHUNK, SAMPLE_KEYS, 0)
        for p in range(A_HEADS // 2):
            pair = jnp.concatenate([_bias_tile(u_ref, 2 * p, ok), _bias_tile(u_ref, 2 * p + 1, ok)], axis=0)
            bias_sc[p] = pair.T

    pad = jnp.zeros((CHUNK, A_WIDTH), BF16)
    kcat = jnp.concatenate([kc_ref[0].astype(BF16), kn_ref[...], pad], axis=0)
    vcat = jnp.concatenate([vc_ref[0].astype(BF16), vn_ref[...], pad], axis=0)
    q = q_ref[...]
    lane = lax.broadcasted_iota(jnp.int32, (CHUNK, LANES), 1)
    first = lane < A_HEAD_DIM
    zero = jnp.zeros((CHUNK, LANES), BF16)
    outs = []
    for p in range(A_HEADS // 2):
        lanes = slice(p * LANES, (p + 1) * LANES)
        qp = q[:, lanes]
        q_rows = jnp.concatenate([jnp.where(first, qp, zero), jnp.where(first, zero, qp)], axis=0)
        s = _dot_nt(kcat[:, lanes], q_rows) + bias_sc[p]
        e = jnp.exp2(s - jnp.max(s, axis=0, keepdims=True))
        pn = (e * (1.0 / jnp.sum(e, axis=0, keepdims=True))).astype(BF16)
        r = _dot_tn(pn, vcat[:, lanes])
        outs.append(jnp.where(first, r[0:CHUNK], r[CHUNK:2 * CHUNK]))
    o_ref[...] = jnp.concatenate(outs, axis=-1).astype(BF16)


def _attn_sample(u, q, k, v, kc, vc, first_chunk, n_seq):
    new = pl.BlockSpec((CHUNK, A_WIDTH), lambda b: (first_chunk + b, 0))
    cache = pl.BlockSpec((1, BAND_PAST, A_WIDTH), lambda b: (b, 0, 0))
    return pl.pallas_call(
        _attn_sample_kernel,
        grid=(n_seq,),
        in_specs=[pl.BlockSpec((A_HEADS, ROLL_W), lambda b: (0, 0)), new, new, new, cache, cache],
        out_specs=pl.BlockSpec((CHUNK, A_WIDTH), lambda b: (b, 0)),
        out_shape=jax.ShapeDtypeStruct((n_seq * CHUNK, A_WIDTH), BF16),
        scratch_shapes=[pltpu.VMEM((A_HEADS // 2, SAMPLE_KEYS, LANES), F32)],
        compiler_params=_params(),
        name="attn_sample",
    )(u, q, k, v, kc, vc)


GLA_CHUNKS = 4
GLA_SUB = 4


def _gla_block(n_chunks, gla_ref, la_ref, ltri_ref, g_ref, st_sc, o_ref):
    rows = n_chunks * CHUNK
    la = la_ref[...]
    la_hi, la_lo = _split(la)
    b = _dot(ltri_ref[...], la_hi) + _dot(ltri_ref[...], la_lo)
    b3 = b.reshape(n_chunks, CHUNK, B_KWIDTH)
    b_mid = b3[:, CHUNK // 2 - 1:CHUNK // 2, :]
    b_last = b3[:, CHUNK - 1:CHUNK, :]
    q = gla_ref[:, 0:B_KWIDTH].astype(F32).reshape(n_chunks, CHUNK, B_KWIDTH)
    k = gla_ref[:, B_KWIDTH:2 * B_KWIDTH].astype(F32).reshape(n_chunks, CHUNK, B_KWIDTH)
    q_start = (q * jnp.exp(b3)).reshape(rows, B_KWIDTH).astype(BF16)
    q_mid = (q * jnp.exp(b3 - b_mid)).reshape(rows, B_KWIDTH).astype(BF16)
    k_mid = (k * jnp.exp(b_mid - b3)).reshape(rows, B_KWIDTH).astype(BF16)
    k_end = (k * jnp.exp(b_last - b3)).reshape(rows, B_KWIDTH).astype(BF16)
    dec = jnp.exp(b_last)

    ti = lax.broadcasted_iota(jnp.int32, (2 * rows, rows), 0) & (rows - 1)
    si = lax.broadcasted_iota(jnp.int32, (2 * rows, rows), 1)
    causal = (si <= ti) & ((si >> LOG_CHUNK) == (ti >> LOG_CHUNK))
    first_r = lax.broadcasted_iota(jnp.int32, (rows, LANES), 1) < B_DK
    first_c = lax.broadcasted_iota(jnp.int32, (CHUNK, LANES), 1) < B_DK
    first_s = lax.broadcasted_iota(jnp.int32, (B_DV, LANES), 1) < B_DK

    def stack_heads(x, first):
        zero = jnp.zeros_like(x)
        return jnp.concatenate([jnp.where(first, x, zero), jnp.where(first, zero, x)], axis=0)

    for p in range(B_HEADS // 2):
        lanes = slice(p * LANES, (p + 1) * LANES)
        qs_p, qm_p, km_p, ke_p = q_start[:, lanes], q_mid[:, lanes], k_mid[:, lanes], k_end[:, lanes]
        v_pair = gla_ref[:, 2 * B_KWIDTH + 2 * p * B_DV:2 * B_KWIDTH + (2 * p + 2) * B_DV]
        sc = jnp.where(causal, _dot_nt(stack_heads(qm_p, first_r), km_p), 0.0)
        o2 = _dot(sc.astype(BF16), v_pair)
        intra = [o2[0:rows, 0:B_DV], o2[rows:2 * rows, B_DV:2 * B_DV]]
        inter = [[], []]
        st = st_sc[p]
        for c in range(n_chunks):
            cr = slice(c * CHUNK, (c + 1) * CHUNK)
            r2 = _dot_nt(stack_heads(qs_p[cr], first_c), st.astype(BF16))
            inter[0].append(r2[0:CHUNK])
            inter[1].append(r2[CHUNK:2 * CHUNK])
            u2 = _dot_tn(v_pair[cr], ke_p[cr])
            st = st * dec[c, :, lanes] + jnp.where(first_s, u2[0:B_DV], u2[B_DV:2 * B_DV])
        st_sc[p] = st
        for hh in range(2):
            h = 2 * p + hh
            o = intra[hh] + jnp.concatenate(inter[hh], axis=0)
            ms = jnp.mean(o * o, axis=-1, keepdims=True)
            on = o * lax.rsqrt(ms + EPS) * g_ref[...]
            r = gla_ref[:, 2 * B_KWIDTH + B_WIDTH + h * B_DV:2 * B_KWIDTH + B_WIDTH + (h + 1) * B_DV]
            o_ref[:, h * B_DV:(h + 1) * B_DV] = (on * _silu(r.astype(F32))).astype(BF16)


def _gla_prompt_kernel(gla_ref, la_ref, ltri_ref, g_ref, o_ref, sfin_ref, st_sc):
    @pl.when(pl.program_id(0) == 0)
    def _():
        st_sc[...] = jnp.zeros_like(st_sc)

    rows = GLA_CHUNKS * CHUNK
    for sub in range(GLA_SUB):
        part = pl.ds(sub * rows, rows)
        _gla_block(GLA_CHUNKS, gla_ref.at[part], la_ref.at[part], ltri_ref, g_ref, st_sc, o_ref.at[part])
    sfin_ref[...] = st_sc[...]


def _gla_sample_kernel(gla_ref, la_ref, ltri_ref, g_ref, s0_ref, o_ref, sfin_ref, st_sc):
    st_sc[...] = s0_ref[0]
    _gla_block(1, gla_ref, la_ref, ltri_ref, g_ref, st_sc, o_ref)
    sfin_ref[0] = st_sc[...]


def _ltri(n_chunks):
    r = np.arange(n_chunks * CHUNK)
    m = (r[None, :] <= r[:, None]) & (r[None, :] // CHUNK == r[:, None] // CHUNK)
    return jnp.asarray(m, BF16)


_GLA_W = 2 * B_KWIDTH + 2 * B_WIDTH
_ST_SHAPE = (B_HEADS // 2, B_DV, LANES)


def _gla_prompt(gla, la, g, n_steps):
    rows = GLA_SUB * GLA_CHUNKS * CHUNK
    const = lambda j: (0, 0)
    return pl.pallas_call(
        _gla_prompt_kernel,
        grid=(n_steps,),
        in_specs=[pl.BlockSpec((rows, _GLA_W), lambda j: (j, 0)),
                  pl.BlockSpec((rows, B_KWIDTH), lambda j: (j, 0)),
                  pl.BlockSpec((GLA_CHUNKS * CHUNK, GLA_CHUNKS * CHUNK), const),
                  pl.BlockSpec((1, B_DV), const)],
        out_specs=[pl.BlockSpec((rows, B_WIDTH), lambda j: (j, 0)),
                   pl.BlockSpec(_ST_SHAPE, lambda j: (0, 0, 0))],
        out_shape=[jax.ShapeDtypeStruct((n_steps * rows, B_WIDTH), BF16),
                   jax.ShapeDtypeStruct(_ST_SHAPE, F32)],
        scratch_shapes=[pltpu.VMEM(_ST_SHAPE, F32)],
        compiler_params=_params(),
        name="gla_prompt",
    )(gla, la, _ltri(GLA_CHUNKS), g)


def _gla_sample(gla, la, g, s0, first_chunk, n_seq):
    const = lambda b: (0, 0)
    st_spec = pl.BlockSpec((1,) + _ST_SHAPE, lambda b: (b, 0, 0, 0))
    return pl.pallas_call(
        _gla_sample_kernel,
        grid=(n_seq,),
        in_specs=[pl.BlockSpec((CHUNK, _GLA_W), lambda b: (first_chunk + b, 0)),
                  pl.BlockSpec((CHUNK, B_KWIDTH), lambda b: (first_chunk + b, 0)),
                  pl.BlockSpec((CHUNK, CHUNK), const),
                  pl.BlockSpec((1, B_DV), const),
                  st_spec],
        out_specs=[pl.BlockSpec((CHUNK, B_WIDTH), lambda b: (b, 0)), st_spec],
        out_shape=[jax.ShapeDtypeStruct((n_seq * CHUNK, B_WIDTH), BF16),
                   jax.ShapeDtypeStruct((n_seq,) + _ST_SHAPE, F32)],
        scratch_shapes=[pltpu.VMEM(_ST_SHAPE, F32)],
        compiler_params=_params(),
        name="gla_sample",
    )(gla, la, _ltri(1), g, s0)


def _state_to_pairs(s):
    lead = s.shape[:-3]
    s = s.reshape(lead + (B_HEADS // 2, 2, B_DK, B_DV))
    s = jnp.moveaxis(s, -1, -3)
    return s.reshape(lead + (B_HEADS // 2, B_DV, 2 * B_DK))


def _pairs_to_state(s):
    lead = s.shape[:-3]
    s = s.reshape(lead + (B_HEADS // 2, B_DV, 2, B_DK))
    s = jnp.moveaxis(s, -3, -1)
    return s.reshape(lead + (B_HEADS, B_DK, B_DV))


def _route(logits):
    lane = lax.broadcasted_iota(jnp.int32, logits.shape, 1)
    lane_f = lane.astype(F32)
    big = float(LANES)
    gmask = lane < N_GROUPS
    gl = jnp.where(gmask, logits, NEG)
    gmax = jnp.max(gl, axis=-1, keepdims=True)
    gsel = jnp.min(jnp.where(gl == gmax, lane_f, big), axis=-1, keepdims=True)
    gsum = jnp.sum(jnp.where(gmask, jnp.exp(gl - gmax), 0.0), axis=-1, keepdims=True)
    g_w = 1.0 / gsum
    e_lo = ROUTE_OFF + gsel * EXPERTS_PER_GROUP
    emask = (lane_f >= e_lo) & (lane_f < e_lo + EXPERTS_PER_GROUP)
    el = jnp.where(emask, logits, NEG)
    v1 = jnp.max(el, axis=-1, keepdims=True)
    i1 = jnp.min(jnp.where(el == v1, lane_f, big), axis=-1, keepdims=True)
    el2 = jnp.where(lane_f == i1, NEG, el)
    v2 = jnp.max(el2, axis=-1, keepdims=True)
    i2 = jnp.min(jnp.where(el2 == v2, lane_f, big), axis=-1, keepdims=True)
    t = jnp.exp(v2 - v1)
    w1 = g_w / (1.0 + t)
    w2 = g_w * t / (1.0 + t)
    return lane_f, i1, i2, w1, w2


ROW_PIECES = D_MODEL // 2 // LANES
SUBLANES = 8
ROW_TILE = ROW_PIECES * SUBLANES


def _pack_rows(z32_sc, x, rows):
    half = D_MODEL // 2
    out = []
    for c in range(ROW_PIECES):
        z32_sc[c, pl.ds(0, rows, stride=2), :] = x[:, c * LANES:(c + 1) * LANES]
        z32_sc[c, pl.ds(1, rows, stride=2), :] = x[:, half + c * LANES:half + (c + 1) * LANES]
        out.append(z32_sc[c].astype(BF16))
    return out


def _unpack_rows(z32_sc, pieces, rows):
    lo, hi = [], []
    for c in range(ROW_PIECES):
        z32_sc[c] = pieces[c].astype(F32)
        lo.append(z32_sc[c, pl.ds(0, rows, stride=2), :])
        hi.append(z32_sc[c, pl.ds(1, rows, stride=2), :])
    return jnp.concatenate(lo, axis=1), jnp.concatenate(hi, axis=1)


def _to_row_tiled(pieces, tokens):
    per_tile = pieces[0].shape[0] * SUBLANES // tokens
    return jnp.stack([p.reshape(tokens // SUBLANES, per_tile, LANES) for p in pieces], axis=1)


def _from_row_tiled(flat, tokens):
    per_tile = flat.shape[0] // (tokens // SUBLANES) // ROW_PIECES
    tiled = flat.reshape(tokens // SUBLANES, ROW_PIECES, per_tile, LANES)
    return [tiled[:, c].reshape(tokens // SUBLANES * per_tile, LANES) for c in range(ROW_PIECES)]


def _flatten_tiled(tiled):
    return tiled.reshape(-1, LANES)


def _outproj_kernel(n_ptiles, tiles_per_sb, oap_ref, oas_ref, obp_ref, obs_ref, wo_ref, xp_ref, xs_ref, mod_ref,
                    gffn_ref, wr_ref, br_ref, ltri_ref, x1_ref, h2p_ref, meta_ref, cnt_ref, z32_sc, cnt_sc):
    i = pl.program_id(0)
    is_prompt = i < n_ptiles
    x = jnp.where(is_prompt, xp_ref[...], xs_ref[...])
    oa = jnp.where(is_prompt, oap_ref[...], oas_ref[...])
    ob = jnp.where(is_prompt, obp_ref[...], obs_ref[...])
    mix = _dot(oa, wo_ref[0:A_WIDTH, :]) + _dot(ob, wo_ref[A_WIDTH:D_MODEL, :])
    gate1 = _rows_to_tokens(mod_ref[:, 2 * D_MODEL:3 * D_MODEL], D_MODEL)
    x1 = x + gate1 * mix
    x1_ref[...] = x1
    ms = jnp.mean(x1 * x1, axis=-1, keepdims=True)
    xn = x1 * lax.rsqrt(ms + EPS) * gffn_ref[...]
    sh = _rows_to_tokens(mod_ref[:, 3 * D_MODEL:4 * D_MODEL], D_MODEL)
    sc = _rows_to_tokens(mod_ref[:, 4 * D_MODEL:5 * D_MODEL], D_MODEL)
    h2 = xn * (1.0 + sc) + sh
    words = [pltpu.bitcast(p, U32) for p in _pack_rows(z32_sc, h2, TOK_TILE)]
    h2p_ref[...] = _to_row_tiled(words, TOK_TILE)

    lane_f, i1, i2, w1, w2 = _route(_dot3(h2, wr_ref[...]) + br_ref[...])

    @pl.when(lax.rem(i, tiles_per_sb) == 0)
    def _():
        cnt_sc[...] = jnp.zeros_like(cnt_sc)

    sel = jnp.where((lane_f == i1) | (lane_f == i2), 1.0, 0.0).astype(BF16)
    before = _dot(ltri_ref[...], sel) + cnt_sc[0:1, :]
    rank1 = jnp.sum(jnp.where(lane_f == i1, before, 0.0), axis=-1, keepdims=True)
    rank2 = jnp.sum(jnp.where(lane_f == i2, before, 0.0), axis=-1, keepdims=True)
    cnt = cnt_sc[...] + _dot(jnp.ones((8, TOK_TILE), BF16), sel)
    cnt_sc[...] = cnt
    cnt_ref[0] = cnt
    cols = (i1, i2, rank1, rank2, w1, w2)
    meta = jnp.zeros_like(lane_f)
    for c, col in enumerate(cols):
        meta = jnp.where(lane_f == float(c), col, meta)
    meta_ref[...] = meta


def _outproj(oa_p, oa_s, ob_p, ob_s, w_out, xp, xs, mod, gffn, wr, br, n_ptiles, n_stiles, prep, sb):
    n_tiles = n_ptiles + n_stiles
    t = n_tiles * TOK_TILE
    pblocks = prep // ROWS_PER_TILE
    tiles_per_sb = sb // TOK_TILE
    const = lambda i: (0, 0)
    row = lambda i: (i, 0)
    prow = lambda i: (jnp.minimum(i, n_ptiles - 1), 0)
    srow = lambda i: (jnp.maximum(i - n_ptiles, 0), 0)
    r = np.arange(TOK_TILE)
    ltri = jnp.asarray(r[None, :] < r[:, None], BF16)
    return pl.pallas_call(
        functools.partial(_outproj_kernel, n_ptiles, tiles_per_sb),
        grid=(n_tiles,),
        in_specs=[pl.BlockSpec((TOK_TILE, A_WIDTH), prow),
                  pl.BlockSpec((TOK_TILE, A_WIDTH), srow),
                  pl.BlockSpec((TOK_TILE, B_WIDTH), prow),
                  pl.BlockSpec((TOK_TILE, B_WIDTH), srow),
                  pl.BlockSpec((D_MODEL, D_MODEL), const),
                  pl.BlockSpec((TOK_TILE, D_MODEL), prow),
                  pl.BlockSpec((TOK_TILE, D_MODEL), srow),
                  pl.BlockSpec((ROWS_PER_TILE, 6 * D_MODEL),
                               lambda i: (jnp.maximum(i - n_ptiles + pblocks, 0), 0)),
                  pl.BlockSpec((1, D_MODEL), const),
                  pl.BlockSpec((D_MODEL, LANES), const),
                  pl.BlockSpec((1, LANES), const),
                  pl.BlockSpec((TOK_TILE, TOK_TILE), const)],
        out_specs=[pl.BlockSpec((TOK_TILE, D_MODEL), row),
                   pl.BlockSpec((TOK_TILE // SUBLANES, ROW_PIECES, SUBLANES, LANES), lambda i: (i, 0, 0, 0)),
                   pl.BlockSpec((TOK_TILE, LANES), row),
                   pl.BlockSpec((1, 8, LANES), lambda i: (i // tiles_per_sb, 0, 0))],
        out_shape=[jax.ShapeDtypeStruct((t, D_MODEL), F32),
                   jax.ShapeDtypeStruct((t // SUBLANES, ROW_PIECES, SUBLANES, LANES), U32),
                   jax.ShapeDtypeStruct((t, LANES), F32),
                   jax.ShapeDtypeStruct((t // sb, 8, LANES), F32)],
        scratch_shapes=[pltpu.VMEM((D_MODEL // 2 // LANES, 2 * TOK_TILE, LANES), F32),
                        pltpu.VMEM((8, LANES), F32)],
        compiler_params=_params(),
        name="outproj",
    )(oa_p, oa_s, ob_p, ob_s, w_out, xp, xs, mod, gffn, wr, br, ltri)


MOE_SUPER_BLOCK = 2048
SEG_ALIGN = SUBLANES
CHUNK_BF16_ROWS = 2 * SEG_ALIGN * ROW_PIECES
SEG_BITS = 9
PAD_BITS = 5
FFN_ROWS = 512
PLAN_ROWS = LANES


def _local_rows(sb):
    return 2 * sb + N_EXPERTS * SEG_ALIGN


def _sorted_tiles(n_tokens, sb):
    rows = 2 * n_tokens + (n_tokens // sb) * N_EXPERTS * SEG_ALIGN + N_EXPERTS * FFN_ROWS
    return -(-rows // FFN_ROWS)


def _moe_plan_kernel(n_blocks, total_chunks, meta_ref, cnt_ref, ustrict_ref, lstrict_ref,
                     posw_ref, tab_ref, tile_ref):
    b = pl.program_id(0)
    per_tile = FFN_ROWS // SEG_ALIGN

    @pl.when(b == 0)
    def _():
        cnt = cnt_ref[...]
        chunks = jnp.floor((cnt + (SEG_ALIGN - 1)) * (1.0 / SEG_ALIGN))
        chunks_b = chunks.astype(BF16)
        loc = _dot(chunks_b, ustrict_ref[...])
        before = _dot(lstrict_ref[...], chunks_b)
        tot = _dot(jnp.ones((PLAN_ROWS, PLAN_ROWS), BF16), chunks_b)
        tiles = jnp.floor((tot + (per_tile - 1)) * (1.0 / per_tile))
        tile_off = _dot(tiles.astype(BF16), ustrict_ref[...])
        n_tiles = jnp.sum(tiles[0:1], axis=-1, keepdims=True)
        lane1 = lax.broadcasted_iota(jnp.int32, (PLAN_ROWS, LANES), 1)
        tail = lane1 == ROUTE_OFF + N_EXPERTS
        pad_off = jnp.where(tail, n_tiles * per_tile, tile_off * per_tile + tot)
        pad_n = jnp.where(tail, total_chunks - n_tiles * per_tile, tiles * per_tile - tot)
        row = lax.broadcasted_iota(jnp.int32, (PLAN_ROWS, LANES), 0)
        tab_ref[0] = loc
        tab_ref[1] = chunks
        tab_ref[2] = tile_off * per_tile + before
        tab_ref[3] = jnp.where(row == 0, pad_off, jnp.where(row == 1, pad_n, jnp.where(row == 2, n_tiles, 0.0)))
        t_idx = lax.broadcasted_iota(jnp.int32, tile_ref.shape, 0).astype(F32)
        lane_t = lax.broadcasted_iota(jnp.int32, tile_ref.shape, 1)
        is_expert = (lane_t >= ROUTE_OFF) & (lane_t < ROUTE_OFF + N_EXPERTS)
        ends = (tile_off + tiles)[0:1, :]
        owner = jnp.sum(jnp.where(is_expert & (ends <= t_idx), 1.0, 0.0), axis=-1, keepdims=True)
        tile_ref[...] = jnp.broadcast_to(jnp.minimum(owner, N_EXPERTS - 1.0), tile_ref.shape)

    own = jnp.floor((cnt_ref[pl.ds(b, 1), :] + (SEG_ALIGN - 1)) * (1.0 / SEG_ALIGN))
    own_off = _dot(jnp.broadcast_to(own, (SUBLANES, LANES)).astype(BF16), ustrict_ref[...]) * SEG_ALIGN
    meta = meta_ref[...]
    lane_f = lax.broadcasted_iota(jnp.int32, meta.shape, 1).astype(F32)
    off_row = own_off[0:1, :]
    pos = []
    for k in range(2):
        e_lane = meta[:, k:k + 1]
        base = jnp.sum(jnp.where(lane_f == e_lane, off_row, 0.0), axis=-1, keepdims=True)
        p = base + meta[:, 2 + k:3 + k]
        tile = jnp.floor(p * (1.0 / SUBLANES))
        pos.append(tile * (ROW_TILE - SUBLANES) + p)
    out = jnp.zeros_like(meta)
    for c, col in enumerate((pos[0], pos[1], meta[:, 4:5], meta[:, 5:6])):
        out = jnp.where(lane_f == float(c), col, out)
    posw_ref[...] = out


def _moe_plan(meta, cnt, sb):
    n_blocks = meta.shape[0] // sb
    assert n_blocks <= PLAN_ROWS and sb // SEG_ALIGN <= 256
    n_tiles = _sorted_tiles(meta.shape[0], sb)
    tile_rows = -(-n_tiles // SUBLANES) * SUBLANES
    r = np.arange(LANES)
    ustrict = jnp.asarray(r[:, None] < r[None, :], BF16)
    lstrict = jnp.asarray(r[None, :] < r[:, None], BF16)
    cnt_all = jnp.pad(cnt[:, 0, :], ((0, PLAN_ROWS - n_blocks), (0, 0)))
    const = lambda s: (0, 0)
    posw, tab, tile_owner = pl.pallas_call(
        functools.partial(_moe_plan_kernel, n_blocks, float(n_tiles * (FFN_ROWS // SEG_ALIGN))),
        grid=(n_blocks,),
        in_specs=[pl.BlockSpec((sb, LANES), lambda s: (s, 0)),
                  pl.BlockSpec((PLAN_ROWS, LANES), const),
                  pl.BlockSpec((LANES, LANES), const),
                  pl.BlockSpec((PLAN_ROWS, PLAN_ROWS), const)],
        out_specs=[pl.BlockSpec((sb, LANES), lambda s: (s, 0)),
                   pl.BlockSpec((4, PLAN_ROWS, LANES), lambda s: (0, 0, 0)),
                   pl.BlockSpec((tile_rows, LANES), const)],
        out_shape=[jax.ShapeDtypeStruct(meta.shape, F32),
                   jax.ShapeDtypeStruct((4, PLAN_ROWS, LANES), F32),
                   jax.ShapeDtypeStruct((tile_rows, LANES), F32)],
        compiler_params=_params(),
        name="moe_plan",
    )(meta, cnt_all, ustrict, lstrict)
    experts = slice(ROUTE_OFF, ROUTE_OFF + N_EXPERTS)
    to_i32 = lambda x: x.astype(jnp.int32).reshape(-1)
    plan = dict(
        loc=to_i32(tab[0, :n_blocks, experts]), n=to_i32(tab[1, :n_blocks, experts]),
        dst=to_i32(tab[2, :n_blocks, experts]),
        pad_off=to_i32(tab[3, 0, ROUTE_OFF:ROUTE_OFF + N_EXPERTS + 1]),
        pad_n=to_i32(tab[3, 1, ROUTE_OFF:ROUTE_OFF + N_EXPERTS + 1]),
        n_tiles=to_i32(tab[3, 2, 0:1]),
        owner=to_i32(tile_owner[:n_tiles, 0]))
    return posw, plan


def _token_rows(start):
    return pl.ds(start, ROW_PIECES, stride=SUBLANES)


def _pow2_copies(src_ref, dst_ref, src_chunk, dst_chunk, n, n_bits, sem, act):
    done = 0
    for k in reversed(range(n_bits)):
        take = (n >> k) & 1
        rows = CHUNK_BF16_ROWS << k
        src0 = 0 if src_chunk is None else pl.multiple_of((src_chunk + done) * CHUNK_BF16_ROWS, CHUNK_BF16_ROWS)
        dst0 = pl.multiple_of((dst_chunk + done) * CHUNK_BF16_ROWS, CHUNK_BF16_ROWS)

        @pl.when(take == 1)
        def _(src0=src0, dst0=dst0, rows=rows):
            act(pltpu.make_async_copy(src_ref.at[pl.ds(src0, rows)], dst_ref.at[pl.ds(dst0, rows)], sem))

        done = done + take * (1 << k)


def _segment_copies(block, loc_ref, n_ref, dst_ref, local_ref, global_ref, to_global, sem, act):
    def per_expert(e, carry):
        seg = block * N_EXPERTS + e
        if to_global:
            _pow2_copies(local_ref, global_ref, loc_ref[seg], dst_ref[seg], n_ref[seg], SEG_BITS, sem, act)
        else:
            _pow2_copies(global_ref, local_ref, dst_ref[seg], loc_ref[seg], n_ref[seg], SEG_BITS, sem, act)
        return carry

    lax.fori_loop(0, N_EXPERTS, per_expert, 0)


def _zero_fill(zero_ref, global_ref, padoff_ref, padn_ref, sem, act):
    full = 1 << PAD_BITS

    def per_pad(e, carry):
        def per_full(c, inner):
            dst0 = pl.multiple_of((padoff_ref[e] + c * full) * CHUNK_BF16_ROWS, CHUNK_BF16_ROWS)
            act(pltpu.make_async_copy(zero_ref, global_ref.at[pl.ds(dst0, full * CHUNK_BF16_ROWS)], sem))
            return inner

        n_full = padn_ref[e] >> PAD_BITS
        lax.fori_loop(0, n_full, per_full, 0)
        _pow2_copies(zero_ref, global_ref, None, padoff_ref[e] + n_full * full, padn_ref[e] & (full - 1),
                     PAD_BITS, sem, act)
        return carry

    lax.fori_loop(0, N_EXPERTS + 1, per_pad, 0)


STAGE_SLAB = 1024


def _restage(src_sc, dst_sc, dst_dtype):
    ratio = dst_sc.shape[0] / src_sc.shape[0]
    n_slabs = src_sc.shape[0] // (STAGE_SLAB if ratio > 1 else 2 * STAGE_SLAB)
    src_rows = src_sc.shape[0] // n_slabs
    dst_rows = dst_sc.shape[0] // n_slabs

    def slab(i, carry):
        s0 = pl.multiple_of(i * src_rows, src_rows)
        d0 = pl.multiple_of(i * dst_rows, dst_rows)
        dst_sc[pl.ds(d0, dst_rows), :] = pltpu.bitcast(src_sc[pl.ds(s0, src_rows), :], dst_dtype)
        return carry

    lax.fori_loop(0, n_slabs, slab, 0)


def _moe_dispatch_kernel(sb, n_blocks, loc_ref, n_ref, dst_ref, padoff_ref, padn_ref,
                         h2p_ref, a1_ref, a2_ref, xs_hbm, local_sc, stage_sc, zero_sc, sems, zero_sem):
    b = pl.program_id(0)
    slot = b & 1

    def segments(block, buf, act):
        _segment_copies(block, loc_ref, n_ref, dst_ref, stage_sc.at[buf], xs_hbm, True, sems.at[buf], act)

    local_sc[...] = jnp.zeros_like(local_sc)

    def step(g, carry):
        src = pl.multiple_of(g * ROW_TILE, ROW_TILE)
        for u in range(SUBLANES):
            t = g * SUBLANES + u
            row = h2p_ref[_token_rows(src + u), :]
            local_sc[_token_rows(a1_ref[t]), :] = row
            local_sc[_token_rows(a2_ref[t]), :] = row
        return carry

    lax.fori_loop(0, sb // SUBLANES, step, 0)

    @pl.when(b > 0)
    def _():
        segments(b - 1, 1 - slot, lambda c: c.wait())

    _restage(local_sc, stage_sc.at[slot], BF16)
    segments(b, slot, lambda c: c.start())

    @pl.when(b == 0)
    def _():
        zero_sc[...] = jnp.zeros_like(zero_sc)
        _zero_fill(zero_sc, xs_hbm, padoff_ref, padn_ref, zero_sem, lambda c: c.start())
        _zero_fill(zero_sc, xs_hbm, padoff_ref, padn_ref, zero_sem, lambda c: c.wait())

    @pl.when(b == n_blocks - 1)
    def _():
        segments(b, slot, lambda c: c.wait())


def _smem_vec(n, index_map):
    return pl.BlockSpec((n,), index_map, memory_space=pltpu.SMEM)


def _moe_dispatch(h2p, a1, a2, plan, sb, n_tiles):
    n_blocks = h2p.shape[0] // (sb * ROW_PIECES)
    local_flat = _local_rows(sb) * ROW_PIECES
    vec = _smem_vec(sb, lambda s, *_: (s,))
    return pl.pallas_call(
        functools.partial(_moe_dispatch_kernel, sb, n_blocks),
        grid_spec=pltpu.PrefetchScalarGridSpec(
            num_scalar_prefetch=5,
            grid=(n_blocks,),
            in_specs=[pl.BlockSpec((sb * ROW_PIECES, LANES), lambda s, *_: (s, 0)), vec, vec],
            out_specs=pl.BlockSpec(memory_space=pl.ANY),
            scratch_shapes=[pltpu.VMEM((local_flat, LANES), U32),
                            pltpu.VMEM((2, 2 * local_flat, LANES), BF16),
                            pltpu.VMEM(((1 << PAD_BITS) * CHUNK_BF16_ROWS, LANES), BF16),
                            pltpu.SemaphoreType.DMA((2,)),
                            pltpu.SemaphoreType.DMA(())]),
        out_shape=jax.ShapeDtypeStruct((n_tiles * FFN_ROWS * ROW_PIECES * 2, LANES), BF16),
        compiler_params=_params(),
        name="moe_dispatch",
    )(plan["loc"], plan["n"], plan["dst"], plan["pad_off"], plan["pad_n"], h2p, a1, a2)


def _moe_ffn_kernel(owner_ref, ntiles_ref, xs_ref, wg_ref, wu_ref, wd_ref, ys_ref,
                    wg_sc, wu_sc, wd_sc, z32_sc):
    i = pl.program_id(0)
    half = D_MODEL // 2
    used = i < ntiles_ref[0]

    @pl.when(used & ((i == 0) | (owner_ref[i] != owner_ref[jnp.maximum(i - 1, 0)])))
    def _():
        wg_sc[...] = wg_ref[0].astype(BF16)
        wu_sc[...] = wu_ref[0].astype(BF16)
        wd_sc[...] = wd_ref[0].astype(BF16)

    @pl.when(used)
    def _():
        lo, hi = _unpack_rows(z32_sc, _from_row_tiled(xs_ref[...], FFN_ROWS), FFN_ROWS)
        lo, hi = lo.astype(BF16), hi.astype(BF16)
        g = _dot(lo, wg_sc[0:half, :]) + _dot(hi, wg_sc[half:D_MODEL, :])
        u = _dot(lo, wu_sc[0:half, :]) + _dot(hi, wu_sc[half:D_MODEL, :])
        y = _dot((_silu(g) * u).astype(BF16), wd_sc[...])
        ys_ref[...] = _flatten_tiled(_to_row_tiled(_pack_rows(z32_sc, y, FFN_ROWS), FFN_ROWS))

    @pl.when(jnp.logical_not(used))
    def _():
        ys_ref[...] = jnp.zeros_like(ys_ref)


def _moe_ffn(xs, plan, wg, wu, wd):
    flat = FFN_ROWS * ROW_PIECES * 2
    n_tiles = xs.shape[0] // flat
    last_used = lambda i, owner, nt: jnp.minimum(i, nt[0] - 1)
    wspec = lambda shape: pl.BlockSpec((1,) + shape, lambda i, owner, nt: (owner[last_used(i, owner, nt)], 0, 0))
    return pl.pallas_call(
        _moe_ffn_kernel,
        grid_spec=pltpu.PrefetchScalarGridSpec(
            num_scalar_prefetch=2,
            grid=(n_tiles,),
            in_specs=[pl.BlockSpec((flat, LANES), lambda i, owner, nt: (last_used(i, owner, nt), 0)),
                      wspec((D_MODEL, EXPERT_FF)), wspec((D_MODEL, EXPERT_FF)), wspec((EXPERT_FF, D_MODEL))],
            out_specs=pl.BlockSpec((flat, LANES), lambda i, owner, nt: (i, 0)),
            scratch_shapes=[pltpu.VMEM((D_MODEL, EXPERT_FF), BF16),
                            pltpu.VMEM((D_MODEL, EXPERT_FF), BF16),
                            pltpu.VMEM((EXPERT_FF, D_MODEL), BF16),
                            pltpu.VMEM((ROW_PIECES, 2 * FFN_ROWS, LANES), F32)]),
        out_shape=jax.ShapeDtypeStruct(xs.shape, BF16),
        compiler_params=_params(),
        name="moe_ffn",
    )(plan["owner"], plan["n_tiles"], xs, wg, wu, wd)


def _moe_combine_kernel(n_psb, n_blocks, loc_ref, n_ref, dst_ref,
                        ys_hbm, a1_ref, a2_ref, posw_ref, x1_ref, mod_ref, yp_ref, yo_ref,
                        local_sc, stage_sc, g1_sc, g2_sc, z32_sc, sems):
    s = pl.program_id(0)
    slot = s & 1

    def segments(block, buf, act):
        _segment_copies(block, loc_ref, n_ref, dst_ref, stage_sc.at[buf], ys_hbm, False, sems.at[buf], act)

    @pl.when(pl.program_id(1) == 0)
    def _():
        @pl.when(s == 0)
        def _():
            segments(s, slot, lambda c: c.start())

        segments(s, slot, lambda c: c.wait())
        _restage(stage_sc.at[slot], local_sc, U32)

        @pl.when(s + 1 < n_blocks)
        def _():
            segments(s + 1, 1 - slot, lambda c: c.start())

    def step(g, carry):
        dst = pl.multiple_of(g * ROW_TILE, ROW_TILE)
        for u in range(SUBLANES):
            t = g * SUBLANES + u
            g1_sc[_token_rows(dst + u), :] = local_sc[_token_rows(a1_ref[t]), :]
            g2_sc[_token_rows(dst + u), :] = local_sc[_token_rows(a2_ref[t]), :]
        return carry

    lax.fori_loop(0, TOK_TILE // SUBLANES, step, 0)
    halves = lambda g_sc: [pltpu.bitcast(p, BF16) for p in _from_row_tiled(g_sc[...], TOK_TILE)]
    lo1, hi1 = _unpack_rows(z32_sc, halves(g1_sc), TOK_TILE)
    lo2, hi2 = _unpack_rows(z32_sc, halves(g2_sc), TOK_TILE)
    w1, w2 = posw_ref[:, 2:3], posw_ref[:, 3:4]
    moe = jnp.concatenate([w1 * lo1 + w2 * lo2, w1 * hi1 + w2 * hi2], axis=1)
    gate2 = _rows_to_tokens(mod_ref[:, 5 * D_MODEL:6 * D_MODEL], D_MODEL)
    y = x1_ref[...] + gate2 * moe

    @pl.when(s < n_psb)
    def _():
        yp_ref[...] = y

    @pl.when(s >= n_psb)
    def _():
        yo_ref[...] = y


def _moe_combine(ys, a1, a2, posw, plan, x1, mod, sb, n_ptiles, n_stiles, prep):
    tps = sb // TOK_TILE
    n_blocks = (n_ptiles + n_stiles) // tps
    n_psb = n_ptiles // tps
    pblocks = prep // ROWS_PER_TILE
    tile = lambda s, j: s * tps + j
    vec = _smem_vec(TOK_TILE, lambda s, j, *_: (tile(s, j),))
    return pl.pallas_call(
        functools.partial(_moe_combine_kernel, n_psb, n_blocks),
        grid_spec=pltpu.PrefetchScalarGridSpec(
            num_scalar_prefetch=3,
            grid=(n_blocks, tps),
            in_specs=[pl.BlockSpec(memory_space=pl.ANY), vec, vec,
                      pl.BlockSpec((TOK_TILE, LANES), lambda s, j, *_: (tile(s, j), 0)),
                      pl.BlockSpec((TOK_TILE, D_MODEL), lambda s, j, *_: (tile(s, j), 0)),
                      pl.BlockSpec((ROWS_PER_TILE, 6 * D_MODEL),
                                   lambda s, j, *_: (jnp.maximum(tile(s, j) - n_ptiles + pblocks, 0), 0))],
            out_specs=[pl.BlockSpec((TOK_TILE, D_MODEL),
                                    lambda s, j, *_: (jnp.minimum(tile(s, j), n_ptiles - 1), 0)),
                       pl.BlockSpec((TOK_TILE, D_MODEL),
                                    lambda s, j, *_: (jnp.maximum(tile(s, j) - n_ptiles, 0), 0))],
            scratch_shapes=[pltpu.VMEM((_local_rows(sb) * ROW_PIECES, LANES), U32),
                            pltpu.VMEM((2, _local_rows(sb) * ROW_PIECES * 2, LANES), BF16),
                            pltpu.VMEM((TOK_TILE * ROW_PIECES, LANES), U32),
                            pltpu.VMEM((TOK_TILE * ROW_PIECES, LANES), U32),
                            pltpu.VMEM((ROW_PIECES, 2 * TOK_TILE, LANES), F32),
                            pltpu.SemaphoreType.DMA((2,))]),
        out_shape=[jax.ShapeDtypeStruct((n_ptiles * TOK_TILE, D_MODEL), F32),
                   jax.ShapeDtypeStruct((n_stiles * TOK_TILE, D_MODEL), F32)],
        compiler_params=_params(2),
        name="moe_combine",
    )(plan["loc"], plan["n"], plan["dst"], ys, a1, a2, posw, x1, mod)


def _layer(xp, xs, cache_k, cache_v, state, c_prompt, c_sample, norm_mix_g, norm_ffn_g, w_ada, b_ada, w_in,
           q_norm_g, k_norm_g, rel_bias, w_gate_up, b_gate, gla_norm_g, w_out, w_route_group,
           b_route_group, w_route_expert, b_route_expert, w_exp_gate, w_exp_up, w_exp_down):
    batch, seq, _ = xp.shape
    n_seq, dec_seq, _ = xs.shape
    assert batch == 1 and dec_seq == CHUNK and cache_k.shape[1] == BAND_PAST
    assert seq % TOK_TILE == 0 and seq >= BAND_PAST and (n_seq * CHUNK) % TOK_TILE == 0
    assert seq % (ATTN_SUB * Q_ROWS) == 0 and seq % (GLA_SUB * GLA_CHUNKS * CHUNK) == 0
    n_ptok, n_stok = seq, n_seq * CHUNK
    n_ptiles, n_stiles = n_ptok // TOK_TILE, n_stok // TOK_TILE
    sb = MOE_SUPER_BLOCK if (n_ptok % MOE_SUPER_BLOCK == 0 and n_stok % MOE_SUPER_BLOCK == 0) else TOK_TILE
    prep = ROWS_PER_TILE

    xp2 = xp.reshape(n_ptok, D_MODEL)
    xs2 = xs.reshape(n_stok, D_MODEL)
    c_rows = jnp.concatenate([jnp.broadcast_to(c_prompt, (prep, D_MODEL)), c_sample], axis=0)
    mod = _adaln(c_rows, w_ada, b_ada)

    w_in_p = jnp.pad(w_in, ((0, 0), (0, IN_PAD - w_in.shape[1]))).astype(BF16)
    wgu_p = jnp.pad(w_gate_up, ((0, LANES - GATE_RANK), (0, 0))).astype(BF16)
    head = np.arange(A_WIDTH) // A_HEAD_DIM
    bd = jnp.asarray(head[:, None] == head[None, :], BF16)
    gq = jnp.tile(q_norm_g, A_HEADS).reshape(1, A_WIDTH)
    gk = jnp.tile(k_norm_g, A_HEADS).reshape(1, A_WIDTH)
    q, k, v, kf, vf, gla, la = _inproj(
        xp2, xs2, mod, norm_mix_g.reshape(1, D_MODEL), w_in_p, bd, gq, gk, wgu_p,
        b_gate.reshape(1, B_KWIDTH), n_ptiles, n_stiles, prep)

    first_chunk = n_ptok // CHUNK
    oa_p = _attn_prompt(rel_bias[:, _bias_lanes(ATTN_WIN * Q_ROWS)], q, k, v, n_ptok // (ATTN_SUB * Q_ROWS))
    oa_s = _attn_sample(rel_bias[:, _bias_lanes(SAMPLE_KEYS)], q, k, v,
                        cache_k.reshape(n_seq, BAND_PAST, A_WIDTH), cache_v.reshape(n_seq, BAND_PAST, A_WIDTH),
                        first_chunk, n_seq)
    g_gla = gla_norm_g.reshape(1, B_DV)
    ob_p, sfin_p = _gla_prompt(gla, la, g_gla, n_ptok // (GLA_SUB * GLA_CHUNKS * CHUNK))
    ob_s, sfin_s = _gla_sample(gla, la, g_gla, _state_to_pairs(state), first_chunk, n_seq)

    wr = jnp.pad(jnp.concatenate([w_route_group, w_route_expert], axis=1),
                 ((0, 0), (0, LANES - N_GROUPS - N_EXPERTS)))
    br = jnp.pad(jnp.concatenate([b_route_group, b_route_expert]), (0, LANES - N_GROUPS - N_EXPERTS))
    x1, h2p, meta, cnt = _outproj(oa_p, oa_s, ob_p, ob_s, w_out.astype(BF16), xp2, xs2, mod,
                                  norm_ffn_g.reshape(1, D_MODEL), wr, br.reshape(1, LANES),
                                  n_ptiles, n_stiles, prep, sb)

    posw, plan = _moe_plan(meta, cnt, sb)
    a1, a2 = posw[:, 0].astype(jnp.int32), posw[:, 1].astype(jnp.int32)
    xs_sorted = _moe_dispatch(h2p.reshape(-1, LANES), a1, a2, plan, sb, _sorted_tiles(n_ptok + n_stok, sb))
    ys_sorted = _moe_ffn(xs_sorted, plan, w_exp_gate, w_exp_up, w_exp_down)
    yp, ys = _moe_combine(ys_sorted, a1, a2, posw, plan, x1, mod, sb, n_ptiles, n_stiles, prep)

    tail = min(BAND_PAST, seq)
    heads = (A_HEADS, A_HEAD_DIM)
    return (yp.reshape(1, seq, D_MODEL), ys.reshape(n_seq, CHUNK, D_MODEL),
            kf[TOK_TILE - tail:TOK_TILE].reshape((1, tail) + heads),
            vf[TOK_TILE - tail:TOK_TILE].reshape((1, tail) + heads),
            _pairs_to_state(sfin_p)[None],
            kf[TOK_TILE:].reshape((n_seq, CHUNK) + heads),
            vf[TOK_TILE:].reshape((n_seq, CHUNK) + heads),
            _pairs_to_state(sfin_s))


def kernel(x_prompt, x_sample, cache_a_k, cache_a_v, state_gla, c_prompt, c_sample, norm_mix_g, norm_ffn_g,
           w_ada, b_ada, w_in, q_norm_g, k_norm_g, rel_bias, w_gate_up, b_gate, gla_norm_g, w_out,
           w_route_group, b_route_group, w_route_expert, b_route_expert, w_exp_gate, w_exp_up, w_exp_down):
    depth = w_in.shape[0]
    yp, ys = x_prompt, x_sample
    outs = [[] for _ in range(6)]
    for l in range(depth):
        yp, ys, kp, vp, sp, ks, vs, ss = _layer(
            yp, ys, cache_a_k[l], cache_a_v[l], state_gla[l], c_prompt, c_sample, norm_mix_g[l], norm_ffn_g[l],
            w_ada[l], b_ada[l], w_in[l], q_norm_g[l], k_norm_g[l], rel_bias[l], w_gate_up[l], b_gate[l],
            gla_norm_g[l], w_out[l], w_route_group[l], b_route_group[l], w_route_expert[l], b_route_expert[l],
            w_exp_gate[l], w_exp_up[l], w_exp_down[l])
        for lst, val in zip(outs, (kp, vp, sp, ks, vs, ss)):
            lst.append(val)
    return (yp, ys) + tuple(jnp.stack(o) for o in outs)
```

```python
import functools

import numpy as np
import jax
import jax.numpy as jnp
from jax import lax
from jax.experimental import pallas as pl
from jax.experimental.pallas import tpu as pltpu

F32 = jnp.float32
BF16 = jnp.bfloat16
U32 = jnp.uint32

D_MODEL = 1024
CHUNK = 64
LOG_CHUNK = 6
BAND_CHUNKS = 8
BAND_PAST = BAND_CHUNKS * CHUNK
A_WIDTH = 512
A_HEADS = 8
A_HEAD_DIM = 64
MAX_REL = 128
N_REL = CHUNK + MAX_REL
B_WIDTH = 512
B_HEADS = 4
B_DV = 128
B_DK = 64
B_KWIDTH = 256
GATE_RANK = 16
GATE_TAU = 16.0
N_GROUPS = 4
EXPERTS_PER_GROUP = 8
N_EXPERTS = 32
EXPERT_FF = 256
EPS = 1e-6

LANES = 128
IN_MAIN = 3 * A_WIDTH + 2 * B_KWIDTH + 2 * B_WIDTH
TOK_TILE = 512
ROWS_PER_TILE = TOK_TILE // CHUNK
Q_CHUNKS = 4
Q_ROWS = Q_CHUNKS * CHUNK
ROLL_W = 1024
NEG = -1e30
ROUTE_OFF = N_GROUPS
VMEM_LIMIT = 56 * 1024 * 1024


def _params(n_axes=1):
    return pltpu.CompilerParams(dimension_semantics=("arbitrary",) * n_axes,
                                vmem_limit_bytes=VMEM_LIMIT)


def _split(a):
    hi = a.astype(BF16)
    lo = (a - hi.astype(F32)).astype(BF16)
    return hi, lo


def _dot(a, b):
    return jnp.dot(a, b, preferred_element_type=F32)


def _dot3(a, b):
    ah, al = _split(a)
    bh, bl = _split(b)
    return _dot(ah, bh) + _dot(al, bh) + _dot(ah, bl)


def _dot_nt(a, b):
    return lax.dot_general(a, b, (((1,), (1,)), ((), ())), preferred_element_type=F32)


def _dot_tn(a, b):
    return lax.dot_general(a, b, (((0,), (0,)), ((), ())), preferred_element_type=F32)


def _silu(x):
    return x / (1.0 + jnp.exp(-x))


def _rows_to_tokens(rows, n):
    r = rows.shape[0]
    return jnp.broadcast_to(rows[:, None, :], (r, CHUNK, n)).reshape(r * CHUNK, n)


def _adaln_kernel(c_ref, w_ref, b_ref, o_ref):
    a = _silu(c_ref[...])
    o_ref[...] = _dot3(a, w_ref[...]) + b_ref[...]


def _adaln(c_rows, w_ada, b_ada):
    r = c_rows.shape[0]
    n = w_ada.shape[1]
    tn = 1024
    return pl.pallas_call(
        _adaln_kernel,
        grid=(n // tn,),
        in_specs=[pl.BlockSpec((r, D_MODEL), lambda j: (0, 0)),
                  pl.BlockSpec((D_MODEL, tn), lambda j: (0, j)),
                  pl.BlockSpec((1, tn), lambda j: (0, j))],
        out_specs=pl.BlockSpec((r, tn), lambda j: (0, j)),
        out_shape=jax.ShapeDtypeStruct((r, n), F32),
        compiler_params=_params(),
        name="adaln",
    )(c_rows, w_ada, b_ada.reshape(1, n))


def _head_rms(z, bd_ref, g):
    ms = _dot((z * z).astype(BF16), bd_ref[...]) * (1.0 / A_HEAD_DIM)
    return z * lax.rsqrt(ms + EPS) * g


def _inproj_kernel(n_ptiles, xp_ref, xs_ref, mod_ref, gmix_ref, w_ref, wgr_ref, bd_ref, gq_ref, gk_ref,
                   wgu_ref, bg_ref,
                   q_ref, k_ref, v_ref, kf_ref, vf_ref, gla_ref, la_ref):
    i = pl.program_id(0)
    x = jnp.where(i < n_ptiles, xp_ref[...], xs_ref[...])
    ms = jnp.mean(x * x, axis=-1, keepdims=True)
    xn = x * lax.rsqrt(ms + EPS) * gmix_ref[...]
    sh = _rows_to_tokens(mod_ref[:, 0:D_MODEL], D_MODEL)
    sc = _rows_to_tokens(mod_ref[:, D_MODEL:2 * D_MODEL], D_MODEL)
    hb = (xn * (1.0 + sc) + sh).astype(BF16)

    zq = _dot(hb, w_ref[:, 0:A_WIDTH])
    q_ref[...] = (_head_rms(zq, bd_ref, gq_ref[...]) * (LOG2E * A_HEAD_DIM ** -0.5)).astype(BF16)
    zk = _dot(hb, w_ref[:, A_WIDTH:2 * A_WIDTH])
    kn = _head_rms(zk, bd_ref, gk_ref[...])
    k_ref[...] = kn.astype(BF16)
    kf_ref[...] = kn
    zv = _dot(hb, w_ref[:, 2 * A_WIDTH:3 * A_WIDTH])
    v_ref[...] = zv.astype(BF16)
    vf_ref[...] = zv

    o = 3 * A_WIDTH
    zqb = _dot(hb, w_ref[:, o:o + B_KWIDTH]) * (B_DK ** -0.5)
    gla_ref[:, 0:B_KWIDTH] = zqb.astype(BF16)
    for c in range(B_KWIDTH, 2 * B_KWIDTH + 2 * B_WIDTH, 256):
        gla_ref[:, c:c + 256] = _dot(hb, w_ref[:, o + c:o + c + 256]).astype(BF16)

    gr = _dot(hb, wgr_ref[...])
    logit = _dot(gr.astype(BF16), wgu_ref[...]) + bg_ref[...]
    log_sig = jnp.minimum(logit, 0.0) - jnp.log1p(jnp.exp(-jnp.abs(logit)))
    la_ref[...] = log_sig * (1.0 / GATE_TAU)


def _inproj(xp, xs, mod, gmix, w_main, w_gr, bd, gq, gk, wgu_p, bg, n_ptiles, n_stiles, prep):
    n_tiles = n_ptiles + n_stiles
    t = n_tiles * TOK_TILE
    tail_tiles = 1 + n_stiles
    pblocks = prep // ROWS_PER_TILE
    const = lambda i: (0, 0)
    row = lambda i: (i, 0)
    tail = lambda i: (jnp.maximum(i - (n_ptiles - 1), 0), 0)
    return pl.pallas_call(
        functools.partial(_inproj_kernel, n_ptiles),
        grid=(n_tiles,),
        in_specs=[pl.BlockSpec((TOK_TILE, D_MODEL), lambda i: (jnp.minimum(i, n_ptiles - 1), 0)),
                  pl.BlockSpec((TOK_TILE, D_MODEL), lambda i: (jnp.maximum(i - n_ptiles, 0), 0)),
                  pl.BlockSpec((ROWS_PER_TILE, 6 * D_MODEL),
                               lambda i: (jnp.maximum(i - n_ptiles + pblocks, 0), 0)),
                  pl.BlockSpec((1, D_MODEL), const),
                  pl.BlockSpec((D_MODEL, IN_MAIN), const),
                  pl.BlockSpec((D_MODEL, LANES), const),
                  pl.BlockSpec((A_WIDTH, A_WIDTH), const),
                  pl.BlockSpec((1, A_WIDTH), const),
                  pl.BlockSpec((1, A_WIDTH), const),
                  pl.BlockSpec((LANES, B_KWIDTH), const),
                  pl.BlockSpec((1, B_KWIDTH), const)],
        out_specs=[pl.BlockSpec((TOK_TILE, A_WIDTH), row),
                   pl.BlockSpec((TOK_TILE, A_WIDTH), row),
                   pl.BlockSpec((TOK_TILE, A_WIDTH), row),
                   pl.BlockSpec((TOK_TILE, A_WIDTH), tail),
                   pl.BlockSpec((TOK_TILE, A_WIDTH), tail),
                   pl.BlockSpec((TOK_TILE, 2 * B_KWIDTH + 2 * B_WIDTH), row),
                   pl.BlockSpec((TOK_TILE, B_KWIDTH), row)],
        out_shape=[jax.ShapeDtypeStruct((t, A_WIDTH), BF16),
                   jax.ShapeDtypeStruct((t, A_WIDTH), BF16),
                   jax.ShapeDtypeStruct((t, A_WIDTH), BF16),
                   jax.ShapeDtypeStruct((tail_tiles * TOK_TILE, A_WIDTH), F32),
                   jax.ShapeDtypeStruct((tail_tiles * TOK_TILE, A_WIDTH), F32),
                   jax.ShapeDtypeStruct((t, 2 * B_KWIDTH + 2 * B_WIDTH), BF16),
                   jax.ShapeDtypeStruct((t, B_KWIDTH), F32)],
        compiler_params=_params(),
        name="inproj",
    )(xp, xs, mod, gmix, w_main, w_gr, bd, gq, gk, wgu_p, bg)


def _bias_lanes(n_keys):
    l = np.arange(ROLL_W)
    d = np.where(l < n_keys, BAND_PAST - l, BAND_PAST - l + ROLL_W)
    return np.clip(d, -(CHUNK - 1), MAX_REL) + (CHUNK - 1)


LOG2E = 1.4426950408889634


def _band_mask(m_rows, n_keys, first_col):
    qi = lax.broadcasted_iota(jnp.int32, (m_rows, n_keys), 0) >> LOG_CHUNK
    kw = lax.broadcasted_iota(jnp.int32, (m_rows, n_keys), 1)
    kc = kw >> LOG_CHUNK
    return (kc >= qi) & (kc <= qi + BAND_CHUNKS) & (kw >= first_col)


def _bias_tile(u_ref, h, ok):
    m_rows, n_keys = ok.shape
    src = jnp.broadcast_to(u_ref[h:h + 1, :] * LOG2E, (m_rows, ROLL_W))
    toe = pltpu.roll(src, 0, 1, stride=1, stride_axis=0)
    return jnp.where(ok, toe[:, 0:n_keys], NEG)


def _attend(q, kcat, vcat, bias_sc):
    m_rows = q.shape[0]
    first = lax.broadcasted_iota(jnp.int32, (m_rows, LANES), 1) < A_HEAD_DIM
    outs = []
    for p in range(A_HEADS // 2):
        lanes = slice(p * LANES, (p + 1) * LANES)
        qp, kp, vp = q[:, lanes], kcat[:, lanes], vcat[:, lanes]
        zero = jnp.zeros_like(qp)
        q2 = jnp.concatenate([jnp.where(first, qp, zero), jnp.where(first, zero, qp)], axis=0)
        s = _dot_nt(q2, kp) + bias_sc[p]
        e = jnp.exp2(s - jnp.max(s, axis=-1, keepdims=True))
        l = jnp.sum(e, axis=-1, keepdims=True)
        o2 = _dot(e.astype(BF16), vp) / l
        outs.append(jnp.where(first, o2[0:m_rows], o2[m_rows:2 * m_rows]))
    return jnp.concatenate(outs, axis=-1)


ATTN_SUB = 4
ATTN_WIN = 3


def _attn_prompt_kernel(u_ref, q_ref, *refs):
    k_refs = refs[0:ATTN_SUB + ATTN_WIN - 1]
    v_refs = refs[ATTN_SUB + ATTN_WIN - 1:2 * (ATTN_SUB + ATTN_WIN - 1)]
    o_ref, bias_sc = refs[-2:]
    j = pl.program_id(0)
    n_keys = ATTN_WIN * Q_ROWS

    @pl.when(j == 0)
    def _():
        for g in range(ATTN_WIN):
            ok = _band_mask(Q_ROWS, n_keys, (ATTN_WIN - 1 - g) * Q_ROWS)
            for h in range(A_HEADS):
                bias_sc[g, h // 2, (h % 2) * Q_ROWS:(h % 2 + 1) * Q_ROWS, :] = _bias_tile(u_ref, h, ok)

    ks = [r[...] for r in k_refs]
    vs = [r[...] for r in v_refs]
    for sub in range(ATTN_SUB):
        rows = slice(sub * Q_ROWS, (sub + 1) * Q_ROWS)
        kcat = jnp.concatenate(ks[sub:sub + ATTN_WIN], axis=0)
        vcat = jnp.concatenate(vs[sub:sub + ATTN_WIN], axis=0)
        bias = bias_sc.at[jnp.minimum(ATTN_SUB * j + sub, ATTN_WIN - 1)]
        o_ref[rows, :] = _attend(q_ref[rows, :], kcat, vcat, bias).astype(BF16)


def _attn_prompt(u, q, k, v, n_steps):
    const = lambda j: (0, 0)
    n_blk = ATTN_SUB + ATTN_WIN - 1
    blk = lambda d: pl.BlockSpec((Q_ROWS, A_WIDTH),
                                 lambda j, d=d: (jnp.maximum(ATTN_SUB * j - (ATTN_WIN - 1) + d, 0), 0))
    step_rows = ATTN_SUB * Q_ROWS
    return pl.pallas_call(
        _attn_prompt_kernel,
        grid=(n_steps,),
        in_specs=[pl.BlockSpec((A_HEADS, ROLL_W), const), pl.BlockSpec((step_rows, A_WIDTH), lambda j: (j, 0))]
                 + [blk(d) for d in range(n_blk)] * 2,
        out_specs=pl.BlockSpec((step_rows, A_WIDTH), lambda j: (j, 0)),
        out_shape=jax.ShapeDtypeStruct((n_steps * step_rows, A_WIDTH), BF16),
        scratch_shapes=[pltpu.VMEM((ATTN_WIN, A_HEADS // 2, 2 * Q_ROWS, ATTN_WIN * Q_ROWS), F32)],
        compiler_params=_params(),
        name="attn_prompt",
    )(u, q, *([k] * n_blk), *([v] * n_blk))


SAMPLE_KEYS = BAND_PAST + 2 * CHUNK


def _attn_sample_kernel(u_ref, q_ref, kn_ref, vn_ref, kc_ref, vc_ref, o_ref, bias_sc):
    @pl.when(pl.program_id(0) == 0)
    def _():
        ok = _band_mask(CHUNK, SAMPLE_KEYS, 0)
        for p in range(A_HEADS // 2):
            pair = jnp.concatenate([_bias_tile(u_ref, 2 * p, ok), _bias_tile(u_ref, 2 * p + 1, ok)], axis=0)
            bias_sc[p] = pair.T

    pad = jnp.zeros((CHUNK, A_WIDTH), BF16)
    kcat = jnp.concatenate([kc_ref[0], kn_ref[...], pad], axis=0)
    vcat = jnp.concatenate([vc_ref[0], vn_ref[...], pad], axis=0)
    q = q_ref[...]
    lane = lax.broadcasted_iota(jnp.int32, (CHUNK, LANES), 1)
    first = lane < A_HEAD_DIM
    zero = jnp.zeros((CHUNK, LANES), BF16)
    outs = []
    for p in range(A_HEADS // 2):
        lanes = slice(p * LANES, (p + 1) * LANES)
        qp = q[:, lanes]
        q_rows = jnp.concatenate([jnp.where(first, qp, zero), jnp.where(first, zero, qp)], axis=0)
        s = _dot_nt(kcat[:, lanes], q_rows) + bias_sc[p]
        e = jnp.exp2(s - jnp.max(s, axis=0, keepdims=True))
        pn = (e * (1.0 / jnp.sum(e, axis=0, keepdims=True))).astype(BF16)
        r = _dot_tn(pn, vcat[:, lanes])
        outs.append(jnp.where(first, r[0:CHUNK], r[CHUNK:2 * CHUNK]))
    o_ref[...] = jnp.concatenate(outs, axis=-1).astype(BF16)


def _attn_sample(u, q, k, v, kc, vc, first_chunk, n_seq):
    new = pl.BlockSpec((CHUNK, A_WIDTH), lambda b: (first_chunk + b, 0))
    cache = pl.BlockSpec((1, BAND_PAST, A_WIDTH), lambda b: (b, 0, 0))
    return pl.pallas_call(
        _attn_sample_kernel,
        grid=(n_seq,),
        in_specs=[pl.BlockSpec((A_HEADS, ROLL_W), lambda b: (0, 0)), new, new, new, cache, cache],
        out_specs=pl.BlockSpec((CHUNK, A_WIDTH), lambda b: (b, 0)),
        out_shape=jax.ShapeDtypeStruct((n_seq * CHUNK, A_WIDTH), BF16),
        scratch_shapes=[pltpu.VMEM((A_HEADS // 2, SAMPLE_KEYS, LANES), F32)],
        compiler_params=_params(),
        name="attn_sample",
    )(u, q, k, v, kc, vc)


GLA_CHUNKS = 4
GLA_SUB = 4


def _gla_block(n_chunks, gla_ref, la_ref, ltri_ref, g_ref, st_sc, o_ref):
    rows = n_chunks * CHUNK
    la = la_ref[...]
    la_hi, la_lo = _split(la)
    b = _dot(ltri_ref[...], la_hi) + _dot(ltri_ref[...], la_lo)
    b3 = b.reshape(n_chunks, CHUNK, B_KWIDTH)
    b_mid = b3[:, CHUNK // 2 - 1:CHUNK // 2, :]
    b_last = b3[:, CHUNK - 1:CHUNK, :]
    q = gla_ref[:, 0:B_KWIDTH].astype(F32).reshape(n_chunks, CHUNK, B_KWIDTH)
    k = gla_ref[:, B_KWIDTH:2 * B_KWIDTH].astype(F32).reshape(n_chunks, CHUNK, B_KWIDTH)
    q_start = (q * jnp.exp(b3)).reshape(rows, B_KWIDTH).astype(BF16)
    q_mid = (q * jnp.exp(b3 - b_mid)).reshape(rows, B_KWIDTH).astype(BF16)
    k_mid = (k * jnp.exp(b_mid - b3)).reshape(rows, B_KWIDTH).astype(BF16)
    k_end = (k * jnp.exp(b_last - b3)).reshape(rows, B_KWIDTH).astype(BF16)
    dec = jnp.exp(b_last)

    ti = lax.broadcasted_iota(jnp.int32, (2 * rows, rows), 0) & (rows - 1)
    si = lax.broadcasted_iota(jnp.int32, (2 * rows, rows), 1)
    causal = (si <= ti) & ((si >> LOG_CHUNK) == (ti >> LOG_CHUNK))
    first_r = lax.broadcasted_iota(jnp.int32, (rows, LANES), 1) < B_DK
    first_c = lax.broadcasted_iota(jnp.int32, (CHUNK, LANES), 1) < B_DK
    first_s = lax.broadcasted_iota(jnp.int32, (B_DV, LANES), 1) < B_DK

    def stack_heads(x, first):
        zero = jnp.zeros_like(x)
        return jnp.concatenate([jnp.where(first, x, zero), jnp.where(first, zero, x)], axis=0)

    for p in range(B_HEADS // 2):
        lanes = slice(p * LANES, (p + 1) * LANES)
        qs_p, qm_p, km_p, ke_p = q_start[:, lanes], q_mid[:, lanes], k_mid[:, lanes], k_end[:, lanes]
        v_pair = gla_ref[:, 2 * B_KWIDTH + 2 * p * B_DV:2 * B_KWIDTH + (2 * p + 2) * B_DV]
        sc = jnp.where(causal, _dot_nt(stack_heads(qm_p, first_r), km_p), 0.0)
        o2 = _dot(sc.astype(BF16), v_pair)
        intra = [o2[0:rows, 0:B_DV], o2[rows:2 * rows, B_DV:2 * B_DV]]
        inter = [[], []]
        st = st_sc[p]
        for c in range(n_chunks):
            cr = slice(c * CHUNK, (c + 1) * CHUNK)
            r2 = _dot_nt(stack_heads(qs_p[cr], first_c), st.astype(BF16))
            inter[0].append(r2[0:CHUNK])
            inter[1].append(r2[CHUNK:2 * CHUNK])
            u2 = _dot_tn(v_pair[cr], ke_p[cr])
            st = st * dec[c, :, lanes] + jnp.where(first_s, u2[0:B_DV], u2[B_DV:2 * B_DV])
        st_sc[p] = st
        for hh in range(2):
            h = 2 * p + hh
            o = intra[hh] + jnp.concatenate(inter[hh], axis=0)
            ms = jnp.mean(o * o, axis=-1, keepdims=True)
            on = o * lax.rsqrt(ms + EPS) * g_ref[...]
            r = gla_ref[:, 2 * B_KWIDTH + B_WIDTH + h * B_DV:2 * B_KWIDTH + B_WIDTH + (h + 1) * B_DV]
            o_ref[:, h * B_DV:(h + 1) * B_DV] = (on * _silu(r.astype(F32))).astype(BF16)


def _gla_prompt_kernel(gla_ref, la_ref, ltri_ref, g_ref, o_ref, sfin_ref, st_sc):
    @pl.when(pl.program_id(0) == 0)
    def _():
        st_sc[...] = jnp.zeros_like(st_sc)

    rows = GLA_CHUNKS * CHUNK
    for sub in range(GLA_SUB):
        part = pl.ds(sub * rows, rows)
        _gla_block(GLA_CHUNKS, gla_ref.at[part], la_ref.at[part], ltri_ref, g_ref, st_sc, o_ref.at[part])
    sfin_ref[...] = st_sc[...]


def _gla_sample_kernel(gla_ref, la_ref, ltri_ref, g_ref, s0_ref, o_ref, sfin_ref, st_sc):
    st_sc[...] = s0_ref[0]
    _gla_block(1, gla_ref, la_ref, ltri_ref, g_ref, st_sc, o_ref)
    sfin_ref[0] = st_sc[...]


def _ltri(n_chunks):
    r = np.arange(n_chunks * CHUNK)
    m = (r[None, :] <= r[:, None]) & (r[None, :] // CHUNK == r[:, None] // CHUNK)
    return jnp.asarray(m, BF16)


_GLA_W = 2 * B_KWIDTH + 2 * B_WIDTH
_ST_SHAPE = (B_HEADS // 2, B_DV, LANES)


def _gla_prompt(gla, la, g, n_steps):
    rows = GLA_SUB * GLA_CHUNKS * CHUNK
    const = lambda j: (0, 0)
    return pl.pallas_call(
        _gla_prompt_kernel,
        grid=(n_steps,),
        in_specs=[pl.BlockSpec((rows, _GLA_W), lambda j: (j, 0)),
                  pl.BlockSpec((rows, B_KWIDTH), lambda j: (j, 0)),
                  pl.BlockSpec((GLA_CHUNKS * CHUNK, GLA_CHUNKS * CHUNK), const),
                  pl.BlockSpec((1, B_DV), const)],
        out_specs=[pl.BlockSpec((rows, B_WIDTH), lambda j: (j, 0)),
                   pl.BlockSpec(_ST_SHAPE, lambda j: (0, 0, 0))],
        out_shape=[jax.ShapeDtypeStruct((n_steps * rows, B_WIDTH), BF16),
                   jax.ShapeDtypeStruct(_ST_SHAPE, F32)],
        scratch_shapes=[pltpu.VMEM(_ST_SHAPE, F32)],
        compiler_params=_params(),
        name="gla_prompt",
    )(gla, la, _ltri(GLA_CHUNKS), g)


def _gla_sample(gla, la, g, s0, first_chunk, n_seq):
    const = lambda b: (0, 0)
    st_spec = pl.BlockSpec((1,) + _ST_SHAPE, lambda b: (b, 0, 0, 0))
    return pl.pallas_call(
        _gla_sample_kernel,
        grid=(n_seq,),
        in_specs=[pl.BlockSpec((CHUNK, _GLA_W), lambda b: (first_chunk + b, 0)),
                  pl.BlockSpec((CHUNK, B_KWIDTH), lambda b: (first_chunk + b, 0)),
                  pl.BlockSpec((CHUNK, CHUNK), const),
                  pl.BlockSpec((1, B_DV), const),
                  st_spec],
        out_specs=[pl.BlockSpec((CHUNK, B_WIDTH), lambda b: (b, 0)), st_spec],
        out_shape=[jax.ShapeDtypeStruct((n_seq * CHUNK, B_WIDTH), BF16),
                   jax.ShapeDtypeStruct((n_seq,) + _ST_SHAPE, F32)],
        scratch_shapes=[pltpu.VMEM(_ST_SHAPE, F32)],
        compiler_params=_params(),
        name="gla_sample",
    )(gla, la, _ltri(1), g, s0)


def _state_to_pairs(s):
    lead = s.shape[:-3]
    s = s.reshape(lead + (B_HEADS // 2, 2, B_DK, B_DV))
    s = jnp.moveaxis(s, -1, -3)
    return s.reshape(lead + (B_HEADS // 2, B_DV, 2 * B_DK))


def _pairs_to_state(s):
    lead = s.shape[:-3]
    s = s.reshape(lead + (B_HEADS // 2, B_DV, 2, B_DK))
    s = jnp.moveaxis(s, -3, -1)
    return s.reshape(lead + (B_HEADS, B_DK, B_DV))


def _route(logits):
    lane = lax.broadcasted_iota(jnp.int32, logits.shape, 1)
    lane_f = lane.astype(F32)
    big = float(LANES)
    gmask = lane < N_GROUPS
    gl = jnp.where(gmask, logits, NEG)
    gmax = jnp.max(gl, axis=-1, keepdims=True)
    gsel = jnp.min(jnp.where(gl == gmax, lane_f, big), axis=-1, keepdims=True)
    gsum = jnp.sum(jnp.where(gmask, jnp.exp(gl - gmax), 0.0), axis=-1, keepdims=True)
    g_w = 1.0 / gsum
    e_lo = ROUTE_OFF + gsel * EXPERTS_PER_GROUP
    emask = (lane_f >= e_lo) & (lane_f < e_lo + EXPERTS_PER_GROUP)
    el = jnp.where(emask, logits, NEG)
    v1 = jnp.max(el, axis=-1, keepdims=True)
    i1 = jnp.min(jnp.where(el == v1, lane_f, big), axis=-1, keepdims=True)
    el2 = jnp.where(lane_f == i1, NEG, el)
    v2 = jnp.max(el2, axis=-1, keepdims=True)
    i2 = jnp.min(jnp.where(el2 == v2, lane_f, big), axis=-1, keepdims=True)
    t = jnp.exp(v2 - v1)
    w1 = g_w / (1.0 + t)
    w2 = g_w * t / (1.0 + t)
    return lane_f, i1, i2, w1, w2


ROW_PIECES = D_MODEL // 2 // LANES
SUBLANES = 8
ROW_TILE = ROW_PIECES * SUBLANES


def _pack_rows(z32_sc, x, rows):
    half = D_MODEL // 2
    out = []
    for c in range(ROW_PIECES):
        z32_sc[c, pl.ds(0, rows, stride=2), :] = x[:, c * LANES:(c + 1) * LANES]
        z32_sc[c, pl.ds(1, rows, stride=2), :] = x[:, half + c * LANES:half + (c + 1) * LANES]
        out.append(z32_sc[c].astype(BF16))
    return out


def _unpack_rows(z32_sc, pieces, rows):
    lo, hi = [], []
    for c in range(ROW_PIECES):
        z32_sc[c] = pieces[c].astype(F32)
        lo.append(z32_sc[c, pl.ds(0, rows, stride=2), :])
        hi.append(z32_sc[c, pl.ds(1, rows, stride=2), :])
    return jnp.concatenate(lo, axis=1), jnp.concatenate(hi, axis=1)


def _to_row_tiled(pieces, tokens):
    per_tile = pieces[0].shape[0] * SUBLANES // tokens
    return jnp.stack([p.reshape(tokens // SUBLANES, per_tile, LANES) for p in pieces], axis=1)


def _from_row_tiled(flat, tokens):
    per_tile = flat.shape[0] // (tokens // SUBLANES) // ROW_PIECES
    tiled = flat.reshape(tokens // SUBLANES, ROW_PIECES, per_tile, LANES)
    return [tiled[:, c].reshape(tokens // SUBLANES * per_tile, LANES) for c in range(ROW_PIECES)]


def _flatten_tiled(tiled):
    return tiled.reshape(-1, LANES)


def _outproj_kernel(n_ptiles, tiles_per_sb, oap_ref, oas_ref, obp_ref, obs_ref, wo_ref, xp_ref, xs_ref, mod_ref,
                    gffn_ref, wr_ref, br_ref, ltri_ref, x1_ref, h2p_ref, meta_ref, cnt_ref, z32_sc, cnt_sc):
    i = pl.program_id(0)
    is_prompt = i < n_ptiles
    x = jnp.where(is_prompt, xp_ref[...], xs_ref[...])
    oa = jnp.where(is_prompt, oap_ref[...], oas_ref[...])
    ob = jnp.where(is_prompt, obp_ref[...], obs_ref[...])
    mix = _dot(oa, wo_ref[0:A_WIDTH, :]) + _dot(ob, wo_ref[A_WIDTH:D_MODEL, :])
    gate1 = _rows_to_tokens(mod_ref[:, 2 * D_MODEL:3 * D_MODEL], D_MODEL)
    x1 = x + gate1 * mix
    x1_ref[...] = x1
    ms = jnp.mean(x1 * x1, axis=-1, keepdims=True)
    xn = x1 * lax.rsqrt(ms + EPS) * gffn_ref[...]
    sh = _rows_to_tokens(mod_ref[:, 3 * D_MODEL:4 * D_MODEL], D_MODEL)
    sc = _rows_to_tokens(mod_ref[:, 4 * D_MODEL:5 * D_MODEL], D_MODEL)
    h2 = xn * (1.0 + sc) + sh
    words = [pltpu.bitcast(p, U32) for p in _pack_rows(z32_sc, h2, TOK_TILE)]
    h2p_ref[...] = _to_row_tiled(words, TOK_TILE)

    lane_f, i1, i2, w1, w2 = _route(_dot3(h2, wr_ref[...]) + br_ref[...])

    @pl.when(lax.rem(i, tiles_per_sb) == 0)
    def _():
        cnt_sc[...] = jnp.zeros_like(cnt_sc)

    sel = jnp.where((lane_f == i1) | (lane_f == i2), 1.0, 0.0).astype(BF16)
    before = _dot(ltri_ref[...], sel) + cnt_sc[0:1, :]
    rank1 = jnp.sum(jnp.where(lane_f == i1, before, 0.0), axis=-1, keepdims=True)
    rank2 = jnp.sum(jnp.where(lane_f == i2, before, 0.0), axis=-1, keepdims=True)
    cnt = cnt_sc[...] + _dot(jnp.ones((8, TOK_TILE), BF16), sel)
    cnt_sc[...] = cnt
    cnt_ref[0] = cnt
    cols = (i1, i2, rank1, rank2, w1, w2)
    meta = jnp.zeros_like(lane_f)
    for c, col in enumerate(cols):
        meta = jnp.where(lane_f == float(c), col, meta)
    meta_ref[...] = meta


def _outproj(oa_p, oa_s, ob_p, ob_s, w_out, xp, xs, mod, gffn, wr, br, n_ptiles, n_stiles, prep, sb):
    n_tiles = n_ptiles + n_stiles
    t = n_tiles * TOK_TILE
    pblocks = prep // ROWS_PER_TILE
    tiles_per_sb = sb // TOK_TILE
    const = lambda i: (0, 0)
    row = lambda i: (i, 0)
    prow = lambda i: (jnp.minimum(i, n_ptiles - 1), 0)
    srow = lambda i: (jnp.maximum(i - n_ptiles, 0), 0)
    r = np.arange(TOK_TILE)
    ltri = jnp.asarray(r[None, :] < r[:, None], BF16)
    return pl.pallas_call(
        functools.partial(_outproj_kernel, n_ptiles, tiles_per_sb),
        grid=(n_tiles,),
        in_specs=[pl.BlockSpec((TOK_TILE, A_WIDTH), prow),
                  pl.BlockSpec((TOK_TILE, A_WIDTH), srow),
                  pl.BlockSpec((TOK_TILE, B_WIDTH), prow),
                  pl.BlockSpec((TOK_TILE, B_WIDTH), srow),
                  pl.BlockSpec((D_MODEL, D_MODEL), const),
                  pl.BlockSpec((TOK_TILE, D_MODEL), prow),
                  pl.BlockSpec((TOK_TILE, D_MODEL), srow),
                  pl.BlockSpec((ROWS_PER_TILE, 6 * D_MODEL),
                               lambda i: (jnp.maximum(i - n_ptiles + pblocks, 0), 0)),
                  pl.BlockSpec((1, D_MODEL), const),
                  pl.BlockSpec((D_MODEL, LANES), const),
                  pl.BlockSpec((1, LANES), const),
                  pl.BlockSpec((TOK_TILE, TOK_TILE), const)],
        out_specs=[pl.BlockSpec((TOK_TILE, D_MODEL), row),
                   pl.BlockSpec((TOK_TILE // SUBLANES, ROW_PIECES, SUBLANES, LANES), lambda i: (i, 0, 0, 0)),
                   pl.BlockSpec((TOK_TILE, LANES), row),
                   pl.BlockSpec((1, 8, LANES), lambda i: (i // tiles_per_sb, 0, 0))],
        out_shape=[jax.ShapeDtypeStruct((t, D_MODEL), F32),
                   jax.ShapeDtypeStruct((t // SUBLANES, ROW_PIECES, SUBLANES, LANES), U32),
                   jax.ShapeDtypeStruct((t, LANES), F32),
                   jax.ShapeDtypeStruct((t // sb, 8, LANES), F32)],
        scratch_shapes=[pltpu.VMEM((D_MODEL // 2 // LANES, 2 * TOK_TILE, LANES), F32),
                        pltpu.VMEM((8, LANES), F32)],
        compiler_params=_params(),
        name="outproj",
    )(oa_p, oa_s, ob_p, ob_s, w_out, xp, xs, mod, gffn, wr, br, ltri)


MOE_SUPER_BLOCK = 2048
SEG_ALIGN = SUBLANES
CHUNK_BF16_ROWS = 2 * SEG_ALIGN * ROW_PIECES
SEG_BITS = 9
PAD_BITS = 5
FFN_ROWS = 512
PLAN_ROWS = LANES


def _local_rows(sb):
    return 2 * sb + N_EXPERTS * SEG_ALIGN


def _sorted_tiles(n_tokens, sb):
    rows = 2 * n_tokens + (n_tokens // sb) * N_EXPERTS * SEG_ALIGN + N_EXPERTS * FFN_ROWS
    return -(-rows // FFN_ROWS)


def _moe_plan_kernel(n_blocks, total_chunks, meta_ref, cnt_ref, ustrict_ref, lstrict_ref,
                     posw_ref, addr_ref, tab_ref, tile_ref):
    b = pl.program_id(0)
    per_tile = FFN_ROWS // SEG_ALIGN

    @pl.when(b == 0)
    def _():
        cnt = cnt_ref[...]
        chunks = jnp.floor((cnt + (SEG_ALIGN - 1)) * (1.0 / SEG_ALIGN))
        chunks_b = chunks.astype(BF16)
        loc = _dot(chunks_b, ustrict_ref[...])
        before = _dot(lstrict_ref[...], chunks_b)
        tot = _dot(jnp.ones((PLAN_ROWS, PLAN_ROWS), BF16), chunks_b)
        tiles = jnp.floor((tot + (per_tile - 1)) * (1.0 / per_tile))
        tile_off = _dot(tiles.astype(BF16), ustrict_ref[...])
        n_tiles = jnp.sum(tiles[0:1], axis=-1, keepdims=True)
        lane1 = lax.broadcasted_iota(jnp.int32, (PLAN_ROWS, LANES), 1)
        tail = lane1 == ROUTE_OFF + N_EXPERTS
        pad_off = jnp.where(tail, n_tiles * per_tile, tile_off * per_tile + tot)
        pad_n = jnp.where(tail, total_chunks - n_tiles * per_tile, tiles * per_tile - tot)
        row = lax.broadcasted_iota(jnp.int32, (PLAN_ROWS, LANES), 0)
        tab_ref[0] = loc
        tab_ref[1] = chunks
        tab_ref[2] = tile_off * per_tile + before
        tab_ref[3] = jnp.where(row == 0, pad_off, jnp.where(row == 1, pad_n, jnp.where(row == 2, n_tiles, 0.0)))
        t_idx = lax.broadcasted_iota(jnp.int32, tile_ref.shape, 0).astype(F32)
        lane_t = lax.broadcasted_iota(jnp.int32, tile_ref.shape, 1)
        is_expert = (lane_t >= ROUTE_OFF) & (lane_t < ROUTE_OFF + N_EXPERTS)
        ends = (tile_off + tiles)[0:1, :]
        owner = jnp.sum(jnp.where(is_expert & (ends <= t_idx), 1.0, 0.0), axis=-1, keepdims=True)
        tile_ref[...] = jnp.broadcast_to(jnp.minimum(owner, N_EXPERTS - 1.0), tile_ref.shape)

    own = jnp.floor((cnt_ref[pl.ds(b, 1), :] + (SEG_ALIGN - 1)) * (1.0 / SEG_ALIGN))
    own_off = _dot(jnp.broadcast_to(own, (SUBLANES, LANES)).astype(BF16), ustrict_ref[...]) * SEG_ALIGN
    meta = meta_ref[...]
    lane_f = lax.broadcasted_iota(jnp.int32, meta.shape, 1).astype(F32)
    off_row = own_off[0:1, :]
    pos = []
    for k in range(2):
        e_lane = meta[:, k:k + 1]
        base = jnp.sum(jnp.where(lane_f == e_lane, off_row, 0.0), axis=-1, keepdims=True)
        p = base + meta[:, 2 + k:3 + k]
        tile = jnp.floor(p * (1.0 / SUBLANES))
        pos.append(tile * (ROW_TILE - SUBLANES) + p)
    out = jnp.zeros_like(meta)
    for c, col in enumerate((pos[0], pos[1], meta[:, 4:5], meta[:, 5:6])):
        out = jnp.where(lane_f == float(c), col, out)
    posw_ref[...] = out
    addr_ref[0] = out.T[0:SUBLANES]


def _moe_plan(meta, cnt, sb):
    n_blocks = meta.shape[0] // sb
    assert n_blocks <= PLAN_ROWS and sb // SEG_ALIGN <= 256
    n_tiles = _sorted_tiles(meta.shape[0], sb)
    tile_rows = -(-n_tiles // SUBLANES) * SUBLANES
    r = np.arange(LANES)
    ustrict = jnp.asarray(r[:, None] < r[None, :], BF16)
    lstrict = jnp.asarray(r[None, :] < r[:, None], BF16)
    cnt_all = jnp.pad(cnt[:, 0, :], ((0, PLAN_ROWS - n_blocks), (0, 0)))
    const = lambda s: (0, 0)
    posw, addr, tab, tile_owner = pl.pallas_call(
        functools.partial(_moe_plan_kernel, n_blocks, float(n_tiles * (FFN_ROWS // SEG_ALIGN))),
        grid=(n_blocks,),
        in_specs=[pl.BlockSpec((sb, LANES), lambda s: (s, 0)),
                  pl.BlockSpec((PLAN_ROWS, LANES), const),
                  pl.BlockSpec((LANES, LANES), const),
                  pl.BlockSpec((PLAN_ROWS, PLAN_ROWS), const)],
        out_specs=[pl.BlockSpec((sb, LANES), lambda s: (s, 0)),
                   pl.BlockSpec((1, SUBLANES, sb), lambda s: (s, 0, 0)),
                   pl.BlockSpec((4, PLAN_ROWS, LANES), lambda s: (0, 0, 0)),
                   pl.BlockSpec((tile_rows, LANES), const)],
        out_shape=[jax.ShapeDtypeStruct(meta.shape, F32),
                   jax.ShapeDtypeStruct((n_blocks, SUBLANES, sb), F32),
                   jax.ShapeDtypeStruct((4, PLAN_ROWS, LANES), F32),
                   jax.ShapeDtypeStruct((tile_rows, LANES), F32)],
        compiler_params=_params(),
        name="moe_plan",
    )(meta, cnt_all, ustrict, lstrict)
    experts = slice(ROUTE_OFF, ROUTE_OFF + N_EXPERTS)
    to_i32 = lambda x: x.astype(jnp.int32).reshape(-1)
    plan = dict(
        loc=to_i32(tab[0, :n_blocks, experts]), n=to_i32(tab[1, :n_blocks, experts]),
        dst=to_i32(tab[2, :n_blocks, experts]),
        pad_off=to_i32(tab[3, 0, ROUTE_OFF:ROUTE_OFF + N_EXPERTS + 1]),
        pad_n=to_i32(tab[3, 1, ROUTE_OFF:ROUTE_OFF + N_EXPERTS + 1]),
        n_tiles=to_i32(tab[3, 2, 0:1]),
        owner=to_i32(tile_owner[:n_tiles, 0]))
    return posw, to_i32(addr[:, 0, :]), to_i32(addr[:, 1, :]), plan


def _token_rows(start):
    return pl.ds(start, ROW_PIECES, stride=SUBLANES)


def _pow2_copies(src_ref, dst_ref, src_chunk, dst_chunk, n, n_bits, sem, act):
    done = 0
    for k in reversed(range(n_bits)):
        take = (n >> k) & 1
        rows = CHUNK_BF16_ROWS << k
        src0 = 0 if src_chunk is None else pl.multiple_of((src_chunk + done) * CHUNK_BF16_ROWS, CHUNK_BF16_ROWS)
        dst0 = pl.multiple_of((dst_chunk + done) * CHUNK_BF16_ROWS, CHUNK_BF16_ROWS)

        @pl.when(take == 1)
        def _(src0=src0, dst0=dst0, rows=rows):
            act(pltpu.make_async_copy(src_ref.at[pl.ds(src0, rows)], dst_ref.at[pl.ds(dst0, rows)], sem))

        done = done + take * (1 << k)


def _segment_copies(block, loc_ref, n_ref, dst_ref, local_ref, global_ref, to_global, sem, act):
    def per_expert(e, carry):
        seg = block * N_EXPERTS + e
        if to_global:
            _pow2_copies(local_ref, global_ref, loc_ref[seg], dst_ref[seg], n_ref[seg], SEG_BITS, sem, act)
        else:
            _pow2_copies(global_ref, local_ref, dst_ref[seg], loc_ref[seg], n_ref[seg], SEG_BITS, sem, act)
        return carry

    lax.fori_loop(0, N_EXPERTS, per_expert, 0)


def _zero_fill(zero_ref, global_ref, padoff_ref, padn_ref, sem, act):
    full = 1 << PAD_BITS

    def per_pad(e, carry):
        def per_full(c, inner):
            dst0 = pl.multiple_of((padoff_ref[e] + c * full) * CHUNK_BF16_ROWS, CHUNK_BF16_ROWS)
            act(pltpu.make_async_copy(zero_ref, global_ref.at[pl.ds(dst0, full * CHUNK_BF16_ROWS)], sem))
            return inner

        n_full = padn_ref[e] >> PAD_BITS
        lax.fori_loop(0, n_full, per_full, 0)
        _pow2_copies(zero_ref, global_ref, None, padoff_ref[e] + n_full * full, padn_ref[e] & (full - 1),
                     PAD_BITS, sem, act)
        return carry

    lax.fori_loop(0, N_EXPERTS + 1, per_pad, 0)


STAGE_SLAB = 1024


def _restage(src_sc, dst_sc, dst_dtype):
    ratio = dst_sc.shape[0] / src_sc.shape[0]
    n_slabs = src_sc.shape[0] // (STAGE_SLAB if ratio > 1 else 2 * STAGE_SLAB)
    src_rows = src_sc.shape[0] // n_slabs
    dst_rows = dst_sc.shape[0] // n_slabs

    def slab(i, carry):
        s0 = pl.multiple_of(i * src_rows, src_rows)
        d0 = pl.multiple_of(i * dst_rows, dst_rows)
        dst_sc[pl.ds(d0, dst_rows), :] = pltpu.bitcast(src_sc[pl.ds(s0, src_rows), :], dst_dtype)
        return carry

    lax.fori_loop(0, n_slabs, slab, 0)


def _moe_dispatch_kernel(sb, n_blocks, loc_ref, n_ref, dst_ref, padoff_ref, padn_ref,
                         h2p_ref, a1_ref, a2_ref, xs_hbm, local_sc, stage_sc, zero_sc, sems, zero_sem):
    b = pl.program_id(0)
    slot = b & 1

    def segments(block, buf, act):
        _segment_copies(block, loc_ref, n_ref, dst_ref, stage_sc.at[buf], xs_hbm, True, sems.at[buf], act)

    local_sc[...] = jnp.zeros_like(local_sc)

    def step(g, carry):
        src = pl.multiple_of(g * ROW_TILE, ROW_TILE)
        for u in range(SUBLANES):
            t = g * SUBLANES + u
            row = h2p_ref[_token_rows(src + u), :]
            local_sc[_token_rows(a1_ref[t]), :] = row
            local_sc[_token_rows(a2_ref[t]), :] = row
        return carry

    lax.fori_loop(0, sb // SUBLANES, step, 0)

    @pl.when(b > 0)
    def _():
        segments(b - 1, 1 - slot, lambda c: c.wait())

    _restage(local_sc, stage_sc.at[slot], BF16)
    segments(b, slot, lambda c: c.start())

    @pl.when(b == 0)
    def _():
        zero_sc[...] = jnp.zeros_like(zero_sc)
        _zero_fill(zero_sc, xs_hbm, padoff_ref, padn_ref, zero_sem, lambda c: c.start())
        _zero_fill(zero_sc, xs_hbm, padoff_ref, padn_ref, zero_sem, lambda c: c.wait())

    @pl.when(b == n_blocks - 1)
    def _():
        segments(b, slot, lambda c: c.wait())


def _smem_vec(n, index_map):
    return pl.BlockSpec((n,), index_map, memory_space=pltpu.SMEM)


def _moe_dispatch(h2p, a1, a2, plan, sb, n_tiles):
    n_blocks = h2p.shape[0] // (sb * ROW_PIECES)
    local_flat = _local_rows(sb) * ROW_PIECES
    vec = _smem_vec(sb, lambda s, *_: (s,))
    return pl.pallas_call(
        functools.partial(_moe_dispatch_kernel, sb, n_blocks),
        grid_spec=pltpu.PrefetchScalarGridSpec(
            num_scalar_prefetch=5,
            grid=(n_blocks,),
            in_specs=[pl.BlockSpec((sb * ROW_PIECES, LANES), lambda s, *_: (s, 0)), vec, vec],
            out_specs=pl.BlockSpec(memory_space=pl.ANY),
            scratch_shapes=[pltpu.VMEM((local_flat, LANES), U32),
                            pltpu.VMEM((2, 2 * local_flat, LANES), BF16),
                            pltpu.VMEM(((1 << PAD_BITS) * CHUNK_BF16_ROWS, LANES), BF16),
                            pltpu.SemaphoreType.DMA((2,)),
                            pltpu.SemaphoreType.DMA(())]),
        out_shape=jax.ShapeDtypeStruct((n_tiles * FFN_ROWS * ROW_PIECES * 2, LANES), BF16),
        compiler_params=_params(),
        name="moe_dispatch",
    )(plan["loc"], plan["n"], plan["dst"], plan["pad_off"], plan["pad_n"], h2p, a1, a2)


def _moe_ffn_kernel(owner_ref, ntiles_ref, xs_ref, wg_ref, wu_ref, wd_ref, ys_ref,
                    wg_sc, wu_sc, wd_sc, z32_sc):
    i = pl.program_id(0)
    half = D_MODEL // 2
    used = i < ntiles_ref[0]

    @pl.when(used & ((i == 0) | (owner_ref[i] != owner_ref[jnp.maximum(i - 1, 0)])))
    def _():
        wg_sc[...] = wg_ref[0].astype(BF16)
        wu_sc[...] = wu_ref[0].astype(BF16)
        wd_sc[...] = wd_ref[0].astype(BF16)

    @pl.when(used)
    def _():
        lo, hi = _unpack_rows(z32_sc, _from_row_tiled(xs_ref[...], FFN_ROWS), FFN_ROWS)
        lo, hi = lo.astype(BF16), hi.astype(BF16)
        g = _dot(lo, wg_sc[0:half, :]) + _dot(hi, wg_sc[half:D_MODEL, :])
        u = _dot(lo, wu_sc[0:half, :]) + _dot(hi, wu_sc[half:D_MODEL, :])
        y = _dot((_silu(g) * u).astype(BF16), wd_sc[...])
        ys_ref[...] = _flatten_tiled(_to_row_tiled(_pack_rows(z32_sc, y, FFN_ROWS), FFN_ROWS))

    @pl.when(jnp.logical_not(used))
    def _():
        ys_ref[...] = jnp.zeros_like(ys_ref)


def _moe_ffn(xs, plan, wg, wu, wd):
    flat = FFN_ROWS * ROW_PIECES * 2
    n_tiles = xs.shape[0] // flat
    last_used = lambda i, owner, nt: jnp.minimum(i, nt[0] - 1)
    wspec = lambda shape: pl.BlockSpec((1,) + shape, lambda i, owner, nt: (owner[last_used(i, owner, nt)], 0, 0))
    return pl.pallas_call(
        _moe_ffn_kernel,
        grid_spec=pltpu.PrefetchScalarGridSpec(
            num_scalar_prefetch=2,
            grid=(n_tiles,),
            in_specs=[pl.BlockSpec((flat, LANES), lambda i, owner, nt: (last_used(i, owner, nt), 0)),
                      wspec((D_MODEL, EXPERT_FF)), wspec((D_MODEL, EXPERT_FF)), wspec((EXPERT_FF, D_MODEL))],
            out_specs=pl.BlockSpec((flat, LANES), lambda i, owner, nt: (i, 0)),
            scratch_shapes=[pltpu.VMEM((D_MODEL, EXPERT_FF), BF16),
                            pltpu.VMEM((D_MODEL, EXPERT_FF), BF16),
                            pltpu.VMEM((EXPERT_FF, D_MODEL), BF16),
                            pltpu.VMEM((ROW_PIECES, 2 * FFN_ROWS, LANES), F32)]),
        out_shape=jax.ShapeDtypeStruct(xs.shape, BF16),
        compiler_params=_params(),
        name="moe_ffn",
    )(plan["owner"], plan["n_tiles"], xs, wg, wu, wd)


def _moe_combine_kernel(n_psb, n_blocks, loc_ref, n_ref, dst_ref,
                        ys_hbm, a1_ref, a2_ref, posw_ref, x1_ref, mod_ref, yp_ref, yo_ref,
                        local_sc, stage_sc, g1_sc, g2_sc, z32_sc, sems):
    s = pl.program_id(0)
    slot = s & 1

    def segments(block, buf, act):
        _segment_copies(block, loc_ref, n_ref, dst_ref, stage_sc.at[buf], ys_hbm, False, sems.at[buf], act)

    @pl.when(pl.program_id(1) == 0)
    def _():
        @pl.when(s == 0)
        def _():
            segments(s, slot, lambda c: c.start())

        segments(s, slot, lambda c: c.wait())
        _restage(stage_sc.at[slot], local_sc, U32)

        @pl.when(s + 1 < n_blocks)
        def _():
            segments(s + 1, 1 - slot, lambda c: c.start())

    def step(g, carry):
        dst = pl.multiple_of(g * ROW_TILE, ROW_TILE)
        for u in range(SUBLANES):
            t = g * SUBLANES + u
            g1_sc[_token_rows(dst + u), :] = local_sc[_token_rows(a1_ref[t]), :]
            g2_sc[_token_rows(dst + u), :] = local_sc[_token_rows(a2_ref[t]), :]
        return carry

    lax.fori_loop(0, TOK_TILE // SUBLANES, step, 0)
    halves = lambda g_sc: [pltpu.bitcast(p, BF16) for p in _from_row_tiled(g_sc[...], TOK_TILE)]
    lo1, hi1 = _unpack_rows(z32_sc, halves(g1_sc), TOK_TILE)
    lo2, hi2 = _unpack_rows(z32_sc, halves(g2_sc), TOK_TILE)
    w1, w2 = posw_ref[:, 2:3], posw_ref[:, 3:4]
    moe = jnp.concatenate([w1 * lo1 + w2 * lo2, w1 * hi1 + w2 * hi2], axis=1)
    gate2 = _rows_to_tokens(mod_ref[:, 5 * D_MODEL:6 * D_MODEL], D_MODEL)
    y = x1_ref[...] + gate2 * moe

    @pl.when(s < n_psb)
    def _():
        yp_ref[...] = y

    @pl.when(s >= n_psb)
    def _():
        yo_ref[...] = y


def _moe_combine(ys, a1, a2, posw, plan, x1, mod, sb, n_ptiles, n_stiles, prep):
    tps = sb // TOK_TILE
    n_blocks = (n_ptiles + n_stiles) // tps
    n_psb = n_ptiles // tps
    pblocks = prep // ROWS_PER_TILE
    tile = lambda s, j: s * tps + j
    vec = _smem_vec(TOK_TILE, lambda s, j, *_: (tile(s, j),))
    return pl.pallas_call(
        functools.partial(_moe_combine_kernel, n_psb, n_blocks),
        grid_spec=pltpu.PrefetchScalarGridSpec(
            num_scalar_prefetch=3,
            grid=(n_blocks, tps),
            in_specs=[pl.BlockSpec(memory_space=pl.ANY), vec, vec,
                      pl.BlockSpec((TOK_TILE, LANES), lambda s, j, *_: (tile(s, j), 0)),
                      pl.BlockSpec((TOK_TILE, D_MODEL), lambda s, j, *_: (tile(s, j), 0)),
                      pl.BlockSpec((ROWS_PER_TILE, 6 * D_MODEL),
                                   lambda s, j, *_: (jnp.maximum(tile(s, j) - n_ptiles + pblocks, 0), 0))],
            out_specs=[pl.BlockSpec((TOK_TILE, D_MODEL),
                                    lambda s, j, *_: (jnp.minimum(tile(s, j), n_ptiles - 1), 0)),
                       pl.BlockSpec((TOK_TILE, D_MODEL),
                                    lambda s, j, *_: (jnp.maximum(tile(s, j) - n_ptiles, 0), 0))],
            scratch_shapes=[pltpu.VMEM((_local_rows(sb) * ROW_PIECES, LANES), U32),
                            pltpu.VMEM((2, _local_rows(sb) * ROW_PIECES * 2, LANES), BF16),
                            pltpu.VMEM((TOK_TILE * ROW_PIECES, LANES), U32),
                            pltpu.VMEM((TOK_TILE * ROW_PIECES, LANES), U32),
                            pltpu.VMEM((ROW_PIECES, 2 * TOK_TILE, LANES), F32),
                            pltpu.SemaphoreType.DMA((2,))]),
        out_shape=[jax.ShapeDtypeStruct((n_ptiles * TOK_TILE, D_MODEL), F32),
                   jax.ShapeDtypeStruct((n_stiles * TOK_TILE, D_MODEL), F32)],
        compiler_params=_params(2),
        name="moe_combine",
    )(plan["loc"], plan["n"], plan["dst"], ys, a1, a2, posw, x1, mod)


def _layer(xp, xs, cache_k, cache_v, state, c_prompt, c_sample, norm_mix_g, norm_ffn_g, w_ada, b_ada, w_in,
           q_norm_g, k_norm_g, rel_bias, w_gate_up, b_gate, gla_norm_g, w_out, w_route_group,
           b_route_group, w_route_expert, b_route_expert, w_exp_gate, w_exp_up, w_exp_down):
    batch, seq, _ = xp.shape
    n_seq, dec_seq, _ = xs.shape
    assert batch == 1 and dec_seq == CHUNK and cache_k.shape[1] == BAND_PAST
    assert seq % TOK_TILE == 0 and seq >= BAND_PAST and (n_seq * CHUNK) % TOK_TILE == 0
    assert seq % (ATTN_SUB * Q_ROWS) == 0 and seq % (GLA_SUB * GLA_CHUNKS * CHUNK) == 0
    n_ptok, n_stok = seq, n_seq * CHUNK
    n_ptiles, n_stiles = n_ptok // TOK_TILE, n_stok // TOK_TILE
    sb = MOE_SUPER_BLOCK if (n_ptok % MOE_SUPER_BLOCK == 0 and n_stok % MOE_SUPER_BLOCK == 0) else TOK_TILE
    prep = ROWS_PER_TILE

    xp2 = xp.reshape(n_ptok, D_MODEL)
    xs2 = xs.reshape(n_stok, D_MODEL)
    c_rows = jnp.concatenate([jnp.broadcast_to(c_prompt, (prep, D_MODEL)), c_sample], axis=0)
    mod = _adaln(c_rows, w_ada, b_ada)

    w_main = w_in[:, 0:IN_MAIN].astype(BF16)
    w_gr = jnp.pad(w_in[:, IN_MAIN:], ((0, 0), (0, LANES - GATE_RANK))).astype(BF16)
    wgu_p = jnp.pad(w_gate_up, ((0, LANES - GATE_RANK), (0, 0))).astype(BF16)
    head = np.arange(A_WIDTH) // A_HEAD_DIM
    bd = jnp.asarray(head[:, None] == head[None, :], BF16)
    gq = jnp.tile(q_norm_g, A_HEADS).reshape(1, A_WIDTH)
    gk = jnp.tile(k_norm_g, A_HEADS).reshape(1, A_WIDTH)
    q, k, v, kf, vf, gla, la = _inproj(
        xp2, xs2, mod, norm_mix_g.reshape(1, D_MODEL), w_main, w_gr, bd, gq, gk, wgu_p,
        b_gate.reshape(1, B_KWIDTH), n_ptiles, n_stiles, prep)

    first_chunk = n_ptok // CHUNK
    oa_p = _attn_prompt(rel_bias[:, _bias_lanes(ATTN_WIN * Q_ROWS)], q, k, v, n_ptok // (ATTN_SUB * Q_ROWS))
    oa_s = _attn_sample(rel_bias[:, _bias_lanes(SAMPLE_KEYS)], q, k, v,
                        cache_k.reshape(n_seq, BAND_PAST, A_WIDTH).astype(BF16),
                        cache_v.reshape(n_seq, BAND_PAST, A_WIDTH).astype(BF16),
                        first_chunk, n_seq)
    g_gla = gla_norm_g.reshape(1, B_DV)
    ob_p, sfin_p = _gla_prompt(gla, la, g_gla, n_ptok // (GLA_SUB * GLA_CHUNKS * CHUNK))
    ob_s, sfin_s = _gla_sample(gla, la, g_gla, _state_to_pairs(state), first_chunk, n_seq)

    wr = jnp.pad(jnp.concatenate([w_route_group, w_route_expert], axis=1),
                 ((0, 0), (0, LANES - N_GROUPS - N_EXPERTS)))
    br = jnp.pad(jnp.concatenate([b_route_group, b_route_expert]), (0, LANES - N_GROUPS - N_EXPERTS))
    x1, h2p, meta, cnt = _outproj(oa_p, oa_s, ob_p, ob_s, w_out.astype(BF16), xp2, xs2, mod,
                                  norm_ffn_g.reshape(1, D_MODEL), wr, br.reshape(1, LANES),
                                  n_ptiles, n_stiles, prep, sb)

    posw, a1, a2, plan = _moe_plan(meta, cnt, sb)
    xs_sorted = _moe_dispatch(h2p.reshape(-1, LANES), a1, a2, plan, sb, _sorted_tiles(n_ptok + n_stok, sb))
    ys_sorted = _moe_ffn(xs_sorted, plan, w_exp_gate, w_exp_up, w_exp_down)
    yp, ys = _moe_combine(ys_sorted, a1, a2, posw, plan, x1, mod, sb, n_ptiles, n_stiles, prep)

    tail = min(BAND_PAST, seq)
    heads = (A_HEADS, A_HEAD_DIM)
    return (yp.reshape(1, seq, D_MODEL), ys.reshape(n_seq, CHUNK, D_MODEL),
            kf[TOK_TILE - tail:TOK_TILE].reshape((1, tail) + heads),
            vf[TOK_TILE - tail:TOK_TILE].reshape((1, tail) + heads),
            _pairs_to_state(sfin_p)[None],
            kf[TOK_TILE:].reshape((n_seq, CHUNK) + heads),
            vf[TOK_TILE:].reshape((n_seq, CHUNK) + heads),
            _pairs_to_state(sfin_s))


def kernel(x_prompt, x_sample, cache_a_k, cache_a_v, state_gla, c_prompt, c_sample, norm_mix_g, norm_ffn_g,
           w_ada, b_ada, w_in, q_norm_g, k_norm_g, rel_bias, w_gate_up, b_gate, gla_norm_g, w_out,
           w_route_group, b_route_group, w_route_expert, b_route_expert, w_exp_gate, w_exp_up, w_exp_down):
    depth = w_in.shape[0]
    yp, ys = x_prompt, x_sample
    outs = [[] for _ in range(6)]
    for l in range(depth):
        yp, ys, kp, vp, sp, ks, vs, ss = _layer(
            yp, ys, cache_a_k[l], cache_a_v[l], state_gla[l], c_prompt, c_sample, norm_mix_g[l], norm_ffn_g[l],
            w_ada[l], b_ada[l], w_in[l], q_norm_g[l], k_norm_g[l], rel_bias[l], w_gate_up[l], b_gate[l],
            gla_norm_g[l], w_out[l], w_route_group[l], b_route_group[l], w_route_expert[l], b_route_expert[l],
            w_exp_gate[l], w_exp_up[l], w_exp_down[l])
        for lst, val in zip(outs, (kp, vp, sp, ks, vs, ss)):
            lst.append(val)
    return (yp, ys) + tuple(jnp.stack(o) for o in outs)
```

```python
import functools

import numpy as np
import jax
import jax.numpy as jnp
from jax import lax
from jax.experimental import pallas as pl
from jax.experimental.pallas import tpu as pltpu

F32 = jnp.float32
BF16 = jnp.bfloat16
U32 = jnp.uint32

D_MODEL = 1024
CHUNK = 64
LOG_CHUNK = 6
BAND_CHUNKS = 8
BAND_PAST = BAND_CHUNKS * CHUNK
A_WIDTH = 512
A_HEADS = 8
A_HEAD_DIM = 64
MAX_REL = 128
N_REL = CHUNK + MAX_REL
B_WIDTH = 512
B_HEADS = 4
B_DV = 128
B_DK = 64
B_KWIDTH = 256
GATE_RANK = 16
GATE_TAU = 16.0
N_GROUPS = 4
EXPERTS_PER_GROUP = 8
N_EXPERTS = 32
EXPERT_FF = 256
EPS = 1e-6

LANES = 128
IN_MAIN = 3 * A_WIDTH + 2 * B_KWIDTH + 2 * B_WIDTH
TOK_TILE = 512
ROWS_PER_TILE = TOK_TILE // CHUNK
Q_CHUNKS = 4
Q_ROWS = Q_CHUNKS * CHUNK
ROLL_W = 1024
NEG = -1e30
ROUTE_OFF = N_GROUPS
VMEM_LIMIT = 56 * 1024 * 1024


def _params(n_axes=1):
    return pltpu.CompilerParams(dimension_semantics=("arbitrary",) * n_axes,
                                vmem_limit_bytes=VMEM_LIMIT)


def _split(a):
    hi = a.astype(BF16)
    lo = (a - hi.astype(F32)).astype(BF16)
    return hi, lo


def _dot(a, b):
    return jnp.dot(a, b, preferred_element_type=F32)


def _dot3(a, b):
    ah, al = _split(a)
    bh, bl = _split(b)
    return _dot(ah, bh) + _dot(al, bh) + _dot(ah, bl)


def _dot_nt(a, b):
    return lax.dot_general(a, b, (((1,), (1,)), ((), ())), preferred_element_type=F32)


def _dot_tn(a, b):
    return lax.dot_general(a, b, (((0,), (0,)), ((), ())), preferred_element_type=F32)


def _silu(x):
    return x / (1.0 + jnp.exp(-x))


def _rows_to_tokens(rows, n):
    r = rows.shape[0]
    return jnp.broadcast_to(rows[:, None, :], (r, CHUNK, n)).reshape(r * CHUNK, n)


def _adaln_kernel(c_ref, w_ref, b_ref, o_ref):
    a = _silu(c_ref[...])
    o_ref[...] = _dot3(a, w_ref[...]) + b_ref[...]


def _adaln(c_rows, w_ada, b_ada):
    r = c_rows.shape[0]
    n = w_ada.shape[1]
    tn = 1024
    return pl.pallas_call(
        _adaln_kernel,
        grid=(n // tn,),
        in_specs=[pl.BlockSpec((r, D_MODEL), lambda j: (0, 0)),
                  pl.BlockSpec((D_MODEL, tn), lambda j: (0, j)),
                  pl.BlockSpec((1, tn), lambda j: (0, j))],
        out_specs=pl.BlockSpec((r, tn), lambda j: (0, j)),
        out_shape=jax.ShapeDtypeStruct((r, n), F32),
        compiler_params=_params(),
        name="adaln",
    )(c_rows, w_ada, b_ada.reshape(1, n))


INPROJ_SUB = 2


def _head_rms(z, bd_ref, g):
    ms = _dot((z * z).astype(BF16), bd_ref[...]) * (1.0 / A_HEAD_DIM)
    return z * lax.rsqrt(ms + EPS) * g


def _inproj_kernel(n_ptiles, xp_ref, xs_ref, mod_ref, gmix_ref, w_ref, wgr_ref, bd_ref, gq_ref, gk_ref,
                   wgu_ref, bg_ref,
                   q_ref, k_ref, v_ref, kf_ref, vf_ref, gla_ref, la_ref):
    i = pl.program_id(0)
    sub = TOK_TILE // INPROJ_SUB
    for s in range(INPROJ_SUB):
        rows = slice(s * sub, (s + 1) * sub)
        mrows = slice(s * sub // CHUNK, (s + 1) * sub // CHUNK)
        x = jnp.where(i < n_ptiles, xp_ref[rows, :], xs_ref[rows, :])
        ms = jnp.mean(x * x, axis=-1, keepdims=True)
        xn = x * lax.rsqrt(ms + EPS) * gmix_ref[...]
        sh = _rows_to_tokens(mod_ref[mrows, 0:D_MODEL], D_MODEL)
        sc = _rows_to_tokens(mod_ref[mrows, D_MODEL:2 * D_MODEL], D_MODEL)
        hb = (xn * (1.0 + sc) + sh).astype(BF16)

        zq = _dot(hb, w_ref[:, 0:A_WIDTH])
        q_ref[rows, :] = (_head_rms(zq, bd_ref, gq_ref[...]) * (LOG2E * A_HEAD_DIM ** -0.5)).astype(BF16)
        zk = _dot(hb, w_ref[:, A_WIDTH:2 * A_WIDTH])
        kn = _head_rms(zk, bd_ref, gk_ref[...])
        k_ref[rows, :] = kn.astype(BF16)
        kf_ref[rows, :] = kn
        zv = _dot(hb, w_ref[:, 2 * A_WIDTH:3 * A_WIDTH])
        v_ref[rows, :] = zv.astype(BF16)
        vf_ref[rows, :] = zv

        o = 3 * A_WIDTH
        zqb = _dot(hb, w_ref[:, o:o + B_KWIDTH]) * (B_DK ** -0.5)
        gla_ref[rows, 0:B_KWIDTH] = zqb.astype(BF16)
        for c in range(B_KWIDTH, 2 * B_KWIDTH + 2 * B_WIDTH, 256):
            gla_ref[rows, c:c + 256] = _dot(hb, w_ref[:, o + c:o + c + 256]).astype(BF16)

        gr = _dot(hb, wgr_ref[...])
        logit = _dot(gr.astype(BF16), wgu_ref[...]) + bg_ref[...]
        log_sig = jnp.minimum(logit, 0.0) - jnp.log1p(jnp.exp(-jnp.abs(logit)))
        la_ref[rows, :] = log_sig * (1.0 / GATE_TAU)


def _inproj(xp, xs, mod, gmix, w_main, w_gr, bd, gq, gk, wgu_p, bg, n_ptiles, n_stiles, prep):
    n_tiles = n_ptiles + n_stiles
    t = n_tiles * TOK_TILE
    tail_tiles = 1 + n_stiles
    pblocks = prep // ROWS_PER_TILE
    const = lambda i: (0, 0)
    row = lambda i: (i, 0)
    tail = lambda i: (jnp.maximum(i - (n_ptiles - 1), 0), 0)
    return pl.pallas_call(
        functools.partial(_inproj_kernel, n_ptiles),
        grid=(n_tiles,),
        in_specs=[pl.BlockSpec((TOK_TILE, D_MODEL), lambda i: (jnp.minimum(i, n_ptiles - 1), 0)),
                  pl.BlockSpec((TOK_TILE, D_MODEL), lambda i: (jnp.maximum(i - n_ptiles, 0), 0)),
                  pl.BlockSpec((ROWS_PER_TILE, 6 * D_MODEL),
                               lambda i: (jnp.maximum(i - n_ptiles + pblocks, 0), 0)),
                  pl.BlockSpec((1, D_MODEL), const),
                  pl.BlockSpec((D_MODEL, IN_MAIN), const),
                  pl.BlockSpec((D_MODEL, LANES), const),
                  pl.BlockSpec((A_WIDTH, A_WIDTH), const),
                  pl.BlockSpec((1, A_WIDTH), const),
                  pl.BlockSpec((1, A_WIDTH), const),
                  pl.BlockSpec((LANES, B_KWIDTH), const),
                  pl.BlockSpec((1, B_KWIDTH), const)],
        out_specs=[pl.BlockSpec((TOK_TILE, A_WIDTH), row),
                   pl.BlockSpec((TOK_TILE, A_WIDTH), row),
                   pl.BlockSpec((TOK_TILE, A_WIDTH), row),
                   pl.BlockSpec((TOK_TILE, A_WIDTH), tail),
                   pl.BlockSpec((TOK_TILE, A_WIDTH), tail),
                   pl.BlockSpec((TOK_TILE, 2 * B_KWIDTH + 2 * B_WIDTH), row),
                   pl.BlockSpec((TOK_TILE, B_KWIDTH), row)],
        out_shape=[jax.ShapeDtypeStruct((t, A_WIDTH), BF16),
                   jax.ShapeDtypeStruct((t, A_WIDTH), BF16),
                   jax.ShapeDtypeStruct((t, A_WIDTH), BF16),
                   jax.ShapeDtypeStruct((tail_tiles * TOK_TILE, A_WIDTH), F32),
                   jax.ShapeDtypeStruct((tail_tiles * TOK_TILE, A_WIDTH), F32),
                   jax.ShapeDtypeStruct((t, 2 * B_KWIDTH + 2 * B_WIDTH), BF16),
                   jax.ShapeDtypeStruct((t, B_KWIDTH), F32)],
        compiler_params=_params(),
        name="inproj",
    )(xp, xs, mod, gmix, w_main, w_gr, bd, gq, gk, wgu_p, bg)


def _bias_lanes(n_keys):
    l = np.arange(ROLL_W)
    d = np.where(l < n_keys, BAND_PAST - l, BAND_PAST - l + ROLL_W)
    return np.clip(d, -(CHUNK - 1), MAX_REL) + (CHUNK - 1)


LOG2E = 1.4426950408889634


def _band_mask(m_rows, n_keys, first_col):
    qi = lax.broadcasted_iota(jnp.int32, (m_rows, n_keys), 0) >> LOG_CHUNK
    kw = lax.broadcasted_iota(jnp.int32, (m_rows, n_keys), 1)
    kc = kw >> LOG_CHUNK
    return (kc >= qi) & (kc <= qi + BAND_CHUNKS) & (kw >= first_col)


def _bias_tile(u_ref, h, ok):
    m_rows, n_keys = ok.shape
    src = jnp.broadcast_to(u_ref[h:h + 1, :] * LOG2E, (m_rows, ROLL_W))
    toe = pltpu.roll(src, 0, 1, stride=1, stride_axis=0)
    return jnp.where(ok, toe[:, 0:n_keys], NEG)


def _attend(q, kcat, vcat, bias_sc):
    m_rows = q.shape[0]
    first = lax.broadcasted_iota(jnp.int32, (m_rows, LANES), 1) < A_HEAD_DIM
    outs = []
    for p in range(A_HEADS // 2):
        lanes = slice(p * LANES, (p + 1) * LANES)
        qp, kp, vp = q[:, lanes], kcat[:, lanes], vcat[:, lanes]
        zero = jnp.zeros_like(qp)
        q2 = jnp.concatenate([jnp.where(first, qp, zero), jnp.where(first, zero, qp)], axis=0)
        s = _dot_nt(q2, kp) + bias_sc[p]
        e = jnp.exp2(s - jnp.max(s, axis=-1, keepdims=True))
        l = jnp.sum(e, axis=-1, keepdims=True)
        o2 = _dot(e.astype(BF16), vp) / l
        outs.append(jnp.where(first, o2[0:m_rows], o2[m_rows:2 * m_rows]))
    return jnp.concatenate(outs, axis=-1)


ATTN_SUB = 4
ATTN_WIN = 3


def _attn_prompt_kernel(u_ref, q_ref, *refs):
    k_refs = refs[0:ATTN_SUB + ATTN_WIN - 1]
    v_refs = refs[ATTN_SUB + ATTN_WIN - 1:2 * (ATTN_SUB + ATTN_WIN - 1)]
    o_ref, bias_sc = refs[-2:]
    j = pl.program_id(0)
    n_keys = ATTN_WIN * Q_ROWS

    @pl.when(j == 0)
    def _():
        for g in range(ATTN_WIN):
            ok = _band_mask(Q_ROWS, n_keys, (ATTN_WIN - 1 - g) * Q_ROWS)
            for h in range(A_HEADS):
                bias_sc[g, h // 2, (h % 2) * Q_ROWS:(h % 2 + 1) * Q_ROWS, :] = _bias_tile(u_ref, h, ok)

    ks = [r[...] for r in k_refs]
    vs = [r[...] for r in v_refs]
    for sub in range(ATTN_SUB):
        rows = slice(sub * Q_ROWS, (sub + 1) * Q_ROWS)
        kcat = jnp.concatenate(ks[sub:sub + ATTN_WIN], axis=0)
        vcat = jnp.concatenate(vs[sub:sub + ATTN_WIN], axis=0)
        bias = bias_sc.at[jnp.minimum(ATTN_SUB * j + sub, ATTN_WIN - 1)]
        o_ref[rows, :] = _attend(q_ref[rows, :], kcat, vcat, bias).astype(BF16)


def _attn_prompt(u, q, k, v, n_steps):
    const = lambda j: (0, 0)
    n_blk = ATTN_SUB + ATTN_WIN - 1
    blk = lambda d: pl.BlockSpec((Q_ROWS, A_WIDTH),
                                 lambda j, d=d: (jnp.maximum(ATTN_SUB * j - (ATTN_WIN - 1) + d, 0), 0))
    step_rows = ATTN_SUB * Q_ROWS
    return pl.pallas_call(
        _attn_prompt_kernel,
        grid=(n_steps,),
        in_specs=[pl.BlockSpec((A_HEADS, ROLL_W), const), pl.BlockSpec((step_rows, A_WIDTH), lambda j: (j, 0))]
                 + [blk(d) for d in range(n_blk)] * 2,
        out_specs=pl.BlockSpec((step_rows, A_WIDTH), lambda j: (j, 0)),
        out_shape=jax.ShapeDtypeStruct((n_steps * step_rows, A_WIDTH), BF16),
        scratch_shapes=[pltpu.VMEM((ATTN_WIN, A_HEADS // 2, 2 * Q_ROWS, ATTN_WIN * Q_ROWS), F32)],
        compiler_params=_params(),
        name="attn_prompt",
    )(u, q, *([k] * n_blk), *([v] * n_blk))


SAMPLE_KEYS = BAND_PAST + 2 * CHUNK


def _attn_sample_kernel(u_ref, q_ref, kn_ref, vn_ref, kc_ref, vc_ref, o_ref, bias_sc):
    @pl.when(pl.program_id(0) == 0)
    def _():
        ok = _band_mask(CHUNK, SAMPLE_KEYS, 0)
        for p in range(A_HEADS // 2):
            pair = jnp.concatenate([_bias_tile(u_ref, 2 * p, ok), _bias_tile(u_ref, 2 * p + 1, ok)], axis=0)
            bias_sc[p] = pair.T

    pad = jnp.zeros((CHUNK, A_WIDTH), BF16)
    kcat = jnp.concatenate([kc_ref[0].astype(BF16), kn_ref[...], pad], axis=0)
    vcat = jnp.concatenate([vc_ref[0].astype(BF16), vn_ref[...], pad], axis=0)
    q = q_ref[...]
    lane = lax.broadcasted_iota(jnp.int32, (CHUNK, LANES), 1)
    first = lane < A_HEAD_DIM
    zero = jnp.zeros((CHUNK, LANES), BF16)
    outs = []
    for p in range(A_HEADS // 2):
        lanes = slice(p * LANES, (p + 1) * LANES)
        qp = q[:, lanes]
        q_rows = jnp.concatenate([jnp.where(first, qp, zero), jnp.where(first, zero, qp)], axis=0)
        s = _dot_nt(kcat[:, lanes], q_rows) + bias_sc[p]
        e = jnp.exp2(s - jnp.max(s, axis=0, keepdims=True))
        pn = (e * (1.0 / jnp.sum(e, axis=0, keepdims=True))).astype(BF16)
        r = _dot_tn(pn, vcat[:, lanes])
        outs.append(jnp.where(first, r[0:CHUNK], r[CHUNK:2 * CHUNK]))
    o_ref[...] = jnp.concatenate(outs, axis=-1).astype(BF16)


def _attn_sample(u, q, k, v, kc, vc, first_chunk, n_seq):
    new = pl.BlockSpec((CHUNK, A_WIDTH), lambda b: (first_chunk + b, 0))
    cache = pl.BlockSpec((1, BAND_PAST, A_WIDTH), lambda b: (b, 0, 0))
    return pl.pallas_call(
        _attn_sample_kernel,
        grid=(n_seq,),
        in_specs=[pl.BlockSpec((A_HEADS, ROLL_W), lambda b: (0, 0)), new, new, new, cache, cache],
        out_specs=pl.BlockSpec((CHUNK, A_WIDTH), lambda b: (b, 0)),
        out_shape=jax.ShapeDtypeStruct((n_seq * CHUNK, A_WIDTH), BF16),
        scratch_shapes=[pltpu.VMEM((A_HEADS // 2, SAMPLE_KEYS, LANES), F32)],
        compiler_params=_params(),
        name="attn_sample",
    )(u, q, k, v, kc, vc)


GLA_CHUNKS = 4
GLA_SUB = 4


def _gla_block(n_chunks, gla_ref, la_ref, ltri_ref, g_ref, st_sc, o_ref):
    rows = n_chunks * CHUNK
    la = la_ref[...]
    la_hi, la_lo = _split(la)
    b = _dot(ltri_ref[...], la_hi) + _dot(ltri_ref[...], la_lo)
    b3 = b.reshape(n_chunks, CHUNK, B_KWIDTH)
    b_mid = b3[:, CHUNK // 2 - 1:CHUNK // 2, :]
    b_last = b3[:, CHUNK - 1:CHUNK, :]
    q = gla_ref[:, 0:B_KWIDTH].astype(F32).reshape(n_chunks, CHUNK, B_KWIDTH)
    k = gla_ref[:, B_KWIDTH:2 * B_KWIDTH].astype(F32).reshape(n_chunks, CHUNK, B_KWIDTH)
    q_start = (q * jnp.exp(b3)).reshape(rows, B_KWIDTH).astype(BF16)
    q_mid = (q * jnp.exp(b3 - b_mid)).reshape(rows, B_KWIDTH).astype(BF16)
    k_mid = (k * jnp.exp(b_mid - b3)).reshape(rows, B_KWIDTH).astype(BF16)
    k_end = (k * jnp.exp(b_last - b3)).reshape(rows, B_KWIDTH).astype(BF16)
    dec = jnp.exp(b_last)

    ti = lax.broadcasted_iota(jnp.int32, (2 * rows, rows), 0) & (rows - 1)
    si = lax.broadcasted_iota(jnp.int32, (2 * rows, rows), 1)
    causal = (si <= ti) & ((si >> LOG_CHUNK) == (ti >> LOG_CHUNK))
    first_r = lax.broadcasted_iota(jnp.int32, (rows, LANES), 1) < B_DK
    first_c = lax.broadcasted_iota(jnp.int32, (CHUNK, LANES), 1) < B_DK
    first_s = lax.broadcasted_iota(jnp.int32, (B_DV, LANES), 1) < B_DK

    def stack_heads(x, first):
        zero = jnp.zeros_like(x)
        return jnp.concatenate([jnp.where(first, x, zero), jnp.where(first, zero, x)], axis=0)

    for p in range(B_HEADS // 2):
        lanes = slice(p * LANES, (p + 1) * LANES)
        qs_p, qm_p, km_p, ke_p = q_start[:, lanes], q_mid[:, lanes], k_mid[:, lanes], k_end[:, lanes]
        v_pair = gla_ref[:, 2 * B_KWIDTH + 2 * p * B_DV:2 * B_KWIDTH + (2 * p + 2) * B_DV]
        sc = jnp.where(causal, _dot_nt(stack_heads(qm_p, first_r), km_p), 0.0)
        o2 = _dot(sc.astype(BF16), v_pair)
        intra = [o2[0:rows, 0:B_DV], o2[rows:2 * rows, B_DV:2 * B_DV]]
        inter = [[], []]
        st = st_sc[p]
        for c in range(n_chunks):
            cr = slice(c * CHUNK, (c + 1) * CHUNK)
            r2 = _dot_nt(stack_heads(qs_p[cr], first_c), st.astype(BF16))
            inter[0].append(r2[0:CHUNK])
            inter[1].append(r2[CHUNK:2 * CHUNK])
            u2 = _dot_tn(v_pair[cr], ke_p[cr])
            st = st * dec[c, :, lanes] + jnp.where(first_s, u2[0:B_DV], u2[B_DV:2 * B_DV])
        st_sc[p] = st
        for hh in range(2):
            h = 2 * p + hh
            o = intra[hh] + jnp.concatenate(inter[hh], axis=0)
            ms = jnp.mean(o * o, axis=-1, keepdims=True)
            on = o * lax.rsqrt(ms + EPS) * g_ref[...]
            r = gla_ref[:, 2 * B_KWIDTH + B_WIDTH + h * B_DV:2 * B_KWIDTH + B_WIDTH + (h + 1) * B_DV]
            o_ref[:, h * B_DV:(h + 1) * B_DV] = (on * _silu(r.astype(F32))).astype(BF16)


def _gla_prompt_kernel(gla_ref, la_ref, ltri_ref, g_ref, o_ref, sfin_ref, st_sc):
    @pl.when(pl.program_id(0) == 0)
    def _():
        st_sc[...] = jnp.zeros_like(st_sc)

    rows = GLA_CHUNKS * CHUNK
    for sub in range(GLA_SUB):
        part = pl.ds(sub * rows, rows)
        _gla_block(GLA_CHUNKS, gla_ref.at[part], la_ref.at[part], ltri_ref, g_ref, st_sc, o_ref.at[part])
    sfin_ref[...] = st_sc[...]


def _gla_sample_kernel(gla_ref, la_ref, ltri_ref, g_ref, s0_ref, o_ref, sfin_ref, st_sc):
    st_sc[...] = s0_ref[0]
    _gla_block(1, gla_ref, la_ref, ltri_ref, g_ref, st_sc, o_ref)
    sfin_ref[0] = st_sc[...]


def _ltri(n_chunks):
    r = np.arange(n_chunks * CHUNK)
    m = (r[None, :] <= r[:, None]) & (r[None, :] // CHUNK == r[:, None] // CHUNK)
    return jnp.asarray(m, BF16)


_GLA_W = 2 * B_KWIDTH + 2 * B_WIDTH
_ST_SHAPE = (B_HEADS // 2, B_DV, LANES)


def _gla_prompt(gla, la, g, n_steps):
    rows = GLA_SUB * GLA_CHUNKS * CHUNK
    const = lambda j: (0, 0)
    return pl.pallas_call(
        _gla_prompt_kernel,
        grid=(n_steps,),
        in_specs=[pl.BlockSpec((rows, _GLA_W), lambda j: (j, 0)),
                  pl.BlockSpec((rows, B_KWIDTH), lambda j: (j, 0)),
                  pl.BlockSpec((GLA_CHUNKS * CHUNK, GLA_CHUNKS * CHUNK), const),
                  pl.BlockSpec((1, B_DV), const)],
        out_specs=[pl.BlockSpec((rows, B_WIDTH), lambda j: (j, 0)),
                   pl.BlockSpec(_ST_SHAPE, lambda j: (0, 0, 0))],
        out_shape=[jax.ShapeDtypeStruct((n_steps * rows, B_WIDTH), BF16),
                   jax.ShapeDtypeStruct(_ST_SHAPE, F32)],
        scratch_shapes=[pltpu.VMEM(_ST_SHAPE, F32)],
        compiler_params=_params(),
        name="gla_prompt",
    )(gla, la, _ltri(GLA_CHUNKS), g)


def _gla_sample(gla, la, g, s0, first_chunk, n_seq):
    const = lambda b: (0, 0)
    st_spec = pl.BlockSpec((1,) + _ST_SHAPE, lambda b: (b, 0, 0, 0))
    return pl.pallas_call(
        _gla_sample_kernel,
        grid=(n_seq,),
        in_specs=[pl.BlockSpec((CHUNK, _GLA_W), lambda b: (first_chunk + b, 0)),
                  pl.BlockSpec((CHUNK, B_KWIDTH), lambda b: (first_chunk + b, 0)),
                  pl.BlockSpec((CHUNK, CHUNK), const),
                  pl.BlockSpec((1, B_DV), const),
                  st_spec],
        out_specs=[pl.BlockSpec((CHUNK, B_WIDTH), lambda b: (b, 0)), st_spec],
        out_shape=[jax.ShapeDtypeStruct((n_seq * CHUNK, B_WIDTH), BF16),
                   jax.ShapeDtypeStruct((n_seq,) + _ST_SHAPE, F32)],
        scratch_shapes=[pltpu.VMEM(_ST_SHAPE, F32)],
        compiler_params=_params(),
        name="gla_sample",
    )(gla, la, _ltri(1), g, s0)


def _state_to_pairs(s):
    lead = s.shape[:-3]
    s = s.reshape(lead + (B_HEADS // 2, 2, B_DK, B_DV))
    s = jnp.moveaxis(s, -1, -3)
    return s.reshape(lead + (B_HEADS // 2, B_DV, 2 * B_DK))


def _pairs_to_state(s):
    lead = s.shape[:-3]
    s = s.reshape(lead + (B_HEADS // 2, B_DV, 2, B_DK))
    s = jnp.moveaxis(s, -3, -1)
    return s.reshape(lead + (B_HEADS, B_DK, B_DV))


def _route(logits):
    lane = lax.broadcasted_iota(jnp.int32, logits.shape, 1)
    lane_f = lane.astype(F32)
    big = float(LANES)
    gmask = lane < N_GROUPS
    gl = jnp.where(gmask, logits, NEG)
    gmax = jnp.max(gl, axis=-1, keepdims=True)
    gsel = jnp.min(jnp.where(gl == gmax, lane_f, big), axis=-1, keepdims=True)
    gsum = jnp.sum(jnp.where(gmask, jnp.exp(gl - gmax), 0.0), axis=-1, keepdims=True)
    g_w = 1.0 / gsum
    e_lo = ROUTE_OFF + gsel * EXPERTS_PER_GROUP
    emask = (lane_f >= e_lo) & (lane_f < e_lo + EXPERTS_PER_GROUP)
    el = jnp.where(emask, logits, NEG)
    v1 = jnp.max(el, axis=-1, keepdims=True)
    i1 = jnp.min(jnp.where(el == v1, lane_f, big), axis=-1, keepdims=True)
    el2 = jnp.where(lane_f == i1, NEG, el)
    v2 = jnp.max(el2, axis=-1, keepdims=True)
    i2 = jnp.min(jnp.where(el2 == v2, lane_f, big), axis=-1, keepdims=True)
    t = jnp.exp(v2 - v1)
    w1 = g_w / (1.0 + t)
    w2 = g_w * t / (1.0 + t)
    return lane_f, i1, i2, w1, w2


ROW_PIECES = D_MODEL // 2 // LANES
SUBLANES = 8
ROW_TILE = ROW_PIECES * SUBLANES


def _pack_rows(z32_sc, x, rows):
    half = D_MODEL // 2
    out = []
    for c in range(ROW_PIECES):
        z32_sc[c, pl.ds(0, rows, stride=2), :] = x[:, c * LANES:(c + 1) * LANES]
        z32_sc[c, pl.ds(1, rows, stride=2), :] = x[:, half + c * LANES:half + (c + 1) * LANES]
        out.append(z32_sc[c].astype(BF16))
    return out


def _unpack_rows(z32_sc, pieces, rows):
    lo, hi = [], []
    for c in range(ROW_PIECES):
        z32_sc[c] = pieces[c].astype(F32)
        lo.append(z32_sc[c, pl.ds(0, rows, stride=2), :])
        hi.append(z32_sc[c, pl.ds(1, rows, stride=2), :])
    return jnp.concatenate(lo, axis=1), jnp.concatenate(hi, axis=1)


def _to_row_tiled(pieces, tokens):
    per_tile = pieces[0].shape[0] * SUBLANES // tokens
    return jnp.stack([p.reshape(tokens // SUBLANES, per_tile, LANES) for p in pieces], axis=1)


def _from_row_tiled(flat, tokens):
    per_tile = flat.shape[0] // (tokens // SUBLANES) // ROW_PIECES
    tiled = flat.reshape(tokens // SUBLANES, ROW_PIECES, per_tile, LANES)
    return [tiled[:, c].reshape(tokens // SUBLANES * per_tile, LANES) for c in range(ROW_PIECES)]


def _flatten_tiled(tiled):
    return tiled.reshape(-1, LANES)


def _outproj_kernel(n_ptiles, tiles_per_sb, oap_ref, oas_ref, obp_ref, obs_ref, wo_ref, xp_ref, xs_ref, mod_ref,
                    gffn_ref, wr_ref, br_ref, ltri_ref, x1_ref, h2p_ref, meta_ref, cnt_ref, z32_sc, cnt_sc):
    i = pl.program_id(0)
    is_prompt = i < n_ptiles
    x = jnp.where(is_prompt, xp_ref[...], xs_ref[...])
    oa = jnp.where(is_prompt, oap_ref[...], oas_ref[...])
    ob = jnp.where(is_prompt, obp_ref[...], obs_ref[...])
    mix = _dot(oa, wo_ref[0:A_WIDTH, :]) + _dot(ob, wo_ref[A_WIDTH:D_MODEL, :])
    gate1 = _rows_to_tokens(mod_ref[:, 2 * D_MODEL:3 * D_MODEL], D_MODEL)
    x1 = x + gate1 * mix
    x1_ref[...] = x1
    ms = jnp.mean(x1 * x1, axis=-1, keepdims=True)
    xn = x1 * lax.rsqrt(ms + EPS) * gffn_ref[...]
    sh = _rows_to_tokens(mod_ref[:, 3 * D_MODEL:4 * D_MODEL], D_MODEL)
    sc = _rows_to_tokens(mod_ref[:, 4 * D_MODEL:5 * D_MODEL], D_MODEL)
    h2 = xn * (1.0 + sc) + sh
    words = [pltpu.bitcast(p, U32) for p in _pack_rows(z32_sc, h2, TOK_TILE)]
    h2p_ref[...] = _to_row_tiled(words, TOK_TILE)

    lane_f, i1, i2, w1, w2 = _route(_dot3(h2, wr_ref[...]) + br_ref[...])

    @pl.when(lax.rem(i, tiles_per_sb) == 0)
    def _():
        cnt_sc[...] = jnp.zeros_like(cnt_sc)

    sel = jnp.where((lane_f == i1) | (lane_f == i2), 1.0, 0.0).astype(BF16)
    before = _dot(ltri_ref[...], sel) + cnt_sc[0:1, :]
    rank1 = jnp.sum(jnp.where(lane_f == i1, before, 0.0), axis=-1, keepdims=True)
    rank2 = jnp.sum(jnp.where(lane_f == i2, before, 0.0), axis=-1, keepdims=True)
    cnt = cnt_sc[...] + _dot(jnp.ones((8, TOK_TILE), BF16), sel)
    cnt_sc[...] = cnt
    cnt_ref[0] = cnt
    cols = (i1, i2, rank1, rank2, w1, w2)
    meta = jnp.zeros_like(lane_f)
    for c, col in enumerate(cols):
        meta = jnp.where(lane_f == float(c), col, meta)
    meta_ref[...] = meta


def _outproj(oa_p, oa_s, ob_p, ob_s, w_out, xp, xs, mod, gffn, wr, br, n_ptiles, n_stiles, prep, sb):
    n_tiles = n_ptiles + n_stiles
    t = n_tiles * TOK_TILE
    pblocks = prep // ROWS_PER_TILE
    tiles_per_sb = sb // TOK_TILE
    const = lambda i: (0, 0)
    row = lambda i: (i, 0)
    prow = lambda i: (jnp.minimum(i, n_ptiles - 1), 0)
    srow = lambda i: (jnp.maximum(i - n_ptiles, 0), 0)
    r = np.arange(TOK_TILE)
    ltri = jnp.asarray(r[None, :] < r[:, None], BF16)
    return pl.pallas_call(
        functools.partial(_outproj_kernel, n_ptiles, tiles_per_sb),
        grid=(n_tiles,),
        in_specs=[pl.BlockSpec((TOK_TILE, A_WIDTH), prow),
                  pl.BlockSpec((TOK_TILE, A_WIDTH), srow),
                  pl.BlockSpec((TOK_TILE, B_WIDTH), prow),
                  pl.BlockSpec((TOK_TILE, B_WIDTH), srow),
                  pl.BlockSpec((D_MODEL, D_MODEL), const),
                  pl.BlockSpec((TOK_TILE, D_MODEL), prow),
                  pl.BlockSpec((TOK_TILE, D_MODEL), srow),
                  pl.BlockSpec((ROWS_PER_TILE, 6 * D_MODEL),
                               lambda i: (jnp.maximum(i - n_ptiles + pblocks, 0), 0)),
                  pl.BlockSpec((1, D_MODEL), const),
                  pl.BlockSpec((D_MODEL, LANES), const),
                  pl.BlockSpec((1, LANES), const),
                  pl.BlockSpec((TOK_TILE, TOK_TILE), const)],
        out_specs=[pl.BlockSpec((TOK_TILE, D_MODEL), row),
                   pl.BlockSpec((TOK_TILE // SUBLANES, ROW_PIECES, SUBLANES, LANES), lambda i: (i, 0, 0, 0)),
                   pl.BlockSpec((TOK_TILE, LANES), row),
                   pl.BlockSpec((1, 8, LANES), lambda i: (i // tiles_per_sb, 0, 0))],
        out_shape=[jax.ShapeDtypeStruct((t, D_MODEL), F32),
                   jax.ShapeDtypeStruct((t // SUBLANES, ROW_PIECES, SUBLANES, LANES), U32),
                   jax.ShapeDtypeStruct((t, LANES), F32),
                   jax.ShapeDtypeStruct((t // sb, 8, LANES), F32)],
        scratch_shapes=[pltpu.VMEM((D_MODEL // 2 // LANES, 2 * TOK_TILE, LANES), F32),
                        pltpu.VMEM((8, LANES), F32)],
        compiler_params=_params(),
        name="outproj",
    )(oa_p, oa_s, ob_p, ob_s, w_out, xp, xs, mod, gffn, wr, br, ltri)


MOE_SUPER_BLOCK = 2048
SEG_ALIGN = SUBLANES
CHUNK_BF16_ROWS = 2 * SEG_ALIGN * ROW_PIECES
SEG_BITS = 9
PAD_BITS = 5
FFN_ROWS = 512
PLAN_ROWS = LANES


def _local_rows(sb):
    return 2 * sb + N_EXPERTS * SEG_ALIGN


def _sorted_tiles(n_tokens, sb):
    rows = 2 * n_tokens + (n_tokens // sb) * N_EXPERTS * SEG_ALIGN + N_EXPERTS * FFN_ROWS
    return -(-rows // FFN_ROWS)


def _moe_plan_kernel(n_blocks, total_chunks, meta_ref, cnt_ref, ustrict_ref, lstrict_ref,
                     posw_ref, addr_ref, tab_ref, tile_ref):
    b = pl.program_id(0)
    per_tile = FFN_ROWS // SEG_ALIGN

    @pl.when(b == 0)
    def _():
        cnt = cnt_ref[...]
        chunks = jnp.floor((cnt + (SEG_ALIGN - 1)) * (1.0 / SEG_ALIGN))
        chunks_b = chunks.astype(BF16)
        loc = _dot(chunks_b, ustrict_ref[...])
        before = _dot(lstrict_ref[...], chunks_b)
        tot = _dot(jnp.ones((PLAN_ROWS, PLAN_ROWS), BF16), chunks_b)
        tiles = jnp.floor((tot + (per_tile - 1)) * (1.0 / per_tile))
        tile_off = _dot(tiles.astype(BF16), ustrict_ref[...])
        n_tiles = jnp.sum(tiles[0:1], axis=-1, keepdims=True)
        lane1 = lax.broadcasted_iota(jnp.int32, (PLAN_ROWS, LANES), 1)
        tail = lane1 == ROUTE_OFF + N_EXPERTS
        pad_off = jnp.where(tail, n_tiles * per_tile, tile_off * per_tile + tot)
        pad_n = jnp.where(tail, total_chunks - n_tiles * per_tile, tiles * per_tile - tot)
        row = lax.broadcasted_iota(jnp.int32, (PLAN_ROWS, LANES), 0)
        tab_ref[0] = loc
        tab_ref[1] = chunks
        tab_ref[2] = tile_off * per_tile + before
        tab_ref[3] = jnp.where(row == 0, pad_off, jnp.where(row == 1, pad_n, jnp.where(row == 2, n_tiles, 0.0)))
        t_idx = lax.broadcasted_iota(jnp.int32, tile_ref.shape, 0).astype(F32)
        lane_t = lax.broadcasted_iota(jnp.int32, tile_ref.shape, 1)
        is_expert = (lane_t >= ROUTE_OFF) & (lane_t < ROUTE_OFF + N_EXPERTS)
        ends = (tile_off + tiles)[0:1, :]
        owner = jnp.sum(jnp.where(is_expert & (ends <= t_idx), 1.0, 0.0), axis=-1, keepdims=True)
        tile_ref[...] = jnp.broadcast_to(jnp.minimum(owner, N_EXPERTS - 1.0), tile_ref.shape)

    own = jnp.floor((cnt_ref[pl.ds(b, 1), :] + (SEG_ALIGN - 1)) * (1.0 / SEG_ALIGN))
    own_off = _dot(jnp.broadcast_to(own, (SUBLANES, LANES)).astype(BF16), ustrict_ref[...]) * SEG_ALIGN
    meta = meta_ref[...]
    lane_f = lax.broadcasted_iota(jnp.int32, meta.shape, 1).astype(F32)
    off_row = own_off[0:1, :]
    pos = []
    for k in range(2):
        e_lane = meta[:, k:k + 1]
        base = jnp.sum(jnp.where(lane_f == e_lane, off_row, 0.0), axis=-1, keepdims=True)
        p = base + meta[:, 2 + k:3 + k]
        tile = jnp.floor(p * (1.0 / SUBLANES))
        pos.append(tile * (ROW_TILE - SUBLANES) + p)
    out = jnp.zeros_like(meta)
    for c, col in enumerate((pos[0], pos[1], meta[:, 4:5], meta[:, 5:6])):
        out = jnp.where(lane_f == float(c), col, out)
    posw_ref[...] = out
    addr_ref[0] = out.T[0:SUBLANES]


def _moe_plan(meta, cnt, sb):
    n_blocks = meta.shape[0] // sb
    assert n_blocks <= PLAN_ROWS and sb // SEG_ALIGN <= 256
    n_tiles = _sorted_tiles(meta.shape[0], sb)
    tile_rows = -(-n_tiles // SUBLANES) * SUBLANES
    r = np.arange(LANES)
    ustrict = jnp.asarray(r[:, None] < r[None, :], BF16)
    lstrict = jnp.asarray(r[None, :] < r[:, None], BF16)
    cnt_all = jnp.pad(cnt[:, 0, :], ((0, PLAN_ROWS - n_blocks), (0, 0)))
    const = lambda s: (0, 0)
    posw, addr, tab, tile_owner = pl.pallas_call(
        functools.partial(_moe_plan_kernel, n_blocks, float(n_tiles * (FFN_ROWS // SEG_ALIGN))),
        grid=(n_blocks,),
        in_specs=[pl.BlockSpec((sb, LANES), lambda s: (s, 0)),
                  pl.BlockSpec((PLAN_ROWS, LANES), const),
                  pl.BlockSpec((LANES, LANES), const),
                  pl.BlockSpec((PLAN_ROWS, PLAN_ROWS), const)],
        out_specs=[pl.BlockSpec((sb, LANES), lambda s: (s, 0)),
                   pl.BlockSpec((1, SUBLANES, sb), lambda s: (s, 0, 0)),
                   pl.BlockSpec((4, PLAN_ROWS, LANES), lambda s: (0, 0, 0)),
                   pl.BlockSpec((tile_rows, LANES), const)],
        out_shape=[jax.ShapeDtypeStruct(meta.shape, F32),
                   jax.ShapeDtypeStruct((n_blocks, SUBLANES, sb), F32),
                   jax.ShapeDtypeStruct((4, PLAN_ROWS, LANES), F32),
                   jax.ShapeDtypeStruct((tile_rows, LANES), F32)],
        compiler_params=_params(),
        name="moe_plan",
    )(meta, cnt_all, ustrict, lstrict)
    experts = slice(ROUTE_OFF, ROUTE_OFF + N_EXPERTS)
    to_i32 = lambda x: x.astype(jnp.int32).reshape(-1)
    plan = dict(
        loc=to_i32(tab[0, :n_blocks, experts]), n=to_i32(tab[1, :n_blocks, experts]),
        dst=to_i32(tab[2, :n_blocks, experts]),
        pad_off=to_i32(tab[3, 0, ROUTE_OFF:ROUTE_OFF + N_EXPERTS + 1]),
        pad_n=to_i32(tab[3, 1, ROUTE_OFF:ROUTE_OFF + N_EXPERTS + 1]),
        n_tiles=to_i32(tab[3, 2, 0:1]),
        owner=to_i32(tile_owner[:n_tiles, 0]))
    return posw, to_i32(addr[:, 0, :]), to_i32(addr[:, 1, :]), plan


def _token_rows(start):
    return pl.ds(start, ROW_PIECES, stride=SUBLANES)


def _pow2_copies(src_ref, dst_ref, src_chunk, dst_chunk, n, n_bits, sem, act):
    done = 0
    for k in reversed(range(n_bits)):
        take = (n >> k) & 1
        rows = CHUNK_BF16_ROWS << k
        src0 = 0 if src_chunk is None else pl.multiple_of((src_chunk + done) * CHUNK_BF16_ROWS, CHUNK_BF16_ROWS)
        dst0 = pl.multiple_of((dst_chunk + done) * CHUNK_BF16_ROWS, CHUNK_BF16_ROWS)

        @pl.when(take == 1)
        def _(src0=src0, dst0=dst0, rows=rows):
            act(pltpu.make_async_copy(src_ref.at[pl.ds(src0, rows)], dst_ref.at[pl.ds(dst0, rows)], sem))

        done = done + take * (1 << k)


def _segment_copies(block, loc_ref, n_ref, dst_ref, local_ref, global_ref, to_global, sem, act):
    def per_expert(e, carry):
        seg = block * N_EXPERTS + e
        if to_global:
            _pow2_copies(local_ref, global_ref, loc_ref[seg], dst_ref[seg], n_ref[seg], SEG_BITS, sem, act)
        else:
            _pow2_copies(global_ref, local_ref, dst_ref[seg], loc_ref[seg], n_ref[seg], SEG_BITS, sem, act)
        return carry

    lax.fori_loop(0, N_EXPERTS, per_expert, 0)


def _zero_fill(zero_ref, global_ref, padoff_ref, padn_ref, sem, act):
    full = 1 << PAD_BITS

    def per_pad(e, carry):
        def per_full(c, inner):
            dst0 = pl.multiple_of((padoff_ref[e] + c * full) * CHUNK_BF16_ROWS, CHUNK_BF16_ROWS)
            act(pltpu.make_async_copy(zero_ref, global_ref.at[pl.ds(dst0, full * CHUNK_BF16_ROWS)], sem))
            return inner

        n_full = padn_ref[e] >> PAD_BITS
        lax.fori_loop(0, n_full, per_full, 0)
        _pow2_copies(zero_ref, global_ref, None, padoff_ref[e] + n_full * full, padn_ref[e] & (full - 1),
                     PAD_BITS, sem, act)
        return carry

    lax.fori_loop(0, N_EXPERTS + 1, per_pad, 0)


STAGE_SLAB = 1024


def _restage(src_sc, dst_sc, dst_dtype):
    ratio = dst_sc.shape[0] / src_sc.shape[0]
    n_slabs = src_sc.shape[0] // (STAGE_SLAB if ratio > 1 else 2 * STAGE_SLAB)
    src_rows = src_sc.shape[0] // n_slabs
    dst_rows = dst_sc.shape[0] // n_slabs

    def slab(i, carry):
        s0 = pl.multiple_of(i * src_rows, src_rows)
        d0 = pl.multiple_of(i * dst_rows, dst_rows)
        dst_sc[pl.ds(d0, dst_rows), :] = pltpu.bitcast(src_sc[pl.ds(s0, src_rows), :], dst_dtype)
        return carry

    lax.fori_loop(0, n_slabs, slab, 0)


def _moe_dispatch_kernel(sb, n_blocks, loc_ref, n_ref, dst_ref, padoff_ref, padn_ref,
                         h2p_ref, a1_ref, a2_ref, xs_hbm, local_sc, stage_sc, zero_sc, sems, zero_sem):
    b = pl.program_id(0)
    slot = b & 1

    def segments(block, buf, act):
        _segment_copies(block, loc_ref, n_ref, dst_ref, stage_sc.at[buf], xs_hbm, True, sems.at[buf], act)

    local_sc[...] = jnp.zeros_like(local_sc)

    def step(g, carry):
        src = pl.multiple_of(g * ROW_TILE, ROW_TILE)
        for u in range(SUBLANES):
            t = g * SUBLANES + u
            row = h2p_ref[_token_rows(src + u), :]
            local_sc[_token_rows(a1_ref[t]), :] = row
            local_sc[_token_rows(a2_ref[t]), :] = row
        return carry

    lax.fori_loop(0, sb // SUBLANES, step, 0)

    @pl.when(b > 0)
    def _():
        segments(b - 1, 1 - slot, lambda c: c.wait())

    _restage(local_sc, stage_sc.at[slot], BF16)
    segments(b, slot, lambda c: c.start())

    @pl.when(b == 0)
    def _():
        zero_sc[...] = jnp.zeros_like(zero_sc)
        _zero_fill(zero_sc, xs_hbm, padoff_ref, padn_ref, zero_sem, lambda c: c.start())
        _zero_fill(zero_sc, xs_hbm, padoff_ref, padn_ref, zero_sem, lambda c: c.wait())

    @pl.when(b == n_blocks - 1)
    def _():
        segments(b, slot, lambda c: c.wait())


def _smem_vec(n, index_map):
    return pl.BlockSpec((n,), index_map, memory_space=pltpu.SMEM)


def _moe_dispatch(h2p, a1, a2, plan, sb, n_tiles):
    n_blocks = h2p.shape[0] // (sb * ROW_PIECES)
    local_flat = _local_rows(sb) * ROW_PIECES
    vec = _smem_vec(sb, lambda s, *_: (s,))
    return pl.pallas_call(
        functools.partial(_moe_dispatch_kernel, sb, n_blocks),
        grid_spec=pltpu.PrefetchScalarGridSpec(
            num_scalar_prefetch=5,
            grid=(n_blocks,),
            in_specs=[pl.BlockSpec((sb * ROW_PIECES, LANES), lambda s, *_: (s, 0)), vec, vec],
            out_specs=pl.BlockSpec(memory_space=pl.ANY),
            scratch_shapes=[pltpu.VMEM((local_flat, LANES), U32),
                            pltpu.VMEM((2, 2 * local_flat, LANES), BF16),
                            pltpu.VMEM(((1 << PAD_BITS) * CHUNK_BF16_ROWS, LANES), BF16),
                            pltpu.SemaphoreType.DMA((2,)),
                            pltpu.SemaphoreType.DMA(())]),
        out_shape=jax.ShapeDtypeStruct((n_tiles * FFN_ROWS * ROW_PIECES * 2, LANES), BF16),
        compiler_params=_params(),
        name="moe_dispatch",
    )(plan["loc"], plan["n"], plan["dst"], plan["pad_off"], plan["pad_n"], h2p, a1, a2)


def _moe_ffn_kernel(owner_ref, ntiles_ref, xs_ref, wg_ref, wu_ref, wd_ref, ys_ref,
                    wg_sc, wu_sc, wd_sc, z32_sc):
    i = pl.program_id(0)
    half = D_MODEL // 2
    used = i < ntiles_ref[0]

    @pl.when(used & ((i == 0) | (owner_ref[i] != owner_ref[jnp.maximum(i - 1, 0)])))
    def _():
        wg_sc[...] = wg_ref[0].astype(BF16)
        wu_sc[...] = wu_ref[0].astype(BF16)
        wd_sc[...] = wd_ref[0].astype(BF16)

    @pl.when(used)
    def _():
        lo, hi = _unpack_rows(z32_sc, _from_row_tiled(xs_ref[...], FFN_ROWS), FFN_ROWS)
        lo, hi = lo.astype(BF16), hi.astype(BF16)
        g = _dot(lo, wg_sc[0:half, :]) + _dot(hi, wg_sc[half:D_MODEL, :])
        u = _dot(lo, wu_sc[0:half, :]) + _dot(hi, wu_sc[half:D_MODEL, :])
        y = _dot((_silu(g) * u).astype(BF16), wd_sc[...])
        ys_ref[...] = _flatten_tiled(_to_row_tiled(_pack_rows(z32_sc, y, FFN_ROWS), FFN_ROWS))

    @pl.when(jnp.logical_not(used))
    def _():
        ys_ref[...] = jnp.zeros_like(ys_ref)


def _moe_ffn(xs, plan, wg, wu, wd):
    flat = FFN_ROWS * ROW_PIECES * 2
    n_tiles = xs.shape[0] // flat
    last_used = lambda i, owner, nt: jnp.minimum(i, nt[0] - 1)
    wspec = lambda shape: pl.BlockSpec((1,) + shape, lambda i, owner, nt: (owner[last_used(i, owner, nt)], 0, 0))
    return pl.pallas_call(
        _moe_ffn_kernel,
        grid_spec=pltpu.PrefetchScalarGridSpec(
            num_scalar_prefetch=2,
            grid=(n_tiles,),
            in_specs=[pl.BlockSpec((flat, LANES), lambda i, owner, nt: (last_used(i, owner, nt), 0)),
                      wspec((D_MODEL, EXPERT_FF)), wspec((D_MODEL, EXPERT_FF)), wspec((EXPERT_FF, D_MODEL))],
            out_specs=pl.BlockSpec((flat, LANES), lambda i, owner, nt: (i, 0)),
            scratch_shapes=[pltpu.VMEM((D_MODEL, EXPERT_FF), BF16),
                            pltpu.VMEM((D_MODEL, EXPERT_FF), BF16),
                            pltpu.VMEM((EXPERT_FF, D_MODEL), BF16),
                            pltpu.VMEM((ROW_PIECES, 2 * FFN_ROWS, LANES), F32)]),
        out_shape=jax.ShapeDtypeStruct(xs.shape, BF16),
        compiler_params=_params(),
        name="moe_ffn",
    )(plan["owner"], plan["n_tiles"], xs, wg, wu, wd)


def _moe_combine_kernel(n_psb, n_blocks, loc_ref, n_ref, dst_ref,
                        ys_hbm, a1_ref, a2_ref, posw_ref, x1_ref, mod_ref, yp_ref, yo_ref,
                        local_sc, stage_sc, g1_sc, g2_sc, z32_sc, sems):
    s = pl.program_id(0)
    slot = s & 1

    def segments(block, buf, act):
        _segment_copies(block, loc_ref, n_ref, dst_ref, stage_sc.at[buf], ys_hbm, False, sems.at[buf], act)

    @pl.when(pl.program_id(1) == 0)
    def _():
        @pl.when(s == 0)
        def _():
            segments(s, slot, lambda c: c.start())

        segments(s, slot, lambda c: c.wait())
        _restage(stage_sc.at[slot], local_sc, U32)

        @pl.when(s + 1 < n_blocks)
        def _():
            segments(s + 1, 1 - slot, lambda c: c.start())

    def step(g, carry):
        dst = pl.multiple_of(g * ROW_TILE, ROW_TILE)
        for u in range(SUBLANES):
            t = g * SUBLANES + u
            g1_sc[_token_rows(dst + u), :] = local_sc[_token_rows(a1_ref[t]), :]
            g2_sc[_token_rows(dst + u), :] = local_sc[_token_rows(a2_ref[t]), :]
        return carry

    lax.fori_loop(0, TOK_TILE // SUBLANES, step, 0)
    halves = lambda g_sc: [pltpu.bitcast(p, BF16) for p in _from_row_tiled(g_sc[...], TOK_TILE)]
    lo1, hi1 = _unpack_rows(z32_sc, halves(g1_sc), TOK_TILE)
    lo2, hi2 = _unpack_rows(z32_sc, halves(g2_sc), TOK_TILE)
    w1, w2 = posw_ref[:, 2:3], posw_ref[:, 3:4]
    moe = jnp.concatenate([w1 * lo1 + w2 * lo2, w1 * hi1 + w2 * hi2], axis=1)
    gate2 = _rows_to_tokens(mod_ref[:, 5 * D_MODEL:6 * D_MODEL], D_MODEL)
    y = x1_ref[...] + gate2 * moe

    @pl.when(s < n_psb)
    def _():
        yp_ref[...] = y

    @pl.when(s >= n_psb)
    def _():
        yo_ref[...] = y


def _moe_combine(ys, a1, a2, posw, plan, x1, mod, sb, n_ptiles, n_stiles, prep):
    tps = sb // TOK_TILE
    n_blocks = (n_ptiles + n_stiles) // tps
    n_psb = n_ptiles // tps
    pblocks = prep // ROWS_PER_TILE
    tile = lambda s, j: s * tps + j
    vec = _smem_vec(TOK_TILE, lambda s, j, *_: (tile(s, j),))
    return pl.pallas_call(
        functools.partial(_moe_combine_kernel, n_psb, n_blocks),
        grid_spec=pltpu.PrefetchScalarGridSpec(
            num_scalar_prefetch=3,
            grid=(n_blocks, tps),
            in_specs=[pl.BlockSpec(memory_space=pl.ANY), vec, vec,
                      pl.BlockSpec((TOK_TILE, LANES), lambda s, j, *_: (tile(s, j), 0)),
                      pl.BlockSpec((TOK_TILE, D_MODEL), lambda s, j, *_: (tile(s, j), 0)),
                      pl.BlockSpec((ROWS_PER_TILE, 6 * D_MODEL),
                                   lambda s, j, *_: (jnp.maximum(tile(s, j) - n_ptiles + pblocks, 0), 0))],
            out_specs=[pl.BlockSpec((TOK_TILE, D_MODEL),
                                    lambda s, j, *_: (jnp.minimum(tile(s, j), n_ptiles - 1), 0)),
                       pl.BlockSpec((TOK_TILE, D_MODEL),
                                    lambda s, j, *_: (jnp.maximum(tile(s, j) - n_ptiles, 0), 0))],
            scratch_shapes=[pltpu.VMEM((_local_rows(sb) * ROW_PIECES, LANES), U32),
                            pltpu.VMEM((2, _local_rows(sb) * ROW_PIECES * 2, LANES), BF16),
                            pltpu.VMEM((TOK_TILE * ROW_PIECES, LANES), U32),
                            pltpu.VMEM((TOK_TILE * ROW_PIECES, LANES), U32),
                            pltpu.VMEM((ROW_PIECES, 2 * TOK_TILE, LANES), F32),
                            pltpu.SemaphoreType.DMA((2,))]),
        out_shape=[jax.ShapeDtypeStruct((n_ptiles * TOK_TILE, D_MODEL), F32),
                   jax.ShapeDtypeStruct((n_stiles * TOK_TILE, D_MODEL), F32)],
        compiler_params=_params(2),
        name="moe_combine",
    )(plan["loc"], plan["n"], plan["dst"], ys, a1, a2, posw, x1, mod)


def _layer(xp, xs, cache_k, cache_v, state, c_prompt, c_sample, norm_mix_g, norm_ffn_g, w_ada, b_ada, w_in,
           q_norm_g, k_norm_g, rel_bias, w_gate_up, b_gate, gla_norm_g, w_out, w_route_group,
           b_route_group, w_route_expert, b_route_expert, w_exp_gate, w_exp_up, w_exp_down):
    batch, seq, _ = xp.shape
    n_seq, dec_seq, _ = xs.shape
    assert batch == 1 and dec_seq == CHUNK and cache_k.shape[1] == BAND_PAST
    assert seq % TOK_TILE == 0 and seq >= BAND_PAST and (n_seq * CHUNK) % TOK_TILE == 0
    assert seq % (ATTN_SUB * Q_ROWS) == 0 and seq % (GLA_SUB * GLA_CHUNKS * CHUNK) == 0
    n_ptok, n_stok = seq, n_seq * CHUNK
    n_ptiles, n_stiles = n_ptok // TOK_TILE, n_stok // TOK_TILE
    sb = MOE_SUPER_BLOCK if (n_ptok % MOE_SUPER_BLOCK == 0 and n_stok % MOE_SUPER_BLOCK == 0) else TOK_TILE
    prep = ROWS_PER_TILE

    xp2 = xp.reshape(n_ptok, D_MODEL)
    xs2 = xs.reshape(n_stok, D_MODEL)
    c_rows = jnp.concatenate([jnp.broadcast_to(c_prompt, (prep, D_MODEL)), c_sample], axis=0)
    mod = _adaln(c_rows, w_ada, b_ada)

    w_main = w_in[:, 0:IN_MAIN].astype(BF16)
    w_gr = jnp.pad(w_in[:, IN_MAIN:], ((0, 0), (0, LANES - GATE_RANK))).astype(BF16)
    wgu_p = jnp.pad(w_gate_up, ((0, LANES - GATE_RANK), (0, 0))).astype(BF16)
    head = np.arange(A_WIDTH) // A_HEAD_DIM
    bd = jnp.asarray(head[:, None] == head[None, :], BF16)
    gq = jnp.tile(q_norm_g, A_HEADS).reshape(1, A_WIDTH)
    gk = jnp.tile(k_norm_g, A_HEADS).reshape(1, A_WIDTH)
    q, k, v, kf, vf, gla, la = _inproj(
        xp2, xs2, mod, norm_mix_g.reshape(1, D_MODEL), w_main, w_gr, bd, gq, gk, wgu_p,
        b_gate.reshape(1, B_KWIDTH), n_ptiles, n_stiles, prep)

    first_chunk = n_ptok // CHUNK
    oa_p = _attn_prompt(rel_bias[:, _bias_lanes(ATTN_WIN * Q_ROWS)], q, k, v, n_ptok // (ATTN_SUB * Q_ROWS))
    oa_s = _attn_sample(rel_bias[:, _bias_lanes(SAMPLE_KEYS)], q, k, v,
                        cache_k.reshape(n_seq, BAND_PAST, A_WIDTH), cache_v.reshape(n_seq, BAND_PAST, A_WIDTH),
                        first_chunk, n_seq)
    g_gla = gla_norm_g.reshape(1, B_DV)
    ob_p, sfin_p = _gla_prompt(gla, la, g_gla, n_ptok // (GLA_SUB * GLA_CHUNKS * CHUNK))
    ob_s, sfin_s = _gla_sample(gla, la, g_gla, _state_to_pairs(state), first_chunk, n_seq)

    wr = jnp.pad(jnp.concatenate([w_route_group, w_route_expert], axis=1),
                 ((0, 0), (0, LANES - N_GROUPS - N_EXPERTS)))
    br = jnp.pad(jnp.concatenate([b_route_group, b_route_expert]), (0, LANES - N_GROUPS - N_EXPERTS))
    x1, h2p, meta, cnt = _outproj(oa_p, oa_s, ob_p, ob_s, w_out.astype(BF16), xp2, xs2, mod,
                                  norm_ffn_g.reshape(1, D_MODEL), wr, br.reshape(1, LANES),
                                  n_ptiles, n_stiles, prep, sb)

    posw, a1, a2, plan = _moe_plan(meta, cnt, sb)
    xs_sorted = _moe_dispatch(h2p.reshape(-1, LANES), a1, a2, plan, sb, _sorted_tiles(n_ptok + n_stok, sb))
    ys_sorted = _moe_ffn(xs_sorted, plan, w_exp_gate, w_exp_up, w_exp_down)
    yp, ys = _moe_combine(ys_sorted, a1, a2, posw, plan, x1, mod, sb, n_ptiles, n_stiles, prep)

    tail = min(BAND_PAST, seq)
    heads = (A_HEADS, A_HEAD_DIM)
    return (yp.reshape(1, seq, D_MODEL), ys.reshape(n_seq, CHUNK, D_MODEL),
            kf[TOK_TILE - tail:TOK_TILE].reshape((1, tail) + heads),
            vf[TOK_TILE - tail:TOK_TILE].reshape((1, tail) + heads),
            _pairs_to_state(sfin_p)[None],
            kf[TOK_TILE:].reshape((n_seq, CHUNK) + heads),
            vf[TOK_TILE:].reshape((n_seq, CHUNK) + heads),
            _pairs_to_state(sfin_s))


def kernel(x_prompt, x_sample, cache_a_k, cache_a_v, state_gla, c_prompt, c_sample, norm_mix_g, norm_ffn_g,
           w_ada, b_ada, w_in, q_norm_g, k_norm_g, rel_bias, w_gate_up, b_gate, gla_norm_g, w_out,
           w_route_group, b_route_group, w_route_expert, b_route_expert, w_exp_gate, w_exp_up, w_exp_down):
    depth = w_in.shape[0]
    yp, ys = x_prompt, x_sample
    outs = [[] for _ in range(6)]
    for l in range(depth):
        yp, ys, kp, vp, sp, ks, vs, ss = _layer(
            yp, ys, cache_a_k[l], cache_a_v[l], state_gla[l], c_prompt, c_sample, norm_mix_g[l], norm_ffn_g[l],
            w_ada[l], b_ada[l], w_in[l], q_norm_g[l], k_norm_g[l], rel_bias[l], w_gate_up[l], b_gate[l],
            gla_norm_g[l], w_out[l], w_route_group[l], b_route_group[l], w_route_expert[l], b_route_expert[l],
            w_exp_gate[l], w_exp_up[l], w_exp_down[l])
        for lst, val in zip(outs, (kp, vp, sp, ks, vs, ss)):
            lst.append(val)
    return (yp, ys) + tuple(jnp.stack(o) for o in outs)
```

```python
import functools

import numpy as np
import jax
import jax.numpy as jnp
from jax import lax
from jax.experimental import pallas as pl
from jax.experimental.pallas import tpu as pltpu

F32 = jnp.float32
BF16 = jnp.bfloat16
U32 = jnp.uint32

D_MODEL = 1024
CHUNK = 64
LOG_CHUNK = 6
BAND_CHUNKS = 8
BAND_PAST = BAND_CHUNKS * CHUNK
A_WIDTH = 512
A_HEADS = 8
A_HEAD_DIM = 64
MAX_REL = 128
N_REL = CHUNK + MAX_REL
B_WIDTH = 512
B_HEADS = 4
B_DV = 128
B_DK = 64
B_KWIDTH = 256
GATE_RANK = 16
GATE_TAU = 16.0
N_GROUPS = 4
EXPERTS_PER_GROUP = 8
N_EXPERTS = 32
EXPERT_FF = 256
EPS = 1e-6

LANES = 128
IN_MAIN = 3 * A_WIDTH + 2 * B_KWIDTH + 2 * B_WIDTH
TOK_TILE = 512
ROWS_PER_TILE = TOK_TILE // CHUNK
Q_CHUNKS = 4
Q_ROWS = Q_CHUNKS * CHUNK
ROLL_W = 1024
NEG = -1e30
ROUTE_OFF = N_GROUPS
VMEM_LIMIT = 56 * 1024 * 1024


def _params(n_axes=1):
    return pltpu.CompilerParams(dimension_semantics=("arbitrary",) * n_axes,
                                vmem_limit_bytes=VMEM_LIMIT)


def _split(a):
    hi = a.astype(BF16)
    lo = (a - hi.astype(F32)).astype(BF16)
    return hi, lo


def _dot(a, b):
    return jnp.dot(a, b, preferred_element_type=F32)


def _dot3(a, b):
    ah, al = _split(a)
    bh, bl = _split(b)
    return _dot(ah, bh) + _dot(al, bh) + _dot(ah, bl)


def _dot_nt(a, b):
    return lax.dot_general(a, b, (((1,), (1,)), ((), ())), preferred_element_type=F32)


def _dot_tn(a, b):
    return lax.dot_general(a, b, (((0,), (0,)), ((), ())), preferred_element_type=F32)


def _silu(x):
    return x / (1.0 + jnp.exp(-x))


def _rows_to_tokens(rows, n):
    r = rows.shape[0]
    return jnp.broadcast_to(rows[:, None, :], (r, CHUNK, n)).reshape(r * CHUNK, n)


def _adaln_kernel(c_ref, w_ref, b_ref, o_ref):
    a = _silu(c_ref[...])
    o_ref[...] = _dot3(a, w_ref[...]) + b_ref[...]


def _adaln(c_rows, w_ada, b_ada):
    r = c_rows.shape[0]
    n = w_ada.shape[1]
    tn = 1024
    return pl.pallas_call(
        _adaln_kernel,
        grid=(n // tn,),
        in_specs=[pl.BlockSpec((r, D_MODEL), lambda j: (0, 0)),
                  pl.BlockSpec((D_MODEL, tn), lambda j: (0, j)),
                  pl.BlockSpec((1, tn), lambda j: (0, j))],
        out_specs=pl.BlockSpec((r, tn), lambda j: (0, j)),
        out_shape=jax.ShapeDtypeStruct((r, n), F32),
        compiler_params=_params(),
        name="adaln",
    )(c_rows, w_ada, b_ada.reshape(1, n))


def _head_rms(z, bd_ref, g):
    ms = _dot((z * z).astype(BF16), bd_ref[...]) * (1.0 / A_HEAD_DIM)
    return z * lax.rsqrt(ms + EPS) * g


def _inproj_kernel(n_ptiles, xp_ref, xs_ref, mod_ref, gmix_ref, w_ref, wgr_ref, bd_ref, gq_ref, gk_ref,
                   wgu_ref, bg_ref,
                   q_ref, k_ref, v_ref, kf_ref, vf_ref, gla_ref, la_ref):
    i = pl.program_id(0)
    x = jnp.where(i < n_ptiles, xp_ref[...], xs_ref[...])
    ms = jnp.mean(x * x, axis=-1, keepdims=True)
    xn = x * lax.rsqrt(ms + EPS) * gmix_ref[...]
    sh = _rows_to_tokens(mod_ref[:, 0:D_MODEL], D_MODEL)
    sc = _rows_to_tokens(mod_ref[:, D_MODEL:2 * D_MODEL], D_MODEL)
    hb = (xn * (1.0 + sc) + sh).astype(BF16)

    zq = _dot(hb, w_ref[:, 0:A_WIDTH])
    q_ref[...] = (_head_rms(zq, bd_ref, gq_ref[...]) * (LOG2E * A_HEAD_DIM ** -0.5)).astype(BF16)
    zk = _dot(hb, w_ref[:, A_WIDTH:2 * A_WIDTH])
    kn = _head_rms(zk, bd_ref, gk_ref[...])
    k_ref[...] = kn.astype(BF16)
    kf_ref[...] = kn
    zv = _dot(hb, w_ref[:, 2 * A_WIDTH:3 * A_WIDTH])
    v_ref[...] = zv.astype(BF16)
    vf_ref[...] = zv

    o = 3 * A_WIDTH
    zqb = _dot(hb, w_ref[:, o:o + B_KWIDTH]) * (B_DK ** -0.5)
    gla_ref[:, 0:B_KWIDTH] = zqb.astype(BF16)
    for c in range(B_KWIDTH, 2 * B_KWIDTH + 2 * B_WIDTH, 256):
        gla_ref[:, c:c + 256] = _dot(hb, w_ref[:, o + c:o + c + 256]).astype(BF16)

    gr = _dot(hb, wgr_ref[...])
    logit = _dot(gr.astype(BF16), wgu_ref[...]) + bg_ref[...]
    log_sig = jnp.minimum(logit, 0.0) - jnp.log1p(jnp.exp(-jnp.abs(logit)))
    la_ref[...] = log_sig * (1.0 / GATE_TAU)


def _inproj(xp, xs, mod, gmix, w_main, w_gr, bd, gq, gk, wgu_p, bg, n_ptiles, n_stiles, prep):
    n_tiles = n_ptiles + n_stiles
    t = n_tiles * TOK_TILE
    tail_tiles = 1 + n_stiles
    pblocks = prep // ROWS_PER_TILE
    const = lambda i: (0, 0)
    row = lambda i: (i, 0)
    tail = lambda i: (jnp.maximum(i - (n_ptiles - 1), 0), 0)
    return pl.pallas_call(
        functools.partial(_inproj_kernel, n_ptiles),
        grid=(n_tiles,),
        in_specs=[pl.BlockSpec((TOK_TILE, D_MODEL), lambda i: (jnp.minimum(i, n_ptiles - 1), 0)),
                  pl.BlockSpec((TOK_TILE, D_MODEL), lambda i: (jnp.maximum(i - n_ptiles, 0), 0)),
                  pl.BlockSpec((ROWS_PER_TILE, 6 * D_MODEL),
                               lambda i: (jnp.maximum(i - n_ptiles + pblocks, 0), 0)),
                  pl.BlockSpec((1, D_MODEL), const),
                  pl.BlockSpec((D_MODEL, IN_MAIN), const),
                  pl.BlockSpec((D_MODEL, LANES), const),
                  pl.BlockSpec((A_WIDTH, A_WIDTH), const),
                  pl.BlockSpec((1, A_WIDTH), const),
                  pl.BlockSpec((1, A_WIDTH), const),
                  pl.BlockSpec((LANES, B_KWIDTH), const),
                  pl.BlockSpec((1, B_KWIDTH), const)],
        out_specs=[pl.BlockSpec((TOK_TILE, A_WIDTH), row),
                   pl.BlockSpec((TOK_TILE, A_WIDTH), row),
                   pl.BlockSpec((TOK_TILE, A_WIDTH), row),
                   pl.BlockSpec((TOK_TILE, A_WIDTH), tail),
                   pl.BlockSpec((TOK_TILE, A_WIDTH), tail),
                   pl.BlockSpec((TOK_TILE, 2 * B_KWIDTH + 2 * B_WIDTH), row),
                   pl.BlockSpec((TOK_TILE, B_KWIDTH), row)],
        out_shape=[jax.ShapeDtypeStruct((t, A_WIDTH), BF16),
                   jax.ShapeDtypeStruct((t, A_WIDTH), BF16),
                   jax.ShapeDtypeStruct((t, A_WIDTH), BF16),
                   jax.ShapeDtypeStruct((tail_tiles * TOK_TILE, A_WIDTH), F32),
                   jax.ShapeDtypeStruct((tail_tiles * TOK_TILE, A_WIDTH), F32),
                   jax.ShapeDtypeStruct((t, 2 * B_KWIDTH + 2 * B_WIDTH), BF16),
                   jax.ShapeDtypeStruct((t, B_KWIDTH), F32)],
        compiler_params=_params(),
        name="inproj",
    )(xp, xs, mod, gmix, w_main, w_gr, bd, gq, gk, wgu_p, bg)


def _bias_lanes(n_keys):
    l = np.arange(ROLL_W)
    d = np.where(l < n_keys, BAND_PAST - l, BAND_PAST - l + ROLL_W)
    return np.clip(d, -(CHUNK - 1), MAX_REL) + (CHUNK - 1)


LOG2E = 1.4426950408889634


def _band_mask(m_rows, n_keys, first_col):
    qi = lax.broadcasted_iota(jnp.int32, (m_rows, n_keys), 0) >> LOG_CHUNK
    kw = lax.broadcasted_iota(jnp.int32, (m_rows, n_keys), 1)
    kc = kw >> LOG_CHUNK
    return (kc >= qi) & (kc <= qi + BAND_CHUNKS) & (kw >= first_col)


def _bias_tile(u_ref, h, ok):
    m_rows, n_keys = ok.shape
    src = jnp.broadcast_to(u_ref[h:h + 1, :] * LOG2E, (m_rows, ROLL_W))
    toe = pltpu.roll(src, 0, 1, stride=1, stride_axis=0)
    return jnp.where(ok, toe[:, 0:n_keys], NEG)


def _attend(q, kcat, vcat, bias_sc):
    m_rows = q.shape[0]
    first = lax.broadcasted_iota(jnp.int32, (m_rows, LANES), 1) < A_HEAD_DIM
    outs = []
    for p in range(A_HEADS // 2):
        lanes = slice(p * LANES, (p + 1) * LANES)
        qp, kp, vp = q[:, lanes], kcat[:, lanes], vcat[:, lanes]
        zero = jnp.zeros_like(qp)
        q2 = jnp.concatenate([jnp.where(first, qp, zero), jnp.where(first, zero, qp)], axis=0)
        s = _dot_nt(q2, kp) + bias_sc[p]
        e = jnp.exp2(s - jnp.max(s, axis=-1, keepdims=True))
        l = jnp.sum(e, axis=-1, keepdims=True)
        o2 = _dot(e.astype(BF16), vp) / l
        outs.append(jnp.where(first, o2[0:m_rows], o2[m_rows:2 * m_rows]))
    return jnp.concatenate(outs, axis=-1)


ATTN_SUB = 4
ATTN_WIN = 3


def _attn_prompt_kernel(u_ref, q_ref, *refs):
    k_refs = refs[0:ATTN_SUB + ATTN_WIN - 1]
    v_refs = refs[ATTN_SUB + ATTN_WIN - 1:2 * (ATTN_SUB + ATTN_WIN - 1)]
    o_ref, bias_sc = refs[-2:]
    j = pl.program_id(0)
    n_keys = ATTN_WIN * Q_ROWS

    @pl.when(j == 0)
    def _():
        for g in range(ATTN_WIN):
            ok = _band_mask(Q_ROWS, n_keys, (ATTN_WIN - 1 - g) * Q_ROWS)
            for h in range(A_HEADS):
                bias_sc[g, h // 2, (h % 2) * Q_ROWS:(h % 2 + 1) * Q_ROWS, :] = _bias_tile(u_ref, h, ok)

    ks = [r[...] for r in k_refs]
    vs = [r[...] for r in v_refs]
    for sub in range(ATTN_SUB):
        rows = slice(sub * Q_ROWS, (sub + 1) * Q_ROWS)
        kcat = jnp.concatenate(ks[sub:sub + ATTN_WIN], axis=0)
        vcat = jnp.concatenate(vs[sub:sub + ATTN_WIN], axis=0)
        bias = bias_sc.at[jnp.minimum(ATTN_SUB * j + sub, ATTN_WIN - 1)]
        o_ref[rows, :] = _attend(q_ref[rows, :], kcat, vcat, bias).astype(BF16)


def _attn_prompt(u, q, k, v, n_steps):
    const = lambda j: (0, 0)
    n_blk = ATTN_SUB + ATTN_WIN - 1
    blk = lambda d: pl.BlockSpec((Q_ROWS, A_WIDTH),
                                 lambda j, d=d: (jnp.maximum(ATTN_SUB * j - (ATTN_WIN - 1) + d, 0), 0))
    step_rows = ATTN_SUB * Q_ROWS
    return pl.pallas_call(
        _attn_prompt_kernel,
        grid=(n_steps,),
        in_specs=[pl.BlockSpec((A_HEADS, ROLL_W), const), pl.BlockSpec((step_rows, A_WIDTH), lambda j: (j, 0))]
                 + [blk(d) for d in range(n_blk)] * 2,
        out_specs=pl.BlockSpec((step_rows, A_WIDTH), lambda j: (j, 0)),
        out_shape=jax.ShapeDtypeStruct((n_steps * step_rows, A_WIDTH), BF16),
        scratch_shapes=[pltpu.VMEM((ATTN_WIN, A_HEADS // 2, 2 * Q_ROWS, ATTN_WIN * Q_ROWS), F32)],
        compiler_params=_params(),
        name="attn_prompt",
    )(u, q, *([k] * n_blk), *([v] * n_blk))


SAMPLE_KEYS = BAND_PAST + 2 * CHUNK
SAMPLE_STREAMS = 4


def _attn_sample_kernel(u_ref, q_ref, kn_ref, vn_ref, kc_ref, vc_ref, o_ref, bias_sc):
    @pl.when(pl.program_id(0) == 0)
    def _():
        ok = _band_mask(CHUNK, SAMPLE_KEYS, 0)
        for p in range(A_HEADS // 2):
            pair = jnp.concatenate([_bias_tile(u_ref, 2 * p, ok), _bias_tile(u_ref, 2 * p + 1, ok)], axis=0)
            bias_sc[p] = pair.T

    pad = jnp.zeros((CHUNK, A_WIDTH), BF16)
    lane = lax.broadcasted_iota(jnp.int32, (CHUNK, LANES), 1)
    first = lane < A_HEAD_DIM
    zero = jnp.zeros((CHUNK, LANES), BF16)
    for n in range(SAMPLE_STREAMS):
        rows = slice(n * CHUNK, (n + 1) * CHUNK)
        kcat = jnp.concatenate([kc_ref[n].astype(BF16), kn_ref[rows, :], pad], axis=0)
        vcat = jnp.concatenate([vc_ref[n].astype(BF16), vn_ref[rows, :], pad], axis=0)
        q = q_ref[rows, :]
        outs = []
        for p in range(A_HEADS // 2):
            lanes = slice(p * LANES, (p + 1) * LANES)
            qp = q[:, lanes]
            q_rows = jnp.concatenate([jnp.where(first, qp, zero), jnp.where(first, zero, qp)], axis=0)
            s = _dot_nt(kcat[:, lanes], q_rows) + bias_sc[p]
            e = jnp.exp2(s - jnp.max(s, axis=0, keepdims=True))
            pn = (e * (1.0 / jnp.sum(e, axis=0, keepdims=True))).astype(BF16)
            r = _dot_tn(pn, vcat[:, lanes])
            outs.append(jnp.where(first, r[0:CHUNK], r[CHUNK:2 * CHUNK]))
        o_ref[rows, :] = jnp.concatenate(outs, axis=-1).astype(BF16)


def _attn_sample(u, q, k, v, kc, vc, first_chunk, n_seq):
    assert n_seq % SAMPLE_STREAMS == 0 and first_chunk % SAMPLE_STREAMS == 0
    rows = SAMPLE_STREAMS * CHUNK
    new = pl.BlockSpec((rows, A_WIDTH), lambda b: (first_chunk // SAMPLE_STREAMS + b, 0))
    cache = pl.BlockSpec((SAMPLE_STREAMS, BAND_PAST, A_WIDTH), lambda b: (b, 0, 0))
    return pl.pallas_call(
        _attn_sample_kernel,
        grid=(n_seq // SAMPLE_STREAMS,),
        in_specs=[pl.BlockSpec((A_HEADS, ROLL_W), lambda b: (0, 0)), new, new, new, cache, cache],
        out_specs=pl.BlockSpec((rows, A_WIDTH), lambda b: (b, 0)),
        out_shape=jax.ShapeDtypeStruct((n_seq * CHUNK, A_WIDTH), BF16),
        scratch_shapes=[pltpu.VMEM((A_HEADS // 2, SAMPLE_KEYS, LANES), F32)],
        compiler_params=_params(),
        name="attn_sample",
    )(u, q, k, v, kc, vc)


GLA_CHUNKS = 4
GLA_SUB = 4


def _gla_block(n_chunks, gla_ref, la_ref, ltri_ref, g_ref, st_sc, o_ref):
    rows = n_chunks * CHUNK
    la = la_ref[...]
    la_hi, la_lo = _split(la)
    b = _dot(ltri_ref[...], la_hi) + _dot(ltri_ref[...], la_lo)
    b3 = b.reshape(n_chunks, CHUNK, B_KWIDTH)
    b_mid = b3[:, CHUNK // 2 - 1:CHUNK // 2, :]
    b_last = b3[:, CHUNK - 1:CHUNK, :]
    q = gla_ref[:, 0:B_KWIDTH].astype(F32).reshape(n_chunks, CHUNK, B_KWIDTH)
    k = gla_ref[:, B_KWIDTH:2 * B_KWIDTH].astype(F32).reshape(n_chunks, CHUNK, B_KWIDTH)
    q_start = (q * jnp.exp(b3)).reshape(rows, B_KWIDTH).astype(BF16)
    q_mid = (q * jnp.exp(b3 - b_mid)).reshape(rows, B_KWIDTH).astype(BF16)
    k_mid = (k * jnp.exp(b_mid - b3)).reshape(rows, B_KWIDTH).astype(BF16)
    k_end = (k * jnp.exp(b_last - b3)).reshape(rows, B_KWIDTH).astype(BF16)
    dec = jnp.exp(b_last)

    ti = lax.broadcasted_iota(jnp.int32, (2 * rows, rows), 0) & (rows - 1)
    si = lax.broadcasted_iota(jnp.int32, (2 * rows, rows), 1)
    causal = (si <= ti) & ((si >> LOG_CHUNK) == (ti >> LOG_CHUNK))
    first_r = lax.broadcasted_iota(jnp.int32, (rows, LANES), 1) < B_DK
    first_c = lax.broadcasted_iota(jnp.int32, (CHUNK, LANES), 1) < B_DK
    first_s = lax.broadcasted_iota(jnp.int32, (B_DV, LANES), 1) < B_DK

    def stack_heads(x, first):
        zero = jnp.zeros_like(x)
        return jnp.concatenate([jnp.where(first, x, zero), jnp.where(first, zero, x)], axis=0)

    for p in range(B_HEADS // 2):
        lanes = slice(p * LANES, (p + 1) * LANES)
        qs_p, qm_p, km_p, ke_p = q_start[:, lanes], q_mid[:, lanes], k_mid[:, lanes], k_end[:, lanes]
        v_pair = gla_ref[:, 2 * B_KWIDTH + 2 * p * B_DV:2 * B_KWIDTH + (2 * p + 2) * B_DV]
        sc = jnp.where(causal, _dot_nt(stack_heads(qm_p, first_r), km_p), 0.0)
        o2 = _dot(sc.astype(BF16), v_pair)
        intra = [o2[0:rows, 0:B_DV], o2[rows:2 * rows, B_DV:2 * B_DV]]
        inter = [[], []]
        st = st_sc[p]
        for c in range(n_chunks):
            cr = slice(c * CHUNK, (c + 1) * CHUNK)
            r2 = _dot_nt(stack_heads(qs_p[cr], first_c), st.astype(BF16))
            inter[0].append(r2[0:CHUNK])
            inter[1].append(r2[CHUNK:2 * CHUNK])
            u2 = _dot_tn(v_pair[cr], ke_p[cr])
            st = st * dec[c, :, lanes] + jnp.where(first_s, u2[0:B_DV], u2[B_DV:2 * B_DV])
        st_sc[p] = st
        for hh in range(2):
            h = 2 * p + hh
            o = intra[hh] + jnp.concatenate(inter[hh], axis=0)
            ms = jnp.mean(o * o, axis=-1, keepdims=True)
            on = o * lax.rsqrt(ms + EPS) * g_ref[...]
            r = gla_ref[:, 2 * B_KWIDTH + B_WIDTH + h * B_DV:2 * B_KWIDTH + B_WIDTH + (h + 1) * B_DV]
            o_ref[:, h * B_DV:(h + 1) * B_DV] = (on * _silu(r.astype(F32))).astype(BF16)


def _gla_prompt_kernel(gla_ref, la_ref, ltri_ref, g_ref, o_ref, sfin_ref, st_sc):
    @pl.when(pl.program_id(0) == 0)
    def _():
        st_sc[...] = jnp.zeros_like(st_sc)

    rows = GLA_CHUNKS * CHUNK
    for sub in range(GLA_SUB):
        part = pl.ds(sub * rows, rows)
        _gla_block(GLA_CHUNKS, gla_ref.at[part], la_ref.at[part], ltri_ref, g_ref, st_sc, o_ref.at[part])
    sfin_ref[...] = st_sc[...]


def _gla_sample_kernel(gla_ref, la_ref, ltri_ref, g_ref, s0_ref, o_ref, sfin_ref, st_sc):
    st_sc[...] = s0_ref[...]
    for n in range(SAMPLE_STREAMS):
        part = pl.ds(n * CHUNK, CHUNK)
        _gla_block(1, gla_ref.at[part], la_ref.at[part], ltri_ref, g_ref, st_sc.at[n], o_ref.at[part])
    sfin_ref[...] = st_sc[...]


def _ltri(n_chunks):
    r = np.arange(n_chunks * CHUNK)
    m = (r[None, :] <= r[:, None]) & (r[None, :] // CHUNK == r[:, None] // CHUNK)
    return jnp.asarray(m, BF16)


_GLA_W = 2 * B_KWIDTH + 2 * B_WIDTH
_ST_SHAPE = (B_HEADS // 2, B_DV, LANES)


def _gla_prompt(gla, la, g, n_steps):
    rows = GLA_SUB * GLA_CHUNKS * CHUNK
    const = lambda j: (0, 0)
    return pl.pallas_call(
        _gla_prompt_kernel,
        grid=(n_steps,),
        in_specs=[pl.BlockSpec((rows, _GLA_W), lambda j: (j, 0)),
                  pl.BlockSpec((rows, B_KWIDTH), lambda j: (j, 0)),
                  pl.BlockSpec((GLA_CHUNKS * CHUNK, GLA_CHUNKS * CHUNK), const),
                  pl.BlockSpec((1, B_DV), const)],
        out_specs=[pl.BlockSpec((rows, B_WIDTH), lambda j: (j, 0)),
                   pl.BlockSpec(_ST_SHAPE, lambda j: (0, 0, 0))],
        out_shape=[jax.ShapeDtypeStruct((n_steps * rows, B_WIDTH), BF16),
                   jax.ShapeDtypeStruct(_ST_SHAPE, F32)],
        scratch_shapes=[pltpu.VMEM(_ST_SHAPE, F32)],
        compiler_params=_params(),
        name="gla_prompt",
    )(gla, la, _ltri(GLA_CHUNKS), g)


def _gla_sample(gla, la, g, s0, first_chunk, n_seq):
    assert n_seq % SAMPLE_STREAMS == 0 and first_chunk % SAMPLE_STREAMS == 0
    const = lambda b: (0, 0)
    rows = SAMPLE_STREAMS * CHUNK
    first = first_chunk // SAMPLE_STREAMS
    st_spec = pl.BlockSpec((SAMPLE_STREAMS,) + _ST_SHAPE, lambda b: (b, 0, 0, 0))
    return pl.pallas_call(
        _gla_sample_kernel,
        grid=(n_seq // SAMPLE_STREAMS,),
        in_specs=[pl.BlockSpec((rows, _GLA_W), lambda b: (first + b, 0)),
                  pl.BlockSpec((rows, B_KWIDTH), lambda b: (first + b, 0)),
                  pl.BlockSpec((CHUNK, CHUNK), const),
                  pl.BlockSpec((1, B_DV), const),
                  st_spec],
        out_specs=[pl.BlockSpec((rows, B_WIDTH), lambda b: (b, 0)), st_spec],
        out_shape=[jax.ShapeDtypeStruct((n_seq * CHUNK, B_WIDTH), BF16),
                   jax.ShapeDtypeStruct((n_seq,) + _ST_SHAPE, F32)],
        scratch_shapes=[pltpu.VMEM((SAMPLE_STREAMS,) + _ST_SHAPE, F32)],
        compiler_params=_params(),
        name="gla_sample",
    )(gla, la, _ltri(1), g, s0)


def _state_to_pairs(s):
    lead = s.shape[:-3]
    s = s.reshape(lead + (B_HEADS // 2, 2, B_DK, B_DV))
    s = jnp.moveaxis(s, -1, -3)
    return s.reshape(lead + (B_HEADS // 2, B_DV, 2 * B_DK))


def _pairs_to_state(s):
    lead = s.shape[:-3]
    s = s.reshape(lead + (B_HEADS // 2, B_DV, 2, B_DK))
    s = jnp.moveaxis(s, -3, -1)
    return s.reshape(lead + (B_HEADS, B_DK, B_DV))


def _route(logits):
    lane = lax.broadcasted_iota(jnp.int32, logits.shape, 1)
    lane_f = lane.astype(F32)
    big = float(LANES)
    gmask = lane < N_GROUPS
    gl = jnp.where(gmask, logits, NEG)
    gmax = jnp.max(gl, axis=-1, keepdims=True)
    gsel = jnp.min(jnp.where(gl == gmax, lane_f, big), axis=-1, keepdims=True)
    gsum = jnp.sum(jnp.where(gmask, jnp.exp(gl - gmax), 0.0), axis=-1, keepdims=True)
    g_w = 1.0 / gsum
    e_lo = ROUTE_OFF + gsel * EXPERTS_PER_GROUP
    emask = (lane_f >= e_lo) & (lane_f < e_lo + EXPERTS_PER_GROUP)
    el = jnp.where(emask, logits, NEG)
    v1 = jnp.max(el, axis=-1, keepdims=True)
    i1 = jnp.min(jnp.where(el == v1, lane_f, big), axis=-1, keepdims=True)
    el2 = jnp.where(lane_f == i1, NEG, el)
    v2 = jnp.max(el2, axis=-1, keepdims=True)
    i2 = jnp.min(jnp.where(el2 == v2, lane_f, big), axis=-1, keepdims=True)
    t = jnp.exp(v2 - v1)
    w1 = g_w / (1.0 + t)
    w2 = g_w * t / (1.0 + t)
    return lane_f, i1, i2, w1, w2


ROW_PIECES = D_MODEL // 2 // LANES
SUBLANES = 8
ROW_TILE = ROW_PIECES * SUBLANES


def _pack_rows(z32_sc, x, rows):
    half = D_MODEL // 2
    out = []
    for c in range(ROW_PIECES):
        z32_sc[c, pl.ds(0, rows, stride=2), :] = x[:, c * LANES:(c + 1) * LANES]
        z32_sc[c, pl.ds(1, rows, stride=2), :] = x[:, half + c * LANES:half + (c + 1) * LANES]
        out.append(z32_sc[c].astype(BF16))
    return out


def _unpack_rows(z32_sc, pieces, rows):
    lo, hi = [], []
    for c in range(ROW_PIECES):
        z32_sc[c] = pieces[c].astype(F32)
        lo.append(z32_sc[c, pl.ds(0, rows, stride=2), :])
        hi.append(z32_sc[c, pl.ds(1, rows, stride=2), :])
    return jnp.concatenate(lo, axis=1), jnp.concatenate(hi, axis=1)


def _to_row_tiled(pieces, tokens):
    per_tile = pieces[0].shape[0] * SUBLANES // tokens
    return jnp.stack([p.reshape(tokens // SUBLANES, per_tile, LANES) for p in pieces], axis=1)


def _from_row_tiled(flat, tokens):
    per_tile = flat.shape[0] // (tokens // SUBLANES) // ROW_PIECES
    tiled = flat.reshape(tokens // SUBLANES, ROW_PIECES, per_tile, LANES)
    return [tiled[:, c].reshape(tokens // SUBLANES * per_tile, LANES) for c in range(ROW_PIECES)]


def _flatten_tiled(tiled):
    return tiled.reshape(-1, LANES)


def _outproj_kernel(n_ptiles, tiles_per_sb, oap_ref, oas_ref, obp_ref, obs_ref, wo_ref, xp_ref, xs_ref, mod_ref,
                    gffn_ref, wr_ref, br_ref, ltri_ref, x1_ref, h2p_ref, meta_ref, cnt_ref, z32_sc, cnt_sc):
    i = pl.program_id(0)
    is_prompt = i < n_ptiles
    x = jnp.where(is_prompt, xp_ref[...], xs_ref[...])
    oa = jnp.where(is_prompt, oap_ref[...], oas_ref[...])
    ob = jnp.where(is_prompt, obp_ref[...], obs_ref[...])
    mix = _dot(oa, wo_ref[0:A_WIDTH, :]) + _dot(ob, wo_ref[A_WIDTH:D_MODEL, :])
    gate1 = _rows_to_tokens(mod_ref[:, 2 * D_MODEL:3 * D_MODEL], D_MODEL)
    x1 = x + gate1 * mix
    x1_ref[...] = x1
    ms = jnp.mean(x1 * x1, axis=-1, keepdims=True)
    xn = x1 * lax.rsqrt(ms + EPS) * gffn_ref[...]
    sh = _rows_to_tokens(mod_ref[:, 3 * D_MODEL:4 * D_MODEL], D_MODEL)
    sc = _rows_to_tokens(mod_ref[:, 4 * D_MODEL:5 * D_MODEL], D_MODEL)
    h2 = xn * (1.0 + sc) + sh
    words = [pltpu.bitcast(p, U32) for p in _pack_rows(z32_sc, h2, TOK_TILE)]
    h2p_ref[...] = _to_row_tiled(words, TOK_TILE)

    lane_f, i1, i2, w1, w2 = _route(_dot3(h2, wr_ref[...]) + br_ref[...])

    @pl.when(lax.rem(i, tiles_per_sb) == 0)
    def _():
        cnt_sc[...] = jnp.zeros_like(cnt_sc)

    sel = jnp.where((lane_f == i1) | (lane_f == i2), 1.0, 0.0).astype(BF16)
    before = _dot(ltri_ref[...], sel) + cnt_sc[0:1, :]
    rank1 = jnp.sum(jnp.where(lane_f == i1, before, 0.0), axis=-1, keepdims=True)
    rank2 = jnp.sum(jnp.where(lane_f == i2, before, 0.0), axis=-1, keepdims=True)
    cnt = cnt_sc[...] + _dot(jnp.ones((8, TOK_TILE), BF16), sel)
    cnt_sc[...] = cnt
    cnt_ref[0] = cnt
    cols = (i1, i2, rank1, rank2, w1, w2)
    meta = jnp.zeros_like(lane_f)
    for c, col in enumerate(cols):
        meta = jnp.where(lane_f == float(c), col, meta)
    meta_ref[...] = meta


def _outproj(oa_p, oa_s, ob_p, ob_s, w_out, xp, xs, mod, gffn, wr, br, n_ptiles, n_stiles, prep, sb):
    n_tiles = n_ptiles + n_stiles
    t = n_tiles * TOK_TILE
    pblocks = prep // ROWS_PER_TILE
    tiles_per_sb = sb // TOK_TILE
    const = lambda i: (0, 0)
    row = lambda i: (i, 0)
    prow = lambda i: (jnp.minimum(i, n_ptiles - 1), 0)
    srow = lambda i: (jnp.maximum(i - n_ptiles, 0), 0)
    r = np.arange(TOK_TILE)
    ltri = jnp.asarray(r[None, :] < r[:, None], BF16)
    return pl.pallas_call(
        functools.partial(_outproj_kernel, n_ptiles, tiles_per_sb),
        grid=(n_tiles,),
        in_specs=[pl.BlockSpec((TOK_TILE, A_WIDTH), prow),
                  pl.BlockSpec((TOK_TILE, A_WIDTH), srow),
                  pl.BlockSpec((TOK_TILE, B_WIDTH), prow),
                  pl.BlockSpec((TOK_TILE, B_WIDTH), srow),
                  pl.BlockSpec((D_MODEL, D_MODEL), const),
                  pl.BlockSpec((TOK_TILE, D_MODEL), prow),
                  pl.BlockSpec((TOK_TILE, D_MODEL), srow),
                  pl.BlockSpec((ROWS_PER_TILE, 6 * D_MODEL),
                               lambda i: (jnp.maximum(i - n_ptiles + pblocks, 0), 0)),
                  pl.BlockSpec((1, D_MODEL), const),
                  pl.BlockSpec((D_MODEL, LANES), const),
                  pl.BlockSpec((1, LANES), const),
                  pl.BlockSpec((TOK_TILE, TOK_TILE), const)],
        out_specs=[pl.BlockSpec((TOK_TILE, D_MODEL), row),
                   pl.BlockSpec((TOK_TILE // SUBLANES, ROW_PIECES, SUBLANES, LANES), lambda i: (i, 0, 0, 0)),
                   pl.BlockSpec((TOK_TILE, LANES), row),
                   pl.BlockSpec((1, 8, LANES), lambda i: (i // tiles_per_sb, 0, 0))],
        out_shape=[jax.ShapeDtypeStruct((t, D_MODEL), F32),
                   jax.ShapeDtypeStruct((t // SUBLANES, ROW_PIECES, SUBLANES, LANES), U32),
                   jax.ShapeDtypeStruct((t, LANES), F32),
                   jax.ShapeDtypeStruct((t // sb, 8, LANES), F32)],
        scratch_shapes=[pltpu.VMEM((D_MODEL // 2 // LANES, 2 * TOK_TILE, LANES), F32),
                        pltpu.VMEM((8, LANES), F32)],
        compiler_params=_params(),
        name="outproj",
    )(oa_p, oa_s, ob_p, ob_s, w_out, xp, xs, mod, gffn, wr, br, ltri)


MOE_SUPER_BLOCK = 2048
SEG_ALIGN = SUBLANES
CHUNK_BF16_ROWS = 2 * SEG_ALIGN * ROW_PIECES
SEG_BITS = 9
PAD_BITS = 5
FFN_ROWS = 512
PLAN_ROWS = LANES


def _local_rows(sb):
    return 2 * sb + N_EXPERTS * SEG_ALIGN


def _sorted_tiles(n_tokens, sb):
    rows = 2 * n_tokens + (n_tokens // sb) * N_EXPERTS * SEG_ALIGN + N_EXPERTS * FFN_ROWS
    return -(-rows // FFN_ROWS)


def _moe_plan_kernel(n_blocks, total_chunks, meta_ref, cnt_ref, ustrict_ref, lstrict_ref,
                     posw_ref, addr_ref, tab_ref, tile_ref):
    b = pl.program_id(0)
    per_tile = FFN_ROWS // SEG_ALIGN

    @pl.when(b == 0)
    def _():
        cnt = cnt_ref[...]
        chunks = jnp.floor((cnt + (SEG_ALIGN - 1)) * (1.0 / SEG_ALIGN))
        chunks_b = chunks.astype(BF16)
        loc = _dot(chunks_b, ustrict_ref[...])
        before = _dot(lstrict_ref[...], chunks_b)
        tot = _dot(jnp.ones((PLAN_ROWS, PLAN_ROWS), BF16), chunks_b)
        tiles = jnp.floor((tot + (per_tile - 1)) * (1.0 / per_tile))
        tile_off = _dot(tiles.astype(BF16), ustrict_ref[...])
        n_tiles = jnp.sum(tiles[0:1], axis=-1, keepdims=True)
        lane1 = lax.broadcasted_iota(jnp.int32, (PLAN_ROWS, LANES), 1)
        tail = lane1 == ROUTE_OFF + N_EXPERTS
        pad_off = jnp.where(tail, n_tiles * per_tile, tile_off * per_tile + tot)
        pad_n = jnp.where(tail, total_chunks - n_tiles * per_tile, tiles * per_tile - tot)
        row = lax.broadcasted_iota(jnp.int32, (PLAN_ROWS, LANES), 0)
        tab_ref[0] = loc
        tab_ref[1] = chunks
        tab_ref[2] = tile_off * per_tile + before
        tab_ref[3] = jnp.where(row == 0, pad_off, jnp.where(row == 1, pad_n, jnp.where(row == 2, n_tiles, 0.0)))
        t_idx = lax.broadcasted_iota(jnp.int32, tile_ref.shape, 0).astype(F32)
        lane_t = lax.broadcasted_iota(jnp.int32, tile_ref.shape, 1)
        is_expert = (lane_t >= ROUTE_OFF) & (lane_t < ROUTE_OFF + N_EXPERTS)
        ends = (tile_off + tiles)[0:1, :]
        owner = jnp.sum(jnp.where(is_expert & (ends <= t_idx), 1.0, 0.0), axis=-1, keepdims=True)
        tile_ref[...] = jnp.broadcast_to(jnp.minimum(owner, N_EXPERTS - 1.0), tile_ref.shape)

    own = jnp.floor((cnt_ref[pl.ds(b, 1), :] + (SEG_ALIGN - 1)) * (1.0 / SEG_ALIGN))
    own_off = _dot(jnp.broadcast_to(own, (SUBLANES, LANES)).astype(BF16), ustrict_ref[...]) * SEG_ALIGN
    meta = meta_ref[...]
    lane_f = lax.broadcasted_iota(jnp.int32, meta.shape, 1).astype(F32)
    off_row = own_off[0:1, :]
    pos = []
    for k in range(2):
        e_lane = meta[:, k:k + 1]
        base = jnp.sum(jnp.where(lane_f == e_lane, off_row, 0.0), axis=-1, keepdims=True)
        p = base + meta[:, 2 + k:3 + k]
        tile = jnp.floor(p * (1.0 / SUBLANES))
        pos.append(tile * (ROW_TILE - SUBLANES) + p)
    out = jnp.zeros_like(meta)
    for c, col in enumerate((pos[0], pos[1], meta[:, 4:5], meta[:, 5:6])):
        out = jnp.where(lane_f == float(c), col, out)
    posw_ref[...] = out
    addr_ref[0] = out.T[0:SUBLANES]


def _moe_plan(meta, cnt, sb):
    n_blocks = meta.shape[0] // sb
    assert n_blocks <= PLAN_ROWS and sb // SEG_ALIGN <= 256
    n_tiles = _sorted_tiles(meta.shape[0], sb)
    tile_rows = -(-n_tiles // SUBLANES) * SUBLANES
    r = np.arange(LANES)
    ustrict = jnp.asarray(r[:, None] < r[None, :], BF16)
    lstrict = jnp.asarray(r[None, :] < r[:, None], BF16)
    cnt_all = jnp.pad(cnt[:, 0, :], ((0, PLAN_ROWS - n_blocks), (0, 0)))
    const = lambda s: (0, 0)
    posw, addr, tab, tile_owner = pl.pallas_call(
        functools.partial(_moe_plan_kernel, n_blocks, float(n_tiles * (FFN_ROWS // SEG_ALIGN))),
        grid=(n_blocks,),
        in_specs=[pl.BlockSpec((sb, LANES), lambda s: (s, 0)),
                  pl.BlockSpec((PLAN_ROWS, LANES), const),
                  pl.BlockSpec((LANES, LANES), const),
                  pl.BlockSpec((PLAN_ROWS, PLAN_ROWS), const)],
        out_specs=[pl.BlockSpec((sb, LANES), lambda s: (s, 0)),
                   pl.BlockSpec((1, SUBLANES, sb), lambda s: (s, 0, 0)),
                   pl.BlockSpec((4, PLAN_ROWS, LANES), lambda s: (0, 0, 0)),
                   pl.BlockSpec((tile_rows, LANES), const)],
        out_shape=[jax.ShapeDtypeStruct(meta.shape, F32),
                   jax.ShapeDtypeStruct((n_blocks, SUBLANES, sb), F32),
                   jax.ShapeDtypeStruct((4, PLAN_ROWS, LANES), F32),
                   jax.ShapeDtypeStruct((tile_rows, LANES), F32)],
        compiler_params=_params(),
        name="moe_plan",
    )(meta, cnt_all, ustrict, lstrict)
    experts = slice(ROUTE_OFF, ROUTE_OFF + N_EXPERTS)
    to_i32 = lambda x: x.astype(jnp.int32).reshape(-1)
    plan = dict(
        loc=to_i32(tab[0, :n_blocks, experts]), n=to_i32(tab[1, :n_blocks, experts]),
        dst=to_i32(tab[2, :n_blocks, experts]),
        pad_off=to_i32(tab[3, 0, ROUTE_OFF:ROUTE_OFF + N_EXPERTS + 1]),
        pad_n=to_i32(tab[3, 1, ROUTE_OFF:ROUTE_OFF + N_EXPERTS + 1]),
        n_tiles=to_i32(tab[3, 2, 0:1]),
        owner=to_i32(tile_owner[:n_tiles, 0]))
    return posw, to_i32(addr[:, 0, :]), to_i32(addr[:, 1, :]), plan


def _token_rows(start):
    return pl.ds(start, ROW_PIECES, stride=SUBLANES)


def _pow2_copies(src_ref, dst_ref, src_chunk, dst_chunk, n, n_bits, sem, act):
    done = 0
    for k in reversed(range(n_bits)):
        take = (n >> k) & 1
        rows = CHUNK_BF16_ROWS << k
        src0 = 0 if src_chunk is None else pl.multiple_of((src_chunk + done) * CHUNK_BF16_ROWS, CHUNK_BF16_ROWS)
        dst0 = pl.multiple_of((dst_chunk + done) * CHUNK_BF16_ROWS, CHUNK_BF16_ROWS)

        @pl.when(take == 1)
        def _(src0=src0, dst0=dst0, rows=rows):
            act(pltpu.make_async_copy(src_ref.at[pl.ds(src0, rows)], dst_ref.at[pl.ds(dst0, rows)], sem))

        done = done + take * (1 << k)


def _segment_copies(block, loc_ref, n_ref, dst_ref, local_ref, global_ref, to_global, sem, act):
    def per_expert(e, carry):
        seg = block * N_EXPERTS + e
        if to_global:
            _pow2_copies(local_ref, global_ref, loc_ref[seg], dst_ref[seg], n_ref[seg], SEG_BITS, sem, act)
        else:
            _pow2_copies(global_ref, local_ref, dst_ref[seg], loc_ref[seg], n_ref[seg], SEG_BITS, sem, act)
        return carry

    lax.fori_loop(0, N_EXPERTS, per_expert, 0)


def _zero_fill(zero_ref, global_ref, padoff_ref, padn_ref, sem, act):
    full = 1 << PAD_BITS

    def per_pad(e, carry):
        def per_full(c, inner):
            dst0 = pl.multiple_of((padoff_ref[e] + c * full) * CHUNK_BF16_ROWS, CHUNK_BF16_ROWS)
            act(pltpu.make_async_copy(zero_ref, global_ref.at[pl.ds(dst0, full * CHUNK_BF16_ROWS)], sem))
            return inner

        n_full = padn_ref[e] >> PAD_BITS
        lax.fori_loop(0, n_full, per_full, 0)
        _pow2_copies(zero_ref, global_ref, None, padoff_ref[e] + n_full * full, padn_ref[e] & (full - 1),
                     PAD_BITS, sem, act)
        return carry

    lax.fori_loop(0, N_EXPERTS + 1, per_pad, 0)


STAGE_SLAB = 1024


def _restage(src_sc, dst_sc, dst_dtype):
    ratio = dst_sc.shape[0] / src_sc.shape[0]
    n_slabs = src_sc.shape[0] // (STAGE_SLAB if ratio > 1 else 2 * STAGE_SLAB)
    src_rows = src_sc.shape[0] // n_slabs
    dst_rows = dst_sc.shape[0] // n_slabs

    def slab(i, carry):
        s0 = pl.multiple_of(i * src_rows, src_rows)
        d0 = pl.multiple_of(i * dst_rows, dst_rows)
        dst_sc[pl.ds(d0, dst_rows), :] = pltpu.bitcast(src_sc[pl.ds(s0, src_rows), :], dst_dtype)
        return carry

    lax.fori_loop(0, n_slabs, slab, 0)


def _moe_dispatch_kernel(sb, n_blocks, loc_ref, n_ref, dst_ref, padoff_ref, padn_ref,
                         h2p_ref, a1_ref, a2_ref, xs_hbm, local_sc, stage_sc, zero_sc, sems, zero_sem):
    b = pl.program_id(0)
    slot = b & 1

    def segments(block, buf, act):
        _segment_copies(block, loc_ref, n_ref, dst_ref, stage_sc.at[buf], xs_hbm, True, sems.at[buf], act)

    local_sc[...] = jnp.zeros_like(local_sc)

    def step(g, carry):
        src = pl.multiple_of(g * ROW_TILE, ROW_TILE)
        for u in range(SUBLANES):
            t = g * SUBLANES + u
            row = h2p_ref[_token_rows(src + u), :]
            local_sc[_token_rows(a1_ref[t]), :] = row
            local_sc[_token_rows(a2_ref[t]), :] = row
        return carry

    lax.fori_loop(0, sb // SUBLANES, step, 0)

    @pl.when(b > 0)
    def _():
        segments(b - 1, 1 - slot, lambda c: c.wait())

    _restage(local_sc, stage_sc.at[slot], BF16)
    segments(b, slot, lambda c: c.start())

    @pl.when(b == 0)
    def _():
        zero_sc[...] = jnp.zeros_like(zero_sc)
        _zero_fill(zero_sc, xs_hbm, padoff_ref, padn_ref, zero_sem, lambda c: c.start())
        _zero_fill(zero_sc, xs_hbm, padoff_ref, padn_ref, zero_sem, lambda c: c.wait())

    @pl.when(b == n_blocks - 1)
    def _():
        segments(b, slot, lambda c: c.wait())


def _smem_vec(n, index_map):
    return pl.BlockSpec((n,), index_map, memory_space=pltpu.SMEM)


def _moe_dispatch(h2p, a1, a2, plan, sb, n_tiles):
    n_blocks = h2p.shape[0] // (sb * ROW_PIECES)
    local_flat = _local_rows(sb) * ROW_PIECES
    vec = _smem_vec(sb, lambda s, *_: (s,))
    return pl.pallas_call(
        functools.partial(_moe_dispatch_kernel, sb, n_blocks),
        grid_spec=pltpu.PrefetchScalarGridSpec(
            num_scalar_prefetch=5,
            grid=(n_blocks,),
            in_specs=[pl.BlockSpec((sb * ROW_PIECES, LANES), lambda s, *_: (s, 0)), vec, vec],
            out_specs=pl.BlockSpec(memory_space=pl.ANY),
            scratch_shapes=[pltpu.VMEM((local_flat, LANES), U32),
                            pltpu.VMEM((2, 2 * local_flat, LANES), BF16),
                            pltpu.VMEM(((1 << PAD_BITS) * CHUNK_BF16_ROWS, LANES), BF16),
                            pltpu.SemaphoreType.DMA((2,)),
                            pltpu.SemaphoreType.DMA(())]),
        out_shape=jax.ShapeDtypeStruct((n_tiles * FFN_ROWS * ROW_PIECES * 2, LANES), BF16),
        compiler_params=_params(),
        name="moe_dispatch",
    )(plan["loc"], plan["n"], plan["dst"], plan["pad_off"], plan["pad_n"], h2p, a1, a2)


def _moe_ffn_kernel(owner_ref, ntiles_ref, xs_ref, wg_ref, wu_ref, wd_ref, ys_ref,
                    wg_sc, wu_sc, wd_sc, z32_sc):
    i = pl.program_id(0)
    half = D_MODEL // 2
    used = i < ntiles_ref[0]

    @pl.when(used & ((i == 0) | (owner_ref[i] != owner_ref[jnp.maximum(i - 1, 0)])))
    def _():
        wg_sc[...] = wg_ref[0].astype(BF16)
        wu_sc[...] = wu_ref[0].astype(BF16)
        wd_sc[...] = wd_ref[0].astype(BF16)

    @pl.when(used)
    def _():
        lo, hi = _unpack_rows(z32_sc, _from_row_tiled(xs_ref[...], FFN_ROWS), FFN_ROWS)
        lo, hi = lo.astype(BF16), hi.astype(BF16)
        g = _dot(lo, wg_sc[0:half, :]) + _dot(hi, wg_sc[half:D_MODEL, :])
        u = _dot(lo, wu_sc[0:half, :]) + _dot(hi, wu_sc[half:D_MODEL, :])
        y = _dot((_silu(g) * u).astype(BF16), wd_sc[...])
        ys_ref[...] = _flatten_tiled(_to_row_tiled(_pack_rows(z32_sc, y, FFN_ROWS), FFN_ROWS))

    @pl.when(jnp.logical_not(used))
    def _():
        ys_ref[...] = jnp.zeros_like(ys_ref)


def _moe_ffn(xs, plan, wg, wu, wd):
    flat = FFN_ROWS * ROW_PIECES * 2
    n_tiles = xs.shape[0] // flat
    last_used = lambda i, owner, nt: jnp.minimum(i, nt[0] - 1)
    wspec = lambda shape: pl.BlockSpec((1,) + shape, lambda i, owner, nt: (owner[last_used(i, owner, nt)], 0, 0))
    return pl.pallas_call(
        _moe_ffn_kernel,
        grid_spec=pltpu.PrefetchScalarGridSpec(
            num_scalar_prefetch=2,
            grid=(n_tiles,),
            in_specs=[pl.BlockSpec((flat, LANES), lambda i, owner, nt: (last_used(i, owner, nt), 0)),
                      wspec((D_MODEL, EXPERT_FF)), wspec((D_MODEL, EXPERT_FF)), wspec((EXPERT_FF, D_MODEL))],
            out_specs=pl.BlockSpec((flat, LANES), lambda i, owner, nt: (i, 0)),
            scratch_shapes=[pltpu.VMEM((D_MODEL, EXPERT_FF), BF16),
                            pltpu.VMEM((D_MODEL, EXPERT_FF), BF16),
                            pltpu.VMEM((EXPERT_FF, D_MODEL), BF16),
                            pltpu.VMEM((ROW_PIECES, 2 * FFN_ROWS, LANES), F32)]),
        out_shape=jax.ShapeDtypeStruct(xs.shape, BF16),
        compiler_params=_params(),
        name="moe_ffn",
    )(plan["owner"], plan["n_tiles"], xs, wg, wu, wd)


def _moe_combine_kernel(n_psb, n_blocks, loc_ref, n_ref, dst_ref,
                        ys_hbm, a1_ref, a2_ref, posw_ref, x1_ref, mod_ref, yp_ref, yo_ref,
                        local_sc, stage_sc, g1_sc, g2_sc, z32_sc, sems):
    s = pl.program_id(0)
    slot = s & 1

    def segments(block, buf, act):
        _segment_copies(block, loc_ref, n_ref, dst_ref, stage_sc.at[buf], ys_hbm, False, sems.at[buf], act)

    @pl.when(pl.program_id(1) == 0)
    def _():
        @pl.when(s == 0)
        def _():
            segments(s, slot, lambda c: c.start())

        segments(s, slot, lambda c: c.wait())
        _restage(stage_sc.at[slot], local_sc, U32)

        @pl.when(s + 1 < n_blocks)
        def _():
            segments(s + 1, 1 - slot, lambda c: c.start())

    def step(g, carry):
        dst = pl.multiple_of(g * ROW_TILE, ROW_TILE)
        for u in range(SUBLANES):
            t = g * SUBLANES + u
            g1_sc[_token_rows(dst + u), :] = local_sc[_token_rows(a1_ref[t]), :]
            g2_sc[_token_rows(dst + u), :] = local_sc[_token_rows(a2_ref[t]), :]
        return carry

    lax.fori_loop(0, TOK_TILE // SUBLANES, step, 0)
    halves = lambda g_sc: [pltpu.bitcast(p, BF16) for p in _from_row_tiled(g_sc[...], TOK_TILE)]
    lo1, hi1 = _unpack_rows(z32_sc, halves(g1_sc), TOK_TILE)
    lo2, hi2 = _unpack_rows(z32_sc, halves(g2_sc), TOK_TILE)
    w1, w2 = posw_ref[:, 2:3], posw_ref[:, 3:4]
    moe = jnp.concatenate([w1 * lo1 + w2 * lo2, w1 * hi1 + w2 * hi2], axis=1)
    gate2 = _rows_to_tokens(mod_ref[:, 5 * D_MODEL:6 * D_MODEL], D_MODEL)
    y = x1_ref[...] + gate2 * moe

    @pl.when(s < n_psb)
    def _():
        yp_ref[...] = y

    @pl.when(s >= n_psb)
    def _():
        yo_ref[...] = y


def _moe_combine(ys, a1, a2, posw, plan, x1, mod, sb, n_ptiles, n_stiles, prep):
    tps = sb // TOK_TILE
    n_blocks = (n_ptiles + n_stiles) // tps
    n_psb = n_ptiles // tps
    pblocks = prep // ROWS_PER_TILE
    tile = lambda s, j: s * tps + j
    vec = _smem_vec(TOK_TILE, lambda s, j, *_: (tile(s, j),))
    return pl.pallas_call(
        functools.partial(_moe_combine_kernel, n_psb, n_blocks),
        grid_spec=pltpu.PrefetchScalarGridSpec(
            num_scalar_prefetch=3,
            grid=(n_blocks, tps),
            in_specs=[pl.BlockSpec(memory_space=pl.ANY), vec, vec,
                      pl.BlockSpec((TOK_TILE, LANES), lambda s, j, *_: (tile(s, j), 0)),
                      pl.BlockSpec((TOK_TILE, D_MODEL), lambda s, j, *_: (tile(s, j), 0)),
                      pl.BlockSpec((ROWS_PER_TILE, 6 * D_MODEL),
                                   lambda s, j, *_: (jnp.maximum(tile(s, j) - n_ptiles + pblocks, 0), 0))],
            out_specs=[pl.BlockSpec((TOK_TILE, D_MODEL),
                                    lambda s, j, *_: (jnp.minimum(tile(s, j), n_ptiles - 1), 0)),
                       pl.BlockSpec((TOK_TILE, D_MODEL),
                                    lambda s, j, *_: (jnp.maximum(tile(s, j) - n_ptiles, 0), 0))],
            scratch_shapes=[pltpu.VMEM((_local_rows(sb) * ROW_PIECES, LANES), U32),
                            pltpu.VMEM((2, _local_rows(sb) * ROW_PIECES * 2, LANES), BF16),
                            pltpu.VMEM((TOK_TILE * ROW_PIECES, LANES), U32),
                            pltpu.VMEM((TOK_TILE * ROW_PIECES, LANES), U32),
                            pltpu.VMEM((ROW_PIECES, 2 * TOK_TILE, LANES), F32),
                            pltpu.SemaphoreType.DMA((2,))]),
        out_shape=[jax.ShapeDtypeStruct((n_ptiles * TOK_TILE, D_MODEL), F32),
                   jax.ShapeDtypeStruct((n_stiles * TOK_TILE, D_MODEL), F32)],
        compiler_params=_params(2),
        name="moe_combine",
    )(plan["loc"], plan["n"], plan["dst"], ys, a1, a2, posw, x1, mod)


def _layer(xp, xs, cache_k, cache_v, state, c_prompt, c_sample, norm_mix_g, norm_ffn_g, w_ada, b_ada, w_in,
           q_norm_g, k_norm_g, rel_bias, w_gate_up, b_gate, gla_norm_g, w_out, w_route_group,
           b_route_group, w_route_expert, b_route_expert, w_exp_gate, w_exp_up, w_exp_down):
    batch, seq, _ = xp.shape
    n_seq, dec_seq, _ = xs.shape
    assert batch == 1 and dec_seq == CHUNK and cache_k.shape[1] == BAND_PAST
    assert seq % TOK_TILE == 0 and seq >= BAND_PAST and (n_seq * CHUNK) % TOK_TILE == 0
    assert seq % (ATTN_SUB * Q_ROWS) == 0 and seq % (GLA_SUB * GLA_CHUNKS * CHUNK) == 0
    n_ptok, n_stok = seq, n_seq * CHUNK
    n_ptiles, n_stiles = n_ptok // TOK_TILE, n_stok // TOK_TILE
    sb = MOE_SUPER_BLOCK if (n_ptok % MOE_SUPER_BLOCK == 0 and n_stok % MOE_SUPER_BLOCK == 0) else TOK_TILE
    prep = ROWS_PER_TILE

    xp2 = xp.reshape(n_ptok, D_MODEL)
    xs2 = xs.reshape(n_stok, D_MODEL)
    c_rows = jnp.concatenate([jnp.broadcast_to(c_prompt, (prep, D_MODEL)), c_sample], axis=0)
    mod = _adaln(c_rows, w_ada, b_ada)

    w_main = w_in[:, 0:IN_MAIN].astype(BF16)
    w_gr = jnp.pad(w_in[:, IN_MAIN:], ((0, 0), (0, LANES - GATE_RANK))).astype(BF16)
    wgu_p = jnp.pad(w_gate_up, ((0, LANES - GATE_RANK), (0, 0))).astype(BF16)
    head = np.arange(A_WIDTH) // A_HEAD_DIM
    bd = jnp.asarray(head[:, None] == head[None, :], BF16)
    gq = jnp.tile(q_norm_g, A_HEADS).reshape(1, A_WIDTH)
    gk = jnp.tile(k_norm_g, A_HEADS).reshape(1, A_WIDTH)
    q, k, v, kf, vf, gla, la = _inproj(
        xp2, xs2, mod, norm_mix_g.reshape(1, D_MODEL), w_main, w_gr, bd, gq, gk, wgu_p,
        b_gate.reshape(1, B_KWIDTH), n_ptiles, n_stiles, prep)

    first_chunk = n_ptok // CHUNK
    oa_p = _attn_prompt(rel_bias[:, _bias_lanes(ATTN_WIN * Q_ROWS)], q, k, v, n_ptok // (ATTN_SUB * Q_ROWS))
    oa_s = _attn_sample(rel_bias[:, _bias_lanes(SAMPLE_KEYS)], q, k, v,
                        cache_k.reshape(n_seq, BAND_PAST, A_WIDTH), cache_v.reshape(n_seq, BAND_PAST, A_WIDTH),
                        first_chunk, n_seq)
    g_gla = gla_norm_g.reshape(1, B_DV)
    ob_p, sfin_p = _gla_prompt(gla, la, g_gla, n_ptok // (GLA_SUB * GLA_CHUNKS * CHUNK))
    ob_s, sfin_s = _gla_sample(gla, la, g_gla, _state_to_pairs(state), first_chunk, n_seq)

    wr = jnp.pad(jnp.concatenate([w_route_group, w_route_expert], axis=1),
                 ((0, 0), (0, LANES - N_GROUPS - N_EXPERTS)))
    br = jnp.pad(jnp.concatenate([b_route_group, b_route_expert]), (0, LANES - N_GROUPS - N_EXPERTS))
    x1, h2p, meta, cnt = _outproj(oa_p, oa_s, ob_p, ob_s, w_out.astype(BF16), xp2, xs2, mod,
                                  norm_ffn_g.reshape(1, D_MODEL), wr, br.reshape(1, LANES),
                                  n_ptiles, n_stiles, prep, sb)

    posw, a1, a2, plan = _moe_plan(meta, cnt, sb)
    xs_sorted = _moe_dispatch(h2p.reshape(-1, LANES), a1, a2, plan, sb, _sorted_tiles(n_ptok + n_stok, sb))
    ys_sorted = _moe_ffn(xs_sorted, plan, w_exp_gate, w_exp_up, w_exp_down)
    yp, ys = _moe_combine(ys_sorted, a1, a2, posw, plan, x1, mod, sb, n_ptiles, n_stiles, prep)

    tail = min(BAND_PAST, seq)
    heads = (A_HEADS, A_HEAD_DIM)
    return (yp.reshape(1, seq, D_MODEL), ys.reshape(n_seq, CHUNK, D_MODEL),
            kf[TOK_TILE - tail:TOK_TILE].reshape((1, tail) + heads),
            vf[TOK_TILE - tail:TOK_TILE].reshape((1, tail) + heads),
            _pairs_to_state(sfin_p)[None],
            kf[TOK_TILE:].reshape((n_seq, CHUNK) + heads),
            vf[TOK_TILE:].reshape((n_seq, CHUNK) + heads),
            _pairs_to_state(sfin_s))


def kernel(x_prompt, x_sample, cache_a_k, cache_a_v, state_gla, c_prompt, c_sample, norm_mix_g, norm_ffn_g,
           w_ada, b_ada, w_in, q_norm_g, k_norm_g, rel_bias, w_gate_up, b_gate, gla_norm_g, w_out,
           w_route_group, b_route_group, w_route_expert, b_route_expert, w_exp_gate, w_exp_up, w_exp_down):
    depth = w_in.shape[0]
    yp, ys = x_prompt, x_sample
    outs = [[] for _ in range(6)]
    for l in range(depth):
        yp, ys, kp, vp, sp, ks, vs, ss = _layer(
            yp, ys, cache_a_k[l], cache_a_v[l], state_gla[l], c_prompt, c_sample, norm_mix_g[l], norm_ffn_g[l],
            w_ada[l], b_ada[l], w_in[l], q_norm_g[l], k_norm_g[l], rel_bias[l], w_gate_up[l], b_gate[l],
            gla_norm_g[l], w_out[l], w_route_group[l], b_route_group[l], w_route_expert[l], b_route_expert[l],
            w_exp_gate[l], w_exp_up[l], w_exp_down[l])
        for lst, val in zip(outs, (kp, vp, sp, ks, vs, ss)):
            lst.append(val)
    return (yp, ys) + tuple(jnp.stack(o) for o in outs)
```

```python
import functools

import numpy as np
import jax
import jax.numpy as jnp
from jax import lax
from jax.experimental import pallas as pl
from jax.experimental.pallas import tpu as pltpu

F32 = jnp.float32
BF16 = jnp.bfloat16
U32 = jnp.uint32

D_MODEL = 1024
CHUNK = 64
LOG_CHUNK = 6
BAND_CHUNKS = 8
BAND_PAST = BAND_CHUNKS * CHUNK
A_WIDTH = 512
A_HEADS = 8
A_HEAD_DIM = 64
MAX_REL = 128
N_REL = CHUNK + MAX_REL
B_WIDTH = 512
B_HEADS = 4
B_DV = 128
B_DK = 64
B_KWIDTH = 256
GATE_RANK = 16
GATE_TAU = 16.0
N_GROUPS = 4
EXPERTS_PER_GROUP = 8
N_EXPERTS = 32
EXPERT_FF = 256
EPS = 1e-6

LANES = 128
IN_MAIN = 3 * A_WIDTH + 2 * B_KWIDTH + 2 * B_WIDTH
TOK_TILE = 512
ROWS_PER_TILE = TOK_TILE // CHUNK
OUT_SUB = 1
Q_CHUNKS = 4
Q_ROWS = Q_CHUNKS * CHUNK
ROLL_W = 1024
NEG = -1e30
ROUTE_OFF = N_GROUPS
VMEM_LIMIT = 56 * 1024 * 1024


def _params(n_axes=1):
    return pltpu.CompilerParams(dimension_semantics=("arbitrary",) * n_axes,
                                vmem_limit_bytes=VMEM_LIMIT)


def _split(a):
    hi = a.astype(BF16)
    lo = (a - hi.astype(F32)).astype(BF16)
    return hi, lo


def _dot(a, b):
    return jnp.dot(a, b, preferred_element_type=F32)


def _dot3(a, b):
    ah, al = _split(a)
    bh, bl = _split(b)
    return _dot(ah, bh) + _dot(al, bh) + _dot(ah, bl)


def _dot_nt(a, b):
    return lax.dot_general(a, b, (((1,), (1,)), ((), ())), preferred_element_type=F32)


def _dot_tn(a, b):
    return lax.dot_general(a, b, (((0,), (0,)), ((), ())), preferred_element_type=F32)


def _silu(x):
    return x / (1.0 + jnp.exp(-x))


def _rows_to_tokens(rows, n):
    r = rows.shape[0]
    return jnp.broadcast_to(rows[:, None, :], (r, CHUNK, n)).reshape(r * CHUNK, n)


def _adaln_kernel(c_ref, w_ref, b_ref, o_ref):
    a = _silu(c_ref[...])
    o_ref[...] = _dot3(a, w_ref[...]) + b_ref[...]


def _adaln(c_rows, w_ada, b_ada):
    r = c_rows.shape[0]
    n = w_ada.shape[1]
    tn = 1024
    return pl.pallas_call(
        _adaln_kernel,
        grid=(n // tn,),
        in_specs=[pl.BlockSpec((r, D_MODEL), lambda j: (0, 0)),
                  pl.BlockSpec((D_MODEL, tn), lambda j: (0, j)),
                  pl.BlockSpec((1, tn), lambda j: (0, j))],
        out_specs=pl.BlockSpec((r, tn), lambda j: (0, j)),
        out_shape=jax.ShapeDtypeStruct((r, n), F32),
        compiler_params=_params(),
        name="adaln",
    )(c_rows, w_ada, b_ada.reshape(1, n))


def _head_rms(z, bd_ref, g):
    ms = _dot((z * z).astype(BF16), bd_ref[...]) * (1.0 / A_HEAD_DIM)
    return z * lax.rsqrt(ms + EPS) * g


def _inproj_kernel(n_ptiles, xp_ref, xs_ref, mod_ref, gmix_ref, w_ref, wgr_ref, bd_ref, gq_ref, gk_ref,
                   wgu_ref, bg_ref,
                   q_ref, k_ref, v_ref, kf_ref, vf_ref, gla_ref, la_ref):
    i = pl.program_id(0)
    x = jnp.where(i < n_ptiles, xp_ref[...], xs_ref[...])
    ms = jnp.mean(x * x, axis=-1, keepdims=True)
    xn = x * lax.rsqrt(ms + EPS) * gmix_ref[...]
    sh = _rows_to_tokens(mod_ref[:, 0:D_MODEL], D_MODEL)
    sc = _rows_to_tokens(mod_ref[:, D_MODEL:2 * D_MODEL], D_MODEL)
    hb = (xn * (1.0 + sc) + sh).astype(BF16)

    zq = _dot(hb, w_ref[:, 0:A_WIDTH])
    q_ref[...] = (_head_rms(zq, bd_ref, gq_ref[...]) * (LOG2E * A_HEAD_DIM ** -0.5)).astype(BF16)
    zk = _dot(hb, w_ref[:, A_WIDTH:2 * A_WIDTH])
    kn = _head_rms(zk, bd_ref, gk_ref[...])
    k_ref[...] = kn.astype(BF16)
    kf_ref[...] = kn
    zv = _dot(hb, w_ref[:, 2 * A_WIDTH:3 * A_WIDTH])
    v_ref[...] = zv.astype(BF16)
    vf_ref[...] = zv

    o = 3 * A_WIDTH
    zqb = _dot(hb, w_ref[:, o:o + B_KWIDTH]) * (B_DK ** -0.5)
    gla_ref[:, 0:B_KWIDTH] = zqb.astype(BF16)
    for c in range(B_KWIDTH, 2 * B_KWIDTH + 2 * B_WIDTH, 256):
        gla_ref[:, c:c + 256] = _dot(hb, w_ref[:, o + c:o + c + 256]).astype(BF16)

    gr = _dot(hb, wgr_ref[...])
    logit = _dot(gr.astype(BF16), wgu_ref[...]) + bg_ref[...]
    log_sig = jnp.minimum(logit, 0.0) - jnp.log1p(jnp.exp(-jnp.abs(logit)))
    la_ref[...] = log_sig * (1.0 / GATE_TAU)


def _inproj(xp, xs, mod, gmix, w_main, w_gr, bd, gq, gk, wgu_p, bg, n_ptiles, n_stiles, prep):
    n_tiles = n_ptiles + n_stiles
    t = n_tiles * TOK_TILE
    tail_tiles = 1 + n_stiles
    pblocks = prep // ROWS_PER_TILE
    const = lambda i: (0, 0)
    row = lambda i: (i, 0)
    tail = lambda i: (jnp.maximum(i - (n_ptiles - 1), 0), 0)
    return pl.pallas_call(
        functools.partial(_inproj_kernel, n_ptiles),
        grid=(n_tiles,),
        in_specs=[pl.BlockSpec((TOK_TILE, D_MODEL), lambda i: (jnp.minimum(i, n_ptiles - 1), 0)),
                  pl.BlockSpec((TOK_TILE, D_MODEL), lambda i: (jnp.maximum(i - n_ptiles, 0), 0)),
                  pl.BlockSpec((ROWS_PER_TILE, 6 * D_MODEL),
                               lambda i: (jnp.maximum(i - n_ptiles + pblocks, 0), 0)),
                  pl.BlockSpec((1, D_MODEL), const),
                  pl.BlockSpec((D_MODEL, IN_MAIN), const),
                  pl.BlockSpec((D_MODEL, LANES), const),
                  pl.BlockSpec((A_WIDTH, A_WIDTH), const),
                  pl.BlockSpec((1, A_WIDTH), const),
                  pl.BlockSpec((1, A_WIDTH), const),
                  pl.BlockSpec((LANES, B_KWIDTH), const),
                  pl.BlockSpec((1, B_KWIDTH), const)],
        out_specs=[pl.BlockSpec((TOK_TILE, A_WIDTH), row),
                   pl.BlockSpec((TOK_TILE, A_WIDTH), row),
                   pl.BlockSpec((TOK_TILE, A_WIDTH), row),
                   pl.BlockSpec((TOK_TILE, A_WIDTH), tail),
                   pl.BlockSpec((TOK_TILE, A_WIDTH), tail),
                   pl.BlockSpec((TOK_TILE, 2 * B_KWIDTH + 2 * B_WIDTH), row),
                   pl.BlockSpec((TOK_TILE, B_KWIDTH), row)],
        out_shape=[jax.ShapeDtypeStruct((t, A_WIDTH), BF16),
                   jax.ShapeDtypeStruct((t, A_WIDTH), BF16),
                   jax.ShapeDtypeStruct((t, A_WIDTH), BF16),
                   jax.ShapeDtypeStruct((tail_tiles * TOK_TILE, A_WIDTH), F32),
                   jax.ShapeDtypeStruct((tail_tiles * TOK_TILE, A_WIDTH), F32),
                   jax.ShapeDtypeStruct((t, 2 * B_KWIDTH + 2 * B_WIDTH), BF16),
                   jax.ShapeDtypeStruct((t, B_KWIDTH), F32)],
        compiler_params=_params(),
        name="inproj",
    )(xp, xs, mod, gmix, w_main, w_gr, bd, gq, gk, wgu_p, bg)


def _bias_lanes(n_keys):
    l = np.arange(ROLL_W)
    d = np.where(l < n_keys, BAND_PAST - l, BAND_PAST - l + ROLL_W)
    return np.clip(d, -(CHUNK - 1), MAX_REL) + (CHUNK - 1)


LOG2E = 1.4426950408889634


def _band_mask(m_rows, n_keys, first_col):
    qi = lax.broadcasted_iota(jnp.int32, (m_rows, n_keys), 0) >> LOG_CHUNK
    kw = lax.broadcasted_iota(jnp.int32, (m_rows, n_keys), 1)
    kc = kw >> LOG_CHUNK
    return (kc >= qi) & (kc <= qi + BAND_CHUNKS) & (kw >= first_col)


def _bias_tile(u_ref, h, ok):
    m_rows, n_keys = ok.shape
    src = jnp.broadcast_to(u_ref[h:h + 1, :] * LOG2E, (m_rows, ROLL_W))
    toe = pltpu.roll(src, 0, 1, stride=1, stride_axis=0)
    return jnp.where(ok, toe[:, 0:n_keys], NEG)


def _attend(q, kcat, vcat, bias_sc):
    m_rows = q.shape[0]
    first = lax.broadcasted_iota(jnp.int32, (m_rows, LANES), 1) < A_HEAD_DIM
    outs = []
    for p in range(A_HEADS // 2):
        lanes = slice(p * LANES, (p + 1) * LANES)
        qp, kp, vp = q[:, lanes], kcat[:, lanes], vcat[:, lanes]
        zero = jnp.zeros_like(qp)
        q2 = jnp.concatenate([jnp.where(first, qp, zero), jnp.where(first, zero, qp)], axis=0)
        s = _dot_nt(q2, kp) + bias_sc[p]
        e = jnp.exp2(s - jnp.max(s, axis=-1, keepdims=True))
        l = jnp.sum(e, axis=-1, keepdims=True)
        o2 = _dot(e.astype(BF16), vp) / l
        outs.append(jnp.where(first, o2[0:m_rows], o2[m_rows:2 * m_rows]))
    return jnp.concatenate(outs, axis=-1)


ATTN_SUB = 4
ATTN_WIN = 3


def _attn_prompt_kernel(u_ref, q_ref, *refs):
    k_refs = refs[0:ATTN_SUB + ATTN_WIN - 1]
    v_refs = refs[ATTN_SUB + ATTN_WIN - 1:2 * (ATTN_SUB + ATTN_WIN - 1)]
    o_ref, bias_sc = refs[-2:]
    j = pl.program_id(0)
    n_keys = ATTN_WIN * Q_ROWS

    @pl.when(j == 0)
    def _():
        for g in range(ATTN_WIN):
            ok = _band_mask(Q_ROWS, n_keys, (ATTN_WIN - 1 - g) * Q_ROWS)
            for h in range(A_HEADS):
                bias_sc[g, h // 2, (h % 2) * Q_ROWS:(h % 2 + 1) * Q_ROWS, :] = _bias_tile(u_ref, h, ok)

    ks = [r[...] for r in k_refs]
    vs = [r[...] for r in v_refs]
    for sub in range(ATTN_SUB):
        rows = slice(sub * Q_ROWS, (sub + 1) * Q_ROWS)
        kcat = jnp.concatenate(ks[sub:sub + ATTN_WIN], axis=0)
        vcat = jnp.concatenate(vs[sub:sub + ATTN_WIN], axis=0)
        bias = bias_sc.at[jnp.minimum(ATTN_SUB * j + sub, ATTN_WIN - 1)]
        o_ref[rows, :] = _attend(q_ref[rows, :], kcat, vcat, bias).astype(BF16)


def _attn_prompt(u, q, k, v, n_steps):
    const = lambda j: (0, 0)
    n_blk = ATTN_SUB + ATTN_WIN - 1
    blk = lambda d: pl.BlockSpec((Q_ROWS, A_WIDTH),
                                 lambda j, d=d: (jnp.maximum(ATTN_SUB * j - (ATTN_WIN - 1) + d, 0), 0))
    step_rows = ATTN_SUB * Q_ROWS
    return pl.pallas_call(
        _attn_prompt_kernel,
        grid=(n_steps,),
        in_specs=[pl.BlockSpec((A_HEADS, ROLL_W), const), pl.BlockSpec((step_rows, A_WIDTH), lambda j: (j, 0))]
                 + [blk(d) for d in range(n_blk)] * 2,
        out_specs=pl.BlockSpec((step_rows, A_WIDTH), lambda j: (j, 0)),
        out_shape=jax.ShapeDtypeStruct((n_steps * step_rows, A_WIDTH), BF16),
        scratch_shapes=[pltpu.VMEM((ATTN_WIN, A_HEADS // 2, 2 * Q_ROWS, ATTN_WIN * Q_ROWS), F32)],
        compiler_params=_params(),
        name="attn_prompt",
    )(u, q, *([k] * n_blk), *([v] * n_blk))


SAMPLE_KEYS = BAND_PAST + 2 * CHUNK
SAMPLE_STREAMS = 4


def _attn_sample_kernel(u_ref, q_ref, kn_ref, vn_ref, kc_ref, vc_ref, o_ref, bias_sc):
    @pl.when(pl.program_id(0) == 0)
    def _():
        ok = _band_mask(CHUNK, SAMPLE_KEYS, 0)
        for p in range(A_HEADS // 2):
            pair = jnp.concatenate([_bias_tile(u_ref, 2 * p, ok), _bias_tile(u_ref, 2 * p + 1, ok)], axis=0)
            bias_sc[p] = pair.T

    pad = jnp.zeros((CHUNK, A_WIDTH), BF16)
    lane = lax.broadcasted_iota(jnp.int32, (CHUNK, LANES), 1)
    first = lane < A_HEAD_DIM
    zero = jnp.zeros((CHUNK, LANES), BF16)
    for n in range(SAMPLE_STREAMS):
        rows = slice(n * CHUNK, (n + 1) * CHUNK)
        kcat = jnp.concatenate([kc_ref[n].astype(BF16), kn_ref[rows, :], pad], axis=0)
        vcat = jnp.concatenate([vc_ref[n].astype(BF16), vn_ref[rows, :], pad], axis=0)
        q = q_ref[rows, :]
        outs = []
        for p in range(A_HEADS // 2):
            lanes = slice(p * LANES, (p + 1) * LANES)
            qp = q[:, lanes]
            q_rows = jnp.concatenate([jnp.where(first, qp, zero), jnp.where(first, zero, qp)], axis=0)
            s = _dot_nt(kcat[:, lanes], q_rows) + bias_sc[p]
            e = jnp.exp2(s - jnp.max(s, axis=0, keepdims=True))
            pn = (e * (1.0 / jnp.sum(e, axis=0, keepdims=True))).astype(BF16)
            r = _dot_tn(pn, vcat[:, lanes])
            outs.append(jnp.where(first, r[0:CHUNK], r[CHUNK:2 * CHUNK]))
        o_ref[rows, :] = jnp.concatenate(outs, axis=-1).astype(BF16)


def _attn_sample(u, q, k, v, kc, vc, first_chunk, n_seq):
    assert n_seq % SAMPLE_STREAMS == 0 and first_chunk % SAMPLE_STREAMS == 0
    rows = SAMPLE_STREAMS * CHUNK
    new = pl.BlockSpec((rows, A_WIDTH), lambda b: (first_chunk // SAMPLE_STREAMS + b, 0))
    cache = pl.BlockSpec((SAMPLE_STREAMS, BAND_PAST, A_WIDTH), lambda b: (b, 0, 0))
    return pl.pallas_call(
        _attn_sample_kernel,
        grid=(n_seq // SAMPLE_STREAMS,),
        in_specs=[pl.BlockSpec((A_HEADS, ROLL_W), lambda b: (0, 0)), new, new, new, cache, cache],
        out_specs=pl.BlockSpec((rows, A_WIDTH), lambda b: (b, 0)),
        out_shape=jax.ShapeDtypeStruct((n_seq * CHUNK, A_WIDTH), BF16),
        scratch_shapes=[pltpu.VMEM((A_HEADS // 2, SAMPLE_KEYS, LANES), F32)],
        compiler_params=_params(),
        name="attn_sample",
    )(u, q, k, v, kc, vc)


GLA_CHUNKS = 4
GLA_SUB = 4


def _gla_block(n_chunks, gla_ref, la_ref, ltri_ref, g_ref, st_sc, o_ref):
    rows = n_chunks * CHUNK
    la = la_ref[...]
    la_hi, la_lo = _split(la)
    b = _dot(ltri_ref[...], la_hi) + _dot(ltri_ref[...], la_lo)
    b3 = b.reshape(n_chunks, CHUNK, B_KWIDTH)
    b_mid = b3[:, CHUNK // 2 - 1:CHUNK // 2, :]
    b_last = b3[:, CHUNK - 1:CHUNK, :]
    q = gla_ref[:, 0:B_KWIDTH].astype(F32).reshape(n_chunks, CHUNK, B_KWIDTH)
    k = gla_ref[:, B_KWIDTH:2 * B_KWIDTH].astype(F32).reshape(n_chunks, CHUNK, B_KWIDTH)
    q_start = (q * jnp.exp(b3)).reshape(rows, B_KWIDTH).astype(BF16)
    q_mid = (q * jnp.exp(b3 - b_mid)).reshape(rows, B_KWIDTH).astype(BF16)
    k_mid = (k * jnp.exp(b_mid - b3)).reshape(rows, B_KWIDTH).astype(BF16)
    k_end = (k * jnp.exp(b_last - b3)).reshape(rows, B_KWIDTH).astype(BF16)
    dec = jnp.exp(b_last)

    ti = lax.broadcasted_iota(jnp.int32, (2 * rows, rows), 0) & (rows - 1)
    si = lax.broadcasted_iota(jnp.int32, (2 * rows, rows), 1)
    causal = (si <= ti) & ((si >> LOG_CHUNK) == (ti >> LOG_CHUNK))
    first_r = lax.broadcasted_iota(jnp.int32, (rows, LANES), 1) < B_DK
    first_c = lax.broadcasted_iota(jnp.int32, (CHUNK, LANES), 1) < B_DK
    first_s = lax.broadcasted_iota(jnp.int32, (B_DV, LANES), 1) < B_DK

    def stack_heads(x, first):
        zero = jnp.zeros_like(x)
        return jnp.concatenate([jnp.where(first, x, zero), jnp.where(first, zero, x)], axis=0)

    for p in range(B_HEADS // 2):
        lanes = slice(p * LANES, (p + 1) * LANES)
        qs_p, qm_p, km_p, ke_p = q_start[:, lanes], q_mid[:, lanes], k_mid[:, lanes], k_end[:, lanes]
        v_pair = gla_ref[:, 2 * B_KWIDTH + 2 * p * B_DV:2 * B_KWIDTH + (2 * p + 2) * B_DV]
        sc = jnp.where(causal, _dot_nt(stack_heads(qm_p, first_r), km_p), 0.0)
        o2 = _dot(sc.astype(BF16), v_pair)
        intra = [o2[0:rows, 0:B_DV], o2[rows:2 * rows, B_DV:2 * B_DV]]
        inter = [[], []]
        st = st_sc[p]
        for c in range(n_chunks):
            cr = slice(c * CHUNK, (c + 1) * CHUNK)
            r2 = _dot_nt(stack_heads(qs_p[cr], first_c), st.astype(BF16))
            inter[0].append(r2[0:CHUNK])
            inter[1].append(r2[CHUNK:2 * CHUNK])
            u2 = _dot_tn(v_pair[cr], ke_p[cr])
            st = st * dec[c, :, lanes] + jnp.where(first_s, u2[0:B_DV], u2[B_DV:2 * B_DV])
        st_sc[p] = st
        for hh in range(2):
            h = 2 * p + hh
            o = intra[hh] + jnp.concatenate(inter[hh], axis=0)
            ms = jnp.mean(o * o, axis=-1, keepdims=True)
            on = o * lax.rsqrt(ms + EPS) * g_ref[...]
            r = gla_ref[:, 2 * B_KWIDTH + B_WIDTH + h * B_DV:2 * B_KWIDTH + B_WIDTH + (h + 1) * B_DV]
            o_ref[:, h * B_DV:(h + 1) * B_DV] = (on * _silu(r.astype(F32))).astype(BF16)


def _gla_prompt_kernel(gla_ref, la_ref, ltri_ref, g_ref, o_ref, sfin_ref, st_sc):
    @pl.when(pl.program_id(0) == 0)
    def _():
        st_sc[...] = jnp.zeros_like(st_sc)

    rows = GLA_CHUNKS * CHUNK
    for sub in range(GLA_SUB):
        part = pl.ds(sub * rows, rows)
        _gla_block(GLA_CHUNKS, gla_ref.at[part], la_ref.at[part], ltri_ref, g_ref, st_sc, o_ref.at[part])
    sfin_ref[...] = st_sc[...]


def _gla_sample_kernel(gla_ref, la_ref, ltri_ref, g_ref, s0_ref, o_ref, sfin_ref, st_sc):
    st_sc[...] = s0_ref[...]
    for n in range(SAMPLE_STREAMS):
        part = pl.ds(n * CHUNK, CHUNK)
        _gla_block(1, gla_ref.at[part], la_ref.at[part], ltri_ref, g_ref, st_sc.at[n], o_ref.at[part])
    sfin_ref[...] = st_sc[...]


def _ltri(n_chunks):
    r = np.arange(n_chunks * CHUNK)
    m = (r[None, :] <= r[:, None]) & (r[None, :] // CHUNK == r[:, None] // CHUNK)
    return jnp.asarray(m, BF16)


_GLA_W = 2 * B_KWIDTH + 2 * B_WIDTH
_ST_SHAPE = (B_HEADS // 2, B_DV, LANES)


def _gla_prompt(gla, la, g, n_steps):
    rows = GLA_SUB * GLA_CHUNKS * CHUNK
    const = lambda j: (0, 0)
    return pl.pallas_call(
        _gla_prompt_kernel,
        grid=(n_steps,),
        in_specs=[pl.BlockSpec((rows, _GLA_W), lambda j: (j, 0)),
                  pl.BlockSpec((rows, B_KWIDTH), lambda j: (j, 0)),
                  pl.BlockSpec((GLA_CHUNKS * CHUNK, GLA_CHUNKS * CHUNK), const),
                  pl.BlockSpec((1, B_DV), const)],
        out_specs=[pl.BlockSpec((rows, B_WIDTH), lambda j: (j, 0)),
                   pl.BlockSpec(_ST_SHAPE, lambda j: (0, 0, 0))],
        out_shape=[jax.ShapeDtypeStruct((n_steps * rows, B_WIDTH), BF16),
                   jax.ShapeDtypeStruct(_ST_SHAPE, F32)],
        scratch_shapes=[pltpu.VMEM(_ST_SHAPE, F32)],
        compiler_params=_params(),
        name="gla_prompt",
    )(gla, la, _ltri(GLA_CHUNKS), g)


def _gla_sample(gla, la, g, s0, first_chunk, n_seq):
    assert n_seq % SAMPLE_STREAMS == 0 and first_chunk % SAMPLE_STREAMS == 0
    const = lambda b: (0, 0)
    rows = SAMPLE_STREAMS * CHUNK
    first = first_chunk // SAMPLE_STREAMS
    st_spec = pl.BlockSpec((SAMPLE_STREAMS,) + _ST_SHAPE, lambda b: (b, 0, 0, 0))
    return pl.pallas_call(
        _gla_sample_kernel,
        grid=(n_seq // SAMPLE_STREAMS,),
        in_specs=[pl.BlockSpec((rows, _GLA_W), lambda b: (first + b, 0)),
                  pl.BlockSpec((rows, B_KWIDTH), lambda b: (first + b, 0)),
                  pl.BlockSpec((CHUNK, CHUNK), const),
                  pl.BlockSpec((1, B_DV), const),
                  st_spec],
        out_specs=[pl.BlockSpec((rows, B_WIDTH), lambda b: (b, 0)), st_spec],
        out_shape=[jax.ShapeDtypeStruct((n_seq * CHUNK, B_WIDTH), BF16),
                   jax.ShapeDtypeStruct((n_seq,) + _ST_SHAPE, F32)],
        scratch_shapes=[pltpu.VMEM((SAMPLE_STREAMS,) + _ST_SHAPE, F32)],
        compiler_params=_params(),
        name="gla_sample",
    )(gla, la, _ltri(1), g, s0)


def _state_to_pairs(s):
    lead = s.shape[:-3]
    s = s.reshape(lead + (B_HEADS // 2, 2, B_DK, B_DV))
    s = jnp.moveaxis(s, -1, -3)
    return s.reshape(lead + (B_HEADS // 2, B_DV, 2 * B_DK))


def _pairs_to_state(s):
    lead = s.shape[:-3]
    s = s.reshape(lead + (B_HEADS // 2, B_DV, 2, B_DK))
    s = jnp.moveaxis(s, -3, -1)
    return s.reshape(lead + (B_HEADS, B_DK, B_DV))


def _route(logits):
    lane = lax.broadcasted_iota(jnp.int32, logits.shape, 1)
    lane_f = lane.astype(F32)
    big = float(LANES)
    gmask = lane < N_GROUPS
    gl = jnp.where(gmask, logits, NEG)
    gmax = jnp.max(gl, axis=-1, keepdims=True)
    gsel = jnp.min(jnp.where(gl == gmax, lane_f, big), axis=-1, keepdims=True)
    gsum = jnp.sum(jnp.where(gmask, jnp.exp(gl - gmax), 0.0), axis=-1, keepdims=True)
    g_w = 1.0 / gsum
    e_lo = ROUTE_OFF + gsel * EXPERTS_PER_GROUP
    emask = (lane_f >= e_lo) & (lane_f < e_lo + EXPERTS_PER_GROUP)
    el = jnp.where(emask, logits, NEG)
    v1 = jnp.max(el, axis=-1, keepdims=True)
    i1 = jnp.min(jnp.where(el == v1, lane_f, big), axis=-1, keepdims=True)
    el2 = jnp.where(lane_f == i1, NEG, el)
    v2 = jnp.max(el2, axis=-1, keepdims=True)
    i2 = jnp.min(jnp.where(el2 == v2, lane_f, big), axis=-1, keepdims=True)
    t = jnp.exp(v2 - v1)
    w1 = g_w / (1.0 + t)
    w2 = g_w * t / (1.0 + t)
    return lane_f, i1, i2, w1, w2


ROW_PIECES = D_MODEL // 2 // LANES
SUBLANES = 8
ROW_TILE = ROW_PIECES * SUBLANES


def _pack_rows(z32_sc, x, rows):
    half = D_MODEL // 2
    out = []
    for c in range(ROW_PIECES):
        z32_sc[c, pl.ds(0, rows, stride=2), :] = x[:, c * LANES:(c + 1) * LANES]
        z32_sc[c, pl.ds(1, rows, stride=2), :] = x[:, half + c * LANES:half + (c + 1) * LANES]
        out.append(z32_sc[c].astype(BF16))
    return out


def _unpack_rows(z32_sc, pieces, rows):
    lo, hi = [], []
    for c in range(ROW_PIECES):
        z32_sc[c] = pieces[c].astype(F32)
        lo.append(z32_sc[c, pl.ds(0, rows, stride=2), :])
        hi.append(z32_sc[c, pl.ds(1, rows, stride=2), :])
    return jnp.concatenate(lo, axis=1), jnp.concatenate(hi, axis=1)


def _to_row_tiled(pieces, tokens):
    per_tile = pieces[0].shape[0] * SUBLANES // tokens
    return jnp.stack([p.reshape(tokens // SUBLANES, per_tile, LANES) for p in pieces], axis=1)


def _from_row_tiled(flat, tokens):
    per_tile = flat.shape[0] // (tokens // SUBLANES) // ROW_PIECES
    tiled = flat.reshape(tokens // SUBLANES, ROW_PIECES, per_tile, LANES)
    return [tiled[:, c].reshape(tokens // SUBLANES * per_tile, LANES) for c in range(ROW_PIECES)]


def _flatten_tiled(tiled):
    return tiled.reshape(-1, LANES)


def _outproj_kernel(n_ptiles, tiles_per_sb, oap_ref, oas_ref, obp_ref, obs_ref, wo_ref, xp_ref, xs_ref, mod_ref,
                    gffn_ref, wr_ref, br_ref, ltri_ref, x1_ref, h2p_ref, meta_ref, cnt_ref, z32_sc, cnt_sc):
    j = pl.program_id(0)
    is_prompt = j < n_ptiles // OUT_SUB
    for s in range(OUT_SUB):
        rows = slice(s * TOK_TILE, (s + 1) * TOK_TILE)
        mrows = slice(s * ROWS_PER_TILE, (s + 1) * ROWS_PER_TILE)
        x = jnp.where(is_prompt, xp_ref[rows, :], xs_ref[rows, :])
        oa = jnp.where(is_prompt, oap_ref[rows, :], oas_ref[rows, :])
        ob = jnp.where(is_prompt, obp_ref[rows, :], obs_ref[rows, :])
        mix = _dot(oa, wo_ref[0:A_WIDTH, :]) + _dot(ob, wo_ref[A_WIDTH:D_MODEL, :])
        gate1 = _rows_to_tokens(mod_ref[mrows, 2 * D_MODEL:3 * D_MODEL], D_MODEL)
        x1 = x + gate1 * mix
        x1_ref[rows, :] = x1
        ms = jnp.mean(x1 * x1, axis=-1, keepdims=True)
        xn = x1 * lax.rsqrt(ms + EPS) * gffn_ref[...]
        sh = _rows_to_tokens(mod_ref[mrows, 3 * D_MODEL:4 * D_MODEL], D_MODEL)
        sc = _rows_to_tokens(mod_ref[mrows, 4 * D_MODEL:5 * D_MODEL], D_MODEL)
        h2 = xn * (1.0 + sc) + sh
        words = [pltpu.bitcast(p, U32) for p in _pack_rows(z32_sc.at[s], h2, TOK_TILE)]
        tiles = slice(s * TOK_TILE // SUBLANES, (s + 1) * TOK_TILE // SUBLANES)
        h2p_ref[tiles] = _to_row_tiled(words, TOK_TILE)

        lane_f, i1, i2, w1, w2 = _route(_dot3(h2, wr_ref[...]) + br_ref[...])

        @pl.when(lax.rem(OUT_SUB * j + s, tiles_per_sb) == 0)
        def _():
            cnt_sc[...] = jnp.zeros_like(cnt_sc)

        sel = jnp.where((lane_f == i1) | (lane_f == i2), 1.0, 0.0).astype(BF16)
        before = _dot(ltri_ref[...], sel) + cnt_sc[0:1, :]
        rank1 = jnp.sum(jnp.where(lane_f == i1, before, 0.0), axis=-1, keepdims=True)
        rank2 = jnp.sum(jnp.where(lane_f == i2, before, 0.0), axis=-1, keepdims=True)
        cnt = cnt_sc[...] + _dot(jnp.ones((8, TOK_TILE), BF16), sel)
        cnt_sc[...] = cnt
        cnt_ref[0] = cnt
        cols = (i1, i2, rank1, rank2, w1, w2)
        meta = jnp.zeros_like(lane_f)
        for c, col in enumerate(cols):
            meta = jnp.where(lane_f == float(c), col, meta)
        meta_ref[rows, :] = meta


def _outproj(oa_p, oa_s, ob_p, ob_s, w_out, xp, xs, mod, gffn, wr, br, n_ptiles, n_stiles, prep, sb):
    n_tiles = n_ptiles + n_stiles
    t = n_tiles * TOK_TILE
    tiles_per_sb = sb // TOK_TILE
    assert n_ptiles % OUT_SUB == 0 and n_stiles % OUT_SUB == 0 and tiles_per_sb % OUT_SUB == 0
    step_rows = OUT_SUB * TOK_TILE
    mod_rows = OUT_SUB * ROWS_PER_TILE
    n_psteps = n_ptiles // OUT_SUB
    pblocks = prep // mod_rows
    const = lambda j: (0, 0)
    row = lambda j: (j, 0)
    prow = lambda j: (jnp.minimum(j, n_psteps - 1), 0)
    srow = lambda j: (jnp.maximum(j - n_psteps, 0), 0)
    r = np.arange(TOK_TILE)
    ltri = jnp.asarray(r[None, :] < r[:, None], BF16)
    return pl.pallas_call(
        functools.partial(_outproj_kernel, n_ptiles, tiles_per_sb),
        grid=(n_tiles // OUT_SUB,),
        in_specs=[pl.BlockSpec((step_rows, A_WIDTH), prow),
                  pl.BlockSpec((step_rows, A_WIDTH), srow),
                  pl.BlockSpec((step_rows, B_WIDTH), prow),
                  pl.BlockSpec((step_rows, B_WIDTH), srow),
                  pl.BlockSpec((D_MODEL, D_MODEL), const),
                  pl.BlockSpec((step_rows, D_MODEL), prow),
                  pl.BlockSpec((step_rows, D_MODEL), srow),
                  pl.BlockSpec((mod_rows, 6 * D_MODEL),
                               lambda j: (jnp.maximum(j - n_psteps + pblocks, 0), 0)),
                  pl.BlockSpec((1, D_MODEL), const),
                  pl.BlockSpec((D_MODEL, LANES), const),
                  pl.BlockSpec((1, LANES), const),
                  pl.BlockSpec((TOK_TILE, TOK_TILE), const)],
        out_specs=[pl.BlockSpec((step_rows, D_MODEL), row),
                   pl.BlockSpec((step_rows // SUBLANES, ROW_PIECES, SUBLANES, LANES), lambda j: (j, 0, 0, 0)),
                   pl.BlockSpec((step_rows, LANES), row),
                   pl.BlockSpec((1, 8, LANES), lambda j: (OUT_SUB * j // tiles_per_sb, 0, 0))],
        out_shape=[jax.ShapeDtypeStruct((t, D_MODEL), F32),
                   jax.ShapeDtypeStruct((t // SUBLANES, ROW_PIECES, SUBLANES, LANES), U32),
                   jax.ShapeDtypeStruct((t, LANES), F32),
                   jax.ShapeDtypeStruct((t // sb, 8, LANES), F32)],
        scratch_shapes=[pltpu.VMEM((OUT_SUB, ROW_PIECES, 2 * TOK_TILE, LANES), F32),
                        pltpu.VMEM((8, LANES), F32)],
        compiler_params=_params(),
        name="outproj",
    )(oa_p, oa_s, ob_p, ob_s, w_out, xp, xs, mod, gffn, wr, br, ltri)


MOE_SUPER_BLOCK = 2048
SEG_ALIGN = SUBLANES
CHUNK_BF16_ROWS = 2 * SEG_ALIGN * ROW_PIECES
SEG_BITS = 9
PAD_BITS = 5
FFN_ROWS = 512
PLAN_ROWS = LANES


def _local_rows(sb):
    return 2 * sb + N_EXPERTS * SEG_ALIGN


def _sorted_tiles(n_tokens, sb):
    rows = 2 * n_tokens + (n_tokens // sb) * N_EXPERTS * SEG_ALIGN + N_EXPERTS * FFN_ROWS
    return -(-rows // (FFN_SUB * FFN_ROWS)) * FFN_SUB


def _moe_plan_kernel(n_blocks, total_chunks, meta_ref, cnt_ref, ustrict_ref, lstrict_ref,
                     posw_ref, addr_ref, tab_ref, tile_ref):
    b = pl.program_id(0)
    per_tile = FFN_ROWS // SEG_ALIGN

    @pl.when(b == 0)
    def _():
        cnt = cnt_ref[...]
        chunks = jnp.floor((cnt + (SEG_ALIGN - 1)) * (1.0 / SEG_ALIGN))
        chunks_b = chunks.astype(BF16)
        loc = _dot(chunks_b, ustrict_ref[...])
        before = _dot(lstrict_ref[...], chunks_b)
        tot = _dot(jnp.ones((PLAN_ROWS, PLAN_ROWS), BF16), chunks_b)
        tiles = jnp.floor((tot + (per_tile - 1)) * (1.0 / per_tile))
        tile_off = _dot(tiles.astype(BF16), ustrict_ref[...])
        n_tiles = jnp.sum(tiles[0:1], axis=-1, keepdims=True)
        lane1 = lax.broadcasted_iota(jnp.int32, (PLAN_ROWS, LANES), 1)
        tail = lane1 == ROUTE_OFF + N_EXPERTS
        pad_off = jnp.where(tail, n_tiles * per_tile, tile_off * per_tile + tot)
        pad_n = jnp.where(tail, total_chunks - n_tiles * per_tile, tiles * per_tile - tot)
        row = lax.broadcasted_iota(jnp.int32, (PLAN_ROWS, LANES), 0)
        tab_ref[0] = loc
        tab_ref[1] = chunks
        tab_ref[2] = tile_off * per_tile + before
        tab_ref[3] = jnp.where(row == 0, pad_off, jnp.where(row == 1, pad_n, jnp.where(row == 2, n_tiles, 0.0)))
        t_idx = lax.broadcasted_iota(jnp.int32, tile_ref.shape, 0).astype(F32)
        lane_t = lax.broadcasted_iota(jnp.int32, tile_ref.shape, 1)
        is_expert = (lane_t >= ROUTE_OFF) & (lane_t < ROUTE_OFF + N_EXPERTS)
        ends = (tile_off + tiles)[0:1, :]
        owner = jnp.sum(jnp.where(is_expert & (ends <= t_idx), 1.0, 0.0), axis=-1, keepdims=True)
        tile_ref[...] = jnp.broadcast_to(jnp.minimum(owner, N_EXPERTS - 1.0), tile_ref.shape)

    own = jnp.floor((cnt_ref[pl.ds(b, 1), :] + (SEG_ALIGN - 1)) * (1.0 / SEG_ALIGN))
    own_off = _dot(jnp.broadcast_to(own, (SUBLANES, LANES)).astype(BF16), ustrict_ref[...]) * SEG_ALIGN
    meta = meta_ref[...]
    lane_f = lax.broadcasted_iota(jnp.int32, meta.shape, 1).astype(F32)
    off_row = own_off[0:1, :]
    pos = []
    for k in range(2):
        e_lane = meta[:, k:k + 1]
        base = jnp.sum(jnp.where(lane_f == e_lane, off_row, 0.0), axis=-1, keepdims=True)
        p = base + meta[:, 2 + k:3 + k]
        tile = jnp.floor(p * (1.0 / SUBLANES))
        pos.append(tile * (ROW_TILE - SUBLANES) + p)
    out = jnp.zeros_like(meta)
    for c, col in enumerate((pos[0], pos[1], meta[:, 4:5], meta[:, 5:6])):
        out = jnp.where(lane_f == float(c), col, out)
    posw_ref[...] = out
    addr_ref[0] = out.T[0:SUBLANES]


def _moe_plan(meta, cnt, sb):
    n_blocks = meta.shape[0] // sb
    assert n_blocks <= PLAN_ROWS and sb // SEG_ALIGN <= 256
    n_tiles = _sorted_tiles(meta.shape[0], sb)
    tile_rows = -(-n_tiles // SUBLANES) * SUBLANES
    r = np.arange(LANES)
    ustrict = jnp.asarray(r[:, None] < r[None, :], BF16)
    lstrict = jnp.asarray(r[None, :] < r[:, None], BF16)
    cnt_all = jnp.pad(cnt[:, 0, :], ((0, PLAN_ROWS - n_blocks), (0, 0)))
    const = lambda s: (0, 0)
    posw, addr, tab, tile_owner = pl.pallas_call(
        functools.partial(_moe_plan_kernel, n_blocks, float(n_tiles * (FFN_ROWS // SEG_ALIGN))),
        grid=(n_blocks,),
        in_specs=[pl.BlockSpec((sb, LANES), lambda s: (s, 0)),
                  pl.BlockSpec((PLAN_ROWS, LANES), const),
                  pl.BlockSpec((LANES, LANES), const),
                  pl.BlockSpec((PLAN_ROWS, PLAN_ROWS), const)],
        out_specs=[pl.BlockSpec((sb, LANES), lambda s: (s, 0)),
                   pl.BlockSpec((1, SUBLANES, sb), lambda s: (s, 0, 0)),
                   pl.BlockSpec((4, PLAN_ROWS, LANES), lambda s: (0, 0, 0)),
                   pl.BlockSpec((tile_rows, LANES), const)],
        out_shape=[jax.ShapeDtypeStruct(meta.shape, F32),
                   jax.ShapeDtypeStruct((n_blocks, SUBLANES, sb), F32),
                   jax.ShapeDtypeStruct((4, PLAN_ROWS, LANES), F32),
                   jax.ShapeDtypeStruct((tile_rows, LANES), F32)],
        compiler_params=_params(),
        name="moe_plan",
    )(meta, cnt_all, ustrict, lstrict)
    experts = slice(ROUTE_OFF, ROUTE_OFF + N_EXPERTS)
    to_i32 = lambda x: x.astype(jnp.int32).reshape(-1)
    plan = dict(
        loc=to_i32(tab[0, :n_blocks, experts]), n=to_i32(tab[1, :n_blocks, experts]),
        dst=to_i32(tab[2, :n_blocks, experts]),
        pad_off=to_i32(tab[3, 0, ROUTE_OFF:ROUTE_OFF + N_EXPERTS + 1]),
        pad_n=to_i32(tab[3, 1, ROUTE_OFF:ROUTE_OFF + N_EXPERTS + 1]),
        n_tiles=to_i32(tab[3, 2, 0:1]),
        owner=to_i32(tile_owner[:n_tiles, 0]))
    return posw, to_i32(addr[:, 0, :]), to_i32(addr[:, 1, :]), plan


def _token_rows(start):
    return pl.ds(start, ROW_PIECES, stride=SUBLANES)


def _pow2_copies(src_ref, dst_ref, src_chunk, dst_chunk, n, n_bits, sem, act):
    done = 0
    for k in reversed(range(n_bits)):
        take = (n >> k) & 1
        rows = CHUNK_BF16_ROWS << k
        src0 = 0 if src_chunk is None else pl.multiple_of((src_chunk + done) * CHUNK_BF16_ROWS, CHUNK_BF16_ROWS)
        dst0 = pl.multiple_of((dst_chunk + done) * CHUNK_BF16_ROWS, CHUNK_BF16_ROWS)

        @pl.when(take == 1)
        def _(src0=src0, dst0=dst0, rows=rows):
            act(pltpu.make_async_copy(src_ref.at[pl.ds(src0, rows)], dst_ref.at[pl.ds(dst0, rows)], sem))

        done = done + take * (1 << k)


def _segment_copies(block, loc_ref, n_ref, dst_ref, local_ref, global_ref, to_global, sem, act):
    def per_expert(e, carry):
        seg = block * N_EXPERTS + e
        if to_global:
            _pow2_copies(local_ref, global_ref, loc_ref[seg], dst_ref[seg], n_ref[seg], SEG_BITS, sem, act)
        else:
            _pow2_copies(global_ref, local_ref, dst_ref[seg], loc_ref[seg], n_ref[seg], SEG_BITS, sem, act)
        return carry

    lax.fori_loop(0, N_EXPERTS, per_expert, 0)


def _zero_fill(zero_ref, global_ref, padoff_ref, padn_ref, sem, act):
    full = 1 << PAD_BITS

    def per_pad(e, carry):
        def per_full(c, inner):
            dst0 = pl.multiple_of((padoff_ref[e] + c * full) * CHUNK_BF16_ROWS, CHUNK_BF16_ROWS)
            act(pltpu.make_async_copy(zero_ref, global_ref.at[pl.ds(dst0, full * CHUNK_BF16_ROWS)], sem))
            return inner

        n_full = padn_ref[e] >> PAD_BITS
        lax.fori_loop(0, n_full, per_full, 0)
        _pow2_copies(zero_ref, global_ref, None, padoff_ref[e] + n_full * full, padn_ref[e] & (full - 1),
                     PAD_BITS, sem, act)
        return carry

    lax.fori_loop(0, N_EXPERTS + 1, per_pad, 0)


STAGE_SLAB = 1024


def _restage(src_sc, dst_sc, dst_dtype):
    ratio = dst_sc.shape[0] / src_sc.shape[0]
    n_slabs = src_sc.shape[0] // (STAGE_SLAB if ratio > 1 else 2 * STAGE_SLAB)
    src_rows = src_sc.shape[0] // n_slabs
    dst_rows = dst_sc.shape[0] // n_slabs

    def slab(i, carry):
        s0 = pl.multiple_of(i * src_rows, src_rows)
        d0 = pl.multiple_of(i * dst_rows, dst_rows)
        dst_sc[pl.ds(d0, dst_rows), :] = pltpu.bitcast(src_sc[pl.ds(s0, src_rows), :], dst_dtype)
        return carry

    lax.fori_loop(0, n_slabs, slab, 0)


def _moe_dispatch_kernel(sb, n_blocks, loc_ref, n_ref, dst_ref, padoff_ref, padn_ref,
                         h2p_ref, a1_ref, a2_ref, xs_hbm, local_sc, stage_sc, zero_sc, sems, zero_sem):
    b = pl.program_id(0)
    slot = b & 1

    def segments(block, buf, act):
        _segment_copies(block, loc_ref, n_ref, dst_ref, stage_sc.at[buf], xs_hbm, True, sems.at[buf], act)

    local_sc[...] = jnp.zeros_like(local_sc)

    def step(g, carry):
        src = pl.multiple_of(g * ROW_TILE, ROW_TILE)
        for u in range(SUBLANES):
            t = g * SUBLANES + u
            row = h2p_ref[_token_rows(src + u), :]
            local_sc[_token_rows(a1_ref[t]), :] = row
            local_sc[_token_rows(a2_ref[t]), :] = row
        return carry

    lax.fori_loop(0, sb // SUBLANES, step, 0)

    @pl.when(b > 0)
    def _():
        segments(b - 1, 1 - slot, lambda c: c.wait())

    _restage(local_sc, stage_sc.at[slot], BF16)
    segments(b, slot, lambda c: c.start())

    @pl.when(b == 0)
    def _():
        zero_sc[...] = jnp.zeros_like(zero_sc)
        _zero_fill(zero_sc, xs_hbm, padoff_ref, padn_ref, zero_sem, lambda c: c.start())
        _zero_fill(zero_sc, xs_hbm, padoff_ref, padn_ref, zero_sem, lambda c: c.wait())

    @pl.when(b == n_blocks - 1)
    def _():
        segments(b, slot, lambda c: c.wait())


def _smem_vec(n, index_map):
    return pl.BlockSpec((n,), index_map, memory_space=pltpu.SMEM)


def _moe_dispatch(h2p, a1, a2, plan, sb, n_tiles):
    n_blocks = h2p.shape[0] // (sb * ROW_PIECES)
    local_flat = _local_rows(sb) * ROW_PIECES
    vec = _smem_vec(sb, lambda s, *_: (s,))
    return pl.pallas_call(
        functools.partial(_moe_dispatch_kernel, sb, n_blocks),
        grid_spec=pltpu.PrefetchScalarGridSpec(
            num_scalar_prefetch=5,
            grid=(n_blocks,),
            in_specs=[pl.BlockSpec((sb * ROW_PIECES, LANES), lambda s, *_: (s, 0)), vec, vec],
            out_specs=pl.BlockSpec(memory_space=pl.ANY),
            scratch_shapes=[pltpu.VMEM((local_flat, LANES), U32),
                            pltpu.VMEM((2, 2 * local_flat, LANES), BF16),
                            pltpu.VMEM(((1 << PAD_BITS) * CHUNK_BF16_ROWS, LANES), BF16),
                            pltpu.SemaphoreType.DMA((2,)),
                            pltpu.SemaphoreType.DMA(())]),
        out_shape=jax.ShapeDtypeStruct((n_tiles * FFN_ROWS * ROW_PIECES * 2, LANES), BF16),
        compiler_params=_params(),
        name="moe_dispatch",
    )(plan["loc"], plan["n"], plan["dst"], plan["pad_off"], plan["pad_n"], h2p, a1, a2)


FFN_SUB = 2


def _moe_ffn_kernel(owner_ref, ntiles_ref, xs_ref, *refs):
    w_refs = refs[0:3 * FFN_SUB]
    ys_ref, wg_sc, wu_sc, wd_sc, z32_sc = refs[3 * FFN_SUB:]
    half = D_MODEL // 2
    flat = FFN_ROWS * ROW_PIECES * 2
    for s in range(FFN_SUB):
        t = pl.program_id(0) * FFN_SUB + s
        wg_ref, wu_ref, wd_ref = w_refs[3 * s:3 * s + 3]
        rows = slice(s * flat, (s + 1) * flat)
        used = t < ntiles_ref[0]

        @pl.when(used & ((t < FFN_SUB) | (owner_ref[t] != owner_ref[jnp.maximum(t - FFN_SUB, 0)])))
        def _():
            wg_sc[s] = wg_ref[0].astype(BF16)
            wu_sc[s] = wu_ref[0].astype(BF16)
            wd_sc[s] = wd_ref[0].astype(BF16)

        @pl.when(used)
        def _():
            z32 = z32_sc.at[s]
            lo, hi = _unpack_rows(z32, _from_row_tiled(xs_ref[rows, :], FFN_ROWS), FFN_ROWS)
            lo, hi = lo.astype(BF16), hi.astype(BF16)
            g = _dot(lo, wg_sc[s, 0:half, :]) + _dot(hi, wg_sc[s, half:D_MODEL, :])
            u = _dot(lo, wu_sc[s, 0:half, :]) + _dot(hi, wu_sc[s, half:D_MODEL, :])
            y = _dot((_silu(g) * u).astype(BF16), wd_sc[s])
            ys_ref[rows, :] = _flatten_tiled(_to_row_tiled(_pack_rows(z32, y, FFN_ROWS), FFN_ROWS))

        @pl.when(jnp.logical_not(used))
        def _():
            ys_ref[rows, :] = jnp.zeros((flat, LANES), BF16)


def _moe_ffn(xs, plan, wg, wu, wd):
    flat = FFN_ROWS * ROW_PIECES * 2
    n_tiles = xs.shape[0] // flat
    assert n_tiles % FFN_SUB == 0

    def wspec(shape, s):
        tile = lambda i, owner, nt: jnp.minimum(i * FFN_SUB + s, nt[0] - 1)
        return pl.BlockSpec((1,) + shape, lambda i, owner, nt: (owner[tile(i, owner, nt)], 0, 0))

    w_specs, w_args = [], []
    for s in range(FFN_SUB):
        w_specs += [wspec((D_MODEL, EXPERT_FF), s), wspec((D_MODEL, EXPERT_FF), s), wspec((EXPERT_FF, D_MODEL), s)]
        w_args += [wg, wu, wd]
    last_step = lambda i, owner, nt: jnp.minimum(i, (nt[0] - 1) // FFN_SUB)
    return pl.pallas_call(
        _moe_ffn_kernel,
        grid_spec=pltpu.PrefetchScalarGridSpec(
            num_scalar_prefetch=2,
            grid=(n_tiles // FFN_SUB,),
            in_specs=[pl.BlockSpec((FFN_SUB * flat, LANES), lambda i, owner, nt: (last_step(i, owner, nt), 0))]
                     + w_specs,
            out_specs=pl.BlockSpec((FFN_SUB * flat, LANES), lambda i, owner, nt: (i, 0)),
            scratch_shapes=[pltpu.VMEM((FFN_SUB, D_MODEL, EXPERT_FF), BF16),
                            pltpu.VMEM((FFN_SUB, D_MODEL, EXPERT_FF), BF16),
                            pltpu.VMEM((FFN_SUB, EXPERT_FF, D_MODEL), BF16),
                            pltpu.VMEM((FFN_SUB, ROW_PIECES, 2 * FFN_ROWS, LANES), F32)]),
        out_shape=jax.ShapeDtypeStruct(xs.shape, BF16),
        compiler_params=_params(),
        name="moe_ffn",
    )(plan["owner"], plan["n_tiles"], xs, *w_args)


def _moe_combine_kernel(n_psb, n_blocks, loc_ref, n_ref, dst_ref,
                        ys_hbm, a1_ref, a2_ref, posw_ref, x1_ref, mod_ref, yp_ref, yo_ref,
                        local_sc, stage_sc, g1_sc, g2_sc, z32_sc, sems):
    s = pl.program_id(0)
    slot = s & 1

    def segments(block, buf, act):
        _segment_copies(block, loc_ref, n_ref, dst_ref, stage_sc.at[buf], ys_hbm, False, sems.at[buf], act)

    @pl.when(pl.program_id(1) == 0)
    def _():
        @pl.when(s == 0)
        def _():
            segments(s, slot, lambda c: c.start())

        segments(s, slot, lambda c: c.wait())
        _restage(stage_sc.at[slot], local_sc, U32)

        @pl.when(s + 1 < n_blocks)
        def _():
            segments(s + 1, 1 - slot, lambda c: c.start())

    def step(g, carry):
        dst = pl.multiple_of(g * ROW_TILE, ROW_TILE)
        for u in range(SUBLANES):
            t = g * SUBLANES + u
            g1_sc[_token_rows(dst + u), :] = local_sc[_token_rows(a1_ref[t]), :]
            g2_sc[_token_rows(dst + u), :] = local_sc[_token_rows(a2_ref[t]), :]
        return carry

    lax.fori_loop(0, TOK_TILE // SUBLANES, step, 0)
    halves = lambda g_sc: [pltpu.bitcast(p, BF16) for p in _from_row_tiled(g_sc[...], TOK_TILE)]
    lo1, hi1 = _unpack_rows(z32_sc, halves(g1_sc), TOK_TILE)
    lo2, hi2 = _unpack_rows(z32_sc, halves(g2_sc), TOK_TILE)
    w1, w2 = posw_ref[:, 2:3], posw_ref[:, 3:4]
    moe = jnp.concatenate([w1 * lo1 + w2 * lo2, w1 * hi1 + w2 * hi2], axis=1)
    gate2 = _rows_to_tokens(mod_ref[:, 5 * D_MODEL:6 * D_MODEL], D_MODEL)
    y = x1_ref[...] + gate2 * moe

    @pl.when(s < n_psb)
    def _():
        yp_ref[...] = y

    @pl.when(s >= n_psb)
    def _():
        yo_ref[...] = y


def _moe_combine(ys, a1, a2, posw, plan, x1, mod, sb, n_ptiles, n_stiles, prep):
    tps = sb // TOK_TILE
    n_blocks = (n_ptiles + n_stiles) // tps
    n_psb = n_ptiles // tps
    pblocks = prep // ROWS_PER_TILE
    tile = lambda s, j: s * tps + j
    vec = _smem_vec(TOK_TILE, lambda s, j, *_: (tile(s, j),))
    return pl.pallas_call(
        functools.partial(_moe_combine_kernel, n_psb, n_blocks),
        grid_spec=pltpu.PrefetchScalarGridSpec(
            num_scalar_prefetch=3,
            grid=(n_blocks, tps),
            in_specs=[pl.BlockSpec(memory_space=pl.ANY), vec, vec,
                      pl.BlockSpec((TOK_TILE, LANES), lambda s, j, *_: (tile(s, j), 0)),
                      pl.BlockSpec((TOK_TILE, D_MODEL), lambda s, j, *_: (tile(s, j), 0)),
                      pl.BlockSpec((ROWS_PER_TILE, 6 * D_MODEL),
                                   lambda s, j, *_: (jnp.maximum(tile(s, j) - n_ptiles + pblocks, 0), 0))],
            out_specs=[pl.BlockSpec((TOK_TILE, D_MODEL),
                                    lambda s, j, *_: (jnp.minimum(tile(s, j), n_ptiles - 1), 0)),
                       pl.BlockSpec((TOK_TILE, D_MODEL),
                                    lambda s, j, *_: (jnp.maximum(tile(s, j) - n_ptiles, 0), 0))],
            scratch_shapes=[pltpu.VMEM((_local_rows(sb) * ROW_PIECES, LANES), U32),
                            pltpu.VMEM((2, _local_rows(sb) * ROW_PIECES * 2, LANES), BF16),
                            pltpu.VMEM((TOK_TILE * ROW_PIECES, LANES), U32),
                            pltpu.VMEM((TOK_TILE * ROW_PIECES, LANES), U32),
                            pltpu.VMEM((ROW_PIECES, 2 * TOK_TILE, LANES), F32),
                            pltpu.SemaphoreType.DMA((2,))]),
        out_shape=[jax.ShapeDtypeStruct((n_ptiles * TOK_TILE, D_MODEL), F32),
                   jax.ShapeDtypeStruct((n_stiles * TOK_TILE, D_MODEL), F32)],
        compiler_params=_params(2),
        name="moe_combine",
    )(plan["loc"], plan["n"], plan["dst"], ys, a1, a2, posw, x1, mod)


def _layer(xp, xs, cache_k, cache_v, state, c_prompt, c_sample, norm_mix_g, norm_ffn_g, w_ada, b_ada, w_in,
           q_norm_g, k_norm_g, rel_bias, w_gate_up, b_gate, gla_norm_g, w_out, w_route_group,
           b_route_group, w_route_expert, b_route_expert, w_exp_gate, w_exp_up, w_exp_down):
    batch, seq, _ = xp.shape
    n_seq, dec_seq, _ = xs.shape
    assert batch == 1 and dec_seq == CHUNK and cache_k.shape[1] == BAND_PAST
    assert seq % TOK_TILE == 0 and seq >= BAND_PAST and (n_seq * CHUNK) % TOK_TILE == 0
    assert seq % (ATTN_SUB * Q_ROWS) == 0 and seq % (GLA_SUB * GLA_CHUNKS * CHUNK) == 0
    n_ptok, n_stok = seq, n_seq * CHUNK
    n_ptiles, n_stiles = n_ptok // TOK_TILE, n_stok // TOK_TILE
    sb = (MOE_SUPER_BLOCK if (n_ptok % MOE_SUPER_BLOCK == 0 and n_stok % MOE_SUPER_BLOCK == 0)
          else OUT_SUB * TOK_TILE)
    prep = OUT_SUB * ROWS_PER_TILE

    xp2 = xp.reshape(n_ptok, D_MODEL)
    xs2 = xs.reshape(n_stok, D_MODEL)
    c_rows = jnp.concatenate([jnp.broadcast_to(c_prompt, (prep, D_MODEL)), c_sample], axis=0)
    mod = _adaln(c_rows, w_ada, b_ada)

    w_main = w_in[:, 0:IN_MAIN].astype(BF16)
    w_gr = jnp.pad(w_in[:, IN_MAIN:], ((0, 0), (0, LANES - GATE_RANK))).astype(BF16)
    wgu_p = jnp.pad(w_gate_up, ((0, LANES - GATE_RANK), (0, 0))).astype(BF16)
    head = np.arange(A_WIDTH) // A_HEAD_DIM
    bd = jnp.asarray(head[:, None] == head[None, :], BF16)
    gq = jnp.tile(q_norm_g, A_HEADS).reshape(1, A_WIDTH)
    gk = jnp.tile(k_norm_g, A_HEADS).reshape(1, A_WIDTH)
    q, k, v, kf, vf, gla, la = _inproj(
        xp2, xs2, mod, norm_mix_g.reshape(1, D_MODEL), w_main, w_gr, bd, gq, gk, wgu_p,
        b_gate.reshape(1, B_KWIDTH), n_ptiles, n_stiles, prep)

    first_chunk = n_ptok // CHUNK
    oa_p = _attn_prompt(rel_bias[:, _bias_lanes(ATTN_WIN * Q_ROWS)], q, k, v, n_ptok // (ATTN_SUB * Q_ROWS))
    oa_s = _attn_sample(rel_bias[:, _bias_lanes(SAMPLE_KEYS)], q, k, v,
                        cache_k.reshape(n_seq, BAND_PAST, A_WIDTH), cache_v.reshape(n_seq, BAND_PAST, A_WIDTH),
                        first_chunk, n_seq)
    g_gla = gla_norm_g.reshape(1, B_DV)
    ob_p, sfin_p = _gla_prompt(gla, la, g_gla, n_ptok // (GLA_SUB * GLA_CHUNKS * CHUNK))
    ob_s, sfin_s = _gla_sample(gla, la, g_gla, _state_to_pairs(state), first_chunk, n_seq)

    wr = jnp.pad(jnp.concatenate([w_route_group, w_route_expert], axis=1),
                 ((0, 0), (0, LANES - N_GROUPS - N_EXPERTS)))
    br = jnp.pad(jnp.concatenate([b_route_group, b_route_expert]), (0, LANES - N_GROUPS - N_EXPERTS))
    x1, h2p, meta, cnt = _outproj(oa_p, oa_s, ob_p, ob_s, w_out.astype(BF16), xp2, xs2, mod,
                                  norm_ffn_g.reshape(1, D_MODEL), wr, br.reshape(1, LANES),
                                  n_ptiles, n_stiles, prep, sb)

    posw, a1, a2, plan = _moe_plan(meta, cnt, sb)
    xs_sorted = _moe_dispatch(h2p.reshape(-1, LANES), a1, a2, plan, sb, _sorted_tiles(n_ptok + n_stok, sb))
    ys_sorted = _moe_ffn(xs_sorted, plan, w_exp_gate, w_exp_up, w_exp_down)
    yp, ys = _moe_combine(ys_sorted, a1, a2, posw, plan, x1, mod, sb, n_ptiles, n_stiles, prep)

    tail = min(BAND_PAST, seq)
    heads = (A_HEADS, A_HEAD_DIM)
    return (yp.reshape(1, seq, D_MODEL), ys.reshape(n_seq, CHUNK, D_MODEL),
            kf[TOK_TILE - tail:TOK_TILE].reshape((1, tail) + heads),
            vf[TOK_TILE - tail:TOK_TILE].reshape((1, tail) + heads),
            _pairs_to_state(sfin_p)[None],
            kf[TOK_TILE:].reshape((n_seq, CHUNK) + heads),
            vf[TOK_TILE:].reshape((n_seq, CHUNK) + heads),
            _pairs_to_state(sfin_s))


def kernel(x_prompt, x_sample, cache_a_k, cache_a_v, state_gla, c_prompt, c_sample, norm_mix_g, norm_ffn_g,
           w_ada, b_ada, w_in, q_norm_g, k_norm_g, rel_bias, w_gate_up, b_gate, gla_norm_g, w_out,
           w_route_group, b_route_group, w_route_expert, b_route_expert, w_exp_gate, w_exp_up, w_exp_down):
    depth = w_in.shape[0]
    yp, ys = x_prompt, x_sample
    outs = [[] for _ in range(6)]
    for l in range(depth):
        yp, ys, kp, vp, sp, ks, vs, ss = _layer(
            yp, ys, cache_a_k[l], cache_a_v[l], state_gla[l], c_prompt, c_sample, norm_mix_g[l], norm_ffn_g[l],
            w_ada[l], b_ada[l], w_in[l], q_norm_g[l], k_norm_g[l], rel_bias[l], w_gate_up[l], b_gate[l],
            gla_norm_g[l], w_out[l], w_route_group[l], b_route_group[l], w_route_expert[l], b_route_expert[l],
            w_exp_gate[l], w_exp_up[l], w_exp_down[l])
        for lst, val in zip(outs, (kp, vp, sp, ks, vs, ss)):
            lst.append(val)
    return (yp, ys) + tuple(jnp.stack(o) for o in outs)
```

```python
import functools

import numpy as np
import jax
import jax.numpy as jnp
from jax import lax
from jax.experimental import pallas as pl
from jax.experimental.pallas import tpu as pltpu

F32 = jnp.float32
BF16 = jnp.bfloat16
U32 = jnp.uint32

D_MODEL = 1024
CHUNK = 64
LOG_CHUNK = 6
BAND_CHUNKS = 8
BAND_PAST = BAND_CHUNKS * CHUNK
A_WIDTH = 512
A_HEADS = 8
A_HEAD_DIM = 64
MAX_REL = 128
B_WIDTH = 512
B_HEADS = 4
B_DV = 128
B_DK = 64
B_KWIDTH = 256
GATE_RANK = 16
GATE_TAU = 16.0
N_GROUPS = 4
EXPERTS_PER_GROUP = 8
N_EXPERTS = 32
EXPERT_FF = 256
EPS = 1e-6

LANES = 128
IN_MAIN = 3 * A_WIDTH + 2 * B_KWIDTH + 2 * B_WIDTH
TOK_TILE = 512
ROWS_PER_TILE = TOK_TILE // CHUNK
OUT_SUB = 1
Q_CHUNKS = 4
Q_ROWS = Q_CHUNKS * CHUNK
ROLL_W = 1024
NEG = -1e30
ROUTE_OFF = N_GROUPS
VMEM_LIMIT = 56 * 1024 * 1024


def _params(n_axes=1):
    return pltpu.CompilerParams(dimension_semantics=("arbitrary",) * n_axes,
                                vmem_limit_bytes=VMEM_LIMIT)


def _split(a):
    hi = a.astype(BF16)
    lo = (a - hi.astype(F32)).astype(BF16)
    return hi, lo


def _dot(a, b):
    return jnp.dot(a, b, preferred_element_type=F32)


def _dot3(a, b):
    ah, al = _split(a)
    bh, bl = _split(b)
    return _dot(ah, bh) + _dot(al, bh) + _dot(ah, bl)


def _dot_nt(a, b):
    return lax.dot_general(a, b, (((1,), (1,)), ((), ())), preferred_element_type=F32)


def _dot_tn(a, b):
    return lax.dot_general(a, b, (((0,), (0,)), ((), ())), preferred_element_type=F32)


def _silu(x):
    return x / (1.0 + jnp.exp(-x))


def _rows_to_tokens(rows, n):
    r = rows.shape[0]
    return jnp.broadcast_to(rows[:, None, :], (r, CHUNK, n)).reshape(r * CHUNK, n)


def _adaln_kernel(c_ref, w_ref, b_ref, o_ref):
    a = _silu(c_ref[...])
    o_ref[...] = _dot3(a, w_ref[...]) + b_ref[...]


def _adaln(c_rows, w_ada, b_ada):
    r = c_rows.shape[0]
    n = w_ada.shape[1]
    tn = 1024
    return pl.pallas_call(
        _adaln_kernel,
        grid=(n // tn,),
        in_specs=[pl.BlockSpec((r, D_MODEL), lambda j: (0, 0)),
                  pl.BlockSpec((D_MODEL, tn), lambda j: (0, j)),
                  pl.BlockSpec((1, tn), lambda j: (0, j))],
        out_specs=pl.BlockSpec((r, tn), lambda j: (0, j)),
        out_shape=jax.ShapeDtypeStruct((r, n), F32),
        compiler_params=_params(),
        name="adaln",
    )(c_rows, w_ada, b_ada.reshape(1, n))


def _head_rms(z, bd_ref, g):
    ms = _dot((z * z).astype(BF16), bd_ref[...]) * (1.0 / A_HEAD_DIM)
    return z * lax.rsqrt(ms + EPS) * g


def _inproj_kernel(n_ptiles, xp_ref, xs_ref, mod_ref, gmix_ref, w_ref, wgr_ref, bd_ref, gq_ref, gk_ref,
                   wgu_ref, bg_ref,
                   q_ref, k_ref, v_ref, kf_ref, vf_ref, gla_ref, la_ref):
    i = pl.program_id(0)
    x = jnp.where(i < n_ptiles, xp_ref[...], xs_ref[...])
    ms = jnp.mean(x * x, axis=-1, keepdims=True)
    xn = x * lax.rsqrt(ms + EPS) * gmix_ref[...]
    sh = _rows_to_tokens(mod_ref[:, 0:D_MODEL], D_MODEL)
    sc = _rows_to_tokens(mod_ref[:, D_MODEL:2 * D_MODEL], D_MODEL)
    hb = (xn * (1.0 + sc) + sh).astype(BF16)

    zq = _dot(hb, w_ref[:, 0:A_WIDTH])
    q_ref[...] = (_head_rms(zq, bd_ref, gq_ref[...]) * (LOG2E * A_HEAD_DIM ** -0.5)).astype(BF16)
    zk = _dot(hb, w_ref[:, A_WIDTH:2 * A_WIDTH])
    kn = _head_rms(zk, bd_ref, gk_ref[...])
    k_ref[...] = kn.astype(BF16)
    kf_ref[...] = kn
    zv = _dot(hb, w_ref[:, 2 * A_WIDTH:3 * A_WIDTH])
    v_ref[...] = zv.astype(BF16)
    vf_ref[...] = zv

    o = 3 * A_WIDTH
    zqb = _dot(hb, w_ref[:, o:o + B_KWIDTH]) * (B_DK ** -0.5)
    gla_ref[:, 0:B_KWIDTH] = zqb.astype(BF16)
    for c in range(B_KWIDTH, 2 * B_KWIDTH + 2 * B_WIDTH, 256):
        gla_ref[:, c:c + 256] = _dot(hb, w_ref[:, o + c:o + c + 256]).astype(BF16)

    gr = _dot(hb, wgr_ref[...])
    logit = _dot(gr.astype(BF16), wgu_ref[...]) + bg_ref[...]
    log_sig = jnp.minimum(logit, 0.0) - jnp.log1p(jnp.exp(-jnp.abs(logit)))
    la_ref[...] = log_sig * (1.0 / GATE_TAU)


def _inproj(xp, xs, mod, gmix, w_main, w_gr, bd, gq, gk, wgu_p, bg, n_ptiles, n_stiles, prep):
    n_tiles = n_ptiles + n_stiles
    t = n_tiles * TOK_TILE
    tail_tiles = 1 + n_stiles
    pblocks = prep // ROWS_PER_TILE
    const = lambda i: (0, 0)
    row = lambda i: (i, 0)
    tail = lambda i: (jnp.maximum(i - (n_ptiles - 1), 0), 0)
    return pl.pallas_call(
        functools.partial(_inproj_kernel, n_ptiles),
        grid=(n_tiles,),
        in_specs=[pl.BlockSpec((TOK_TILE, D_MODEL), lambda i: (jnp.minimum(i, n_ptiles - 1), 0)),
                  pl.BlockSpec((TOK_TILE, D_MODEL), lambda i: (jnp.maximum(i - n_ptiles, 0), 0)),
                  pl.BlockSpec((ROWS_PER_TILE, 6 * D_MODEL),
                               lambda i: (jnp.maximum(i - n_ptiles + pblocks, 0), 0)),
                  pl.BlockSpec((1, D_MODEL), const),
                  pl.BlockSpec((D_MODEL, IN_MAIN), const),
                  pl.BlockSpec((D_MODEL, LANES), const),
                  pl.BlockSpec((A_WIDTH, A_WIDTH), const),
                  pl.BlockSpec((1, A_WIDTH), const),
                  pl.BlockSpec((1, A_WIDTH), const),
                  pl.BlockSpec((LANES, B_KWIDTH), const),
                  pl.BlockSpec((1, B_KWIDTH), const)],
        out_specs=[pl.BlockSpec((TOK_TILE, A_WIDTH), row),
                   pl.BlockSpec((TOK_TILE, A_WIDTH), row),
                   pl.BlockSpec((TOK_TILE, A_WIDTH), row),
                   pl.BlockSpec((TOK_TILE, A_WIDTH), tail),
                   pl.BlockSpec((TOK_TILE, A_WIDTH), tail),
                   pl.BlockSpec((TOK_TILE, 2 * B_KWIDTH + 2 * B_WIDTH), row),
                   pl.BlockSpec((TOK_TILE, B_KWIDTH), row)],
        out_shape=[jax.ShapeDtypeStruct((t, A_WIDTH), BF16),
                   jax.ShapeDtypeStruct((t, A_WIDTH), BF16),
                   jax.ShapeDtypeStruct((t, A_WIDTH), BF16),
                   jax.ShapeDtypeStruct((tail_tiles * TOK_TILE, A_WIDTH), F32),
                   jax.ShapeDtypeStruct((tail_tiles * TOK_TILE, A_WIDTH), F32),
                   jax.ShapeDtypeStruct((t, 2 * B_KWIDTH + 2 * B_WIDTH), BF16),
                   jax.ShapeDtypeStruct((t, B_KWIDTH), F32)],
        compiler_params=_params(),
        name="inproj",
    )(xp, xs, mod, gmix, w_main, w_gr, bd, gq, gk, wgu_p, bg)


def _bias_lanes(n_keys):
    l = np.arange(ROLL_W)
    d = np.where(l < n_keys, BAND_PAST - l, BAND_PAST - l + ROLL_W)
    return np.clip(d, -(CHUNK - 1), MAX_REL) + (CHUNK - 1)


LOG2E = 1.4426950408889634


def _band_mask(m_rows, n_keys, first_col):
    qi = lax.broadcasted_iota(jnp.int32, (m_rows, n_keys), 0) >> LOG_CHUNK
    kw = lax.broadcasted_iota(jnp.int32, (m_rows, n_keys), 1)
    kc = kw >> LOG_CHUNK
    return (kc >= qi) & (kc <= qi + BAND_CHUNKS) & (kw >= first_col)


def _bias_tile(u_ref, h, ok):
    m_rows, n_keys = ok.shape
    src = jnp.broadcast_to(u_ref[h:h + 1, :] * LOG2E, (m_rows, ROLL_W))
    toe = pltpu.roll(src, 0, 1, stride=1, stride_axis=0)
    return jnp.where(ok, toe[:, 0:n_keys], NEG)


def _attend(q, kcat, vcat, bias_sc):
    m_rows = q.shape[0]
    first = lax.broadcasted_iota(jnp.int32, (m_rows, LANES), 1) < A_HEAD_DIM
    outs = []
    for p in range(A_HEADS // 2):
        lanes = slice(p * LANES, (p + 1) * LANES)
        qp, kp, vp = q[:, lanes], kcat[:, lanes], vcat[:, lanes]
        zero = jnp.zeros_like(qp)
        q2 = jnp.concatenate([jnp.where(first, qp, zero), jnp.where(first, zero, qp)], axis=0)
        s = _dot_nt(q2, kp) + bias_sc[p]
        e = jnp.exp2(s - jnp.max(s, axis=-1, keepdims=True))
        l = jnp.sum(e, axis=-1, keepdims=True)
        o2 = _dot(e.astype(BF16), vp) / l
        outs.append(jnp.where(first, o2[0:m_rows], o2[m_rows:2 * m_rows]))
    return jnp.concatenate(outs, axis=-1)


ATTN_SUB = 4
ATTN_WIN = 3


def _attn_prompt_kernel(u_ref, q_ref, *refs):
    k_refs = refs[0:ATTN_SUB + ATTN_WIN - 1]
    v_refs = refs[ATTN_SUB + ATTN_WIN - 1:2 * (ATTN_SUB + ATTN_WIN - 1)]
    o_ref, bias_sc = refs[-2:]
    j = pl.program_id(0)
    n_keys = ATTN_WIN * Q_ROWS

    @pl.when(j == 0)
    def _():
        for g in range(ATTN_WIN):
            ok = _band_mask(Q_ROWS, n_keys, (ATTN_WIN - 1 - g) * Q_ROWS)
            for h in range(A_HEADS):
                bias_sc[g, h // 2, (h % 2) * Q_ROWS:(h % 2 + 1) * Q_ROWS, :] = _bias_tile(u_ref, h, ok)

    ks = [r[...] for r in k_refs]
    vs = [r[...] for r in v_refs]
    for sub in range(ATTN_SUB):
        rows = slice(sub * Q_ROWS, (sub + 1) * Q_ROWS)
        kcat = jnp.concatenate(ks[sub:sub + ATTN_WIN], axis=0)
        vcat = jnp.concatenate(vs[sub:sub + ATTN_WIN], axis=0)
        bias = bias_sc.at[jnp.minimum(ATTN_SUB * j + sub, ATTN_WIN - 1)]
        o_ref[rows, :] = _attend(q_ref[rows, :], kcat, vcat, bias).astype(BF16)


def _attn_prompt(u, q, k, v, n_steps):
    const = lambda j: (0, 0)
    n_blk = ATTN_SUB + ATTN_WIN - 1
    blk = lambda d: pl.BlockSpec((Q_ROWS, A_WIDTH),
                                 lambda j, d=d: (jnp.maximum(ATTN_SUB * j - (ATTN_WIN - 1) + d, 0), 0))
    step_rows = ATTN_SUB * Q_ROWS
    return pl.pallas_call(
        _attn_prompt_kernel,
        grid=(n_steps,),
        in_specs=[pl.BlockSpec((A_HEADS, ROLL_W), const), pl.BlockSpec((step_rows, A_WIDTH), lambda j: (j, 0))]
                 + [blk(d) for d in range(n_blk)] * 2,
        out_specs=pl.BlockSpec((step_rows, A_WIDTH), lambda j: (j, 0)),
        out_shape=jax.ShapeDtypeStruct((n_steps * step_rows, A_WIDTH), BF16),
        scratch_shapes=[pltpu.VMEM((ATTN_WIN, A_HEADS // 2, 2 * Q_ROWS, ATTN_WIN * Q_ROWS), F32)],
        compiler_params=_params(),
        name="attn_prompt",
    )(u, q, *([k] * n_blk), *([v] * n_blk))


SAMPLE_KEYS = BAND_PAST + 2 * CHUNK
SAMPLE_STREAMS = 4


def _attn_sample_kernel(u_ref, q_ref, kn_ref, vn_ref, kc_ref, vc_ref, o_ref, bias_sc):
    @pl.when(pl.program_id(0) == 0)
    def _():
        ok = _band_mask(CHUNK, SAMPLE_KEYS, 0)
        for p in range(A_HEADS // 2):
            pair = jnp.concatenate([_bias_tile(u_ref, 2 * p, ok), _bias_tile(u_ref, 2 * p + 1, ok)], axis=0)
            bias_sc[p] = pair.T

    pad = jnp.zeros((CHUNK, A_WIDTH), BF16)
    lane = lax.broadcasted_iota(jnp.int32, (CHUNK, LANES), 1)
    first = lane < A_HEAD_DIM
    zero = jnp.zeros((CHUNK, LANES), BF16)
    for n in range(SAMPLE_STREAMS):
        rows = slice(n * CHUNK, (n + 1) * CHUNK)
        kcat = jnp.concatenate([kc_ref[n].astype(BF16), kn_ref[rows, :], pad], axis=0)
        vcat = jnp.concatenate([vc_ref[n].astype(BF16), vn_ref[rows, :], pad], axis=0)
        q = q_ref[rows, :]
        outs = []
        for p in range(A_HEADS // 2):
            lanes = slice(p * LANES, (p + 1) * LANES)
            qp = q[:, lanes]
            q_rows = jnp.concatenate([jnp.where(first, qp, zero), jnp.where(first, zero, qp)], axis=0)
            s = _dot_nt(kcat[:, lanes], q_rows) + bias_sc[p]
            e = jnp.exp2(s - jnp.max(s, axis=0, keepdims=True))
            pn = (e * (1.0 / jnp.sum(e, axis=0, keepdims=True))).astype(BF16)
            r = _dot_tn(pn, vcat[:, lanes])
            outs.append(jnp.where(first, r[0:CHUNK], r[CHUNK:2 * CHUNK]))
        o_ref[rows, :] = jnp.concatenate(outs, axis=-1).astype(BF16)


def _attn_sample(u, q, k, v, kc, vc, first_chunk, n_seq):
    assert n_seq % SAMPLE_STREAMS == 0 and first_chunk % SAMPLE_STREAMS == 0
    rows = SAMPLE_STREAMS * CHUNK
    new = pl.BlockSpec((rows, A_WIDTH), lambda b: (first_chunk // SAMPLE_STREAMS + b, 0))
    cache = pl.BlockSpec((SAMPLE_STREAMS, BAND_PAST, A_WIDTH), lambda b: (b, 0, 0))
    return pl.pallas_call(
        _attn_sample_kernel,
        grid=(n_seq // SAMPLE_STREAMS,),
        in_specs=[pl.BlockSpec((A_HEADS, ROLL_W), lambda b: (0, 0)), new, new, new, cache, cache],
        out_specs=pl.BlockSpec((rows, A_WIDTH), lambda b: (b, 0)),
        out_shape=jax.ShapeDtypeStruct((n_seq * CHUNK, A_WIDTH), BF16),
        scratch_shapes=[pltpu.VMEM((A_HEADS // 2, SAMPLE_KEYS, LANES), F32)],
        compiler_params=_params(),
        name="attn_sample",
    )(u, q, k, v, kc, vc)


GLA_CHUNKS = 4
GLA_SUB = 8


def _gla_block(n_chunks, gla_ref, la_ref, ltri_ref, g_ref, st_sc, o_ref):
    rows = n_chunks * CHUNK
    la = la_ref[...]
    la_hi, la_lo = _split(la)
    b = _dot(ltri_ref[...], la_hi) + _dot(ltri_ref[...], la_lo)
    b3 = b.reshape(n_chunks, CHUNK, B_KWIDTH)
    b_mid = b3[:, CHUNK // 2 - 1:CHUNK // 2, :]
    b_last = b3[:, CHUNK - 1:CHUNK, :]
    q = gla_ref[:, 0:B_KWIDTH].astype(F32).reshape(n_chunks, CHUNK, B_KWIDTH)
    k = gla_ref[:, B_KWIDTH:2 * B_KWIDTH].astype(F32).reshape(n_chunks, CHUNK, B_KWIDTH)
    q_start = (q * jnp.exp(b3)).reshape(rows, B_KWIDTH).astype(BF16)
    q_mid = (q * jnp.exp(b3 - b_mid)).reshape(rows, B_KWIDTH).astype(BF16)
    k_mid = (k * jnp.exp(b_mid - b3)).reshape(rows, B_KWIDTH).astype(BF16)
    k_end = (k * jnp.exp(b_last - b3)).reshape(rows, B_KWIDTH).astype(BF16)
    dec = jnp.exp(b_last)

    ti = lax.broadcasted_iota(jnp.int32, (2 * rows, rows), 0) & (rows - 1)
    si = lax.broadcasted_iota(jnp.int32, (2 * rows, rows), 1)
    causal = (si <= ti) & ((si >> LOG_CHUNK) == (ti >> LOG_CHUNK))
    first_r = lax.broadcasted_iota(jnp.int32, (rows, LANES), 1) < B_DK
    first_c = lax.broadcasted_iota(jnp.int32, (CHUNK, LANES), 1) < B_DK
    first_s = lax.broadcasted_iota(jnp.int32, (B_DV, LANES), 1) < B_DK

    def stack_heads(x, first):
        zero = jnp.zeros_like(x)
        return jnp.concatenate([jnp.where(first, x, zero), jnp.where(first, zero, x)], axis=0)

    for p in range(B_HEADS // 2):
        lanes = slice(p * LANES, (p + 1) * LANES)
        qs_p, qm_p, km_p, ke_p = q_start[:, lanes], q_mid[:, lanes], k_mid[:, lanes], k_end[:, lanes]
        v_pair = gla_ref[:, 2 * B_KWIDTH + 2 * p * B_DV:2 * B_KWIDTH + (2 * p + 2) * B_DV]
        sc = jnp.where(causal, _dot_nt(stack_heads(qm_p, first_r), km_p), 0.0)
        o2 = _dot(sc.astype(BF16), v_pair)
        intra = [o2[0:rows, 0:B_DV], o2[rows:2 * rows, B_DV:2 * B_DV]]
        inter = [[], []]
        st = st_sc[p]
        for c in range(n_chunks):
            cr = slice(c * CHUNK, (c + 1) * CHUNK)
            r2 = _dot_nt(stack_heads(qs_p[cr], first_c), st.astype(BF16))
            inter[0].append(r2[0:CHUNK])
            inter[1].append(r2[CHUNK:2 * CHUNK])
            u2 = _dot_tn(v_pair[cr], ke_p[cr])
            st = st * dec[c, :, lanes] + jnp.where(first_s, u2[0:B_DV], u2[B_DV:2 * B_DV])
        st_sc[p] = st
        for hh in range(2):
            h = 2 * p + hh
            o = intra[hh] + jnp.concatenate(inter[hh], axis=0)
            ms = jnp.mean(o * o, axis=-1, keepdims=True)
            on = o * lax.rsqrt(ms + EPS) * g_ref[...]
            r = gla_ref[:, 2 * B_KWIDTH + B_WIDTH + h * B_DV:2 * B_KWIDTH + B_WIDTH + (h + 1) * B_DV]
            o_ref[:, h * B_DV:(h + 1) * B_DV] = (on * _silu(r.astype(F32))).astype(BF16)


def _gla_prompt_kernel(gla_ref, la_ref, ltri_ref, g_ref, o_ref, sfin_ref, st_sc):
    @pl.when(pl.program_id(0) == 0)
    def _():
        st_sc[...] = jnp.zeros_like(st_sc)

    rows = GLA_CHUNKS * CHUNK
    for sub in range(GLA_SUB):
        part = pl.ds(sub * rows, rows)
        _gla_block(GLA_CHUNKS, gla_ref.at[part], la_ref.at[part], ltri_ref, g_ref, st_sc, o_ref.at[part])
    sfin_ref[...] = st_sc[...]


def _gla_sample_kernel(gla_ref, la_ref, ltri_ref, g_ref, s0_ref, o_ref, sfin_ref, st_sc):
    st_sc[...] = s0_ref[...]
    for n in range(SAMPLE_STREAMS):
        part = pl.ds(n * CHUNK, CHUNK)
        _gla_block(1, gla_ref.at[part], la_ref.at[part], ltri_ref, g_ref, st_sc.at[n], o_ref.at[part])
    sfin_ref[...] = st_sc[...]


def _ltri(n_chunks):
    r = np.arange(n_chunks * CHUNK)
    m = (r[None, :] <= r[:, None]) & (r[None, :] // CHUNK == r[:, None] // CHUNK)
    return jnp.asarray(m, BF16)


_GLA_W = 2 * B_KWIDTH + 2 * B_WIDTH
_ST_SHAPE = (B_HEADS // 2, B_DV, LANES)


def _gla_prompt(gla, la, g, n_steps):
    rows = GLA_SUB * GLA_CHUNKS * CHUNK
    const = lambda j: (0, 0)
    return pl.pallas_call(
        _gla_prompt_kernel,
        grid=(n_steps,),
        in_specs=[pl.BlockSpec((rows, _GLA_W), lambda j: (j, 0)),
                  pl.BlockSpec((rows, B_KWIDTH), lambda j: (j, 0)),
                  pl.BlockSpec((GLA_CHUNKS * CHUNK, GLA_CHUNKS * CHUNK), const),
                  pl.BlockSpec((1, B_DV), const)],
        out_specs=[pl.BlockSpec((rows, B_WIDTH), lambda j: (j, 0)),
                   pl.BlockSpec(_ST_SHAPE, lambda j: (0, 0, 0))],
        out_shape=[jax.ShapeDtypeStruct((n_steps * rows, B_WIDTH), BF16),
                   jax.ShapeDtypeStruct(_ST_SHAPE, F32)],
        scratch_shapes=[pltpu.VMEM(_ST_SHAPE, F32)],
        compiler_params=_params(),
        name="gla_prompt",
    )(gla, la, _ltri(GLA_CHUNKS), g)


def _gla_sample(gla, la, g, s0, first_chunk, n_seq):
    assert n_seq % SAMPLE_STREAMS == 0 and first_chunk % SAMPLE_STREAMS == 0
    const = lambda b: (0, 0)
    rows = SAMPLE_STREAMS * CHUNK
    first = first_chunk // SAMPLE_STREAMS
    st_spec = pl.BlockSpec((SAMPLE_STREAMS,) + _ST_SHAPE, lambda b: (b, 0, 0, 0))
    return pl.pallas_call(
        _gla_sample_kernel,
        grid=(n_seq // SAMPLE_STREAMS,),
        in_specs=[pl.BlockSpec((rows, _GLA_W), lambda b: (first + b, 0)),
                  pl.BlockSpec((rows, B_KWIDTH), lambda b: (first + b, 0)),
                  pl.BlockSpec((CHUNK, CHUNK), const),
                  pl.BlockSpec((1, B_DV), const),
                  st_spec],
        out_specs=[pl.BlockSpec((rows, B_WIDTH), lambda b: (b, 0)), st_spec],
        out_shape=[jax.ShapeDtypeStruct((n_seq * CHUNK, B_WIDTH), BF16),
                   jax.ShapeDtypeStruct((n_seq,) + _ST_SHAPE, F32)],
        scratch_shapes=[pltpu.VMEM((SAMPLE_STREAMS,) + _ST_SHAPE, F32)],
        compiler_params=_params(),
        name="gla_sample",
    )(gla, la, _ltri(1), g, s0)


def _state_to_pairs(s):
    lead = s.shape[:-3]
    s = s.reshape(lead + (B_HEADS // 2, 2, B_DK, B_DV))
    s = jnp.moveaxis(s, -1, -3)
    return s.reshape(lead + (B_HEADS // 2, B_DV, 2 * B_DK))


def _pairs_to_state(s):
    lead = s.shape[:-3]
    s = s.reshape(lead + (B_HEADS // 2, B_DV, 2, B_DK))
    s = jnp.moveaxis(s, -3, -1)
    return s.reshape(lead + (B_HEADS, B_DK, B_DV))


def _route(logits):
    lane = lax.broadcasted_iota(jnp.int32, logits.shape, 1)
    lane_f = lane.astype(F32)
    big = float(LANES)
    gmask = lane < N_GROUPS
    gl = jnp.where(gmask, logits, NEG)
    gmax = jnp.max(gl, axis=-1, keepdims=True)
    gsel = jnp.min(jnp.where(gl == gmax, lane_f, big), axis=-1, keepdims=True)
    gsum = jnp.sum(jnp.where(gmask, jnp.exp(gl - gmax), 0.0), axis=-1, keepdims=True)
    g_w = 1.0 / gsum
    e_lo = ROUTE_OFF + gsel * EXPERTS_PER_GROUP
    emask = (lane_f >= e_lo) & (lane_f < e_lo + EXPERTS_PER_GROUP)
    el = jnp.where(emask, logits, NEG)
    v1 = jnp.max(el, axis=-1, keepdims=True)
    i1 = jnp.min(jnp.where(el == v1, lane_f, big), axis=-1, keepdims=True)
    el2 = jnp.where(lane_f == i1, NEG, el)
    v2 = jnp.max(el2, axis=-1, keepdims=True)
    i2 = jnp.min(jnp.where(el2 == v2, lane_f, big), axis=-1, keepdims=True)
    t = jnp.exp(v2 - v1)
    w1 = g_w / (1.0 + t)
    w2 = g_w * t / (1.0 + t)
    return lane_f, i1, i2, w1, w2


ROW_PIECES = D_MODEL // 2 // LANES
SUBLANES = 8
ROW_TILE = ROW_PIECES * SUBLANES


def _pack_rows(z32_sc, x, rows):
    half = D_MODEL // 2
    out = []
    for c in range(ROW_PIECES):
        z32_sc[c, pl.ds(0, rows, stride=2), :] = x[:, c * LANES:(c + 1) * LANES]
        z32_sc[c, pl.ds(1, rows, stride=2), :] = x[:, half + c * LANES:half + (c + 1) * LANES]
        out.append(z32_sc[c].astype(BF16))
    return out


def _unpack_rows(z32_sc, pieces, rows):
    lo, hi = [], []
    for c in range(ROW_PIECES):
        z32_sc[c] = pieces[c].astype(F32)
        lo.append(z32_sc[c, pl.ds(0, rows, stride=2), :])
        hi.append(z32_sc[c, pl.ds(1, rows, stride=2), :])
    return jnp.concatenate(lo, axis=1), jnp.concatenate(hi, axis=1)


def _to_row_tiled(pieces, tokens):
    per_tile = pieces[0].shape[0] * SUBLANES // tokens
    return jnp.stack([p.reshape(tokens // SUBLANES, per_tile, LANES) for p in pieces], axis=1)


def _from_row_tiled(flat, tokens):
    per_tile = flat.shape[0] // (tokens // SUBLANES) // ROW_PIECES
    tiled = flat.reshape(tokens // SUBLANES, ROW_PIECES, per_tile, LANES)
    return [tiled[:, c].reshape(tokens // SUBLANES * per_tile, LANES) for c in range(ROW_PIECES)]


def _flatten_tiled(tiled):
    return tiled.reshape(-1, LANES)


def _outproj_kernel(n_ptiles, tiles_per_sb, oap_ref, oas_ref, obp_ref, obs_ref, wo_ref, xp_ref, xs_ref, mod_ref,
                    gffn_ref, wr_ref, br_ref, ltri_ref, x1_ref, h2p_ref, meta_ref, cnt_ref, z32_sc, cnt_sc):
    j = pl.program_id(0)
    is_prompt = j < n_ptiles // OUT_SUB
    for s in range(OUT_SUB):
        rows = slice(s * TOK_TILE, (s + 1) * TOK_TILE)
        mrows = slice(s * ROWS_PER_TILE, (s + 1) * ROWS_PER_TILE)
        x = jnp.where(is_prompt, xp_ref[rows, :], xs_ref[rows, :])
        oa = jnp.where(is_prompt, oap_ref[rows, :], oas_ref[rows, :])
        ob = jnp.where(is_prompt, obp_ref[rows, :], obs_ref[rows, :])
        mix = _dot(oa, wo_ref[0:A_WIDTH, :]) + _dot(ob, wo_ref[A_WIDTH:D_MODEL, :])
        gate1 = _rows_to_tokens(mod_ref[mrows, 2 * D_MODEL:3 * D_MODEL], D_MODEL)
        x1 = x + gate1 * mix
        x1_ref[rows, :] = x1
        ms = jnp.mean(x1 * x1, axis=-1, keepdims=True)
        xn = x1 * lax.rsqrt(ms + EPS) * gffn_ref[...]
        sh = _rows_to_tokens(mod_ref[mrows, 3 * D_MODEL:4 * D_MODEL], D_MODEL)
        sc = _rows_to_tokens(mod_ref[mrows, 4 * D_MODEL:5 * D_MODEL], D_MODEL)
        h2 = xn * (1.0 + sc) + sh
        words = [pltpu.bitcast(p, U32) for p in _pack_rows(z32_sc.at[s], h2, TOK_TILE)]
        tiles = slice(s * TOK_TILE // SUBLANES, (s + 1) * TOK_TILE // SUBLANES)
        h2p_ref[tiles] = _to_row_tiled(words, TOK_TILE)

        lane_f, i1, i2, w1, w2 = _route(_dot3(h2, wr_ref[...]) + br_ref[...])

        @pl.when(lax.rem(OUT_SUB * j + s, tiles_per_sb) == 0)
        def _():
            cnt_sc[...] = jnp.zeros_like(cnt_sc)

        sel = jnp.where((lane_f == i1) | (lane_f == i2), 1.0, 0.0).astype(BF16)
        before = _dot(ltri_ref[...], sel) + cnt_sc[0:1, :]
        rank1 = jnp.sum(jnp.where(lane_f == i1, before, 0.0), axis=-1, keepdims=True)
        rank2 = jnp.sum(jnp.where(lane_f == i2, before, 0.0), axis=-1, keepdims=True)
        cnt = cnt_sc[...] + _dot(jnp.ones((8, TOK_TILE), BF16), sel)
        cnt_sc[...] = cnt
        cnt_ref[0] = cnt
        cols = (i1, i2, rank1, rank2, w1, w2)
        meta = jnp.zeros_like(lane_f)
        for c, col in enumerate(cols):
            meta = jnp.where(lane_f == float(c), col, meta)
        meta_ref[rows, :] = meta


def _outproj(oa_p, oa_s, ob_p, ob_s, w_out, xp, xs, mod, gffn, wr, br, n_ptiles, n_stiles, prep, sb):
    n_tiles = n_ptiles + n_stiles
    t = n_tiles * TOK_TILE
    tiles_per_sb = sb // TOK_TILE
    assert n_ptiles % OUT_SUB == 0 and n_stiles % OUT_SUB == 0 and tiles_per_sb % OUT_SUB == 0
    step_rows = OUT_SUB * TOK_TILE
    mod_rows = OUT_SUB * ROWS_PER_TILE
    n_psteps = n_ptiles // OUT_SUB
    pblocks = prep // mod_rows
    const = lambda j: (0, 0)
    row = lambda j: (j, 0)
    prow = lambda j: (jnp.minimum(j, n_psteps - 1), 0)
    srow = lambda j: (jnp.maximum(j - n_psteps, 0), 0)
    r = np.arange(TOK_TILE)
    ltri = jnp.asarray(r[None, :] < r[:, None], BF16)
    return pl.pallas_call(
        functools.partial(_outproj_kernel, n_ptiles, tiles_per_sb),
        grid=(n_tiles // OUT_SUB,),
        in_specs=[pl.BlockSpec((step_rows, A_WIDTH), prow),
                  pl.BlockSpec((step_rows, A_WIDTH), srow),
                  pl.BlockSpec((step_rows, B_WIDTH), prow),
                  pl.BlockSpec((step_rows, B_WIDTH), srow),
                  pl.BlockSpec((D_MODEL, D_MODEL), const),
                  pl.BlockSpec((step_rows, D_MODEL), prow),
                  pl.BlockSpec((step_rows, D_MODEL), srow),
                  pl.BlockSpec((mod_rows, 6 * D_MODEL),
                               lambda j: (jnp.maximum(j - n_psteps + pblocks, 0), 0)),
                  pl.BlockSpec((1, D_MODEL), const),
                  pl.BlockSpec((D_MODEL, LANES), const),
                  pl.BlockSpec((1, LANES), const),
                  pl.BlockSpec((TOK_TILE, TOK_TILE), const)],
        out_specs=[pl.BlockSpec((step_rows, D_MODEL), row),
                   pl.BlockSpec((step_rows // SUBLANES, ROW_PIECES, SUBLANES, LANES), lambda j: (j, 0, 0, 0)),
                   pl.BlockSpec((step_rows, LANES), row),
                   pl.BlockSpec((1, 8, LANES), lambda j: (OUT_SUB * j // tiles_per_sb, 0, 0))],
        out_shape=[jax.ShapeDtypeStruct((t, D_MODEL), F32),
                   jax.ShapeDtypeStruct((t // SUBLANES, ROW_PIECES, SUBLANES, LANES), U32),
                   jax.ShapeDtypeStruct((t, LANES), F32),
                   jax.ShapeDtypeStruct((t // sb, 8, LANES), F32)],
        scratch_shapes=[pltpu.VMEM((OUT_SUB, ROW_PIECES, 2 * TOK_TILE, LANES), F32),
                        pltpu.VMEM((8, LANES), F32)],
        compiler_params=_params(),
        name="outproj",
    )(oa_p, oa_s, ob_p, ob_s, w_out, xp, xs, mod, gffn, wr, br, ltri)


MOE_SUPER_BLOCK = 2048
SEG_ALIGN = SUBLANES
CHUNK_BF16_ROWS = 2 * SEG_ALIGN * ROW_PIECES
SEG_BITS = 9
PAD_BITS = 5
FFN_ROWS = 512
PLAN_ROWS = LANES


def _local_rows(sb):
    return 2 * sb + N_EXPERTS * SEG_ALIGN


def _sorted_tiles(n_tokens, sb):
    rows = 2 * n_tokens + (n_tokens // sb) * N_EXPERTS * SEG_ALIGN + N_EXPERTS * FFN_ROWS
    return -(-rows // (FFN_SUB * FFN_ROWS)) * FFN_SUB


def _moe_plan_kernel(total_chunks, meta_ref, cnt_ref, ustrict_ref, lstrict_ref,
                     posw_ref, addr_ref, tab_ref, tile_ref):
    b = pl.program_id(0)
    per_tile = FFN_ROWS // SEG_ALIGN

    @pl.when(b == 0)
    def _():
        cnt = cnt_ref[...]
        chunks = jnp.floor((cnt + (SEG_ALIGN - 1)) * (1.0 / SEG_ALIGN))
        chunks_b = chunks.astype(BF16)
        loc = _dot(chunks_b, ustrict_ref[...])
        before = _dot(lstrict_ref[...], chunks_b)
        tot = _dot(jnp.ones((PLAN_ROWS, PLAN_ROWS), BF16), chunks_b)
        tiles = jnp.floor((tot + (per_tile - 1)) * (1.0 / per_tile))
        tile_off = _dot(tiles.astype(BF16), ustrict_ref[...])
        n_tiles = jnp.sum(tiles[0:1], axis=-1, keepdims=True)
        lane1 = lax.broadcasted_iota(jnp.int32, (PLAN_ROWS, LANES), 1)
        tail = lane1 == ROUTE_OFF + N_EXPERTS
        pad_off = jnp.where(tail, n_tiles * per_tile, tile_off * per_tile + tot)
        pad_n = jnp.where(tail, total_chunks - n_tiles * per_tile, tiles * per_tile - tot)
        row = lax.broadcasted_iota(jnp.int32, (PLAN_ROWS, LANES), 0)
        tab_ref[0] = loc
        tab_ref[1] = chunks
        tab_ref[2] = tile_off * per_tile + before
        tab_ref[3] = jnp.where(row == 0, pad_off, jnp.where(row == 1, pad_n, jnp.where(row == 2, n_tiles, 0.0)))
        t_idx = lax.broadcasted_iota(jnp.int32, tile_ref.shape, 0).astype(F32)
        lane_t = lax.broadcasted_iota(jnp.int32, tile_ref.shape, 1)
        is_expert = (lane_t >= ROUTE_OFF) & (lane_t < ROUTE_OFF + N_EXPERTS)
        ends = (tile_off + tiles)[0:1, :]
        owner = jnp.sum(jnp.where(is_expert & (ends <= t_idx), 1.0, 0.0), axis=-1, keepdims=True)
        tile_ref[...] = jnp.broadcast_to(jnp.minimum(owner, N_EXPERTS - 1.0), tile_ref.shape)

    own = jnp.floor((cnt_ref[pl.ds(b, 1), :] + (SEG_ALIGN - 1)) * (1.0 / SEG_ALIGN))
    own_off = _dot(jnp.broadcast_to(own, (SUBLANES, LANES)).astype(BF16), ustrict_ref[...]) * SEG_ALIGN
    meta = meta_ref[...]
    lane_f = lax.broadcasted_iota(jnp.int32, meta.shape, 1).astype(F32)
    off_row = own_off[0:1, :]
    pos = []
    for k in range(2):
        e_lane = meta[:, k:k + 1]
        base = jnp.sum(jnp.where(lane_f == e_lane, off_row, 0.0), axis=-1, keepdims=True)
        p = base + meta[:, 2 + k:3 + k]
        tile = jnp.floor(p * (1.0 / SUBLANES))
        pos.append(tile * (ROW_TILE - SUBLANES) + p)
    out = jnp.zeros_like(meta)
    for c, col in enumerate((pos[0], pos[1], meta[:, 4:5], meta[:, 5:6])):
        out = jnp.where(lane_f == float(c), col, out)
    posw_ref[...] = out
    addr_ref[0] = out.T[0:SUBLANES]


def _moe_plan(meta, cnt, sb):
    n_blocks = meta.shape[0] // sb
    assert n_blocks <= PLAN_ROWS and sb // SEG_ALIGN <= 256
    n_tiles = _sorted_tiles(meta.shape[0], sb)
    tile_rows = -(-n_tiles // SUBLANES) * SUBLANES
    r = np.arange(LANES)
    ustrict = jnp.asarray(r[:, None] < r[None, :], BF16)
    lstrict = jnp.asarray(r[None, :] < r[:, None], BF16)
    cnt_all = jnp.pad(cnt[:, 0, :], ((0, PLAN_ROWS - n_blocks), (0, 0)))
    const = lambda s: (0, 0)
    posw, addr, tab, tile_owner = pl.pallas_call(
        functools.partial(_moe_plan_kernel, float(n_tiles * (FFN_ROWS // SEG_ALIGN))),
        grid=(n_blocks,),
        in_specs=[pl.BlockSpec((sb, LANES), lambda s: (s, 0)),
                  pl.BlockSpec((PLAN_ROWS, LANES), const),
                  pl.BlockSpec((LANES, LANES), const),
                  pl.BlockSpec((PLAN_ROWS, PLAN_ROWS), const)],
        out_specs=[pl.BlockSpec((sb, LANES), lambda s: (s, 0)),
                   pl.BlockSpec((1, SUBLANES, sb), lambda s: (s, 0, 0)),
                   pl.BlockSpec((4, PLAN_ROWS, LANES), lambda s: (0, 0, 0)),
                   pl.BlockSpec((tile_rows, LANES), const)],
        out_shape=[jax.ShapeDtypeStruct(meta.shape, F32),
                   jax.ShapeDtypeStruct((n_blocks, SUBLANES, sb), F32),
                   jax.ShapeDtypeStruct((4, PLAN_ROWS, LANES), F32),
                   jax.ShapeDtypeStruct((tile_rows, LANES), F32)],
        compiler_params=_params(),
        name="moe_plan",
    )(meta, cnt_all, ustrict, lstrict)
    experts = slice(ROUTE_OFF, ROUTE_OFF + N_EXPERTS)
    to_i32 = lambda x: x.astype(jnp.int32).reshape(-1)
    plan = dict(
        loc=to_i32(tab[0, :n_blocks, experts]), n=to_i32(tab[1, :n_blocks, experts]),
        dst=to_i32(tab[2, :n_blocks, experts]),
        pad_off=to_i32(tab[3, 0, ROUTE_OFF:ROUTE_OFF + N_EXPERTS + 1]),
        pad_n=to_i32(tab[3, 1, ROUTE_OFF:ROUTE_OFF + N_EXPERTS + 1]),
        n_tiles=to_i32(tab[3, 2, 0:1]),
        owner=to_i32(tile_owner[:n_tiles, 0]))
    return posw, to_i32(addr[:, 0, :]), to_i32(addr[:, 1, :]), plan


def _token_rows(start):
    return pl.ds(start, ROW_PIECES, stride=SUBLANES)


def _pow2_copies(src_ref, dst_ref, src_chunk, dst_chunk, n, n_bits, sem, act):
    done = 0
    for k in reversed(range(n_bits)):
        take = (n >> k) & 1
        rows = CHUNK_BF16_ROWS << k
        src0 = 0 if src_chunk is None else pl.multiple_of((src_chunk + done) * CHUNK_BF16_ROWS, CHUNK_BF16_ROWS)
        dst0 = pl.multiple_of((dst_chunk + done) * CHUNK_BF16_ROWS, CHUNK_BF16_ROWS)

        @pl.when(take == 1)
        def _(src0=src0, dst0=dst0, rows=rows):
            act(pltpu.make_async_copy(src_ref.at[pl.ds(src0, rows)], dst_ref.at[pl.ds(dst0, rows)], sem))

        done = done + take * (1 << k)


def _segment_copies(block, loc_ref, n_ref, dst_ref, local_ref, global_ref, to_global, sem, act):
    def per_expert(e, carry):
        seg = block * N_EXPERTS + e
        if to_global:
            _pow2_copies(local_ref, global_ref, loc_ref[seg], dst_ref[seg], n_ref[seg], SEG_BITS, sem, act)
        else:
            _pow2_copies(global_ref, local_ref, dst_ref[seg], loc_ref[seg], n_ref[seg], SEG_BITS, sem, act)
        return carry

    lax.fori_loop(0, N_EXPERTS, per_expert, 0)


def _zero_fill(zero_ref, global_ref, padoff_ref, padn_ref, sem, act):
    full = 1 << PAD_BITS

    def per_pad(e, carry):
        def per_full(c, inner):
            dst0 = pl.multiple_of((padoff_ref[e] + c * full) * CHUNK_BF16_ROWS, CHUNK_BF16_ROWS)
            act(pltpu.make_async_copy(zero_ref, global_ref.at[pl.ds(dst0, full * CHUNK_BF16_ROWS)], sem))
            return inner

        n_full = padn_ref[e] >> PAD_BITS
        lax.fori_loop(0, n_full, per_full, 0)
        _pow2_copies(zero_ref, global_ref, None, padoff_ref[e] + n_full * full, padn_ref[e] & (full - 1),
                     PAD_BITS, sem, act)
        return carry

    lax.fori_loop(0, N_EXPERTS + 1, per_pad, 0)


STAGE_SLAB = 1024


def _restage(src_sc, dst_sc, dst_dtype):
    ratio = dst_sc.shape[0] / src_sc.shape[0]
    n_slabs = src_sc.shape[0] // (STAGE_SLAB if ratio > 1 else 2 * STAGE_SLAB)
    src_rows = src_sc.shape[0] // n_slabs
    dst_rows = dst_sc.shape[0] // n_slabs

    def slab(i, carry):
        s0 = pl.multiple_of(i * src_rows, src_rows)
        d0 = pl.multiple_of(i * dst_rows, dst_rows)
        dst_sc[pl.ds(d0, dst_rows), :] = pltpu.bitcast(src_sc[pl.ds(s0, src_rows), :], dst_dtype)
        return carry

    lax.fori_loop(0, n_slabs, slab, 0)


def _moe_dispatch_kernel(sb, n_blocks, loc_ref, n_ref, dst_ref, padoff_ref, padn_ref,
                         h2p_ref, a1_ref, a2_ref, xs_hbm, local_sc, stage_sc, zero_sc, sems, zero_sem):
    b = pl.program_id(0)
    slot = b & 1

    def segments(block, buf, act):
        _segment_copies(block, loc_ref, n_ref, dst_ref, stage_sc.at[buf], xs_hbm, True, sems.at[buf], act)

    local_sc[...] = jnp.zeros_like(local_sc)

    def step(g, carry):
        src = pl.multiple_of(g * ROW_TILE, ROW_TILE)
        for u in range(SUBLANES):
            t = g * SUBLANES + u
            row = h2p_ref[_token_rows(src + u), :]
            local_sc[_token_rows(a1_ref[t]), :] = row
            local_sc[_token_rows(a2_ref[t]), :] = row
        return carry

    lax.fori_loop(0, sb // SUBLANES, step, 0)

    @pl.when(b > 0)
    def _():
        segments(b - 1, 1 - slot, lambda c: c.wait())

    _restage(local_sc, stage_sc.at[slot], BF16)
    segments(b, slot, lambda c: c.start())

    @pl.when(b == 0)
    def _():
        zero_sc[...] = jnp.zeros_like(zero_sc)
        _zero_fill(zero_sc, xs_hbm, padoff_ref, padn_ref, zero_sem, lambda c: c.start())
        _zero_fill(zero_sc, xs_hbm, padoff_ref, padn_ref, zero_sem, lambda c: c.wait())

    @pl.when(b == n_blocks - 1)
    def _():
        segments(b, slot, lambda c: c.wait())


def _smem_vec(n, index_map):
    return pl.BlockSpec((n,), index_map, memory_space=pltpu.SMEM)


def _moe_dispatch(h2p, a1, a2, plan, sb, n_tiles):
    n_blocks = h2p.shape[0] // (sb * ROW_PIECES)
    local_flat = _local_rows(sb) * ROW_PIECES
    vec = _smem_vec(sb, lambda s, *_: (s,))
    return pl.pallas_call(
        functools.partial(_moe_dispatch_kernel, sb, n_blocks),
        grid_spec=pltpu.PrefetchScalarGridSpec(
            num_scalar_prefetch=5,
            grid=(n_blocks,),
            in_specs=[pl.BlockSpec((sb * ROW_PIECES, LANES), lambda s, *_: (s, 0)), vec, vec],
            out_specs=pl.BlockSpec(memory_space=pl.ANY),
            scratch_shapes=[pltpu.VMEM((local_flat, LANES), U32),
                            pltpu.VMEM((2, 2 * local_flat, LANES), BF16),
                            pltpu.VMEM(((1 << PAD_BITS) * CHUNK_BF16_ROWS, LANES), BF16),
                            pltpu.SemaphoreType.DMA((2,)),
                            pltpu.SemaphoreType.DMA(())]),
        out_shape=jax.ShapeDtypeStruct((n_tiles * FFN_ROWS * ROW_PIECES * 2, LANES), BF16),
        compiler_params=_params(),
        name="moe_dispatch",
    )(plan["loc"], plan["n"], plan["dst"], plan["pad_off"], plan["pad_n"], h2p, a1, a2)


FFN_SUB = 3


def _moe_ffn_kernel(owner_ref, ntiles_ref, xs_ref, *refs):
    w_refs = refs[0:3 * FFN_SUB]
    ys_ref, wg_sc, wu_sc, wd_sc, z32_sc = refs[3 * FFN_SUB:]
    half = D_MODEL // 2
    flat = FFN_ROWS * ROW_PIECES * 2
    for s in range(FFN_SUB):
        t = pl.program_id(0) * FFN_SUB + s
        wg_ref, wu_ref, wd_ref = w_refs[3 * s:3 * s + 3]
        rows = slice(s * flat, (s + 1) * flat)
        used = t < ntiles_ref[0]

        @pl.when(used & ((t < FFN_SUB) | (owner_ref[t] != owner_ref[jnp.maximum(t - FFN_SUB, 0)])))
        def _():
            wg_sc[s] = wg_ref[0].astype(BF16)
            wu_sc[s] = wu_ref[0].astype(BF16)
            wd_sc[s] = wd_ref[0].astype(BF16)

        @pl.when(used)
        def _():
            z32 = z32_sc.at[s]
            lo, hi = _unpack_rows(z32, _from_row_tiled(xs_ref[rows, :], FFN_ROWS), FFN_ROWS)
            lo, hi = lo.astype(BF16), hi.astype(BF16)
            g = _dot(lo, wg_sc[s, 0:half, :]) + _dot(hi, wg_sc[s, half:D_MODEL, :])
            u = _dot(lo, wu_sc[s, 0:half, :]) + _dot(hi, wu_sc[s, half:D_MODEL, :])
            y = _dot((_silu(g) * u).astype(BF16), wd_sc[s])
            ys_ref[rows, :] = _flatten_tiled(_to_row_tiled(_pack_rows(z32, y, FFN_ROWS), FFN_ROWS))

        @pl.when(jnp.logical_not(used))
        def _():
            ys_ref[rows, :] = jnp.zeros((flat, LANES), BF16)


def _moe_ffn(xs, plan, wg, wu, wd):
    flat = FFN_ROWS * ROW_PIECES * 2
    n_tiles = xs.shape[0] // flat
    assert n_tiles % FFN_SUB == 0

    def wspec(shape, s):
        tile = lambda i, owner, nt: jnp.minimum(i * FFN_SUB + s, nt[0] - 1)
        return pl.BlockSpec((1,) + shape, lambda i, owner, nt: (owner[tile(i, owner, nt)], 0, 0))

    w_specs, w_args = [], []
    for s in range(FFN_SUB):
        w_specs += [wspec((D_MODEL, EXPERT_FF), s), wspec((D_MODEL, EXPERT_FF), s), wspec((EXPERT_FF, D_MODEL), s)]
        w_args += [wg, wu, wd]
    last_step = lambda i, owner, nt: jnp.minimum(i, (nt[0] - 1) // FFN_SUB)
    return pl.pallas_call(
        _moe_ffn_kernel,
        grid_spec=pltpu.PrefetchScalarGridSpec(
            num_scalar_prefetch=2,
            grid=(n_tiles // FFN_SUB,),
            in_specs=[pl.BlockSpec((FFN_SUB * flat, LANES), lambda i, owner, nt: (last_step(i, owner, nt), 0))]
                     + w_specs,
            out_specs=pl.BlockSpec((FFN_SUB * flat, LANES), lambda i, owner, nt: (i, 0)),
            scratch_shapes=[pltpu.VMEM((FFN_SUB, D_MODEL, EXPERT_FF), BF16),
                            pltpu.VMEM((FFN_SUB, D_MODEL, EXPERT_FF), BF16),
                            pltpu.VMEM((FFN_SUB, EXPERT_FF, D_MODEL), BF16),
                            pltpu.VMEM((FFN_SUB, ROW_PIECES, 2 * FFN_ROWS, LANES), F32)]),
        out_shape=jax.ShapeDtypeStruct(xs.shape, BF16),
        compiler_params=_params(),
        name="moe_ffn",
    )(plan["owner"], plan["n_tiles"], xs, *w_args)


def _moe_combine_kernel(n_psb, n_blocks, loc_ref, n_ref, dst_ref,
                        ys_hbm, a1_ref, a2_ref, posw_ref, x1_ref, mod_ref, yp_ref, yo_ref,
                        local_sc, stage_sc, g1_sc, g2_sc, z32_sc, sems):
    s = pl.program_id(0)
    slot = s & 1

    def segments(block, buf, act):
        _segment_copies(block, loc_ref, n_ref, dst_ref, stage_sc.at[buf], ys_hbm, False, sems.at[buf], act)

    @pl.when(pl.program_id(1) == 0)
    def _():
        @pl.when(s == 0)
        def _():
            segments(s, slot, lambda c: c.start())

        segments(s, slot, lambda c: c.wait())
        _restage(stage_sc.at[slot], local_sc, U32)

        @pl.when(s + 1 < n_blocks)
        def _():
            segments(s + 1, 1 - slot, lambda c: c.start())

    def step(g, carry):
        dst = pl.multiple_of(g * ROW_TILE, ROW_TILE)
        for u in range(SUBLANES):
            t = g * SUBLANES + u
            g1_sc[_token_rows(dst + u), :] = local_sc[_token_rows(a1_ref[t]), :]
            g2_sc[_token_rows(dst + u), :] = local_sc[_token_rows(a2_ref[t]), :]
        return carry

    lax.fori_loop(0, TOK_TILE // SUBLANES, step, 0)
    halves = lambda g_sc: [pltpu.bitcast(p, BF16) for p in _from_row_tiled(g_sc[...], TOK_TILE)]
    lo1, hi1 = _unpack_rows(z32_sc, halves(g1_sc), TOK_TILE)
    lo2, hi2 = _unpack_rows(z32_sc, halves(g2_sc), TOK_TILE)
    w1, w2 = posw_ref[:, 2:3], posw_ref[:, 3:4]
    moe = jnp.concatenate([w1 * lo1 + w2 * lo2, w1 * hi1 + w2 * hi2], axis=1)
    gate2 = _rows_to_tokens(mod_ref[:, 5 * D_MODEL:6 * D_MODEL], D_MODEL)
    y = x1_ref[...] + gate2 * moe

    @pl.when(s < n_psb)
    def _():
        yp_ref[...] = y

    @pl.when(s >= n_psb)
    def _():
        yo_ref[...] = y


def _moe_combine(ys, a1, a2, posw, plan, x1, mod, sb, n_ptiles, n_stiles, prep):
    tps = sb // TOK_TILE
    n_blocks = (n_ptiles + n_stiles) // tps
    n_psb = n_ptiles // tps
    pblocks = prep // ROWS_PER_TILE
    tile = lambda s, j: s * tps + j
    vec = _smem_vec(TOK_TILE, lambda s, j, *_: (tile(s, j),))
    return pl.pallas_call(
        functools.partial(_moe_combine_kernel, n_psb, n_blocks),
        grid_spec=pltpu.PrefetchScalarGridSpec(
            num_scalar_prefetch=3,
            grid=(n_blocks, tps),
            in_specs=[pl.BlockSpec(memory_space=pl.ANY), vec, vec,
                      pl.BlockSpec((TOK_TILE, LANES), lambda s, j, *_: (tile(s, j), 0)),
                      pl.BlockSpec((TOK_TILE, D_MODEL), lambda s, j, *_: (tile(s, j), 0)),
                      pl.BlockSpec((ROWS_PER_TILE, 6 * D_MODEL),
                                   lambda s, j, *_: (jnp.maximum(tile(s, j) - n_ptiles + pblocks, 0), 0))],
            out_specs=[pl.BlockSpec((TOK_TILE, D_MODEL),
                                    lambda s, j, *_: (jnp.minimum(tile(s, j), n_ptiles - 1), 0)),
                       pl.BlockSpec((TOK_TILE, D_MODEL),
                                    lambda s, j, *_: (jnp.maximum(tile(s, j) - n_ptiles, 0), 0))],
            scratch_shapes=[pltpu.VMEM((_local_rows(sb) * ROW_PIECES, LANES), U32),
                            pltpu.VMEM((2, _local_rows(sb) * ROW_PIECES * 2, LANES), BF16),
                            pltpu.VMEM((TOK_TILE * ROW_PIECES, LANES), U32),
                            pltpu.VMEM((TOK_TILE * ROW_PIECES, LANES), U32),
                            pltpu.VMEM((ROW_PIECES, 2 * TOK_TILE, LANES), F32),
                            pltpu.SemaphoreType.DMA((2,))]),
        out_shape=[jax.ShapeDtypeStruct((n_ptiles * TOK_TILE, D_MODEL), F32),
                   jax.ShapeDtypeStruct((n_stiles * TOK_TILE, D_MODEL), F32)],
        compiler_params=_params(2),
        name="moe_combine",
    )(plan["loc"], plan["n"], plan["dst"], ys, a1, a2, posw, x1, mod)


def _layer(xp, xs, cache_k, cache_v, state, c_prompt, c_sample, norm_mix_g, norm_ffn_g, w_ada, b_ada, w_in,
           q_norm_g, k_norm_g, rel_bias, w_gate_up, b_gate, gla_norm_g, w_out, w_route_group,
           b_route_group, w_route_expert, b_route_expert, w_exp_gate, w_exp_up, w_exp_down):
    batch, seq, _ = xp.shape
    n_seq, dec_seq, _ = xs.shape
    assert batch == 1 and dec_seq == CHUNK and cache_k.shape[1] == BAND_PAST
    assert seq % TOK_TILE == 0 and seq >= BAND_PAST and (n_seq * CHUNK) % TOK_TILE == 0
    assert seq % (ATTN_SUB * Q_ROWS) == 0 and seq % (GLA_SUB * GLA_CHUNKS * CHUNK) == 0
    n_ptok, n_stok = seq, n_seq * CHUNK
    n_ptiles, n_stiles = n_ptok // TOK_TILE, n_stok // TOK_TILE
    sb = (MOE_SUPER_BLOCK if (n_ptok % MOE_SUPER_BLOCK == 0 and n_stok % MOE_SUPER_BLOCK == 0)
          else OUT_SUB * TOK_TILE)
    prep = OUT_SUB * ROWS_PER_TILE

    xp2 = xp.reshape(n_ptok, D_MODEL)
    xs2 = xs.reshape(n_stok, D_MODEL)
    c_rows = jnp.concatenate([jnp.broadcast_to(c_prompt, (prep, D_MODEL)), c_sample], axis=0)
    mod = _adaln(c_rows, w_ada, b_ada)

    w_main = w_in[:, 0:IN_MAIN].astype(BF16)
    w_gr = jnp.pad(w_in[:, IN_MAIN:], ((0, 0), (0, LANES - GATE_RANK))).astype(BF16)
    wgu_p = jnp.pad(w_gate_up, ((0, LANES - GATE_RANK), (0, 0))).astype(BF16)
    head = np.arange(A_WIDTH) // A_HEAD_DIM
    bd = jnp.asarray(head[:, None] == head[None, :], BF16)
    gq = jnp.tile(q_norm_g, A_HEADS).reshape(1, A_WIDTH)
    gk = jnp.tile(k_norm_g, A_HEADS).reshape(1, A_WIDTH)
    q, k, v, kf, vf, gla, la = _inproj(
        xp2, xs2, mod, norm_mix_g.reshape(1, D_MODEL), w_main, w_gr, bd, gq, gk, wgu_p,
        b_gate.reshape(1, B_KWIDTH), n_ptiles, n_stiles, prep)

    first_chunk = n_ptok // CHUNK
    oa_p = _attn_prompt(rel_bias[:, _bias_lanes(ATTN_WIN * Q_ROWS)], q, k, v, n_ptok // (ATTN_SUB * Q_ROWS))
    oa_s = _attn_sample(rel_bias[:, _bias_lanes(SAMPLE_KEYS)], q, k, v,
                        cache_k.reshape(n_seq, BAND_PAST, A_WIDTH), cache_v.reshape(n_seq, BAND_PAST, A_WIDTH),
                        first_chunk, n_seq)
    g_gla = gla_norm_g.reshape(1, B_DV)
    ob_p, sfin_p = _gla_prompt(gla, la, g_gla, n_ptok // (GLA_SUB * GLA_CHUNKS * CHUNK))
    ob_s, sfin_s = _gla_sample(gla, la, g_gla, _state_to_pairs(state), first_chunk, n_seq)

    wr = jnp.pad(jnp.concatenate([w_route_group, w_route_expert], axis=1),
                 ((0, 0), (0, LANES - N_GROUPS - N_EXPERTS)))
    br = jnp.pad(jnp.concatenate([b_route_group, b_route_expert]), (0, LANES - N_GROUPS - N_EXPERTS))
    x1, h2p, meta, cnt = _outproj(oa_p, oa_s, ob_p, ob_s, w_out.astype(BF16), xp2, xs2, mod,
                                  norm_ffn_g.reshape(1, D_MODEL), wr, br.reshape(1, LANES),
                                  n_ptiles, n_stiles, prep, sb)

    posw, a1, a2, plan = _moe_plan(meta, cnt, sb)
    xs_sorted = _moe_dispatch(h2p.reshape(-1, LANES), a1, a2, plan, sb, _sorted_tiles(n_ptok + n_stok, sb))
    ys_sorted = _moe_ffn(xs_sorted, plan, w_exp_gate, w_exp_up, w_exp_down)
    yp, ys = _moe_combine(ys_sorted, a1, a2, posw, plan, x1, mod, sb, n_ptiles, n_stiles, prep)

    tail = min(BAND_PAST, seq)
    heads = (A_HEADS, A_HEAD_DIM)
    return (yp.reshape(1, seq, D_MODEL), ys.reshape(n_seq, CHUNK, D_MODEL),
            kf[TOK_TILE - tail:TOK_TILE].reshape((1, tail) + heads),
            vf[TOK_TILE - tail:TOK_TILE].reshape((1, tail) + heads),
            _pairs_to_state(sfin_p)[None],
            kf[TOK_TILE:].reshape((n_seq, CHUNK) + heads),
            vf[TOK_TILE:].reshape((n_seq, CHUNK) + heads),
            _pairs_to_state(sfin_s))


def kernel(x_prompt, x_sample, cache_a_k, cache_a_v, state_gla, c_prompt, c_sample, norm_mix_g, norm_ffn_g,
           w_ada, b_ada, w_in, q_norm_g, k_norm_g, rel_bias, w_gate_up, b_gate, gla_norm_g, w_out,
           w_route_group, b_route_group, w_route_expert, b_route_expert, w_exp_gate, w_exp_up, w_exp_down):
    depth = w_in.shape[0]
    yp, ys = x_prompt, x_sample
    outs = [[] for _ in range(6)]
    for l in range(depth):
        yp, ys, kp, vp, sp, ks, vs, ss = _layer(
            yp, ys, cache_a_k[l], cache_a_v[l], state_gla[l], c_prompt, c_sample, norm_mix_g[l], norm_ffn_g[l],
            w_ada[l], b_ada[l], w_in[l], q_norm_g[l], k_norm_g[l], rel_bias[l], w_gate_up[l], b_gate[l],
            gla_norm_g[l], w_out[l], w_route_group[l], b_route_group[l], w_route_expert[l], b_route_expert[l],
            w_exp_gate[l], w_exp_up[l], w_exp_down[l])
        for lst, val in zip(outs, (kp, vp, sp, ks, vs, ss)):
            lst.append(val)
    return (yp, ys) + tuple(jnp.stack(o) for o in outs)
```

```python
import functools

import numpy as np
import jax
import jax.numpy as jnp
from jax import lax
from jax.experimental import pallas as pl
from jax.experimental.pallas import tpu as pltpu

F32 = jnp.float32
BF16 = jnp.bfloat16
U32 = jnp.uint32

D_MODEL = 1024
CHUNK = 64
LOG_CHUNK = 6
BAND_CHUNKS = 8
BAND_PAST = BAND_CHUNKS * CHUNK
A_WIDTH = 512
A_HEADS = 8
A_HEAD_DIM = 64
MAX_REL = 128
B_WIDTH = 512
B_HEADS = 4
B_DV = 128
B_DK = 64
B_KWIDTH = 256
GATE_RANK = 16
GATE_TAU = 16.0
N_GROUPS = 4
EXPERTS_PER_GROUP = 8
N_EXPERTS = 32
EXPERT_FF = 256
EPS = 1e-6

LANES = 128
IN_MAIN = 3 * A_WIDTH + 2 * B_KWIDTH + 2 * B_WIDTH
TOK_TILE = 512
ROWS_PER_TILE = TOK_TILE // CHUNK
OUT_SUB = 1
Q_CHUNKS = 4
Q_ROWS = Q_CHUNKS * CHUNK
ROLL_W = 1024
NEG = -1e30
ROUTE_OFF = N_GROUPS
VMEM_LIMIT = 56 * 1024 * 1024


def _params(n_axes=1):
    return pltpu.CompilerParams(dimension_semantics=("arbitrary",) * n_axes,
                                vmem_limit_bytes=VMEM_LIMIT)


def _split(a):
    hi = a.astype(BF16)
    lo = (a - hi.astype(F32)).astype(BF16)
    return hi, lo


def _dot(a, b):
    return jnp.dot(a, b, preferred_element_type=F32)


def _dot3(a, b):
    ah, al = _split(a)
    bh, bl = _split(b)
    return _dot(ah, bh) + _dot(al, bh) + _dot(ah, bl)


def _dot_nt(a, b):
    return lax.dot_general(a, b, (((1,), (1,)), ((), ())), preferred_element_type=F32)


def _dot_tn(a, b):
    return lax.dot_general(a, b, (((0,), (0,)), ((), ())), preferred_element_type=F32)


def _silu(x):
    return x / (1.0 + jnp.exp(-x))


def _rows_to_tokens(rows, n):
    r = rows.shape[0]
    return jnp.broadcast_to(rows[:, None, :], (r, CHUNK, n)).reshape(r * CHUNK, n)


def _adaln_kernel(c_ref, w_ref, b_ref, o_ref):
    a = _silu(c_ref[...])
    o_ref[...] = _dot3(a, w_ref[...]) + b_ref[...]


def _adaln(c_rows, w_ada, b_ada):
    r = c_rows.shape[0]
    n = w_ada.shape[1]
    tn = 1024
    return pl.pallas_call(
        _adaln_kernel,
        grid=(n // tn,),
        in_specs=[pl.BlockSpec((r, D_MODEL), lambda j: (0, 0)),
                  pl.BlockSpec((D_MODEL, tn), lambda j: (0, j)),
                  pl.BlockSpec((1, tn), lambda j: (0, j))],
        out_specs=pl.BlockSpec((r, tn), lambda j: (0, j)),
        out_shape=jax.ShapeDtypeStruct((r, n), F32),
        compiler_params=_params(),
        name="adaln",
    )(c_rows, w_ada, b_ada.reshape(1, n))


def _head_rms(z, bd_ref, g):
    ms = _dot((z * z).astype(BF16), bd_ref[...]) * (1.0 / A_HEAD_DIM)
    return z * lax.rsqrt(ms + EPS) * g


def _inproj_kernel(n_ptiles, xp_ref, xs_ref, mod_ref, gmix_ref, w_ref, wgr_ref, bd_ref, gq_ref, gk_ref,
                   wgu_ref, bg_ref,
                   q_ref, k_ref, v_ref, kf_ref, vf_ref, gla_ref, la_ref):
    i = pl.program_id(0)
    x = jnp.where(i < n_ptiles, xp_ref[...], xs_ref[...])
    ms = jnp.mean(x * x, axis=-1, keepdims=True)
    xn = x * lax.rsqrt(ms + EPS) * gmix_ref[...]
    sh = _rows_to_tokens(mod_ref[:, 0:D_MODEL], D_MODEL)
    sc = _rows_to_tokens(mod_ref[:, D_MODEL:2 * D_MODEL], D_MODEL)
    hb = (xn * (1.0 + sc) + sh).astype(BF16)

    zq = _dot(hb, w_ref[:, 0:A_WIDTH])
    q_ref[...] = (_head_rms(zq, bd_ref, gq_ref[...]) * (LOG2E * A_HEAD_DIM ** -0.5)).astype(BF16)
    zk = _dot(hb, w_ref[:, A_WIDTH:2 * A_WIDTH])
    kn = _head_rms(zk, bd_ref, gk_ref[...])
    k_ref[...] = kn.astype(BF16)
    kf_ref[...] = kn
    zv = _dot(hb, w_ref[:, 2 * A_WIDTH:3 * A_WIDTH])
    v_ref[...] = zv.astype(BF16)
    vf_ref[...] = zv

    o = 3 * A_WIDTH
    zqb = _dot(hb, w_ref[:, o:o + B_KWIDTH]) * (B_DK ** -0.5)
    gla_ref[:, 0:B_KWIDTH] = zqb.astype(BF16)
    for c in range(B_KWIDTH, 2 * B_KWIDTH + 2 * B_WIDTH, 256):
        gla_ref[:, c:c + 256] = _dot(hb, w_ref[:, o + c:o + c + 256]).astype(BF16)

    gr = _dot(hb, wgr_ref[...])
    logit = _dot(gr.astype(BF16), wgu_ref[...]) + bg_ref[...]
    log_sig = jnp.minimum(logit, 0.0) - jnp.log1p(jnp.exp(-jnp.abs(logit)))
    la_ref[...] = log_sig * (1.0 / GATE_TAU)


def _inproj(xp, xs, mod, gmix, w_main, w_gr, bd, gq, gk, wgu_p, bg, n_ptiles, n_stiles, prep):
    n_tiles = n_ptiles + n_stiles
    t = n_tiles * TOK_TILE
    tail_tiles = 1 + n_stiles
    pblocks = prep // ROWS_PER_TILE
    const = lambda i: (0, 0)
    row = lambda i: (i, 0)
    tail = lambda i: (jnp.maximum(i - (n_ptiles - 1), 0), 0)
    return pl.pallas_call(
        functools.partial(_inproj_kernel, n_ptiles),
        grid=(n_tiles,),
        in_specs=[pl.BlockSpec((TOK_TILE, D_MODEL), lambda i: (jnp.minimum(i, n_ptiles - 1), 0)),
                  pl.BlockSpec((TOK_TILE, D_MODEL), lambda i: (jnp.maximum(i - n_ptiles, 0), 0)),
                  pl.BlockSpec((ROWS_PER_TILE, 6 * D_MODEL),
                               lambda i: (jnp.maximum(i - n_ptiles + pblocks, 0), 0)),
                  pl.BlockSpec((1, D_MODEL), const),
                  pl.BlockSpec((D_MODEL, IN_MAIN), const),
                  pl.BlockSpec((D_MODEL, LANES), const),
                  pl.BlockSpec((A_WIDTH, A_WIDTH), const),
                  pl.BlockSpec((1, A_WIDTH), const),
                  pl.BlockSpec((1, A_WIDTH), const),
                  pl.BlockSpec((LANES, B_KWIDTH), const),
                  pl.BlockSpec((1, B_KWIDTH), const)],
        out_specs=[pl.BlockSpec((TOK_TILE, A_WIDTH), row),
                   pl.BlockSpec((TOK_TILE, A_WIDTH), row),
                   pl.BlockSpec((TOK_TILE, A_WIDTH), row),
                   pl.BlockSpec((TOK_TILE, A_WIDTH), tail),
                   pl.BlockSpec((TOK_TILE, A_WIDTH), tail),
                   pl.BlockSpec((TOK_TILE, 2 * B_KWIDTH + 2 * B_WIDTH), row),
                   pl.BlockSpec((TOK_TILE, B_KWIDTH), row)],
        out_shape=[jax.ShapeDtypeStruct((t, A_WIDTH), BF16),
                   jax.ShapeDtypeStruct((t, A_WIDTH), BF16),
                   jax.ShapeDtypeStruct((t, A_WIDTH), BF16),
                   jax.ShapeDtypeStruct((tail_tiles * TOK_TILE, A_WIDTH), F32),
                   jax.ShapeDtypeStruct((tail_tiles * TOK_TILE, A_WIDTH), F32),
                   jax.ShapeDtypeStruct((t, 2 * B_KWIDTH + 2 * B_WIDTH), BF16),
                   jax.ShapeDtypeStruct((t, B_KWIDTH), F32)],
        compiler_params=_params(),
        name="inproj",
    )(xp, xs, mod, gmix, w_main, w_gr, bd, gq, gk, wgu_p, bg)


def _bias_lanes(n_keys):
    l = np.arange(ROLL_W)
    d = np.where(l < n_keys, BAND_PAST - l, BAND_PAST - l + ROLL_W)
    return np.clip(d, -(CHUNK - 1), MAX_REL) + (CHUNK - 1)


LOG2E = 1.4426950408889634


def _band_mask(m_rows, n_keys, first_col):
    qi = lax.broadcasted_iota(jnp.int32, (m_rows, n_keys), 0) >> LOG_CHUNK
    kw = lax.broadcasted_iota(jnp.int32, (m_rows, n_keys), 1)
    kc = kw >> LOG_CHUNK
    return (kc >= qi) & (kc <= qi + BAND_CHUNKS) & (kw >= first_col)


def _bias_tile(u_ref, h, ok):
    m_rows, n_keys = ok.shape
    src = jnp.broadcast_to(u_ref[h:h + 1, :] * LOG2E, (m_rows, ROLL_W))
    toe = pltpu.roll(src, 0, 1, stride=1, stride_axis=0)
    return jnp.where(ok, toe[:, 0:n_keys], NEG)


def _attend(q, kcat, vcat, bias_sc):
    m_rows = q.shape[0]
    first = lax.broadcasted_iota(jnp.int32, (m_rows, LANES), 1) < A_HEAD_DIM
    outs = []
    for p in range(A_HEADS // 2):
        lanes = slice(p * LANES, (p + 1) * LANES)
        qp, kp, vp = q[:, lanes], kcat[:, lanes], vcat[:, lanes]
        zero = jnp.zeros_like(qp)
        q2 = jnp.concatenate([jnp.where(first, qp, zero), jnp.where(first, zero, qp)], axis=0)
        s = _dot_nt(q2, kp) + bias_sc[p]
        e = jnp.exp2(s - jnp.max(s, axis=-1, keepdims=True))
        l = jnp.sum(e, axis=-1, keepdims=True)
        o2 = _dot(e.astype(BF16), vp) / l
        outs.append(jnp.where(first, o2[0:m_rows], o2[m_rows:2 * m_rows]))
    return jnp.concatenate(outs, axis=-1)


ATTN_SUB = 4
ATTN_WIN = 3


def _attn_prompt_kernel(u_ref, q_ref, *refs):
    k_refs = refs[0:ATTN_SUB + ATTN_WIN - 1]
    v_refs = refs[ATTN_SUB + ATTN_WIN - 1:2 * (ATTN_SUB + ATTN_WIN - 1)]
    o_ref, bias_sc = refs[-2:]
    j = pl.program_id(0)
    n_keys = ATTN_WIN * Q_ROWS

    @pl.when(j == 0)
    def _():
        for g in range(ATTN_WIN):
            ok = _band_mask(Q_ROWS, n_keys, (ATTN_WIN - 1 - g) * Q_ROWS)
            for h in range(A_HEADS):
                bias_sc[g, h // 2, (h % 2) * Q_ROWS:(h % 2 + 1) * Q_ROWS, :] = _bias_tile(u_ref, h, ok)

    ks = [r[...] for r in k_refs]
    vs = [r[...] for r in v_refs]
    for sub in range(ATTN_SUB):
        rows = slice(sub * Q_ROWS, (sub + 1) * Q_ROWS)
        kcat = jnp.concatenate(ks[sub:sub + ATTN_WIN], axis=0)
        vcat = jnp.concatenate(vs[sub:sub + ATTN_WIN], axis=0)
        bias = bias_sc.at[jnp.minimum(ATTN_SUB * j + sub, ATTN_WIN - 1)]
        o_ref[rows, :] = _attend(q_ref[rows, :], kcat, vcat, bias).astype(BF16)


def _attn_prompt(u, q, k, v, n_steps):
    const = lambda j: (0, 0)
    n_blk = ATTN_SUB + ATTN_WIN - 1
    blk = lambda d: pl.BlockSpec((Q_ROWS, A_WIDTH),
                                 lambda j, d=d: (jnp.maximum(ATTN_SUB * j - (ATTN_WIN - 1) + d, 0), 0))
    step_rows = ATTN_SUB * Q_ROWS
    return pl.pallas_call(
        _attn_prompt_kernel,
        grid=(n_steps,),
        in_specs=[pl.BlockSpec((A_HEADS, ROLL_W), const), pl.BlockSpec((step_rows, A_WIDTH), lambda j: (j, 0))]
                 + [blk(d) for d in range(n_blk)] * 2,
        out_specs=pl.BlockSpec((step_rows, A_WIDTH), lambda j: (j, 0)),
        out_shape=jax.ShapeDtypeStruct((n_steps * step_rows, A_WIDTH), BF16),
        scratch_shapes=[pltpu.VMEM((ATTN_WIN, A_HEADS // 2, 2 * Q_ROWS, ATTN_WIN * Q_ROWS), F32)],
        compiler_params=_params(),
        name="attn_prompt",
    )(u, q, *([k] * n_blk), *([v] * n_blk))


SAMPLE_KEYS = BAND_PAST + 2 * CHUNK
SAMPLE_STREAMS = 4


def _attn_sample_kernel(u_ref, q_ref, kn_ref, vn_ref, kc_ref, vc_ref, o_ref, bias_sc):
    @pl.when(pl.program_id(0) == 0)
    def _():
        ok = _band_mask(CHUNK, SAMPLE_KEYS, 0)
        for p in range(A_HEADS // 2):
            pair = jnp.concatenate([_bias_tile(u_ref, 2 * p, ok), _bias_tile(u_ref, 2 * p + 1, ok)], axis=0)
            bias_sc[p] = pair.T

    pad = jnp.zeros((CHUNK, A_WIDTH), BF16)
    lane = lax.broadcasted_iota(jnp.int32, (CHUNK, LANES), 1)
    first = lane < A_HEAD_DIM
    zero = jnp.zeros((CHUNK, LANES), BF16)
    for n in range(SAMPLE_STREAMS):
        rows = slice(n * CHUNK, (n + 1) * CHUNK)
        kcat = jnp.concatenate([kc_ref[n].astype(BF16), kn_ref[rows, :], pad], axis=0)
        vcat = jnp.concatenate([vc_ref[n].astype(BF16), vn_ref[rows, :], pad], axis=0)
        q = q_ref[rows, :]
        outs = []
        for p in range(A_HEADS // 2):
            lanes = slice(p * LANES, (p + 1) * LANES)
            qp = q[:, lanes]
            q_rows = jnp.concatenate([jnp.where(first, qp, zero), jnp.where(first, zero, qp)], axis=0)
            s = _dot_nt(kcat[:, lanes], q_rows) + bias_sc[p]
            e = jnp.exp2(s - jnp.max(s, axis=0, keepdims=True))
            pn = (e * (1.0 / jnp.sum(e, axis=0, keepdims=True))).astype(BF16)
            r = _dot_tn(pn, vcat[:, lanes])
            outs.append(jnp.where(first, r[0:CHUNK], r[CHUNK:2 * CHUNK]))
        o_ref[rows, :] = jnp.concatenate(outs, axis=-1).astype(BF16)


def _attn_sample(u, q, k, v, kc, vc, first_chunk, n_seq):
    assert n_seq % SAMPLE_STREAMS == 0 and first_chunk % SAMPLE_STREAMS == 0
    rows = SAMPLE_STREAMS * CHUNK
    new = pl.BlockSpec((rows, A_WIDTH), lambda b: (first_chunk // SAMPLE_STREAMS + b, 0))
    cache = pl.BlockSpec((SAMPLE_STREAMS, BAND_PAST, A_WIDTH), lambda b: (b, 0, 0))
    return pl.pallas_call(
        _attn_sample_kernel,
        grid=(n_seq // SAMPLE_STREAMS,),
        in_specs=[pl.BlockSpec((A_HEADS, ROLL_W), lambda b: (0, 0)), new, new, new, cache, cache],
        out_specs=pl.BlockSpec((rows, A_WIDTH), lambda b: (b, 0)),
        out_shape=jax.ShapeDtypeStruct((n_seq * CHUNK, A_WIDTH), BF16),
        scratch_shapes=[pltpu.VMEM((A_HEADS // 2, SAMPLE_KEYS, LANES), F32)],
        compiler_params=_params(),
        name="attn_sample",
    )(u, q, k, v, kc, vc)


GLA_CHUNKS = 4
GLA_SUB = 4


def _gla_block(n_chunks, gla_ref, la_ref, ltri_ref, g_ref, st_sc, o_ref):
    rows = n_chunks * CHUNK
    la = la_ref[...]
    la_hi, la_lo = _split(la)
    b = _dot(ltri_ref[...], la_hi) + _dot(ltri_ref[...], la_lo)
    b3 = b.reshape(n_chunks, CHUNK, B_KWIDTH)
    b_mid = b3[:, CHUNK // 2 - 1:CHUNK // 2, :]
    b_last = b3[:, CHUNK - 1:CHUNK, :]
    q = gla_ref[:, 0:B_KWIDTH].astype(F32).reshape(n_chunks, CHUNK, B_KWIDTH)
    k = gla_ref[:, B_KWIDTH:2 * B_KWIDTH].astype(F32).reshape(n_chunks, CHUNK, B_KWIDTH)
    q_start = (q * jnp.exp(b3)).reshape(rows, B_KWIDTH).astype(BF16)
    q_mid = (q * jnp.exp(b3 - b_mid)).reshape(rows, B_KWIDTH).astype(BF16)
    k_mid = (k * jnp.exp(b_mid - b3)).reshape(rows, B_KWIDTH).astype(BF16)
    k_end = (k * jnp.exp(b_last - b3)).reshape(rows, B_KWIDTH).astype(BF16)
    dec = jnp.exp(b_last)

    ti = lax.broadcasted_iota(jnp.int32, (2 * rows, rows), 0) & (rows - 1)
    si = lax.broadcasted_iota(jnp.int32, (2 * rows, rows), 1)
    causal = (si <= ti) & ((si >> LOG_CHUNK) == (ti >> LOG_CHUNK))
    first_r = lax.broadcasted_iota(jnp.int32, (rows, LANES), 1) < B_DK
    first_c = lax.broadcasted_iota(jnp.int32, (CHUNK, LANES), 1) < B_DK
    first_s = lax.broadcasted_iota(jnp.int32, (B_DV, LANES), 1) < B_DK

    def stack_heads(x, first):
        zero = jnp.zeros_like(x)
        return jnp.concatenate([jnp.where(first, x, zero), jnp.where(first, zero, x)], axis=0)

    for p in range(B_HEADS // 2):
        lanes = slice(p * LANES, (p + 1) * LANES)
        qs_p, qm_p, km_p, ke_p = q_start[:, lanes], q_mid[:, lanes], k_mid[:, lanes], k_end[:, lanes]
        v_pair = gla_ref[:, 2 * B_KWIDTH + 2 * p * B_DV:2 * B_KWIDTH + (2 * p + 2) * B_DV]
        sc = jnp.where(causal, _dot_nt(stack_heads(qm_p, first_r), km_p), 0.0)
        o2 = _dot(sc.astype(BF16), v_pair)
        intra = [o2[0:rows, 0:B_DV], o2[rows:2 * rows, B_DV:2 * B_DV]]
        inter = [[], []]
        st = st_sc[p]
        for c in range(n_chunks):
            cr = slice(c * CHUNK, (c + 1) * CHUNK)
            r2 = _dot_nt(stack_heads(qs_p[cr], first_c), st.astype(BF16))
            inter[0].append(r2[0:CHUNK])
            inter[1].append(r2[CHUNK:2 * CHUNK])
            u2 = _dot_tn(v_pair[cr], ke_p[cr])
            st = st * dec[c, :, lanes] + jnp.where(first_s, u2[0:B_DV], u2[B_DV:2 * B_DV])
        st_sc[p] = st
        for hh in range(2):
            h = 2 * p + hh
            o = intra[hh] + jnp.concatenate(inter[hh], axis=0)
            ms = jnp.mean(o * o, axis=-1, keepdims=True)
            on = o * lax.rsqrt(ms + EPS) * g_ref[...]
            r = gla_ref[:, 2 * B_KWIDTH + B_WIDTH + h * B_DV:2 * B_KWIDTH + B_WIDTH + (h + 1) * B_DV]
            o_ref[:, h * B_DV:(h + 1) * B_DV] = (on * _silu(r.astype(F32))).astype(BF16)


def _gla_prompt_kernel(gla_ref, la_ref, ltri_ref, g_ref, o_ref, sfin_ref, st_sc):
    @pl.when(pl.program_id(0) == 0)
    def _():
        st_sc[...] = jnp.zeros_like(st_sc)

    rows = GLA_CHUNKS * CHUNK
    for sub in range(GLA_SUB):
        part = pl.ds(sub * rows, rows)
        _gla_block(GLA_CHUNKS, gla_ref.at[part], la_ref.at[part], ltri_ref, g_ref, st_sc, o_ref.at[part])
    sfin_ref[...] = st_sc[...]


def _gla_sample_kernel(gla_ref, la_ref, ltri_ref, g_ref, s0_ref, o_ref, sfin_ref, st_sc):
    st_sc[...] = s0_ref[...]
    for n in range(SAMPLE_STREAMS):
        part = pl.ds(n * CHUNK, CHUNK)
        _gla_block(1, gla_ref.at[part], la_ref.at[part], ltri_ref, g_ref, st_sc.at[n], o_ref.at[part])
    sfin_ref[...] = st_sc[...]


def _ltri(n_chunks):
    r = np.arange(n_chunks * CHUNK)
    m = (r[None, :] <= r[:, None]) & (r[None, :] // CHUNK == r[:, None] // CHUNK)
    return jnp.asarray(m, BF16)


_GLA_W = 2 * B_KWIDTH + 2 * B_WIDTH
_ST_SHAPE = (B_HEADS // 2, B_DV, LANES)


def _gla_prompt(gla, la, g, n_steps):
    rows = GLA_SUB * GLA_CHUNKS * CHUNK
    const = lambda j: (0, 0)
    return pl.pallas_call(
        _gla_prompt_kernel,
        grid=(n_steps,),
        in_specs=[pl.BlockSpec((rows, _GLA_W), lambda j: (j, 0)),
                  pl.BlockSpec((rows, B_KWIDTH), lambda j: (j, 0)),
                  pl.BlockSpec((GLA_CHUNKS * CHUNK, GLA_CHUNKS * CHUNK), const),
                  pl.BlockSpec((1, B_DV), const)],
        out_specs=[pl.BlockSpec((rows, B_WIDTH), lambda j: (j, 0)),
                   pl.BlockSpec(_ST_SHAPE, lambda j: (0, 0, 0))],
        out_shape=[jax.ShapeDtypeStruct((n_steps * rows, B_WIDTH), BF16),
                   jax.ShapeDtypeStruct(_ST_SHAPE, F32)],
        scratch_shapes=[pltpu.VMEM(_ST_SHAPE, F32)],
        compiler_params=_params(),
        name="gla_prompt",
    )(gla, la, _ltri(GLA_CHUNKS), g)


def _gla_sample(gla, la, g, s0, first_chunk, n_seq):
    assert n_seq % SAMPLE_STREAMS == 0 and first_chunk % SAMPLE_STREAMS == 0
    const = lambda b: (0, 0)
    rows = SAMPLE_STREAMS * CHUNK
    first = first_chunk // SAMPLE_STREAMS
    st_spec = pl.BlockSpec((SAMPLE_STREAMS,) + _ST_SHAPE, lambda b: (b, 0, 0, 0))
    return pl.pallas_call(
        _gla_sample_kernel,
        grid=(n_seq // SAMPLE_STREAMS,),
        in_specs=[pl.BlockSpec((rows, _GLA_W), lambda b: (first + b, 0)),
                  pl.BlockSpec((rows, B_KWIDTH), lambda b: (first + b, 0)),
                  pl.BlockSpec((CHUNK, CHUNK), const),
                  pl.BlockSpec((1, B_DV), const),
                  st_spec],
        out_specs=[pl.BlockSpec((rows, B_WIDTH), lambda b: (b, 0)), st_spec],
        out_shape=[jax.ShapeDtypeStruct((n_seq * CHUNK, B_WIDTH), BF16),
                   jax.ShapeDtypeStruct((n_seq,) + _ST_SHAPE, F32)],
        scratch_shapes=[pltpu.VMEM((SAMPLE_STREAMS,) + _ST_SHAPE, F32)],
        compiler_params=_params(),
        name="gla_sample",
    )(gla, la, _ltri(1), g, s0)


def _state_to_pairs(s):
    lead = s.shape[:-3]
    s = s.reshape(lead + (B_HEADS // 2, 2, B_DK, B_DV))
    s = jnp.moveaxis(s, -1, -3)
    return s.reshape(lead + (B_HEADS // 2, B_DV, 2 * B_DK))


def _pairs_to_state(s):
    lead = s.shape[:-3]
    s = s.reshape(lead + (B_HEADS // 2, B_DV, 2, B_DK))
    s = jnp.moveaxis(s, -3, -1)
    return s.reshape(lead + (B_HEADS, B_DK, B_DV))


def _route(logits):
    lane = lax.broadcasted_iota(jnp.int32, logits.shape, 1)
    lane_f = lane.astype(F32)
    big = float(LANES)
    gmask = lane < N_GROUPS
    gl = jnp.where(gmask, logits, NEG)
    gmax = jnp.max(gl, axis=-1, keepdims=True)
    gsel = jnp.min(jnp.where(gl == gmax, lane_f, big), axis=-1, keepdims=True)
    gsum = jnp.sum(jnp.where(gmask, jnp.exp(gl - gmax), 0.0), axis=-1, keepdims=True)
    g_w = 1.0 / gsum
    e_lo = ROUTE_OFF + gsel * EXPERTS_PER_GROUP
    emask = (lane_f >= e_lo) & (lane_f < e_lo + EXPERTS_PER_GROUP)
    el = jnp.where(emask, logits, NEG)
    v1 = jnp.max(el, axis=-1, keepdims=True)
    i1 = jnp.min(jnp.where(el == v1, lane_f, big), axis=-1, keepdims=True)
    el2 = jnp.where(lane_f == i1, NEG, el)
    v2 = jnp.max(el2, axis=-1, keepdims=True)
    i2 = jnp.min(jnp.where(el2 == v2, lane_f, big), axis=-1, keepdims=True)
    t = jnp.exp(v2 - v1)
    w1 = g_w / (1.0 + t)
    w2 = g_w * t / (1.0 + t)
    return lane_f, i1, i2, w1, w2


ROW_PIECES = D_MODEL // 2 // LANES
SUBLANES = 8
ROW_TILE = ROW_PIECES * SUBLANES


def _pack_rows(z32_sc, x, rows):
    half = D_MODEL // 2
    out = []
    for c in range(ROW_PIECES):
        z32_sc[c, pl.ds(0, rows, stride=2), :] = x[:, c * LANES:(c + 1) * LANES]
        z32_sc[c, pl.ds(1, rows, stride=2), :] = x[:, half + c * LANES:half + (c + 1) * LANES]
        out.append(z32_sc[c].astype(BF16))
    return out


def _unpack_rows(z32_sc, pieces, rows):
    lo, hi = [], []
    for c in range(ROW_PIECES):
        z32_sc[c] = pieces[c].astype(F32)
        lo.append(z32_sc[c, pl.ds(0, rows, stride=2), :])
        hi.append(z32_sc[c, pl.ds(1, rows, stride=2), :])
    return jnp.concatenate(lo, axis=1), jnp.concatenate(hi, axis=1)


def _to_row_tiled(pieces, tokens):
    per_tile = pieces[0].shape[0] * SUBLANES // tokens
    return jnp.stack([p.reshape(tokens // SUBLANES, per_tile, LANES) for p in pieces], axis=1)


def _from_row_tiled(flat, tokens):
    per_tile = flat.shape[0] // (tokens // SUBLANES) // ROW_PIECES
    tiled = flat.reshape(tokens // SUBLANES, ROW_PIECES, per_tile, LANES)
    return [tiled[:, c].reshape(tokens // SUBLANES * per_tile, LANES) for c in range(ROW_PIECES)]


def _flatten_tiled(tiled):
    return tiled.reshape(-1, LANES)


def _outproj_kernel(n_ptiles, tiles_per_sb, oap_ref, oas_ref, obp_ref, obs_ref, wo_ref, xp_ref, xs_ref, mod_ref,
                    gffn_ref, wr_ref, br_ref, ltri_ref, x1_ref, h2p_ref, meta_ref, cnt_ref, z32_sc, cnt_sc):
    j = pl.program_id(0)
    is_prompt = j < n_ptiles // OUT_SUB
    for s in range(OUT_SUB):
        rows = slice(s * TOK_TILE, (s + 1) * TOK_TILE)
        mrows = slice(s * ROWS_PER_TILE, (s + 1) * ROWS_PER_TILE)
        x = jnp.where(is_prompt, xp_ref[rows, :], xs_ref[rows, :])
        oa = jnp.where(is_prompt, oap_ref[rows, :], oas_ref[rows, :])
        ob = jnp.where(is_prompt, obp_ref[rows, :], obs_ref[rows, :])
        mix = _dot(oa, wo_ref[0:A_WIDTH, :]) + _dot(ob, wo_ref[A_WIDTH:D_MODEL, :])
        gate1 = _rows_to_tokens(mod_ref[mrows, 2 * D_MODEL:3 * D_MODEL], D_MODEL)
        x1 = x + gate1 * mix
        x1_ref[rows, :] = x1
        ms = jnp.mean(x1 * x1, axis=-1, keepdims=True)
        xn = x1 * lax.rsqrt(ms + EPS) * gffn_ref[...]
        sh = _rows_to_tokens(mod_ref[mrows, 3 * D_MODEL:4 * D_MODEL], D_MODEL)
        sc = _rows_to_tokens(mod_ref[mrows, 4 * D_MODEL:5 * D_MODEL], D_MODEL)
        h2 = xn * (1.0 + sc) + sh
        words = [pltpu.bitcast(p, U32) for p in _pack_rows(z32_sc.at[s], h2, TOK_TILE)]
        tiles = slice(s * TOK_TILE // SUBLANES, (s + 1) * TOK_TILE // SUBLANES)
        h2p_ref[tiles] = _to_row_tiled(words, TOK_TILE)

        lane_f, i1, i2, w1, w2 = _route(_dot3(h2, wr_ref[...]) + br_ref[...])

        @pl.when(lax.rem(OUT_SUB * j + s, tiles_per_sb) == 0)
        def _():
            cnt_sc[...] = jnp.zeros_like(cnt_sc)

        sel = jnp.where((lane_f == i1) | (lane_f == i2), 1.0, 0.0).astype(BF16)
        before = _dot(ltri_ref[...], sel) + cnt_sc[0:1, :]
        rank1 = jnp.sum(jnp.where(lane_f == i1, before, 0.0), axis=-1, keepdims=True)
        rank2 = jnp.sum(jnp.where(lane_f == i2, before, 0.0), axis=-1, keepdims=True)
        cnt = cnt_sc[...] + _dot(jnp.ones((8, TOK_TILE), BF16), sel)
        cnt_sc[...] = cnt
        cnt_ref[0] = cnt
        cols = (i1, i2, rank1, rank2, w1, w2)
        meta = jnp.zeros_like(lane_f)
        for c, col in enumerate(cols):
            meta = jnp.where(lane_f == float(c), col, meta)
        meta_ref[rows, :] = meta


def _outproj(oa_p, oa_s, ob_p, ob_s, w_out, xp, xs, mod, gffn, wr, br, n_ptiles, n_stiles, prep, sb):
    n_tiles = n_ptiles + n_stiles
    t = n_tiles * TOK_TILE
    tiles_per_sb = sb // TOK_TILE
    assert n_ptiles % OUT_SUB == 0 and n_stiles % OUT_SUB == 0 and tiles_per_sb % OUT_SUB == 0
    step_rows = OUT_SUB * TOK_TILE
    mod_rows = OUT_SUB * ROWS_PER_TILE
    n_psteps = n_ptiles // OUT_SUB
    pblocks = prep // mod_rows
    const = lambda j: (0, 0)
    row = lambda j: (j, 0)
    prow = lambda j: (jnp.minimum(j, n_psteps - 1), 0)
    srow = lambda j: (jnp.maximum(j - n_psteps, 0), 0)
    r = np.arange(TOK_TILE)
    ltri = jnp.asarray(r[None, :] < r[:, None], BF16)
    return pl.pallas_call(
        functools.partial(_outproj_kernel, n_ptiles, tiles_per_sb),
        grid=(n_tiles // OUT_SUB,),
        in_specs=[pl.BlockSpec((step_rows, A_WIDTH), prow),
                  pl.BlockSpec((step_rows, A_WIDTH), srow),
                  pl.BlockSpec((step_rows, B_WIDTH), prow),
                  pl.BlockSpec((step_rows, B_WIDTH), srow),
                  pl.BlockSpec((D_MODEL, D_MODEL), const),
                  pl.BlockSpec((step_rows, D_MODEL), prow),
                  pl.BlockSpec((step_rows, D_MODEL), srow),
                  pl.BlockSpec((mod_rows, 6 * D_MODEL),
                               lambda j: (jnp.maximum(j - n_psteps + pblocks, 0), 0)),
                  pl.BlockSpec((1, D_MODEL), const),
                  pl.BlockSpec((D_MODEL, LANES), const),
                  pl.BlockSpec((1, LANES), const),
                  pl.BlockSpec((TOK_TILE, TOK_TILE), const)],
        out_specs=[pl.BlockSpec((step_rows, D_MODEL), row),
                   pl.BlockSpec((step_rows // SUBLANES, ROW_PIECES, SUBLANES, LANES), lambda j: (j, 0, 0, 0)),
                   pl.BlockSpec((step_rows, LANES), row),
                   pl.BlockSpec((1, 8, LANES), lambda j: (OUT_SUB * j // tiles_per_sb, 0, 0))],
        out_shape=[jax.ShapeDtypeStruct((t, D_MODEL), F32),
                   jax.ShapeDtypeStruct((t // SUBLANES, ROW_PIECES, SUBLANES, LANES), U32),
                   jax.ShapeDtypeStruct((t, LANES), F32),
                   jax.ShapeDtypeStruct((t // sb, 8, LANES), F32)],
        scratch_shapes=[pltpu.VMEM((OUT_SUB, ROW_PIECES, 2 * TOK_TILE, LANES), F32),
                        pltpu.VMEM((8, LANES), F32)],
        compiler_params=_params(),
        name="outproj",
    )(oa_p, oa_s, ob_p, ob_s, w_out, xp, xs, mod, gffn, wr, br, ltri)


MOE_SUPER_BLOCK = 2048
SEG_ALIGN = SUBLANES
CHUNK_BF16_ROWS = 2 * SEG_ALIGN * ROW_PIECES
SEG_BITS = 9
PAD_BITS = 5
FFN_ROWS = 512
PLAN_ROWS = LANES


def _local_rows(sb):
    return 2 * sb + N_EXPERTS * SEG_ALIGN


def _sorted_tiles(n_tokens, sb):
    rows = 2 * n_tokens + (n_tokens // sb) * N_EXPERTS * SEG_ALIGN + N_EXPERTS * FFN_ROWS
    return -(-rows // (FFN_SUB * FFN_ROWS)) * FFN_SUB


def _moe_plan_kernel(total_chunks, meta_ref, cnt_ref, ustrict_ref, lstrict_ref,
                     posw_ref, addr_ref, tab_ref, tile_ref):
    b = pl.program_id(0)
    per_tile = FFN_ROWS // SEG_ALIGN

    @pl.when(b == 0)
    def _():
        cnt = cnt_ref[...]
        chunks = jnp.floor((cnt + (SEG_ALIGN - 1)) * (1.0 / SEG_ALIGN))
        chunks_b = chunks.astype(BF16)
        loc = _dot(chunks_b, ustrict_ref[...])
        before = _dot(lstrict_ref[...], chunks_b)
        tot = _dot(jnp.ones((PLAN_ROWS, PLAN_ROWS), BF16), chunks_b)
        tiles = jnp.floor((tot + (per_tile - 1)) * (1.0 / per_tile))
        tile_off = _dot(tiles.astype(BF16), ustrict_ref[...])
        n_tiles = jnp.sum(tiles[0:1], axis=-1, keepdims=True)
        lane1 = lax.broadcasted_iota(jnp.int32, (PLAN_ROWS, LANES), 1)
        tail = lane1 == ROUTE_OFF + N_EXPERTS
        pad_off = jnp.where(tail, n_tiles * per_tile, tile_off * per_tile + tot)
        pad_n = jnp.where(tail, total_chunks - n_tiles * per_tile, tiles * per_tile - tot)
        row = lax.broadcasted_iota(jnp.int32, (PLAN_ROWS, LANES), 0)
        tab_ref[0] = loc
        tab_ref[1] = chunks
        tab_ref[2] = tile_off * per_tile + before
        tab_ref[3] = jnp.where(row == 0, pad_off, jnp.where(row == 1, pad_n, jnp.where(row == 2, n_tiles, 0.0)))
        t_idx = lax.broadcasted_iota(jnp.int32, tile_ref.shape, 0).astype(F32)
        lane_t = lax.broadcasted_iota(jnp.int32, tile_ref.shape, 1)
        is_expert = (lane_t >= ROUTE_OFF) & (lane_t < ROUTE_OFF + N_EXPERTS)
        ends = (tile_off + tiles)[0:1, :]
        owner = jnp.sum(jnp.where(is_expert & (ends <= t_idx), 1.0, 0.0), axis=-1, keepdims=True)
        tile_ref[...] = jnp.broadcast_to(jnp.minimum(owner, N_EXPERTS - 1.0), tile_ref.shape)

    own = jnp.floor((cnt_ref[pl.ds(b, 1), :] + (SEG_ALIGN - 1)) * (1.0 / SEG_ALIGN))
    own_off = _dot(jnp.broadcast_to(own, (SUBLANES, LANES)).astype(BF16), ustrict_ref[...]) * SEG_ALIGN
    meta = meta_ref[...]
    lane_f = lax.broadcasted_iota(jnp.int32, meta.shape, 1).astype(F32)
    off_row = own_off[0:1, :]
    pos = []
    for k in range(2):
        e_lane = meta[:, k:k + 1]
        base = jnp.sum(jnp.where(lane_f == e_lane, off_row, 0.0), axis=-1, keepdims=True)
        p = base + meta[:, 2 + k:3 + k]
        tile = jnp.floor(p * (1.0 / SUBLANES))
        pos.append(tile * (ROW_TILE - SUBLANES) + p)
    out = jnp.zeros_like(meta)
    for c, col in enumerate((pos[0], pos[1], meta[:, 4:5], meta[:, 5:6])):
        out = jnp.where(lane_f == float(c), col, out)
    posw_ref[...] = out
    addr_ref[0] = out.T[0:SUBLANES]


def _moe_plan(meta, cnt, sb):
    n_blocks = meta.shape[0] // sb
    assert n_blocks <= PLAN_ROWS and sb // SEG_ALIGN <= 256
    n_tiles = _sorted_tiles(meta.shape[0], sb)
    tile_rows = -(-n_tiles // SUBLANES) * SUBLANES
    r = np.arange(LANES)
    ustrict = jnp.asarray(r[:, None] < r[None, :], BF16)
    lstrict = jnp.asarray(r[None, :] < r[:, None], BF16)
    cnt_all = jnp.pad(cnt[:, 0, :], ((0, PLAN_ROWS - n_blocks), (0, 0)))
    const = lambda s: (0, 0)
    posw, addr, tab, tile_owner = pl.pallas_call(
        functools.partial(_moe_plan_kernel, float(n_tiles * (FFN_ROWS // SEG_ALIGN))),
        grid=(n_blocks,),
        in_specs=[pl.BlockSpec((sb, LANES), lambda s: (s, 0)),
                  pl.BlockSpec((PLAN_ROWS, LANES), const),
                  pl.BlockSpec((LANES, LANES), const),
                  pl.BlockSpec((PLAN_ROWS, PLAN_ROWS), const)],
        out_specs=[pl.BlockSpec((sb, LANES), lambda s: (s, 0)),
                   pl.BlockSpec((1, SUBLANES, sb), lambda s: (s, 0, 0)),
                   pl.BlockSpec((4, PLAN_ROWS, LANES), lambda s: (0, 0, 0)),
                   pl.BlockSpec((tile_rows, LANES), const)],
        out_shape=[jax.ShapeDtypeStruct(meta.shape, F32),
                   jax.ShapeDtypeStruct((n_blocks, SUBLANES, sb), F32),
                   jax.ShapeDtypeStruct((4, PLAN_ROWS, LANES), F32),
                   jax.ShapeDtypeStruct((tile_rows, LANES), F32)],
        compiler_params=_params(),
        name="moe_plan",
    )(meta, cnt_all, ustrict, lstrict)
    experts = slice(ROUTE_OFF, ROUTE_OFF + N_EXPERTS)
    to_i32 = lambda x: x.astype(jnp.int32).reshape(-1)
    plan = dict(
        loc=to_i32(tab[0, :n_blocks, experts]), n=to_i32(tab[1, :n_blocks, experts]),
        dst=to_i32(tab[2, :n_blocks, experts]),
        pad_off=to_i32(tab[3, 0, ROUTE_OFF:ROUTE_OFF + N_EXPERTS + 1]),
        pad_n=to_i32(tab[3, 1, ROUTE_OFF:ROUTE_OFF + N_EXPERTS + 1]),
        n_tiles=to_i32(tab[3, 2, 0:1]),
        owner=to_i32(tile_owner[:n_tiles, 0]))
    return posw, to_i32(addr[:, 0, :]), to_i32(addr[:, 1, :]), plan


def _token_rows(start):
    return pl.ds(start, ROW_PIECES, stride=SUBLANES)


def _pow2_copies(src_ref, dst_ref, src_chunk, dst_chunk, n, n_bits, sem, act):
    done = 0
    for k in reversed(range(n_bits)):
        take = (n >> k) & 1
        rows = CHUNK_BF16_ROWS << k
        src0 = 0 if src_chunk is None else pl.multiple_of((src_chunk + done) * CHUNK_BF16_ROWS, CHUNK_BF16_ROWS)
        dst0 = pl.multiple_of((dst_chunk + done) * CHUNK_BF16_ROWS, CHUNK_BF16_ROWS)

        @pl.when(take == 1)
        def _(src0=src0, dst0=dst0, rows=rows):
            act(pltpu.make_async_copy(src_ref.at[pl.ds(src0, rows)], dst_ref.at[pl.ds(dst0, rows)], sem))

        done = done + take * (1 << k)


def _segment_copies(block, loc_ref, n_ref, dst_ref, local_ref, global_ref, to_global, sem, act):
    def per_expert(e, carry):
        seg = block * N_EXPERTS + e
        if to_global:
            _pow2_copies(local_ref, global_ref, loc_ref[seg], dst_ref[seg], n_ref[seg], SEG_BITS, sem, act)
        else:
            _pow2_copies(global_ref, local_ref, dst_ref[seg], loc_ref[seg], n_ref[seg], SEG_BITS, sem, act)
        return carry

    lax.fori_loop(0, N_EXPERTS, per_expert, 0)


def _zero_fill(zero_ref, global_ref, padoff_ref, padn_ref, sem, act):
    full = 1 << PAD_BITS

    def per_pad(e, carry):
        def per_full(c, inner):
            dst0 = pl.multiple_of((padoff_ref[e] + c * full) * CHUNK_BF16_ROWS, CHUNK_BF16_ROWS)
            act(pltpu.make_async_copy(zero_ref, global_ref.at[pl.ds(dst0, full * CHUNK_BF16_ROWS)], sem))
            return inner

        n_full = padn_ref[e] >> PAD_BITS
        lax.fori_loop(0, n_full, per_full, 0)
        _pow2_copies(zero_ref, global_ref, None, padoff_ref[e] + n_full * full, padn_ref[e] & (full - 1),
                     PAD_BITS, sem, act)
        return carry

    lax.fori_loop(0, N_EXPERTS + 1, per_pad, 0)


STAGE_SLAB = 1024


def _restage(src_sc, dst_sc, dst_dtype):
    ratio = dst_sc.shape[0] / src_sc.shape[0]
    n_slabs = src_sc.shape[0] // (STAGE_SLAB if ratio > 1 else 2 * STAGE_SLAB)
    src_rows = src_sc.shape[0] // n_slabs
    dst_rows = dst_sc.shape[0] // n_slabs

    def slab(i, carry):
        s0 = pl.multiple_of(i * src_rows, src_rows)
        d0 = pl.multiple_of(i * dst_rows, dst_rows)
        dst_sc[pl.ds(d0, dst_rows), :] = pltpu.bitcast(src_sc[pl.ds(s0, src_rows), :], dst_dtype)
        return carry

    lax.fori_loop(0, n_slabs, slab, 0)


def _moe_dispatch_kernel(sb, n_blocks, loc_ref, n_ref, dst_ref, padoff_ref, padn_ref,
                         h2p_ref, a1_ref, a2_ref, xs_hbm, local_sc, stage_sc, zero_sc, sems, zero_sem):
    b = pl.program_id(0)
    slot = b & 1

    def segments(block, buf, act):
        _segment_copies(block, loc_ref, n_ref, dst_ref, stage_sc.at[buf], xs_hbm, True, sems.at[buf], act)

    local_sc[...] = jnp.zeros_like(local_sc)

    def step(g, carry):
        src = pl.multiple_of(g * ROW_TILE, ROW_TILE)
        for u in range(SUBLANES):
            t = g * SUBLANES + u
            row = h2p_ref[_token_rows(src + u), :]
            local_sc[_token_rows(a1_ref[t]), :] = row
            local_sc[_token_rows(a2_ref[t]), :] = row
        return carry

    lax.fori_loop(0, sb // SUBLANES, step, 0)

    @pl.when(b > 0)
    def _():
        segments(b - 1, 1 - slot, lambda c: c.wait())

    _restage(local_sc, stage_sc.at[slot], BF16)
    segments(b, slot, lambda c: c.start())

    @pl.when(b == 0)
    def _():
        zero_sc[...] = jnp.zeros_like(zero_sc)
        _zero_fill(zero_sc, xs_hbm, padoff_ref, padn_ref, zero_sem, lambda c: c.start())
        _zero_fill(zero_sc, xs_hbm, padoff_ref, padn_ref, zero_sem, lambda c: c.wait())

    @pl.when(b == n_blocks - 1)
    def _():
        segments(b, slot, lambda c: c.wait())


def _smem_vec(n, index_map):
    return pl.BlockSpec((n,), index_map, memory_space=pltpu.SMEM)


def _moe_dispatch(h2p, a1, a2, plan, sb, n_tiles):
    n_blocks = h2p.shape[0] // (sb * ROW_PIECES)
    local_flat = _local_rows(sb) * ROW_PIECES
    vec = _smem_vec(sb, lambda s, *_: (s,))
    return pl.pallas_call(
        functools.partial(_moe_dispatch_kernel, sb, n_blocks),
        grid_spec=pltpu.PrefetchScalarGridSpec(
            num_scalar_prefetch=5,
            grid=(n_blocks,),
            in_specs=[pl.BlockSpec((sb * ROW_PIECES, LANES), lambda s, *_: (s, 0)), vec, vec],
            out_specs=pl.BlockSpec(memory_space=pl.ANY),
            scratch_shapes=[pltpu.VMEM((local_flat, LANES), U32),
                            pltpu.VMEM((2, 2 * local_flat, LANES), BF16),
                            pltpu.VMEM(((1 << PAD_BITS) * CHUNK_BF16_ROWS, LANES), BF16),
                            pltpu.SemaphoreType.DMA((2,)),
                            pltpu.SemaphoreType.DMA(())]),
        out_shape=jax.ShapeDtypeStruct((n_tiles * FFN_ROWS * ROW_PIECES * 2, LANES), BF16),
        compiler_params=_params(),
        name="moe_dispatch",
    )(plan["loc"], plan["n"], plan["dst"], plan["pad_off"], plan["pad_n"], h2p, a1, a2)


FFN_SUB = 2


def _moe_ffn_kernel(owner_ref, ntiles_ref, xs_ref, *refs):
    w_refs = refs[0:3 * FFN_SUB]
    ys_ref, wg_sc, wu_sc, wd_sc, z32_sc = refs[3 * FFN_SUB:]
    half = D_MODEL // 2
    flat = FFN_ROWS * ROW_PIECES * 2
    for s in range(FFN_SUB):
        t = pl.program_id(0) * FFN_SUB + s
        wg_ref, wu_ref, wd_ref = w_refs[3 * s:3 * s + 3]
        rows = slice(s * flat, (s + 1) * flat)
        used = t < ntiles_ref[0]

        @pl.when(used & ((t < FFN_SUB) | (owner_ref[t] != owner_ref[jnp.maximum(t - FFN_SUB, 0)])))
        def _():
            wg_sc[s] = wg_ref[0].astype(BF16)
            wu_sc[s] = wu_ref[0].astype(BF16)
            wd_sc[s] = wd_ref[0].astype(BF16)

        @pl.when(used)
        def _():
            z32 = z32_sc.at[s]
            lo, hi = _unpack_rows(z32, _from_row_tiled(xs_ref[rows, :], FFN_ROWS), FFN_ROWS)
            lo, hi = lo.astype(BF16), hi.astype(BF16)
            g = _dot(lo, wg_sc[s, 0:half, :]) + _dot(hi, wg_sc[s, half:D_MODEL, :])
            u = _dot(lo, wu_sc[s, 0:half, :]) + _dot(hi, wu_sc[s, half:D_MODEL, :])
            y = _dot((_silu(g) * u).astype(BF16), wd_sc[s])
            ys_ref[rows, :] = _flatten_tiled(_to_row_tiled(_pack_rows(z32, y, FFN_ROWS), FFN_ROWS))

        @pl.when(jnp.logical_not(used))
        def _():
            ys_ref[rows, :] = jnp.zeros((flat, LANES), BF16)


def _moe_ffn(xs, plan, wg, wu, wd):
    flat = FFN_ROWS * ROW_PIECES * 2
    n_tiles = xs.shape[0] // flat
    assert n_tiles % FFN_SUB == 0

    def wspec(shape, s):
        tile = lambda i, owner, nt: jnp.minimum(i * FFN_SUB + s, nt[0] - 1)
        return pl.BlockSpec((1,) + shape, lambda i, owner, nt: (owner[tile(i, owner, nt)], 0, 0))

    w_specs, w_args = [], []
    for s in range(FFN_SUB):
        w_specs += [wspec((D_MODEL, EXPERT_FF), s), wspec((D_MODEL, EXPERT_FF), s), wspec((EXPERT_FF, D_MODEL), s)]
        w_args += [wg, wu, wd]
    last_step = lambda i, owner, nt: jnp.minimum(i, (nt[0] - 1) // FFN_SUB)
    return pl.pallas_call(
        _moe_ffn_kernel,
        grid_spec=pltpu.PrefetchScalarGridSpec(
            num_scalar_prefetch=2,
            grid=(n_tiles // FFN_SUB,),
            in_specs=[pl.BlockSpec((FFN_SUB * flat, LANES), lambda i, owner, nt: (last_step(i, owner, nt), 0))]
                     + w_specs,
            out_specs=pl.BlockSpec((FFN_SUB * flat, LANES), lambda i, owner, nt: (i, 0)),
            scratch_shapes=[pltpu.VMEM((FFN_SUB, D_MODEL, EXPERT_FF), BF16),
                            pltpu.VMEM((FFN_SUB, D_MODEL, EXPERT_FF), BF16),
                            pltpu.VMEM((FFN_SUB, EXPERT_FF, D_MODEL), BF16),
                            pltpu.VMEM((FFN_SUB, ROW_PIECES, 2 * FFN_ROWS, LANES), F32)]),
        out_shape=jax.ShapeDtypeStruct(xs.shape, BF16),
        compiler_params=_params(),
        name="moe_ffn",
    )(plan["owner"], plan["n_tiles"], xs, *w_args)


def _moe_combine_kernel(n_psb, n_blocks, loc_ref, n_ref, dst_ref,
                        ys_hbm, a1_ref, a2_ref, posw_ref, x1_ref, mod_ref, yp_ref, yo_ref,
                        local_sc, stage_sc, g1_sc, g2_sc, z32_sc, sems):
    s = pl.program_id(0)
    slot = s & 1

    def segments(block, buf, act):
        _segment_copies(block, loc_ref, n_ref, dst_ref, stage_sc.at[buf], ys_hbm, False, sems.at[buf], act)

    @pl.when(pl.program_id(1) == 0)
    def _():
        @pl.when(s == 0)
        def _():
            segments(s, slot, lambda c: c.start())

        segments(s, slot, lambda c: c.wait())
        _restage(stage_sc.at[slot], local_sc, U32)

        @pl.when(s + 1 < n_blocks)
        def _():
            segments(s + 1, 1 - slot, lambda c: c.start())

    def step(g, carry):
        dst = pl.multiple_of(g * ROW_TILE, ROW_TILE)
        for u in range(SUBLANES):
            t = g * SUBLANES + u
            g1_sc[_token_rows(dst + u), :] = local_sc[_token_rows(a1_ref[t]), :]
            g2_sc[_token_rows(dst + u), :] = local_sc[_token_rows(a2_ref[t]), :]
        return carry

    lax.fori_loop(0, TOK_TILE // SUBLANES, step, 0)
    halves = lambda g_sc: [pltpu.bitcast(p, BF16) for p in _from_row_tiled(g_sc[...], TOK_TILE)]
    lo1, hi1 = _unpack_rows(z32_sc, halves(g1_sc), TOK_TILE)
    lo2, hi2 = _unpack_rows(z32_sc, halves(g2_sc), TOK_TILE)
    w1, w2 = posw_ref[:, 2:3], posw_ref[:, 3:4]
    moe = jnp.concatenate([w1 * lo1 + w2 * lo2, w1 * hi1 + w2 * hi2], axis=1)
    gate2 = _rows_to_tokens(mod_ref[:, 5 * D_MODEL:6 * D_MODEL], D_MODEL)
    y = x1_ref[...] + gate2 * moe

    @pl.when(s < n_psb)
    def _():
        yp_ref[...] = y

    @pl.when(s >= n_psb)
    def _():
        yo_ref[...] = y


def _moe_combine(ys, a1, a2, posw, plan, x1, mod, sb, n_ptiles, n_stiles, prep):
    tps = sb // TOK_TILE
    n_blocks = (n_ptiles + n_stiles) // tps
    n_psb = n_ptiles // tps
    pblocks = prep // ROWS_PER_TILE
    tile = lambda s, j: s * tps + j
    vec = _smem_vec(TOK_TILE, lambda s, j, *_: (tile(s, j),))
    return pl.pallas_call(
        functools.partial(_moe_combine_kernel, n_psb, n_blocks),
        grid_spec=pltpu.PrefetchScalarGridSpec(
            num_scalar_prefetch=3,
            grid=(n_blocks, tps),
            in_specs=[pl.BlockSpec(memory_space=pl.ANY), vec, vec,
                      pl.BlockSpec((TOK_TILE, LANES), lambda s, j, *_: (tile(s, j), 0)),
                      pl.BlockSpec((TOK_TILE, D_MODEL), lambda s, j, *_: (tile(s, j), 0)),
                      pl.BlockSpec((ROWS_PER_TILE, 6 * D_MODEL),
                                   lambda s, j, *_: (jnp.maximum(tile(s, j) - n_ptiles + pblocks, 0), 0))],
            out_specs=[pl.BlockSpec((TOK_TILE, D_MODEL),
                                    lambda s, j, *_: (jnp.minimum(tile(s, j), n_ptiles - 1), 0)),
                       pl.BlockSpec((TOK_TILE, D_MODEL),
                                    lambda s, j, *_: (jnp.maximum(tile(s, j) - n_ptiles, 0), 0))],
            scratch_shapes=[pltpu.VMEM((_local_rows(sb) * ROW_PIECES, LANES), U32),
                            pltpu.VMEM((2, _local_rows(sb) * ROW_PIECES * 2, LANES), BF16),
                            pltpu.VMEM((TOK_TILE * ROW_PIECES, LANES), U32),
                            pltpu.VMEM((TOK_TILE * ROW_PIECES, LANES), U32),
                            pltpu.VMEM((ROW_PIECES, 2 * TOK_TILE, LANES), F32),
                            pltpu.SemaphoreType.DMA((2,))]),
        out_shape=[jax.ShapeDtypeStruct((n_ptiles * TOK_TILE, D_MODEL), F32),
                   jax.ShapeDtypeStruct((n_stiles * TOK_TILE, D_MODEL), F32)],
        compiler_params=_params(2),
        name="moe_combine",
    )(plan["loc"], plan["n"], plan["dst"], ys, a1, a2, posw, x1, mod)


def _layer(xp, xs, cache_k, cache_v, state, c_prompt, c_sample, norm_mix_g, norm_ffn_g, w_ada, b_ada, w_in,
           q_norm_g, k_norm_g, rel_bias, w_gate_up, b_gate, gla_norm_g, w_out, w_route_group,
           b_route_group, w_route_expert, b_route_expert, w_exp_gate, w_exp_up, w_exp_down):
    batch, seq, _ = xp.shape
    n_seq, dec_seq, _ = xs.shape
    assert batch == 1 and dec_seq == CHUNK and cache_k.shape[1] == BAND_PAST
    assert seq % TOK_TILE == 0 and seq >= BAND_PAST and (n_seq * CHUNK) % TOK_TILE == 0
    assert seq % (ATTN_SUB * Q_ROWS) == 0 and seq % (GLA_SUB * GLA_CHUNKS * CHUNK) == 0
    n_ptok, n_stok = seq, n_seq * CHUNK
    n_ptiles, n_stiles = n_ptok // TOK_TILE, n_stok // TOK_TILE
    sb = (MOE_SUPER_BLOCK if (n_ptok % MOE_SUPER_BLOCK == 0 and n_stok % MOE_SUPER_BLOCK == 0)
          else OUT_SUB * TOK_TILE)
    prep = OUT_SUB * ROWS_PER_TILE

    xp2 = xp.reshape(n_ptok, D_MODEL)
    xs2 = xs.reshape(n_stok, D_MODEL)
    c_rows = jnp.concatenate([jnp.broadcast_to(c_prompt, (prep, D_MODEL)), c_sample], axis=0)
    mod = _adaln(c_rows, w_ada, b_ada)

    w_main = w_in[:, 0:IN_MAIN].astype(BF16)
    w_gr = jnp.pad(w_in[:, IN_MAIN:], ((0, 0), (0, LANES - GATE_RANK))).astype(BF16)
    wgu_p = jnp.pad(w_gate_up, ((0, LANES - GATE_RANK), (0, 0))).astype(BF16)
    head = np.arange(A_WIDTH) // A_HEAD_DIM
    bd = jnp.asarray(head[:, None] == head[None, :], BF16)
    gq = jnp.tile(q_norm_g, A_HEADS).reshape(1, A_WIDTH)
    gk = jnp.tile(k_norm_g, A_HEADS).reshape(1, A_WIDTH)
    q, k, v, kf, vf, gla, la = _inproj(
        xp2, xs2, mod, norm_mix_g.reshape(1, D_MODEL), w_main, w_gr, bd, gq, gk, wgu_p,
        b_gate.reshape(1, B_KWIDTH), n_ptiles, n_stiles, prep)

    first_chunk = n_ptok // CHUNK
    oa_p = _attn_prompt(rel_bias[:, _bias_lanes(ATTN_WIN * Q_ROWS)], q, k, v, n_ptok // (ATTN_SUB * Q_ROWS))
    oa_s = _attn_sample(rel_bias[:, _bias_lanes(SAMPLE_KEYS)], q, k, v,
                        cache_k.reshape(n_seq, BAND_PAST, A_WIDTH), cache_v.reshape(n_seq, BAND_PAST, A_WIDTH),
                        first_chunk, n_seq)
    g_gla = gla_norm_g.reshape(1, B_DV)
    ob_p, sfin_p = _gla_prompt(gla, la, g_gla, n_ptok // (GLA_SUB * GLA_CHUNKS * CHUNK))
    ob_s, sfin_s = _gla_sample(gla, la, g_gla, _state_to_pairs(state), first_chunk, n_seq)

    wr = jnp.pad(jnp.concatenate([w_route_group, w_route_expert], axis=1),
                 ((0, 0), (0, LANES - N_GROUPS - N_EXPERTS)))
    br = jnp.pad(jnp.concatenate([b_route_group, b_route_expert]), (0, LANES - N_GROUPS - N_EXPERTS))
    x1, h2p, meta, cnt = _outproj(oa_p, oa_s, ob_p, ob_s, w_out.astype(BF16), xp2, xs2, mod,
                                  norm_ffn_g.reshape(1, D_MODEL), wr, br.reshape(1, LANES),
                                  n_ptiles, n_stiles, prep, sb)

    posw, a1, a2, plan = _moe_plan(meta, cnt, sb)
    xs_sorted = _moe_dispatch(h2p.reshape(-1, LANES), a1, a2, plan, sb, _sorted_tiles(n_ptok + n_stok, sb))
    ys_sorted = _moe_ffn(xs_sorted, plan, w_exp_gate, w_exp_up, w_exp_down)
    yp, ys = _moe_combine(ys_sorted, a1, a2, posw, plan, x1, mod, sb, n_ptiles, n_stiles, prep)

    tail = min(BAND_PAST, seq)
    heads = (A_HEADS, A_HEAD_DIM)
    return (yp.reshape(1, seq, D_MODEL), ys.reshape(n_seq, CHUNK, D_MODEL),
            kf[TOK_TILE - tail:TOK_TILE].reshape((1, tail) + heads),
            vf[TOK_TILE - tail:TOK_TILE].reshape((1, tail) + heads),
            _pairs_to_state(sfin_p)[None],
            kf[TOK_TILE:].reshape((n_seq, CHUNK) + heads),
            vf[TOK_TILE:].reshape((n_seq, CHUNK) + heads),
            _pairs_to_state(sfin_s))


def kernel(x_prompt, x_sample, cache_a_k, cache_a_v, state_gla, c_prompt, c_sample, norm_mix_g, norm_ffn_g,
           w_ada, b_ada, w_in, q_norm_g, k_norm_g, rel_bias, w_gate_up, b_gate, gla_norm_g, w_out,
           w_route_group, b_route_group, w_route_expert, b_route_expert, w_exp_gate, w_exp_up, w_exp_down):
    depth = w_in.shape[0]
    yp, ys = x_prompt, x_sample
    outs = [[] for _ in range(6)]
    for l in range(depth):
        yp, ys, kp, vp, sp, ks, vs, ss = _layer(
            yp, ys, cache_a_k[l], cache_a_v[l], state_gla[l], c_prompt, c_sample, norm_mix_g[l], norm_ffn_g[l],
            w_ada[l], b_ada[l], w_in[l], q_norm_g[l], k_norm_g[l], rel_bias[l], w_gate_up[l], b_gate[l],
            gla_norm_g[l], w_out[l], w_route_group[l], b_route_group[l], w_route_expert[l], b_route_expert[l],
            w_exp_gate[l], w_exp_up[l], w_exp_down[l])
        for lst, val in zip(outs, (kp, vp, sp, ks, vs, ss)):
            lst.append(val)
    return (yp, ys) + tuple(jnp.stack(o) for o in outs)
```

```python
import functools

import numpy as np
import jax
import jax.numpy as jnp
from jax import lax
from jax.experimental import pallas as pl
from jax.experimental.pallas import tpu as pltpu

F32 = jnp.float32
BF16 = jnp.bfloat16
U32 = jnp.uint32

D_MODEL = 1024
CHUNK = 64
LOG_CHUNK = 6
BAND_CHUNKS = 8
BAND_PAST = BAND_CHUNKS * CHUNK
A_WIDTH = 512
A_HEADS = 8
A_HEAD_DIM = 64
MAX_REL = 128
B_WIDTH = 512
B_HEADS = 4
B_DV = 128
B_DK = 64
B_KWIDTH = 256
GATE_RANK = 16
GATE_TAU = 16.0
N_GROUPS = 4
EXPERTS_PER_GROUP = 8
N_EXPERTS = 32
EXPERT_FF = 256
EPS = 1e-6

LANES = 128
IN_MAIN = 3 * A_WIDTH + 2 * B_KWIDTH + 2 * B_WIDTH
TOK_TILE = 512
ROWS_PER_TILE = TOK_TILE // CHUNK
OUT_SUB = 1
Q_CHUNKS = 4
Q_ROWS = Q_CHUNKS * CHUNK
ROLL_W = 1024
NEG = -1e30
ROUTE_OFF = N_GROUPS
VMEM_LIMIT = 56 * 1024 * 1024


def _params(n_axes=1):
    return pltpu.CompilerParams(dimension_semantics=("arbitrary",) * n_axes,
                                vmem_limit_bytes=VMEM_LIMIT)


def _split(a):
    hi = a.astype(BF16)
    lo = (a - hi.astype(F32)).astype(BF16)
    return hi, lo


def _dot(a, b):
    return jnp.dot(a, b, preferred_element_type=F32)


def _dot3(a, b):
    ah, al = _split(a)
    bh, bl = _split(b)
    return _dot(ah, bh) + _dot(al, bh) + _dot(ah, bl)


def _dot_nt(a, b):
    return lax.dot_general(a, b, (((1,), (1,)), ((), ())), preferred_element_type=F32)


def _dot_tn(a, b):
    return lax.dot_general(a, b, (((0,), (0,)), ((), ())), preferred_element_type=F32)


def _silu(x):
    return x / (1.0 + jnp.exp(-x))


def _rows_to_tokens(rows, n):
    r = rows.shape[0]
    return jnp.broadcast_to(rows[:, None, :], (r, CHUNK, n)).reshape(r * CHUNK, n)


def _adaln_kernel(c_ref, w_ref, b_ref, o_ref):
    a = _silu(c_ref[...])
    o_ref[...] = _dot3(a, w_ref[...]) + b_ref[...]


def _adaln(c_rows, w_ada, b_ada):
    r = c_rows.shape[0]
    n = w_ada.shape[1]
    tn = 1024
    return pl.pallas_call(
        _adaln_kernel,
        grid=(n // tn,),
        in_specs=[pl.BlockSpec((r, D_MODEL), lambda j: (0, 0)),
                  pl.BlockSpec((D_MODEL, tn), lambda j: (0, j)),
                  pl.BlockSpec((1, tn), lambda j: (0, j))],
        out_specs=pl.BlockSpec((r, tn), lambda j: (0, j)),
        out_shape=jax.ShapeDtypeStruct((r, n), F32),
        compiler_params=_params(),
        name="adaln",
    )(c_rows, w_ada, b_ada.reshape(1, n))


def _head_rms(z, bd_ref, g):
    ms = _dot((z * z).astype(BF16), bd_ref[...]) * (1.0 / A_HEAD_DIM)
    return z * lax.rsqrt(ms + EPS) * g


def _inproj_kernel(n_ptiles, xp_ref, xs_ref, mod_ref, gmix_ref, w_ref, wgr_ref, bd_ref, gq_ref, gk_ref,
                   wgu_ref, bg_ref,
                   q_ref, k_ref, v_ref, kf_ref, vf_ref, gla_ref, la_ref):
    i = pl.program_id(0)
    x = jnp.where(i < n_ptiles, xp_ref[...], xs_ref[...])
    ms = jnp.mean(x * x, axis=-1, keepdims=True)
    xn = x * lax.rsqrt(ms + EPS) * gmix_ref[...]
    sh = _rows_to_tokens(mod_ref[:, 0:D_MODEL], D_MODEL)
    sc = _rows_to_tokens(mod_ref[:, D_MODEL:2 * D_MODEL], D_MODEL)
    hb = (xn * (1.0 + sc) + sh).astype(BF16)

    zq = _dot(hb, w_ref[:, 0:A_WIDTH])
    q_ref[...] = (_head_rms(zq, bd_ref, gq_ref[...]) * (LOG2E * A_HEAD_DIM ** -0.5)).astype(BF16)
    zk = _dot(hb, w_ref[:, A_WIDTH:2 * A_WIDTH])
    kn = _head_rms(zk, bd_ref, gk_ref[...])
    k_ref[...] = kn.astype(BF16)
    kf_ref[...] = kn
    zv = _dot(hb, w_ref[:, 2 * A_WIDTH:3 * A_WIDTH])
    v_ref[...] = zv.astype(BF16)
    vf_ref[...] = zv

    o = 3 * A_WIDTH
    zqb = _dot(hb, w_ref[:, o:o + B_KWIDTH]) * (B_DK ** -0.5)
    gla_ref[:, 0:B_KWIDTH] = zqb.astype(BF16)
    for c in range(B_KWIDTH, 2 * B_KWIDTH + 2 * B_WIDTH, 256):
        gla_ref[:, c:c + 256] = _dot(hb, w_ref[:, o + c:o + c + 256]).astype(BF16)

    gr = _dot(hb, wgr_ref[...])
    logit = _dot(gr.astype(BF16), wgu_ref[...]) + bg_ref[...]
    log_sig = jnp.minimum(logit, 0.0) - jnp.log1p(jnp.exp(-jnp.abs(logit)))
    la_ref[...] = log_sig * (1.0 / GATE_TAU)


def _inproj(xp, xs, mod, gmix, w_main, w_gr, bd, gq, gk, wgu_p, bg, n_ptiles, n_stiles, prep):
    n_tiles = n_ptiles + n_stiles
    t = n_tiles * TOK_TILE
    tail_tiles = 1 + n_stiles
    pblocks = prep // ROWS_PER_TILE
    const = lambda i: (0, 0)
    row = lambda i: (i, 0)
    tail = lambda i: (jnp.maximum(i - (n_ptiles - 1), 0), 0)
    return pl.pallas_call(
        functools.partial(_inproj_kernel, n_ptiles),
        grid=(n_tiles,),
        in_specs=[pl.BlockSpec((TOK_TILE, D_MODEL), lambda i: (jnp.minimum(i, n_ptiles - 1), 0)),
                  pl.BlockSpec((TOK_TILE, D_MODEL), lambda i: (jnp.maximum(i - n_ptiles, 0), 0)),
                  pl.BlockSpec((ROWS_PER_TILE, 6 * D_MODEL),
                               lambda i: (jnp.maximum(i - n_ptiles + pblocks, 0), 0)),
                  pl.BlockSpec((1, D_MODEL), const),
                  pl.BlockSpec((D_MODEL, IN_MAIN), const),
                  pl.BlockSpec((D_MODEL, LANES), const),
                  pl.BlockSpec((A_WIDTH, A_WIDTH), const),
                  pl.BlockSpec((1, A_WIDTH), const),
                  pl.BlockSpec((1, A_WIDTH), const),
                  pl.BlockSpec((LANES, B_KWIDTH), const),
                  pl.BlockSpec((1, B_KWIDTH), const)],
        out_specs=[pl.BlockSpec((TOK_TILE, A_WIDTH), row),
                   pl.BlockSpec((TOK_TILE, A_WIDTH), row),
                   pl.BlockSpec((TOK_TILE, A_WIDTH), row),
                   pl.BlockSpec((TOK_TILE, A_WIDTH), tail),
                   pl.BlockSpec((TOK_TILE, A_WIDTH), tail),
                   pl.BlockSpec((TOK_TILE, 2 * B_KWIDTH + 2 * B_WIDTH), row),
                   pl.BlockSpec((TOK_TILE, B_KWIDTH), row)],
        out_shape=[jax.ShapeDtypeStruct((t, A_WIDTH), BF16),
                   jax.ShapeDtypeStruct((t, A_WIDTH), BF16),
                   jax.ShapeDtypeStruct((t, A_WIDTH), BF16),
                   jax.ShapeDtypeStruct((tail_tiles * TOK_TILE, A_WIDTH), F32),
                   jax.ShapeDtypeStruct((tail_tiles * TOK_TILE, A_WIDTH), F32),
                   jax.ShapeDtypeStruct((t, 2 * B_KWIDTH + 2 * B_WIDTH), BF16),
                   jax.ShapeDtypeStruct((t, B_KWIDTH), F32)],
        compiler_params=_params(),
        name="inproj",
    )(xp, xs, mod, gmix, w_main, w_gr, bd, gq, gk, wgu_p, bg)


def _bias_lanes(n_keys):
    l = np.arange(ROLL_W)
    d = np.where(l < n_keys, BAND_PAST - l, BAND_PAST - l + ROLL_W)
    return np.clip(d, -(CHUNK - 1), MAX_REL) + (CHUNK - 1)


LOG2E = 1.4426950408889634


def _band_mask(m_rows, n_keys, first_col):
    qi = lax.broadcasted_iota(jnp.int32, (m_rows, n_keys), 0) >> LOG_CHUNK
    kw = lax.broadcasted_iota(jnp.int32, (m_rows, n_keys), 1)
    kc = kw >> LOG_CHUNK
    return (kc >= qi) & (kc <= qi + BAND_CHUNKS) & (kw >= first_col)


def _bias_tile(u_ref, h, ok):
    m_rows, n_keys = ok.shape
    src = jnp.broadcast_to(u_ref[h:h + 1, :] * LOG2E, (m_rows, ROLL_W))
    toe = pltpu.roll(src, 0, 1, stride=1, stride_axis=0)
    return jnp.where(ok, toe[:, 0:n_keys], NEG)


def _attend(q, kcat, vcat, bias_sc):
    m_rows = q.shape[0]
    first = lax.broadcasted_iota(jnp.int32, (m_rows, LANES), 1) < A_HEAD_DIM
    outs = []
    for p in range(A_HEADS // 2):
        lanes = slice(p * LANES, (p + 1) * LANES)
        qp, kp, vp = q[:, lanes], kcat[:, lanes], vcat[:, lanes]
        zero = jnp.zeros_like(qp)
        q2 = jnp.concatenate([jnp.where(first, qp, zero), jnp.where(first, zero, qp)], axis=0)
        s = _dot_nt(q2, kp) + bias_sc[p]
        e = jnp.exp2(s - jnp.max(s, axis=-1, keepdims=True))
        l = jnp.sum(e, axis=-1, keepdims=True)
        o2 = _dot(e.astype(BF16), vp) / l
        outs.append(jnp.where(first, o2[0:m_rows], o2[m_rows:2 * m_rows]))
    return jnp.concatenate(outs, axis=-1)


ATTN_SUB = 4
ATTN_WIN = 3


def _attn_prompt_kernel(u_ref, q_ref, *refs):
    k_refs = refs[0:ATTN_SUB + ATTN_WIN - 1]
    v_refs = refs[ATTN_SUB + ATTN_WIN - 1:2 * (ATTN_SUB + ATTN_WIN - 1)]
    o_ref, bias_sc = refs[-2:]
    j = pl.program_id(0)
    n_keys = ATTN_WIN * Q_ROWS

    @pl.when(j == 0)
    def _():
        for g in range(ATTN_WIN):
            ok = _band_mask(Q_ROWS, n_keys, (ATTN_WIN - 1 - g) * Q_ROWS)
            for h in range(A_HEADS):
                bias_sc[g, h // 2, (h % 2) * Q_ROWS:(h % 2 + 1) * Q_ROWS, :] = _bias_tile(u_ref, h, ok)

    ks = [r[...] for r in k_refs]
    vs = [r[...] for r in v_refs]
    for sub in range(ATTN_SUB):
        rows = slice(sub * Q_ROWS, (sub + 1) * Q_ROWS)
        kcat = jnp.concatenate(ks[sub:sub + ATTN_WIN], axis=0)
        vcat = jnp.concatenate(vs[sub:sub + ATTN_WIN], axis=0)
        bias = bias_sc.at[jnp.minimum(ATTN_SUB * j + sub, ATTN_WIN - 1)]
        o_ref[rows, :] = _attend(q_ref[rows, :], kcat, vcat, bias).astype(BF16)


def _attn_prompt(u, q, k, v, n_steps):
    const = lambda j: (0, 0)
    n_blk = ATTN_SUB + ATTN_WIN - 1
    blk = lambda d: pl.BlockSpec((Q_ROWS, A_WIDTH),
                                 lambda j, d=d: (jnp.maximum(ATTN_SUB * j - (ATTN_WIN - 1) + d, 0), 0))
    step_rows = ATTN_SUB * Q_ROWS
    return pl.pallas_call(
        _attn_prompt_kernel,
        grid=(n_steps,),
        in_specs=[pl.BlockSpec((A_HEADS, ROLL_W), const), pl.BlockSpec((step_rows, A_WIDTH), lambda j: (j, 0))]
                 + [blk(d) for d in range(n_blk)] * 2,
        out_specs=pl.BlockSpec((step_rows, A_WIDTH), lambda j: (j, 0)),
        out_shape=jax.ShapeDtypeStruct((n_steps * step_rows, A_WIDTH), BF16),
        scratch_shapes=[pltpu.VMEM((ATTN_WIN, A_HEADS // 2, 2 * Q_ROWS, ATTN_WIN * Q_ROWS), F32)],
        compiler_params=_params(),
        name="attn_prompt",
    )(u, q, *([k] * n_blk), *([v] * n_blk))


SAMPLE_KEYS = BAND_PAST + 2 * CHUNK
SAMPLE_STREAMS = 4


def _attn_sample_kernel(u_ref, q_ref, kn_ref, vn_ref, kc_ref, vc_ref, o_ref, bias_sc):
    @pl.when(pl.program_id(0) == 0)
    def _():
        ok = _band_mask(CHUNK, SAMPLE_KEYS, 0)
        for p in range(A_HEADS // 2):
            pair = jnp.concatenate([_bias_tile(u_ref, 2 * p, ok), _bias_tile(u_ref, 2 * p + 1, ok)], axis=0)
            bias_sc[p] = pair.T

    pad = jnp.zeros((CHUNK, A_WIDTH), BF16)
    lane = lax.broadcasted_iota(jnp.int32, (CHUNK, LANES), 1)
    first = lane < A_HEAD_DIM
    zero = jnp.zeros((CHUNK, LANES), BF16)
    for n in range(SAMPLE_STREAMS):
        rows = slice(n * CHUNK, (n + 1) * CHUNK)
        kcat = jnp.concatenate([kc_ref[n].astype(BF16), kn_ref[rows, :], pad], axis=0)
        vcat = jnp.concatenate([vc_ref[n].astype(BF16), vn_ref[rows, :], pad], axis=0)
        q = q_ref[rows, :]
        outs = []
        for p in range(A_HEADS // 2):
            lanes = slice(p * LANES, (p + 1) * LANES)
            qp = q[:, lanes]
            q_rows = jnp.concatenate([jnp.where(first, qp, zero), jnp.where(first, zero, qp)], axis=0)
            s = _dot_nt(kcat[:, lanes], q_rows) + bias_sc[p]
            e = jnp.exp2(s - jnp.max(s, axis=0, keepdims=True))
            pn = (e * (1.0 / jnp.sum(e, axis=0, keepdims=True))).astype(BF16)
            r = _dot_tn(pn, vcat[:, lanes])
            outs.append(jnp.where(first, r[0:CHUNK], r[CHUNK:2 * CHUNK]))
        o_ref[rows, :] = jnp.concatenate(outs, axis=-1).astype(BF16)


def _attn_sample(u, q, k, v, kc, vc, first_chunk, n_seq):
    assert n_seq % SAMPLE_STREAMS == 0 and first_chunk % SAMPLE_STREAMS == 0
    rows = SAMPLE_STREAMS * CHUNK
    new = pl.BlockSpec((rows, A_WIDTH), lambda b: (first_chunk // SAMPLE_STREAMS + b, 0))
    cache = pl.BlockSpec((SAMPLE_STREAMS, BAND_PAST, A_WIDTH), lambda b: (b, 0, 0))
    return pl.pallas_call(
        _attn_sample_kernel,
        grid=(n_seq // SAMPLE_STREAMS,),
        in_specs=[pl.BlockSpec((A_HEADS, ROLL_W), lambda b: (0, 0)), new, new, new, cache, cache],
        out_specs=pl.BlockSpec((rows, A_WIDTH), lambda b: (b, 0)),
        out_shape=jax.ShapeDtypeStruct((n_seq * CHUNK, A_WIDTH), BF16),
        scratch_shapes=[pltpu.VMEM((A_HEADS // 2, SAMPLE_KEYS, LANES), F32)],
        compiler_params=_params(),
        name="attn_sample",
    )(u, q, k, v, kc, vc)


GLA_CHUNKS = 4
GLA_SUB = 4


def _gla_block(n_chunks, gla_ref, la_ref, ltri_ref, g_ref, st_sc, o_ref):
    rows = n_chunks * CHUNK
    la = la_ref[...]
    la_hi, la_lo = _split(la)
    b = _dot(ltri_ref[...], la_hi) + _dot(ltri_ref[...], la_lo)
    b3 = b.reshape(n_chunks, CHUNK, B_KWIDTH)
    b_mid = b3[:, CHUNK // 2 - 1:CHUNK // 2, :]
    b_last = b3[:, CHUNK - 1:CHUNK, :]
    q = gla_ref[:, 0:B_KWIDTH].astype(F32).reshape(n_chunks, CHUNK, B_KWIDTH)
    k = gla_ref[:, B_KWIDTH:2 * B_KWIDTH].astype(F32).reshape(n_chunks, CHUNK, B_KWIDTH)
    q_start = (q * jnp.exp(b3)).reshape(rows, B_KWIDTH).astype(BF16)
    q_mid = (q * jnp.exp(b3 - b_mid)).reshape(rows, B_KWIDTH).astype(BF16)
    k_mid = (k * jnp.exp(b_mid - b3)).reshape(rows, B_KWIDTH).astype(BF16)
    k_end = (k * jnp.exp(b_last - b3)).reshape(rows, B_KWIDTH).astype(BF16)
    dec = jnp.exp(b_last)

    ti = lax.broadcasted_iota(jnp.int32, (2 * rows, rows), 0) & (rows - 1)
    si = lax.broadcasted_iota(jnp.int32, (2 * rows, rows), 1)
    causal = (si <= ti) & ((si >> LOG_CHUNK) == (ti >> LOG_CHUNK))
    first_r = lax.broadcasted_iota(jnp.int32, (rows, LANES), 1) < B_DK
    first_c = lax.broadcasted_iota(jnp.int32, (CHUNK, LANES), 1) < B_DK
    first_s = lax.broadcasted_iota(jnp.int32, (B_DV, LANES), 1) < B_DK

    def stack_heads(x, first):
        zero = jnp.zeros_like(x)
        return jnp.concatenate([jnp.where(first, x, zero), jnp.where(first, zero, x)], axis=0)

    for p in range(B_HEADS // 2):
        lanes = slice(p * LANES, (p + 1) * LANES)
        qs_p, qm_p, km_p, ke_p = q_start[:, lanes], q_mid[:, lanes], k_mid[:, lanes], k_end[:, lanes]
        v_pair = gla_ref[:, 2 * B_KWIDTH + 2 * p * B_DV:2 * B_KWIDTH + (2 * p + 2) * B_DV]
        sc = jnp.where(causal, _dot_nt(stack_heads(qm_p, first_r), km_p), 0.0)
        o2 = _dot(sc.astype(BF16), v_pair)
        intra = [o2[0:rows, 0:B_DV], o2[rows:2 * rows, B_DV:2 * B_DV]]
        inter = [[], []]
        st = st_sc[p]
        for c in range(n_chunks):
            cr = slice(c * CHUNK, (c + 1) * CHUNK)
            r2 = _dot_nt(stack_heads(qs_p[cr], first_c), st.astype(BF16))
            inter[0].append(r2[0:CHUNK])
            inter[1].append(r2[CHUNK:2 * CHUNK])
            u2 = _dot_tn(v_pair[cr], ke_p[cr])
            st = st * dec[c, :, lanes] + jnp.where(first_s, u2[0:B_DV], u2[B_DV:2 * B_DV])
        st_sc[p] = st
        for hh in range(2):
            h = 2 * p + hh
            o = intra[hh] + jnp.concatenate(inter[hh], axis=0)
            ms = jnp.mean(o * o, axis=-1, keepdims=True)
            on = o * lax.rsqrt(ms + EPS) * g_ref[...]
            r = gla_ref[:, 2 * B_KWIDTH + B_WIDTH + h * B_DV:2 * B_KWIDTH + B_WIDTH + (h + 1) * B_DV]
            o_ref[:, h * B_DV:(h + 1) * B_DV] = (on * _silu(r.astype(F32))).astype(BF16)


def _gla_prompt_kernel(gla_ref, la_ref, ltri_ref, g_ref, o_ref, sfin_ref, st_sc):
    @pl.when(pl.program_id(0) == 0)
    def _():
        st_sc[...] = jnp.zeros_like(st_sc)

    rows = GLA_CHUNKS * CHUNK
    for sub in range(GLA_SUB):
        part = pl.ds(sub * rows, rows)
        _gla_block(GLA_CHUNKS, gla_ref.at[part], la_ref.at[part], ltri_ref, g_ref, st_sc, o_ref.at[part])
    sfin_ref[...] = st_sc[...]


def _gla_sample_kernel(gla_ref, la_ref, ltri_ref, g_ref, s0_ref, o_ref, sfin_ref, st_sc):
    st_sc[...] = s0_ref[...]
    for n in range(SAMPLE_STREAMS):
        part = pl.ds(n * CHUNK, CHUNK)
        _gla_block(1, gla_ref.at[part], la_ref.at[part], ltri_ref, g_ref, st_sc.at[n], o_ref.at[part])
    sfin_ref[...] = st_sc[...]


def _ltri(n_chunks):
    r = np.arange(n_chunks * CHUNK)
    m = (r[None, :] <= r[:, None]) & (r[None, :] // CHUNK == r[:, None] // CHUNK)
    return jnp.asarray(m, BF16)


_GLA_W = 2 * B_KWIDTH + 2 * B_WIDTH
_ST_SHAPE = (B_HEADS // 2, B_DV, LANES)


def _gla_prompt(gla, la, g, n_steps):
    rows = GLA_SUB * GLA_CHUNKS * CHUNK
    const = lambda j: (0, 0)
    return pl.pallas_call(
        _gla_prompt_kernel,
        grid=(n_steps,),
        in_specs=[pl.BlockSpec((rows, _GLA_W), lambda j: (j, 0)),
                  pl.BlockSpec((rows, B_KWIDTH), lambda j: (j, 0)),
                  pl.BlockSpec((GLA_CHUNKS * CHUNK, GLA_CHUNKS * CHUNK), const),
                  pl.BlockSpec((1, B_DV), const)],
        out_specs=[pl.BlockSpec((rows, B_WIDTH), lambda j: (j, 0)),
                   pl.BlockSpec(_ST_SHAPE, lambda j: (0, 0, 0))],
        out_shape=[jax.ShapeDtypeStruct((n_steps * rows, B_WIDTH), BF16),
                   jax.ShapeDtypeStruct(_ST_SHAPE, F32)],
        scratch_shapes=[pltpu.VMEM(_ST_SHAPE, F32)],
        compiler_params=_params(),
        name="gla_prompt",
    )(gla, la, _ltri(GLA_CHUNKS), g)


def _gla_sample(gla, la, g, s0, first_chunk, n_seq):
    assert n_seq % SAMPLE_STREAMS == 0 and first_chunk % SAMPLE_STREAMS == 0
    const = lambda b: (0, 0)
    rows = SAMPLE_STREAMS * CHUNK
    first = first_chunk // SAMPLE_STREAMS
    st_spec = pl.BlockSpec((SAMPLE_STREAMS,) + _ST_SHAPE, lambda b: (b, 0, 0, 0))
    return pl.pallas_call(
        _gla_sample_kernel,
        grid=(n_seq // SAMPLE_STREAMS,),
        in_specs=[pl.BlockSpec((rows, _GLA_W), lambda b: (first + b, 0)),
                  pl.BlockSpec((rows, B_KWIDTH), lambda b: (first + b, 0)),
                  pl.BlockSpec((CHUNK, CHUNK), const),
                  pl.BlockSpec((1, B_DV), const),
                  st_spec],
        out_specs=[pl.BlockSpec((rows, B_WIDTH), lambda b: (b, 0)), st_spec],
        out_shape=[jax.ShapeDtypeStruct((n_seq * CHUNK, B_WIDTH), BF16),
                   jax.ShapeDtypeStruct((n_seq,) + _ST_SHAPE, F32)],
        scratch_shapes=[pltpu.VMEM((SAMPLE_STREAMS,) + _ST_SHAPE, F32)],
        compiler_params=_params(),
        name="gla_sample",
    )(gla, la, _ltri(1), g, s0)


def _state_to_pairs(s):
    lead = s.shape[:-3]
    s = s.reshape(lead + (B_HEADS // 2, 2, B_DK, B_DV))
    s = jnp.moveaxis(s, -1, -3)
    return s.reshape(lead + (B_HEADS // 2, B_DV, 2 * B_DK))


def _pairs_to_state(s):
    lead = s.shape[:-3]
    s = s.reshape(lead + (B_HEADS // 2, B_DV, 2, B_DK))
    s = jnp.moveaxis(s, -3, -1)
    return s.reshape(lead + (B_HEADS, B_DK, B_DV))


def _route(logits):
    lane = lax.broadcasted_iota(jnp.int32, logits.shape, 1)
    lane_f = lane.astype(F32)
    big = float(LANES)
    gmask = lane < N_GROUPS
    gl = jnp.where(gmask, logits, NEG)
    gmax = jnp.max(gl, axis=-1, keepdims=True)
    gsel = jnp.min(jnp.where(gl == gmax, lane_f, big), axis=-1, keepdims=True)
    gsum = jnp.sum(jnp.where(gmask, jnp.exp(gl - gmax), 0.0), axis=-1, keepdims=True)
    g_w = 1.0 / gsum
    e_lo = ROUTE_OFF + gsel * EXPERTS_PER_GROUP
    emask = (lane_f >= e_lo) & (lane_f < e_lo + EXPERTS_PER_GROUP)
    el = jnp.where(emask, logits, NEG)
    v1 = jnp.max(el, axis=-1, keepdims=True)
    i1 = jnp.min(jnp.where(el == v1, lane_f, big), axis=-1, keepdims=True)
    el2 = jnp.where(lane_f == i1, NEG, el)
    v2 = jnp.max(el2, axis=-1, keepdims=True)
    i2 = jnp.min(jnp.where(el2 == v2, lane_f, big), axis=-1, keepdims=True)
    t = jnp.exp(v2 - v1)
    w1 = g_w / (1.0 + t)
    w2 = g_w * t / (1.0 + t)
    return lane_f, i1, i2, w1, w2


ROW_PIECES = D_MODEL // 2 // LANES
SUBLANES = 8
ROW_TILE = ROW_PIECES * SUBLANES


def _pack_rows(z32_sc, x, rows):
    half = D_MODEL // 2
    out = []
    for c in range(ROW_PIECES):
        z32_sc[c, pl.ds(0, rows, stride=2), :] = x[:, c * LANES:(c + 1) * LANES]
        z32_sc[c, pl.ds(1, rows, stride=2), :] = x[:, half + c * LANES:half + (c + 1) * LANES]
        out.append(z32_sc[c].astype(BF16))
    return out


def _unpack_rows(z32_sc, pieces, rows):
    lo, hi = [], []
    for c in range(ROW_PIECES):
        z32_sc[c] = pieces[c].astype(F32)
        lo.append(z32_sc[c, pl.ds(0, rows, stride=2), :])
        hi.append(z32_sc[c, pl.ds(1, rows, stride=2), :])
    return jnp.concatenate(lo, axis=1), jnp.concatenate(hi, axis=1)


def _to_row_tiled(pieces, tokens):
    per_tile = pieces[0].shape[0] * SUBLANES // tokens
    return jnp.stack([p.reshape(tokens // SUBLANES, per_tile, LANES) for p in pieces], axis=1)


def _from_row_tiled(flat, tokens):
    per_tile = flat.shape[0] // (tokens // SUBLANES) // ROW_PIECES
    tiled = flat.reshape(tokens // SUBLANES, ROW_PIECES, per_tile, LANES)
    return [tiled[:, c].reshape(tokens // SUBLANES * per_tile, LANES) for c in range(ROW_PIECES)]


def _flatten_tiled(tiled):
    return tiled.reshape(-1, LANES)


def _outproj_kernel(n_ptiles, tiles_per_sb, oap_ref, oas_ref, obp_ref, obs_ref, wo_ref, xp_ref, xs_ref, mod_ref,
                    gffn_ref, wr_ref, br_ref, ltri_ref, x1_ref, h2p_ref, meta_ref, cnt_ref, z32_sc, cnt_sc):
    j = pl.program_id(0)
    is_prompt = j < n_ptiles // OUT_SUB
    for s in range(OUT_SUB):
        rows = slice(s * TOK_TILE, (s + 1) * TOK_TILE)
        mrows = slice(s * ROWS_PER_TILE, (s + 1) * ROWS_PER_TILE)
        x = jnp.where(is_prompt, xp_ref[rows, :], xs_ref[rows, :])
        oa = jnp.where(is_prompt, oap_ref[rows, :], oas_ref[rows, :])
        ob = jnp.where(is_prompt, obp_ref[rows, :], obs_ref[rows, :])
        mix = _dot(oa, wo_ref[0:A_WIDTH, :]) + _dot(ob, wo_ref[A_WIDTH:D_MODEL, :])
        gate1 = _rows_to_tokens(mod_ref[mrows, 2 * D_MODEL:3 * D_MODEL], D_MODEL)
        x1 = x + gate1 * mix
        x1_ref[rows, :] = x1
        ms = jnp.mean(x1 * x1, axis=-1, keepdims=True)
        xn = x1 * lax.rsqrt(ms + EPS) * gffn_ref[...]
        sh = _rows_to_tokens(mod_ref[mrows, 3 * D_MODEL:4 * D_MODEL], D_MODEL)
        sc = _rows_to_tokens(mod_ref[mrows, 4 * D_MODEL:5 * D_MODEL], D_MODEL)
        h2 = xn * (1.0 + sc) + sh
        words = [pltpu.bitcast(p, U32) for p in _pack_rows(z32_sc.at[s], h2, TOK_TILE)]
        tiles = slice(s * TOK_TILE // SUBLANES, (s + 1) * TOK_TILE // SUBLANES)
        h2p_ref[tiles] = _to_row_tiled(words, TOK_TILE)

        lane_f, i1, i2, w1, w2 = _route(_dot3(h2, wr_ref[...]) + br_ref[...])

        @pl.when(lax.rem(OUT_SUB * j + s, tiles_per_sb) == 0)
        def _():
            cnt_sc[...] = jnp.zeros_like(cnt_sc)

        sel = jnp.where((lane_f == i1) | (lane_f == i2), 1.0, 0.0).astype(BF16)
        before = _dot(ltri_ref[...], sel) + cnt_sc[0:1, :]
        rank1 = jnp.sum(jnp.where(lane_f == i1, before, 0.0), axis=-1, keepdims=True)
        rank2 = jnp.sum(jnp.where(lane_f == i2, before, 0.0), axis=-1, keepdims=True)
        cnt = cnt_sc[...] + _dot(jnp.ones((8, TOK_TILE), BF16), sel)
        cnt_sc[...] = cnt
        cnt_ref[0] = cnt
        cols = (i1, i2, rank1, rank2, w1, w2)
        meta = jnp.zeros_like(lane_f)
        for c, col in enumerate(cols):
            meta = jnp.where(lane_f == float(c), col, meta)
        meta_ref[rows, :] = meta


def _outproj(oa_p, oa_s, ob_p, ob_s, w_out, xp, xs, mod, gffn, wr, br, n_ptiles, n_stiles, prep, sb):
    n_tiles = n_ptiles + n_stiles
    t = n_tiles * TOK_TILE
    tiles_per_sb = sb // TOK_TILE
    assert n_ptiles % OUT_SUB == 0 and n_stiles % OUT_SUB == 0 and tiles_per_sb % OUT_SUB == 0
    step_rows = OUT_SUB * TOK_TILE
    mod_rows = OUT_SUB * ROWS_PER_TILE
    n_psteps = n_ptiles // OUT_SUB
    pblocks = prep // mod_rows
    const = lambda j: (0, 0)
    row = lambda j: (j, 0)
    prow = lambda j: (jnp.minimum(j, n_psteps - 1), 0)
    srow = lambda j: (jnp.maximum(j - n_psteps, 0), 0)
    r = np.arange(TOK_TILE)
    ltri = jnp.asarray(r[None, :] < r[:, None], BF16)
    return pl.pallas_call(
        functools.partial(_outproj_kernel, n_ptiles, tiles_per_sb),
        grid=(n_tiles // OUT_SUB,),
        in_specs=[pl.BlockSpec((step_rows, A_WIDTH), prow),
                  pl.BlockSpec((step_rows, A_WIDTH), srow),
                  pl.BlockSpec((step_rows, B_WIDTH), prow),
                  pl.BlockSpec((step_rows, B_WIDTH), srow),
                  pl.BlockSpec((D_MODEL, D_MODEL), const),
                  pl.BlockSpec((step_rows, D_MODEL), prow),
                  pl.BlockSpec((step_rows, D_MODEL), srow),
                  pl.BlockSpec((mod_rows, 6 * D_MODEL),
                               lambda j: (jnp.maximum(j - n_psteps + pblocks, 0), 0)),
                  pl.BlockSpec((1, D_MODEL), const),
                  pl.BlockSpec((D_MODEL, LANES), const),
                  pl.BlockSpec((1, LANES), const),
                  pl.BlockSpec((TOK_TILE, TOK_TILE), const)],
        out_specs=[pl.BlockSpec((step_rows, D_MODEL), row),
                   pl.BlockSpec((step_rows // SUBLANES, ROW_PIECES, SUBLANES, LANES), lambda j: (j, 0, 0, 0)),
                   pl.BlockSpec((step_rows, LANES), row),
                   pl.BlockSpec((1, 8, LANES), lambda j: (OUT_SUB * j // tiles_per_sb, 0, 0))],
        out_shape=[jax.ShapeDtypeStruct((t, D_MODEL), F32),
                   jax.ShapeDtypeStruct((t // SUBLANES, ROW_PIECES, SUBLANES, LANES), U32),
                   jax.ShapeDtypeStruct((t, LANES), F32),
                   jax.ShapeDtypeStruct((t // sb, 8, LANES), F32)],
        scratch_shapes=[pltpu.VMEM((OUT_SUB, ROW_PIECES, 2 * TOK_TILE, LANES), F32),
                        pltpu.VMEM((8, LANES), F32)],
        compiler_params=_params(),
        name="outproj",
    )(oa_p, oa_s, ob_p, ob_s, w_out, xp, xs, mod, gffn, wr, br, ltri)


MOE_SUPER_BLOCK = 2048
SEG_ALIGN = SUBLANES
CHUNK_BF16_ROWS = 2 * SEG_ALIGN * ROW_PIECES
SEG_BITS = 9
PAD_BITS = 5
FFN_ROWS = 512
PLAN_ROWS = LANES


def _local_rows(sb):
    return 2 * sb + N_EXPERTS * SEG_ALIGN


def _sorted_tiles(n_tokens, sb):
    rows = 2 * n_tokens + (n_tokens // sb) * N_EXPERTS * SEG_ALIGN + N_EXPERTS * FFN_ROWS
    return -(-rows // (FFN_SUB * FFN_ROWS)) * FFN_SUB


def _moe_plan_kernel(total_chunks, meta_ref, cnt_ref, ustrict_ref, lstrict_ref,
                     posw_ref, addr_ref, tab_ref, tile_ref):
    b = pl.program_id(0)
    per_tile = FFN_ROWS // SEG_ALIGN

    @pl.when(b == 0)
    def _():
        cnt = cnt_ref[...]
        chunks = jnp.floor((cnt + (SEG_ALIGN - 1)) * (1.0 / SEG_ALIGN))
        chunks_b = chunks.astype(BF16)
        loc = _dot(chunks_b, ustrict_ref[...])
        before = _dot(lstrict_ref[...], chunks_b)
        tot = _dot(jnp.ones((PLAN_ROWS, PLAN_ROWS), BF16), chunks_b)
        tiles = jnp.floor((tot + (per_tile - 1)) * (1.0 / per_tile))
        tile_off = _dot(tiles.astype(BF16), ustrict_ref[...])
        n_tiles = jnp.sum(tiles[0:1], axis=-1, keepdims=True)
        lane1 = lax.broadcasted_iota(jnp.int32, (PLAN_ROWS, LANES), 1)
        tail = lane1 == ROUTE_OFF + N_EXPERTS
        pad_off = jnp.where(tail, n_tiles * per_tile, tile_off * per_tile + tot)
        pad_n = jnp.where(tail, total_chunks - n_tiles * per_tile, tiles * per_tile - tot)
        row = lax.broadcasted_iota(jnp.int32, (PLAN_ROWS, LANES), 0)
        tab_ref[0] = loc
        tab_ref[1] = chunks
        tab_ref[2] = tile_off * per_tile + before
        tab_ref[3] = jnp.where(row == 0, pad_off, jnp.where(row == 1, pad_n, jnp.where(row == 2, n_tiles, 0.0)))
        t_idx = lax.broadcasted_iota(jnp.int32, tile_ref.shape, 0).astype(F32)
        lane_t = lax.broadcasted_iota(jnp.int32, tile_ref.shape, 1)
        is_expert = (lane_t >= ROUTE_OFF) & (lane_t < ROUTE_OFF + N_EXPERTS)
        ends = (tile_off + tiles)[0:1, :]
        owner = jnp.sum(jnp.where(is_expert & (ends <= t_idx), 1.0, 0.0), axis=-1, keepdims=True)
        tile_ref[...] = jnp.broadcast_to(jnp.minimum(owner, N_EXPERTS - 1.0), tile_ref.shape)

    own = jnp.floor((cnt_ref[pl.ds(b, 1), :] + (SEG_ALIGN - 1)) * (1.0 / SEG_ALIGN))
    own_off = _dot(jnp.broadcast_to(own, (SUBLANES, LANES)).astype(BF16), ustrict_ref[...]) * SEG_ALIGN
    meta = meta_ref[...]
    lane_f = lax.broadcasted_iota(jnp.int32, meta.shape, 1).astype(F32)
    off_row = own_off[0:1, :]
    pos = []
    for k in range(2):
        e_lane = meta[:, k:k + 1]
        base = jnp.sum(jnp.where(lane_f == e_lane, off_row, 0.0), axis=-1, keepdims=True)
        p = base + meta[:, 2 + k:3 + k]
        tile = jnp.floor(p * (1.0 / SUBLANES))
        pos.append(tile * (ROW_TILE - SUBLANES) + p)
    out = jnp.zeros_like(meta)
    for c, col in enumerate((pos[0], pos[1], meta[:, 4:5], meta[:, 5:6])):
        out = jnp.where(lane_f == float(c), col, out)
    posw_ref[...] = out
    addr_ref[0] = out.T[0:SUBLANES]


def _moe_plan(meta, cnt, sb):
    n_blocks = meta.shape[0] // sb
    assert n_blocks <= PLAN_ROWS and sb // SEG_ALIGN <= 256
    n_tiles = _sorted_tiles(meta.shape[0], sb)
    tile_rows = -(-n_tiles // SUBLANES) * SUBLANES
    r = np.arange(LANES)
    ustrict = jnp.asarray(r[:, None] < r[None, :], BF16)
    lstrict = jnp.asarray(r[None, :] < r[:, None], BF16)
    cnt_all = jnp.pad(cnt[:, 0, :], ((0, PLAN_ROWS - n_blocks), (0, 0)))
    const = lambda s: (0, 0)
    posw, addr, tab, tile_owner = pl.pallas_call(
        functools.partial(_moe_plan_kernel, float(n_tiles * (FFN_ROWS // SEG_ALIGN))),
        grid=(n_blocks,),
        in_specs=[pl.BlockSpec((sb, LANES), lambda s: (s, 0)),
                  pl.BlockSpec((PLAN_ROWS, LANES), const),
                  pl.BlockSpec((LANES, LANES), const),
                  pl.BlockSpec((PLAN_ROWS, PLAN_ROWS), const)],
        out_specs=[pl.BlockSpec((sb, LANES), lambda s: (s, 0)),
                   pl.BlockSpec((1, SUBLANES, sb), lambda s: (s, 0, 0)),
                   pl.BlockSpec((4, PLAN_ROWS, LANES), lambda s: (0, 0, 0)),
                   pl.BlockSpec((tile_rows, LANES), const)],
        out_shape=[jax.ShapeDtypeStruct(meta.shape, F32),
                   jax.ShapeDtypeStruct((n_blocks, SUBLANES, sb), F32),
                   jax.ShapeDtypeStruct((4, PLAN_ROWS, LANES), F32),
                   jax.ShapeDtypeStruct((tile_rows, LANES), F32)],
        compiler_params=_params(),
        name="moe_plan",
    )(meta, cnt_all, ustrict, lstrict)
    experts = slice(ROUTE_OFF, ROUTE_OFF + N_EXPERTS)
    to_i32 = lambda x: x.astype(jnp.int32).reshape(-1)
    plan = dict(
        loc=to_i32(tab[0, :n_blocks, experts]), n=to_i32(tab[1, :n_blocks, experts]),
        dst=to_i32(tab[2, :n_blocks, experts]),
        pad_off=to_i32(tab[3, 0, ROUTE_OFF:ROUTE_OFF + N_EXPERTS + 1]),
        pad_n=to_i32(tab[3, 1, ROUTE_OFF:ROUTE_OFF + N_EXPERTS + 1]),
        n_tiles=to_i32(tab[3, 2, 0:1]),
        owner=to_i32(tile_owner[:n_tiles, 0]))
    return posw, to_i32(addr[:, 0, :]), to_i32(addr[:, 1, :]), plan


def _token_rows(start):
    return pl.ds(start, ROW_PIECES, stride=SUBLANES)


def _pow2_copies(src_ref, dst_ref, src_chunk, dst_chunk, n, n_bits, sem, act):
    done = 0
    for k in reversed(range(n_bits)):
        take = (n >> k) & 1
        rows = CHUNK_BF16_ROWS << k
        src0 = 0 if src_chunk is None else pl.multiple_of((src_chunk + done) * CHUNK_BF16_ROWS, CHUNK_BF16_ROWS)
        dst0 = pl.multiple_of((dst_chunk + done) * CHUNK_BF16_ROWS, CHUNK_BF16_ROWS)

        @pl.when(take == 1)
        def _(src0=src0, dst0=dst0, rows=rows):
            act(pltpu.make_async_copy(src_ref.at[pl.ds(src0, rows)], dst_ref.at[pl.ds(dst0, rows)], sem))

        done = done + take * (1 << k)


def _segment_copies(block, loc_ref, n_ref, dst_ref, local_ref, global_ref, to_global, sem, act):
    def per_expert(e, carry):
        seg = block * N_EXPERTS + e
        if to_global:
            _pow2_copies(local_ref, global_ref, loc_ref[seg], dst_ref[seg], n_ref[seg], SEG_BITS, sem, act)
        else:
            _pow2_copies(global_ref, local_ref, dst_ref[seg], loc_ref[seg], n_ref[seg], SEG_BITS, sem, act)
        return carry

    lax.fori_loop(0, N_EXPERTS, per_expert, 0)


def _zero_fill(zero_ref, global_ref, padoff_ref, padn_ref, sem, act):
    full = 1 << PAD_BITS

    def per_pad(e, carry):
        def per_full(c, inner):
            dst0 = pl.multiple_of((padoff_ref[e] + c * full) * CHUNK_BF16_ROWS, CHUNK_BF16_ROWS)
            act(pltpu.make_async_copy(zero_ref, global_ref.at[pl.ds(dst0, full * CHUNK_BF16_ROWS)], sem))
            return inner

        n_full = padn_ref[e] >> PAD_BITS
        lax.fori_loop(0, n_full, per_full, 0)
        _pow2_copies(zero_ref, global_ref, None, padoff_ref[e] + n_full * full, padn_ref[e] & (full - 1),
                     PAD_BITS, sem, act)
        return carry

    lax.fori_loop(0, N_EXPERTS + 1, per_pad, 0)


STAGE_SLAB = 1024


def _restage(src_sc, dst_sc, dst_dtype):
    ratio = dst_sc.shape[0] / src_sc.shape[0]
    n_slabs = src_sc.shape[0] // (STAGE_SLAB if ratio > 1 else 2 * STAGE_SLAB)
    src_rows = src_sc.shape[0] // n_slabs
    dst_rows = dst_sc.shape[0] // n_slabs

    def slab(i, carry):
        s0 = pl.multiple_of(i * src_rows, src_rows)
        d0 = pl.multiple_of(i * dst_rows, dst_rows)
        dst_sc[pl.ds(d0, dst_rows), :] = pltpu.bitcast(src_sc[pl.ds(s0, src_rows), :], dst_dtype)
        return carry

    lax.fori_loop(0, n_slabs, slab, 0)


def _moe_dispatch_kernel(sb, n_blocks, loc_ref, n_ref, dst_ref, padoff_ref, padn_ref,
                         h2p_ref, a1_ref, a2_ref, xs_hbm, local_sc, stage_sc, zero_sc, sems, zero_sem):
    b = pl.program_id(0)
    slot = b & 1

    def segments(block, buf, act):
        _segment_copies(block, loc_ref, n_ref, dst_ref, stage_sc.at[buf], xs_hbm, True, sems.at[buf], act)

    local_sc[...] = jnp.zeros_like(local_sc)

    def step(g, carry):
        src = pl.multiple_of(g * ROW_TILE, ROW_TILE)
        for u in range(SUBLANES):
            t = g * SUBLANES + u
            row = h2p_ref[_token_rows(src + u), :]
            local_sc[_token_rows(a1_ref[t]), :] = row
            local_sc[_token_rows(a2_ref[t]), :] = row
        return carry

    lax.fori_loop(0, sb // SUBLANES, step, 0, unroll=2)

    @pl.when(b > 0)
    def _():
        segments(b - 1, 1 - slot, lambda c: c.wait())

    _restage(local_sc, stage_sc.at[slot], BF16)
    segments(b, slot, lambda c: c.start())

    @pl.when(b == 0)
    def _():
        zero_sc[...] = jnp.zeros_like(zero_sc)
        _zero_fill(zero_sc, xs_hbm, padoff_ref, padn_ref, zero_sem, lambda c: c.start())
        _zero_fill(zero_sc, xs_hbm, padoff_ref, padn_ref, zero_sem, lambda c: c.wait())

    @pl.when(b == n_blocks - 1)
    def _():
        segments(b, slot, lambda c: c.wait())


def _smem_vec(n, index_map):
    return pl.BlockSpec((n,), index_map, memory_space=pltpu.SMEM)


def _moe_dispatch(h2p, a1, a2, plan, sb, n_tiles):
    n_blocks = h2p.shape[0] // (sb * ROW_PIECES)
    local_flat = _local_rows(sb) * ROW_PIECES
    vec = _smem_vec(sb, lambda s, *_: (s,))
    return pl.pallas_call(
        functools.partial(_moe_dispatch_kernel, sb, n_blocks),
        grid_spec=pltpu.PrefetchScalarGridSpec(
            num_scalar_prefetch=5,
            grid=(n_blocks,),
            in_specs=[pl.BlockSpec((sb * ROW_PIECES, LANES), lambda s, *_: (s, 0)), vec, vec],
            out_specs=pl.BlockSpec(memory_space=pl.ANY),
            scratch_shapes=[pltpu.VMEM((local_flat, LANES), U32),
                            pltpu.VMEM((2, 2 * local_flat, LANES), BF16),
                            pltpu.VMEM(((1 << PAD_BITS) * CHUNK_BF16_ROWS, LANES), BF16),
                            pltpu.SemaphoreType.DMA((2,)),
                            pltpu.SemaphoreType.DMA(())]),
        out_shape=jax.ShapeDtypeStruct((n_tiles * FFN_ROWS * ROW_PIECES * 2, LANES), BF16),
        compiler_params=_params(),
        name="moe_dispatch",
    )(plan["loc"], plan["n"], plan["dst"], plan["pad_off"], plan["pad_n"], h2p, a1, a2)


FFN_SUB = 2


def _moe_ffn_kernel(owner_ref, ntiles_ref, xs_ref, *refs):
    w_refs = refs[0:3 * FFN_SUB]
    ys_ref, wg_sc, wu_sc, wd_sc, z32_sc = refs[3 * FFN_SUB:]
    half = D_MODEL // 2
    flat = FFN_ROWS * ROW_PIECES * 2
    for s in range(FFN_SUB):
        t = pl.program_id(0) * FFN_SUB + s
        wg_ref, wu_ref, wd_ref = w_refs[3 * s:3 * s + 3]
        rows = slice(s * flat, (s + 1) * flat)
        used = t < ntiles_ref[0]

        @pl.when(used & ((t < FFN_SUB) | (owner_ref[t] != owner_ref[jnp.maximum(t - FFN_SUB, 0)])))
        def _():
            wg_sc[s] = wg_ref[0].astype(BF16)
            wu_sc[s] = wu_ref[0].astype(BF16)
            wd_sc[s] = wd_ref[0].astype(BF16)

        @pl.when(used)
        def _():
            z32 = z32_sc.at[s]
            lo, hi = _unpack_rows(z32, _from_row_tiled(xs_ref[rows, :], FFN_ROWS), FFN_ROWS)
            lo, hi = lo.astype(BF16), hi.astype(BF16)
            g = _dot(lo, wg_sc[s, 0:half, :]) + _dot(hi, wg_sc[s, half:D_MODEL, :])
            u = _dot(lo, wu_sc[s, 0:half, :]) + _dot(hi, wu_sc[s, half:D_MODEL, :])
            y = _dot((_silu(g) * u).astype(BF16), wd_sc[s])
            ys_ref[rows, :] = _flatten_tiled(_to_row_tiled(_pack_rows(z32, y, FFN_ROWS), FFN_ROWS))

        @pl.when(jnp.logical_not(used))
        def _():
            ys_ref[rows, :] = jnp.zeros((flat, LANES), BF16)


def _moe_ffn(xs, plan, wg, wu, wd):
    flat = FFN_ROWS * ROW_PIECES * 2
    n_tiles = xs.shape[0] // flat
    assert n_tiles % FFN_SUB == 0

    def wspec(shape, s):
        tile = lambda i, owner, nt: jnp.minimum(i * FFN_SUB + s, nt[0] - 1)
        return pl.BlockSpec((1,) + shape, lambda i, owner, nt: (owner[tile(i, owner, nt)], 0, 0))

    w_specs, w_args = [], []
    for s in range(FFN_SUB):
        w_specs += [wspec((D_MODEL, EXPERT_FF), s), wspec((D_MODEL, EXPERT_FF), s), wspec((EXPERT_FF, D_MODEL), s)]
        w_args += [wg, wu, wd]
    last_step = lambda i, owner, nt: jnp.minimum(i, (nt[0] - 1) // FFN_SUB)
    return pl.pallas_call(
        _moe_ffn_kernel,
        grid_spec=pltpu.PrefetchScalarGridSpec(
            num_scalar_prefetch=2,
            grid=(n_tiles // FFN_SUB,),
            in_specs=[pl.BlockSpec((FFN_SUB * flat, LANES), lambda i, owner, nt: (last_step(i, owner, nt), 0))]
                     + w_specs,
            out_specs=pl.BlockSpec((FFN_SUB * flat, LANES), lambda i, owner, nt: (i, 0)),
            scratch_shapes=[pltpu.VMEM((FFN_SUB, D_MODEL, EXPERT_FF), BF16),
                            pltpu.VMEM((FFN_SUB, D_MODEL, EXPERT_FF), BF16),
                            pltpu.VMEM((FFN_SUB, EXPERT_FF, D_MODEL), BF16),
                            pltpu.VMEM((FFN_SUB, ROW_PIECES, 2 * FFN_ROWS, LANES), F32)]),
        out_shape=jax.ShapeDtypeStruct(xs.shape, BF16),
        compiler_params=_params(),
        name="moe_ffn",
    )(plan["owner"], plan["n_tiles"], xs, *w_args)


def _moe_combine_kernel(n_psb, n_blocks, loc_ref, n_ref, dst_ref,
                        ys_hbm, a1_ref, a2_ref, posw_ref, x1_ref, mod_ref, yp_ref, yo_ref,
                        local_sc, stage_sc, g1_sc, g2_sc, z32_sc, sems):
    s = pl.program_id(0)
    slot = s & 1

    def segments(block, buf, act):
        _segment_copies(block, loc_ref, n_ref, dst_ref, stage_sc.at[buf], ys_hbm, False, sems.at[buf], act)

    @pl.when(pl.program_id(1) == 0)
    def _():
        @pl.when(s == 0)
        def _():
            segments(s, slot, lambda c: c.start())

        segments(s, slot, lambda c: c.wait())
        _restage(stage_sc.at[slot], local_sc, U32)

        @pl.when(s + 1 < n_blocks)
        def _():
            segments(s + 1, 1 - slot, lambda c: c.start())

    def step(g, carry):
        dst = pl.multiple_of(g * ROW_TILE, ROW_TILE)
        for u in range(SUBLANES):
            t = g * SUBLANES + u
            g1_sc[_token_rows(dst + u), :] = local_sc[_token_rows(a1_ref[t]), :]
            g2_sc[_token_rows(dst + u), :] = local_sc[_token_rows(a2_ref[t]), :]
        return carry

    lax.fori_loop(0, TOK_TILE // SUBLANES, step, 0)
    halves = lambda g_sc: [pltpu.bitcast(p, BF16) for p in _from_row_tiled(g_sc[...], TOK_TILE)]
    lo1, hi1 = _unpack_rows(z32_sc, halves(g1_sc), TOK_TILE)
    lo2, hi2 = _unpack_rows(z32_sc, halves(g2_sc), TOK_TILE)
    w1, w2 = posw_ref[:, 2:3], posw_ref[:, 3:4]
    moe = jnp.concatenate([w1 * lo1 + w2 * lo2, w1 * hi1 + w2 * hi2], axis=1)
    gate2 = _rows_to_tokens(mod_ref[:, 5 * D_MODEL:6 * D_MODEL], D_MODEL)
    y = x1_ref[...] + gate2 * moe

    @pl.when(s < n_psb)
    def _():
        yp_ref[...] = y

    @pl.when(s >= n_psb)
    def _():
        yo_ref[...] = y


def _moe_combine(ys, a1, a2, posw, plan, x1, mod, sb, n_ptiles, n_stiles, prep):
    tps = sb // TOK_TILE
    n_blocks = (n_ptiles + n_stiles) // tps
    n_psb = n_ptiles // tps
    pblocks = prep // ROWS_PER_TILE
    tile = lambda s, j: s * tps + j
    vec = _smem_vec(TOK_TILE, lambda s, j, *_: (tile(s, j),))
    return pl.pallas_call(
        functools.partial(_moe_combine_kernel, n_psb, n_blocks),
        grid_spec=pltpu.PrefetchScalarGridSpec(
            num_scalar_prefetch=3,
            grid=(n_blocks, tps),
            in_specs=[pl.BlockSpec(memory_space=pl.ANY), vec, vec,
                      pl.BlockSpec((TOK_TILE, LANES), lambda s, j, *_: (tile(s, j), 0)),
                      pl.BlockSpec((TOK_TILE, D_MODEL), lambda s, j, *_: (tile(s, j), 0)),
                      pl.BlockSpec((ROWS_PER_TILE, 6 * D_MODEL),
                                   lambda s, j, *_: (jnp.maximum(tile(s, j) - n_ptiles + pblocks, 0), 0))],
            out_specs=[pl.BlockSpec((TOK_TILE, D_MODEL),
                                    lambda s, j, *_: (jnp.minimum(tile(s, j), n_ptiles - 1), 0)),
                       pl.BlockSpec((TOK_TILE, D_MODEL),
                                    lambda s, j, *_: (jnp.maximum(tile(s, j) - n_ptiles, 0), 0))],
            scratch_shapes=[pltpu.VMEM((_local_rows(sb) * ROW_PIECES, LANES), U32),
                            pltpu.VMEM((2, _local_rows(sb) * ROW_PIECES * 2, LANES), BF16),
                            pltpu.VMEM((TOK_TILE * ROW_PIECES, LANES), U32),
                            pltpu.VMEM((TOK_TILE * ROW_PIECES, LANES), U32),
                            pltpu.VMEM((ROW_PIECES, 2 * TOK_TILE, LANES), F32),
                            pltpu.SemaphoreType.DMA((2,))]),
        out_shape=[jax.ShapeDtypeStruct((n_ptiles * TOK_TILE, D_MODEL), F32),
                   jax.ShapeDtypeStruct((n_stiles * TOK_TILE, D_MODEL), F32)],
        compiler_params=_params(2),
        name="moe_combine",
    )(plan["loc"], plan["n"], plan["dst"], ys, a1, a2, posw, x1, mod)


def _layer(xp, xs, cache_k, cache_v, state, c_prompt, c_sample, norm_mix_g, norm_ffn_g, w_ada, b_ada, w_in,
           q_norm_g, k_norm_g, rel_bias, w_gate_up, b_gate, gla_norm_g, w_out, w_route_group,
           b_route_group, w_route_expert, b_route_expert, w_exp_gate, w_exp_up, w_exp_down):
    batch, seq, _ = xp.shape
    n_seq, dec_seq, _ = xs.shape
    assert batch == 1 and dec_seq == CHUNK and cache_k.shape[1] == BAND_PAST
    assert seq % TOK_TILE == 0 and seq >= BAND_PAST and (n_seq * CHUNK) % TOK_TILE == 0
    assert seq % (ATTN_SUB * Q_ROWS) == 0 and seq % (GLA_SUB * GLA_CHUNKS * CHUNK) == 0
    n_ptok, n_stok = seq, n_seq * CHUNK
    n_ptiles, n_stiles = n_ptok // TOK_TILE, n_stok // TOK_TILE
    sb = (MOE_SUPER_BLOCK if (n_ptok % MOE_SUPER_BLOCK == 0 and n_stok % MOE_SUPER_BLOCK == 0)
          else OUT_SUB * TOK_TILE)
    prep = OUT_SUB * ROWS_PER_TILE

    xp2 = xp.reshape(n_ptok, D_MODEL)
    xs2 = xs.reshape(n_stok, D_MODEL)
    c_rows = jnp.concatenate([jnp.broadcast_to(c_prompt, (prep, D_MODEL)), c_sample], axis=0)
    mod = _adaln(c_rows, w_ada, b_ada)

    w_main = w_in[:, 0:IN_MAIN].astype(BF16)
    w_gr = jnp.pad(w_in[:, IN_MAIN:], ((0, 0), (0, LANES - GATE_RANK))).astype(BF16)
    wgu_p = jnp.pad(w_gate_up, ((0, LANES - GATE_RANK), (0, 0))).astype(BF16)
    head = np.arange(A_WIDTH) // A_HEAD_DIM
    bd = jnp.asarray(head[:, None] == head[None, :], BF16)
    gq = jnp.tile(q_norm_g, A_HEADS).reshape(1, A_WIDTH)
    gk = jnp.tile(k_norm_g, A_HEADS).reshape(1, A_WIDTH)
    q, k, v, kf, vf, gla, la = _inproj(
        xp2, xs2, mod, norm_mix_g.reshape(1, D_MODEL), w_main, w_gr, bd, gq, gk, wgu_p,
        b_gate.reshape(1, B_KWIDTH), n_ptiles, n_stiles, prep)

    first_chunk = n_ptok // CHUNK
    oa_p = _attn_prompt(rel_bias[:, _bias_lanes(ATTN_WIN * Q_ROWS)], q, k, v, n_ptok // (ATTN_SUB * Q_ROWS))
    oa_s = _attn_sample(rel_bias[:, _bias_lanes(SAMPLE_KEYS)], q, k, v,
                        cache_k.reshape(n_seq, BAND_PAST, A_WIDTH), cache_v.reshape(n_seq, BAND_PAST, A_WIDTH),
                        first_chunk, n_seq)
    g_gla = gla_norm_g.reshape(1, B_DV)
    ob_p, sfin_p = _gla_prompt(gla, la, g_gla, n_ptok // (GLA_SUB * GLA_CHUNKS * CHUNK))
    ob_s, sfin_s = _gla_sample(gla, la, g_gla, _state_to_pairs(state), first_chunk, n_seq)

    wr = jnp.pad(jnp.concatenate([w_route_group, w_route_expert], axis=1),
                 ((0, 0), (0, LANES - N_GROUPS - N_EXPERTS)))
    br = jnp.pad(jnp.concatenate([b_route_group, b_route_expert]), (0, LANES - N_GROUPS - N_EXPERTS))
    x1, h2p, meta, cnt = _outproj(oa_p, oa_s, ob_p, ob_s, w_out.astype(BF16), xp2, xs2, mod,
                                  norm_ffn_g.reshape(1, D_MODEL), wr, br.reshape(1, LANES),
                                  n_ptiles, n_stiles, prep, sb)

    posw, a1, a2, plan = _moe_plan(meta, cnt, sb)
    xs_sorted = _moe_dispatch(h2p.reshape(-1, LANES), a1, a2, plan, sb, _sorted_tiles(n_ptok + n_stok, sb))
    ys_sorted = _moe_ffn(xs_sorted, plan, w_exp_gate, w_exp_up, w_exp_down)
    yp, ys = _moe_combine(ys_sorted, a1, a2, posw, plan, x1, mod, sb, n_ptiles, n_stiles, prep)

    tail = min(BAND_PAST, seq)
    heads = (A_HEADS, A_HEAD_DIM)
    return (yp.reshape(1, seq, D_MODEL), ys.reshape(n_seq, CHUNK, D_MODEL),
            kf[TOK_TILE - tail:TOK_TILE].reshape((1, tail) + heads),
            vf[TOK_TILE - tail:TOK_TILE].reshape((1, tail) + heads),
            _pairs_to_state(sfin_p)[None],
            kf[TOK_TILE:].reshape((n_seq, CHUNK) + heads),
            vf[TOK_TILE:].reshape((n_seq, CHUNK) + heads),
            _pairs_to_state(sfin_s))


def kernel(x_prompt, x_sample, cache_a_k, cache_a_v, state_gla, c_prompt, c_sample, norm_mix_g, norm_ffn_g,
           w_ada, b_ada, w_in, q_norm_g, k_norm_g, rel_bias, w_gate_up, b_gate, gla_norm_g, w_out,
           w_route_group, b_route_group, w_route_expert, b_route_expert, w_exp_gate, w_exp_up, w_exp_down):
    depth = w_in.shape[0]
    yp, ys = x_prompt, x_sample
    outs = [[] for _ in range(6)]
    for l in range(depth):
        yp, ys, kp, vp, sp, ks, vs, ss = _layer(
            yp, ys, cache_a_k[l], cache_a_v[l], state_gla[l], c_prompt, c_sample, norm_mix_g[l], norm_ffn_g[l],
            w_ada[l], b_ada[l], w_in[l], q_norm_g[l], k_norm_g[l], rel_bias[l], w_gate_up[l], b_gate[l],
            gla_norm_g[l], w_out[l], w_route_group[l], b_route_group[l], w_route_expert[l], b_route_expert[l],
            w_exp_gate[l], w_exp_up[l], w_exp_down[l])
        for lst, val in zip(outs, (kp, vp, sp, ks, vs, ss)):
            lst.append(val)
    return (yp, ys) + tuple(jnp.stack(o) for o in outs)
```

```python
import functools

import numpy as np
import jax
import jax.numpy as jnp
from jax import lax
from jax.experimental import pallas as pl
from jax.experimental.pallas import tpu as pltpu

F32 = jnp.float32
BF16 = jnp.bfloat16
U32 = jnp.uint32

D_MODEL = 1024
CHUNK = 64
LOG_CHUNK = 6
BAND_CHUNKS = 8
BAND_PAST = BAND_CHUNKS * CHUNK
A_WIDTH = 512
A_HEADS = 8
A_HEAD_DIM = 64
MAX_REL = 128
B_WIDTH = 512
B_HEADS = 4
B_DV = 128
B_DK = 64
B_KWIDTH = 256
GATE_RANK = 16
GATE_TAU = 16.0
N_GROUPS = 4
EXPERTS_PER_GROUP = 8
N_EXPERTS = 32
EXPERT_FF = 256
EPS = 1e-6

LANES = 128
IN_MAIN = 3 * A_WIDTH + 2 * B_KWIDTH + 2 * B_WIDTH
TOK_TILE = 512
ROWS_PER_TILE = TOK_TILE // CHUNK
OUT_SUB = 1
Q_CHUNKS = 4
Q_ROWS = Q_CHUNKS * CHUNK
ROLL_W = 1024
NEG = -1e30
ROUTE_OFF = N_GROUPS
VMEM_LIMIT = 56 * 1024 * 1024


def _params(n_axes=1):
    return pltpu.CompilerParams(dimension_semantics=("arbitrary",) * n_axes,
                                vmem_limit_bytes=VMEM_LIMIT)


def _split(a):
    hi = a.astype(BF16)
    lo = (a - hi.astype(F32)).astype(BF16)
    return hi, lo


def _dot(a, b):
    return jnp.dot(a, b, preferred_element_type=F32)


def _dot3(a, b):
    ah, al = _split(a)
    bh, bl = _split(b)
    return _dot(ah, bh) + _dot(al, bh) + _dot(ah, bl)


def _dot_nt(a, b):
    return lax.dot_general(a, b, (((1,), (1,)), ((), ())), preferred_element_type=F32)


def _dot_tn(a, b):
    return lax.dot_general(a, b, (((0,), (0,)), ((), ())), preferred_element_type=F32)


def _silu(x):
    return x / (1.0 + jnp.exp(-x))


def _rows_to_tokens(rows, n):
    r = rows.shape[0]
    return jnp.broadcast_to(rows[:, None, :], (r, CHUNK, n)).reshape(r * CHUNK, n)


def _adaln_kernel(c_ref, w_ref, b_ref, o_ref):
    a = _silu(c_ref[...])
    o_ref[...] = _dot3(a, w_ref[...]) + b_ref[...]


def _adaln(c_rows, w_ada, b_ada):
    r = c_rows.shape[0]
    n = w_ada.shape[1]
    tn = 1024
    return pl.pallas_call(
        _adaln_kernel,
        grid=(n // tn,),
        in_specs=[pl.BlockSpec((r, D_MODEL), lambda j: (0, 0)),
                  pl.BlockSpec((D_MODEL, tn), lambda j: (0, j)),
                  pl.BlockSpec((1, tn), lambda j: (0, j))],
        out_specs=pl.BlockSpec((r, tn), lambda j: (0, j)),
        out_shape=jax.ShapeDtypeStruct((r, n), F32),
        compiler_params=_params(),
        name="adaln",
    )(c_rows, w_ada, b_ada.reshape(1, n))


def _head_rms(z, bd_ref, g):
    ms = _dot((z * z).astype(BF16), bd_ref[...]) * (1.0 / A_HEAD_DIM)
    return z * lax.rsqrt(ms + EPS) * g


def _inproj_kernel(n_ptiles, xp_ref, xs_ref, mod_ref, gmix_ref, w_ref, wgr_ref, bd_ref, gq_ref, gk_ref,
                   wgu_ref, bg_ref,
                   q_ref, k_ref, v_ref, kf_ref, vf_ref, gla_ref, la_ref):
    i = pl.program_id(0)
    x = jnp.where(i < n_ptiles, xp_ref[...], xs_ref[...])
    ms = jnp.mean(x * x, axis=-1, keepdims=True)
    xn = x * lax.rsqrt(ms + EPS) * gmix_ref[...]
    sh = _rows_to_tokens(mod_ref[:, 0:D_MODEL], D_MODEL)
    sc = _rows_to_tokens(mod_ref[:, D_MODEL:2 * D_MODEL], D_MODEL)
    hb = (xn * (1.0 + sc) + sh).astype(BF16)

    zq = _dot(hb, w_ref[:, 0:A_WIDTH])
    q_ref[...] = (_head_rms(zq, bd_ref, gq_ref[...]) * (LOG2E * A_HEAD_DIM ** -0.5)).astype(BF16)
    zk = _dot(hb, w_ref[:, A_WIDTH:2 * A_WIDTH])
    kn = _head_rms(zk, bd_ref, gk_ref[...])
    k_ref[...] = kn.astype(BF16)
    kf_ref[...] = kn
    zv = _dot(hb, w_ref[:, 2 * A_WIDTH:3 * A_WIDTH])
    v_ref[...] = zv.astype(BF16)
    vf_ref[...] = zv

    o = 3 * A_WIDTH
    zqb = _dot(hb, w_ref[:, o:o + B_KWIDTH]) * (B_DK ** -0.5)
    gla_ref[:, 0:B_KWIDTH] = zqb.astype(BF16)
    for c in range(B_KWIDTH, 2 * B_KWIDTH + 2 * B_WIDTH, 256):
        gla_ref[:, c:c + 256] = _dot(hb, w_ref[:, o + c:o + c + 256]).astype(BF16)

    gr = _dot(hb, wgr_ref[...])
    logit = _dot(gr.astype(BF16), wgu_ref[...]) + bg_ref[...]
    log_sig = jnp.minimum(logit, 0.0) - jnp.log1p(jnp.exp(-jnp.abs(logit)))
    la_ref[...] = log_sig * (1.0 / GATE_TAU)


def _inproj(xp, xs, mod, gmix, w_main, w_gr, bd, gq, gk, wgu_p, bg, n_ptiles, n_stiles, prep):
    n_tiles = n_ptiles + n_stiles
    t = n_tiles * TOK_TILE
    tail_tiles = 1 + n_stiles
    pblocks = prep // ROWS_PER_TILE
    const = lambda i: (0, 0)
    row = lambda i: (i, 0)
    tail = lambda i: (jnp.maximum(i - (n_ptiles - 1), 0), 0)
    return pl.pallas_call(
        functools.partial(_inproj_kernel, n_ptiles),
        grid=(n_tiles,),
        in_specs=[pl.BlockSpec((TOK_TILE, D_MODEL), lambda i: (jnp.minimum(i, n_ptiles - 1), 0)),
                  pl.BlockSpec((TOK_TILE, D_MODEL), lambda i: (jnp.maximum(i - n_ptiles, 0), 0)),
                  pl.BlockSpec((ROWS_PER_TILE, 6 * D_MODEL),
                               lambda i: (jnp.maximum(i - n_ptiles + pblocks, 0), 0)),
                  pl.BlockSpec((1, D_MODEL), const),
                  pl.BlockSpec((D_MODEL, IN_MAIN), const),
                  pl.BlockSpec((D_MODEL, LANES), const),
                  pl.BlockSpec((A_WIDTH, A_WIDTH), const),
                  pl.BlockSpec((1, A_WIDTH), const),
                  pl.BlockSpec((1, A_WIDTH), const),
                  pl.BlockSpec((LANES, B_KWIDTH), const),
                  pl.BlockSpec((1, B_KWIDTH), const)],
        out_specs=[pl.BlockSpec((TOK_TILE, A_WIDTH), row),
                   pl.BlockSpec((TOK_TILE, A_WIDTH), row),
                   pl.BlockSpec((TOK_TILE, A_WIDTH), row),
                   pl.BlockSpec((TOK_TILE, A_WIDTH), tail),
                   pl.BlockSpec((TOK_TILE, A_WIDTH), tail),
                   pl.BlockSpec((TOK_TILE, 2 * B_KWIDTH + 2 * B_WIDTH), row),
                   pl.BlockSpec((TOK_TILE, B_KWIDTH), row)],
        out_shape=[jax.ShapeDtypeStruct((t, A_WIDTH), BF16),
                   jax.ShapeDtypeStruct((t, A_WIDTH), BF16),
                   jax.ShapeDtypeStruct((t, A_WIDTH), BF16),
                   jax.ShapeDtypeStruct((tail_tiles * TOK_TILE, A_WIDTH), F32),
                   jax.ShapeDtypeStruct((tail_tiles * TOK_TILE, A_WIDTH), F32),
                   jax.ShapeDtypeStruct((t, 2 * B_KWIDTH + 2 * B_WIDTH), BF16),
                   jax.ShapeDtypeStruct((t, B_KWIDTH), F32)],
        compiler_params=_params(),
        name="inproj",
    )(xp, xs, mod, gmix, w_main, w_gr, bd, gq, gk, wgu_p, bg)


def _bias_lanes(n_keys):
    l = np.arange(ROLL_W)
    d = np.where(l < n_keys, BAND_PAST - l, BAND_PAST - l + ROLL_W)
    return np.clip(d, -(CHUNK - 1), MAX_REL) + (CHUNK - 1)


LOG2E = 1.4426950408889634


def _band_mask(m_rows, n_keys, first_col):
    qi = lax.broadcasted_iota(jnp.int32, (m_rows, n_keys), 0) >> LOG_CHUNK
    kw = lax.broadcasted_iota(jnp.int32, (m_rows, n_keys), 1)
    kc = kw >> LOG_CHUNK
    return (kc >= qi) & (kc <= qi + BAND_CHUNKS) & (kw >= first_col)


def _bias_tile(u_ref, h, ok):
    m_rows, n_keys = ok.shape
    src = jnp.broadcast_to(u_ref[h:h + 1, :] * LOG2E, (m_rows, ROLL_W))
    toe = pltpu.roll(src, 0, 1, stride=1, stride_axis=0)
    return jnp.where(ok, toe[:, 0:n_keys], NEG)


def _attend(q, kcat, vcat, bias_sc):
    m_rows = q.shape[0]
    first = lax.broadcasted_iota(jnp.int32, (m_rows, LANES), 1) < A_HEAD_DIM
    outs = []
    for p in range(A_HEADS // 2):
        lanes = slice(p * LANES, (p + 1) * LANES)
        qp, kp, vp = q[:, lanes], kcat[:, lanes], vcat[:, lanes]
        zero = jnp.zeros_like(qp)
        q2 = jnp.concatenate([jnp.where(first, qp, zero), jnp.where(first, zero, qp)], axis=0)
        s = _dot_nt(q2, kp) + bias_sc[p]
        e = jnp.exp2(s - jnp.max(s, axis=-1, keepdims=True))
        l = jnp.sum(e, axis=-1, keepdims=True)
        o2 = _dot(e.astype(BF16), vp) / l
        outs.append(jnp.where(first, o2[0:m_rows], o2[m_rows:2 * m_rows]))
    return jnp.concatenate(outs, axis=-1)


ATTN_SUB = 4
ATTN_WIN = 3


def _attn_prompt_kernel(u_ref, q_ref, *refs):
    k_refs = refs[0:ATTN_SUB + ATTN_WIN - 1]
    v_refs = refs[ATTN_SUB + ATTN_WIN - 1:2 * (ATTN_SUB + ATTN_WIN - 1)]
    o_ref, bias_sc = refs[-2:]
    j = pl.program_id(0)
    n_keys = ATTN_WIN * Q_ROWS

    @pl.when(j == 0)
    def _():
        for g in range(ATTN_WIN):
            ok = _band_mask(Q_ROWS, n_keys, (ATTN_WIN - 1 - g) * Q_ROWS)
            for h in range(A_HEADS):
                bias_sc[g, h // 2, (h % 2) * Q_ROWS:(h % 2 + 1) * Q_ROWS, :] = _bias_tile(u_ref, h, ok)

    ks = [r[...] for r in k_refs]
    vs = [r[...] for r in v_refs]
    for sub in range(ATTN_SUB):
        rows = slice(sub * Q_ROWS, (sub + 1) * Q_ROWS)
        kcat = jnp.concatenate(ks[sub:sub + ATTN_WIN], axis=0)
        vcat = jnp.concatenate(vs[sub:sub + ATTN_WIN], axis=0)
        bias = bias_sc.at[jnp.minimum(ATTN_SUB * j + sub, ATTN_WIN - 1)]
        o_ref[rows, :] = _attend(q_ref[rows, :], kcat, vcat, bias).astype(BF16)


def _attn_prompt(u, q, k, v, n_steps):
    const = lambda j: (0, 0)
    n_blk = ATTN_SUB + ATTN_WIN - 1
    blk = lambda d: pl.BlockSpec((Q_ROWS, A_WIDTH),
                                 lambda j, d=d: (jnp.maximum(ATTN_SUB * j - (ATTN_WIN - 1) + d, 0), 0))
    step_rows = ATTN_SUB * Q_ROWS
    return pl.pallas_call(
        _attn_prompt_kernel,
        grid=(n_steps,),
        in_specs=[pl.BlockSpec((A_HEADS, ROLL_W), const), pl.BlockSpec((step_rows, A_WIDTH), lambda j: (j, 0))]
                 + [blk(d) for d in range(n_blk)] * 2,
        out_specs=pl.BlockSpec((step_rows, A_WIDTH), lambda j: (j, 0)),
        out_shape=jax.ShapeDtypeStruct((n_steps * step_rows, A_WIDTH), BF16),
        scratch_shapes=[pltpu.VMEM((ATTN_WIN, A_HEADS // 2, 2 * Q_ROWS, ATTN_WIN * Q_ROWS), F32)],
        compiler_params=_params(),
        name="attn_prompt",
    )(u, q, *([k] * n_blk), *([v] * n_blk))


SAMPLE_KEYS = BAND_PAST + 2 * CHUNK
SAMPLE_STREAMS = 4


def _attn_sample_kernel(u_ref, q_ref, kn_ref, vn_ref, kc_ref, vc_ref, o_ref, bias_sc):
    @pl.when(pl.program_id(0) == 0)
    def _():
        ok = _band_mask(CHUNK, SAMPLE_KEYS, 0)
        for p in range(A_HEADS // 2):
            pair = jnp.concatenate([_bias_tile(u_ref, 2 * p, ok), _bias_tile(u_ref, 2 * p + 1, ok)], axis=0)
            bias_sc[p] = pair.T

    pad = jnp.zeros((CHUNK, A_WIDTH), BF16)
    lane = lax.broadcasted_iota(jnp.int32, (CHUNK, LANES), 1)
    first = lane < A_HEAD_DIM
    zero = jnp.zeros((CHUNK, LANES), BF16)
    for n in range(SAMPLE_STREAMS):
        rows = slice(n * CHUNK, (n + 1) * CHUNK)
        kcat = jnp.concatenate([kc_ref[n].astype(BF16), kn_ref[rows, :], pad], axis=0)
        vcat = jnp.concatenate([vc_ref[n].astype(BF16), vn_ref[rows, :], pad], axis=0)
        q = q_ref[rows, :]
        outs = []
        for p in range(A_HEADS // 2):
            lanes = slice(p * LANES, (p + 1) * LANES)
            qp = q[:, lanes]
            q_rows = jnp.concatenate([jnp.where(first, qp, zero), jnp.where(first, zero, qp)], axis=0)
            s = _dot_nt(kcat[:, lanes], q_rows) + bias_sc[p]
            e = jnp.exp2(s - jnp.max(s, axis=0, keepdims=True))
            pn = (e * (1.0 / jnp.sum(e, axis=0, keepdims=True))).astype(BF16)
            r = _dot_tn(pn, vcat[:, lanes])
            outs.append(jnp.where(first, r[0:CHUNK], r[CHUNK:2 * CHUNK]))
        o_ref[rows, :] = jnp.concatenate(outs, axis=-1).astype(BF16)


def _attn_sample(u, q, k, v, kc, vc, first_chunk, n_seq):
    assert n_seq % SAMPLE_STREAMS == 0 and first_chunk % SAMPLE_STREAMS == 0
    rows = SAMPLE_STREAMS * CHUNK
    new = pl.BlockSpec((rows, A_WIDTH), lambda b: (first_chunk // SAMPLE_STREAMS + b, 0))
    cache = pl.BlockSpec((SAMPLE_STREAMS, BAND_PAST, A_WIDTH), lambda b: (b, 0, 0))
    return pl.pallas_call(
        _attn_sample_kernel,
        grid=(n_seq // SAMPLE_STREAMS,),
        in_specs=[pl.BlockSpec((A_HEADS, ROLL_W), lambda b: (0, 0)), new, new, new, cache, cache],
        out_specs=pl.BlockSpec((rows, A_WIDTH), lambda b: (b, 0)),
        out_shape=jax.ShapeDtypeStruct((n_seq * CHUNK, A_WIDTH), BF16),
        scratch_shapes=[pltpu.VMEM((A_HEADS // 2, SAMPLE_KEYS, LANES), F32)],
        compiler_params=_params(),
        name="attn_sample",
    )(u, q, k, v, kc, vc)


GLA_CHUNKS = 4
GLA_SUB = 4


def _gla_block(n_chunks, gla_ref, la_ref, ltri_ref, g_ref, st_sc, o_ref):
    rows = n_chunks * CHUNK
    la = la_ref[...]
    la_hi, la_lo = _split(la)
    b = _dot(ltri_ref[...], la_hi) + _dot(ltri_ref[...], la_lo)
    b3 = b.reshape(n_chunks, CHUNK, B_KWIDTH)
    b_mid = b3[:, CHUNK // 2 - 1:CHUNK // 2, :]
    b_last = b3[:, CHUNK - 1:CHUNK, :]
    q = gla_ref[:, 0:B_KWIDTH].astype(F32).reshape(n_chunks, CHUNK, B_KWIDTH)
    k = gla_ref[:, B_KWIDTH:2 * B_KWIDTH].astype(F32).reshape(n_chunks, CHUNK, B_KWIDTH)
    q_start = (q * jnp.exp(b3)).reshape(rows, B_KWIDTH).astype(BF16)
    q_mid = (q * jnp.exp(b3 - b_mid)).reshape(rows, B_KWIDTH).astype(BF16)
    k_mid = (k * jnp.exp(b_mid - b3)).reshape(rows, B_KWIDTH).astype(BF16)
    k_end = (k * jnp.exp(b_last - b3)).reshape(rows, B_KWIDTH).astype(BF16)
    dec = jnp.exp(b_last)

    ti = lax.broadcasted_iota(jnp.int32, (2 * rows, rows), 0) & (rows - 1)
    si = lax.broadcasted_iota(jnp.int32, (2 * rows, rows), 1)
    causal = (si <= ti) & ((si >> LOG_CHUNK) == (ti >> LOG_CHUNK))
    first_r = lax.broadcasted_iota(jnp.int32, (rows, LANES), 1) < B_DK
    first_c = lax.broadcasted_iota(jnp.int32, (CHUNK, LANES), 1) < B_DK
    first_s = lax.broadcasted_iota(jnp.int32, (B_DV, LANES), 1) < B_DK

    def stack_heads(x, first):
        zero = jnp.zeros_like(x)
        return jnp.concatenate([jnp.where(first, x, zero), jnp.where(first, zero, x)], axis=0)

    for p in range(B_HEADS // 2):
        lanes = slice(p * LANES, (p + 1) * LANES)
        qs_p, qm_p, km_p, ke_p = q_start[:, lanes], q_mid[:, lanes], k_mid[:, lanes], k_end[:, lanes]
        v_pair = gla_ref[:, 2 * B_KWIDTH + 2 * p * B_DV:2 * B_KWIDTH + (2 * p + 2) * B_DV]
        sc = jnp.where(causal, _dot_nt(stack_heads(qm_p, first_r), km_p), 0.0)
        o2 = _dot(sc.astype(BF16), v_pair)
        intra = [o2[0:rows, 0:B_DV], o2[rows:2 * rows, B_DV:2 * B_DV]]
        inter = [[], []]
        st = st_sc[p]
        for c in range(n_chunks):
            cr = slice(c * CHUNK, (c + 1) * CHUNK)
            r2 = _dot_nt(stack_heads(qs_p[cr], first_c), st.astype(BF16))
            inter[0].append(r2[0:CHUNK])
            inter[1].append(r2[CHUNK:2 * CHUNK])
            u2 = _dot_tn(v_pair[cr], ke_p[cr])
            st = st * dec[c, :, lanes] + jnp.where(first_s, u2[0:B_DV], u2[B_DV:2 * B_DV])
        st_sc[p] = st
        for hh in range(2):
            h = 2 * p + hh
            o = intra[hh] + jnp.concatenate(inter[hh], axis=0)
            ms = jnp.mean(o * o, axis=-1, keepdims=True)
            on = o * lax.rsqrt(ms + EPS) * g_ref[...]
            r = gla_ref[:, 2 * B_KWIDTH + B_WIDTH + h * B_DV:2 * B_KWIDTH + B_WIDTH + (h + 1) * B_DV]
            o_ref[:, h * B_DV:(h + 1) * B_DV] = (on * _silu(r.astype(F32))).astype(BF16)


def _gla_prompt_kernel(gla_ref, la_ref, ltri_ref, g_ref, o_ref, sfin_ref, st_sc):
    @pl.when(pl.program_id(0) == 0)
    def _():
        st_sc[...] = jnp.zeros_like(st_sc)

    rows = GLA_CHUNKS * CHUNK
    for sub in range(GLA_SUB):
        part = pl.ds(sub * rows, rows)
        _gla_block(GLA_CHUNKS, gla_ref.at[part], la_ref.at[part], ltri_ref, g_ref, st_sc, o_ref.at[part])
    sfin_ref[...] = st_sc[...]


def _gla_sample_kernel(gla_ref, la_ref, ltri_ref, g_ref, s0_ref, o_ref, sfin_ref, st_sc):
    st_sc[...] = s0_ref[...]
    for n in range(SAMPLE_STREAMS):
        part = pl.ds(n * CHUNK, CHUNK)
        _gla_block(1, gla_ref.at[part], la_ref.at[part], ltri_ref, g_ref, st_sc.at[n], o_ref.at[part])
    sfin_ref[...] = st_sc[...]


def _ltri(n_chunks):
    r = np.arange(n_chunks * CHUNK)
    m = (r[None, :] <= r[:, None]) & (r[None, :] // CHUNK == r[:, None] // CHUNK)
    return jnp.asarray(m, BF16)


_GLA_W = 2 * B_KWIDTH + 2 * B_WIDTH
_ST_SHAPE = (B_HEADS // 2, B_DV, LANES)


def _gla_prompt(gla, la, g, n_steps):
    rows = GLA_SUB * GLA_CHUNKS * CHUNK
    const = lambda j: (0, 0)
    return pl.pallas_call(
        _gla_prompt_kernel,
        grid=(n_steps,),
        in_specs=[pl.BlockSpec((rows, _GLA_W), lambda j: (j, 0)),
                  pl.BlockSpec((rows, B_KWIDTH), lambda j: (j, 0)),
                  pl.BlockSpec((GLA_CHUNKS * CHUNK, GLA_CHUNKS * CHUNK), const),
                  pl.BlockSpec((1, B_DV), const)],
        out_specs=[pl.BlockSpec((rows, B_WIDTH), lambda j: (j, 0)),
                   pl.BlockSpec(_ST_SHAPE, lambda j: (0, 0, 0))],
        out_shape=[jax.ShapeDtypeStruct((n_steps * rows, B_WIDTH), BF16),
                   jax.ShapeDtypeStruct(_ST_SHAPE, F32)],
        scratch_shapes=[pltpu.VMEM(_ST_SHAPE, F32)],
        compiler_params=_params(),
        name="gla_prompt",
    )(gla, la, _ltri(GLA_CHUNKS), g)


def _gla_sample(gla, la, g, s0, first_chunk, n_seq):
    assert n_seq % SAMPLE_STREAMS == 0 and first_chunk % SAMPLE_STREAMS == 0
    const = lambda b: (0, 0)
    rows = SAMPLE_STREAMS * CHUNK
    first = first_chunk // SAMPLE_STREAMS
    st_spec = pl.BlockSpec((SAMPLE_STREAMS,) + _ST_SHAPE, lambda b: (b, 0, 0, 0))
    return pl.pallas_call(
        _gla_sample_kernel,
        grid=(n_seq // SAMPLE_STREAMS,),
        in_specs=[pl.BlockSpec((rows, _GLA_W), lambda b: (first + b, 0)),
                  pl.BlockSpec((rows, B_KWIDTH), lambda b: (first + b, 0)),
                  pl.BlockSpec((CHUNK, CHUNK), const),
                  pl.BlockSpec((1, B_DV), const),
                  st_spec],
        out_specs=[pl.BlockSpec((rows, B_WIDTH), lambda b: (b, 0)), st_spec],
        out_shape=[jax.ShapeDtypeStruct((n_seq * CHUNK, B_WIDTH), BF16),
                   jax.ShapeDtypeStruct((n_seq,) + _ST_SHAPE, F32)],
        scratch_shapes=[pltpu.VMEM((SAMPLE_STREAMS,) + _ST_SHAPE, F32)],
        compiler_params=_params(),
        name="gla_sample",
    )(gla, la, _ltri(1), g, s0)


def _state_to_pairs(s):
    lead = s.shape[:-3]
    s = s.reshape(lead + (B_HEADS // 2, 2, B_DK, B_DV))
    s = jnp.moveaxis(s, -1, -3)
    return s.reshape(lead + (B_HEADS // 2, B_DV, 2 * B_DK))


def _pairs_to_state(s):
    lead = s.shape[:-3]
    s = s.reshape(lead + (B_HEADS // 2, B_DV, 2, B_DK))
    s = jnp.moveaxis(s, -3, -1)
    return s.reshape(lead + (B_HEADS, B_DK, B_DV))


def _route(logits):
    lane = lax.broadcasted_iota(jnp.int32, logits.shape, 1)
    lane_f = lane.astype(F32)
    big = float(LANES)
    gmask = lane < N_GROUPS
    gl = jnp.where(gmask, logits, NEG)
    gmax = jnp.max(gl, axis=-1, keepdims=True)
    gsel = jnp.min(jnp.where(gl == gmax, lane_f, big), axis=-1, keepdims=True)
    gsum = jnp.sum(jnp.where(gmask, jnp.exp(gl - gmax), 0.0), axis=-1, keepdims=True)
    g_w = 1.0 / gsum
    e_lo = ROUTE_OFF + gsel * EXPERTS_PER_GROUP
    emask = (lane_f >= e_lo) & (lane_f < e_lo + EXPERTS_PER_GROUP)
    el = jnp.where(emask, logits, NEG)
    v1 = jnp.max(el, axis=-1, keepdims=True)
    i1 = jnp.min(jnp.where(el == v1, lane_f, big), axis=-1, keepdims=True)
    el2 = jnp.where(lane_f == i1, NEG, el)
    v2 = jnp.max(el2, axis=-1, keepdims=True)
    i2 = jnp.min(jnp.where(el2 == v2, lane_f, big), axis=-1, keepdims=True)
    t = jnp.exp(v2 - v1)
    w1 = g_w / (1.0 + t)
    w2 = g_w * t / (1.0 + t)
    return lane_f, i1, i2, w1, w2


ROW_PIECES = D_MODEL // 2 // LANES
SUBLANES = 8
ROW_TILE = ROW_PIECES * SUBLANES


def _pack_rows(z32_sc, x, rows):
    half = D_MODEL // 2
    out = []
    for c in range(ROW_PIECES):
        z32_sc[c, pl.ds(0, rows, stride=2), :] = x[:, c * LANES:(c + 1) * LANES]
        z32_sc[c, pl.ds(1, rows, stride=2), :] = x[:, half + c * LANES:half + (c + 1) * LANES]
        out.append(z32_sc[c].astype(BF16))
    return out


def _unpack_rows(z32_sc, pieces, rows):
    lo, hi = [], []
    for c in range(ROW_PIECES):
        z32_sc[c] = pieces[c].astype(F32)
        lo.append(z32_sc[c, pl.ds(0, rows, stride=2), :])
        hi.append(z32_sc[c, pl.ds(1, rows, stride=2), :])
    return jnp.concatenate(lo, axis=1), jnp.concatenate(hi, axis=1)


def _to_row_tiled(pieces, tokens):
    per_tile = pieces[0].shape[0] * SUBLANES // tokens
    return jnp.stack([p.reshape(tokens // SUBLANES, per_tile, LANES) for p in pieces], axis=1)


def _from_row_tiled(flat, tokens):
    per_tile = flat.shape[0] // (tokens // SUBLANES) // ROW_PIECES
    tiled = flat.reshape(tokens // SUBLANES, ROW_PIECES, per_tile, LANES)
    return [tiled[:, c].reshape(tokens // SUBLANES * per_tile, LANES) for c in range(ROW_PIECES)]


def _flatten_tiled(tiled):
    return tiled.reshape(-1, LANES)


def _outproj_kernel(n_ptiles, tiles_per_sb, oap_ref, oas_ref, obp_ref, obs_ref, wo_ref, xp_ref, xs_ref, mod_ref,
                    gffn_ref, wr_ref, br_ref, ltri_ref, x1_ref, h2p_ref, meta_ref, cnt_ref, z32_sc, cnt_sc):
    j = pl.program_id(0)
    is_prompt = j < n_ptiles // OUT_SUB
    for s in range(OUT_SUB):
        rows = slice(s * TOK_TILE, (s + 1) * TOK_TILE)
        mrows = slice(s * ROWS_PER_TILE, (s + 1) * ROWS_PER_TILE)
        x = jnp.where(is_prompt, xp_ref[rows, :], xs_ref[rows, :])
        oa = jnp.where(is_prompt, oap_ref[rows, :], oas_ref[rows, :])
        ob = jnp.where(is_prompt, obp_ref[rows, :], obs_ref[rows, :])
        mix = _dot(oa, wo_ref[0:A_WIDTH, :]) + _dot(ob, wo_ref[A_WIDTH:D_MODEL, :])
        gate1 = _rows_to_tokens(mod_ref[mrows, 2 * D_MODEL:3 * D_MODEL], D_MODEL)
        x1 = x + gate1 * mix
        x1_ref[rows, :] = x1
        ms = jnp.mean(x1 * x1, axis=-1, keepdims=True)
        xn = x1 * lax.rsqrt(ms + EPS) * gffn_ref[...]
        sh = _rows_to_tokens(mod_ref[mrows, 3 * D_MODEL:4 * D_MODEL], D_MODEL)
        sc = _rows_to_tokens(mod_ref[mrows, 4 * D_MODEL:5 * D_MODEL], D_MODEL)
        h2 = xn * (1.0 + sc) + sh
        words = [pltpu.bitcast(p, U32) for p in _pack_rows(z32_sc.at[s], h2, TOK_TILE)]
        tiles = slice(s * TOK_TILE // SUBLANES, (s + 1) * TOK_TILE // SUBLANES)
        h2p_ref[tiles] = _to_row_tiled(words, TOK_TILE)

        lane_f, i1, i2, w1, w2 = _route(_dot3(h2, wr_ref[...]) + br_ref[...])

        @pl.when(lax.rem(OUT_SUB * j + s, tiles_per_sb) == 0)
        def _():
            cnt_sc[...] = jnp.zeros_like(cnt_sc)

        sel = jnp.where((lane_f == i1) | (lane_f == i2), 1.0, 0.0).astype(BF16)
        before = _dot(ltri_ref[...], sel) + cnt_sc[0:1, :]
        rank1 = jnp.sum(jnp.where(lane_f == i1, before, 0.0), axis=-1, keepdims=True)
        rank2 = jnp.sum(jnp.where(lane_f == i2, before, 0.0), axis=-1, keepdims=True)
        cnt = cnt_sc[...] + _dot(jnp.ones((8, TOK_TILE), BF16), sel)
        cnt_sc[...] = cnt
        cnt_ref[0] = cnt
        cols = (i1, i2, rank1, rank2, w1, w2)
        meta = jnp.zeros_like(lane_f)
        for c, col in enumerate(cols):
            meta = jnp.where(lane_f == float(c), col, meta)
        meta_ref[rows, :] = meta


def _outproj(oa_p, oa_s, ob_p, ob_s, w_out, xp, xs, mod, gffn, wr, br, n_ptiles, n_stiles, prep, sb):
    n_tiles = n_ptiles + n_stiles
    t = n_tiles * TOK_TILE
    tiles_per_sb = sb // TOK_TILE
    assert n_ptiles % OUT_SUB == 0 and n_stiles % OUT_SUB == 0 and tiles_per_sb % OUT_SUB == 0
    step_rows = OUT_SUB * TOK_TILE
    mod_rows = OUT_SUB * ROWS_PER_TILE
    n_psteps = n_ptiles // OUT_SUB
    pblocks = prep // mod_rows
    const = lambda j: (0, 0)
    row = lambda j: (j, 0)
    prow = lambda j: (jnp.minimum(j, n_psteps - 1), 0)
    srow = lambda j: (jnp.maximum(j - n_psteps, 0), 0)
    r = np.arange(TOK_TILE)
    ltri = jnp.asarray(r[None, :] < r[:, None], BF16)
    return pl.pallas_call(
        functools.partial(_outproj_kernel, n_ptiles, tiles_per_sb),
        grid=(n_tiles // OUT_SUB,),
        in_specs=[pl.BlockSpec((step_rows, A_WIDTH), prow),
                  pl.BlockSpec((step_rows, A_WIDTH), srow),
                  pl.BlockSpec((step_rows, B_WIDTH), prow),
                  pl.BlockSpec((step_rows, B_WIDTH), srow),
                  pl.BlockSpec((D_MODEL, D_MODEL), const),
                  pl.BlockSpec((step_rows, D_MODEL), prow),
                  pl.BlockSpec((step_rows, D_MODEL), srow),
                  pl.BlockSpec((mod_rows, 6 * D_MODEL),
                               lambda j: (jnp.maximum(j - n_psteps + pblocks, 0), 0)),
                  pl.BlockSpec((1, D_MODEL), const),
                  pl.BlockSpec((D_MODEL, LANES), const),
                  pl.BlockSpec((1, LANES), const),
                  pl.BlockSpec((TOK_TILE, TOK_TILE), const)],
        out_specs=[pl.BlockSpec((step_rows, D_MODEL), row),
                   pl.BlockSpec((step_rows // SUBLANES, ROW_PIECES, SUBLANES, LANES), lambda j: (j, 0, 0, 0)),
                   pl.BlockSpec((step_rows, LANES), row),
                   pl.BlockSpec((1, 8, LANES), lambda j: (OUT_SUB * j // tiles_per_sb, 0, 0))],
        out_shape=[jax.ShapeDtypeStruct((t, D_MODEL), F32),
                   jax.ShapeDtypeStruct((t // SUBLANES, ROW_PIECES, SUBLANES, LANES), U32),
                   jax.ShapeDtypeStruct((t, LANES), F32),
                   jax.ShapeDtypeStruct((t // sb, 8, LANES), F32)],
        scratch_shapes=[pltpu.VMEM((OUT_SUB, ROW_PIECES, 2 * TOK_TILE, LANES), F32),
                        pltpu.VMEM((8, LANES), F32)],
        compiler_params=_params(),
        name="outproj",
    )(oa_p, oa_s, ob_p, ob_s, w_out, xp, xs, mod, gffn, wr, br, ltri)


MOE_SUPER_BLOCK = 2048
SEG_ALIGN = SUBLANES
CHUNK_BF16_ROWS = 2 * SEG_ALIGN * ROW_PIECES
SEG_BITS = 9
PAD_BITS = 5
FFN_ROWS = 512
PLAN_ROWS = LANES


def _local_rows(sb):
    return 2 * sb + N_EXPERTS * SEG_ALIGN


def _sorted_tiles(n_tokens, sb):
    rows = 2 * n_tokens + (n_tokens // sb) * N_EXPERTS * SEG_ALIGN + N_EXPERTS * FFN_ROWS
    return -(-rows // (FFN_SUB * FFN_ROWS)) * FFN_SUB


def _moe_plan_kernel(total_chunks, meta_ref, cnt_ref, ustrict_ref, lstrict_ref,
                     posw_ref, addr_ref, tab_ref, tile_ref):
    b = pl.program_id(0)
    per_tile = FFN_ROWS // SEG_ALIGN

    @pl.when(b == 0)
    def _():
        cnt = cnt_ref[...]
        chunks = jnp.floor((cnt + (SEG_ALIGN - 1)) * (1.0 / SEG_ALIGN))
        chunks_b = chunks.astype(BF16)
        loc = _dot(chunks_b, ustrict_ref[...])
        before = _dot(lstrict_ref[...], chunks_b)
        tot = _dot(jnp.ones((PLAN_ROWS, PLAN_ROWS), BF16), chunks_b)
        tiles = jnp.floor((tot + (per_tile - 1)) * (1.0 / per_tile))
        tile_off = _dot(tiles.astype(BF16), ustrict_ref[...])
        n_tiles = jnp.sum(tiles[0:1], axis=-1, keepdims=True)
        lane1 = lax.broadcasted_iota(jnp.int32, (PLAN_ROWS, LANES), 1)
        tail = lane1 == ROUTE_OFF + N_EXPERTS
        pad_off = jnp.where(tail, n_tiles * per_tile, tile_off * per_tile + tot)
        pad_n = jnp.where(tail, total_chunks - n_tiles * per_tile, tiles * per_tile - tot)
        row = lax.broadcasted_iota(jnp.int32, (PLAN_ROWS, LANES), 0)
        tab_ref[0] = loc
        tab_ref[1] = chunks
        tab_ref[2] = tile_off * per_tile + before
        tab_ref[3] = jnp.where(row == 0, pad_off, jnp.where(row == 1, pad_n, jnp.where(row == 2, n_tiles, 0.0)))
        t_idx = lax.broadcasted_iota(jnp.int32, tile_ref.shape, 0).astype(F32)
        lane_t = lax.broadcasted_iota(jnp.int32, tile_ref.shape, 1)
        is_expert = (lane_t >= ROUTE_OFF) & (lane_t < ROUTE_OFF + N_EXPERTS)
        ends = (tile_off + tiles)[0:1, :]
        owner = jnp.sum(jnp.where(is_expert & (ends <= t_idx), 1.0, 0.0), axis=-1, keepdims=True)
        tile_ref[...] = jnp.broadcast_to(jnp.minimum(owner, N_EXPERTS - 1.0), tile_ref.shape)

    own = jnp.floor((cnt_ref[pl.ds(b, 1), :] + (SEG_ALIGN - 1)) * (1.0 / SEG_ALIGN))
    own_off = _dot(jnp.broadcast_to(own, (SUBLANES, LANES)).astype(BF16), ustrict_ref[...]) * SEG_ALIGN
    meta = meta_ref[...]
    lane_f = lax.broadcasted_iota(jnp.int32, meta.shape, 1).astype(F32)
    off_row = own_off[0:1, :]
    pos = []
    for k in range(2):
        e_lane = meta[:, k:k + 1]
        base = jnp.sum(jnp.where(lane_f == e_lane, off_row, 0.0), axis=-1, keepdims=True)
        p = base + meta[:, 2 + k:3 + k]
        tile = jnp.floor(p * (1.0 / SUBLANES))
        pos.append(tile * (ROW_TILE - SUBLANES) + p)
    out = jnp.zeros_like(meta)
    for c, col in enumerate((pos[0], pos[1], meta[:, 4:5], meta[:, 5:6])):
        out = jnp.where(lane_f == float(c), col, out)
    posw_ref[...] = out
    addr_ref[0] = out.T[0:SUBLANES]


def _moe_plan(meta, cnt, sb):
    n_blocks = meta.shape[0] // sb
    assert n_blocks <= PLAN_ROWS and sb // SEG_ALIGN <= 256
    n_tiles = _sorted_tiles(meta.shape[0], sb)
    tile_rows = -(-n_tiles // SUBLANES) * SUBLANES
    r = np.arange(LANES)
    ustrict = jnp.asarray(r[:, None] < r[None, :], BF16)
    lstrict = jnp.asarray(r[None, :] < r[:, None], BF16)
    cnt_all = jnp.pad(cnt[:, 0, :], ((0, PLAN_ROWS - n_blocks), (0, 0)))
    const = lambda s: (0, 0)
    posw, addr, tab, tile_owner = pl.pallas_call(
        functools.partial(_moe_plan_kernel, float(n_tiles * (FFN_ROWS // SEG_ALIGN))),
        grid=(n_blocks,),
        in_specs=[pl.BlockSpec((sb, LANES), lambda s: (s, 0)),
                  pl.BlockSpec((PLAN_ROWS, LANES), const),
                  pl.BlockSpec((LANES, LANES), const),
                  pl.BlockSpec((PLAN_ROWS, PLAN_ROWS), const)],
        out_specs=[pl.BlockSpec((sb, LANES), lambda s: (s, 0)),
                   pl.BlockSpec((1, SUBLANES, sb), lambda s: (s, 0, 0)),
                   pl.BlockSpec((4, PLAN_ROWS, LANES), lambda s: (0, 0, 0)),
                   pl.BlockSpec((tile_rows, LANES), const)],
        out_shape=[jax.ShapeDtypeStruct(meta.shape, F32),
                   jax.ShapeDtypeStruct((n_blocks, SUBLANES, sb), F32),
                   jax.ShapeDtypeStruct((4, PLAN_ROWS, LANES), F32),
                   jax.ShapeDtypeStruct((tile_rows, LANES), F32)],
        compiler_params=_params(),
        name="moe_plan",
    )(meta, cnt_all, ustrict, lstrict)
    experts = slice(ROUTE_OFF, ROUTE_OFF + N_EXPERTS)
    to_i32 = lambda x: x.astype(jnp.int32).reshape(-1)
    plan = dict(
        loc=to_i32(tab[0, :n_blocks, experts]), n=to_i32(tab[1, :n_blocks, experts]),
        dst=to_i32(tab[2, :n_blocks, experts]),
        pad_off=to_i32(tab[3, 0, ROUTE_OFF:ROUTE_OFF + N_EXPERTS + 1]),
        pad_n=to_i32(tab[3, 1, ROUTE_OFF:ROUTE_OFF + N_EXPERTS + 1]),
        n_tiles=to_i32(tab[3, 2, 0:1]),
        owner=to_i32(tile_owner[:n_tiles, 0]))
    return posw, to_i32(addr[:, 0, :]), to_i32(addr[:, 1, :]), plan


def _token_rows(start):
    return pl.ds(start, ROW_PIECES, stride=SUBLANES)


def _pow2_copies(src_ref, dst_ref, src_chunk, dst_chunk, n, n_bits, sem, act):
    done = 0
    for k in reversed(range(n_bits)):
        take = (n >> k) & 1
        rows = CHUNK_BF16_ROWS << k
        src0 = 0 if src_chunk is None else pl.multiple_of((src_chunk + done) * CHUNK_BF16_ROWS, CHUNK_BF16_ROWS)
        dst0 = pl.multiple_of((dst_chunk + done) * CHUNK_BF16_ROWS, CHUNK_BF16_ROWS)

        @pl.when(take == 1)
        def _(src0=src0, dst0=dst0, rows=rows):
            act(pltpu.make_async_copy(src_ref.at[pl.ds(src0, rows)], dst_ref.at[pl.ds(dst0, rows)], sem))

        done = done + take * (1 << k)


def _segment_copies(block, loc_ref, n_ref, dst_ref, local_ref, global_ref, to_global, sem, act):
    def per_expert(e, carry):
        seg = block * N_EXPERTS + e
        if to_global:
            _pow2_copies(local_ref, global_ref, loc_ref[seg], dst_ref[seg], n_ref[seg], SEG_BITS, sem, act)
        else:
            _pow2_copies(global_ref, local_ref, dst_ref[seg], loc_ref[seg], n_ref[seg], SEG_BITS, sem, act)
        return carry

    lax.fori_loop(0, N_EXPERTS, per_expert, 0)


def _zero_fill(zero_ref, global_ref, padoff_ref, padn_ref, sem, act):
    full = 1 << PAD_BITS

    def per_pad(e, carry):
        def per_full(c, inner):
            dst0 = pl.multiple_of((padoff_ref[e] + c * full) * CHUNK_BF16_ROWS, CHUNK_BF16_ROWS)
            act(pltpu.make_async_copy(zero_ref, global_ref.at[pl.ds(dst0, full * CHUNK_BF16_ROWS)], sem))
            return inner

        n_full = padn_ref[e] >> PAD_BITS
        lax.fori_loop(0, n_full, per_full, 0)
        _pow2_copies(zero_ref, global_ref, None, padoff_ref[e] + n_full * full, padn_ref[e] & (full - 1),
                     PAD_BITS, sem, act)
        return carry

    lax.fori_loop(0, N_EXPERTS + 1, per_pad, 0)


STAGE_SLAB = 1024


def _restage(src_sc, dst_sc, dst_dtype):
    ratio = dst_sc.shape[0] / src_sc.shape[0]
    n_slabs = src_sc.shape[0] // (STAGE_SLAB if ratio > 1 else 2 * STAGE_SLAB)
    src_rows = src_sc.shape[0] // n_slabs
    dst_rows = dst_sc.shape[0] // n_slabs

    def slab(i, carry):
        s0 = pl.multiple_of(i * src_rows, src_rows)
        d0 = pl.multiple_of(i * dst_rows, dst_rows)
        dst_sc[pl.ds(d0, dst_rows), :] = pltpu.bitcast(src_sc[pl.ds(s0, src_rows), :], dst_dtype)
        return carry

    lax.fori_loop(0, n_slabs, slab, 0)


def _moe_dispatch_kernel(sb, n_blocks, loc_ref, n_ref, dst_ref, padoff_ref, padn_ref,
                         h2p_ref, a1_ref, a2_ref, xs_hbm, local_sc, stage_sc, zero_sc, sems, zero_sem):
    b = pl.program_id(0)
    slot = b & 1

    def segments(block, buf, act):
        _segment_copies(block, loc_ref, n_ref, dst_ref, stage_sc.at[buf], xs_hbm, True, sems.at[buf], act)

    local_sc[...] = jnp.zeros_like(local_sc)

    def step(g, carry):
        src = pl.multiple_of(g * ROW_TILE, ROW_TILE)
        for u in range(SUBLANES):
            t = g * SUBLANES + u
            row = h2p_ref[_token_rows(src + u), :]
            local_sc[_token_rows(a1_ref[t]), :] = row
            local_sc[_token_rows(a2_ref[t]), :] = row
        return carry

    lax.fori_loop(0, sb // SUBLANES, step, 0, unroll=2)

    @pl.when(b > 0)
    def _():
        segments(b - 1, 1 - slot, lambda c: c.wait())

    _restage(local_sc, stage_sc.at[slot], BF16)
    segments(b, slot, lambda c: c.start())

    @pl.when(b == 0)
    def _():
        zero_sc[...] = jnp.zeros_like(zero_sc)
        _zero_fill(zero_sc, xs_hbm, padoff_ref, padn_ref, zero_sem, lambda c: c.start())
        _zero_fill(zero_sc, xs_hbm, padoff_ref, padn_ref, zero_sem, lambda c: c.wait())

    @pl.when(b == n_blocks - 1)
    def _():
        segments(b, slot, lambda c: c.wait())


def _smem_vec(n, index_map):
    return pl.BlockSpec((n,), index_map, memory_space=pltpu.SMEM)


def _moe_dispatch(h2p, a1, a2, plan, sb, n_tiles):
    n_blocks = h2p.shape[0] // (sb * ROW_PIECES)
    local_flat = _local_rows(sb) * ROW_PIECES
    vec = _smem_vec(sb, lambda s, *_: (s,))
    return pl.pallas_call(
        functools.partial(_moe_dispatch_kernel, sb, n_blocks),
        grid_spec=pltpu.PrefetchScalarGridSpec(
            num_scalar_prefetch=5,
            grid=(n_blocks,),
            in_specs=[pl.BlockSpec((sb * ROW_PIECES, LANES), lambda s, *_: (s, 0)), vec, vec],
            out_specs=pl.BlockSpec(memory_space=pl.ANY),
            scratch_shapes=[pltpu.VMEM((local_flat, LANES), U32),
                            pltpu.VMEM((2, 2 * local_flat, LANES), BF16),
                            pltpu.VMEM(((1 << PAD_BITS) * CHUNK_BF16_ROWS, LANES), BF16),
                            pltpu.SemaphoreType.DMA((2,)),
                            pltpu.SemaphoreType.DMA(())]),
        out_shape=jax.ShapeDtypeStruct((n_tiles * FFN_ROWS * ROW_PIECES * 2, LANES), BF16),
        compiler_params=_params(),
        name="moe_dispatch",
    )(plan["loc"], plan["n"], plan["dst"], plan["pad_off"], plan["pad_n"], h2p, a1, a2)


FFN_SUB = 2


def _moe_ffn_kernel(owner_ref, ntiles_ref, xs_ref, *refs):
    w_refs = refs[0:3 * FFN_SUB]
    ys_ref, wg_sc, wu_sc, wd_sc, z32_sc = refs[3 * FFN_SUB:]
    half = D_MODEL // 2
    flat = FFN_ROWS * ROW_PIECES * 2
    for s in range(FFN_SUB):
        t = pl.program_id(0) * FFN_SUB + s
        wg_ref, wu_ref, wd_ref = w_refs[3 * s:3 * s + 3]
        rows = slice(s * flat, (s + 1) * flat)
        used = t < ntiles_ref[0]

        @pl.when(used & ((t < FFN_SUB) | (owner_ref[t] != owner_ref[jnp.maximum(t - FFN_SUB, 0)])))
        def _():
            wg_sc[s] = wg_ref[0].astype(BF16)
            wu_sc[s] = wu_ref[0].astype(BF16)
            wd_sc[s] = wd_ref[0].astype(BF16)

        @pl.when(used)
        def _():
            z32 = z32_sc.at[s]
            lo, hi = _unpack_rows(z32, _from_row_tiled(xs_ref[rows, :], FFN_ROWS), FFN_ROWS)
            lo, hi = lo.astype(BF16), hi.astype(BF16)
            g = _dot(lo, wg_sc[s, 0:half, :]) + _dot(hi, wg_sc[s, half:D_MODEL, :])
            u = _dot(lo, wu_sc[s, 0:half, :]) + _dot(hi, wu_sc[s, half:D_MODEL, :])
            y = _dot((_silu(g) * u).astype(BF16), wd_sc[s])
            ys_ref[rows, :] = _flatten_tiled(_to_row_tiled(_pack_rows(z32, y, FFN_ROWS), FFN_ROWS))

        @pl.when(jnp.logical_not(used))
        def _():
            ys_ref[rows, :] = jnp.zeros((flat, LANES), BF16)


def _moe_ffn(xs, plan, wg, wu, wd):
    flat = FFN_ROWS * ROW_PIECES * 2
    n_tiles = xs.shape[0] // flat
    assert n_tiles % FFN_SUB == 0

    def wspec(shape, s):
        tile = lambda i, owner, nt: jnp.minimum(i * FFN_SUB + s, nt[0] - 1)
        return pl.BlockSpec((1,) + shape, lambda i, owner, nt: (owner[tile(i, owner, nt)], 0, 0))

    w_specs, w_args = [], []
    for s in range(FFN_SUB):
        w_specs += [wspec((D_MODEL, EXPERT_FF), s), wspec((D_MODEL, EXPERT_FF), s), wspec((EXPERT_FF, D_MODEL), s)]
        w_args += [wg, wu, wd]
    last_step = lambda i, owner, nt: jnp.minimum(i, (nt[0] - 1) // FFN_SUB)
    return pl.pallas_call(
        _moe_ffn_kernel,
        grid_spec=pltpu.PrefetchScalarGridSpec(
            num_scalar_prefetch=2,
            grid=(n_tiles // FFN_SUB,),
            in_specs=[pl.BlockSpec((FFN_SUB * flat, LANES), lambda i, owner, nt: (last_step(i, owner, nt), 0))]
                     + w_specs,
            out_specs=pl.BlockSpec((FFN_SUB * flat, LANES), lambda i, owner, nt: (i, 0)),
            scratch_shapes=[pltpu.VMEM((FFN_SUB, D_MODEL, EXPERT_FF), BF16),
                            pltpu.VMEM((FFN_SUB, D_MODEL, EXPERT_FF), BF16),
                            pltpu.VMEM((FFN_SUB, EXPERT_FF, D_MODEL), BF16),
                            pltpu.VMEM((FFN_SUB, ROW_PIECES, 2 * FFN_ROWS, LANES), F32)]),
        out_shape=jax.ShapeDtypeStruct(xs.shape, BF16),
        compiler_params=_params(),
        name="moe_ffn",
    )(plan["owner"], plan["n_tiles"], xs, *w_args)


def _moe_combine_kernel(n_psb, n_blocks, loc_ref, n_ref, dst_ref,
                        ys_hbm, a1_ref, a2_ref, posw_ref, x1_ref, mod_ref, yp_ref, yo_ref,
                        local_sc, stage_sc, g1_sc, g2_sc, z32_sc, sems):
    s = pl.program_id(0)
    slot = s & 1

    def segments(block, buf, act):
        _segment_copies(block, loc_ref, n_ref, dst_ref, stage_sc.at[buf], ys_hbm, False, sems.at[buf], act)

    @pl.when(pl.program_id(1) == 0)
    def _():
        @pl.when(s == 0)
        def _():
            segments(s, slot, lambda c: c.start())

        segments(s, slot, lambda c: c.wait())
        _restage(stage_sc.at[slot], local_sc, U32)

        @pl.when(s + 1 < n_blocks)
        def _():
            segments(s + 1, 1 - slot, lambda c: c.start())

    def step(g, carry):
        dst = pl.multiple_of(g * ROW_TILE, ROW_TILE)
        for u in range(SUBLANES):
            t = g * SUBLANES + u
            g1_sc[_token_rows(dst + u), :] = local_sc[_token_rows(a1_ref[t]), :]
            g2_sc[_token_rows(dst + u), :] = local_sc[_token_rows(a2_ref[t]), :]
        return carry

    lax.fori_loop(0, TOK_TILE // SUBLANES, step, 0, unroll=2)
    halves = lambda g_sc: [pltpu.bitcast(p, BF16) for p in _from_row_tiled(g_sc[...], TOK_TILE)]
    lo1, hi1 = _unpack_rows(z32_sc, halves(g1_sc), TOK_TILE)
    lo2, hi2 = _unpack_rows(z32_sc, halves(g2_sc), TOK_TILE)
    w1, w2 = posw_ref[:, 2:3], posw_ref[:, 3:4]
    moe = jnp.concatenate([w1 * lo1 + w2 * lo2, w1 * hi1 + w2 * hi2], axis=1)
    gate2 = _rows_to_tokens(mod_ref[:, 5 * D_MODEL:6 * D_MODEL], D_MODEL)
    y = x1_ref[...] + gate2 * moe

    @pl.when(s < n_psb)
    def _():
        yp_ref[...] = y

    @pl.when(s >= n_psb)
    def _():
        yo_ref[...] = y


def _moe_combine(ys, a1, a2, posw, plan, x1, mod, sb, n_ptiles, n_stiles, prep):
    tps = sb // TOK_TILE
    n_blocks = (n_ptiles + n_stiles) // tps
    n_psb = n_ptiles // tps
    pblocks = prep // ROWS_PER_TILE
    tile = lambda s, j: s * tps + j
    vec = _smem_vec(TOK_TILE, lambda s, j, *_: (tile(s, j),))
    return pl.pallas_call(
        functools.partial(_moe_combine_kernel, n_psb, n_blocks),
        grid_spec=pltpu.PrefetchScalarGridSpec(
            num_scalar_prefetch=3,
            grid=(n_blocks, tps),
            in_specs=[pl.BlockSpec(memory_space=pl.ANY), vec, vec,
                      pl.BlockSpec((TOK_TILE, LANES), lambda s, j, *_: (tile(s, j), 0)),
                      pl.BlockSpec((TOK_TILE, D_MODEL), lambda s, j, *_: (tile(s, j), 0)),
                      pl.BlockSpec((ROWS_PER_TILE, 6 * D_MODEL),
                                   lambda s, j, *_: (jnp.maximum(tile(s, j) - n_ptiles + pblocks, 0), 0))],
            out_specs=[pl.BlockSpec((TOK_TILE, D_MODEL),
                                    lambda s, j, *_: (jnp.minimum(tile(s, j), n_ptiles - 1), 0)),
                       pl.BlockSpec((TOK_TILE, D_MODEL),
                                    lambda s, j, *_: (jnp.maximum(tile(s, j) - n_ptiles, 0), 0))],
            scratch_shapes=[pltpu.VMEM((_local_rows(sb) * ROW_PIECES, LANES), U32),
                            pltpu.VMEM((2, _local_rows(sb) * ROW_PIECES * 2, LANES), BF16),
                            pltpu.VMEM((TOK_TILE * ROW_PIECES, LANES), U32),
                            pltpu.VMEM((TOK_TILE * ROW_PIECES, LANES), U32),
                            pltpu.VMEM((ROW_PIECES, 2 * TOK_TILE, LANES), F32),
                            pltpu.SemaphoreType.DMA((2,))]),
        out_shape=[jax.ShapeDtypeStruct((n_ptiles * TOK_TILE, D_MODEL), F32),
                   jax.ShapeDtypeStruct((n_stiles * TOK_TILE, D_MODEL), F32)],
        compiler_params=_params(2),
        name="moe_combine",
    )(plan["loc"], plan["n"], plan["dst"], ys, a1, a2, posw, x1, mod)


def _layer(xp, xs, cache_k, cache_v, state, c_prompt, c_sample, norm_mix_g, norm_ffn_g, w_ada, b_ada, w_in,
           q_norm_g, k_norm_g, rel_bias, w_gate_up, b_gate, gla_norm_g, w_out, w_route_group,
           b_route_group, w_route_expert, b_route_expert, w_exp_gate, w_exp_up, w_exp_down):
    batch, seq, _ = xp.shape
    n_seq, dec_seq, _ = xs.shape
    assert batch == 1 and dec_seq == CHUNK and cache_k.shape[1] == BAND_PAST
    assert seq % TOK_TILE == 0 and seq >= BAND_PAST and (n_seq * CHUNK) % TOK_TILE == 0
    assert seq % (ATTN_SUB * Q_ROWS) == 0 and seq % (GLA_SUB * GLA_CHUNKS * CHUNK) == 0
    n_ptok, n_stok = seq, n_seq * CHUNK
    n_ptiles, n_stiles = n_ptok // TOK_TILE, n_stok // TOK_TILE
    sb = (MOE_SUPER_BLOCK if (n_ptok % MOE_SUPER_BLOCK == 0 and n_stok % MOE_SUPER_BLOCK == 0)
          else OUT_SUB * TOK_TILE)
    prep = OUT_SUB * ROWS_PER_TILE

    xp2 = xp.reshape(n_ptok, D_MODEL)
    xs2 = xs.reshape(n_stok, D_MODEL)
    c_rows = jnp.concatenate([jnp.broadcast_to(c_prompt, (prep, D_MODEL)), c_sample], axis=0)
    mod = _adaln(c_rows, w_ada, b_ada)

    w_main = w_in[:, 0:IN_MAIN].astype(BF16)
    w_gr = jnp.pad(w_in[:, IN_MAIN:], ((0, 0), (0, LANES - GATE_RANK))).astype(BF16)
    wgu_p = jnp.pad(w_gate_up, ((0, LANES - GATE_RANK), (0, 0))).astype(BF16)
    head = np.arange(A_WIDTH) // A_HEAD_DIM
    bd = jnp.asarray(head[:, None] == head[None, :], BF16)
    gq = jnp.tile(q_norm_g, A_HEADS).reshape(1, A_WIDTH)
    gk = jnp.tile(k_norm_g, A_HEADS).reshape(1, A_WIDTH)
    q, k, v, kf, vf, gla, la = _inproj(
        xp2, xs2, mod, norm_mix_g.reshape(1, D_MODEL), w_main, w_gr, bd, gq, gk, wgu_p,
        b_gate.reshape(1, B_KWIDTH), n_ptiles, n_stiles, prep)

    first_chunk = n_ptok // CHUNK
    oa_p = _attn_prompt(rel_bias[:, _bias_lanes(ATTN_WIN * Q_ROWS)], q, k, v, n_ptok // (ATTN_SUB * Q_ROWS))
    oa_s = _attn_sample(rel_bias[:, _bias_lanes(SAMPLE_KEYS)], q, k, v,
                        cache_k.reshape(n_seq, BAND_PAST, A_WIDTH), cache_v.reshape(n_seq, BAND_PAST, A_WIDTH),
                        first_chunk, n_seq)
    g_gla = gla_norm_g.reshape(1, B_DV)
    ob_p, sfin_p = _gla_prompt(gla, la, g_gla, n_ptok // (GLA_SUB * GLA_CHUNKS * CHUNK))
    ob_s, sfin_s = _gla_sample(gla, la, g_gla, _state_to_pairs(state), first_chunk, n_seq)

    wr = jnp.pad(jnp.concatenate([w_route_group, w_route_expert], axis=1),
                 ((0, 0), (0, LANES - N_GROUPS - N_EXPERTS)))
    br = jnp.pad(jnp.concatenate([b_route_group, b_route_expert]), (0, LANES - N_GROUPS - N_EXPERTS))
    x1, h2p, meta, cnt = _outproj(oa_p, oa_s, ob_p, ob_s, w_out.astype(BF16), xp2, xs2, mod,
                                  norm_ffn_g.reshape(1, D_MODEL), wr, br.reshape(1, LANES),
                                  n_ptiles, n_stiles, prep, sb)

    posw, a1, a2, plan = _moe_plan(meta, cnt, sb)
    xs_sorted = _moe_dispatch(h2p.reshape(-1, LANES), a1, a2, plan, sb, _sorted_tiles(n_ptok + n_stok, sb))
    ys_sorted = _moe_ffn(xs_sorted, plan, w_exp_gate, w_exp_up, w_exp_down)
    yp, ys = _moe_combine(ys_sorted, a1, a2, posw, plan, x1, mod, sb, n_ptiles, n_stiles, prep)

    tail = min(BAND_PAST, seq)
    heads = (A_HEADS, A_HEAD_DIM)
    return (yp.reshape(1, seq, D_MODEL), ys.reshape(n_seq, CHUNK, D_MODEL),
            kf[TOK_TILE - tail:TOK_TILE].reshape((1, tail) + heads),
            vf[TOK_TILE - tail:TOK_TILE].reshape((1, tail) + heads),
            _pairs_to_state(sfin_p)[None],
            kf[TOK_TILE:].reshape((n_seq, CHUNK) + heads),
            vf[TOK_TILE:].reshape((n_seq, CHUNK) + heads),
            _pairs_to_state(sfin_s))


def kernel(x_prompt, x_sample, cache_a_k, cache_a_v, state_gla, c_prompt, c_sample, norm_mix_g, norm_ffn_g,
           w_ada, b_ada, w_in, q_norm_g, k_norm_g, rel_bias, w_gate_up, b_gate, gla_norm_g, w_out,
           w_route_group, b_route_group, w_route_expert, b_route_expert, w_exp_gate, w_exp_up, w_exp_down):
    depth = w_in.shape[0]
    yp, ys = x_prompt, x_sample
    outs = [[] for _ in range(6)]
    for l in range(depth):
        yp, ys, kp, vp, sp, ks, vs, ss = _layer(
            yp, ys, cache_a_k[l], cache_a_v[l], state_gla[l], c_prompt, c_sample, norm_mix_g[l], norm_ffn_g[l],
            w_ada[l], b_ada[l], w_in[l], q_norm_g[l], k_norm_g[l], rel_bias[l], w_gate_up[l], b_gate[l],
            gla_norm_g[l], w_out[l], w_route_group[l], b_route_group[l], w_route_expert[l], b_route_expert[l],
            w_exp_gate[l], w_exp_up[l], w_exp_down[l])
        for lst, val in zip(outs, (kp, vp, sp, ks, vs, ss)):
            lst.append(val)
    return (yp, ys) + tuple(jnp.stack(o) for o in outs)
```

```python
import functools

import numpy as np
import jax
import jax.numpy as jnp
from jax import lax
from jax.experimental import pallas as pl
from jax.experimental.pallas import tpu as pltpu

F32 = jnp.float32
BF16 = jnp.bfloat16
U32 = jnp.uint32

D_MODEL = 1024
CHUNK = 64
LOG_CHUNK = 6
BAND_CHUNKS = 8
BAND_PAST = BAND_CHUNKS * CHUNK
A_WIDTH = 512
A_HEADS = 8
A_HEAD_DIM = 64
MAX_REL = 128
B_WIDTH = 512
B_HEADS = 4
B_DV = 128
B_DK = 64
B_KWIDTH = 256
GATE_RANK = 16
GATE_TAU = 16.0
N_GROUPS = 4
EXPERTS_PER_GROUP = 8
N_EXPERTS = 32
EXPERT_FF = 256
EPS = 1e-6

LANES = 128
IN_MAIN = 3 * A_WIDTH + 2 * B_KWIDTH + 2 * B_WIDTH
TOK_TILE = 512
ROWS_PER_TILE = TOK_TILE // CHUNK
OUT_SUB = 1
Q_CHUNKS = 4
Q_ROWS = Q_CHUNKS * CHUNK
ROLL_W = 1024
NEG = -1e30
ROUTE_OFF = N_GROUPS
VMEM_LIMIT = 56 * 1024 * 1024


def _params(n_axes=1):
    return pltpu.CompilerParams(dimension_semantics=("arbitrary",) * n_axes,
                                vmem_limit_bytes=VMEM_LIMIT)


def _split(a):
    hi = a.astype(BF16)
    lo = (a - hi.astype(F32)).astype(BF16)
    return hi, lo


def _dot(a, b):
    return jnp.dot(a, b, preferred_element_type=F32)


def _dot3(a, b):
    ah, al = _split(a)
    bh, bl = _split(b)
    return _dot(ah, bh) + _dot(al, bh) + _dot(ah, bl)


def _dot_nt(a, b):
    return lax.dot_general(a, b, (((1,), (1,)), ((), ())), preferred_element_type=F32)


def _dot_tn(a, b):
    return lax.dot_general(a, b, (((0,), (0,)), ((), ())), preferred_element_type=F32)


def _silu(x):
    return x / (1.0 + jnp.exp(-x))


def _rows_to_tokens(rows, n):
    r = rows.shape[0]
    return jnp.broadcast_to(rows[:, None, :], (r, CHUNK, n)).reshape(r * CHUNK, n)


def _adaln_kernel(c_ref, w_ref, b_ref, o_ref):
    a = _silu(c_ref[...])
    o_ref[...] = _dot3(a, w_ref[...]) + b_ref[...]


def _adaln(c_rows, w_ada, b_ada):
    r = c_rows.shape[0]
    n = w_ada.shape[1]
    tn = 1024
    return pl.pallas_call(
        _adaln_kernel,
        grid=(n // tn,),
        in_specs=[pl.BlockSpec((r, D_MODEL), lambda j: (0, 0)),
                  pl.BlockSpec((D_MODEL, tn), lambda j: (0, j)),
                  pl.BlockSpec((1, tn), lambda j: (0, j))],
        out_specs=pl.BlockSpec((r, tn), lambda j: (0, j)),
        out_shape=jax.ShapeDtypeStruct((r, n), F32),
        compiler_params=_params(),
        name="adaln",
    )(c_rows, w_ada, b_ada.reshape(1, n))


def _head_rms(z, bd_ref, g):
    ms = _dot((z * z).astype(BF16), bd_ref[...]) * (1.0 / A_HEAD_DIM)
    return z * lax.rsqrt(ms + EPS) * g


def _inproj_kernel(n_ptiles, xp_ref, xs_ref, mod_ref, gmix_ref, w_ref, wgr_ref, bd_ref, gq_ref, gk_ref,
                   wgu_ref, bg_ref,
                   q_ref, k_ref, v_ref, kf_ref, vf_ref, gla_ref, la_ref):
    i = pl.program_id(0)
    x = jnp.where(i < n_ptiles, xp_ref[...], xs_ref[...])
    ms = jnp.mean(x * x, axis=-1, keepdims=True)
    xn = x * lax.rsqrt(ms + EPS) * gmix_ref[...]
    sh = _rows_to_tokens(mod_ref[:, 0:D_MODEL], D_MODEL)
    sc = _rows_to_tokens(mod_ref[:, D_MODEL:2 * D_MODEL], D_MODEL)
    hb = (xn * (1.0 + sc) + sh).astype(BF16)

    zq = _dot(hb, w_ref[:, 0:A_WIDTH])
    q_ref[...] = (_head_rms(zq, bd_ref, gq_ref[...]) * (LOG2E * A_HEAD_DIM ** -0.5)).astype(BF16)
    zk = _dot(hb, w_ref[:, A_WIDTH:2 * A_WIDTH])
    kn = _head_rms(zk, bd_ref, gk_ref[...])
    k_ref[...] = kn.astype(BF16)
    kf_ref[...] = kn
    zv = _dot(hb, w_ref[:, 2 * A_WIDTH:3 * A_WIDTH])
    v_ref[...] = zv.astype(BF16)
    vf_ref[...] = zv

    o = 3 * A_WIDTH
    zqb = _dot(hb, w_ref[:, o:o + B_KWIDTH]) * (B_DK ** -0.5)
    gla_ref[:, 0:B_KWIDTH] = zqb.astype(BF16)
    for c in range(B_KWIDTH, 2 * B_KWIDTH + 2 * B_WIDTH, 256):
        gla_ref[:, c:c + 256] = _dot(hb, w_ref[:, o + c:o + c + 256]).astype(BF16)

    gr = _dot(hb, wgr_ref[...])
    logit = _dot(gr.astype(BF16), wgu_ref[...]) + bg_ref[...]
    log_sig = jnp.minimum(logit, 0.0) - jnp.log1p(jnp.exp(-jnp.abs(logit)))
    la_ref[...] = log_sig * (1.0 / GATE_TAU)


def _inproj(xp, xs, mod, gmix, w_main, w_gr, bd, gq, gk, wgu_p, bg, n_ptiles, n_stiles, prep):
    n_tiles = n_ptiles + n_stiles
    t = n_tiles * TOK_TILE
    tail_tiles = 1 + n_stiles
    pblocks = prep // ROWS_PER_TILE
    const = lambda i: (0, 0)
    row = lambda i: (i, 0)
    tail = lambda i: (jnp.maximum(i - (n_ptiles - 1), 0), 0)
    return pl.pallas_call(
        functools.partial(_inproj_kernel, n_ptiles),
        grid=(n_tiles,),
        in_specs=[pl.BlockSpec((TOK_TILE, D_MODEL), lambda i: (jnp.minimum(i, n_ptiles - 1), 0)),
                  pl.BlockSpec((TOK_TILE, D_MODEL), lambda i: (jnp.maximum(i - n_ptiles, 0), 0)),
                  pl.BlockSpec((ROWS_PER_TILE, 6 * D_MODEL),
                               lambda i: (jnp.maximum(i - n_ptiles + pblocks, 0), 0)),
                  pl.BlockSpec((1, D_MODEL), const),
                  pl.BlockSpec((D_MODEL, IN_MAIN), const),
                  pl.BlockSpec((D_MODEL, LANES), const),
                  pl.BlockSpec((A_WIDTH, A_WIDTH), const),
                  pl.BlockSpec((1, A_WIDTH), const),
                  pl.BlockSpec((1, A_WIDTH), const),
                  pl.BlockSpec((LANES, B_KWIDTH), const),
                  pl.BlockSpec((1, B_KWIDTH), const)],
        out_specs=[pl.BlockSpec((TOK_TILE, A_WIDTH), row),
                   pl.BlockSpec((TOK_TILE, A_WIDTH), row),
                   pl.BlockSpec((TOK_TILE, A_WIDTH), row),
                   pl.BlockSpec((TOK_TILE, A_WIDTH), tail),
                   pl.BlockSpec((TOK_TILE, A_WIDTH), tail),
                   pl.BlockSpec((TOK_TILE, 2 * B_KWIDTH + 2 * B_WIDTH), row),
                   pl.BlockSpec((TOK_TILE, B_KWIDTH), row)],
        out_shape=[jax.ShapeDtypeStruct((t, A_WIDTH), BF16),
                   jax.ShapeDtypeStruct((t, A_WIDTH), BF16),
                   jax.ShapeDtypeStruct((t, A_WIDTH), BF16),
                   jax.ShapeDtypeStruct((tail_tiles * TOK_TILE, A_WIDTH), F32),
                   jax.ShapeDtypeStruct((tail_tiles * TOK_TILE, A_WIDTH), F32),
                   jax.ShapeDtypeStruct((t, 2 * B_KWIDTH + 2 * B_WIDTH), BF16),
                   jax.ShapeDtypeStruct((t, B_KWIDTH), F32)],
        compiler_params=_params(),
        name="inproj",
    )(xp, xs, mod, gmix, w_main, w_gr, bd, gq, gk, wgu_p, bg)


def _bias_lanes(n_keys):
    l = np.arange(ROLL_W)
    d = np.where(l < n_keys, BAND_PAST - l, BAND_PAST - l + ROLL_W)
    return np.clip(d, -(CHUNK - 1), MAX_REL) + (CHUNK - 1)


LOG2E = 1.4426950408889634


def _band_mask(m_rows, n_keys, first_col):
    qi = lax.broadcasted_iota(jnp.int32, (m_rows, n_keys), 0) >> LOG_CHUNK
    kw = lax.broadcasted_iota(jnp.int32, (m_rows, n_keys), 1)
    kc = kw >> LOG_CHUNK
    return (kc >= qi) & (kc <= qi + BAND_CHUNKS) & (kw >= first_col)


def _bias_tile(u_ref, h, ok):
    m_rows, n_keys = ok.shape
    src = jnp.broadcast_to(u_ref[h:h + 1, :] * LOG2E, (m_rows, ROLL_W))
    toe = pltpu.roll(src, 0, 1, stride=1, stride_axis=0)
    return jnp.where(ok, toe[:, 0:n_keys], NEG)


def _attend(q, kcat, vcat, bias_sc):
    m_rows = q.shape[0]
    first = lax.broadcasted_iota(jnp.int32, (m_rows, LANES), 1) < A_HEAD_DIM
    outs = []
    for p in range(A_HEADS // 2):
        lanes = slice(p * LANES, (p + 1) * LANES)
        qp, kp, vp = q[:, lanes], kcat[:, lanes], vcat[:, lanes]
        zero = jnp.zeros_like(qp)
        q2 = jnp.concatenate([jnp.where(first, qp, zero), jnp.where(first, zero, qp)], axis=0)
        s = _dot_nt(q2, kp) + bias_sc[p]
        e = jnp.exp2(s - jnp.max(s, axis=-1, keepdims=True))
        l = jnp.sum(e, axis=-1, keepdims=True)
        o2 = _dot(e.astype(BF16), vp) / l
        outs.append(jnp.where(first, o2[0:m_rows], o2[m_rows:2 * m_rows]))
    return jnp.concatenate(outs, axis=-1)


ATTN_SUB = 4
ATTN_WIN = 3


def _attn_prompt_kernel(u_ref, q_ref, *refs):
    k_refs = refs[0:ATTN_SUB + ATTN_WIN - 1]
    v_refs = refs[ATTN_SUB + ATTN_WIN - 1:2 * (ATTN_SUB + ATTN_WIN - 1)]
    o_ref, bias_sc = refs[-2:]
    j = pl.program_id(0)
    n_keys = ATTN_WIN * Q_ROWS

    @pl.when(j == 0)
    def _():
        for g in range(ATTN_WIN):
            ok = _band_mask(Q_ROWS, n_keys, (ATTN_WIN - 1 - g) * Q_ROWS)
            for h in range(A_HEADS):
                bias_sc[g, h // 2, (h % 2) * Q_ROWS:(h % 2 + 1) * Q_ROWS, :] = _bias_tile(u_ref, h, ok)

    ks = [r[...] for r in k_refs]
    vs = [r[...] for r in v_refs]
    for sub in range(ATTN_SUB):
        rows = slice(sub * Q_ROWS, (sub + 1) * Q_ROWS)
        kcat = jnp.concatenate(ks[sub:sub + ATTN_WIN], axis=0)
        vcat = jnp.concatenate(vs[sub:sub + ATTN_WIN], axis=0)
        bias = bias_sc.at[jnp.minimum(ATTN_SUB * j + sub, ATTN_WIN - 1)]
        o_ref[rows, :] = _attend(q_ref[rows, :], kcat, vcat, bias).astype(BF16)


def _attn_prompt(u, q, k, v, n_steps):
    const = lambda j: (0, 0)
    n_blk = ATTN_SUB + ATTN_WIN - 1
    blk = lambda d: pl.BlockSpec((Q_ROWS, A_WIDTH),
                                 lambda j, d=d: (jnp.maximum(ATTN_SUB * j - (ATTN_WIN - 1) + d, 0), 0))
    step_rows = ATTN_SUB * Q_ROWS
    return pl.pallas_call(
        _attn_prompt_kernel,
        grid=(n_steps,),
        in_specs=[pl.BlockSpec((A_HEADS, ROLL_W), const), pl.BlockSpec((step_rows, A_WIDTH), lambda j: (j, 0))]
                 + [blk(d) for d in range(n_blk)] * 2,
        out_specs=pl.BlockSpec((step_rows, A_WIDTH), lambda j: (j, 0)),
        out_shape=jax.ShapeDtypeStruct((n_steps * step_rows, A_WIDTH), BF16),
        scratch_shapes=[pltpu.VMEM((ATTN_WIN, A_HEADS // 2, 2 * Q_ROWS, ATTN_WIN * Q_ROWS), F32)],
        compiler_params=_params(),
        name="attn_prompt",
    )(u, q, *([k] * n_blk), *([v] * n_blk))


SAMPLE_KEYS = BAND_PAST + 2 * CHUNK
SAMPLE_STREAMS = 4


def _attn_sample_kernel(u_ref, q_ref, kn_ref, vn_ref, kc_ref, vc_ref, o_ref, bias_sc):
    @pl.when(pl.program_id(0) == 0)
    def _():
        ok = _band_mask(CHUNK, SAMPLE_KEYS, 0)
        for p in range(A_HEADS // 2):
            pair = jnp.concatenate([_bias_tile(u_ref, 2 * p, ok), _bias_tile(u_ref, 2 * p + 1, ok)], axis=0)
            bias_sc[p] = pair.T

    pad = jnp.zeros((CHUNK, A_WIDTH), BF16)
    lane = lax.broadcasted_iota(jnp.int32, (CHUNK, LANES), 1)
    first = lane < A_HEAD_DIM
    zero = jnp.zeros((CHUNK, LANES), BF16)
    for n in range(SAMPLE_STREAMS):
        rows = slice(n * CHUNK, (n + 1) * CHUNK)
        kcat = jnp.concatenate([kc_ref[n].astype(BF16), kn_ref[rows, :], pad], axis=0)
        vcat = jnp.concatenate([vc_ref[n].astype(BF16), vn_ref[rows, :], pad], axis=0)
        q = q_ref[rows, :]
        outs = []
        for p in range(A_HEADS // 2):
            lanes = slice(p * LANES, (p + 1) * LANES)
            qp = q[:, lanes]
            q_rows = jnp.concatenate([jnp.where(first, qp, zero), jnp.where(first, zero, qp)], axis=0)
            s = _dot_nt(kcat[:, lanes], q_rows) + bias_sc[p]
            e = jnp.exp2(s - jnp.max(s, axis=0, keepdims=True))
            pn = (e * (1.0 / jnp.sum(e, axis=0, keepdims=True))).astype(BF16)
            r = _dot_tn(pn, vcat[:, lanes])
            outs.append(jnp.where(first, r[0:CHUNK], r[CHUNK:2 * CHUNK]))
        o_ref[rows, :] = jnp.concatenate(outs, axis=-1).astype(BF16)


def _attn_sample(u, q, k, v, kc, vc, first_chunk, n_seq):
    assert n_seq % SAMPLE_STREAMS == 0 and first_chunk % SAMPLE_STREAMS == 0
    rows = SAMPLE_STREAMS * CHUNK
    new = pl.BlockSpec((rows, A_WIDTH), lambda b: (first_chunk // SAMPLE_STREAMS + b, 0))
    cache = pl.BlockSpec((SAMPLE_STREAMS, BAND_PAST, A_WIDTH), lambda b: (b, 0, 0))
    return pl.pallas_call(
        _attn_sample_kernel,
        grid=(n_seq // SAMPLE_STREAMS,),
        in_specs=[pl.BlockSpec((A_HEADS, ROLL_W), lambda b: (0, 0)), new, new, new, cache, cache],
        out_specs=pl.BlockSpec((rows, A_WIDTH), lambda b: (b, 0)),
        out_shape=jax.ShapeDtypeStruct((n_seq * CHUNK, A_WIDTH), BF16),
        scratch_shapes=[pltpu.VMEM((A_HEADS // 2, SAMPLE_KEYS, LANES), F32)],
        compiler_params=_params(),
        name="attn_sample",
    )(u, q, k, v, kc, vc)


GLA_CHUNKS = 4
GLA_SUB = 4


def _gla_block(n_chunks, gla_ref, la_ref, ltri_ref, g_ref, st_sc, o_ref):
    rows = n_chunks * CHUNK
    la = la_ref[...]
    la_hi, la_lo = _split(la)
    b = _dot(ltri_ref[...], la_hi) + _dot(ltri_ref[...], la_lo)
    b3 = b.reshape(n_chunks, CHUNK, B_KWIDTH)
    b_mid = b3[:, CHUNK // 2 - 1:CHUNK // 2, :]
    b_last = b3[:, CHUNK - 1:CHUNK, :]
    q = gla_ref[:, 0:B_KWIDTH].astype(F32).reshape(n_chunks, CHUNK, B_KWIDTH)
    k = gla_ref[:, B_KWIDTH:2 * B_KWIDTH].astype(F32).reshape(n_chunks, CHUNK, B_KWIDTH)
    q_start = (q * jnp.exp(b3)).reshape(rows, B_KWIDTH).astype(BF16)
    q_mid = (q * jnp.exp(b3 - b_mid)).reshape(rows, B_KWIDTH).astype(BF16)
    k_mid = (k * jnp.exp(b_mid - b3)).reshape(rows, B_KWIDTH).astype(BF16)
    k_end = (k * jnp.exp(b_last - b3)).reshape(rows, B_KWIDTH).astype(BF16)
    dec = jnp.exp(b_last)

    ti = lax.broadcasted_iota(jnp.int32, (2 * rows, rows), 0) & (rows - 1)
    si = lax.broadcasted_iota(jnp.int32, (2 * rows, rows), 1)
    causal = (si <= ti) & ((si >> LOG_CHUNK) == (ti >> LOG_CHUNK))
    first_r = lax.broadcasted_iota(jnp.int32, (rows, LANES), 1) < B_DK
    first_c = lax.broadcasted_iota(jnp.int32, (CHUNK, LANES), 1) < B_DK
    first_s = lax.broadcasted_iota(jnp.int32, (B_DV, LANES), 1) < B_DK

    def stack_heads(x, first):
        zero = jnp.zeros_like(x)
        return jnp.concatenate([jnp.where(first, x, zero), jnp.where(first, zero, x)], axis=0)

    for p in range(B_HEADS // 2):
        lanes = slice(p * LANES, (p + 1) * LANES)
        qs_p, qm_p, km_p, ke_p = q_start[:, lanes], q_mid[:, lanes], k_mid[:, lanes], k_end[:, lanes]
        v_pair = gla_ref[:, 2 * B_KWIDTH + 2 * p * B_DV:2 * B_KWIDTH + (2 * p + 2) * B_DV]
        sc = jnp.where(causal, _dot_nt(stack_heads(qm_p, first_r), km_p), 0.0)
        o2 = _dot(sc.astype(BF16), v_pair)
        intra = [o2[0:rows, 0:B_DV], o2[rows:2 * rows, B_DV:2 * B_DV]]
        inter = [[], []]
        st = st_sc[p]
        for c in range(n_chunks):
            cr = slice(c * CHUNK, (c + 1) * CHUNK)
            r2 = _dot_nt(stack_heads(qs_p[cr], first_c), st.astype(BF16))
            inter[0].append(r2[0:CHUNK])
            inter[1].append(r2[CHUNK:2 * CHUNK])
            u2 = _dot_tn(v_pair[cr], ke_p[cr])
            st = st * dec[c, :, lanes] + jnp.where(first_s, u2[0:B_DV], u2[B_DV:2 * B_DV])
        st_sc[p] = st
        for hh in range(2):
            h = 2 * p + hh
            o = intra[hh] + jnp.concatenate(inter[hh], axis=0)
            ms = jnp.mean(o * o, axis=-1, keepdims=True)
            on = o * lax.rsqrt(ms + EPS) * g_ref[...]
            r = gla_ref[:, 2 * B_KWIDTH + B_WIDTH + h * B_DV:2 * B_KWIDTH + B_WIDTH + (h + 1) * B_DV]
            o_ref[:, h * B_DV:(h + 1) * B_DV] = (on * _silu(r.astype(F32))).astype(BF16)


def _gla_prompt_kernel(gla_ref, la_ref, ltri_ref, g_ref, o_ref, sfin_ref, st_sc):
    @pl.when(pl.program_id(0) == 0)
    def _():
        st_sc[...] = jnp.zeros_like(st_sc)

    rows = GLA_CHUNKS * CHUNK
    for sub in range(GLA_SUB):
        part = pl.ds(sub * rows, rows)
        _gla_block(GLA_CHUNKS, gla_ref.at[part], la_ref.at[part], ltri_ref, g_ref, st_sc, o_ref.at[part])
    sfin_ref[...] = st_sc[...]


def _gla_sample_kernel(gla_ref, la_ref, ltri_ref, g_ref, s0_ref, o_ref, sfin_ref, st_sc):
    st_sc[...] = s0_ref[...]
    for n in range(SAMPLE_STREAMS):
        part = pl.ds(n * CHUNK, CHUNK)
        _gla_block(1, gla_ref.at[part], la_ref.at[part], ltri_ref, g_ref, st_sc.at[n], o_ref.at[part])
    sfin_ref[...] = st_sc[...]


def _ltri(n_chunks):
    r = np.arange(n_chunks * CHUNK)
    m = (r[None, :] <= r[:, None]) & (r[None, :] // CHUNK == r[:, None] // CHUNK)
    return jnp.asarray(m, BF16)


_GLA_W = 2 * B_KWIDTH + 2 * B_WIDTH
_ST_SHAPE = (B_HEADS // 2, B_DV, LANES)


def _gla_prompt(gla, la, g, n_steps):
    rows = GLA_SUB * GLA_CHUNKS * CHUNK
    const = lambda j: (0, 0)
    return pl.pallas_call(
        _gla_prompt_kernel,
        grid=(n_steps,),
        in_specs=[pl.BlockSpec((rows, _GLA_W), lambda j: (j, 0)),
                  pl.BlockSpec((rows, B_KWIDTH), lambda j: (j, 0)),
                  pl.BlockSpec((GLA_CHUNKS * CHUNK, GLA_CHUNKS * CHUNK), const),
                  pl.BlockSpec((1, B_DV), const)],
        out_specs=[pl.BlockSpec((rows, B_WIDTH), lambda j: (j, 0)),
                   pl.BlockSpec(_ST_SHAPE, lambda j: (0, 0, 0))],
        out_shape=[jax.ShapeDtypeStruct((n_steps * rows, B_WIDTH), BF16),
                   jax.ShapeDtypeStruct(_ST_SHAPE, F32)],
        scratch_shapes=[pltpu.VMEM(_ST_SHAPE, F32)],
        compiler_params=_params(),
        name="gla_prompt",
    )(gla, la, _ltri(GLA_CHUNKS), g)


def _gla_sample(gla, la, g, s0, first_chunk, n_seq):
    assert n_seq % SAMPLE_STREAMS == 0 and first_chunk % SAMPLE_STREAMS == 0
    const = lambda b: (0, 0)
    rows = SAMPLE_STREAMS * CHUNK
    first = first_chunk // SAMPLE_STREAMS
    st_spec = pl.BlockSpec((SAMPLE_STREAMS,) + _ST_SHAPE, lambda b: (b, 0, 0, 0))
    return pl.pallas_call(
        _gla_sample_kernel,
        grid=(n_seq // SAMPLE_STREAMS,),
        in_specs=[pl.BlockSpec((rows, _GLA_W), lambda b: (first + b, 0)),
                  pl.BlockSpec((rows, B_KWIDTH), lambda b: (first + b, 0)),
                  pl.BlockSpec((CHUNK, CHUNK), const),
                  pl.BlockSpec((1, B_DV), const),
                  st_spec],
        out_specs=[pl.BlockSpec((rows, B_WIDTH), lambda b: (b, 0)), st_spec],
        out_shape=[jax.ShapeDtypeStruct((n_seq * CHUNK, B_WIDTH), BF16),
                   jax.ShapeDtypeStruct((n_seq,) + _ST_SHAPE, F32)],
        scratch_shapes=[pltpu.VMEM((SAMPLE_STREAMS,) + _ST_SHAPE, F32)],
        compiler_params=_params(),
        name="gla_sample",
    )(gla, la, _ltri(1), g, s0)


def _state_to_pairs(s):
    lead = s.shape[:-3]
    s = s.reshape(lead + (B_HEADS // 2, 2, B_DK, B_DV))
    s = jnp.moveaxis(s, -1, -3)
    return s.reshape(lead + (B_HEADS // 2, B_DV, 2 * B_DK))


def _pairs_to_state(s):
    lead = s.shape[:-3]
    s = s.reshape(lead + (B_HEADS // 2, B_DV, 2, B_DK))
    s = jnp.moveaxis(s, -3, -1)
    return s.reshape(lead + (B_HEADS, B_DK, B_DV))


def _route(logits):
    lane = lax.broadcasted_iota(jnp.int32, logits.shape, 1)
    lane_f = lane.astype(F32)
    big = float(LANES)
    gmask = lane < N_GROUPS
    gl = jnp.where(gmask, logits, NEG)
    gmax = jnp.max(gl, axis=-1, keepdims=True)
    gsel = jnp.min(jnp.where(gl == gmax, lane_f, big), axis=-1, keepdims=True)
    gsum = jnp.sum(jnp.where(gmask, jnp.exp(gl - gmax), 0.0), axis=-1, keepdims=True)
    g_w = 1.0 / gsum
    e_lo = ROUTE_OFF + gsel * EXPERTS_PER_GROUP
    emask = (lane_f >= e_lo) & (lane_f < e_lo + EXPERTS_PER_GROUP)
    el = jnp.where(emask, logits, NEG)
    v1 = jnp.max(el, axis=-1, keepdims=True)
    i1 = jnp.min(jnp.where(el == v1, lane_f, big), axis=-1, keepdims=True)
    el2 = jnp.where(lane_f == i1, NEG, el)
    v2 = jnp.max(el2, axis=-1, keepdims=True)
    i2 = jnp.min(jnp.where(el2 == v2, lane_f, big), axis=-1, keepdims=True)
    t = jnp.exp(v2 - v1)
    w1 = g_w / (1.0 + t)
    w2 = g_w * t / (1.0 + t)
    return lane_f, i1, i2, w1, w2


ROW_PIECES = D_MODEL // 2 // LANES
SUBLANES = 8
ROW_TILE = ROW_PIECES * SUBLANES


def _pack_rows(z32_sc, x, rows):
    half = D_MODEL // 2
    out = []
    for c in range(ROW_PIECES):
        z32_sc[c, pl.ds(0, rows, stride=2), :] = x[:, c * LANES:(c + 1) * LANES]
        z32_sc[c, pl.ds(1, rows, stride=2), :] = x[:, half + c * LANES:half + (c + 1) * LANES]
        out.append(z32_sc[c].astype(BF16))
    return out


def _unpack_rows(z32_sc, pieces, rows):
    lo, hi = [], []
    for c in range(ROW_PIECES):
        z32_sc[c] = pieces[c].astype(F32)
        lo.append(z32_sc[c, pl.ds(0, rows, stride=2), :])
        hi.append(z32_sc[c, pl.ds(1, rows, stride=2), :])
    return jnp.concatenate(lo, axis=1), jnp.concatenate(hi, axis=1)


def _to_row_tiled(pieces, tokens):
    per_tile = pieces[0].shape[0] * SUBLANES // tokens
    return jnp.stack([p.reshape(tokens // SUBLANES, per_tile, LANES) for p in pieces], axis=1)


def _from_row_tiled(flat, tokens):
    per_tile = flat.shape[0] // (tokens // SUBLANES) // ROW_PIECES
    tiled = flat.reshape(tokens // SUBLANES, ROW_PIECES, per_tile, LANES)
    return [tiled[:, c].reshape(tokens // SUBLANES * per_tile, LANES) for c in range(ROW_PIECES)]


def _flatten_tiled(tiled):
    return tiled.reshape(-1, LANES)


def _outproj_kernel(n_ptiles, tiles_per_sb, oap_ref, oas_ref, obp_ref, obs_ref, wo_ref, xp_ref, xs_ref, mod_ref,
                    gffn_ref, wr_ref, br_ref, ltri_ref, x1_ref, h2p_ref, meta_ref, cnt_ref, z32_sc, cnt_sc):
    j = pl.program_id(0)
    is_prompt = j < n_ptiles // OUT_SUB
    for s in range(OUT_SUB):
        rows = slice(s * TOK_TILE, (s + 1) * TOK_TILE)
        mrows = slice(s * ROWS_PER_TILE, (s + 1) * ROWS_PER_TILE)
        x = jnp.where(is_prompt, xp_ref[rows, :], xs_ref[rows, :])
        oa = jnp.where(is_prompt, oap_ref[rows, :], oas_ref[rows, :])
        ob = jnp.where(is_prompt, obp_ref[rows, :], obs_ref[rows, :])
        mix = _dot(oa, wo_ref[0:A_WIDTH, :]) + _dot(ob, wo_ref[A_WIDTH:D_MODEL, :])
        gate1 = _rows_to_tokens(mod_ref[mrows, 2 * D_MODEL:3 * D_MODEL], D_MODEL)
        x1 = x + gate1 * mix
        x1_ref[rows, :] = x1
        ms = jnp.mean(x1 * x1, axis=-1, keepdims=True)
        xn = x1 * lax.rsqrt(ms + EPS) * gffn_ref[...]
        sh = _rows_to_tokens(mod_ref[mrows, 3 * D_MODEL:4 * D_MODEL], D_MODEL)
        sc = _rows_to_tokens(mod_ref[mrows, 4 * D_MODEL:5 * D_MODEL], D_MODEL)
        h2 = xn * (1.0 + sc) + sh
        words = [pltpu.bitcast(p, U32) for p in _pack_rows(z32_sc.at[s], h2, TOK_TILE)]
        tiles = slice(s * TOK_TILE // SUBLANES, (s + 1) * TOK_TILE // SUBLANES)
        h2p_ref[tiles] = _to_row_tiled(words, TOK_TILE)

        lane_f, i1, i2, w1, w2 = _route(_dot3(h2, wr_ref[...]) + br_ref[...])

        @pl.when(lax.rem(OUT_SUB * j + s, tiles_per_sb) == 0)
        def _():
            cnt_sc[...] = jnp.zeros_like(cnt_sc)

        sel = jnp.where((lane_f == i1) | (lane_f == i2), 1.0, 0.0).astype(BF16)
        before = _dot(ltri_ref[...], sel) + cnt_sc[0:1, :]
        rank1 = jnp.sum(jnp.where(lane_f == i1, before, 0.0), axis=-1, keepdims=True)
        rank2 = jnp.sum(jnp.where(lane_f == i2, before, 0.0), axis=-1, keepdims=True)
        cnt = cnt_sc[...] + _dot(jnp.ones((8, TOK_TILE), BF16), sel)
        cnt_sc[...] = cnt
        cnt_ref[0] = cnt
        cols = (i1, i2, rank1, rank2, w1, w2)
        meta = jnp.zeros_like(lane_f)
        for c, col in enumerate(cols):
            meta = jnp.where(lane_f == float(c), col, meta)
        meta_ref[rows, :] = meta


def _outproj(oa_p, oa_s, ob_p, ob_s, w_out, xp, xs, mod, gffn, wr, br, n_ptiles, n_stiles, prep, sb):
    n_tiles = n_ptiles + n_stiles
    t = n_tiles * TOK_TILE
    tiles_per_sb = sb // TOK_TILE
    assert n_ptiles % OUT_SUB == 0 and n_stiles % OUT_SUB == 0 and tiles_per_sb % OUT_SUB == 0
    step_rows = OUT_SUB * TOK_TILE
    mod_rows = OUT_SUB * ROWS_PER_TILE
    n_psteps = n_ptiles // OUT_SUB
    pblocks = prep // mod_rows
    const = lambda j: (0, 0)
    row = lambda j: (j, 0)
    prow = lambda j: (jnp.minimum(j, n_psteps - 1), 0)
    srow = lambda j: (jnp.maximum(j - n_psteps, 0), 0)
    r = np.arange(TOK_TILE)
    ltri = jnp.asarray(r[None, :] < r[:, None], BF16)
    return pl.pallas_call(
        functools.partial(_outproj_kernel, n_ptiles, tiles_per_sb),
        grid=(n_tiles // OUT_SUB,),
        in_specs=[pl.BlockSpec((step_rows, A_WIDTH), prow),
                  pl.BlockSpec((step_rows, A_WIDTH), srow),
                  pl.BlockSpec((step_rows, B_WIDTH), prow),
                  pl.BlockSpec((step_rows, B_WIDTH), srow),
                  pl.BlockSpec((D_MODEL, D_MODEL), const),
                  pl.BlockSpec((step_rows, D_MODEL), prow),
                  pl.BlockSpec((step_rows, D_MODEL), srow),
                  pl.BlockSpec((mod_rows, 6 * D_MODEL),
                               lambda j: (jnp.maximum(j - n_psteps + pblocks, 0), 0)),
                  pl.BlockSpec((1, D_MODEL), const),
                  pl.BlockSpec((D_MODEL, LANES), const),
                  pl.BlockSpec((1, LANES), const),
                  pl.BlockSpec((TOK_TILE, TOK_TILE), const)],
        out_specs=[pl.BlockSpec((step_rows, D_MODEL), row),
                   pl.BlockSpec((step_rows // SUBLANES, ROW_PIECES, SUBLANES, LANES), lambda j: (j, 0, 0, 0)),
                   pl.BlockSpec((step_rows, LANES), row),
                   pl.BlockSpec((1, 8, LANES), lambda j: (OUT_SUB * j // tiles_per_sb, 0, 0))],
        out_shape=[jax.ShapeDtypeStruct((t, D_MODEL), F32),
                   jax.ShapeDtypeStruct((t // SUBLANES, ROW_PIECES, SUBLANES, LANES), U32),
                   jax.ShapeDtypeStruct((t, LANES), F32),
                   jax.ShapeDtypeStruct((t // sb, 8, LANES), F32)],
        scratch_shapes=[pltpu.VMEM((OUT_SUB, ROW_PIECES, 2 * TOK_TILE, LANES), F32),
                        pltpu.VMEM((8, LANES), F32)],
        compiler_params=_params(),
        name="outproj",
    )(oa_p, oa_s, ob_p, ob_s, w_out, xp, xs, mod, gffn, wr, br, ltri)


MOE_SUPER_BLOCK = 2048
SEG_ALIGN = SUBLANES
CHUNK_BF16_ROWS = 2 * SEG_ALIGN * ROW_PIECES
SEG_BITS = 9
PAD_BITS = 5
FFN_ROWS = 512
PLAN_ROWS = LANES


def _local_rows(sb):
    return 2 * sb + N_EXPERTS * SEG_ALIGN


def _sorted_tiles(n_tokens, sb):
    rows = 2 * n_tokens + (n_tokens // sb) * N_EXPERTS * SEG_ALIGN + N_EXPERTS * FFN_ROWS
    return -(-rows // (FFN_SUB * FFN_ROWS)) * FFN_SUB


def _moe_plan_kernel(total_chunks, meta_ref, cnt_ref, ustrict_ref, lstrict_ref,
                     posw_ref, addr_ref, tab_ref, tile_ref):
    b = pl.program_id(0)
    per_tile = FFN_ROWS // SEG_ALIGN

    @pl.when(b == 0)
    def _():
        cnt = cnt_ref[...]
        chunks = jnp.floor((cnt + (SEG_ALIGN - 1)) * (1.0 / SEG_ALIGN))
        chunks_b = chunks.astype(BF16)
        loc = _dot(chunks_b, ustrict_ref[...])
        before = _dot(lstrict_ref[...], chunks_b)
        tot = _dot(jnp.ones((PLAN_ROWS, PLAN_ROWS), BF16), chunks_b)
        tiles = jnp.floor((tot + (per_tile - 1)) * (1.0 / per_tile))
        tile_off = _dot(tiles.astype(BF16), ustrict_ref[...])
        n_tiles = jnp.sum(tiles[0:1], axis=-1, keepdims=True)
        lane1 = lax.broadcasted_iota(jnp.int32, (PLAN_ROWS, LANES), 1)
        tail = lane1 == ROUTE_OFF + N_EXPERTS
        pad_off = jnp.where(tail, n_tiles * per_tile, tile_off * per_tile + tot)
        pad_n = jnp.where(tail, total_chunks - n_tiles * per_tile, tiles * per_tile - tot)
        row = lax.broadcasted_iota(jnp.int32, (PLAN_ROWS, LANES), 0)
        tab_ref[0] = loc
        tab_ref[1] = chunks
        tab_ref[2] = tile_off * per_tile + before
        tab_ref[3] = jnp.where(row == 0, pad_off, jnp.where(row == 1, pad_n, jnp.where(row == 2, n_tiles, 0.0)))
        t_idx = lax.broadcasted_iota(jnp.int32, tile_ref.shape, 0).astype(F32)
        lane_t = lax.broadcasted_iota(jnp.int32, tile_ref.shape, 1)
        is_expert = (lane_t >= ROUTE_OFF) & (lane_t < ROUTE_OFF + N_EXPERTS)
        ends = (tile_off + tiles)[0:1, :]
        owner = jnp.sum(jnp.where(is_expert & (ends <= t_idx), 1.0, 0.0), axis=-1, keepdims=True)
        tile_ref[...] = jnp.broadcast_to(jnp.minimum(owner, N_EXPERTS - 1.0), tile_ref.shape)

    own = jnp.floor((cnt_ref[pl.ds(b, 1), :] + (SEG_ALIGN - 1)) * (1.0 / SEG_ALIGN))
    own_off = _dot(jnp.broadcast_to(own, (SUBLANES, LANES)).astype(BF16), ustrict_ref[...]) * SEG_ALIGN
    meta = meta_ref[...]
    lane_f = lax.broadcasted_iota(jnp.int32, meta.shape, 1).astype(F32)
    off_row = own_off[0:1, :]
    pos = []
    for k in range(2):
        e_lane = meta[:, k:k + 1]
        base = jnp.sum(jnp.where(lane_f == e_lane, off_row, 0.0), axis=-1, keepdims=True)
        p = base + meta[:, 2 + k:3 + k]
        tile = jnp.floor(p * (1.0 / SUBLANES))
        pos.append(tile * (ROW_TILE - SUBLANES) + p)
    out = jnp.zeros_like(meta)
    for c, col in enumerate((pos[0], pos[1], meta[:, 4:5], meta[:, 5:6])):
        out = jnp.where(lane_f == float(c), col, out)
    posw_ref[...] = out
    addr_ref[0] = out.T[0:SUBLANES]


def _moe_plan(meta, cnt, sb):
    n_blocks = meta.shape[0] // sb
    assert n_blocks <= PLAN_ROWS and sb // SEG_ALIGN <= 256
    n_tiles = _sorted_tiles(meta.shape[0], sb)
    tile_rows = -(-n_tiles // SUBLANES) * SUBLANES
    r = np.arange(LANES)
    ustrict = jnp.asarray(r[:, None] < r[None, :], BF16)
    lstrict = jnp.asarray(r[None, :] < r[:, None], BF16)
    cnt_all = jnp.pad(cnt[:, 0, :], ((0, PLAN_ROWS - n_blocks), (0, 0)))
    const = lambda s: (0, 0)
    posw, addr, tab, tile_owner = pl.pallas_call(
        functools.partial(_moe_plan_kernel, float(n_tiles * (FFN_ROWS // SEG_ALIGN))),
        grid=(n_blocks,),
        in_specs=[pl.BlockSpec((sb, LANES), lambda s: (s, 0)),
                  pl.BlockSpec((PLAN_ROWS, LANES), const),
                  pl.BlockSpec((LANES, LANES), const),
                  pl.BlockSpec((PLAN_ROWS, PLAN_ROWS), const)],
        out_specs=[pl.BlockSpec((sb, LANES), lambda s: (s, 0)),
                   pl.BlockSpec((1, SUBLANES, sb), lambda s: (s, 0, 0)),
                   pl.BlockSpec((4, PLAN_ROWS, LANES), lambda s: (0, 0, 0)),
                   pl.BlockSpec((tile_rows, LANES), const)],
        out_shape=[jax.ShapeDtypeStruct(meta.shape, F32),
                   jax.ShapeDtypeStruct((n_blocks, SUBLANES, sb), F32),
                   jax.ShapeDtypeStruct((4, PLAN_ROWS, LANES), F32),
                   jax.ShapeDtypeStruct((tile_rows, LANES), F32)],
        compiler_params=_params(),
        name="moe_plan",
    )(meta, cnt_all, ustrict, lstrict)
    experts = slice(ROUTE_OFF, ROUTE_OFF + N_EXPERTS)
    to_i32 = lambda x: x.astype(jnp.int32).reshape(-1)
    plan = dict(
        loc=to_i32(tab[0, :n_blocks, experts]), n=to_i32(tab[1, :n_blocks, experts]),
        dst=to_i32(tab[2, :n_blocks, experts]),
        pad_off=to_i32(tab[3, 0, ROUTE_OFF:ROUTE_OFF + N_EXPERTS + 1]),
        pad_n=to_i32(tab[3, 1, ROUTE_OFF:ROUTE_OFF + N_EXPERTS + 1]),
        n_tiles=to_i32(tab[3, 2, 0:1]),
        owner=to_i32(tile_owner[:n_tiles, 0]))
    return posw, to_i32(addr[:, 0, :]), to_i32(addr[:, 1, :]), plan


def _token_rows(start):
    return pl.ds(start, ROW_PIECES, stride=SUBLANES)


def _pow2_copies(src_ref, dst_ref, src_chunk, dst_chunk, n, n_bits, sem, act):
    done = 0
    for k in reversed(range(n_bits)):
        take = (n >> k) & 1
        rows = CHUNK_BF16_ROWS << k
        src0 = 0 if src_chunk is None else pl.multiple_of((src_chunk + done) * CHUNK_BF16_ROWS, CHUNK_BF16_ROWS)
        dst0 = pl.multiple_of((dst_chunk + done) * CHUNK_BF16_ROWS, CHUNK_BF16_ROWS)

        @pl.when(take == 1)
        def _(src0=src0, dst0=dst0, rows=rows):
            act(pltpu.make_async_copy(src_ref.at[pl.ds(src0, rows)], dst_ref.at[pl.ds(dst0, rows)], sem))

        done = done + take * (1 << k)


def _segment_copies(block, loc_ref, n_ref, dst_ref, local_ref, global_ref, to_global, sem, act):
    def per_expert(e, carry):
        seg = block * N_EXPERTS + e
        if to_global:
            _pow2_copies(local_ref, global_ref, loc_ref[seg], dst_ref[seg], n_ref[seg], SEG_BITS, sem, act)
        else:
            _pow2_copies(global_ref, local_ref, dst_ref[seg], loc_ref[seg], n_ref[seg], SEG_BITS, sem, act)
        return carry

    lax.fori_loop(0, N_EXPERTS, per_expert, 0)


def _zero_fill(zero_ref, global_ref, padoff_ref, padn_ref, sem, act):
    full = 1 << PAD_BITS

    def per_pad(e, carry):
        def per_full(c, inner):
            dst0 = pl.multiple_of((padoff_ref[e] + c * full) * CHUNK_BF16_ROWS, CHUNK_BF16_ROWS)
            act(pltpu.make_async_copy(zero_ref, global_ref.at[pl.ds(dst0, full * CHUNK_BF16_ROWS)], sem))
            return inner

        n_full = padn_ref[e] >> PAD_BITS
        lax.fori_loop(0, n_full, per_full, 0)
        _pow2_copies(zero_ref, global_ref, None, padoff_ref[e] + n_full * full, padn_ref[e] & (full - 1),
                     PAD_BITS, sem, act)
        return carry

    lax.fori_loop(0, N_EXPERTS + 1, per_pad, 0)


STAGE_SLAB = 1024


def _restage(src_sc, dst_sc, dst_dtype):
    ratio = dst_sc.shape[0] / src_sc.shape[0]
    n_slabs = src_sc.shape[0] // (STAGE_SLAB if ratio > 1 else 2 * STAGE_SLAB)
    src_rows = src_sc.shape[0] // n_slabs
    dst_rows = dst_sc.shape[0] // n_slabs

    def slab(i, carry):
        s0 = pl.multiple_of(i * src_rows, src_rows)
        d0 = pl.multiple_of(i * dst_rows, dst_rows)
        dst_sc[pl.ds(d0, dst_rows), :] = pltpu.bitcast(src_sc[pl.ds(s0, src_rows), :], dst_dtype)
        return carry

    lax.fori_loop(0, n_slabs, slab, 0)


def _moe_dispatch_kernel(sb, n_blocks, loc_ref, n_ref, dst_ref, padoff_ref, padn_ref,
                         h2p_ref, a1_ref, a2_ref, xs_hbm, local_sc, stage_sc, zero_sc, sems, zero_sem):
    b = pl.program_id(0)
    slot = b & 1

    def segments(block, buf, act):
        _segment_copies(block, loc_ref, n_ref, dst_ref, stage_sc.at[buf], xs_hbm, True, sems.at[buf], act)

    local_sc[...] = jnp.zeros_like(local_sc)

    def step(g, carry):
        src = pl.multiple_of(g * ROW_TILE, ROW_TILE)
        for u in range(SUBLANES):
            t = g * SUBLANES + u
            row = h2p_ref[_token_rows(src + u), :]
            local_sc[_token_rows(a1_ref[t]), :] = row
            local_sc[_token_rows(a2_ref[t]), :] = row
        return carry

    lax.fori_loop(0, sb // SUBLANES, step, 0, unroll=4)

    @pl.when(b > 0)
    def _():
        segments(b - 1, 1 - slot, lambda c: c.wait())

    _restage(local_sc, stage_sc.at[slot], BF16)
    segments(b, slot, lambda c: c.start())

    @pl.when(b == 0)
    def _():
        zero_sc[...] = jnp.zeros_like(zero_sc)
        _zero_fill(zero_sc, xs_hbm, padoff_ref, padn_ref, zero_sem, lambda c: c.start())
        _zero_fill(zero_sc, xs_hbm, padoff_ref, padn_ref, zero_sem, lambda c: c.wait())

    @pl.when(b == n_blocks - 1)
    def _():
        segments(b, slot, lambda c: c.wait())


def _smem_vec(n, index_map):
    return pl.BlockSpec((n,), index_map, memory_space=pltpu.SMEM)


def _moe_dispatch(h2p, a1, a2, plan, sb, n_tiles):
    n_blocks = h2p.shape[0] // (sb * ROW_PIECES)
    local_flat = _local_rows(sb) * ROW_PIECES
    vec = _smem_vec(sb, lambda s, *_: (s,))
    return pl.pallas_call(
        functools.partial(_moe_dispatch_kernel, sb, n_blocks),
        grid_spec=pltpu.PrefetchScalarGridSpec(
            num_scalar_prefetch=5,
            grid=(n_blocks,),
            in_specs=[pl.BlockSpec((sb * ROW_PIECES, LANES), lambda s, *_: (s, 0)), vec, vec],
            out_specs=pl.BlockSpec(memory_space=pl.ANY),
            scratch_shapes=[pltpu.VMEM((local_flat, LANES), U32),
                            pltpu.VMEM((2, 2 * local_flat, LANES), BF16),
                            pltpu.VMEM(((1 << PAD_BITS) * CHUNK_BF16_ROWS, LANES), BF16),
                            pltpu.SemaphoreType.DMA((2,)),
                            pltpu.SemaphoreType.DMA(())]),
        out_shape=jax.ShapeDtypeStruct((n_tiles * FFN_ROWS * ROW_PIECES * 2, LANES), BF16),
        compiler_params=_params(),
        name="moe_dispatch",
    )(plan["loc"], plan["n"], plan["dst"], plan["pad_off"], plan["pad_n"], h2p, a1, a2)


FFN_SUB = 2


def _moe_ffn_kernel(owner_ref, ntiles_ref, xs_ref, *refs):
    w_refs = refs[0:3 * FFN_SUB]
    ys_ref, wg_sc, wu_sc, wd_sc, z32_sc = refs[3 * FFN_SUB:]
    half = D_MODEL // 2
    flat = FFN_ROWS * ROW_PIECES * 2
    for s in range(FFN_SUB):
        t = pl.program_id(0) * FFN_SUB + s
        wg_ref, wu_ref, wd_ref = w_refs[3 * s:3 * s + 3]
        rows = slice(s * flat, (s + 1) * flat)
        used = t < ntiles_ref[0]

        @pl.when(used & ((t < FFN_SUB) | (owner_ref[t] != owner_ref[jnp.maximum(t - FFN_SUB, 0)])))
        def _():
            wg_sc[s] = wg_ref[0].astype(BF16)
            wu_sc[s] = wu_ref[0].astype(BF16)
            wd_sc[s] = wd_ref[0].astype(BF16)

        @pl.when(used)
        def _():
            z32 = z32_sc.at[s]
            lo, hi = _unpack_rows(z32, _from_row_tiled(xs_ref[rows, :], FFN_ROWS), FFN_ROWS)
            lo, hi = lo.astype(BF16), hi.astype(BF16)
            g = _dot(lo, wg_sc[s, 0:half, :]) + _dot(hi, wg_sc[s, half:D_MODEL, :])
            u = _dot(lo, wu_sc[s, 0:half, :]) + _dot(hi, wu_sc[s, half:D_MODEL, :])
            y = _dot((_silu(g) * u).astype(BF16), wd_sc[s])
            ys_ref[rows, :] = _flatten_tiled(_to_row_tiled(_pack_rows(z32, y, FFN_ROWS), FFN_ROWS))

        @pl.when(jnp.logical_not(used))
        def _():
            ys_ref[rows, :] = jnp.zeros((flat, LANES), BF16)


def _moe_ffn(xs, plan, wg, wu, wd):
    flat = FFN_ROWS * ROW_PIECES * 2
    n_tiles = xs.shape[0] // flat
    assert n_tiles % FFN_SUB == 0

    def wspec(shape, s):
        tile = lambda i, owner, nt: jnp.minimum(i * FFN_SUB + s, nt[0] - 1)
        return pl.BlockSpec((1,) + shape, lambda i, owner, nt: (owner[tile(i, owner, nt)], 0, 0))

    w_specs, w_args = [], []
    for s in range(FFN_SUB):
        w_specs += [wspec((D_MODEL, EXPERT_FF), s), wspec((D_MODEL, EXPERT_FF), s), wspec((EXPERT_FF, D_MODEL), s)]
        w_args += [wg, wu, wd]
    last_step = lambda i, owner, nt: jnp.minimum(i, (nt[0] - 1) // FFN_SUB)
    return pl.pallas_call(
        _moe_ffn_kernel,
        grid_spec=pltpu.PrefetchScalarGridSpec(
            num_scalar_prefetch=2,
            grid=(n_tiles // FFN_SUB,),
            in_specs=[pl.BlockSpec((FFN_SUB * flat, LANES), lambda i, owner, nt: (last_step(i, owner, nt), 0))]
                     + w_specs,
            out_specs=pl.BlockSpec((FFN_SUB * flat, LANES), lambda i, owner, nt: (i, 0)),
            scratch_shapes=[pltpu.VMEM((FFN_SUB, D_MODEL, EXPERT_FF), BF16),
                            pltpu.VMEM((FFN_SUB, D_MODEL, EXPERT_FF), BF16),
                            pltpu.VMEM((FFN_SUB, EXPERT_FF, D_MODEL), BF16),
                            pltpu.VMEM((FFN_SUB, ROW_PIECES, 2 * FFN_ROWS, LANES), F32)]),
        out_shape=jax.ShapeDtypeStruct(xs.shape, BF16),
        compiler_params=_params(),
        name="moe_ffn",
    )(plan["owner"], plan["n_tiles"], xs, *w_args)


def _moe_combine_kernel(n_psb, n_blocks, loc_ref, n_ref, dst_ref,
                        ys_hbm, a1_ref, a2_ref, posw_ref, x1_ref, mod_ref, yp_ref, yo_ref,
                        local_sc, stage_sc, g1_sc, g2_sc, z32_sc, sems):
    s = pl.program_id(0)
    slot = s & 1

    def segments(block, buf, act):
        _segment_copies(block, loc_ref, n_ref, dst_ref, stage_sc.at[buf], ys_hbm, False, sems.at[buf], act)

    @pl.when(pl.program_id(1) == 0)
    def _():
        @pl.when(s == 0)
        def _():
            segments(s, slot, lambda c: c.start())

        segments(s, slot, lambda c: c.wait())
        _restage(stage_sc.at[slot], local_sc, U32)

        @pl.when(s + 1 < n_blocks)
        def _():
            segments(s + 1, 1 - slot, lambda c: c.start())

    def step(g, carry):
        dst = pl.multiple_of(g * ROW_TILE, ROW_TILE)
        for u in range(SUBLANES):
            t = g * SUBLANES + u
            g1_sc[_token_rows(dst + u), :] = local_sc[_token_rows(a1_ref[t]), :]
            g2_sc[_token_rows(dst + u), :] = local_sc[_token_rows(a2_ref[t]), :]
        return carry

    lax.fori_loop(0, TOK_TILE // SUBLANES, step, 0, unroll=4)
    halves = lambda g_sc: [pltpu.bitcast(p, BF16) for p in _from_row_tiled(g_sc[...], TOK_TILE)]
    lo1, hi1 = _unpack_rows(z32_sc, halves(g1_sc), TOK_TILE)
    lo2, hi2 = _unpack_rows(z32_sc, halves(g2_sc), TOK_TILE)
    w1, w2 = posw_ref[:, 2:3], posw_ref[:, 3:4]
    moe = jnp.concatenate([w1 * lo1 + w2 * lo2, w1 * hi1 + w2 * hi2], axis=1)
    gate2 = _rows_to_tokens(mod_ref[:, 5 * D_MODEL:6 * D_MODEL], D_MODEL)
    y = x1_ref[...] + gate2 * moe

    @pl.when(s < n_psb)
    def _():
        yp_ref[...] = y

    @pl.when(s >= n_psb)
    def _():
        yo_ref[...] = y


def _moe_combine(ys, a1, a2, posw, plan, x1, mod, sb, n_ptiles, n_stiles, prep):
    tps = sb // TOK_TILE
    n_blocks = (n_ptiles + n_stiles) // tps
    n_psb = n_ptiles // tps
    pblocks = prep // ROWS_PER_TILE
    tile = lambda s, j: s * tps + j
    vec = _smem_vec(TOK_TILE, lambda s, j, *_: (tile(s, j),))
    return pl.pallas_call(
        functools.partial(_moe_combine_kernel, n_psb, n_blocks),
        grid_spec=pltpu.PrefetchScalarGridSpec(
            num_scalar_prefetch=3,
            grid=(n_blocks, tps),
            in_specs=[pl.BlockSpec(memory_space=pl.ANY), vec, vec,
                      pl.BlockSpec((TOK_TILE, LANES), lambda s, j, *_: (tile(s, j), 0)),
                      pl.BlockSpec((TOK_TILE, D_MODEL), lambda s, j, *_: (tile(s, j), 0)),
                      pl.BlockSpec((ROWS_PER_TILE, 6 * D_MODEL),
                                   lambda s, j, *_: (jnp.maximum(tile(s, j) - n_ptiles + pblocks, 0), 0))],
            out_specs=[pl.BlockSpec((TOK_TILE, D_MODEL),
                                    lambda s, j, *_: (jnp.minimum(tile(s, j), n_ptiles - 1), 0)),
                       pl.BlockSpec((TOK_TILE, D_MODEL),
                                    lambda s, j, *_: (jnp.maximum(tile(s, j) - n_ptiles, 0), 0))],
            scratch_shapes=[pltpu.VMEM((_local_rows(sb) * ROW_PIECES, LANES), U32),
                            pltpu.VMEM((2, _local_rows(sb) * ROW_PIECES * 2, LANES), BF16),
                            pltpu.VMEM((TOK_TILE * ROW_PIECES, LANES), U32),
                            pltpu.VMEM((TOK_TILE * ROW_PIECES, LANES), U32),
                            pltpu.VMEM((ROW_PIECES, 2 * TOK_TILE, LANES), F32),
                            pltpu.SemaphoreType.DMA((2,))]),
        out_shape=[jax.ShapeDtypeStruct((n_ptiles * TOK_TILE, D_MODEL), F32),
                   jax.ShapeDtypeStruct((n_stiles * TOK_TILE, D_MODEL), F32)],
        compiler_params=_params(2),
        name="moe_combine",
    )(plan["loc"], plan["n"], plan["dst"], ys, a1, a2, posw, x1, mod)


def _layer(xp, xs, cache_k, cache_v, state, c_prompt, c_sample, norm_mix_g, norm_ffn_g, w_ada, b_ada, w_in,
           q_norm_g, k_norm_g, rel_bias, w_gate_up, b_gate, gla_norm_g, w_out, w_route_group,
           b_route_group, w_route_expert, b_route_expert, w_exp_gate, w_exp_up, w_exp_down):
    batch, seq, _ = xp.shape
    n_seq, dec_seq, _ = xs.shape
    assert batch == 1 and dec_seq == CHUNK and cache_k.shape[1] == BAND_PAST
    assert seq % TOK_TILE == 0 and seq >= BAND_PAST and (n_seq * CHUNK) % TOK_TILE == 0
    assert seq % (ATTN_SUB * Q_ROWS) == 0 and seq % (GLA_SUB * GLA_CHUNKS * CHUNK) == 0
    n_ptok, n_stok = seq, n_seq * CHUNK
    n_ptiles, n_stiles = n_ptok // TOK_TILE, n_stok // TOK_TILE
    sb = (MOE_SUPER_BLOCK if (n_ptok % MOE_SUPER_BLOCK == 0 and n_stok % MOE_SUPER_BLOCK == 0)
          else OUT_SUB * TOK_TILE)
    prep = OUT_SUB * ROWS_PER_TILE

    xp2 = xp.reshape(n_ptok, D_MODEL)
    xs2 = xs.reshape(n_stok, D_MODEL)
    c_rows = jnp.concatenate([jnp.broadcast_to(c_prompt, (prep, D_MODEL)), c_sample], axis=0)
    mod = _adaln(c_rows, w_ada, b_ada)

    w_main = w_in[:, 0:IN_MAIN].astype(BF16)
    w_gr = jnp.pad(w_in[:, IN_MAIN:], ((0, 0), (0, LANES - GATE_RANK))).astype(BF16)
    wgu_p = jnp.pad(w_gate_up, ((0, LANES - GATE_RANK), (0, 0))).astype(BF16)
    head = np.arange(A_WIDTH) // A_HEAD_DIM
    bd = jnp.asarray(head[:, None] == head[None, :], BF16)
    gq = jnp.tile(q_norm_g, A_HEADS).reshape(1, A_WIDTH)
    gk = jnp.tile(k_norm_g, A_HEADS).reshape(1, A_WIDTH)
    q, k, v, kf, vf, gla, la = _inproj(
        xp2, xs2, mod, norm_mix_g.reshape(1, D_MODEL), w_main, w_gr, bd, gq, gk, wgu_p,
        b_gate.reshape(1, B_KWIDTH), n_ptiles, n_stiles, prep)

    first_chunk = n_ptok // CHUNK
    oa_p = _attn_prompt(rel_bias[:, _bias_lanes(ATTN_WIN * Q_ROWS)], q, k, v, n_ptok // (ATTN_SUB * Q_ROWS))
    oa_s = _attn_sample(rel_bias[:, _bias_lanes(SAMPLE_KEYS)], q, k, v,
                        cache_k.reshape(n_seq, BAND_PAST, A_WIDTH), cache_v.reshape(n_seq, BAND_PAST, A_WIDTH),
                        first_chunk, n_seq)
    g_gla = gla_norm_g.reshape(1, B_DV)
    ob_p, sfin_p = _gla_prompt(gla, la, g_gla, n_ptok // (GLA_SUB * GLA_CHUNKS * CHUNK))
    ob_s, sfin_s = _gla_sample(gla, la, g_gla, _state_to_pairs(state), first_chunk, n_seq)

    wr = jnp.pad(jnp.concatenate([w_route_group, w_route_expert], axis=1),
                 ((0, 0), (0, LANES - N_GROUPS - N_EXPERTS)))
    br = jnp.pad(jnp.concatenate([b_route_group, b_route_expert]), (0, LANES - N_GROUPS - N_EXPERTS))
    x1, h2p, meta, cnt = _outproj(oa_p, oa_s, ob_p, ob_s, w_out.astype(BF16), xp2, xs2, mod,
                                  norm_ffn_g.reshape(1, D_MODEL), wr, br.reshape(1, LANES),
                                  n_ptiles, n_stiles, prep, sb)

    posw, a1, a2, plan = _moe_plan(meta, cnt, sb)
    xs_sorted = _moe_dispatch(h2p.reshape(-1, LANES), a1, a2, plan, sb, _sorted_tiles(n_ptok + n_stok, sb))
    ys_sorted = _moe_ffn(xs_sorted, plan, w_exp_gate, w_exp_up, w_exp_down)
    yp, ys = _moe_combine(ys_sorted, a1, a2, posw, plan, x1, mod, sb, n_ptiles, n_stiles, prep)

    tail = min(BAND_PAST, seq)
    heads = (A_HEADS, A_HEAD_DIM)
    return (yp.reshape(1, seq, D_MODEL), ys.reshape(n_seq, CHUNK, D_MODEL),
            kf[TOK_TILE - tail:TOK_TILE].reshape((1, tail) + heads),
            vf[TOK_TILE - tail:TOK_TILE].reshape((1, tail) + heads),
            _pairs_to_state(sfin_p)[None],
            kf[TOK_TILE:].reshape((n_seq, CHUNK) + heads),
            vf[TOK_TILE:].reshape((n_seq, CHUNK) + heads),
            _pairs_to_state(sfin_s))


def kernel(x_prompt, x_sample, cache_a_k, cache_a_v, state_gla, c_prompt, c_sample, norm_mix_g, norm_ffn_g,
           w_ada, b_ada, w_in, q_norm_g, k_norm_g, rel_bias, w_gate_up, b_gate, gla_norm_g, w_out,
           w_route_group, b_route_group, w_route_expert, b_route_expert, w_exp_gate, w_exp_up, w_exp_down):
    depth = w_in.shape[0]
    yp, ys = x_prompt, x_sample
    outs = [[] for _ in range(6)]
    for l in range(depth):
        yp, ys, kp, vp, sp, ks, vs, ss = _layer(
            yp, ys, cache_a_k[l], cache_a_v[l], state_gla[l], c_prompt, c_sample, norm_mix_g[l], norm_ffn_g[l],
            w_ada[l], b_ada[l], w_in[l], q_norm_g[l], k_norm_g[l], rel_bias[l], w_gate_up[l], b_gate[l],
            gla_norm_g[l], w_out[l], w_route_group[l], b_route_group[l], w_route_expert[l], b_route_expert[l],
            w_exp_gate[l], w_exp_up[l], w_exp_down[l])
        for lst, val in zip(outs, (kp, vp, sp, ks, vs, ss)):
            lst.append(val)
    return (yp, ys) + tuple(jnp.stack(o) for o in outs)
```
